```python
import jax, jax.numpy as jnp
from jax import lax
import numpy as np

D_MODEL = 1024
BATCH = 8
SEQ = 4096
DEPTH = 4

GRID_W = 64
CTX_LEN = 256

NA_HEADS = 6
NA_DIM = 64
NA_WIN_ROWS = 8
NA_WIN_COLS = 16
MLA_HEADS = 4
MLA_Q_RANK = 256
MLA_KV_RANK = 128
MLA_NOPE = 128
MLA_ROPE = 64
MLA_V = 128
MLA_BLOCK = 128
SWA_HEADS = 6
SWA_KV_HEADS = 2
SWA_DIM = 64
SWA_WINDOW = 128
SWA_BLOCK = 128
N_EXPERTS = 16
EXPERT_FF = 1024
CAPACITY_FACTOR = 2

N_BRANCH = 3
ROPE_BASE = 10000.0
EPS = 1e-6

NA_W = NA_HEADS * NA_DIM
MLA_QK = MLA_NOPE + MLA_ROPE
MLA_W = MLA_HEADS * MLA_V
SWA_W = SWA_HEADS * SWA_DIM
SWA_KV_W = SWA_KV_HEADS * SWA_DIM
IN_SPLITS = (NA_W, NA_W, NA_W, MLA_Q_RANK, MLA_KV_RANK, MLA_ROPE, SWA_W, SWA_KV_W, SWA_KV_W, N_BRANCH * D_MODEL)
D_IN = sum(IN_SPLITS)

kernel_name = 'hybrid_na_mla_swa_ecmoe_dit'


def _rmsnorm(x, g):
    xf = x.astype(jnp.float32)
    y = xf * lax.rsqrt(jnp.mean(xf * xf, axis=-1, keepdims=True) + EPS)
    return (y * g.astype(jnp.float32)).astype(x.dtype)


def _heads(t, n):
    B, T, _ = t.shape
    return t.reshape(B, T, n, -1).transpose(0, 2, 1, 3)


def _merge_heads(t):
    B, n, T, d = t.shape
    return t.transpose(0, 2, 1, 3).reshape(B, T, n * d)


def _softmax32(s):
    return jax.nn.softmax(s.astype(jnp.float32), axis=-1)


def _rotate(x, ang):
    f = ang.shape[-1]
    x1, x2 = x[..., :f], x[..., f:]
    cos, sin = jnp.cos(ang), jnp.sin(ang)
    return jnp.concatenate([x1 * cos - x2 * sin, x2 * cos + x1 * sin], axis=-1)


def _axial_rope(x, row, col):
    half = x.shape[-1] // 2
    f = half // 2
    inv = ROPE_BASE ** (-jnp.arange(f, dtype=jnp.float32) / f)
    ar = row.astype(jnp.float32)[:, None] * inv
    ac = col.astype(jnp.float32)[:, None] * inv
    out = jnp.concatenate([_rotate(x[..., :half], ar), _rotate(x[..., half:], ac)], axis=-1)
    return out.astype(x.dtype)


def _rope_tail(t, n_rope, row, col):
    return jnp.concatenate([t[..., :-n_rope], _axial_rope(t[..., -n_rope:], row, col)], axis=-1)


def _ctx_attention(qc, kc, vc, scale, sink=None):
    B, Hq, L, _ = qc.shape
    Hk = kc.shape[1]
    G = Hq // Hk
    qg = qc.reshape(B, Hk, G, L, -1)
    s = jnp.einsum('bkgqd,bkcd->bkgqc', qg, kc).astype(jnp.float32) * scale
    if sink is not None:
        sk = jnp.broadcast_to(sink.reshape(1, Hk, G, 1, 1).astype(jnp.float32), s.shape[:-1] + (1,))
        p = _softmax32(jnp.concatenate([s, sk], axis=-1))[..., :L]
    else:
        p = _softmax32(s)
    o = jnp.einsum('bkgqc,bkcd->bkgqd', p.astype(vc.dtype), vc)
    return o.reshape(B, Hq, L, -1)


def _na_latent(q, k, v, kc, vc, rpb):
    B, H, S, d = q.shape
    rows = S // GRID_W
    wr = min(NA_WIN_ROWS, rows)
    scale = d ** -0.5
    r = jnp.arange(rows)
    ridx = jnp.clip(r - wr // 2, 0, rows - wr)[:, None] + jnp.arange(wr)[None, :]
    col = jnp.arange(GRID_W)
    c0 = jnp.clip(col - NA_WIN_COLS // 2, 0, GRID_W - NA_WIN_COLS)
    col_ok = (col[None, :] >= c0[:, None]) & (col[None, :] < c0[:, None] + NA_WIN_COLS)
    dr = ridx - r[:, None] + (NA_WIN_ROWS - 1)
    dc = jnp.clip(col[None, :] - col[:, None] + NA_WIN_COLS - 1, 0, 2 * NA_WIN_COLS - 2)
    bias = rpb[:, dr[:, None, :, None], dc[None, :, None, :]]
    qg = q.reshape(B, H, rows, GRID_W, d)
    kb = k.reshape(B, H, rows, GRID_W, d)[:, :, ridx]
    vb = v.reshape(B, H, rows, GRID_W, d)[:, :, ridx].reshape(B, H, rows, wr * GRID_W, d)
    s_loc = jnp.einsum('bhrqd,bhrwkd->bhrqwk', qg, kb).astype(jnp.float32) * scale + bias.astype(jnp.float32)
    s_loc = jnp.where(col_ok[:, None, :], s_loc, -jnp.inf).reshape(B, H, rows, GRID_W, wr * GRID_W)
    s_ctx = jnp.einsum('bhrqd,bhcd->bhrqc', qg, kc).astype(jnp.float32) * scale
    p = _softmax32(jnp.concatenate([s_loc, s_ctx], axis=-1)).astype(v.dtype)
    nl = wr * GRID_W
    o = jnp.einsum('bhrqj,bhrjd->bhrqd', p[..., :nl], vb) + jnp.einsum('bhrqc,bhcd->bhrqd', p[..., nl:], vc)
    return o.reshape(B, H, S, d)


def _mla_q(cq, q_norm, w_uq, q_g):
    B, T, _ = cq.shape
    q = (_rmsnorm(cq, q_norm) @ w_uq).reshape(B, T, MLA_HEADS, MLA_QK)
    return _rmsnorm(q, q_g).transpose(0, 2, 1, 3)


def _mla_kv(ckv, kr, kv_norm, w_ukv, k_g):
    B, T, _ = ckv.shape
    kv = (_rmsnorm(ckv, kv_norm) @ w_ukv).reshape(B, T, MLA_HEADS, MLA_NOPE + MLA_V)
    k_nope, v = kv[..., :MLA_NOPE], kv[..., MLA_NOPE:]
    k = jnp.concatenate([k_nope, jnp.broadcast_to(kr[:, :, None, :], (B, T, MLA_HEADS, MLA_ROPE))], axis=-1)
    return _rmsnorm(k, k_g).transpose(0, 2, 1, 3), v.transpose(0, 2, 1, 3)


def _mla_latent(q, k, v, kc, vc):
    B, H, S, dq = q.shape
    scale = dq ** -0.5
    k_all = jnp.concatenate([k, kc], axis=2)
    v_all = jnp.concatenate([v, vc], axis=2)
    nb = S // MLA_BLOCK
    qb = q.reshape(B, H, nb, MLA_BLOCK, dq).transpose(2, 0, 1, 3, 4)

    def block(qi):
        s = jnp.einsum('bhqd,bhkd->bhqk', qi, k_all).astype(jnp.float32) * scale
        return jnp.einsum('bhqk,bhkd->bhqd', _softmax32(s).astype(v_all.dtype), v_all)

    o = lax.map(block, qb)
    return o.transpose(1, 2, 0, 3, 4).reshape(B, H, S, -1)


def _swa_latent(q, k, v, kc, vc, sink):
    B, Hq, S, d = q.shape
    Hk = k.shape[1]
    G = Hq // Hk
    scale = d ** -0.5
    blk = SWA_BLOCK
    nb = S // blk
    qb = q.reshape(B, Hk, G, nb, blk, d)
    pad = jnp.zeros((B, Hk, blk, d), k.dtype)

    def band(t):
        tp = jnp.concatenate([pad, t, pad], axis=2).reshape(B, Hk, nb + 2, blk, d)
        return jnp.concatenate([tp[:, :, :-2], tp[:, :, 1:-1], tp[:, :, 2:]], axis=3)

    kb, vb = band(k), band(v)
    qpos = jnp.arange(nb)[:, None] * blk + jnp.arange(blk)[None, :]
    kpos = (jnp.arange(nb)[:, None] - 1) * blk + jnp.arange(3 * blk)[None, :]
    ok = (jnp.abs(qpos[:, :, None] - kpos[:, None, :]) <= SWA_WINDOW) & (kpos[:, None, :] >= 0) & (kpos[:, None, :] < S)
    s_loc = jnp.where(ok, jnp.einsum('bkgnqd,bknjd->bkgnqj', qb, kb).astype(jnp.float32) * scale, -jnp.inf)
    s_ctx = jnp.einsum('bkgnqd,bkcd->bkgnqc', qb, kc).astype(jnp.float32) * scale
    sk = jnp.broadcast_to(sink.reshape(1, Hk, G, 1, 1, 1).astype(jnp.float32), s_ctx.shape[:-1] + (1,))
    p = _softmax32(jnp.concatenate([s_loc, s_ctx, sk], axis=-1)).astype(v.dtype)
    nl = 3 * blk
    L = kc.shape[2]
    o = jnp.einsum('bkgnqj,bknjd->bkgnqd', p[..., :nl], vb) + jnp.einsum('bkgnqc,bkcd->bkgnqd', p[..., nl:nl + L], vc)
    return o.reshape(B, Hq, S, d)


def _ec_moe(h, router, w_gate, w_up, w_down):
    B, T, _ = h.shape
    cap = CAPACITY_FACTOR * T // N_EXPERTS
    aff = jax.nn.softmax((h @ router).astype(jnp.float32), axis=-1)
    gate, idx = lax.top_k(jnp.swapaxes(aff, 1, 2), cap)
    bidx = jnp.arange(B)[:, None, None]
    xg = h[bidx, idx]
    a = jnp.einsum('becd,edf->becf', xg, w_gate)
    u = jnp.einsum('becd,edf->becf', xg, w_up)
    y = jnp.einsum('becf,efd->becd', jax.nn.silu(a) * u, w_down) * gate[..., None].astype(h.dtype)
    return jnp.zeros_like(h).at[bidx, idx].add(y)


def _gated_merge(gl, y_na, y_mla, y_swa, w_o):
    g = jax.nn.sigmoid(gl.astype(jnp.float32)).astype(y_na.dtype)
    g = g.reshape(gl.shape[:-1] + (N_BRANCH, D_MODEL))
    return (g[..., 0, :] * y_na + g[..., 1, :] * y_mla + g[..., 2, :] * y_swa) @ w_o


def _layer(x, ctx, c_act, cctx_act, row, col, last,
           norm1_g, norm2_g, w_ada, b_ada, w_in,
           na_q_g, na_k_g, na_rpb,
           mla_q_norm, mla_w_uq, mla_kv_norm, mla_w_ukv, mla_q_g, mla_k_g,
           swa_q_g, swa_k_g, swa_sink,
           w_na_o, w_mla_o, w_swa_o, w_o,
           router, w_gate, w_up, w_down):
    mod = jnp.split((c_act @ w_ada + b_ada)[:, None, :], 6, axis=-1)
    mod_c = jnp.split(cctx_act @ w_ada + b_ada, 6, axis=-1)
    splits = [int(i) for i in np.cumsum(IN_SPLITS)[:-1]]

    h = _rmsnorm(x, norm1_g) * (1 + mod[1]) + mod[0]
    hc = _rmsnorm(ctx, norm1_g) * (1 + mod_c[1]) + mod_c[0]
    (na_q, na_k, na_v, mla_cq, mla_ckv, mla_kr, sw_q, sw_k, sw_v, gl) = jnp.split(h @ w_in, splits, axis=-1)
    (na_qc, na_kc, na_vc, mla_cqc, mla_ckvc, mla_krc, sw_qc, sw_kc, sw_vc, glc) = jnp.split(hc @ w_in, splits, axis=-1)

    na_kc_h = _rmsnorm(_heads(na_kc, NA_HEADS), na_k_g)
    na_vc_h = _heads(na_vc, NA_HEADS)
    o_na = _na_latent(_rmsnorm(_heads(na_q, NA_HEADS), na_q_g), _rmsnorm(_heads(na_k, NA_HEADS), na_k_g),
                      _heads(na_v, NA_HEADS), na_kc_h, na_vc_h, na_rpb)
    y_na = _merge_heads(o_na) @ w_na_o

    mla_kc_h, mla_vc_h = _mla_kv(mla_ckvc, mla_krc, mla_kv_norm, mla_w_ukv, mla_k_g)
    q_m = _rope_tail(_mla_q(mla_cq, mla_q_norm, mla_w_uq, mla_q_g), MLA_ROPE, row, col)
    k_m, v_m = _mla_kv(mla_ckv, mla_kr, mla_kv_norm, mla_w_ukv, mla_k_g)
    y_mla = _merge_heads(_mla_latent(q_m, _rope_tail(k_m, MLA_ROPE, row, col), v_m, mla_kc_h, mla_vc_h)) @ w_mla_o

    sw_kc_h = _rmsnorm(_heads(sw_kc, SWA_KV_HEADS), swa_k_g)
    sw_vc_h = _heads(sw_vc, SWA_KV_HEADS)
    q_s = _axial_rope(_rmsnorm(_heads(sw_q, SWA_HEADS), swa_q_g), row, col)
    k_s = _axial_rope(_rmsnorm(_heads(sw_k, SWA_KV_HEADS), swa_k_g), row, col)
    y_swa = _merge_heads(_swa_latent(q_s, k_s, _heads(sw_v, SWA_KV_HEADS), sw_kc_h, sw_vc_h, swa_sink)) @ w_swa_o

    x_new = x + mod[2] * _gated_merge(gl, y_na, y_mla, y_swa, w_o)
    h2 = _rmsnorm(x_new, norm2_g) * (1 + mod[4]) + mod[3]
    x_new = x_new + mod[5] * _ec_moe(h2, router, w_gate, w_up, w_down)
    if last:
        return x_new, ctx

    yc_na = _merge_heads(_ctx_attention(_rmsnorm(_heads(na_qc, NA_HEADS), na_q_g), na_kc_h, na_vc_h, NA_DIM ** -0.5)) @ w_na_o
    yc_mla = _merge_heads(_ctx_attention(_mla_q(mla_cqc, mla_q_norm, mla_w_uq, mla_q_g), mla_kc_h, mla_vc_h, MLA_QK ** -0.5)) @ w_mla_o
    yc_swa = _merge_heads(_ctx_attention(_rmsnorm(_heads(sw_qc, SWA_HEADS), swa_q_g), sw_kc_h, sw_vc_h, SWA_DIM ** -0.5, swa_sink)) @ w_swa_o
    ctx_new = ctx + mod_c[2] * _gated_merge(glc, yc_na, yc_mla, yc_swa, w_o)
    hc2 = _rmsnorm(ctx_new, norm2_g) * (1 + mod_c[4]) + mod_c[3]
    ctx_new = ctx_new + mod_c[5] * _ec_moe(hc2, router, w_gate, w_up, w_down)
    return x_new, ctx_new


def setup_inputs(seed: int = 0) -> dict:
    key = jax.random.key(seed)
    ks = jax.random.split(key, 32)
    L, D = DEPTH, D_MODEL

    def nrm(k, shape, scale):
        return jax.random.normal(k, shape, jnp.float32) * scale

    def gain(k, shape):
        return 1.0 + 0.1 * jax.random.normal(k, shape, jnp.float32)

    return {
        'x': nrm(ks[0], (BATCH, SEQ, D), 1.0),
        'c': nrm(ks[1], (BATCH, D), 1.0),
        'ctx': nrm(ks[2], (BATCH, CTX_LEN, D), 1.0),
        'c_ctx': nrm(ks[3], (D,), 1.0),
        'norm1_g': gain(ks[4], (L, D)),
        'norm2_g': gain(ks[5], (L, D)),
        'w_ada': nrm(ks[6], (L, D, 6 * D), 0.5 * D ** -0.5),
        'b_ada': nrm(ks[7], (L, 6 * D), 0.01),
        'w_in': nrm(ks[8], (L, D, D_IN), D ** -0.5),
        'na_q_g': gain(ks[9], (L, NA_DIM)),
        'na_k_g': gain(ks[10], (L, NA_DIM)),
        'na_rpb': nrm(ks[11], (L, NA_HEADS, 2 * NA_WIN_ROWS - 1, 2 * NA_WIN_COLS - 1), 0.5),
        'mla_q_norm': gain(ks[12], (L, MLA_Q_RANK)),
        'mla_w_uq': nrm(ks[13], (L, MLA_Q_RANK, MLA_HEADS * MLA_QK), MLA_Q_RANK ** -0.5),
        'mla_kv_norm': gain(ks[14], (L, MLA_KV_RANK)),
        'mla_w_ukv': nrm(ks[15], (L, MLA_KV_RANK, MLA_HEADS * (MLA_NOPE + MLA_V)), MLA_KV_RANK ** -0.5),
        'mla_q_g': gain(ks[16], (L, MLA_QK)),
        'mla_k_g': gain(ks[17], (L, MLA_QK)),
        'swa_q_g': gain(ks[18], (L, SWA_DIM)),
        'swa_k_g': gain(ks[19], (L, SWA_DIM)),
        'swa_sink': nrm(ks[20], (L, SWA_HEADS), 0.5),
        'w_na_o': nrm(ks[21], (L, NA_W, D), NA_W ** -0.5),
        'w_mla_o': nrm(ks[22], (L, MLA_W, D), MLA_W ** -0.5),
        'w_swa_o': nrm(ks[23], (L, SWA_W, D), SWA_W ** -0.5),
        'w_o': nrm(ks[24], (L, D, D), D ** -0.5),
        'router': nrm(ks[25], (L, D, N_EXPERTS), D ** -0.5),
        'w_gate': nrm(ks[26], (L, N_EXPERTS, D, EXPERT_FF), D ** -0.5),
        'w_up': nrm(ks[27], (L, N_EXPERTS, D, EXPERT_FF), D ** -0.5),
        'w_down': nrm(ks[28], (L, N_EXPERTS, EXPERT_FF, D), EXPERT_FF ** -0.5),
    }


def reference(x, c, ctx, c_ctx, norm1_g, norm2_g, w_ada, b_ada, w_in,
              na_q_g, na_k_g, na_rpb,
              mla_q_norm, mla_w_uq, mla_kv_norm, mla_w_ukv, mla_q_g, mla_k_g,
              swa_q_g, swa_k_g, swa_sink,
              w_na_o, w_mla_o, w_swa_o, w_o,
              router, w_gate, w_up, w_down):
    S = x.shape[1]
    t = jnp.arange(S)
    row = t // GRID_W
    col = t % GRID_W
    c_act = jax.nn.silu(c)
    cctx_act = jax.nn.silu(c_ctx)
    for l in range(DEPTH):
        x, ctx = _layer(x, ctx, c_act, cctx_act, row, col, l == DEPTH - 1,
                        norm1_g[l], norm2_g[l], w_ada[l], b_ada[l], w_in[l],
                        na_q_g[l], na_k_g[l], na_rpb[l],
                        mla_q_norm[l], mla_w_uq[l], mla_kv_norm[l], mla_w_ukv[l], mla_q_g[l], mla_k_g[l],
                        swa_q_g[l], swa_k_g[l], swa_sink[l],
                        w_na_o[l], w_mla_o[l], w_swa_o[l], w_o[l],
                        router[l], w_gate[l], w_up[l], w_down[l])
    return x
```

```python
import functools

import numpy as np
import jax
import jax.numpy as jnp
from jax import lax
from jax.experimental import pallas as pl
from jax.experimental.pallas import tpu as pltpu

GRID_W = 64
NA_HEADS, NA_DIM, NA_WIN_ROWS, NA_WIN_COLS = 6, 64, 8, 16
MLA_HEADS, MLA_Q_RANK, MLA_KV_RANK, MLA_NOPE, MLA_ROPE, MLA_V = 4, 256, 128, 128, 64, 128
MLA_QK = MLA_NOPE + MLA_ROPE
MLA_PAD = 256
SWA_HEADS, SWA_KV_HEADS, SWA_DIM, SWA_WINDOW = 6, 2, 64, 128
CAPACITY_FACTOR = 2
N_BRANCH = 3
ROPE_BASE = 10000.0
EPS = 1e-6

NA_W = NA_HEADS * NA_DIM
SWA_W = SWA_HEADS * SWA_DIM
SWA_KV_W = SWA_KV_HEADS * SWA_DIM
MLA_W = MLA_HEADS * MLA_V

LANES = 128
TM = 256
NA_TILE_ROWS = TM // GRID_W
NA_BAND_ROWS = NA_TILE_ROWS + NA_WIN_ROWS - 1
NA_BAND = NA_BAND_ROWS * GRID_W
SWA_KEYS = TM + 2 * SWA_WINDOW
MOE_CHUNK = 256
NEG = -1e30
VMEM_LIMIT = 56 * 1024 * 1024

P_NAQ, P_NAK, P_NAV, P_SWQ = 0, 384, 768, 1152
P_CQ, P_CKV, P_KR, P_SWK, P_SWV = 1536, 1792, 1920, 2048, 2176
PROJ_W = 2304
N_CHUNK = 768
A_NAQ, A_NAK, A_SWQ, A_SWK = 0, 384, 768, 1152
QKA_W = 1280
SWA_HEAD_ORDER = (0, 3, 1, 4, 2, 5)

bf16 = jnp.bfloat16
f32 = jnp.float32


def _mm(a, b):
    return jnp.dot(a, b, preferred_element_type=f32)


def _nt(a, b):
    return lax.dot_general(a, b, (((1,), (1,)), ((), ())), preferred_element_type=f32)


def _params(*sem):
    return pltpu.CompilerParams(dimension_semantics=sem, vmem_limit_bytes=VMEM_LIMIT)


def _ada_kernel(c_ref, w_ref, b_ref, o_ref):
    a = c_ref[...]
    a = (a * jax.nn.sigmoid(a)).astype(bf16)
    o_ref[...] = _mm(a, w_ref[...].astype(bf16)) + b_ref[...]


def _ada(cpad, w_ada, b_ada):
    L, D, N = w_ada.shape
    tn = 1536
    return pl.pallas_call(
        _ada_kernel,
        grid=(L, N // tn),
        in_specs=[
            pl.BlockSpec((16, D), lambda l, j: (0, 0)),
            pl.BlockSpec((None, D, tn), lambda l, j: (l, 0, j)),
            pl.BlockSpec((None, 1, tn), lambda l, j: (l, 0, j)),
        ],
        out_specs=pl.BlockSpec((None, 16, tn), lambda l, j: (l, 0, j)),
        out_shape=jax.ShapeDtypeStruct((L, 16, N), f32),
        compiler_params=_params("arbitrary", "arbitrary"),
        name="ada",
    )(cpad, w_ada, b_ada.reshape(L, 1, N))


def _in_proj_kernel(has_moe, n_gl_chunks, *refs):
    if has_moe:
        x_ref, moe_ref, modp_ref, mod_ref, g_ref, w_ref, xo_ref, proj_ref, gl_ref = refs
        x = x_ref[...] + modp_ref[5:6, :] * moe_ref[...].T
        xo_ref[...] = x
    else:
        x_ref, mod_ref, g_ref, w_ref, proj_ref, gl_ref = refs
        x = x_ref[...]
    ms = jnp.mean(x * x, axis=-1, keepdims=True)
    h = x * lax.rsqrt(ms + EPS) * g_ref[...]
    h = (h * (1.0 + mod_ref[1:2, :]) + mod_ref[0:1, :]).astype(bf16)
    for c in range(PROJ_W // N_CHUNK):
        sl = slice(c * N_CHUNK, (c + 1) * N_CHUNK)
        proj_ref[:, sl] = _mm(h, w_ref[:, sl]).astype(bf16)
    for c in range(n_gl_chunks):
        sl = slice(c * N_CHUNK, (c + 1) * N_CHUNK)
        gl_ref[:, sl] = _mm(h, w_ref[:, PROJ_W + c * N_CHUNK:PROJ_W + (c + 1) * N_CHUNK]).astype(bf16)


def _mod_spec(n_lat, D):
    return pl.BlockSpec((None, None, 6, D), lambda b, i: (b, jnp.where(i >= n_lat, 1, 0), 0, 0))


def _in_proj(x, moe, modp, mod, g1, w, n_lat):
    B, T, D = x.shape
    nt = T // TM
    has_moe = moe is not None
    n_gl_chunks = N_BRANCH * D // N_CHUNK
    tok = pl.BlockSpec((None, TM, D), lambda b, i: (b, i, 0))
    in_specs = [tok]
    args = [x]
    if has_moe:
        in_specs += [pl.BlockSpec((None, None, D, TM), lambda b, i: (b, i, 0, 0)), _mod_spec(n_lat, D)]
        args += [moe, modp]
    in_specs += [
        _mod_spec(n_lat, D),
        pl.BlockSpec((1, D), lambda b, i: (0, 0)),
        pl.BlockSpec(w.shape, lambda b, i: (0, 0), pipeline_mode=pl.Buffered(1)),
    ]
    args += [mod, g1, w]
    out_specs = [
        pl.BlockSpec((None, TM, PROJ_W), lambda b, i: (b, i, 0)),
        pl.BlockSpec((None, TM, N_BRANCH * D), lambda b, i: (b, i, 0)),
    ]
    out_shape = [
        jax.ShapeDtypeStruct((B, T, PROJ_W), bf16),
        jax.ShapeDtypeStruct((B, T, N_BRANCH * D), bf16),
    ]
    if has_moe:
        out_specs = [tok] + out_specs
        out_shape = [jax.ShapeDtypeStruct((B, T, D), f32)] + out_shape
    res = pl.pallas_call(
        functools.partial(_in_proj_kernel, has_moe, n_gl_chunks),
        grid=(B, nt),
        in_specs=in_specs,
        out_specs=out_specs,
        out_shape=out_shape,
        compiler_params=_params("arbitrary", "arbitrary"),
        name="in_proj",
    )(*args)
    if has_moe:
        return res
    return [x] + list(res)


def _prep_kernel(proj_ref, cos_ref, sin_ref, gmat_ref, naq_g, nak_g, swq_g, swk_g,
                 qn_ref, wuq_ref, qg_ref, kvn_ref, wuk_ref, wuv_ref, kg_ref,
                 qka_ref, mq_ref, mk_ref, mv_ref):
    cos = cos_ref[...]
    sin = sin_ref[...]
    lane = lax.broadcasted_iota(jnp.int32, cos.shape, 1)
    first = (lane & 16) == 0

    def rope(x):
        partner = jnp.where(first, pltpu.roll(x, LANES - 16, 1), pltpu.roll(x, 16, 1))
        return x * cos + partner * sin

    def headnorm(x, gain):
        w = x.shape[1]
        ms = _mm((x * x).astype(bf16), gmat_ref[:w, :w])
        return x * lax.rsqrt(ms + EPS) * gain

    def rmsnorm(x, gain):
        ms = jnp.mean(x * x, axis=-1, keepdims=True)
        return x * lax.rsqrt(ms + EPS) * gain

    na_scale = NA_DIM ** -0.5
    naq = headnorm(proj_ref[:, P_NAQ:P_NAQ + NA_W].astype(f32), naq_g[...])
    qka_ref[:, A_NAQ:A_NAQ + NA_W] = (naq * na_scale).astype(bf16)
    nak = headnorm(proj_ref[:, P_NAK:P_NAK + NA_W].astype(f32), nak_g[...])
    qka_ref[:, A_NAK:A_NAK + NA_W] = nak.astype(bf16)

    sw_scale = SWA_DIM ** -0.5
    swq = headnorm(proj_ref[:, P_SWQ:P_SWQ + SWA_W].astype(f32), swq_g[...])
    for p in range(SWA_W // LANES):
        sl = slice(p * LANES, (p + 1) * LANES)
        qka_ref[:, A_SWQ + p * LANES:A_SWQ + (p + 1) * LANES] = (rope(swq[:, sl]) * sw_scale).astype(bf16)
    swk = headnorm(proj_ref[:, P_SWK:P_SWK + SWA_KV_W].astype(f32), swk_g[...])
    qka_ref[:, A_SWK:A_SWK + SWA_KV_W] = rope(swk).astype(bf16)

    mla_scale = MLA_QK ** -0.5
    cq = rmsnorm(proj_ref[:, P_CQ:P_CQ + MLA_Q_RANK].astype(f32), qn_ref[...]).astype(bf16)
    q = _mm(cq, wuq_ref[...])
    ckv = rmsnorm(proj_ref[:, P_CKV:P_CKV + MLA_KV_RANK].astype(f32), kvn_ref[...]).astype(bf16)
    kn = _mm(ckv, wuk_ref[...])
    mv_ref[...] = _mm(ckv, wuv_ref[...]).astype(bf16)
    kr = proj_ref[:, P_KR:P_KR + LANES].astype(f32)
    kr_ss = jnp.sum(kr * kr, axis=-1, keepdims=True)
    for h in range(MLA_HEADS):
        o = h * MLA_PAD
        qh = q[:, o:o + MLA_PAD]
        r = lax.rsqrt(jnp.sum(qh * qh, axis=-1, keepdims=True) * (1.0 / MLA_QK) + EPS)
        qh = qh * r * qg_ref[:, o:o + MLA_PAD]
        mq_ref[:, o:o + LANES] = (qh[:, :LANES] * mla_scale).astype(bf16)
        mq_ref[:, o + LANES:o + MLA_PAD] = (rope(qh[:, LANES:]) * mla_scale).astype(bf16)
        kh = kn[:, h * MLA_NOPE:(h + 1) * MLA_NOPE]
        r = lax.rsqrt((jnp.sum(kh * kh, axis=-1, keepdims=True) + kr_ss) * (1.0 / MLA_QK) + EPS)
        mk_ref[:, o:o + LANES] = (kh * r * kg_ref[:, o:o + LANES]).astype(bf16)
        mk_ref[:, o + LANES:o + MLA_PAD] = rope(kr * r * kg_ref[:, o + LANES:o + MLA_PAD]).astype(bf16)


def _prep(proj, cos, sin, gmat, lw):
    B, T, _ = proj.shape
    nt = T // TM

    def const(a):
        return pl.BlockSpec(a.shape, lambda b, i: (0,) * a.ndim)

    consts = [gmat, lw["naq_g"], lw["nak_g"], lw["swq_g"], lw["swk_g"], lw["qn"], lw["wuq"], lw["qg"],
              lw["kvn"], lw["wuk"], lw["wuv"], lw["kg"]]
    widths = (QKA_W, MLA_HEADS * MLA_PAD, MLA_HEADS * MLA_PAD, MLA_W)
    return pl.pallas_call(
        _prep_kernel,
        grid=(B, nt),
        in_specs=[
            pl.BlockSpec((None, TM, PROJ_W), lambda b, i: (b, i, 0)),
            pl.BlockSpec((TM, LANES), lambda b, i: (i, 0)),
            pl.BlockSpec((TM, LANES), lambda b, i: (i, 0)),
        ] + [const(a) for a in consts],
        out_specs=[pl.BlockSpec((None, TM, w), lambda b, i: (b, i, 0)) for w in widths],
        out_shape=[jax.ShapeDtypeStruct((B, T, w), bf16) for w in widths],
        compiler_params=_params("arbitrary", "arbitrary"),
        name="prep",
    )(proj, cos, sin, *consts)


def _half_masks():
    lane = lax.broadcasted_iota(jnp.int32, (1, LANES), 1)
    lo = jnp.where(lane < 64, 1.0, 0.0).astype(bf16)
    return lo, (1.0 - lo.astype(f32)).astype(bf16)


def _pair_attend(qp, parts, masks, sinks=None):
    out = None
    for half in range(2):
        hm = masks[half]
        qm = qp * hm
        scores = []
        for k, _, bias in parts:
            s = _nt(qm, k)
            if bias is not None:
                s = s + bias[half]
            scores.append(s)
        m = functools.reduce(jnp.maximum, [jnp.max(s, axis=-1, keepdims=True) for s in scores])
        if sinks is not None:
            m = jnp.maximum(m, sinks[half])
        ps = [jnp.exp(s - m) for s in scores]
        l = functools.reduce(jnp.add, [jnp.sum(p, axis=-1, keepdims=True) for p in ps])
        if sinks is not None:
            l = l + jnp.exp(sinks[half] - m)
        o = functools.reduce(jnp.add, [_mm(p.astype(bf16), v * hm) for p, (_, v, _) in zip(ps, parts)])
        o = o / l
        out = o if out is None else out + o
    return out


def _na_kernel(n_lat, S, rows, q_ref, k_ref, v_ref, bias_ref, o_ref):
    i = pl.program_id(1)
    masks = _half_masks()
    T = k_ref.shape[0]

    @pl.when(i < n_lat)
    def _():
        start = GRID_W * jnp.clip(NA_TILE_ROWS * i - NA_WIN_ROWS // 2, 0, rows - NA_BAND_ROWS)
        start = pl.multiple_of(start, GRID_W)
        for p in range(NA_W // LANES):
            sl = slice(p * LANES, (p + 1) * LANES)
            band = (k_ref[pl.ds(start, NA_BAND), sl], v_ref[pl.ds(start, NA_BAND), sl],
                    (bias_ref[2 * p], bias_ref[2 * p + 1]))
            ctx = (k_ref[S:T, sl], v_ref[S:T, sl], None)
            o_ref[:, sl] = _pair_attend(q_ref[:, sl], [band, ctx], masks).astype(bf16)

    @pl.when(i >= n_lat)
    def _():
        for p in range(NA_W // LANES):
            sl = slice(p * LANES, (p + 1) * LANES)
            ctx = (k_ref[S:T, sl], v_ref[S:T, sl], None)
            o_ref[:, sl] = _pair_attend(q_ref[:, sl], [ctx], masks).astype(bf16)


def _na(qka, proj, bias, S):
    B, T, _ = qka.shape
    nt, n_lat, rows = T // TM, S // TM, S // GRID_W

    def bias_idx(b, i):
        return (jnp.where(i == 0, 0, jnp.where(i >= n_lat - 1, 2, 1)), 0, 0, 0)

    return pl.pallas_call(
        functools.partial(_na_kernel, n_lat, S, rows),
        grid=(B, nt),
        in_specs=[
            pl.BlockSpec((None, TM, NA_W), lambda b, i: (b, i, A_NAQ // NA_W)),
            pl.BlockSpec((None, T, NA_W), lambda b, i: (b, 0, A_NAK // NA_W)),
            pl.BlockSpec((None, T, NA_W), lambda b, i: (b, 0, P_NAV // NA_W)),
            pl.BlockSpec((None, NA_HEADS, TM, NA_BAND), bias_idx),
        ],
        out_specs=pl.BlockSpec((None, TM, NA_W), lambda b, i: (b, i, 0)),
        out_shape=jax.ShapeDtypeStruct((B, T, NA_W), bf16),
        compiler_params=_params("arbitrary", "arbitrary"),
        name="na_attn",
    )(qka, qka, proj, bias)


def _na_bias(rpb, rows):
    n_tiles = rows // NA_TILE_ROWS
    col = np.arange(GRID_W)
    c0 = np.clip(col - NA_WIN_COLS // 2, 0, GRID_W - NA_WIN_COLS)
    col_ok = (col[None, :] >= c0[:, None]) & (col[None, :] < c0[:, None] + NA_WIN_COLS)
    dc = np.clip(col[None, :] - col[:, None] + NA_WIN_COLS - 1, 0, 2 * NA_WIN_COLS - 2)
    tz = jnp.where(col_ok, rpb[:, :, dc], NEG)
    dr = np.zeros((3, NA_TILE_ROWS, NA_BAND_ROWS), np.int32)
    ok = np.zeros((3, NA_TILE_ROWS, NA_BAND_ROWS), bool)
    for ty, rt in enumerate((0, 1, n_tiles - 1)):
        bs = int(np.clip(NA_TILE_ROWS * rt - NA_WIN_ROWS // 2, 0, rows - NA_BAND_ROWS))
        for qr in range(NA_TILE_ROWS):
            r = NA_TILE_ROWS * rt + qr
            s_r = int(np.clip(r - NA_WIN_ROWS // 2, 0, rows - NA_WIN_ROWS))
            for w in range(NA_BAND_ROWS):
                kr = bs + w
                ok[ty, qr, w] = s_r <= kr < s_r + NA_WIN_ROWS
                dr[ty, qr, w] = np.clip(kr - r + NA_WIN_ROWS - 1, 0, 2 * NA_WIN_ROWS - 2)
    blk = tz[:, dr]
    blk = jnp.where(ok[None, :, :, :, None, None], blk, NEG)
    blk = blk.transpose(1, 0, 2, 4, 3, 5)
    return blk.reshape(3, NA_HEADS, TM, NA_BAND)


def _mla_kernel(n_lat, S, q_ref, k_ref, v_ref, o_ref):
    i = pl.program_id(2)
    T = k_ref.shape[0]

    def attend(k, v):
        s = _nt(q_ref[...], k)
        m = jnp.max(s, axis=-1, keepdims=True)
        p = jnp.exp(s - m)
        l = jnp.sum(p, axis=-1, keepdims=True)
        o_ref[...] = (_mm(p.astype(bf16), v) / l).astype(bf16)

    @pl.when(i < n_lat)
    def _():
        attend(k_ref[...], v_ref[...])

    @pl.when(i >= n_lat)
    def _():
        attend(k_ref[S:T, :], v_ref[S:T, :])


def _mla(mq, mk, mv, S):
    B, T, _ = mq.shape
    nt, n_lat = T // TM, S // TM
    return pl.pallas_call(
        functools.partial(_mla_kernel, n_lat, S),
        grid=(B, MLA_HEADS, nt),
        in_specs=[
            pl.BlockSpec((None, TM, MLA_PAD), lambda b, h, i: (b, i, h)),
            pl.BlockSpec((None, T, MLA_PAD), lambda b, h, i: (b, 0, h)),
            pl.BlockSpec((None, T, MLA_V), lambda b, h, i: (b, 0, h)),
        ],
        out_specs=pl.BlockSpec((None, TM, MLA_V), lambda b, h, i: (b, i, h)),
        out_shape=jax.ShapeDtypeStruct((B, T, MLA_W), bf16),
        compiler_params=_params("arbitrary", "arbitrary", "arbitrary"),
        name="mla_attn",
    )(mq, mk, mv)


def _swa_kernel(n_lat, S, sink_ref, q_ref, k_ref, v_ref, o_ref):
    i = pl.program_id(1)
    masks = _half_masks()
    T = k_ref.shape[0]
    kc, vc = k_ref[S:T, :], v_ref[S:T, :]

    def sinks(p):
        return (sink_ref[SWA_HEAD_ORDER[2 * p]], sink_ref[SWA_HEAD_ORDER[2 * p + 1]])

    @pl.when(i < n_lat)
    def _():
        start = pl.multiple_of(jnp.clip(i * TM - SWA_WINDOW, 0, S - SWA_KEYS), SWA_WINDOW)
        qpos = i * TM + lax.broadcasted_iota(jnp.int32, (TM, SWA_KEYS), 0)
        kpos = start + lax.broadcasted_iota(jnp.int32, (TM, SWA_KEYS), 1)
        band = jnp.where(jnp.abs(qpos - kpos) <= SWA_WINDOW, 0.0, NEG)
        kb, vb = k_ref[pl.ds(start, SWA_KEYS), :], v_ref[pl.ds(start, SWA_KEYS), :]
        for p in range(SWA_W // LANES):
            sl = slice(p * LANES, (p + 1) * LANES)
            parts = [(kb, vb, (band, band)), (kc, vc, None)]
            o_ref[:, sl] = _pair_attend(q_ref[:, sl], parts, masks, sinks(p)).astype(bf16)

    @pl.when(i >= n_lat)
    def _():
        for p in range(SWA_W // LANES):
            sl = slice(p * LANES, (p + 1) * LANES)
            o_ref[:, sl] = _pair_attend(q_ref[:, sl], [(kc, vc, None)], masks, sinks(p)).astype(bf16)


def _swa(qka, proj, sink, S):
    B, T, _ = qka.shape
    nt, n_lat = T // TM, S // TM
    return pl.pallas_call(
        functools.partial(_swa_kernel, n_lat, S),
        grid=(B, nt),
        in_specs=[
            pl.BlockSpec(memory_space=pltpu.SMEM),
            pl.BlockSpec((None, TM, SWA_W), lambda b, i: (b, i, A_SWQ // SWA_W)),
            pl.BlockSpec((None, T, SWA_KV_W), lambda b, i: (b, 0, A_SWK // SWA_KV_W)),
            pl.BlockSpec((None, T, SWA_KV_W), lambda b, i: (b, 0, P_SWV // SWA_KV_W)),
        ],
        out_specs=pl.BlockSpec((None, TM, SWA_W), lambda b, i: (b, i, 0)),
        out_shape=jax.ShapeDtypeStruct((B, T, SWA_W), bf16),
        compiler_params=_params("arbitrary", "arbitrary"),
        name="swa_attn",
    )(sink, qka, qka, proj)


def _post_kernel(ona_ref, omla_ref, oswa_ref, gl_ref, x_ref, mod_ref, wna_ref, wmla_ref, wswa_ref, wo_ref,
                 g2_ref, rt_ref, xo_ref, h2_ref, aff_ref):
    D = x_ref.shape[1]
    merged = None
    for j, (o_ref, w_ref) in enumerate(((ona_ref, wna_ref), (omla_ref, wmla_ref), (oswa_ref, wswa_ref))):
        y = _mm(o_ref[...], w_ref[...])
        g = jax.nn.sigmoid(gl_ref[:, j * D:(j + 1) * D].astype(f32))
        merged = g * y if merged is None else merged + g * y
    res = _mm(merged.astype(bf16), wo_ref[...])
    x = x_ref[...] + mod_ref[2:3, :] * res
    xo_ref[...] = x
    ms = jnp.mean(x * x, axis=-1, keepdims=True)
    h2 = x * lax.rsqrt(ms + EPS) * g2_ref[...]
    h2 = h2 * (1.0 + mod_ref[4:5, :]) + mod_ref[3:4, :]
    h2_ref[...] = h2.astype(bf16)
    logits = lax.dot_general(rt_ref[...], h2, (((1,), (1,)), ((), ())), preferred_element_type=f32,
                             precision=lax.Precision.HIGHEST)
    e = jnp.exp(logits - jnp.max(logits, axis=0, keepdims=True))
    aff_ref[...] = e / jnp.sum(e, axis=0, keepdims=True)


def _post(ona, omla, oswa, gl, x, mod, lw, n_lat):
    B, T, D = x.shape
    nt = T // TM
    E = lw["routerT"].shape[0]

    def tok(w):
        return pl.BlockSpec((None, TM, w), lambda b, i: (b, i, 0))

    def const(a):
        return pl.BlockSpec(a.shape, lambda b, i: (0,) * a.ndim)

    consts = [lw["wna"], lw["wmla"], lw["wswa"], lw["wo"], lw["g2"], lw["routerT"]]
    return pl.pallas_call(
        _post_kernel,
        grid=(B, nt),
        in_specs=[tok(NA_W), tok(MLA_W), tok(SWA_W), tok(N_BRANCH * D), tok(D), _mod_spec(n_lat, D)]
        + [const(a) for a in consts],
        out_specs=[tok(D), tok(D), pl.BlockSpec((None, E, TM), lambda b, i: (b, 0, i))],
        out_shape=[jax.ShapeDtypeStruct((B, T, D), f32), jax.ShapeDtypeStruct((B, T, D), bf16),
                   jax.ShapeDtypeStruct((B, E, T), f32)],
        compiler_params=_params("arbitrary", "arbitrary"),
        name="post_attn",
    )(ona, omla, oswa, gl, x, mod, *consts)


def _lane_cumsum(mask, tri):
    E, n = mask.shape
    carry = jnp.zeros((E, 1), f32)
    outs = []
    for k in range(n // LANES):
        w = _mm(mask[:, k * LANES:(k + 1) * LANES].astype(bf16), tri) + carry
        outs.append(w)
        carry = w[:, LANES - 1:LANES]
    return jnp.concatenate(outs, axis=1)


def _select_slots(aff, cap, base, tri):
    bits = lax.bitcast_convert_type(aff, jnp.int32)
    thr = jnp.zeros((aff.shape[0], 1), jnp.int32)
    for bit in range(30, -1, -1):
        cand = thr | (1 << bit)
        cnt = jnp.sum(jnp.where(bits >= cand, 1.0, 0.0), axis=1, keepdims=True)
        thr = jnp.where(cnt >= cap, cand, thr)
    gt = jnp.where(bits > thr, 1.0, 0.0)
    eq = jnp.where(bits == thr, 1.0, 0.0)
    need = cap - jnp.sum(gt, axis=1, keepdims=True)
    sel = jnp.maximum(gt, jnp.where(_lane_cumsum(eq, tri) <= need, eq, 0.0))
    return jnp.where(sel > 0.0, _lane_cumsum(sel, tri) + base, 0.0)


def _topk_kernel(S, cap_s, cap_l, aff_ref, tri_ref, cp_ref):
    T = aff_ref.shape[1]
    tri = tri_ref[...]
    cp_ref[:, 0:S] = _select_slots(aff_ref[:, 0:S], cap_s, 0.0, tri)
    cp_ref[:, S:T] = _select_slots(aff_ref[:, S:T], cap_l, float(cap_s), tri)


def _topk(aff, S, cap_s, cap_l):
    B, E, T = aff.shape
    tri = jnp.asarray(np.triu(np.ones((LANES, LANES), np.float32)), bf16)
    return pl.pallas_call(
        functools.partial(_topk_kernel, S, cap_s, cap_l),
        grid=(B,),
        in_specs=[pl.BlockSpec((None, E, T), lambda b: (b, 0, 0)),
                  pl.BlockSpec((LANES, LANES), lambda b: (0, 0))],
        out_specs=pl.BlockSpec((None, E, T), lambda b: (b, 0, 0)),
        out_shape=jax.ShapeDtypeStruct((B, E, T), f32),
        compiler_params=_params("arbitrary"),
        name="expert_select",
    )(aff, tri)


def _moe_kernel(n_slots, h2_ref, cp_ref, aff_ref, wg_ref, wu_ref, wdt_ref, out_ref, xg_ref):
    e = pl.program_id(1)
    n_chunks = cp_ref.shape[0]

    @pl.when(e == 0)
    def _():
        out_ref[...] = jnp.zeros_like(out_ref)

    slot = (lax.broadcasted_iota(jnp.int32, (n_slots, 1), 0) + 1).astype(f32)
    xg_ref[...] = jnp.zeros_like(xg_ref)

    def gather(k, gate):
        hit = cp_ref[k] == slot
        tok = h2_ref[pl.ds(pl.multiple_of(k * MOE_CHUNK, MOE_CHUNK), MOE_CHUNK), :]
        xg_ref[...] += _mm(jnp.where(hit, 1.0, 0.0).astype(bf16), tok)
        return gate + jnp.sum(jnp.where(hit, aff_ref[k], 0.0), axis=1, keepdims=True)

    gate = lax.fori_loop(0, n_chunks, gather, jnp.zeros((n_slots, 1), f32))
    xg = xg_ref[...].astype(bf16)
    a = _mm(xg, wg_ref[...])
    u = _mm(xg, wu_ref[...])
    act = (a * jax.nn.sigmoid(a) * u * gate).astype(bf16)
    yt = _nt(wdt_ref[...], act).astype(bf16)

    def scatter(k, carry):
        hit = cp_ref[k] == slot
        out_ref[k] += _mm(yt, jnp.where(hit, 1.0, 0.0).astype(bf16))
        return carry

    lax.fori_loop(0, n_chunks, scatter, 0)


def _moe(h2, cp, aff, wg, wu, wdt, n_slots):
    B, T, D = h2.shape
    E = cp.shape[1]
    F = wg.shape[2]
    nc = T // MOE_CHUNK
    cp = cp.reshape(B, E, nc, 1, MOE_CHUNK)
    aff = aff.reshape(B, E, nc, 1, MOE_CHUNK)
    row = pl.BlockSpec((None, None, nc, 1, MOE_CHUNK), lambda b, e: (b, e, 0, 0, 0))
    return pl.pallas_call(
        functools.partial(_moe_kernel, n_slots),
        grid=(B, E),
        in_specs=[
            pl.BlockSpec((None, T, D), lambda b, e: (b, 0, 0), pipeline_mode=pl.Buffered(1)),
            row, row,
            pl.BlockSpec((None, D, F), lambda b, e: (e, 0, 0)),
            pl.BlockSpec((None, D, F), lambda b, e: (e, 0, 0)),
            pl.BlockSpec((None, D, F), lambda b, e: (e, 0, 0)),
        ],
        out_specs=pl.BlockSpec((None, nc, D, MOE_CHUNK), lambda b, e: (b, 0, 0, 0), pipeline_mode=pl.Buffered(1)),
        out_shape=jax.ShapeDtypeStruct((B, nc, D, MOE_CHUNK), f32),
        scratch_shapes=[pltpu.VMEM((n_slots, D), f32)],
        compiler_params=_params("arbitrary", "arbitrary"),
        name="moe_ffn",
    )(h2, cp, aff, wg, wu, wdt)


def _final_kernel(x_ref, moe_ref, mod_ref, o_ref):
    o_ref[...] = x_ref[...] + mod_ref[5:6, :] * moe_ref[...].T


def _final(x, moe, mod, S):
    B, T, D = x.shape
    n_lat = S // TM
    return pl.pallas_call(
        _final_kernel,
        grid=(B, n_lat),
        in_specs=[
            pl.BlockSpec((None, TM, D), lambda b, i: (b, i, 0)),
            pl.BlockSpec((None, None, D, TM), lambda b, i: (b, i, 0, 0)),
            pl.BlockSpec((None, None, 6, D), lambda b, i: (b, 0, 0, 0)),
        ],
        out_specs=pl.BlockSpec((None, TM, D), lambda b, i: (b, i, 0)),
        out_shape=jax.ShapeDtypeStruct((B, S, D), f32),
        compiler_params=_params("arbitrary", "arbitrary"),
        name="final_residual",
    )(x, moe, mod)


def _rope_tables(S, L):
    t = jnp.arange(S)
    row, col = t // GRID_W, t % GRID_W
    f = 16
    inv = ROPE_BASE ** (-jnp.arange(f, dtype=f32) / f)
    ar = row.astype(f32)[:, None] * inv
    ac = col.astype(f32)[:, None] * inv
    cos = jnp.concatenate([jnp.cos(ar), jnp.cos(ar), jnp.cos(ac), jnp.cos(ac)], axis=1)
    sin = jnp.concatenate([-jnp.sin(ar), jnp.sin(ar), -jnp.sin(ac), jnp.sin(ac)], axis=1)
    cos = jnp.concatenate([jnp.tile(cos, (1, 2)), jnp.ones((L, LANES), f32)], axis=0)
    sin = jnp.concatenate([jnp.tile(sin, (1, 2)), jnp.zeros((L, LANES), f32)], axis=0)
    return cos, sin


def _pad_heads(w, n_heads, width, padded):
    lead = w.shape[:-1]
    w = w.reshape(lead + (n_heads, width))
    w = jnp.pad(w, [(0, 0)] * len(lead) + [(0, 0), (0, padded - width)])
    return w.reshape(lead + (n_heads * padded,))


def _layer_weights(l, D, w_in, norm2_g, na_q_g, na_k_g, mla_q_norm, mla_w_uq, mla_kv_norm, mla_w_ukv,
                   mla_q_g, mla_k_g, swa_q_g, swa_k_g, w_na_o, w_mla_o, w_swa_o, w_o, router,
                   w_gate, w_up, w_down):
    o = np.cumsum((0, NA_W, NA_W, NA_W, MLA_Q_RANK, MLA_KV_RANK, MLA_ROPE, SWA_W, SWA_KV_W, SWA_KV_W))
    o_naq, o_nak, o_nav, o_cq, o_ckv, o_kr, o_swq, o_swk, o_swv, o_gl = (int(v) for v in o)
    wi = w_in[l]
    swq = wi[:, o_swq:o_swq + SWA_W].reshape(D, SWA_HEADS, SWA_DIM)[:, SWA_HEAD_ORDER, :].reshape(D, SWA_W)
    kr = jnp.pad(wi[:, o_kr:o_kr + MLA_ROPE], ((0, 0), (0, LANES - MLA_ROPE)))
    w_all = jnp.concatenate([
        wi[:, o_naq:o_naq + 3 * NA_W], swq, wi[:, o_cq:o_cq + MLA_Q_RANK], wi[:, o_ckv:o_ckv + MLA_KV_RANK], kr,
        wi[:, o_swk:o_swk + 2 * SWA_KV_W], wi[:, o_gl:]], axis=1).astype(bf16)
    ukv = mla_w_ukv[l].reshape(MLA_KV_RANK, MLA_HEADS, MLA_NOPE + MLA_V)
    swa_o = w_swa_o[l].reshape(SWA_HEADS, SWA_DIM, D)[SWA_HEAD_ORDER, :, :].reshape(SWA_W, D)
    return dict(
        w_all=w_all,
        naq_g=jnp.tile(na_q_g[l], NA_HEADS)[None], nak_g=jnp.tile(na_k_g[l], NA_HEADS)[None],
        swq_g=jnp.tile(swa_q_g[l], SWA_HEADS)[None], swk_g=jnp.tile(swa_k_g[l], SWA_KV_HEADS)[None],
        qn=mla_q_norm[l][None], kvn=mla_kv_norm[l][None],
        wuq=_pad_heads(mla_w_uq[l], MLA_HEADS, MLA_QK, MLA_PAD).astype(bf16),
        wuk=ukv[:, :, :MLA_NOPE].reshape(MLA_KV_RANK, MLA_HEADS * MLA_NOPE).astype(bf16),
        wuv=ukv[:, :, MLA_NOPE:].reshape(MLA_KV_RANK, MLA_W).astype(bf16),
        qg=jnp.tile(jnp.pad(mla_q_g[l], (0, MLA_PAD - MLA_QK)), MLA_HEADS)[None],
        kg=jnp.tile(jnp.pad(mla_k_g[l], (0, MLA_PAD - MLA_QK)), MLA_HEADS)[None],
        wna=w_na_o[l].astype(bf16), wmla=w_mla_o[l].astype(bf16), wswa=swa_o.astype(bf16),
        wo=w_o[l].astype(bf16), g2=norm2_g[l][None], routerT=router[l].T,
        wg=w_gate[l].astype(bf16), wu=w_up[l].astype(bf16), wdt=jnp.swapaxes(w_down[l], 1, 2).astype(bf16),
    )


def kernel(x, c, ctx, c_ctx, norm1_g, norm2_g, w_ada, b_ada, w_in, na_q_g, na_k_g, na_rpb, mla_q_norm, mla_w_uq, mla_kv_norm, mla_w_ukv, mla_q_g, mla_k_g, swa_q_g, swa_k_g, swa_sink, w_na_o, w_mla_o, w_swa_o, w_o, router, w_gate, w_up, w_down):
    B, S, D = x.shape
    L = ctx.shape[1]
    depth = w_in.shape[0]
    E = router.shape[2]
    T = S + L
    n_lat = S // TM
    rows = S // GRID_W
    assert S % TM == 0 and L % TM == 0 and L % LANES == 0 and rows >= NA_BAND_ROWS and S >= SWA_KEYS and B < 16
    cap_s = CAPACITY_FACTOR * S // E
    cap_l = CAPACITY_FACTOR * L // E
    n_slots = cap_s + cap_l

    cpad = jnp.zeros((16, D), f32).at[:B].set(c).at[B].set(c_ctx)
    mod = _ada(cpad, w_ada, b_ada).reshape(depth, 16, 6, D)
    mod = jnp.stack([mod[:, :B], jnp.broadcast_to(mod[:, B:B + 1], (depth, B, 6, D))], axis=2)

    cos, sin = _rope_tables(S, L)
    gmat = jnp.asarray(np.kron(np.eye(NA_HEADS), np.full((NA_DIM, NA_DIM), 1.0 / NA_DIM)), bf16)

    xs = jnp.concatenate([x, ctx], axis=1)
    moe = None
    for l in range(depth):
        lw = _layer_weights(l, D, w_in, norm2_g, na_q_g, na_k_g, mla_q_norm, mla_w_uq, mla_kv_norm, mla_w_ukv,
                            mla_q_g, mla_k_g, swa_q_g, swa_k_g, w_na_o, w_mla_o, w_swa_o, w_o, router,
                            w_gate, w_up, w_down)
        modp = mod[l - 1] if l > 0 else None
        xs, proj, gl = _in_proj(xs, moe, modp, mod[l], norm1_g[l][None], lw["w_all"], n_lat)
        qka, mq, mk, mv = _prep(proj, cos, sin, gmat, lw)
        ona = _na(qka, proj, _na_bias(na_rpb[l], rows), S)
        omla = _mla(mq, mk, mv, S)
        oswa = _swa(qka, proj, swa_sink[l], S)
        xs, h2, aff = _post(ona, omla, oswa, gl, xs, mod[l], lw, n_lat)
        cp = _topk(aff, S, cap_s, cap_l)
        moe = _moe(h2, cp, aff, lw["wg"], lw["wu"], lw["wdt"], n_slots)
    return _final(xs, moe, mod[depth - 1], S)
```

```python
import functools

import numpy as np
import jax
import jax.numpy as jnp
from jax import lax
from jax.experimental import pallas as pl
from jax.experimental.pallas import tpu as pltpu

GRID_W = 64
NA_HEADS, NA_DIM, NA_WIN_ROWS, NA_WIN_COLS = 6, 64, 8, 16
MLA_HEADS, MLA_Q_RANK, MLA_KV_RANK, MLA_NOPE, MLA_ROPE, MLA_V = 4, 256, 128, 128, 64, 128
MLA_QK = MLA_NOPE + MLA_ROPE
MLA_PAD = 256
SWA_HEADS, SWA_KV_HEADS, SWA_DIM, SWA_WINDOW = 6, 2, 64, 128
CAPACITY_FACTOR = 2
N_BRANCH = 3
ROPE_BASE = 10000.0
EPS = 1e-6

NA_W = NA_HEADS * NA_DIM
SWA_W = SWA_HEADS * SWA_DIM
SWA_KV_W = SWA_KV_HEADS * SWA_DIM
MLA_W = MLA_HEADS * MLA_V

LANES = 128
TM = 256
NA_TILE_ROWS = TM // GRID_W
NA_BAND_ROWS = NA_TILE_ROWS + NA_WIN_ROWS - 1
NA_BAND = NA_BAND_ROWS * GRID_W
SWA_KEYS = TM + 2 * SWA_WINDOW
MOE_CHUNK = 256
NEG = -1e30
VMEM_LIMIT = 56 * 1024 * 1024

P_NAQ, P_NAK, P_NAV, P_SWQ = 0, 384, 768, 1152
P_CQ, P_CKV, P_KR, P_SWK, P_SWV = 1536, 1792, 1920, 2048, 2176
PROJ_W = 2304
N_CHUNK = 768
A_NAQ, A_NAK, A_SWQ, A_SWK = 0, 384, 768, 1152
QKA_W = 1280
SWA_HEAD_ORDER = (0, 3, 1, 4, 2, 5)

bf16 = jnp.bfloat16
f32 = jnp.float32


def _mm(a, b):
    return jnp.dot(a, b, preferred_element_type=f32)


def _nt(a, b):
    return lax.dot_general(a, b, (((1,), (1,)), ((), ())), preferred_element_type=f32)


def _params(*sem):
    return pltpu.CompilerParams(dimension_semantics=sem, vmem_limit_bytes=VMEM_LIMIT)


def _ada_kernel(c_ref, w_ref, b_ref, o_ref):
    a = c_ref[...]
    a = (a * jax.nn.sigmoid(a)).astype(bf16)
    o_ref[...] = _mm(a, w_ref[...].astype(bf16)) + b_ref[...]


def _ada(cpad, w_ada, b_ada):
    L, D, N = w_ada.shape
    tn = 1536
    return pl.pallas_call(
        _ada_kernel,
        grid=(L, N // tn),
        in_specs=[
            pl.BlockSpec((16, D), lambda l, j: (0, 0)),
            pl.BlockSpec((None, D, tn), lambda l, j: (l, 0, j)),
            pl.BlockSpec((None, 1, tn), lambda l, j: (l, 0, j)),
        ],
        out_specs=pl.BlockSpec((None, 16, tn), lambda l, j: (l, 0, j)),
        out_shape=jax.ShapeDtypeStruct((L, 16, N), f32),
        compiler_params=_params("arbitrary", "arbitrary"),
        name="ada",
    )(cpad, w_ada, b_ada.reshape(L, 1, N))


def _in_proj_kernel(has_moe, n_gl_chunks, *refs):
    if has_moe:
        x_ref, moe_ref, modp_ref, mod_ref, g_ref, w_ref, xo_ref, proj_ref, gl_ref = refs
        x = x_ref[...] + modp_ref[5:6, :] * moe_ref[...].T
        xo_ref[...] = x
    else:
        x_ref, mod_ref, g_ref, w_ref, proj_ref, gl_ref = refs
        x = x_ref[...]
    ms = jnp.mean(x * x, axis=-1, keepdims=True)
    h = x * lax.rsqrt(ms + EPS) * g_ref[...]
    h = (h * (1.0 + mod_ref[1:2, :]) + mod_ref[0:1, :]).astype(bf16)
    for c in range(PROJ_W // N_CHUNK):
        sl = slice(c * N_CHUNK, (c + 1) * N_CHUNK)
        proj_ref[:, sl] = _mm(h, w_ref[:, sl]).astype(bf16)
    for c in range(n_gl_chunks):
        sl = slice(c * N_CHUNK, (c + 1) * N_CHUNK)
        gl_ref[:, sl] = _mm(h, w_ref[:, PROJ_W + c * N_CHUNK:PROJ_W + (c + 1) * N_CHUNK]).astype(bf16)


def _mod_spec(n_lat, D):
    return pl.BlockSpec((None, None, 6, D), lambda b, i: (b, jnp.where(i >= n_lat, 1, 0), 0, 0))


def _in_proj(x, moe, modp, mod, g1, w, n_lat):
    B, T, D = x.shape
    nt = T // TM
    has_moe = moe is not None
    n_gl_chunks = N_BRANCH * D // N_CHUNK
    tok = pl.BlockSpec((None, TM, D), lambda b, i: (b, i, 0))
    in_specs = [tok]
    args = [x]
    if has_moe:
        in_specs += [pl.BlockSpec((None, None, D, TM), lambda b, i: (b, i, 0, 0)), _mod_spec(n_lat, D)]
        args += [moe, modp]
    in_specs += [
        _mod_spec(n_lat, D),
        pl.BlockSpec((1, D), lambda b, i: (0, 0)),
        pl.BlockSpec(w.shape, lambda b, i: (0, 0), pipeline_mode=pl.Buffered(1)),
    ]
    args += [mod, g1, w]
    out_specs = [
        pl.BlockSpec((None, TM, PROJ_W), lambda b, i: (b, i, 0)),
        pl.BlockSpec((None, TM, N_BRANCH * D), lambda b, i: (b, i, 0)),
    ]
    out_shape = [
        jax.ShapeDtypeStruct((B, T, PROJ_W), bf16),
        jax.ShapeDtypeStruct((B, T, N_BRANCH * D), bf16),
    ]
    if has_moe:
        out_specs = [tok] + out_specs
        out_shape = [jax.ShapeDtypeStruct((B, T, D), f32)] + out_shape
    res = pl.pallas_call(
        functools.partial(_in_proj_kernel, has_moe, n_gl_chunks),
        grid=(B, nt),
        in_specs=in_specs,
        out_specs=out_specs,
        out_shape=out_shape,
        compiler_params=_params("arbitrary", "arbitrary"),
        name="in_proj",
    )(*args)
    if has_moe:
        return res
    return [x] + list(res)


def _prep_kernel(proj_ref, cos_ref, sin_ref, gmat_ref, naq_g, nak_g, swq_g, swk_g,
                 qn_ref, wuq_ref, qg_ref, kvn_ref, wuk_ref, wuv_ref, kg_ref,
                 qka_ref, mq_ref, mk_ref, mv_ref):
    cos = cos_ref[...]
    sin = sin_ref[...]
    lane = lax.broadcasted_iota(jnp.int32, cos.shape, 1)
    first = (lane & 16) == 0

    def rope(x):
        partner = jnp.where(first, pltpu.roll(x, LANES - 16, 1), pltpu.roll(x, 16, 1))
        return x * cos + partner * sin

    def headnorm(x, gain):
        w = x.shape[1]
        ms = _mm((x * x).astype(bf16), gmat_ref[:w, :w])
        return x * lax.rsqrt(ms + EPS) * gain

    def rmsnorm(x, gain):
        ms = jnp.mean(x * x, axis=-1, keepdims=True)
        return x * lax.rsqrt(ms + EPS) * gain

    na_scale = NA_DIM ** -0.5
    naq = headnorm(proj_ref[:, P_NAQ:P_NAQ + NA_W].astype(f32), naq_g[...])
    qka_ref[:, A_NAQ:A_NAQ + NA_W] = (naq * na_scale).astype(bf16)
    nak = headnorm(proj_ref[:, P_NAK:P_NAK + NA_W].astype(f32), nak_g[...])
    qka_ref[:, A_NAK:A_NAK + NA_W] = nak.astype(bf16)

    sw_scale = SWA_DIM ** -0.5
    swq = headnorm(proj_ref[:, P_SWQ:P_SWQ + SWA_W].astype(f32), swq_g[...])
    for p in range(SWA_W // LANES):
        sl = slice(p * LANES, (p + 1) * LANES)
        qka_ref[:, A_SWQ + p * LANES:A_SWQ + (p + 1) * LANES] = (rope(swq[:, sl]) * sw_scale).astype(bf16)
    swk = headnorm(proj_ref[:, P_SWK:P_SWK + SWA_KV_W].astype(f32), swk_g[...])
    qka_ref[:, A_SWK:A_SWK + SWA_KV_W] = rope(swk).astype(bf16)

    mla_scale = MLA_QK ** -0.5
    cq = rmsnorm(proj_ref[:, P_CQ:P_CQ + MLA_Q_RANK].astype(f32), qn_ref[...]).astype(bf16)
    q = _mm(cq, wuq_ref[...])
    ckv = rmsnorm(proj_ref[:, P_CKV:P_CKV + MLA_KV_RANK].astype(f32), kvn_ref[...]).astype(bf16)
    kn = _mm(ckv, wuk_ref[...])
    mv_ref[...] = _mm(ckv, wuv_ref[...]).astype(bf16)
    kr = proj_ref[:, P_KR:P_KR + LANES].astype(f32)
    kr_ss = jnp.sum(kr * kr, axis=-1, keepdims=True)
    for h in range(MLA_HEADS):
        o = h * MLA_PAD
        qh = q[:, o:o + MLA_PAD]
        r = lax.rsqrt(jnp.sum(qh * qh, axis=-1, keepdims=True) * (1.0 / MLA_QK) + EPS)
        qh = qh * r * qg_ref[:, o:o + MLA_PAD]
        mq_ref[:, o:o + LANES] = (qh[:, :LANES] * mla_scale).astype(bf16)
        mq_ref[:, o + LANES:o + MLA_PAD] = (rope(qh[:, LANES:]) * mla_scale).astype(bf16)
        kh = kn[:, h * MLA_NOPE:(h + 1) * MLA_NOPE]
        r = lax.rsqrt((jnp.sum(kh * kh, axis=-1, keepdims=True) + kr_ss) * (1.0 / MLA_QK) + EPS)
        mk_ref[:, o:o + LANES] = (kh * r * kg_ref[:, o:o + LANES]).astype(bf16)
        mk_ref[:, o + LANES:o + MLA_PAD] = rope(kr * r * kg_ref[:, o + LANES:o + MLA_PAD]).astype(bf16)


def _prep(proj, cos, sin, gmat, lw):
    B, T, _ = proj.shape
    nt = T // TM

    def const(a):
        return pl.BlockSpec(a.shape, lambda b, i: (0,) * a.ndim)

    consts = [gmat, lw["naq_g"], lw["nak_g"], lw["swq_g"], lw["swk_g"], lw["qn"], lw["wuq"], lw["qg"],
              lw["kvn"], lw["wuk"], lw["wuv"], lw["kg"]]
    widths = (QKA_W, MLA_HEADS * MLA_PAD, MLA_HEADS * MLA_PAD, MLA_W)
    return pl.pallas_call(
        _prep_kernel,
        grid=(B, nt),
        in_specs=[
            pl.BlockSpec((None, TM, PROJ_W), lambda b, i: (b, i, 0)),
            pl.BlockSpec((TM, LANES), lambda b, i: (i, 0)),
            pl.BlockSpec((TM, LANES), lambda b, i: (i, 0)),
        ] + [const(a) for a in consts],
        out_specs=[pl.BlockSpec((None, TM, w), lambda b, i: (b, i, 0)) for w in widths],
        out_shape=[jax.ShapeDtypeStruct((B, T, w), bf16) for w in widths],
        compiler_params=_params("arbitrary", "arbitrary"),
        name="prep",
    )(proj, cos, sin, *consts)


def _half_masks():
    lane = lax.broadcasted_iota(jnp.int32, (1, LANES), 1)
    lo = jnp.where(lane < 64, 1.0, 0.0).astype(bf16)
    return lo, (1.0 - lo.astype(f32)).astype(bf16)


def _pair_attend(qp, parts, masks, sinks=None):
    out = None
    for half in range(2):
        hm = masks[half]
        qm = qp * hm
        scores = []
        for k, _, bias in parts:
            s = _nt(qm, k)
            if bias is not None:
                s = s + bias[half]
            scores.append(s)
        m = functools.reduce(jnp.maximum, [jnp.max(s, axis=-1, keepdims=True) for s in scores])
        if sinks is not None:
            m = jnp.maximum(m, sinks[half])
        ps = [jnp.exp(s - m) for s in scores]
        l = functools.reduce(jnp.add, [jnp.sum(p, axis=-1, keepdims=True) for p in ps])
        if sinks is not None:
            l = l + jnp.exp(sinks[half] - m)
        o = functools.reduce(jnp.add, [_mm(p.astype(bf16), v * hm) for p, (_, v, _) in zip(ps, parts)])
        o = o / l
        out = o if out is None else out + o
    return out


def _na_kernel(n_lat, S, rows, q_ref, k_ref, v_ref, bias_ref, o_ref):
    i = pl.program_id(1)
    masks = _half_masks()
    T = k_ref.shape[0]

    @pl.when(i < n_lat)
    def _():
        start = GRID_W * jnp.clip(NA_TILE_ROWS * i - NA_WIN_ROWS // 2, 0, rows - NA_BAND_ROWS)
        start = pl.multiple_of(start, GRID_W)
        for p in range(NA_W // LANES):
            sl = slice(p * LANES, (p + 1) * LANES)
            band = (k_ref[pl.ds(start, NA_BAND), sl], v_ref[pl.ds(start, NA_BAND), sl],
                    (bias_ref[2 * p], bias_ref[2 * p + 1]))
            ctx = (k_ref[S:T, sl], v_ref[S:T, sl], None)
            o_ref[:, sl] = _pair_attend(q_ref[:, sl], [band, ctx], masks).astype(bf16)

    @pl.when(i >= n_lat)
    def _():
        for p in range(NA_W // LANES):
            sl = slice(p * LANES, (p + 1) * LANES)
            ctx = (k_ref[S:T, sl], v_ref[S:T, sl], None)
            o_ref[:, sl] = _pair_attend(q_ref[:, sl], [ctx], masks).astype(bf16)


def _na(qka, proj, bias, S):
    B, T, _ = qka.shape
    nt, n_lat, rows = T // TM, S // TM, S // GRID_W

    def bias_idx(b, i):
        return (jnp.where(i == 0, 0, jnp.where(i >= n_lat - 1, 2, 1)), 0, 0, 0)

    return pl.pallas_call(
        functools.partial(_na_kernel, n_lat, S, rows),
        grid=(B, nt),
        in_specs=[
            pl.BlockSpec((None, TM, NA_W), lambda b, i: (b, i, A_NAQ // NA_W)),
            pl.BlockSpec((None, T, NA_W), lambda b, i: (b, 0, A_NAK // NA_W)),
            pl.BlockSpec((None, T, NA_W), lambda b, i: (b, 0, P_NAV // NA_W)),
            pl.BlockSpec((None, NA_HEADS, TM, NA_BAND), bias_idx),
        ],
        out_specs=pl.BlockSpec((None, TM, NA_W), lambda b, i: (b, i, 0)),
        out_shape=jax.ShapeDtypeStruct((B, T, NA_W), bf16),
        compiler_params=_params("arbitrary", "arbitrary"),
        name="na_attn",
    )(qka, qka, proj, bias)


def _na_bias(rpb, rows):
    n_tiles = rows // NA_TILE_ROWS
    col = np.arange(GRID_W)
    c0 = np.clip(col - NA_WIN_COLS // 2, 0, GRID_W - NA_WIN_COLS)
    col_ok = (col[None, :] >= c0[:, None]) & (col[None, :] < c0[:, None] + NA_WIN_COLS)
    dc = np.clip(col[None, :] - col[:, None] + NA_WIN_COLS - 1, 0, 2 * NA_WIN_COLS - 2)
    tz = jnp.where(col_ok, rpb[:, :, dc], NEG)
    dr = np.zeros((3, NA_TILE_ROWS, NA_BAND_ROWS), np.int32)
    ok = np.zeros((3, NA_TILE_ROWS, NA_BAND_ROWS), bool)
    for ty, rt in enumerate((0, 1, n_tiles - 1)):
        bs = int(np.clip(NA_TILE_ROWS * rt - NA_WIN_ROWS // 2, 0, rows - NA_BAND_ROWS))
        for qr in range(NA_TILE_ROWS):
            r = NA_TILE_ROWS * rt + qr
            s_r = int(np.clip(r - NA_WIN_ROWS // 2, 0, rows - NA_WIN_ROWS))
            for w in range(NA_BAND_ROWS):
                kr = bs + w
                ok[ty, qr, w] = s_r <= kr < s_r + NA_WIN_ROWS
                dr[ty, qr, w] = np.clip(kr - r + NA_WIN_ROWS - 1, 0, 2 * NA_WIN_ROWS - 2)
    blk = tz[:, dr]
    blk = jnp.where(ok[None, :, :, :, None, None], blk, NEG)
    blk = blk.transpose(1, 0, 2, 4, 3, 5)
    return blk.reshape(3, NA_HEADS, TM, NA_BAND)


def _mla_kernel(n_lat, S, q_ref, k_ref, v_ref, o_ref):
    i = pl.program_id(2)
    T = k_ref.shape[0]

    def attend(k, v):
        s = _nt(q_ref[...], k)
        m = jnp.max(s, axis=-1, keepdims=True)
        p = jnp.exp(s - m)
        l = jnp.sum(p, axis=-1, keepdims=True)
        o_ref[...] = (_mm(p.astype(bf16), v) / l).astype(bf16)

    @pl.when(i < n_lat)
    def _():
        attend(k_ref[...], v_ref[...])

    @pl.when(i >= n_lat)
    def _():
        attend(k_ref[S:T, :], v_ref[S:T, :])


def _mla(mq, mk, mv, S):
    B, T, _ = mq.shape
    nt, n_lat = T // TM, S // TM
    return pl.pallas_call(
        functools.partial(_mla_kernel, n_lat, S),
        grid=(B, MLA_HEADS, nt),
        in_specs=[
            pl.BlockSpec((None, TM, MLA_PAD), lambda b, h, i: (b, i, h)),
            pl.BlockSpec((None, T, MLA_PAD), lambda b, h, i: (b, 0, h)),
            pl.BlockSpec((None, T, MLA_V), lambda b, h, i: (b, 0, h)),
        ],
        out_specs=pl.BlockSpec((None, TM, MLA_V), lambda b, h, i: (b, i, h)),
        out_shape=jax.ShapeDtypeStruct((B, T, MLA_W), bf16),
        compiler_params=_params("arbitrary", "arbitrary", "arbitrary"),
        name="mla_attn",
    )(mq, mk, mv)


def _swa_kernel(n_lat, S, sink_ref, q_ref, k_ref, v_ref, o_ref):
    i = pl.program_id(1)
    masks = _half_masks()
    T = k_ref.shape[0]
    kc, vc = k_ref[S:T, :], v_ref[S:T, :]

    def sinks(p):
        return (sink_ref[SWA_HEAD_ORDER[2 * p]], sink_ref[SWA_HEAD_ORDER[2 * p + 1]])

    @pl.when(i < n_lat)
    def _():
        start = pl.multiple_of(jnp.clip(i * TM - SWA_WINDOW, 0, S - SWA_KEYS), SWA_WINDOW)
        qpos = i * TM + lax.broadcasted_iota(jnp.int32, (TM, SWA_KEYS), 0)
        kpos = start + lax.broadcasted_iota(jnp.int32, (TM, SWA_KEYS), 1)
        band = jnp.where(jnp.abs(qpos - kpos) <= SWA_WINDOW, 0.0, NEG)
        kb, vb = k_ref[pl.ds(start, SWA_KEYS), :], v_ref[pl.ds(start, SWA_KEYS), :]
        for p in range(SWA_W // LANES):
            sl = slice(p * LANES, (p + 1) * LANES)
            parts = [(kb, vb, (band, band)), (kc, vc, None)]
            o_ref[:, sl] = _pair_attend(q_ref[:, sl], parts, masks, sinks(p)).astype(bf16)

    @pl.when(i >= n_lat)
    def _():
        for p in range(SWA_W // LANES):
            sl = slice(p * LANES, (p + 1) * LANES)
            o_ref[:, sl] = _pair_attend(q_ref[:, sl], [(kc, vc, None)], masks, sinks(p)).astype(bf16)


def _swa(qka, proj, sink, S):
    B, T, _ = qka.shape
    nt, n_lat = T // TM, S // TM
    return pl.pallas_call(
        functools.partial(_swa_kernel, n_lat, S),
        grid=(B, nt),
        in_specs=[
            pl.BlockSpec(memory_space=pltpu.SMEM),
            pl.BlockSpec((None, TM, SWA_W), lambda b, i: (b, i, A_SWQ // SWA_W)),
            pl.BlockSpec((None, T, SWA_KV_W), lambda b, i: (b, 0, A_SWK // SWA_KV_W)),
            pl.BlockSpec((None, T, SWA_KV_W), lambda b, i: (b, 0, P_SWV // SWA_KV_W)),
        ],
        out_specs=pl.BlockSpec((None, TM, SWA_W), lambda b, i: (b, i, 0)),
        out_shape=jax.ShapeDtypeStruct((B, T, SWA_W), bf16),
        compiler_params=_params("arbitrary", "arbitrary"),
        name="swa_attn",
    )(sink, qka, qka, proj)


def _post_kernel(ona_ref, omla_ref, oswa_ref, gl_ref, x_ref, mod_ref, wna_ref, wmla_ref, wswa_ref, wo_ref,
                 g2_ref, rt_ref, xo_ref, h2_ref, aff_ref):
    D = x_ref.shape[1]
    merged = None
    for j, (o_ref, w_ref) in enumerate(((ona_ref, wna_ref), (omla_ref, wmla_ref), (oswa_ref, wswa_ref))):
        y = _mm(o_ref[...], w_ref[...])
        g = jax.nn.sigmoid(gl_ref[:, j * D:(j + 1) * D].astype(f32))
        merged = g * y if merged is None else merged + g * y
    res = _mm(merged.astype(bf16), wo_ref[...])
    x = x_ref[...] + mod_ref[2:3, :] * res
    xo_ref[...] = x
    ms = jnp.mean(x * x, axis=-1, keepdims=True)
    h2 = x * lax.rsqrt(ms + EPS) * g2_ref[...]
    h2 = h2 * (1.0 + mod_ref[4:5, :]) + mod_ref[3:4, :]
    h2_ref[...] = h2.T.astype(bf16)
    logits = lax.dot_general(rt_ref[...], h2, (((1,), (1,)), ((), ())), preferred_element_type=f32,
                             precision=lax.Precision.HIGHEST)
    e = jnp.exp(logits - jnp.max(logits, axis=0, keepdims=True))
    aff_ref[...] = e / jnp.sum(e, axis=0, keepdims=True)


def _post(ona, omla, oswa, gl, x, mod, lw, n_lat):
    B, T, D = x.shape
    nt = T // TM
    E = lw["routerT"].shape[0]

    def tok(w):
        return pl.BlockSpec((None, TM, w), lambda b, i: (b, i, 0))

    def const(a):
        return pl.BlockSpec(a.shape, lambda b, i: (0,) * a.ndim)

    consts = [lw["wna"], lw["wmla"], lw["wswa"], lw["wo"], lw["g2"], lw["routerT"]]
    return pl.pallas_call(
        _post_kernel,
        grid=(B, nt),
        in_specs=[tok(NA_W), tok(MLA_W), tok(SWA_W), tok(N_BRANCH * D), tok(D), _mod_spec(n_lat, D)]
        + [const(a) for a in consts],
        out_specs=[tok(D), pl.BlockSpec((None, None, D, TM), lambda b, i: (b, i, 0, 0)),
                   pl.BlockSpec((None, E, TM), lambda b, i: (b, 0, i))],
        out_shape=[jax.ShapeDtypeStruct((B, T, D), f32), jax.ShapeDtypeStruct((B, nt, D, TM), bf16),
                   jax.ShapeDtypeStruct((B, E, T), f32)],
        compiler_params=_params("arbitrary", "arbitrary"),
        name="post_attn",
    )(ona, omla, oswa, gl, x, mod, *consts)


def _lane_cumsum(mask, tri):
    E, n = mask.shape
    carry = jnp.zeros((E, 1), f32)
    outs = []
    for k in range(n // LANES):
        w = _mm(mask[:, k * LANES:(k + 1) * LANES].astype(bf16), tri) + carry
        outs.append(w)
        carry = w[:, LANES - 1:LANES]
    return jnp.concatenate(outs, axis=1)


def _select_slots(aff, cap, base, tri):
    bits = lax.bitcast_convert_type(aff, jnp.int32)
    thr = jnp.zeros((aff.shape[0], 1), jnp.int32)
    for bit in range(30, -1, -1):
        cand = thr | (1 << bit)
        cnt = jnp.sum(jnp.where(bits >= cand, 1.0, 0.0), axis=1, keepdims=True)
        thr = jnp.where(cnt >= cap, cand, thr)
    gt = jnp.where(bits > thr, 1.0, 0.0)
    eq = jnp.where(bits == thr, 1.0, 0.0)
    need = cap - jnp.sum(gt, axis=1, keepdims=True)
    sel = jnp.maximum(gt, jnp.where(_lane_cumsum(eq, tri) <= need, eq, 0.0))
    return jnp.where(sel > 0.0, _lane_cumsum(sel, tri) + base, 0.0)


def _topk_kernel(S, cap_s, cap_l, aff_ref, tri_ref, before_ref, cp_ref, bnd_ref):
    T = aff_ref.shape[1]
    tri = tri_ref[...]
    cp_ref[:, 0:S] = _select_slots(aff_ref[:, 0:S], cap_s, 0.0, tri)
    cp_ref[:, S:T] = _select_slots(aff_ref[:, S:T], cap_l, float(cap_s), tri)
    bnd_ref[...] = _mm(jnp.where(cp_ref[...] > 0.0, 1.0, 0.0).astype(bf16), before_ref[...])


def _topk(aff, S, cap_s, cap_l):
    B, E, T = aff.shape
    tri = jnp.asarray(np.triu(np.ones((LANES, LANES), np.float32)), bf16)
    before = jnp.asarray(np.arange(T)[:, None] < MOE_CHUNK * np.arange(LANES)[None, :], bf16)
    return pl.pallas_call(
        functools.partial(_topk_kernel, S, cap_s, cap_l),
        grid=(B,),
        in_specs=[pl.BlockSpec((None, E, T), lambda b: (b, 0, 0)),
                  pl.BlockSpec((LANES, LANES), lambda b: (0, 0)),
                  pl.BlockSpec((T, LANES), lambda b: (0, 0))],
        out_specs=[pl.BlockSpec((None, E, T), lambda b: (b, 0, 0)),
                   pl.BlockSpec((None, E, LANES), lambda b: (b, 0, 0))],
        out_shape=[jax.ShapeDtypeStruct((B, E, T), f32), jax.ShapeDtypeStruct((B, E, LANES), f32)],
        compiler_params=_params("arbitrary"),
        name="expert_select",
    )(aff, tri, before)


def _moe_kernel(blocks, n_e, bnd_ref, h2t_ref, cpt_ref, cp_ref, aff_ref, wgt_ref, wut_ref, wdt_ref,
                out_ref, xg_ref, yt_ref):
    b, e = pl.program_id(0), pl.program_id(1)
    n_chunks = cp_ref.shape[0]
    base = (b * n_e + e) * LANES

    @pl.when(e == 0)
    def _():
        out_ref[...] = jnp.zeros_like(out_ref)

    def overlaps(k, s0, w):
        lo, hi = bnd_ref[base + k], bnd_ref[base + k + 1]
        return jnp.logical_and(jnp.logical_and(hi > s0, lo < s0 + w), hi > lo)

    xg_ref[...] = jnp.zeros_like(xg_ref)
    is_e = lax.broadcasted_iota(jnp.int32, (1, cpt_ref.shape[1]), 1) == e
    for s0, w in blocks:
        slot_lane = (lax.broadcasted_iota(jnp.int32, (1, w), 1) + (s0 + 1)).astype(f32)

        def gather(k, carry, s0=s0, w=w, slot_lane=slot_lane):
            @pl.when(overlaps(k, s0, w))
            def _():
                rows = cpt_ref[pl.ds(pl.multiple_of(k * MOE_CHUNK, MOE_CHUNK), MOE_CHUNK), :]
                rank = jnp.sum(jnp.where(is_e, rows, 0.0), axis=1, keepdims=True)
                onehot = jnp.where(rank == slot_lane, 1.0, 0.0).astype(bf16)
                xg_ref[:, s0:s0 + w] += _mm(h2t_ref[k], onehot).astype(bf16)
            return carry

        lax.fori_loop(0, n_chunks, gather, 0)

    xg = xg_ref[...]
    at = _mm(wgt_ref[...], xg)
    ut = _mm(wut_ref[...], xg)
    act = (at * jax.nn.sigmoid(at) * ut).astype(bf16)
    yt_ref[...] = _mm(wdt_ref[...], act).astype(bf16)

    for s0, w in blocks:
        slot_sub = (lax.broadcasted_iota(jnp.int32, (w, 1), 0) + (s0 + 1)).astype(f32)

        def scatter(k, carry, s0=s0, w=w, slot_sub=slot_sub):
            @pl.when(overlaps(k, s0, w))
            def _():
                onehot = jnp.where(cp_ref[k] == slot_sub, 1.0, 0.0).astype(bf16)
                out_ref[k] += _mm(yt_ref[:, s0:s0 + w], onehot) * aff_ref[k]
            return carry

        lax.fori_loop(0, n_chunks, scatter, 0)


def _moe(h2t, cp, bnd, aff, wgt, wut, wdt, cap_s, cap_l):
    B, nc, D, _ = h2t.shape
    E, T = cp.shape[1], cp.shape[2]
    F = wgt.shape[1]
    blk = 2 * LANES
    assert cap_s % blk == 0 and cap_l <= LANES and nc + 1 <= LANES
    blocks = tuple((s, blk) for s in range(0, cap_s, blk)) + ((cap_s, LANES),)
    n_slots = cap_s + LANES
    cpt = cp.transpose(0, 2, 1)
    cp = cp.reshape(B, E, nc, 1, MOE_CHUNK)
    aff = aff.reshape(B, E, nc, 1, MOE_CHUNK)
    bnd = bnd.astype(jnp.int32).reshape(-1)
    row = pl.BlockSpec((None, None, nc, 1, MOE_CHUNK), lambda b, e, s: (b, e, 0, 0, 0))
    return pl.pallas_call(
        functools.partial(_moe_kernel, blocks, E),
        grid_spec=pltpu.PrefetchScalarGridSpec(
            num_scalar_prefetch=1,
            grid=(B, E),
            in_specs=[
                pl.BlockSpec((None, nc, D, MOE_CHUNK), lambda b, e, s: (b, 0, 0, 0), pipeline_mode=pl.Buffered(1)),
                pl.BlockSpec((None, T, E), lambda b, e, s: (b, 0, 0), pipeline_mode=pl.Buffered(1)),
                row, row,
                pl.BlockSpec((None, F, D), lambda b, e, s: (e, 0, 0)),
                pl.BlockSpec((None, F, D), lambda b, e, s: (e, 0, 0)),
                pl.BlockSpec((None, D, F), lambda b, e, s: (e, 0, 0)),
            ],
            out_specs=pl.BlockSpec((None, nc, D, MOE_CHUNK), lambda b, e, s: (b, 0, 0, 0),
                                   pipeline_mode=pl.Buffered(1)),
            scratch_shapes=[pltpu.VMEM((D, n_slots), bf16), pltpu.VMEM((D, n_slots), bf16)],
        ),
        out_shape=jax.ShapeDtypeStruct((B, nc, D, MOE_CHUNK), f32),
        compiler_params=_params("arbitrary", "arbitrary"),
        name="moe_ffn",
    )(bnd, h2t, cpt, cp, aff, wgt, wut, wdt)


def _final_kernel(x_ref, moe_ref, mod_ref, o_ref):
    o_ref[...] = x_ref[...] + mod_ref[5:6, :] * moe_ref[...].T


def _final(x, moe, mod, S):
    B, T, D = x.shape
    n_lat = S // TM
    return pl.pallas_call(
        _final_kernel,
        grid=(B, n_lat),
        in_specs=[
            pl.BlockSpec((None, TM, D), lambda b, i: (b, i, 0)),
            pl.BlockSpec((None, None, D, TM), lambda b, i: (b, i, 0, 0)),
            pl.BlockSpec((None, None, 6, D), lambda b, i: (b, 0, 0, 0)),
        ],
        out_specs=pl.BlockSpec((None, TM, D), lambda b, i: (b, i, 0)),
        out_shape=jax.ShapeDtypeStruct((B, S, D), f32),
        compiler_params=_params("arbitrary", "arbitrary"),
        name="final_residual",
    )(x, moe, mod)


def _rope_tables(S, L):
    t = jnp.arange(S)
    row, col = t // GRID_W, t % GRID_W
    f = 16
    inv = ROPE_BASE ** (-jnp.arange(f, dtype=f32) / f)
    ar = row.astype(f32)[:, None] * inv
    ac = col.astype(f32)[:, None] * inv
    cos = jnp.concatenate([jnp.cos(ar), jnp.cos(ar), jnp.cos(ac), jnp.cos(ac)], axis=1)
    sin = jnp.concatenate([-jnp.sin(ar), jnp.sin(ar), -jnp.sin(ac), jnp.sin(ac)], axis=1)
    cos = jnp.concatenate([jnp.tile(cos, (1, 2)), jnp.ones((L, LANES), f32)], axis=0)
    sin = jnp.concatenate([jnp.tile(sin, (1, 2)), jnp.zeros((L, LANES), f32)], axis=0)
    return cos, sin


def _pad_heads(w, n_heads, width, padded):
    lead = w.shape[:-1]
    w = w.reshape(lead + (n_heads, width))
    w = jnp.pad(w, [(0, 0)] * len(lead) + [(0, 0), (0, padded - width)])
    return w.reshape(lead + (n_heads * padded,))


def _layer_weights(l, D, w_in, norm2_g, na_q_g, na_k_g, mla_q_norm, mla_w_uq, mla_kv_norm, mla_w_ukv,
                   mla_q_g, mla_k_g, swa_q_g, swa_k_g, w_na_o, w_mla_o, w_swa_o, w_o, router,
                   w_gate, w_up, w_down):
    o = np.cumsum((0, NA_W, NA_W, NA_W, MLA_Q_RANK, MLA_KV_RANK, MLA_ROPE, SWA_W, SWA_KV_W, SWA_KV_W))
    o_naq, o_nak, o_nav, o_cq, o_ckv, o_kr, o_swq, o_swk, o_swv, o_gl = (int(v) for v in o)
    wi = w_in[l]
    swq = wi[:, o_swq:o_swq + SWA_W].reshape(D, SWA_HEADS, SWA_DIM)[:, SWA_HEAD_ORDER, :].reshape(D, SWA_W)
    kr = jnp.pad(wi[:, o_kr:o_kr + MLA_ROPE], ((0, 0), (0, LANES - MLA_ROPE)))
    w_all = jnp.concatenate([
        wi[:, o_naq:o_naq + 3 * NA_W], swq, wi[:, o_cq:o_cq + MLA_Q_RANK], wi[:, o_ckv:o_ckv + MLA_KV_RANK], kr,
        wi[:, o_swk:o_swk + 2 * SWA_KV_W], wi[:, o_gl:]], axis=1).astype(bf16)
    ukv = mla_w_ukv[l].reshape(MLA_KV_RANK, MLA_HEADS, MLA_NOPE + MLA_V)
    swa_o = w_swa_o[l].reshape(SWA_HEADS, SWA_DIM, D)[SWA_HEAD_ORDER, :, :].reshape(SWA_W, D)
    return dict(
        w_all=w_all,
        naq_g=jnp.tile(na_q_g[l], NA_HEADS)[None], nak_g=jnp.tile(na_k_g[l], NA_HEADS)[None],
        swq_g=jnp.tile(swa_q_g[l], SWA_HEADS)[None], swk_g=jnp.tile(swa_k_g[l], SWA_KV_HEADS)[None],
        qn=mla_q_norm[l][None], kvn=mla_kv_norm[l][None],
        wuq=_pad_heads(mla_w_uq[l], MLA_HEADS, MLA_QK, MLA_PAD).astype(bf16),
        wuk=ukv[:, :, :MLA_NOPE].reshape(MLA_KV_RANK, MLA_HEADS * MLA_NOPE).astype(bf16),
        wuv=ukv[:, :, MLA_NOPE:].reshape(MLA_KV_RANK, MLA_W).astype(bf16),
        qg=jnp.tile(jnp.pad(mla_q_g[l], (0, MLA_PAD - MLA_QK)), MLA_HEADS)[None],
        kg=jnp.tile(jnp.pad(mla_k_g[l], (0, MLA_PAD - MLA_QK)), MLA_HEADS)[None],
        wna=w_na_o[l].astype(bf16), wmla=w_mla_o[l].astype(bf16), wswa=swa_o.astype(bf16),
        wo=w_o[l].astype(bf16), g2=norm2_g[l][None], routerT=router[l].T,
        wgt=jnp.swapaxes(w_gate[l], 1, 2).astype(bf16), wut=jnp.swapaxes(w_up[l], 1, 2).astype(bf16), wdt=jnp.swapaxes(w_down[l], 1, 2).astype(bf16),
    )


def kernel(x, c, ctx, c_ctx, norm1_g, norm2_g, w_ada, b_ada, w_in, na_q_g, na_k_g, na_rpb, mla_q_norm, mla_w_uq, mla_kv_norm, mla_w_ukv, mla_q_g, mla_k_g, swa_q_g, swa_k_g, swa_sink, w_na_o, w_mla_o, w_swa_o, w_o, router, w_gate, w_up, w_down):
    B, S, D = x.shape
    L = ctx.shape[1]
    depth = w_in.shape[0]
    E = router.shape[2]
    T = S + L
    n_lat = S // TM
    rows = S // GRID_W
    assert S % TM == 0 and L % TM == 0 and L % LANES == 0 and rows >= NA_BAND_ROWS and S >= SWA_KEYS and B < 16
    cap_s = CAPACITY_FACTOR * S // E
    cap_l = CAPACITY_FACTOR * L // E

    cpad = jnp.zeros((16, D), f32).at[:B].set(c).at[B].set(c_ctx)
    mod = _ada(cpad, w_ada, b_ada).reshape(depth, 16, 6, D)
    mod = jnp.stack([mod[:, :B], jnp.broadcast_to(mod[:, B:B + 1], (depth, B, 6, D))], axis=2)

    cos, sin = _rope_tables(S, L)
    gmat = jnp.asarray(np.kron(np.eye(NA_HEADS), np.full((NA_DIM, NA_DIM), 1.0 / NA_DIM)), bf16)

    xs = jnp.concatenate([x, ctx], axis=1)
    moe = None
    for l in range(depth):
        lw = _layer_weights(l, D, w_in, norm2_g, na_q_g, na_k_g, mla_q_norm, mla_w_uq, mla_kv_norm, mla_w_ukv,
                            mla_q_g, mla_k_g, swa_q_g, swa_k_g, w_na_o, w_mla_o, w_swa_o, w_o, router,
                            w_gate, w_up, w_down)
        modp = mod[l - 1] if l > 0 else None
        xs, proj, gl = _in_proj(xs, moe, modp, mod[l], norm1_g[l][None], lw["w_all"], n_lat)
        qka, mq, mk, mv = _prep(proj, cos, sin, gmat, lw)
        ona = _na(qka, proj, _na_bias(na_rpb[l], rows), S)
        omla = _mla(mq, mk, mv, S)
        oswa = _swa(qka, proj, swa_sink[l], S)
        xs, h2, aff = _post(ona, omla, oswa, gl, xs, mod[l], lw, n_lat)
        cp, bnd = _topk(aff, S, cap_s, cap_l)
        moe = _moe(h2, cp, bnd, aff, lw["wgt"], lw["wut"], lw["wdt"], cap_s, cap_l)
    return _final(xs, moe, mod[depth - 1], S)
```

```python
import functools

import numpy as np
import jax
import jax.numpy as jnp
from jax import lax
from jax.experimental import pallas as pl
from jax.experimental.pallas import tpu as pltpu

GRID_W = 64
NA_HEADS, NA_DIM, NA_WIN_ROWS, NA_WIN_COLS = 6, 64, 8, 16
MLA_HEADS, MLA_Q_RANK, MLA_KV_RANK, MLA_NOPE, MLA_ROPE, MLA_V = 4, 256, 128, 128, 64, 128
MLA_QK = MLA_NOPE + MLA_ROPE
MLA_PAD = 256
SWA_HEADS, SWA_KV_HEADS, SWA_DIM, SWA_WINDOW = 6, 2, 64, 128
CAPACITY_FACTOR = 2
N_BRANCH = 3
ROPE_BASE = 10000.0
EPS = 1e-6

NA_W = NA_HEADS * NA_DIM
SWA_W = SWA_HEADS * SWA_DIM
SWA_KV_W = SWA_KV_HEADS * SWA_DIM
MLA_W = MLA_HEADS * MLA_V

LANES = 128
TM = 256
TT = 512
MLA_TQ = 1024
MLA_TK = 1024
LOG2E = 1.4426950408889634
NA_TILE_ROWS = TM // GRID_W
NA_BAND_ROWS = NA_TILE_ROWS + NA_WIN_ROWS - 1
NA_BAND = NA_BAND_ROWS * GRID_W
SWA_KEYS = TM + 2 * SWA_WINDOW
MOE_CHUNK = 256
NEG = -1e30
VMEM_LIMIT = 56 * 1024 * 1024

P_NAQ, P_NAK, P_NAV, P_SWQ = 0, 384, 768, 1152
P_CQ, P_CKV, P_KR, P_SWK, P_SWV = 1536, 1792, 1920, 2048, 2176
PROJ_W = 2304
N_CHUNK = 768
A_NAQ, A_NAK, A_SWQ, A_SWK = 0, 384, 768, 1152
QKA_W = 1280
SWA_HEAD_ORDER = (0, 3, 1, 4, 2, 5)

bf16 = jnp.bfloat16
f32 = jnp.float32


def _mm(a, b):
    return jnp.dot(a, b, preferred_element_type=f32)


def _nt(a, b):
    return lax.dot_general(a, b, (((1,), (1,)), ((), ())), preferred_element_type=f32)


def _params(*sem):
    return pltpu.CompilerParams(dimension_semantics=sem, vmem_limit_bytes=VMEM_LIMIT)


def _ada_kernel(c_ref, w_ref, b_ref, o_ref):
    a = c_ref[...]
    a = (a * jax.nn.sigmoid(a)).astype(bf16)
    o_ref[...] = _mm(a, w_ref[...].astype(bf16)) + b_ref[...]


def _ada(cpad, w_ada, b_ada):
    L, D, N = w_ada.shape
    tn = 1536
    return pl.pallas_call(
        _ada_kernel,
        grid=(L, N // tn),
        in_specs=[
            pl.BlockSpec((16, D), lambda l, j: (0, 0)),
            pl.BlockSpec((None, D, tn), lambda l, j: (l, 0, j)),
            pl.BlockSpec((None, 1, tn), lambda l, j: (l, 0, j)),
        ],
        out_specs=pl.BlockSpec((None, 16, tn), lambda l, j: (l, 0, j)),
        out_shape=jax.ShapeDtypeStruct((L, 16, N), f32),
        compiler_params=_params("arbitrary", "arbitrary"),
        name="ada",
    )(cpad, w_ada, b_ada.reshape(L, 1, N))


def _in_proj_kernel(has_moe, n_gl_chunks, *refs):
    if has_moe:
        x_ref, moe_ref, modp_ref, mod_ref, g_ref, w_ref, xo_ref, proj_ref, gl_ref = refs
        moe = jnp.concatenate([moe_ref[j].T for j in range(moe_ref.shape[0])], axis=0)
        x = x_ref[...] + modp_ref[5:6, :] * moe
        xo_ref[...] = x
    else:
        x_ref, mod_ref, g_ref, w_ref, proj_ref, gl_ref = refs
        x = x_ref[...]
    ms = jnp.mean(x * x, axis=-1, keepdims=True)
    h = x * lax.rsqrt(ms + EPS) * g_ref[...]
    h = (h * (1.0 + mod_ref[1:2, :]) + mod_ref[0:1, :]).astype(bf16)
    for c in range(PROJ_W // N_CHUNK):
        sl = slice(c * N_CHUNK, (c + 1) * N_CHUNK)
        proj_ref[:, sl] = _mm(h, w_ref[:, sl]).astype(bf16)
    for c in range(n_gl_chunks):
        sl = slice(c * N_CHUNK, (c + 1) * N_CHUNK)
        gl_ref[:, sl] = _mm(h, w_ref[:, PROJ_W + c * N_CHUNK:PROJ_W + (c + 1) * N_CHUNK]).astype(bf16)


def _mod_spec(n_lat, D):
    return pl.BlockSpec((None, None, 6, D), lambda b, i: (b, jnp.where(i >= n_lat, 1, 0), 0, 0))


def _moe_spec(D):
    return pl.BlockSpec((None, TT // TM, D, TM), lambda b, i: (b, i, 0, 0))


def _in_proj(x, moe, modp, mod, g1, w, n_lat):
    B, T, D = x.shape
    nt = pl.cdiv(T, TT)
    has_moe = moe is not None
    n_gl_chunks = N_BRANCH * D // N_CHUNK
    tok = pl.BlockSpec((None, TT, D), lambda b, i: (b, i, 0))
    in_specs = [tok]
    args = [x]
    if has_moe:
        in_specs += [_moe_spec(D), _mod_spec(n_lat, D)]
        args += [moe, modp]
    in_specs += [
        _mod_spec(n_lat, D),
        pl.BlockSpec((1, D), lambda b, i: (0, 0)),
        pl.BlockSpec(w.shape, lambda b, i: (0, 0), pipeline_mode=pl.Buffered(1)),
    ]
    args += [mod, g1, w]
    out_specs = [
        pl.BlockSpec((None, TT, PROJ_W), lambda b, i: (b, i, 0)),
        pl.BlockSpec((None, TT, N_BRANCH * D), lambda b, i: (b, i, 0)),
    ]
    out_shape = [
        jax.ShapeDtypeStruct((B, T, PROJ_W), bf16),
        jax.ShapeDtypeStruct((B, T, N_BRANCH * D), bf16),
    ]
    if has_moe:
        out_specs = [tok] + out_specs
        out_shape = [jax.ShapeDtypeStruct((B, T, D), f32)] + out_shape
    res = pl.pallas_call(
        functools.partial(_in_proj_kernel, has_moe, n_gl_chunks),
        grid=(B, nt),
        in_specs=in_specs,
        out_specs=out_specs,
        out_shape=out_shape,
        compiler_params=_params("arbitrary", "arbitrary"),
        name="in_proj",
    )(*args)
    if has_moe:
        return res
    return [x] + list(res)


def _prep_kernel(proj_ref, cos_ref, sin_ref, gmat_ref, naq_g, nak_g, swq_g, swk_g,
                 qn_ref, wuq_ref, qg_ref, kvn_ref, wuk_ref, wuv_ref, kg_ref,
                 qka_ref, mq_ref, mk_ref, mv_ref):
    cos = cos_ref[...]
    sin = sin_ref[...]
    lane = lax.broadcasted_iota(jnp.int32, cos.shape, 1)
    first = (lane & 16) == 0

    def rope(x):
        partner = jnp.where(first, pltpu.roll(x, LANES - 16, 1), pltpu.roll(x, 16, 1))
        return x * cos + partner * sin

    def headnorm(x, gain):
        w = x.shape[1]
        ms = _mm((x * x).astype(bf16), gmat_ref[:w, :w])
        return x * lax.rsqrt(ms + EPS) * gain

    def rmsnorm(x, gain):
        ms = jnp.mean(x * x, axis=-1, keepdims=True)
        return x * lax.rsqrt(ms + EPS) * gain

    na_scale = NA_DIM ** -0.5
    naq = headnorm(proj_ref[:, P_NAQ:P_NAQ + NA_W].astype(f32), naq_g[...])
    qka_ref[:, A_NAQ:A_NAQ + NA_W] = (naq * na_scale).astype(bf16)
    nak = headnorm(proj_ref[:, P_NAK:P_NAK + NA_W].astype(f32), nak_g[...])
    qka_ref[:, A_NAK:A_NAK + NA_W] = nak.astype(bf16)

    sw_scale = SWA_DIM ** -0.5
    swq = headnorm(proj_ref[:, P_SWQ:P_SWQ + SWA_W].astype(f32), swq_g[...])
    for p in range(SWA_W // LANES):
        sl = slice(p * LANES, (p + 1) * LANES)
        qka_ref[:, A_SWQ + p * LANES:A_SWQ + (p + 1) * LANES] = (rope(swq[:, sl]) * sw_scale).astype(bf16)
    swk = headnorm(proj_ref[:, P_SWK:P_SWK + SWA_KV_W].astype(f32), swk_g[...])
    qka_ref[:, A_SWK:A_SWK + SWA_KV_W] = rope(swk).astype(bf16)

    mla_scale = MLA_QK ** -0.5 * LOG2E
    cq = rmsnorm(proj_ref[:, P_CQ:P_CQ + MLA_Q_RANK].astype(f32), qn_ref[...]).astype(bf16)
    q = _mm(cq, wuq_ref[...])
    ckv = rmsnorm(proj_ref[:, P_CKV:P_CKV + MLA_KV_RANK].astype(f32), kvn_ref[...]).astype(bf16)
    kn = _mm(ckv, wuk_ref[...])
    v = _mm(ckv, wuv_ref[...]).astype(bf16)
    ones_col = jnp.where(lax.broadcasted_iota(jnp.int32, (v.shape[0], LANES), 1) == 0, 1.0, 0.0).astype(bf16)
    for h in range(MLA_HEADS):
        mv_ref[:, h * MLA_PAD:h * MLA_PAD + MLA_V] = v[:, h * MLA_V:(h + 1) * MLA_V]
        mv_ref[:, h * MLA_PAD + MLA_V:(h + 1) * MLA_PAD] = ones_col
    kr = proj_ref[:, P_KR:P_KR + LANES].astype(f32)
    kr_ss = jnp.sum(kr * kr, axis=-1, keepdims=True)
    for h in range(MLA_HEADS):
        o = h * MLA_PAD
        qh = q[:, o:o + MLA_PAD]
        r = lax.rsqrt(jnp.sum(qh * qh, axis=-1, keepdims=True) * (1.0 / MLA_QK) + EPS)
        qh = qh * r * qg_ref[:, o:o + MLA_PAD]
        mq_ref[:, o:o + LANES] = (qh[:, :LANES] * mla_scale).astype(bf16)
        mq_ref[:, o + LANES:o + MLA_PAD] = (rope(qh[:, LANES:]) * mla_scale).astype(bf16)
        kh = kn[:, h * MLA_NOPE:(h + 1) * MLA_NOPE]
        r = lax.rsqrt((jnp.sum(kh * kh, axis=-1, keepdims=True) + kr_ss) * (1.0 / MLA_QK) + EPS)
        mk_ref[:, o:o + LANES] = (kh * r * kg_ref[:, o:o + LANES]).astype(bf16)
        mk_ref[:, o + LANES:o + MLA_PAD] = rope(kr * r * kg_ref[:, o + LANES:o + MLA_PAD]).astype(bf16)


def _prep(proj, cos, sin, gmat, lw):
    B, T, _ = proj.shape
    nt = pl.cdiv(T, TT)

    def const(a):
        return pl.BlockSpec(a.shape, lambda b, i: (0,) * a.ndim)

    consts = [gmat, lw["naq_g"], lw["nak_g"], lw["swq_g"], lw["swk_g"], lw["qn"], lw["wuq"], lw["qg"],
              lw["kvn"], lw["wuk"], lw["wuv"], lw["kg"]]
    widths = (QKA_W, MLA_HEADS * MLA_PAD, MLA_HEADS * MLA_PAD, MLA_HEADS * MLA_PAD)
    return pl.pallas_call(
        _prep_kernel,
        grid=(B, nt),
        in_specs=[
            pl.BlockSpec((None, TT, PROJ_W), lambda b, i: (b, i, 0)),
            pl.BlockSpec((TT, LANES), lambda b, i: (i, 0)),
            pl.BlockSpec((TT, LANES), lambda b, i: (i, 0)),
        ] + [const(a) for a in consts],
        out_specs=[pl.BlockSpec((None, TT, w), lambda b, i: (b, i, 0)) for w in widths],
        out_shape=[jax.ShapeDtypeStruct((B, T, w), bf16) for w in widths],
        compiler_params=_params("arbitrary", "arbitrary"),
        name="prep",
    )(proj, cos, sin, *consts)


def _half_masks():
    lane = lax.broadcasted_iota(jnp.int32, (1, LANES), 1)
    lo = jnp.where(lane < 64, 1.0, 0.0).astype(bf16)
    return lo, (1.0 - lo.astype(f32)).astype(bf16)


def _pair_attend(qp, parts, masks, sinks=None):
    out = None
    for half in range(2):
        hm = masks[half]
        qm = qp * hm
        scores = []
        for k, _, bias in parts:
            s = _nt(qm, k)
            if bias is not None:
                s = s + bias[half]
            scores.append(s)
        m = functools.reduce(jnp.maximum, [jnp.max(s, axis=-1, keepdims=True) for s in scores])
        if sinks is not None:
            m = jnp.maximum(m, sinks[half])
        ps = [jnp.exp(s - m) for s in scores]
        l = functools.reduce(jnp.add, [jnp.sum(p, axis=-1, keepdims=True) for p in ps])
        if sinks is not None:
            l = l + jnp.exp(sinks[half] - m)
        o = functools.reduce(jnp.add, [_mm(p.astype(bf16), v * hm) for p, (_, v, _) in zip(ps, parts)])
        o = o / l
        out = o if out is None else out + o
    return out


def _na_kernel(n_lat, S, rows, q_ref, k_ref, v_ref, bias_ref, o_ref):
    i = pl.program_id(1)
    masks = _half_masks()
    T = k_ref.shape[0]

    @pl.when(i < n_lat)
    def _():
        start = GRID_W * jnp.clip(NA_TILE_ROWS * i - NA_WIN_ROWS // 2, 0, rows - NA_BAND_ROWS)
        start = pl.multiple_of(start, GRID_W)
        for p in range(NA_W // LANES):
            sl = slice(p * LANES, (p + 1) * LANES)
            band = (k_ref[pl.ds(start, NA_BAND), sl], v_ref[pl.ds(start, NA_BAND), sl],
                    (bias_ref[2 * p], bias_ref[2 * p + 1]))
            ctx = (k_ref[S:T, sl], v_ref[S:T, sl], None)
            o_ref[:, sl] = _pair_attend(q_ref[:, sl], [band, ctx], masks).astype(bf16)

    @pl.when(i >= n_lat)
    def _():
        for p in range(NA_W // LANES):
            sl = slice(p * LANES, (p + 1) * LANES)
            ctx = (k_ref[S:T, sl], v_ref[S:T, sl], None)
            o_ref[:, sl] = _pair_attend(q_ref[:, sl], [ctx], masks).astype(bf16)


def _na(qka, proj, bias, S):
    B, T, _ = qka.shape
    nt, n_lat, rows = T // TM, S // TM, S // GRID_W

    def bias_idx(b, i):
        return (jnp.where(i == 0, 0, jnp.where(i >= n_lat - 1, 2, 1)), 0, 0, 0)

    return pl.pallas_call(
        functools.partial(_na_kernel, n_lat, S, rows),
        grid=(B, nt),
        in_specs=[
            pl.BlockSpec((None, TM, NA_W), lambda b, i: (b, i, A_NAQ // NA_W)),
            pl.BlockSpec((None, T, NA_W), lambda b, i: (b, 0, A_NAK // NA_W)),
            pl.BlockSpec((None, T, NA_W), lambda b, i: (b, 0, P_NAV // NA_W)),
            pl.BlockSpec((None, NA_HEADS, TM, NA_BAND), bias_idx),
        ],
        out_specs=pl.BlockSpec((None, TM, NA_W), lambda b, i: (b, i, 0)),
        out_shape=jax.ShapeDtypeStruct((B, T, NA_W), bf16),
        compiler_params=_params("arbitrary", "arbitrary"),
        name="na_attn",
    )(qka, qka, proj, bias)


def _na_bias(rpb, rows):
    n_tiles = rows // NA_TILE_ROWS
    col = np.arange(GRID_W)
    c0 = np.clip(col - NA_WIN_COLS // 2, 0, GRID_W - NA_WIN_COLS)
    col_ok = (col[None, :] >= c0[:, None]) & (col[None, :] < c0[:, None] + NA_WIN_COLS)
    dc = np.clip(col[None, :] - col[:, None] + NA_WIN_COLS - 1, 0, 2 * NA_WIN_COLS - 2)
    tz = jnp.where(col_ok, rpb[:, :, dc], NEG)
    dr = np.zeros((3, NA_TILE_ROWS, NA_BAND_ROWS), np.int32)
    ok = np.zeros((3, NA_TILE_ROWS, NA_BAND_ROWS), bool)
    for ty, rt in enumerate((0, 1, n_tiles - 1)):
        bs = int(np.clip(NA_TILE_ROWS * rt - NA_WIN_ROWS // 2, 0, rows - NA_BAND_ROWS))
        for qr in range(NA_TILE_ROWS):
            r = NA_TILE_ROWS * rt + qr
            s_r = int(np.clip(r - NA_WIN_ROWS // 2, 0, rows - NA_WIN_ROWS))
            for w in range(NA_BAND_ROWS):
                kr = bs + w
                ok[ty, qr, w] = s_r <= kr < s_r + NA_WIN_ROWS
                dr[ty, qr, w] = np.clip(kr - r + NA_WIN_ROWS - 1, 0, 2 * NA_WIN_ROWS - 2)
    blk = tz[:, dr]
    blk = jnp.where(ok[None, :, :, :, None, None], blk, NEG)
    blk = blk.transpose(1, 0, 2, 4, 3, 5)
    return blk.reshape(3, NA_HEADS, TM, NA_BAND)


def _mla_kernel(n_q, S, q_ref, k_ref, v_ref, o_ref):
    i = pl.program_id(2)
    T = k_ref.shape[0]

    def attend(q, bounds):
        m_run = acc = None
        for lo, hi in bounds:
            s = _nt(q, k_ref[lo:hi, :])
            m_new = jnp.max(s, axis=-1, keepdims=True)
            if m_run is not None:
                m_new = jnp.maximum(m_run, m_new)
            pv = _mm(jnp.exp2(s - m_new).astype(bf16), v_ref[lo:hi, :])
            acc = pv if acc is None else acc * jnp.exp2(m_run - m_new) + pv
            m_run = m_new
        return (acc[:, :MLA_V] / acc[:, MLA_V:MLA_V + 1]).astype(bf16)

    @pl.when(i < n_q)
    def _():
        chunks = [(lo, lo + MLA_TK) for lo in range(0, S, MLA_TK)] + [(S, T)]
        o_ref[...] = attend(q_ref[...], chunks)

    @pl.when(i >= n_q)
    def _():
        o_ref[0:T - S, :] = attend(q_ref[0:T - S, :], [(S, T)])


def _mla(mq, mk, mv, S):
    B, T, _ = mq.shape
    n_q = S // MLA_TQ
    assert S % MLA_TQ == 0 and S % MLA_TK == 0 and T - S <= MLA_TQ
    return pl.pallas_call(
        functools.partial(_mla_kernel, n_q, S),
        grid=(B, MLA_HEADS, n_q + 1),
        in_specs=[
            pl.BlockSpec((None, MLA_TQ, MLA_PAD), lambda b, h, i: (b, i, h)),
            pl.BlockSpec((None, T, MLA_PAD), lambda b, h, i: (b, 0, h)),
            pl.BlockSpec((None, T, MLA_PAD), lambda b, h, i: (b, 0, h)),
        ],
        out_specs=pl.BlockSpec((None, MLA_TQ, MLA_V), lambda b, h, i: (b, i, h)),
        out_shape=jax.ShapeDtypeStruct((B, T, MLA_W), bf16),
        compiler_params=_params("arbitrary", "arbitrary", "arbitrary"),
        name="mla_attn",
    )(mq, mk, mv)


def _swa_kernel(n_lat, S, sink_ref, q_ref, k_ref, v_ref, o_ref):
    i = pl.program_id(1)
    masks = _half_masks()
    T = k_ref.shape[0]
    kc, vc = k_ref[S:T, :], v_ref[S:T, :]

    def sinks(p):
        return (sink_ref[SWA_HEAD_ORDER[2 * p]], sink_ref[SWA_HEAD_ORDER[2 * p + 1]])

    @pl.when(i < n_lat)
    def _():
        start = pl.multiple_of(jnp.clip(i * TM - SWA_WINDOW, 0, S - SWA_KEYS), SWA_WINDOW)
        qpos = i * TM + lax.broadcasted_iota(jnp.int32, (TM, SWA_KEYS), 0)
        kpos = start + lax.broadcasted_iota(jnp.int32, (TM, SWA_KEYS), 1)
        band = jnp.where(jnp.abs(qpos - kpos) <= SWA_WINDOW, 0.0, NEG)
        kb, vb = k_ref[pl.ds(start, SWA_KEYS), :], v_ref[pl.ds(start, SWA_KEYS), :]
        for p in range(SWA_W // LANES):
            sl = slice(p * LANES, (p + 1) * LANES)
            parts = [(kb, vb, (band, band)), (kc, vc, None)]
            o_ref[:, sl] = _pair_attend(q_ref[:, sl], parts, masks, sinks(p)).astype(bf16)

    @pl.when(i >= n_lat)
    def _():
        for p in range(SWA_W // LANES):
            sl = slice(p * LANES, (p + 1) * LANES)
            o_ref[:, sl] = _pair_attend(q_ref[:, sl], [(kc, vc, None)], masks, sinks(p)).astype(bf16)


def _swa(qka, proj, sink, S):
    B, T, _ = qka.shape
    nt, n_lat = T // TM, S // TM
    return pl.pallas_call(
        functools.partial(_swa_kernel, n_lat, S),
        grid=(B, nt),
        in_specs=[
            pl.BlockSpec(memory_space=pltpu.SMEM),
            pl.BlockSpec((None, TM, SWA_W), lambda b, i: (b, i, A_SWQ // SWA_W)),
            pl.BlockSpec((None, T, SWA_KV_W), lambda b, i: (b, 0, A_SWK // SWA_KV_W)),
            pl.BlockSpec((None, T, SWA_KV_W), lambda b, i: (b, 0, P_SWV // SWA_KV_W)),
        ],
        out_specs=pl.BlockSpec((None, TM, SWA_W), lambda b, i: (b, i, 0)),
        out_shape=jax.ShapeDtypeStruct((B, T, SWA_W), bf16),
        compiler_params=_params("arbitrary", "arbitrary"),
        name="swa_attn",
    )(sink, qka, qka, proj)


def _post_kernel(ona_ref, omla_ref, oswa_ref, gl_ref, x_ref, mod_ref, wna_ref, wmla_ref, wswa_ref, wo_ref,
                 g2_ref, rt_ref, xo_ref, h2_ref, aff_ref):
    D = x_ref.shape[1]
    merged = None
    for j, (o_ref, w_ref) in enumerate(((ona_ref, wna_ref), (omla_ref, wmla_ref), (oswa_ref, wswa_ref))):
        y = _mm(o_ref[...], w_ref[...])
        g = jax.nn.sigmoid(gl_ref[:, j * D:(j + 1) * D].astype(f32))
        merged = g * y if merged is None else merged + g * y
    res = _mm(merged.astype(bf16), wo_ref[...])
    x = x_ref[...] + mod_ref[2:3, :] * res
    xo_ref[...] = x
    ms = jnp.mean(x * x, axis=-1, keepdims=True)
    h2 = x * lax.rsqrt(ms + EPS) * g2_ref[...]
    h2 = h2 * (1.0 + mod_ref[4:5, :]) + mod_ref[3:4, :]
    for j in range(h2_ref.shape[0]):
        h2_ref[j] = h2[j * TM:(j + 1) * TM, :].T.astype(bf16)
    logits = jnp.dot(h2, rt_ref[...], preferred_element_type=f32, precision=lax.Precision.HIGHEST)
    e = jnp.exp(logits - jnp.max(logits, axis=1, keepdims=True))
    aff_ref[...] = e / jnp.sum(e, axis=1, keepdims=True)


def _post(ona, omla, oswa, gl, x, mod, lw, n_lat):
    B, T, D = x.shape
    nt = pl.cdiv(T, TT)
    E = lw["router"].shape[1]

    def tok(w):
        return pl.BlockSpec((None, TT, w), lambda b, i: (b, i, 0))

    def const(a):
        return pl.BlockSpec(a.shape, lambda b, i: (0,) * a.ndim)

    consts = [lw["wna"], lw["wmla"], lw["wswa"], lw["wo"], lw["g2"], lw["router"]]
    return pl.pallas_call(
        _post_kernel,
        grid=(B, nt),
        in_specs=[tok(NA_W), tok(MLA_W), tok(SWA_W), tok(N_BRANCH * D), tok(D), _mod_spec(n_lat, D)]
        + [const(a) for a in consts],
        out_specs=[tok(D), _moe_spec(D), tok(E)],
        out_shape=[jax.ShapeDtypeStruct((B, T, D), f32), jax.ShapeDtypeStruct((B, T // TM, D, TM), bf16),
                   jax.ShapeDtypeStruct((B, T, E), f32)],
        compiler_params=_params("arbitrary", "arbitrary"),
        name="post_attn",
    )(ona, omla, oswa, gl, x, mod, *consts)


def _lane_cumsum(mask, tri):
    E, n = mask.shape
    carry = jnp.zeros((E, 1), f32)
    outs = []
    for k in range(n // LANES):
        w = _mm(mask[:, k * LANES:(k + 1) * LANES].astype(bf16), tri) + carry
        outs.append(w)
        carry = w[:, LANES - 1:LANES]
    return jnp.concatenate(outs, axis=1)


def _select_slots(aff, cap, base, tri):
    bits = lax.bitcast_convert_type(aff, jnp.int32)
    thr = jnp.zeros((aff.shape[0], 1), jnp.int32)
    for bit in range(30, -1, -1):
        cand = thr | (1 << bit)
        cnt = jnp.sum(jnp.where(bits >= cand, 1.0, 0.0), axis=1, keepdims=True)
        thr = jnp.where(cnt >= cap, cand, thr)
    gt = jnp.where(bits > thr, 1.0, 0.0)
    eq = jnp.where(bits == thr, 1.0, 0.0)
    need = cap - jnp.sum(gt, axis=1, keepdims=True)
    sel = jnp.maximum(gt, jnp.where(_lane_cumsum(eq, tri) <= need, eq, 0.0))
    return jnp.where(sel > 0.0, _lane_cumsum(sel, tri) + base, 0.0)


def _topk_kernel(S, cap_s, cap_l, aff_ref, tri_ref, before_ref, cp_ref, bnd_ref):
    T = aff_ref.shape[1]
    tri = tri_ref[...]
    cp_ref[:, 0:S] = _select_slots(aff_ref[:, 0:S], cap_s, 0.0, tri)
    cp_ref[:, S:T] = _select_slots(aff_ref[:, S:T], cap_l, float(cap_s), tri)
    bnd_ref[...] = _mm(jnp.where(cp_ref[...] > 0.0, 1.0, 0.0).astype(bf16), before_ref[...])


def _topk(aff, S, cap_s, cap_l):
    B, E, T = aff.shape
    tri = jnp.asarray(np.triu(np.ones((LANES, LANES), np.float32)), bf16)
    before = jnp.asarray(np.arange(T)[:, None] < MOE_CHUNK * np.arange(LANES)[None, :], bf16)
    return pl.pallas_call(
        functools.partial(_topk_kernel, S, cap_s, cap_l),
        grid=(B,),
        in_specs=[pl.BlockSpec((None, E, T), lambda b: (b, 0, 0)),
                  pl.BlockSpec((LANES, LANES), lambda b: (0, 0)),
                  pl.BlockSpec((T, LANES), lambda b: (0, 0))],
        out_specs=[pl.BlockSpec((None, E, T), lambda b: (b, 0, 0)),
                   pl.BlockSpec((None, E, LANES), lambda b: (b, 0, 0))],
        out_shape=[jax.ShapeDtypeStruct((B, E, T), f32), jax.ShapeDtypeStruct((B, E, LANES), f32)],
        compiler_params=_params("arbitrary"),
        name="expert_select",
    )(aff, tri, before)


def _moe_kernel(blocks, n_e, bnd_ref, h2t_ref, cpt_ref, cp_ref, aff_ref, wgt_ref, wut_ref, wdt_ref,
                out_ref, xg_ref, yt_ref):
    b, e = pl.program_id(0), pl.program_id(1)
    n_chunks = cp_ref.shape[0]
    base = (b * n_e + e) * LANES

    @pl.when(e == 0)
    def _():
        out_ref[...] = jnp.zeros_like(out_ref)

    def overlaps(k, s0, w):
        lo, hi = bnd_ref[base + k], bnd_ref[base + k + 1]
        return jnp.logical_and(jnp.logical_and(hi > s0, lo < s0 + w), hi > lo)

    xg_ref[...] = jnp.zeros_like(xg_ref)
    is_e = lax.broadcasted_iota(jnp.int32, (1, cpt_ref.shape[1]), 1) == e
    for s0, w in blocks:
        slot_lane = (lax.broadcasted_iota(jnp.int32, (1, w), 1) + (s0 + 1)).astype(f32)

        def gather(k, carry, s0=s0, w=w, slot_lane=slot_lane):
            @pl.when(overlaps(k, s0, w))
            def _():
                rows = cpt_ref[pl.ds(pl.multiple_of(k * MOE_CHUNK, MOE_CHUNK), MOE_CHUNK), :]
                rank = jnp.sum(jnp.where(is_e, rows, 0.0), axis=1, keepdims=True)
                onehot = jnp.where(rank == slot_lane, 1.0, 0.0).astype(bf16)
                xg_ref[:, s0:s0 + w] += _mm(h2t_ref[k], onehot).astype(bf16)
            return carry

        lax.fori_loop(0, n_chunks, gather, 0)

    xg = xg_ref[...]
    at = _mm(wgt_ref[...], xg)
    ut = _mm(wut_ref[...], xg)
    act = (at * jax.nn.sigmoid(at) * ut).astype(bf16)
    yt_ref[...] = _mm(wdt_ref[...], act).astype(bf16)

    for s0, w in blocks:
        slot_sub = (lax.broadcasted_iota(jnp.int32, (w, 1), 0) + (s0 + 1)).astype(f32)

        def scatter(k, carry, s0=s0, w=w, slot_sub=slot_sub):
            @pl.when(overlaps(k, s0, w))
            def _():
                onehot = jnp.where(cp_ref[k] == slot_sub, 1.0, 0.0).astype(bf16)
                out_ref[k] += _mm(yt_ref[:, s0:s0 + w], onehot) * aff_ref[k]
            return carry

        lax.fori_loop(0, n_chunks, scatter, 0)


def _moe(h2t, cp, bnd, aff, wgt, wut, wdt, cap_s, cap_l):
    B, nc, D, _ = h2t.shape
    E, T = cp.shape[1], cp.shape[2]
    F = wgt.shape[1]
    blk = 2 * LANES
    assert cap_s % blk == 0 and cap_l <= LANES and nc + 1 <= LANES
    blocks = tuple((s, blk) for s in range(0, cap_s, blk)) + ((cap_s, LANES),)
    n_slots = cap_s + LANES
    cpt = cp.transpose(0, 2, 1)
    cp = cp.reshape(B, E, nc, 1, MOE_CHUNK)
    aff = aff.reshape(B, E, nc, 1, MOE_CHUNK)
    bnd = bnd.astype(jnp.int32).reshape(-1)
    row = pl.BlockSpec((None, None, nc, 1, MOE_CHUNK), lambda b, e, s: (b, e, 0, 0, 0))
    return pl.pallas_call(
        functools.partial(_moe_kernel, blocks, E),
        grid_spec=pltpu.PrefetchScalarGridSpec(
            num_scalar_prefetch=1,
            grid=(B, E),
            in_specs=[
                pl.BlockSpec((None, nc, D, MOE_CHUNK), lambda b, e, s: (b, 0, 0, 0), pipeline_mode=pl.Buffered(1)),
                pl.BlockSpec((None, T, E), lambda b, e, s: (b, 0, 0), pipeline_mode=pl.Buffered(1)),
                row, row,
                pl.BlockSpec((None, F, D), lambda b, e, s: (e, 0, 0)),
                pl.BlockSpec((None, F, D), lambda b, e, s: (e, 0, 0)),
                pl.BlockSpec((None, D, F), lambda b, e, s: (e, 0, 0)),
            ],
            out_specs=pl.BlockSpec((None, nc, D, MOE_CHUNK), lambda b, e, s: (b, 0, 0, 0),
                                   pipeline_mode=pl.Buffered(1)),
            scratch_shapes=[pltpu.VMEM((D, n_slots), bf16), pltpu.VMEM((D, n_slots), bf16)],
        ),
        out_shape=jax.ShapeDtypeStruct((B, nc, D, MOE_CHUNK), f32),
        compiler_params=_params("arbitrary", "arbitrary"),
        name="moe_ffn",
    )(bnd, h2t, cpt, cp, aff, wgt, wut, wdt)


def _final_kernel(x_ref, moe_ref, mod_ref, o_ref):
    moe = jnp.concatenate([moe_ref[j].T for j in range(moe_ref.shape[0])], axis=0)
    o_ref[...] = x_ref[...] + mod_ref[5:6, :] * moe


def _final(x, moe, mod, S):
    B, T, D = x.shape
    return pl.pallas_call(
        _final_kernel,
        grid=(B, S // TT),
        in_specs=[
            pl.BlockSpec((None, TT, D), lambda b, i: (b, i, 0)),
            _moe_spec(D),
            pl.BlockSpec((None, None, 6, D), lambda b, i: (b, 0, 0, 0)),
        ],
        out_specs=pl.BlockSpec((None, TT, D), lambda b, i: (b, i, 0)),
        out_shape=jax.ShapeDtypeStruct((B, S, D), f32),
        compiler_params=_params("arbitrary", "arbitrary"),
        name="final_residual",
    )(x, moe, mod)


def _rope_tables(S, L):
    t = jnp.arange(S)
    row, col = t // GRID_W, t % GRID_W
    f = 16
    inv = ROPE_BASE ** (-jnp.arange(f, dtype=f32) / f)
    ar = row.astype(f32)[:, None] * inv
    ac = col.astype(f32)[:, None] * inv
    cos = jnp.concatenate([jnp.cos(ar), jnp.cos(ar), jnp.cos(ac), jnp.cos(ac)], axis=1)
    sin = jnp.concatenate([-jnp.sin(ar), jnp.sin(ar), -jnp.sin(ac), jnp.sin(ac)], axis=1)
    cos = jnp.concatenate([jnp.tile(cos, (1, 2)), jnp.ones((L, LANES), f32)], axis=0)
    sin = jnp.concatenate([jnp.tile(sin, (1, 2)), jnp.zeros((L, LANES), f32)], axis=0)
    return cos, sin


def _pad_heads(w, n_heads, width, padded):
    lead = w.shape[:-1]
    w = w.reshape(lead + (n_heads, width))
    w = jnp.pad(w, [(0, 0)] * len(lead) + [(0, 0), (0, padded - width)])
    return w.reshape(lead + (n_heads * padded,))


def _layer_weights(l, D, w_in, norm2_g, na_q_g, na_k_g, mla_q_norm, mla_w_uq, mla_kv_norm, mla_w_ukv,
                   mla_q_g, mla_k_g, swa_q_g, swa_k_g, w_na_o, w_mla_o, w_swa_o, w_o, router,
                   w_gate, w_up, w_down):
    o = np.cumsum((0, NA_W, NA_W, NA_W, MLA_Q_RANK, MLA_KV_RANK, MLA_ROPE, SWA_W, SWA_KV_W, SWA_KV_W))
    o_naq, o_nak, o_nav, o_cq, o_ckv, o_kr, o_swq, o_swk, o_swv, o_gl = (int(v) for v in o)
    wi = w_in[l]
    swq = wi[:, o_swq:o_swq + SWA_W].reshape(D, SWA_HEADS, SWA_DIM)[:, SWA_HEAD_ORDER, :].reshape(D, SWA_W)
    kr = jnp.pad(wi[:, o_kr:o_kr + MLA_ROPE], ((0, 0), (0, LANES - MLA_ROPE)))
    w_all = jnp.concatenate([
        wi[:, o_naq:o_naq + 3 * NA_W], swq, wi[:, o_cq:o_cq + MLA_Q_RANK], wi[:, o_ckv:o_ckv + MLA_KV_RANK], kr,
        wi[:, o_swk:o_swk + 2 * SWA_KV_W], wi[:, o_gl:]], axis=1).astype(bf16)
    ukv = mla_w_ukv[l].reshape(MLA_KV_RANK, MLA_HEADS, MLA_NOPE + MLA_V)
    swa_o = w_swa_o[l].reshape(SWA_HEADS, SWA_DIM, D)[SWA_HEAD_ORDER, :, :].reshape(SWA_W, D)
    return dict(
        w_all=w_all,
        naq_g=jnp.tile(na_q_g[l], NA_HEADS)[None], nak_g=jnp.tile(na_k_g[l], NA_HEADS)[None],
        swq_g=jnp.tile(swa_q_g[l], SWA_HEADS)[None], swk_g=jnp.tile(swa_k_g[l], SWA_KV_HEADS)[None],
        qn=mla_q_norm[l][None], kvn=mla_kv_norm[l][None],
        wuq=_pad_heads(mla_w_uq[l], MLA_HEADS, MLA_QK, MLA_PAD).astype(bf16),
        wuk=ukv[:, :, :MLA_NOPE].reshape(MLA_KV_RANK, MLA_HEADS * MLA_NOPE).astype(bf16),
        wuv=ukv[:, :, MLA_NOPE:].reshape(MLA_KV_RANK, MLA_W).astype(bf16),
        qg=jnp.tile(jnp.pad(mla_q_g[l], (0, MLA_PAD - MLA_QK)), MLA_HEADS)[None],
        kg=jnp.tile(jnp.pad(mla_k_g[l], (0, MLA_PAD - MLA_QK)), MLA_HEADS)[None],
        wna=w_na_o[l].astype(bf16), wmla=w_mla_o[l].astype(bf16), wswa=swa_o.astype(bf16),
        wo=w_o[l].astype(bf16), g2=norm2_g[l][None], router=router[l],
        wgt=jnp.swapaxes(w_gate[l], 1, 2).astype(bf16), wut=jnp.swapaxes(w_up[l], 1, 2).astype(bf16), wdt=jnp.swapaxes(w_down[l], 1, 2).astype(bf16),
    )


def kernel(x, c, ctx, c_ctx, norm1_g, norm2_g, w_ada, b_ada, w_in, na_q_g, na_k_g, na_rpb, mla_q_norm, mla_w_uq, mla_kv_norm, mla_w_ukv, mla_q_g, mla_k_g, swa_q_g, swa_k_g, swa_sink, w_na_o, w_mla_o, w_swa_o, w_o, router, w_gate, w_up, w_down):
    B, S, D = x.shape
    L = ctx.shape[1]
    depth = w_in.shape[0]
    E = router.shape[2]
    T = S + L
    n_lat = S // TT
    rows = S // GRID_W
    assert S % TT == 0 and L % TM == 0 and L <= TT and rows >= NA_BAND_ROWS and S >= SWA_KEYS and B < 16
    cap_s = CAPACITY_FACTOR * S // E
    cap_l = CAPACITY_FACTOR * L // E

    cpad = jnp.zeros((16, D), f32).at[:B].set(c).at[B].set(c_ctx)
    mod = _ada(cpad, w_ada, b_ada).reshape(depth, 16, 6, D)
    mod = jnp.stack([mod[:, :B], jnp.broadcast_to(mod[:, B:B + 1], (depth, B, 6, D))], axis=2)

    cos, sin = _rope_tables(S, L)
    gmat = jnp.asarray(np.kron(np.eye(NA_HEADS), np.full((NA_DIM, NA_DIM), 1.0 / NA_DIM)), bf16)

    xs = jnp.concatenate([x, ctx], axis=1)
    moe = None
    for l in range(depth):
        lw = _layer_weights(l, D, w_in, norm2_g, na_q_g, na_k_g, mla_q_norm, mla_w_uq, mla_kv_norm, mla_w_ukv,
                            mla_q_g, mla_k_g, swa_q_g, swa_k_g, w_na_o, w_mla_o, w_swa_o, w_o, router,
                            w_gate, w_up, w_down)
        modp = mod[l - 1] if l > 0 else None
        xs, proj, gl = _in_proj(xs, moe, modp, mod[l], norm1_g[l][None], lw["w_all"], n_lat)
        qka, mq, mk, mv = _prep(proj, cos, sin, gmat, lw)
        ona = _na(qka, proj, _na_bias(na_rpb[l], rows), S)
        omla = _mla(mq, mk, mv, S)
        oswa = _swa(qka, proj, swa_sink[l], S)
        xs, h2, aff = _post(ona, omla, oswa, gl, xs, mod[l], lw, n_lat)
        aff = aff.transpose(0, 2, 1)
        cp, bnd = _topk(aff, S, cap_s, cap_l)
        moe = _moe(h2, cp, bnd, aff, lw["wgt"], lw["wut"], lw["wdt"], cap_s, cap_l)
    return _final(xs, moe, mod[depth - 1], S)
```

```python
import functools

import numpy as np
import jax
import jax.numpy as jnp
from jax import lax
from jax.experimental import pallas as pl
from jax.experimental.pallas import tpu as pltpu

GRID_W = 64
NA_HEADS, NA_DIM, NA_WIN_ROWS, NA_WIN_COLS = 6, 64, 8, 16
MLA_HEADS, MLA_Q_RANK, MLA_KV_RANK, MLA_NOPE, MLA_ROPE, MLA_V = 4, 256, 128, 128, 64, 128
MLA_QK = MLA_NOPE + MLA_ROPE
MLA_PAD = 256
SWA_HEADS, SWA_KV_HEADS, SWA_DIM, SWA_WINDOW = 6, 2, 64, 128
CAPACITY_FACTOR = 2
N_BRANCH = 3
ROPE_BASE = 10000.0
EPS = 1e-6

NA_W = NA_HEADS * NA_DIM
SWA_W = SWA_HEADS * SWA_DIM
SWA_KV_W = SWA_KV_HEADS * SWA_DIM
MLA_W = MLA_HEADS * MLA_V

LANES = 128
TM = 256
TT = 512
MLA_TQ = 1024
MLA_TK = 1024
LOG2E = 1.4426950408889634
NA_TILE_ROWS = TM // GRID_W
NA_BAND_ROWS = NA_TILE_ROWS + NA_WIN_ROWS - 1
NA_BAND = NA_BAND_ROWS * GRID_W
SWA_KEYS = TM + 2 * SWA_WINDOW
MOE_CHUNK = 256
MOE_WIN = 64
NEG = -1e30
VMEM_LIMIT = 56 * 1024 * 1024

P_NAQ, P_NAK, P_NAV, P_SWQ = 0, 384, 768, 1152
P_CQ, P_CKV, P_KR, P_SWK, P_SWV = 1536, 1792, 1920, 2048, 2176
PROJ_W = 2304
N_CHUNK = 768
A_NAQ, A_NAK, A_SWQ, A_SWK = 0, 384, 768, 1152
QKA_W = 1280
SWA_HEAD_ORDER = (0, 3, 1, 4, 2, 5)

bf16 = jnp.bfloat16
f32 = jnp.float32


def _mm(a, b):
    return jnp.dot(a, b, preferred_element_type=f32)


def _nt(a, b):
    return lax.dot_general(a, b, (((1,), (1,)), ((), ())), preferred_element_type=f32)


def _params(*sem):
    return pltpu.CompilerParams(dimension_semantics=sem, vmem_limit_bytes=VMEM_LIMIT)


def _ada_kernel(c_ref, w_ref, b_ref, o_ref):
    a = c_ref[...]
    a = (a * jax.nn.sigmoid(a)).astype(bf16)
    o_ref[...] = _mm(a, w_ref[...].astype(bf16)) + b_ref[...]


def _ada(cpad, w_ada, b_ada):
    L, D, N = w_ada.shape
    tn = 1536
    return pl.pallas_call(
        _ada_kernel,
        grid=(L, N // tn),
        in_specs=[
            pl.BlockSpec((16, D), lambda l, j: (0, 0)),
            pl.BlockSpec((None, D, tn), lambda l, j: (l, 0, j)),
            pl.BlockSpec((None, 1, tn), lambda l, j: (l, 0, j)),
        ],
        out_specs=pl.BlockSpec((None, 16, tn), lambda l, j: (l, 0, j)),
        out_shape=jax.ShapeDtypeStruct((L, 16, N), f32),
        compiler_params=_params("arbitrary", "arbitrary"),
        name="ada",
    )(cpad, w_ada, b_ada.reshape(L, 1, N))


def _in_proj_kernel(has_moe, n_gl_chunks, *refs):
    if has_moe:
        x_ref, moe_ref, modp_ref, mod_ref, g_ref, w_ref, xo_ref, proj_ref, gl_ref = refs
        x = x_ref[...] + modp_ref[5:6, :] * moe_ref[...]
        xo_ref[...] = x
    else:
        x_ref, mod_ref, g_ref, w_ref, proj_ref, gl_ref = refs
        x = x_ref[...]
    ms = jnp.mean(x * x, axis=-1, keepdims=True)
    h = x * lax.rsqrt(ms + EPS) * g_ref[...]
    h = (h * (1.0 + mod_ref[1:2, :]) + mod_ref[0:1, :]).astype(bf16)
    for c in range(PROJ_W // N_CHUNK):
        sl = slice(c * N_CHUNK, (c + 1) * N_CHUNK)
        proj_ref[:, sl] = _mm(h, w_ref[:, sl]).astype(bf16)
    for c in range(n_gl_chunks):
        sl = slice(c * N_CHUNK, (c + 1) * N_CHUNK)
        gl_ref[:, sl] = _mm(h, w_ref[:, PROJ_W + c * N_CHUNK:PROJ_W + (c + 1) * N_CHUNK]).astype(bf16)


def _mod_spec(n_lat, D):
    return pl.BlockSpec((None, None, 6, D), lambda b, i: (b, jnp.where(i >= n_lat, 1, 0), 0, 0))


def _in_proj(x, moe, modp, mod, g1, w, n_lat):
    B, T, D = x.shape
    nt = pl.cdiv(T, TT)
    has_moe = moe is not None
    n_gl_chunks = N_BRANCH * D // N_CHUNK
    tok = pl.BlockSpec((None, TT, D), lambda b, i: (b, i, 0))
    in_specs = [tok]
    args = [x]
    if has_moe:
        in_specs += [tok, _mod_spec(n_lat, D)]
        args += [moe, modp]
    in_specs += [
        _mod_spec(n_lat, D),
        pl.BlockSpec((1, D), lambda b, i: (0, 0)),
        pl.BlockSpec(w.shape, lambda b, i: (0, 0), pipeline_mode=pl.Buffered(1)),
    ]
    args += [mod, g1, w]
    out_specs = [
        pl.BlockSpec((None, TT, PROJ_W), lambda b, i: (b, i, 0)),
        pl.BlockSpec((None, TT, N_BRANCH * D), lambda b, i: (b, i, 0)),
    ]
    out_shape = [
        jax.ShapeDtypeStruct((B, T, PROJ_W), bf16),
        jax.ShapeDtypeStruct((B, T, N_BRANCH * D), bf16),
    ]
    if has_moe:
        out_specs = [tok] + out_specs
        out_shape = [jax.ShapeDtypeStruct((B, T, D), f32)] + out_shape
    res = pl.pallas_call(
        functools.partial(_in_proj_kernel, has_moe, n_gl_chunks),
        grid=(B, nt),
        in_specs=in_specs,
        out_specs=out_specs,
        out_shape=out_shape,
        compiler_params=_params("arbitrary", "arbitrary"),
        name="in_proj",
    )(*args)
    if has_moe:
        return res
    return [x] + list(res)


def _prep_kernel(proj_ref, cos_ref, sin_ref, gmat_ref, naq_g, nak_g, swq_g, swk_g,
                 qn_ref, wuq_ref, qg_ref, kvn_ref, wuk_ref, wuv_ref, kg_ref,
                 qka_ref, mq_ref, mk_ref, mv_ref):
    cos = cos_ref[...]
    sin = sin_ref[...]
    lane = lax.broadcasted_iota(jnp.int32, cos.shape, 1)
    first = (lane & 16) == 0

    def rope(x):
        partner = jnp.where(first, pltpu.roll(x, LANES - 16, 1), pltpu.roll(x, 16, 1))
        return x * cos + partner * sin

    def headnorm(x, gain):
        w = x.shape[1]
        ms = _mm((x * x).astype(bf16), gmat_ref[:w, :w])
        return x * lax.rsqrt(ms + EPS) * gain

    def rmsnorm(x, gain):
        ms = jnp.mean(x * x, axis=-1, keepdims=True)
        return x * lax.rsqrt(ms + EPS) * gain

    na_scale = NA_DIM ** -0.5
    naq = headnorm(proj_ref[:, P_NAQ:P_NAQ + NA_W].astype(f32), naq_g[...])
    qka_ref[:, A_NAQ:A_NAQ + NA_W] = (naq * na_scale).astype(bf16)
    nak = headnorm(proj_ref[:, P_NAK:P_NAK + NA_W].astype(f32), nak_g[...])
    qka_ref[:, A_NAK:A_NAK + NA_W] = nak.astype(bf16)

    sw_scale = SWA_DIM ** -0.5
    swq = headnorm(proj_ref[:, P_SWQ:P_SWQ + SWA_W].astype(f32), swq_g[...])
    for p in range(SWA_W // LANES):
        sl = slice(p * LANES, (p + 1) * LANES)
        qka_ref[:, A_SWQ + p * LANES:A_SWQ + (p + 1) * LANES] = (rope(swq[:, sl]) * sw_scale).astype(bf16)
    swk = headnorm(proj_ref[:, P_SWK:P_SWK + SWA_KV_W].astype(f32), swk_g[...])
    qka_ref[:, A_SWK:A_SWK + SWA_KV_W] = rope(swk).astype(bf16)

    mla_scale = MLA_QK ** -0.5 * LOG2E
    cq = rmsnorm(proj_ref[:, P_CQ:P_CQ + MLA_Q_RANK].astype(f32), qn_ref[...]).astype(bf16)
    q = _mm(cq, wuq_ref[...])
    ckv = rmsnorm(proj_ref[:, P_CKV:P_CKV + MLA_KV_RANK].astype(f32), kvn_ref[...]).astype(bf16)
    kn = _mm(ckv, wuk_ref[...])
    v = _mm(ckv, wuv_ref[...]).astype(bf16)
    ones_col = jnp.where(lax.broadcasted_iota(jnp.int32, (v.shape[0], LANES), 1) == 0, 1.0, 0.0).astype(bf16)
    for h in range(MLA_HEADS):
        mv_ref[:, h * MLA_PAD:h * MLA_PAD + MLA_V] = v[:, h * MLA_V:(h + 1) * MLA_V]
        mv_ref[:, h * MLA_PAD + MLA_V:(h + 1) * MLA_PAD] = ones_col
    kr = proj_ref[:, P_KR:P_KR + LANES].astype(f32)
    kr_ss = jnp.sum(kr * kr, axis=-1, keepdims=True)
    for h in range(MLA_HEADS):
        o = h * MLA_PAD
        qh = q[:, o:o + MLA_PAD]
        r = lax.rsqrt(jnp.sum(qh * qh, axis=-1, keepdims=True) * (1.0 / MLA_QK) + EPS)
        qh = qh * r * qg_ref[:, o:o + MLA_PAD]
        mq_ref[:, o:o + LANES] = (qh[:, :LANES] * mla_scale).astype(bf16)
        mq_ref[:, o + LANES:o + MLA_PAD] = (rope(qh[:, LANES:]) * mla_scale).astype(bf16)
        kh = kn[:, h * MLA_NOPE:(h + 1) * MLA_NOPE]
        r = lax.rsqrt((jnp.sum(kh * kh, axis=-1, keepdims=True) + kr_ss) * (1.0 / MLA_QK) + EPS)
        mk_ref[:, o:o + LANES] = (kh * r * kg_ref[:, o:o + LANES]).astype(bf16)
        mk_ref[:, o + LANES:o + MLA_PAD] = rope(kr * r * kg_ref[:, o + LANES:o + MLA_PAD]).astype(bf16)


def _prep(proj, cos, sin, gmat, lw):
    B, T, _ = proj.shape
    nt = pl.cdiv(T, TT)

    def const(a):
        return pl.BlockSpec(a.shape, lambda b, i: (0,) * a.ndim)

    consts = [gmat, lw["naq_g"], lw["nak_g"], lw["swq_g"], lw["swk_g"], lw["qn"], lw["wuq"], lw["qg"],
              lw["kvn"], lw["wuk"], lw["wuv"], lw["kg"]]
    widths = (QKA_W, MLA_HEADS * MLA_PAD, MLA_HEADS * MLA_PAD, MLA_HEADS * MLA_PAD)
    return pl.pallas_call(
        _prep_kernel,
        grid=(B, nt),
        in_specs=[
            pl.BlockSpec((None, TT, PROJ_W), lambda b, i: (b, i, 0)),
            pl.BlockSpec((TT, LANES), lambda b, i: (i, 0)),
            pl.BlockSpec((TT, LANES), lambda b, i: (i, 0)),
        ] + [const(a) for a in consts],
        out_specs=[pl.BlockSpec((None, TT, w), lambda b, i: (b, i, 0)) for w in widths],
        out_shape=[jax.ShapeDtypeStruct((B, T, w), bf16) for w in widths],
        compiler_params=_params("arbitrary", "arbitrary"),
        name="prep",
    )(proj, cos, sin, *consts)


def _half_masks():
    lane = lax.broadcasted_iota(jnp.int32, (1, LANES), 1)
    lo = jnp.where(lane < 64, 1.0, 0.0).astype(bf16)
    return lo, (1.0 - lo.astype(f32)).astype(bf16)


def _pair_attend(qp, parts, masks, sinks=None):
    out = None
    for half in range(2):
        hm = masks[half]
        qm = qp * hm
        scores = []
        for k, _, bias in parts:
            s = _nt(qm, k)
            if bias is not None:
                s = s + bias[half]
            scores.append(s)
        m = functools.reduce(jnp.maximum, [jnp.max(s, axis=-1, keepdims=True) for s in scores])
        if sinks is not None:
            m = jnp.maximum(m, sinks[half])
        ps = [jnp.exp(s - m) for s in scores]
        l = functools.reduce(jnp.add, [jnp.sum(p, axis=-1, keepdims=True) for p in ps])
        if sinks is not None:
            l = l + jnp.exp(sinks[half] - m)
        o = functools.reduce(jnp.add, [_mm(p.astype(bf16), v * hm) for p, (_, v, _) in zip(ps, parts)])
        o = o / l
        out = o if out is None else out + o
    return out


def _na_kernel(n_lat, S, rows, q_ref, k_ref, v_ref, bias_ref, o_ref):
    i = pl.program_id(1)
    masks = _half_masks()
    T = k_ref.shape[0]

    @pl.when(i < n_lat)
    def _():
        start = GRID_W * jnp.clip(NA_TILE_ROWS * i - NA_WIN_ROWS // 2, 0, rows - NA_BAND_ROWS)
        start = pl.multiple_of(start, GRID_W)
        for p in range(NA_W // LANES):
            sl = slice(p * LANES, (p + 1) * LANES)
            band = (k_ref[pl.ds(start, NA_BAND), sl], v_ref[pl.ds(start, NA_BAND), sl],
                    (bias_ref[2 * p], bias_ref[2 * p + 1]))
            ctx = (k_ref[S:T, sl], v_ref[S:T, sl], None)
            o_ref[:, sl] = _pair_attend(q_ref[:, sl], [band, ctx], masks).astype(bf16)

    @pl.when(i >= n_lat)
    def _():
        for p in range(NA_W // LANES):
            sl = slice(p * LANES, (p + 1) * LANES)
            ctx = (k_ref[S:T, sl], v_ref[S:T, sl], None)
            o_ref[:, sl] = _pair_attend(q_ref[:, sl], [ctx], masks).astype(bf16)


def _na(qka, proj, bias, S):
    B, T, _ = qka.shape
    nt, n_lat, rows = T // TM, S // TM, S // GRID_W

    def bias_idx(b, i):
        return (jnp.where(i == 0, 0, jnp.where(i >= n_lat - 1, 2, 1)), 0, 0, 0)

    return pl.pallas_call(
        functools.partial(_na_kernel, n_lat, S, rows),
        grid=(B, nt),
        in_specs=[
            pl.BlockSpec((None, TM, NA_W), lambda b, i: (b, i, A_NAQ // NA_W)),
            pl.BlockSpec((None, T, NA_W), lambda b, i: (b, 0, A_NAK // NA_W)),
            pl.BlockSpec((None, T, NA_W), lambda b, i: (b, 0, P_NAV // NA_W)),
            pl.BlockSpec((None, NA_HEADS, TM, NA_BAND), bias_idx),
        ],
        out_specs=pl.BlockSpec((None, TM, NA_W), lambda b, i: (b, i, 0)),
        out_shape=jax.ShapeDtypeStruct((B, T, NA_W), bf16),
        compiler_params=_params("arbitrary", "arbitrary"),
        name="na_attn",
    )(qka, qka, proj, bias)


def _na_bias(rpb, rows):
    n_tiles = rows // NA_TILE_ROWS
    col = np.arange(GRID_W)
    c0 = np.clip(col - NA_WIN_COLS // 2, 0, GRID_W - NA_WIN_COLS)
    col_ok = (col[None, :] >= c0[:, None]) & (col[None, :] < c0[:, None] + NA_WIN_COLS)
    dc = np.clip(col[None, :] - col[:, None] + NA_WIN_COLS - 1, 0, 2 * NA_WIN_COLS - 2)
    tz = jnp.where(col_ok, rpb[:, :, dc], NEG)
    dr = np.zeros((3, NA_TILE_ROWS, NA_BAND_ROWS), np.int32)
    ok = np.zeros((3, NA_TILE_ROWS, NA_BAND_ROWS), bool)
    for ty, rt in enumerate((0, 1, n_tiles - 1)):
        bs = int(np.clip(NA_TILE_ROWS * rt - NA_WIN_ROWS // 2, 0, rows - NA_BAND_ROWS))
        for qr in range(NA_TILE_ROWS):
            r = NA_TILE_ROWS * rt + qr
            s_r = int(np.clip(r - NA_WIN_ROWS // 2, 0, rows - NA_WIN_ROWS))
            for w in range(NA_BAND_ROWS):
                kr = bs + w
                ok[ty, qr, w] = s_r <= kr < s_r + NA_WIN_ROWS
                dr[ty, qr, w] = np.clip(kr - r + NA_WIN_ROWS - 1, 0, 2 * NA_WIN_ROWS - 2)
    blk = tz[:, dr]
    blk = jnp.where(ok[None, :, :, :, None, None], blk, NEG)
    blk = blk.transpose(1, 0, 2, 4, 3, 5)
    return blk.reshape(3, NA_HEADS, TM, NA_BAND)


def _mla_kernel(n_q, S, q_ref, k_ref, v_ref, o_ref):
    i = pl.program_id(2)
    T = k_ref.shape[0]

    def attend(q, bounds):
        m_run = acc = None
        for lo, hi in bounds:
            s = _nt(q, k_ref[lo:hi, :])
            m_new = jnp.max(s, axis=-1, keepdims=True)
            if m_run is not None:
                m_new = jnp.maximum(m_run, m_new)
            pv = _mm(jnp.exp2(s - m_new).astype(bf16), v_ref[lo:hi, :])
            acc = pv if acc is None else acc * jnp.exp2(m_run - m_new) + pv
            m_run = m_new
        return (acc[:, :MLA_V] / acc[:, MLA_V:MLA_V + 1]).astype(bf16)

    @pl.when(i < n_q)
    def _():
        chunks = [(lo, lo + MLA_TK) for lo in range(0, S, MLA_TK)] + [(S, T)]
        o_ref[...] = attend(q_ref[...], chunks)

    @pl.when(i >= n_q)
    def _():
        o_ref[0:T - S, :] = attend(q_ref[0:T - S, :], [(S, T)])


def _mla(mq, mk, mv, S):
    B, T, _ = mq.shape
    n_q = S // MLA_TQ
    assert S % MLA_TQ == 0 and S % MLA_TK == 0 and T - S <= MLA_TQ
    return pl.pallas_call(
        functools.partial(_mla_kernel, n_q, S),
        grid=(B, MLA_HEADS, n_q + 1),
        in_specs=[
            pl.BlockSpec((None, MLA_TQ, MLA_PAD), lambda b, h, i: (b, i, h)),
            pl.BlockSpec((None, T, MLA_PAD), lambda b, h, i: (b, 0, h)),
            pl.BlockSpec((None, T, MLA_PAD), lambda b, h, i: (b, 0, h)),
        ],
        out_specs=pl.BlockSpec((None, MLA_TQ, MLA_V), lambda b, h, i: (b, i, h)),
        out_shape=jax.ShapeDtypeStruct((B, T, MLA_W), bf16),
        compiler_params=_params("arbitrary", "arbitrary", "arbitrary"),
        name="mla_attn",
    )(mq, mk, mv)


def _swa_kernel(n_lat, S, sink_ref, q_ref, k_ref, v_ref, o_ref):
    i = pl.program_id(1)
    masks = _half_masks()
    T = k_ref.shape[0]
    kc, vc = k_ref[S:T, :], v_ref[S:T, :]

    def sinks(p):
        return (sink_ref[SWA_HEAD_ORDER[2 * p]], sink_ref[SWA_HEAD_ORDER[2 * p + 1]])

    @pl.when(i < n_lat)
    def _():
        start = pl.multiple_of(jnp.clip(i * TM - SWA_WINDOW, 0, S - SWA_KEYS), SWA_WINDOW)
        qpos = i * TM + lax.broadcasted_iota(jnp.int32, (TM, SWA_KEYS), 0)
        kpos = start + lax.broadcasted_iota(jnp.int32, (TM, SWA_KEYS), 1)
        band = jnp.where(jnp.abs(qpos - kpos) <= SWA_WINDOW, 0.0, NEG)
        kb, vb = k_ref[pl.ds(start, SWA_KEYS), :], v_ref[pl.ds(start, SWA_KEYS), :]
        for p in range(SWA_W // LANES):
            sl = slice(p * LANES, (p + 1) * LANES)
            parts = [(kb, vb, (band, band)), (kc, vc, None)]
            o_ref[:, sl] = _pair_attend(q_ref[:, sl], parts, masks, sinks(p)).astype(bf16)

    @pl.when(i >= n_lat)
    def _():
        for p in range(SWA_W // LANES):
            sl = slice(p * LANES, (p + 1) * LANES)
            o_ref[:, sl] = _pair_attend(q_ref[:, sl], [(kc, vc, None)], masks, sinks(p)).astype(bf16)


def _swa(qka, proj, sink, S):
    B, T, _ = qka.shape
    nt, n_lat = T // TM, S // TM
    return pl.pallas_call(
        functools.partial(_swa_kernel, n_lat, S),
        grid=(B, nt),
        in_specs=[
            pl.BlockSpec(memory_space=pltpu.SMEM),
            pl.BlockSpec((None, TM, SWA_W), lambda b, i: (b, i, A_SWQ // SWA_W)),
            pl.BlockSpec((None, T, SWA_KV_W), lambda b, i: (b, 0, A_SWK // SWA_KV_W)),
            pl.BlockSpec((None, T, SWA_KV_W), lambda b, i: (b, 0, P_SWV // SWA_KV_W)),
        ],
        out_specs=pl.BlockSpec((None, TM, SWA_W), lambda b, i: (b, i, 0)),
        out_shape=jax.ShapeDtypeStruct((B, T, SWA_W), bf16),
        compiler_params=_params("arbitrary", "arbitrary"),
        name="swa_attn",
    )(sink, qka, qka, proj)


def _post_kernel(ona_ref, omla_ref, oswa_ref, gl_ref, x_ref, mod_ref, wna_ref, wmla_ref, wswa_ref, wo_ref,
                 g2_ref, rt_ref, xo_ref, h2_ref, aff_ref):
    D = x_ref.shape[1]
    merged = None
    for j, (o_ref, w_ref) in enumerate(((ona_ref, wna_ref), (omla_ref, wmla_ref), (oswa_ref, wswa_ref))):
        y = _mm(o_ref[...], w_ref[...])
        g = jax.nn.sigmoid(gl_ref[:, j * D:(j + 1) * D].astype(f32))
        merged = g * y if merged is None else merged + g * y
    res = _mm(merged.astype(bf16), wo_ref[...])
    x = x_ref[...] + mod_ref[2:3, :] * res
    xo_ref[...] = x
    ms = jnp.mean(x * x, axis=-1, keepdims=True)
    h2 = x * lax.rsqrt(ms + EPS) * g2_ref[...]
    h2 = h2 * (1.0 + mod_ref[4:5, :]) + mod_ref[3:4, :]
    h2_ref[...] = h2.astype(bf16)
    logits = jnp.dot(h2, rt_ref[...], preferred_element_type=f32, precision=lax.Precision.HIGHEST)
    e = jnp.exp(logits - jnp.max(logits, axis=1, keepdims=True))
    aff_ref[...] = e / jnp.sum(e, axis=1, keepdims=True)


def _post(ona, omla, oswa, gl, x, mod, lw, n_lat):
    B, T, D = x.shape
    nt = pl.cdiv(T, TT)
    E = lw["router"].shape[1]

    def tok(w):
        return pl.BlockSpec((None, TT, w), lambda b, i: (b, i, 0))

    def const(a):
        return pl.BlockSpec(a.shape, lambda b, i: (0,) * a.ndim)

    consts = [lw["wna"], lw["wmla"], lw["wswa"], lw["wo"], lw["g2"], lw["router"]]
    return pl.pallas_call(
        _post_kernel,
        grid=(B, nt),
        in_specs=[tok(NA_W), tok(MLA_W), tok(SWA_W), tok(N_BRANCH * D), tok(D), _mod_spec(n_lat, D)]
        + [const(a) for a in consts],
        out_specs=[tok(D), tok(D), tok(E)],
        out_shape=[jax.ShapeDtypeStruct((B, T, D), f32), jax.ShapeDtypeStruct((B, T, D), bf16),
                   jax.ShapeDtypeStruct((B, T, E), f32)],
        compiler_params=_params("arbitrary", "arbitrary"),
        name="post_attn",
    )(ona, omla, oswa, gl, x, mod, *consts)


def _lane_cumsum(mask, tri):
    E, n = mask.shape
    carry = jnp.zeros((E, 1), f32)
    outs = []
    for k in range(n // LANES):
        w = _mm(mask[:, k * LANES:(k + 1) * LANES].astype(bf16), tri) + carry
        outs.append(w)
        carry = w[:, LANES - 1:LANES]
    return jnp.concatenate(outs, axis=1)


def _select_slots(aff, cap, base, tri):
    bits = lax.bitcast_convert_type(aff, jnp.int32)
    thr = jnp.zeros((aff.shape[0], 1), jnp.int32)
    for bit in range(30, -1, -1):
        cand = thr | (1 << bit)
        cnt = jnp.sum(jnp.where(bits >= cand, 1.0, 0.0), axis=1, keepdims=True)
        thr = jnp.where(cnt >= cap, cand, thr)
    gt = jnp.where(bits > thr, 1.0, 0.0)
    eq = jnp.where(bits == thr, 1.0, 0.0)
    need = cap - jnp.sum(gt, axis=1, keepdims=True)
    sel = jnp.maximum(gt, jnp.where(_lane_cumsum(eq, tri) <= need, eq, 0.0))
    return jnp.where(sel > 0.0, _lane_cumsum(sel, tri) + base, 0.0)


def _topk_kernel(S, cap_s, cap_l, aff_ref, tri_ref, before_ref, cp_ref, bnd_ref):
    T = aff_ref.shape[1]
    tri = tri_ref[...]
    cp_ref[:, 0:S] = _select_slots(aff_ref[:, 0:S], cap_s, 0.0, tri)
    cp_ref[:, S:T] = _select_slots(aff_ref[:, S:T], cap_l, float(cap_s), tri)
    bnd_ref[...] = _mm(jnp.where(cp_ref[...] > 0.0, 1.0, 0.0).astype(bf16), before_ref[...])


def _topk(aff, S, cap_s, cap_l):
    B, E, T = aff.shape
    tri = jnp.asarray(np.triu(np.ones((LANES, LANES), np.float32)), bf16)
    before = jnp.asarray(np.arange(T)[:, None] < MOE_CHUNK * np.arange(LANES)[None, :], bf16)
    return pl.pallas_call(
        functools.partial(_topk_kernel, S, cap_s, cap_l),
        grid=(B,),
        in_specs=[pl.BlockSpec((None, E, T), lambda b: (b, 0, 0)),
                  pl.BlockSpec((LANES, LANES), lambda b: (0, 0)),
                  pl.BlockSpec((T, LANES), lambda b: (0, 0))],
        out_specs=[pl.BlockSpec((None, E, T), lambda b: (b, 0, 0)),
                   pl.BlockSpec((None, E, LANES), lambda b: (b, 0, 0))],
        out_shape=[jax.ShapeDtypeStruct((B, E, T), f32), jax.ShapeDtypeStruct((B, E, LANES), f32)],
        compiler_params=_params("arbitrary"),
        name="expert_select",
    )(aff, tri, before)


def _window_count(lo, hi, r0):
    return jnp.where(hi > lo, lax.div(hi - r0 + (MOE_WIN - 1), MOE_WIN), 0)


def _moe_kernel(n_slots, n_e, bnd_ref, h2_ref, cp_ref, aff_ref, wg_ref, wu_ref, wd_ref, y_ref, xg_ref, gate_ref):
    b, e = pl.program_id(0), pl.program_id(1)
    n_chunks = cp_ref.shape[0]
    base = (b * n_e + e) * LANES
    xg_ref[...] = jnp.zeros_like(xg_ref)
    gate_ref[...] = jnp.zeros_like(gate_ref)
    sub = lax.broadcasted_iota(jnp.int32, (MOE_WIN, 1), 0)

    def chunk(k, carry):
        lo, hi = bnd_ref[base + k], bnd_ref[base + k + 1]
        r0 = lax.div(lo, 16) * 16

        def window(w, c):
            r = pl.multiple_of(r0 + w * MOE_WIN, 16)
            hit = cp_ref[k] == (sub + (r + 1)).astype(f32)
            tok = h2_ref[pl.ds(pl.multiple_of(k * MOE_CHUNK, MOE_CHUNK), MOE_CHUNK), :]
            xg_ref[pl.ds(r, MOE_WIN), :] += _mm(jnp.where(hit, 1.0, 0.0).astype(bf16), tok)
            gate_ref[pl.ds(r, MOE_WIN), :] += jnp.sum(jnp.where(hit, aff_ref[k], 0.0), axis=1, keepdims=True)
            return c

        lax.fori_loop(0, _window_count(lo, hi, r0), window, 0)
        return carry

    lax.fori_loop(0, n_chunks, chunk, 0)
    xg = xg_ref[0:n_slots, :].astype(bf16)
    a = _mm(xg, wg_ref[...])
    u = _mm(xg, wu_ref[...])
    act = (a * jax.nn.sigmoid(a) * u).astype(bf16)
    y_ref[0:n_slots, :] = (_mm(act, wd_ref[...]) * gate_ref[0:n_slots, :]).astype(bf16)
    y_ref[n_slots:, :] = jnp.zeros((y_ref.shape[0] - n_slots, y_ref.shape[1]), bf16)


def _moe(h2, cp, bnd, aff, wg, wu, wd, n_slots):
    B, T, D = h2.shape
    E = cp.shape[1]
    F = wg.shape[2]
    nc = T // MOE_CHUNK
    rows = n_slots + MOE_WIN
    assert n_slots % 16 == 0 and nc + 1 <= LANES
    cp = cp.reshape(B, E, nc, 1, MOE_CHUNK)
    aff = aff.reshape(B, E, nc, 1, MOE_CHUNK)
    row = pl.BlockSpec((None, None, nc, 1, MOE_CHUNK), lambda b, e, s: (b, e, 0, 0, 0))
    return pl.pallas_call(
        functools.partial(_moe_kernel, n_slots, E),
        grid_spec=pltpu.PrefetchScalarGridSpec(
            num_scalar_prefetch=1,
            grid=(B, E),
            in_specs=[
                pl.BlockSpec((None, T, D), lambda b, e, s: (b, 0, 0), pipeline_mode=pl.Buffered(1)),
                row, row,
                pl.BlockSpec((None, D, F), lambda b, e, s: (e, 0, 0)),
                pl.BlockSpec((None, D, F), lambda b, e, s: (e, 0, 0)),
                pl.BlockSpec((None, F, D), lambda b, e, s: (e, 0, 0)),
            ],
            out_specs=pl.BlockSpec((None, None, rows, D), lambda b, e, s: (b, e, 0, 0)),
            scratch_shapes=[pltpu.VMEM((rows, D), f32), pltpu.VMEM((rows, 1), f32)],
        ),
        out_shape=jax.ShapeDtypeStruct((B, E, rows, D), bf16),
        compiler_params=_params("arbitrary", "arbitrary"),
        name="moe_ffn",
    )(bnd, h2, cp, aff, wg, wu, wd)


def _combine_kernel(n_e, rows, bnd_ref, y_ref, cpt_ref, expand_ref, o_ref):
    b, k = pl.program_id(0), pl.program_id(1)
    cpt = cpt_ref[...]
    hi_part = jnp.floor(cpt * (1.0 / 32.0))
    lo_part = cpt - 32.0 * hi_part
    expand = expand_ref[...]
    rank = 32.0 * _mm(hi_part.astype(bf16), expand) + _mm(lo_part.astype(bf16), expand)
    lane = lax.broadcasted_iota(jnp.int32, (1, n_e * MOE_WIN), 1)
    lane_e = lax.div(lane, MOE_WIN)
    starts = []
    tgt = (lane - lane_e * MOE_WIN + 1).astype(f32)
    for e in range(n_e):
        r0 = lax.div(bnd_ref[(b * n_e + e) * LANES + k], 16) * 16
        starts.append(r0)
        tgt = tgt + jnp.where(lane_e == e, r0.astype(f32), 0.0)
    onehot = jnp.where(rank == tgt, 1.0, 0.0).astype(bf16)
    ycat = jnp.concatenate(
        [y_ref[pl.ds(pl.multiple_of(e * rows + starts[e], 16), MOE_WIN), :] for e in range(n_e)], axis=0)
    o_ref[...] = _mm(onehot, ycat)

    lane_w = lax.broadcasted_iota(jnp.int32, (1, MOE_WIN), 1)
    for e in range(n_e):
        lo, hi = bnd_ref[(b * n_e + e) * LANES + k], bnd_ref[(b * n_e + e) * LANES + k + 1]
        n_win = _window_count(lo, hi, starts[e])

        @pl.when(n_win > 1)
        def _(e=e, n_win=n_win):
            rank_e = rank[:, e * MOE_WIN:(e + 1) * MOE_WIN]

            def extra(w, c):
                r = pl.multiple_of(starts[e] + w * MOE_WIN, 16)
                hit = rank_e == (lane_w + (r + 1)).astype(f32)
                o_ref[...] += _mm(jnp.where(hit, 1.0, 0.0).astype(bf16),
                                  y_ref[pl.ds(pl.multiple_of(e * rows + r, 16), MOE_WIN), :])
                return c

            lax.fori_loop(1, n_win, extra, 0)


def _combine(y, cp, bnd):
    B, E, rows, D = y.shape
    T = cp.shape[2]
    nc = T // MOE_CHUNK
    expand = jnp.asarray(np.kron(np.eye(E), np.ones((1, MOE_WIN))), bf16)
    return pl.pallas_call(
        functools.partial(_combine_kernel, E, rows),
        grid_spec=pltpu.PrefetchScalarGridSpec(
            num_scalar_prefetch=1,
            grid=(B, nc),
            in_specs=[
                pl.BlockSpec((None, E * rows, D), lambda b, k, s: (b, 0, 0), pipeline_mode=pl.Buffered(1)),
                pl.BlockSpec((None, MOE_CHUNK, E), lambda b, k, s: (b, k, 0)),
                pl.BlockSpec(expand.shape, lambda b, k, s: (0, 0)),
            ],
            out_specs=pl.BlockSpec((None, MOE_CHUNK, D), lambda b, k, s: (b, k, 0)),
        ),
        out_shape=jax.ShapeDtypeStruct((B, T, D), f32),
        compiler_params=_params("arbitrary", "arbitrary"),
        name="moe_combine",
    )(bnd, y.reshape(B, E * rows, D), cp.transpose(0, 2, 1), expand)


def _final_kernel(x_ref, moe_ref, mod_ref, o_ref):
    o_ref[...] = x_ref[...] + mod_ref[5:6, :] * moe_ref[...]


def _final(x, moe, mod, S):
    B, T, D = x.shape
    return pl.pallas_call(
        _final_kernel,
        grid=(B, S // TT),
        in_specs=[
            pl.BlockSpec((None, TT, D), lambda b, i: (b, i, 0)),
            pl.BlockSpec((None, TT, D), lambda b, i: (b, i, 0)),
            pl.BlockSpec((None, None, 6, D), lambda b, i: (b, 0, 0, 0)),
        ],
        out_specs=pl.BlockSpec((None, TT, D), lambda b, i: (b, i, 0)),
        out_shape=jax.ShapeDtypeStruct((B, S, D), f32),
        compiler_params=_params("arbitrary", "arbitrary"),
        name="final_residual",
    )(x, moe, mod)


def _rope_tables(S, L):
    t = jnp.arange(S)
    row, col = t // GRID_W, t % GRID_W
    f = 16
    inv = ROPE_BASE ** (-jnp.arange(f, dtype=f32) / f)
    ar = row.astype(f32)[:, None] * inv
    ac = col.astype(f32)[:, None] * inv
    cos = jnp.concatenate([jnp.cos(ar), jnp.cos(ar), jnp.cos(ac), jnp.cos(ac)], axis=1)
    sin = jnp.concatenate([-jnp.sin(ar), jnp.sin(ar), -jnp.sin(ac), jnp.sin(ac)], axis=1)
    cos = jnp.concatenate([jnp.tile(cos, (1, 2)), jnp.ones((L, LANES), f32)], axis=0)
    sin = jnp.concatenate([jnp.tile(sin, (1, 2)), jnp.zeros((L, LANES), f32)], axis=0)
    return cos, sin


def _pad_heads(w, n_heads, width, padded):
    lead = w.shape[:-1]
    w = w.reshape(lead + (n_heads, width))
    w = jnp.pad(w, [(0, 0)] * len(lead) + [(0, 0), (0, padded - width)])
    return w.reshape(lead + (n_heads * padded,))


def _layer_weights(l, D, w_in, norm2_g, na_q_g, na_k_g, mla_q_norm, mla_w_uq, mla_kv_norm, mla_w_ukv,
                   mla_q_g, mla_k_g, swa_q_g, swa_k_g, w_na_o, w_mla_o, w_swa_o, w_o, router,
                   w_gate, w_up, w_down):
    o = np.cumsum((0, NA_W, NA_W, NA_W, MLA_Q_RANK, MLA_KV_RANK, MLA_ROPE, SWA_W, SWA_KV_W, SWA_KV_W))
    o_naq, o_nak, o_nav, o_cq, o_ckv, o_kr, o_swq, o_swk, o_swv, o_gl = (int(v) for v in o)
    wi = w_in[l]
    swq = wi[:, o_swq:o_swq + SWA_W].reshape(D, SWA_HEADS, SWA_DIM)[:, SWA_HEAD_ORDER, :].reshape(D, SWA_W)
    kr = jnp.pad(wi[:, o_kr:o_kr + MLA_ROPE], ((0, 0), (0, LANES - MLA_ROPE)))
    w_all = jnp.concatenate([
        wi[:, o_naq:o_naq + 3 * NA_W], swq, wi[:, o_cq:o_cq + MLA_Q_RANK], wi[:, o_ckv:o_ckv + MLA_KV_RANK], kr,
        wi[:, o_swk:o_swk + 2 * SWA_KV_W], wi[:, o_gl:]], axis=1).astype(bf16)
    ukv = mla_w_ukv[l].reshape(MLA_KV_RANK, MLA_HEADS, MLA_NOPE + MLA_V)
    swa_o = w_swa_o[l].reshape(SWA_HEADS, SWA_DIM, D)[SWA_HEAD_ORDER, :, :].reshape(SWA_W, D)
    return dict(
        w_all=w_all,
        naq_g=jnp.tile(na_q_g[l], NA_HEADS)[None], nak_g=jnp.tile(na_k_g[l], NA_HEADS)[None],
        swq_g=jnp.tile(swa_q_g[l], SWA_HEADS)[None], swk_g=jnp.tile(swa_k_g[l], SWA_KV_HEADS)[None],
        qn=mla_q_norm[l][None], kvn=mla_kv_norm[l][None],
        wuq=_pad_heads(mla_w_uq[l], MLA_HEADS, MLA_QK, MLA_PAD).astype(bf16),
        wuk=ukv[:, :, :MLA_NOPE].reshape(MLA_KV_RANK, MLA_HEADS * MLA_NOPE).astype(bf16),
        wuv=ukv[:, :, MLA_NOPE:].reshape(MLA_KV_RANK, MLA_W).astype(bf16),
        qg=jnp.tile(jnp.pad(mla_q_g[l], (0, MLA_PAD - MLA_QK)), MLA_HEADS)[None],
        kg=jnp.tile(jnp.pad(mla_k_g[l], (0, MLA_PAD - MLA_QK)), MLA_HEADS)[None],
        wna=w_na_o[l].astype(bf16), wmla=w_mla_o[l].astype(bf16), wswa=swa_o.astype(bf16),
        wo=w_o[l].astype(bf16), g2=norm2_g[l][None], router=router[l],
        wg=w_gate[l].astype(bf16), wu=w_up[l].astype(bf16), wd=w_down[l].astype(bf16),
    )


def kernel(x, c, ctx, c_ctx, norm1_g, norm2_g, w_ada, b_ada, w_in, na_q_g, na_k_g, na_rpb, mla_q_norm, mla_w_uq, mla_kv_norm, mla_w_ukv, mla_q_g, mla_k_g, swa_q_g, swa_k_g, swa_sink, w_na_o, w_mla_o, w_swa_o, w_o, router, w_gate, w_up, w_down):
    B, S, D = x.shape
    L = ctx.shape[1]
    depth = w_in.shape[0]
    E = router.shape[2]
    T = S + L
    n_lat = S // TT
    rows = S // GRID_W
    assert S % TT == 0 and L % TM == 0 and L <= TT and rows >= NA_BAND_ROWS and S >= SWA_KEYS and B < 16
    cap_s = CAPACITY_FACTOR * S // E
    cap_l = CAPACITY_FACTOR * L // E

    cpad = jnp.zeros((16, D), f32).at[:B].set(c).at[B].set(c_ctx)
    mod = _ada(cpad, w_ada, b_ada).reshape(depth, 16, 6, D)
    mod = jnp.stack([mod[:, :B], jnp.broadcast_to(mod[:, B:B + 1], (depth, B, 6, D))], axis=2)

    cos, sin = _rope_tables(S, L)
    gmat = jnp.asarray(np.kron(np.eye(NA_HEADS), np.full((NA_DIM, NA_DIM), 1.0 / NA_DIM)), bf16)

    xs = jnp.concatenate([x, ctx], axis=1)
    moe = None
    for l in range(depth):
        lw = _layer_weights(l, D, w_in, norm2_g, na_q_g, na_k_g, mla_q_norm, mla_w_uq, mla_kv_norm, mla_w_ukv,
                            mla_q_g, mla_k_g, swa_q_g, swa_k_g, w_na_o, w_mla_o, w_swa_o, w_o, router,
                            w_gate, w_up, w_down)
        modp = mod[l - 1] if l > 0 else None
        xs, proj, gl = _in_proj(xs, moe, modp, mod[l], norm1_g[l][None], lw["w_all"], n_lat)
        qka, mq, mk, mv = _prep(proj, cos, sin, gmat, lw)
        ona = _na(qka, proj, _na_bias(na_rpb[l], rows), S)
        omla = _mla(mq, mk, mv, S)
        oswa = _swa(qka, proj, swa_sink[l], S)
        xs, h2, aff = _post(ona, omla, oswa, gl, xs, mod[l], lw, n_lat)
        aff = aff.transpose(0, 2, 1)
        cp, bnd = _topk(aff, S, cap_s, cap_l)
        bnd = bnd.astype(jnp.int32).reshape(-1)
        y = _moe(h2, cp, bnd, aff, lw["wg"], lw["wu"], lw["wd"], cap_s + cap_l)
        moe = _combine(y, cp, bnd)
    return _final(xs, moe, mod[depth - 1], S)
```

```python
import functools

import numpy as np
import jax
import jax.numpy as jnp
from jax import lax
from jax.experimental import pallas as pl
from jax.experimental.pallas import tpu as pltpu

GRID_W = 64
NA_HEADS, NA_DIM, NA_WIN_ROWS, NA_WIN_COLS = 6, 64, 8, 16
MLA_HEADS, MLA_Q_RANK, MLA_KV_RANK, MLA_NOPE, MLA_ROPE, MLA_V = 4, 256, 128, 128, 64, 128
MLA_QK = MLA_NOPE + MLA_ROPE
MLA_PAD = 256
SWA_HEADS, SWA_KV_HEADS, SWA_DIM, SWA_WINDOW = 6, 2, 64, 128
CAPACITY_FACTOR = 2
N_BRANCH = 3
ROPE_BASE = 10000.0
EPS = 1e-6

NA_W = NA_HEADS * NA_DIM
SWA_W = SWA_HEADS * SWA_DIM
SWA_KV_W = SWA_KV_HEADS * SWA_DIM
MLA_W = MLA_HEADS * MLA_V

LANES = 128
TM = 256
TT = 512
MLA_TQ = 1024
MLA_TK = 1024
LOG2E = 1.4426950408889634
NA_TILE_ROWS = TM // GRID_W
NA_BAND_ROWS = NA_TILE_ROWS + NA_WIN_ROWS - 1
NA_BAND = NA_BAND_ROWS * GRID_W
SWA_KEYS = TM + 2 * SWA_WINDOW
MOE_CHUNK = 256
MOE_WIN = 64
NEG = -1e30
VMEM_LIMIT = 56 * 1024 * 1024

P_NAQ, P_NAK, P_NAV, P_SWQ = 0, 384, 768, 1152
P_CQ, P_CKV, P_KR, P_SWK, P_SWV = 1536, 1792, 1920, 2048, 2176
PROJ_W = 2304
N_CHUNK = 768
A_NAQ, A_NAK, A_SWQ, A_SWK = 0, 384, 768, 1152
QKA_W = 1280
SWA_HEAD_ORDER = (0, 3, 1, 4, 2, 5)

bf16 = jnp.bfloat16
f32 = jnp.float32


def _mm(a, b):
    return jnp.dot(a, b, preferred_element_type=f32)


def _nt(a, b):
    return lax.dot_general(a, b, (((1,), (1,)), ((), ())), preferred_element_type=f32)


def _params(*sem):
    return pltpu.CompilerParams(dimension_semantics=sem, vmem_limit_bytes=VMEM_LIMIT)


def _ada_kernel(c_ref, w_ref, b_ref, o_ref):
    a = c_ref[...]
    a = (a * jax.nn.sigmoid(a)).astype(bf16)
    o_ref[...] = _mm(a, w_ref[...].astype(bf16)) + b_ref[...]


def _ada(cpad, w_ada, b_ada):
    L, D, N = w_ada.shape
    tn = 1536
    return pl.pallas_call(
        _ada_kernel,
        grid=(L, N // tn),
        in_specs=[
            pl.BlockSpec((16, D), lambda l, j: (0, 0)),
            pl.BlockSpec((None, D, tn), lambda l, j: (l, 0, j)),
            pl.BlockSpec((None, 1, tn), lambda l, j: (l, 0, j)),
        ],
        out_specs=pl.BlockSpec((None, 16, tn), lambda l, j: (l, 0, j)),
        out_shape=jax.ShapeDtypeStruct((L, 16, N), f32),
        compiler_params=_params("arbitrary", "arbitrary"),
        name="ada",
    )(cpad, w_ada, b_ada.reshape(L, 1, N))


def _in_proj_kernel(has_moe, n_gl_chunks, *refs):
    if has_moe:
        x_ref, moe_ref, modp_ref, mod_ref, g_ref, w_ref, xo_ref, proj_ref, gl_ref = refs
        x = x_ref[...] + modp_ref[5:6, :] * moe_ref[...]
        xo_ref[...] = x
    else:
        x_ref, mod_ref, g_ref, w_ref, proj_ref, gl_ref = refs
        x = x_ref[...]
    ms = jnp.mean(x * x, axis=-1, keepdims=True)
    h = x * lax.rsqrt(ms + EPS) * g_ref[...]
    h = (h * (1.0 + mod_ref[1:2, :]) + mod_ref[0:1, :]).astype(bf16)
    for c in range(PROJ_W // N_CHUNK):
        sl = slice(c * N_CHUNK, (c + 1) * N_CHUNK)
        proj_ref[:, sl] = _mm(h, w_ref[:, sl]).astype(bf16)
    for c in range(n_gl_chunks):
        sl = slice(c * N_CHUNK, (c + 1) * N_CHUNK)
        gl_ref[:, sl] = _mm(h, w_ref[:, PROJ_W + c * N_CHUNK:PROJ_W + (c + 1) * N_CHUNK]).astype(bf16)


def _mod_spec(n_lat, D):
    return pl.BlockSpec((None, None, 6, D), lambda b, i: (b, jnp.where(i >= n_lat, 1, 0), 0, 0))


def _in_proj(x, moe, modp, mod, g1, w, n_lat):
    B, T, D = x.shape
    nt = pl.cdiv(T, TT)
    has_moe = moe is not None
    n_gl_chunks = N_BRANCH * D // N_CHUNK
    tok = pl.BlockSpec((None, TT, D), lambda b, i: (b, i, 0))
    in_specs = [tok]
    args = [x]
    if has_moe:
        in_specs += [tok, _mod_spec(n_lat, D)]
        args += [moe, modp]
    in_specs += [
        _mod_spec(n_lat, D),
        pl.BlockSpec((1, D), lambda b, i: (0, 0)),
        pl.BlockSpec(w.shape, lambda b, i: (0, 0), pipeline_mode=pl.Buffered(1)),
    ]
    args += [mod, g1, w]
    out_specs = [
        pl.BlockSpec((None, TT, PROJ_W), lambda b, i: (b, i, 0)),
        pl.BlockSpec((None, TT, N_BRANCH * D), lambda b, i: (b, i, 0)),
    ]
    out_shape = [
        jax.ShapeDtypeStruct((B, T, PROJ_W), bf16),
        jax.ShapeDtypeStruct((B, T, N_BRANCH * D), bf16),
    ]
    if has_moe:
        out_specs = [tok] + out_specs
        out_shape = [jax.ShapeDtypeStruct((B, T, D), f32)] + out_shape
    res = pl.pallas_call(
        functools.partial(_in_proj_kernel, has_moe, n_gl_chunks),
        grid=(B, nt),
        in_specs=in_specs,
        out_specs=out_specs,
        out_shape=out_shape,
        compiler_params=_params("arbitrary", "arbitrary"),
        name="in_proj",
    )(*args)
    if has_moe:
        return res
    return [x] + list(res)


def _prep_kernel(proj_ref, cos_ref, sin_ref, gmat_ref, naq_g, nak_g, swq_g, swk_g,
                 qn_ref, wuq_ref, qg_ref, kvn_ref, wuk_ref, wuv_ref, kg_ref,
                 qka_ref, mq_ref, mk_ref, mv_ref):
    cos = cos_ref[...]
    sin = sin_ref[...]
    lane = lax.broadcasted_iota(jnp.int32, cos.shape, 1)
    first = (lane & 16) == 0

    def rope(x):
        partner = jnp.where(first, pltpu.roll(x, LANES - 16, 1), pltpu.roll(x, 16, 1))
        return x * cos + partner * sin

    def headnorm(x, gain):
        w = x.shape[1]
        ms = _mm((x * x).astype(bf16), gmat_ref[:w, :w])
        return x * lax.rsqrt(ms + EPS) * gain

    def rmsnorm(x, gain):
        ms = jnp.mean(x * x, axis=-1, keepdims=True)
        return x * lax.rsqrt(ms + EPS) * gain

    na_scale = NA_DIM ** -0.5
    naq = headnorm(proj_ref[:, P_NAQ:P_NAQ + NA_W].astype(f32), naq_g[...])
    qka_ref[:, A_NAQ:A_NAQ + NA_W] = (naq * na_scale).astype(bf16)
    nak = headnorm(proj_ref[:, P_NAK:P_NAK + NA_W].astype(f32), nak_g[...])
    qka_ref[:, A_NAK:A_NAK + NA_W] = nak.astype(bf16)

    sw_scale = SWA_DIM ** -0.5
    swq = headnorm(proj_ref[:, P_SWQ:P_SWQ + SWA_W].astype(f32), swq_g[...])
    for p in range(SWA_W // LANES):
        sl = slice(p * LANES, (p + 1) * LANES)
        qka_ref[:, A_SWQ + p * LANES:A_SWQ + (p + 1) * LANES] = (rope(swq[:, sl]) * sw_scale).astype(bf16)
    swk = headnorm(proj_ref[:, P_SWK:P_SWK + SWA_KV_W].astype(f32), swk_g[...])
    qka_ref[:, A_SWK:A_SWK + SWA_KV_W] = rope(swk).astype(bf16)

    mla_scale = MLA_QK ** -0.5 * LOG2E
    cq = rmsnorm(proj_ref[:, P_CQ:P_CQ + MLA_Q_RANK].astype(f32), qn_ref[...]).astype(bf16)
    q = _mm(cq, wuq_ref[...])
    ckv = rmsnorm(proj_ref[:, P_CKV:P_CKV + MLA_KV_RANK].astype(f32), kvn_ref[...]).astype(bf16)
    kn = _mm(ckv, wuk_ref[...])
    v = _mm(ckv, wuv_ref[...]).astype(bf16)
    ones_col = jnp.where(lax.broadcasted_iota(jnp.int32, (v.shape[0], LANES), 1) == 0, 1.0, 0.0).astype(bf16)
    for h in range(MLA_HEADS):
        mv_ref[:, h * MLA_PAD:h * MLA_PAD + MLA_V] = v[:, h * MLA_V:(h + 1) * MLA_V]
        mv_ref[:, h * MLA_PAD + MLA_V:(h + 1) * MLA_PAD] = ones_col
    kr = proj_ref[:, P_KR:P_KR + LANES].astype(f32)
    kr_ss = jnp.sum(kr * kr, axis=-1, keepdims=True)
    for h in range(MLA_HEADS):
        o = h * MLA_PAD
        qh = q[:, o:o + MLA_PAD]
        r = lax.rsqrt(jnp.sum(qh * qh, axis=-1, keepdims=True) * (1.0 / MLA_QK) + EPS)
        qh = qh * r * qg_ref[:, o:o + MLA_PAD]
        mq_ref[:, o:o + LANES] = (qh[:, :LANES] * mla_scale).astype(bf16)
        mq_ref[:, o + LANES:o + MLA_PAD] = (rope(qh[:, LANES:]) * mla_scale).astype(bf16)
        kh = kn[:, h * MLA_NOPE:(h + 1) * MLA_NOPE]
        r = lax.rsqrt((jnp.sum(kh * kh, axis=-1, keepdims=True) + kr_ss) * (1.0 / MLA_QK) + EPS)
        mk_ref[:, o:o + LANES] = (kh * r * kg_ref[:, o:o + LANES]).astype(bf16)
        mk_ref[:, o + LANES:o + MLA_PAD] = rope(kr * r * kg_ref[:, o + LANES:o + MLA_PAD]).astype(bf16)


def _prep(proj, cos, sin, gmat, lw):
    B, T, _ = proj.shape
    nt = pl.cdiv(T, TT)

    def const(a):
        return pl.BlockSpec(a.shape, lambda b, i: (0,) * a.ndim)

    consts = [gmat, lw["naq_g"], lw["nak_g"], lw["swq_g"], lw["swk_g"], lw["qn"], lw["wuq"], lw["qg"],
              lw["kvn"], lw["wuk"], lw["wuv"], lw["kg"]]
    widths = (QKA_W, MLA_HEADS * MLA_PAD, MLA_HEADS * MLA_PAD, MLA_HEADS * MLA_PAD)
    return pl.pallas_call(
        _prep_kernel,
        grid=(B, nt),
        in_specs=[
            pl.BlockSpec((None, TT, PROJ_W), lambda b, i: (b, i, 0)),
            pl.BlockSpec((TT, LANES), lambda b, i: (i, 0)),
            pl.BlockSpec((TT, LANES), lambda b, i: (i, 0)),
        ] + [const(a) for a in consts],
        out_specs=[pl.BlockSpec((None, TT, w), lambda b, i: (b, i, 0)) for w in widths],
        out_shape=[jax.ShapeDtypeStruct((B, T, w), bf16) for w in widths],
        compiler_params=_params("arbitrary", "arbitrary"),
        name="prep",
    )(proj, cos, sin, *consts)


def _half_masks():
    lane = lax.broadcasted_iota(jnp.int32, (1, LANES), 1)
    lo = jnp.where(lane < 64, 1.0, 0.0).astype(bf16)
    return lo, (1.0 - lo.astype(f32)).astype(bf16)


def _pair_attend(qp, parts, masks, sinks=None):
    n_q = qp.shape[0]
    q2 = jnp.concatenate([qp * masks[0], qp * masks[1]], axis=0)
    scores = []
    for k, _, bias in parts:
        s = _nt(q2, k)
        scores.append(s if bias is None else s + bias)
    m = functools.reduce(jnp.maximum, [jnp.max(s, axis=-1, keepdims=True) for s in scores])
    if sinks is not None:
        first = lax.broadcasted_iota(jnp.int32, (2 * n_q, 1), 0) < n_q
        sink = jnp.where(first, sinks[0], sinks[1])
        m = jnp.maximum(m, sink)
    ps = [jnp.exp(s - m) for s in scores]
    l = functools.reduce(jnp.add, [jnp.sum(p, axis=-1, keepdims=True) for p in ps])
    if sinks is not None:
        l = l + jnp.exp(sink - m)
    o = functools.reduce(jnp.add, [_mm(p.astype(bf16), v) for p, (_, v, _) in zip(ps, parts)]) / l
    lane = lax.broadcasted_iota(jnp.int32, (1, LANES), 1)
    return jnp.where(lane < 64, o[:n_q], o[n_q:])


def _na_kernel(n_lat, S, rows, q_ref, k_ref, v_ref, bias_ref, o_ref):
    i = pl.program_id(1)
    masks = _half_masks()
    T = k_ref.shape[0]

    @pl.when(i < n_lat)
    def _():
        start = GRID_W * jnp.clip(NA_TILE_ROWS * i - NA_WIN_ROWS // 2, 0, rows - NA_BAND_ROWS)
        start = pl.multiple_of(start, GRID_W)
        for p in range(NA_W // LANES):
            sl = slice(p * LANES, (p + 1) * LANES)
            band = (k_ref[pl.ds(start, NA_BAND), sl], v_ref[pl.ds(start, NA_BAND), sl],
                    bias_ref[2 * p:2 * p + 2].reshape(2 * TM, NA_BAND))
            ctx = (k_ref[S:T, sl], v_ref[S:T, sl], None)
            o_ref[:, sl] = _pair_attend(q_ref[:, sl], [band, ctx], masks).astype(bf16)

    @pl.when(i >= n_lat)
    def _():
        for p in range(NA_W // LANES):
            sl = slice(p * LANES, (p + 1) * LANES)
            ctx = (k_ref[S:T, sl], v_ref[S:T, sl], None)
            o_ref[:, sl] = _pair_attend(q_ref[:, sl], [ctx], masks).astype(bf16)


def _na(qka, proj, bias, S):
    B, T, _ = qka.shape
    nt, n_lat, rows = T // TM, S // TM, S // GRID_W

    def bias_idx(b, i):
        return (jnp.where(i == 0, 0, jnp.where(i >= n_lat - 1, 2, 1)), 0, 0, 0)

    return pl.pallas_call(
        functools.partial(_na_kernel, n_lat, S, rows),
        grid=(B, nt),
        in_specs=[
            pl.BlockSpec((None, TM, NA_W), lambda b, i: (b, i, A_NAQ // NA_W)),
            pl.BlockSpec((None, T, NA_W), lambda b, i: (b, 0, A_NAK // NA_W)),
            pl.BlockSpec((None, T, NA_W), lambda b, i: (b, 0, P_NAV // NA_W)),
            pl.BlockSpec((None, NA_HEADS, TM, NA_BAND), bias_idx),
        ],
        out_specs=pl.BlockSpec((None, TM, NA_W), lambda b, i: (b, i, 0)),
        out_shape=jax.ShapeDtypeStruct((B, T, NA_W), bf16),
        compiler_params=_params("arbitrary", "arbitrary"),
        name="na_attn",
    )(qka, qka, proj, bias)


def _na_bias(rpb, rows):
    n_tiles = rows // NA_TILE_ROWS
    col = np.arange(GRID_W)
    c0 = np.clip(col - NA_WIN_COLS // 2, 0, GRID_W - NA_WIN_COLS)
    col_ok = (col[None, :] >= c0[:, None]) & (col[None, :] < c0[:, None] + NA_WIN_COLS)
    dc = np.clip(col[None, :] - col[:, None] + NA_WIN_COLS - 1, 0, 2 * NA_WIN_COLS - 2)
    tz = jnp.where(col_ok, rpb[:, :, dc], NEG)
    dr = np.zeros((3, NA_TILE_ROWS, NA_BAND_ROWS), np.int32)
    ok = np.zeros((3, NA_TILE_ROWS, NA_BAND_ROWS), bool)
    for ty, rt in enumerate((0, 1, n_tiles - 1)):
        bs = int(np.clip(NA_TILE_ROWS * rt - NA_WIN_ROWS // 2, 0, rows - NA_BAND_ROWS))
        for qr in range(NA_TILE_ROWS):
            r = NA_TILE_ROWS * rt + qr
            s_r = int(np.clip(r - NA_WIN_ROWS // 2, 0, rows - NA_WIN_ROWS))
            for w in range(NA_BAND_ROWS):
                kr = bs + w
                ok[ty, qr, w] = s_r <= kr < s_r + NA_WIN_ROWS
                dr[ty, qr, w] = np.clip(kr - r + NA_WIN_ROWS - 1, 0, 2 * NA_WIN_ROWS - 2)
    blk = tz[:, dr]
    blk = jnp.where(ok[None, :, :, :, None, None], blk, NEG)
    blk = blk.transpose(1, 0, 2, 4, 3, 5)
    return blk.reshape(3, NA_HEADS, TM, NA_BAND)


def _mla_kernel(n_q, S, q_ref, k_ref, v_ref, o_ref):
    i = pl.program_id(2)
    T = k_ref.shape[0]

    def attend(q, bounds):
        m_run = acc = None
        for lo, hi in bounds:
            s = _nt(q, k_ref[lo:hi, :])
            m_new = jnp.max(s, axis=-1, keepdims=True)
            if m_run is not None:
                m_new = jnp.maximum(m_run, m_new)
            pv = _mm(jnp.exp2(s - m_new).astype(bf16), v_ref[lo:hi, :])
            acc = pv if acc is None else acc * jnp.exp2(m_run - m_new) + pv
            m_run = m_new
        return (acc[:, :MLA_V] / acc[:, MLA_V:MLA_V + 1]).astype(bf16)

    @pl.when(i < n_q)
    def _():
        chunks = [(lo, lo + MLA_TK) for lo in range(0, S, MLA_TK)] + [(S, T)]
        o_ref[...] = attend(q_ref[...], chunks)

    @pl.when(i >= n_q)
    def _():
        o_ref[0:T - S, :] = attend(q_ref[0:T - S, :], [(S, T)])


def _mla(mq, mk, mv, S):
    B, T, _ = mq.shape
    n_q = S // MLA_TQ
    assert S % MLA_TQ == 0 and S % MLA_TK == 0 and T - S <= MLA_TQ
    return pl.pallas_call(
        functools.partial(_mla_kernel, n_q, S),
        grid=(B, MLA_HEADS, n_q + 1),
        in_specs=[
            pl.BlockSpec((None, MLA_TQ, MLA_PAD), lambda b, h, i: (b, i, h)),
            pl.BlockSpec((None, T, MLA_PAD), lambda b, h, i: (b, 0, h)),
            pl.BlockSpec((None, T, MLA_PAD), lambda b, h, i: (b, 0, h)),
        ],
        out_specs=pl.BlockSpec((None, MLA_TQ, MLA_V), lambda b, h, i: (b, i, h)),
        out_shape=jax.ShapeDtypeStruct((B, T, MLA_W), bf16),
        compiler_params=_params("arbitrary", "arbitrary", "arbitrary"),
        name="mla_attn",
    )(mq, mk, mv)


def _swa_kernel(n_lat, S, sink_ref, q_ref, k_ref, v_ref, o_ref):
    i = pl.program_id(1)
    masks = _half_masks()
    T = k_ref.shape[0]
    kc, vc = k_ref[S:T, :], v_ref[S:T, :]

    def sinks(p):
        return (sink_ref[SWA_HEAD_ORDER[2 * p]], sink_ref[SWA_HEAD_ORDER[2 * p + 1]])

    @pl.when(i < n_lat)
    def _():
        start = pl.multiple_of(jnp.clip(i * TM - SWA_WINDOW, 0, S - SWA_KEYS), SWA_WINDOW)
        row = lax.broadcasted_iota(jnp.int32, (2 * TM, SWA_KEYS), 0)
        qpos = i * TM + jnp.where(row >= TM, row - TM, row)
        kpos = start + lax.broadcasted_iota(jnp.int32, (2 * TM, SWA_KEYS), 1)
        band = jnp.where(jnp.abs(qpos - kpos) <= SWA_WINDOW, 0.0, NEG)
        kb, vb = k_ref[pl.ds(start, SWA_KEYS), :], v_ref[pl.ds(start, SWA_KEYS), :]
        for p in range(SWA_W // LANES):
            sl = slice(p * LANES, (p + 1) * LANES)
            parts = [(kb, vb, band), (kc, vc, None)]
            o_ref[:, sl] = _pair_attend(q_ref[:, sl], parts, masks, sinks(p)).astype(bf16)

    @pl.when(i >= n_lat)
    def _():
        for p in range(SWA_W // LANES):
            sl = slice(p * LANES, (p + 1) * LANES)
            o_ref[:, sl] = _pair_attend(q_ref[:, sl], [(kc, vc, None)], masks, sinks(p)).astype(bf16)


def _swa(qka, proj, sink, S):
    B, T, _ = qka.shape
    nt, n_lat = T // TM, S // TM
    return pl.pallas_call(
        functools.partial(_swa_kernel, n_lat, S),
        grid=(B, nt),
        in_specs=[
            pl.BlockSpec(memory_space=pltpu.SMEM),
            pl.BlockSpec((None, TM, SWA_W), lambda b, i: (b, i, A_SWQ // SWA_W)),
            pl.BlockSpec((None, T, SWA_KV_W), lambda b, i: (b, 0, A_SWK // SWA_KV_W)),
            pl.BlockSpec((None, T, SWA_KV_W), lambda b, i: (b, 0, P_SWV // SWA_KV_W)),
        ],
        out_specs=pl.BlockSpec((None, TM, SWA_W), lambda b, i: (b, i, 0)),
        out_shape=jax.ShapeDtypeStruct((B, T, SWA_W), bf16),
        compiler_params=_params("arbitrary", "arbitrary"),
        name="swa_attn",
    )(sink, qka, qka, proj)


def _post_kernel(ona_ref, omla_ref, oswa_ref, gl_ref, x_ref, mod_ref, wna_ref, wmla_ref, wswa_ref, wo_ref,
                 g2_ref, rt_ref, xo_ref, h2_ref, aff_ref):
    D = x_ref.shape[1]
    merged = None
    for j, (o_ref, w_ref) in enumerate(((ona_ref, wna_ref), (omla_ref, wmla_ref), (oswa_ref, wswa_ref))):
        y = _mm(o_ref[...], w_ref[...])
        g = 0.5 * jnp.tanh(0.5 * gl_ref[:, j * D:(j + 1) * D].astype(f32)) + 0.5
        merged = g * y if merged is None else merged + g * y
    res = _mm(merged.astype(bf16), wo_ref[...])
    x = x_ref[...] + mod_ref[2:3, :] * res
    xo_ref[...] = x
    ms = jnp.mean(x * x, axis=-1, keepdims=True)
    h2 = x * lax.rsqrt(ms + EPS) * g2_ref[...]
    h2 = h2 * (1.0 + mod_ref[4:5, :]) + mod_ref[3:4, :]
    h_hi = h2.astype(bf16)
    h2_ref[...] = h_hi
    h_lo = (h2 - h_hi.astype(f32)).astype(bf16)
    parts = _mm(h_hi, rt_ref[...]) + _mm(h_lo, rt_ref[...])
    n_e = aff_ref.shape[1]
    logits = parts[:, :n_e] + parts[:, n_e:]
    e = jnp.exp(logits - jnp.max(logits, axis=1, keepdims=True))
    aff_ref[...] = e / jnp.sum(e, axis=1, keepdims=True)


def _post(ona, omla, oswa, gl, x, mod, lw, n_lat):
    B, T, D = x.shape
    nt = pl.cdiv(T, TT)
    E = lw["router"].shape[1] // 2

    def tok(w):
        return pl.BlockSpec((None, TT, w), lambda b, i: (b, i, 0))

    def const(a):
        return pl.BlockSpec(a.shape, lambda b, i: (0,) * a.ndim)

    consts = [lw["wna"], lw["wmla"], lw["wswa"], lw["wo"], lw["g2"], lw["router"]]
    return pl.pallas_call(
        _post_kernel,
        grid=(B, nt),
        in_specs=[tok(NA_W), tok(MLA_W), tok(SWA_W), tok(N_BRANCH * D), tok(D), _mod_spec(n_lat, D)]
        + [const(a) for a in consts],
        out_specs=[tok(D), tok(D), tok(E)],
        out_shape=[jax.ShapeDtypeStruct((B, T, D), f32), jax.ShapeDtypeStruct((B, T, D), bf16),
                   jax.ShapeDtypeStruct((B, T, E), f32)],
        compiler_params=_params("arbitrary", "arbitrary"),
        name="post_attn",
    )(ona, omla, oswa, gl, x, mod, *consts)


def _lane_cumsum(mask, tri):
    E, n = mask.shape
    carry = jnp.zeros((E, 1), f32)
    outs = []
    for k in range(n // LANES):
        w = _mm(mask[:, k * LANES:(k + 1) * LANES].astype(bf16), tri) + carry
        outs.append(w)
        carry = w[:, LANES - 1:LANES]
    return jnp.concatenate(outs, axis=1)


def _select_slots(aff, cap, base, tri):
    bits = lax.bitcast_convert_type(aff, jnp.int32)
    thr = jnp.zeros((aff.shape[0], 1), jnp.int32)
    for bit in range(30, -1, -1):
        cand = thr | (1 << bit)
        cnt = jnp.sum(jnp.where(bits >= cand, 1.0, 0.0), axis=1, keepdims=True)
        thr = jnp.where(cnt >= cap, cand, thr)
    gt = jnp.where(bits > thr, 1.0, 0.0)
    eq = jnp.where(bits == thr, 1.0, 0.0)
    need = cap - jnp.sum(gt, axis=1, keepdims=True)
    sel = jnp.maximum(gt, jnp.where(_lane_cumsum(eq, tri) <= need, eq, 0.0))
    return jnp.where(sel > 0.0, _lane_cumsum(sel, tri) + base, 0.0)


def _topk_kernel(S, cap_s, cap_l, aff_ref, tri_ref, before_ref, cp_ref, bnd_ref):
    T = aff_ref.shape[1]
    tri = tri_ref[...]
    cp_ref[:, 0:S] = _select_slots(aff_ref[:, 0:S], cap_s, 0.0, tri)
    cp_ref[:, S:T] = _select_slots(aff_ref[:, S:T], cap_l, float(cap_s), tri)
    bnd_ref[...] = _mm(jnp.where(cp_ref[...] > 0.0, 1.0, 0.0).astype(bf16), before_ref[...])


def _topk(aff, S, cap_s, cap_l):
    B, E, T = aff.shape
    tri = jnp.asarray(np.triu(np.ones((LANES, LANES), np.float32)), bf16)
    before = jnp.asarray(np.arange(T)[:, None] < MOE_CHUNK * np.arange(LANES)[None, :], bf16)
    return pl.pallas_call(
        functools.partial(_topk_kernel, S, cap_s, cap_l),
        grid=(B,),
        in_specs=[pl.BlockSpec((None, E, T), lambda b: (b, 0, 0)),
                  pl.BlockSpec((LANES, LANES), lambda b: (0, 0)),
                  pl.BlockSpec((T, LANES), lambda b: (0, 0))],
        out_specs=[pl.BlockSpec((None, E, T), lambda b: (b, 0, 0)),
                   pl.BlockSpec((None, E, LANES), lambda b: (b, 0, 0))],
        out_shape=[jax.ShapeDtypeStruct((B, E, T), f32), jax.ShapeDtypeStruct((B, E, LANES), f32)],
        compiler_params=_params("arbitrary"),
        name="expert_select",
    )(aff, tri, before)


def _window_count(lo, hi, r0):
    return jnp.where(hi > lo, lax.div(hi - r0 + (MOE_WIN - 1), MOE_WIN), 0)


def _moe_kernel(n_slots, n_e, bnd_ref, h2_ref, cp_ref, aff_ref, wg_ref, wu_ref, wd_ref, y_ref, xg_ref, gate_ref):
    b, e = pl.program_id(0), pl.program_id(1)
    n_chunks = cp_ref.shape[0]
    base = (b * n_e + e) * LANES
    xg_ref[...] = jnp.zeros_like(xg_ref)
    gate_ref[...] = jnp.zeros_like(gate_ref)
    sub = lax.broadcasted_iota(jnp.int32, (MOE_WIN, 1), 0)

    def chunk(k, carry):
        lo, hi = bnd_ref[base + k], bnd_ref[base + k + 1]
        r0 = lax.div(lo, 16) * 16

        def window(w, c):
            r = pl.multiple_of(r0 + w * MOE_WIN, 16)
            hit = cp_ref[k] == (sub + (r + 1)).astype(f32)
            tok = h2_ref[pl.ds(pl.multiple_of(k * MOE_CHUNK, MOE_CHUNK), MOE_CHUNK), :]
            xg_ref[pl.ds(r, MOE_WIN), :] += _mm(jnp.where(hit, 1.0, 0.0).astype(bf16), tok)
            gate_ref[pl.ds(r, MOE_WIN), :] += jnp.sum(jnp.where(hit, aff_ref[k], 0.0), axis=1, keepdims=True)
            return c

        lax.fori_loop(0, _window_count(lo, hi, r0), window, 0)
        return carry

    lax.fori_loop(0, n_chunks, chunk, 0)
    xg = xg_ref[0:n_slots, :].astype(bf16)
    a = _mm(xg, wg_ref[...])
    u = _mm(xg, wu_ref[...])
    act = (a * jax.nn.sigmoid(a) * u).astype(bf16)
    y_ref[0:n_slots, :] = (_mm(act, wd_ref[...]) * gate_ref[0:n_slots, :]).astype(bf16)
    y_ref[n_slots:, :] = jnp.zeros((y_ref.shape[0] - n_slots, y_ref.shape[1]), bf16)


def _moe(h2, cp, bnd, aff, wg, wu, wd, n_slots):
    B, T, D = h2.shape
    E = cp.shape[1]
    F = wg.shape[2]
    nc = T // MOE_CHUNK
    rows = n_slots + MOE_WIN
    assert n_slots % 16 == 0 and nc + 1 <= LANES
    cp = cp.reshape(B, E, nc, 1, MOE_CHUNK)
    aff = aff.reshape(B, E, nc, 1, MOE_CHUNK)
    row = pl.BlockSpec((None, None, nc, 1, MOE_CHUNK), lambda b, e, s: (b, e, 0, 0, 0))
    return pl.pallas_call(
        functools.partial(_moe_kernel, n_slots, E),
        grid_spec=pltpu.PrefetchScalarGridSpec(
            num_scalar_prefetch=1,
            grid=(B, E),
            in_specs=[
                pl.BlockSpec((None, T, D), lambda b, e, s: (b, 0, 0), pipeline_mode=pl.Buffered(1)),
                row, row,
                pl.BlockSpec((None, D, F), lambda b, e, s: (e, 0, 0)),
                pl.BlockSpec((None, D, F), lambda b, e, s: (e, 0, 0)),
                pl.BlockSpec((None, F, D), lambda b, e, s: (e, 0, 0)),
            ],
            out_specs=pl.BlockSpec((None, None, rows, D), lambda b, e, s: (b, e, 0, 0)),
            scratch_shapes=[pltpu.VMEM((rows, D), f32), pltpu.VMEM((rows, 1), f32)],
        ),
        out_shape=jax.ShapeDtypeStruct((B, E, rows, D), bf16),
        compiler_params=_params("arbitrary", "arbitrary"),
        name="moe_ffn",
    )(bnd, h2, cp, aff, wg, wu, wd)


def _combine_kernel(n_e, rows, bnd_ref, y_ref, cpt_ref, expand_ref, o_ref):
    b, k = pl.program_id(0), pl.program_id(1)
    cpt = cpt_ref[...]
    hi_part = jnp.floor(cpt * (1.0 / 32.0))
    lo_part = cpt - 32.0 * hi_part
    expand = expand_ref[...]
    rank = 32.0 * _mm(hi_part.astype(bf16), expand) + _mm(lo_part.astype(bf16), expand)
    lane = lax.broadcasted_iota(jnp.int32, (1, n_e * MOE_WIN), 1)
    lane_e = lax.div(lane, MOE_WIN)
    starts = []
    tgt = (lane - lane_e * MOE_WIN + 1).astype(f32)
    for e in range(n_e):
        r0 = lax.div(bnd_ref[(b * n_e + e) * LANES + k], 16) * 16
        starts.append(r0)
        tgt = tgt + jnp.where(lane_e == e, r0.astype(f32), 0.0)
    onehot = jnp.where(rank == tgt, 1.0, 0.0).astype(bf16)
    ycat = jnp.concatenate(
        [y_ref[pl.ds(pl.multiple_of(e * rows + starts[e], 16), MOE_WIN), :] for e in range(n_e)], axis=0)
    o_ref[...] = _mm(onehot, ycat)

    lane_w = lax.broadcasted_iota(jnp.int32, (1, MOE_WIN), 1)
    for e in range(n_e):
        lo, hi = bnd_ref[(b * n_e + e) * LANES + k], bnd_ref[(b * n_e + e) * LANES + k + 1]
        n_win = _window_count(lo, hi, starts[e])

        @pl.when(n_win > 1)
        def _(e=e, n_win=n_win):
            rank_e = rank[:, e * MOE_WIN:(e + 1) * MOE_WIN]

            def extra(w, c):
                r = pl.multiple_of(starts[e] + w * MOE_WIN, 16)
                hit = rank_e == (lane_w + (r + 1)).astype(f32)
                o_ref[...] += _mm(jnp.where(hit, 1.0, 0.0).astype(bf16),
                                  y_ref[pl.ds(pl.multiple_of(e * rows + r, 16), MOE_WIN), :])
                return c

            lax.fori_loop(1, n_win, extra, 0)


def _combine(y, cp, bnd):
    B, E, rows, D = y.shape
    T = cp.shape[2]
    nc = T // MOE_CHUNK
    expand = jnp.asarray(np.kron(np.eye(E), np.ones((1, MOE_WIN))), bf16)
    return pl.pallas_call(
        functools.partial(_combine_kernel, E, rows),
        grid_spec=pltpu.PrefetchScalarGridSpec(
            num_scalar_prefetch=1,
            grid=(B, nc),
            in_specs=[
                pl.BlockSpec((None, E * rows, D), lambda b, k, s: (b, 0, 0), pipeline_mode=pl.Buffered(1)),
                pl.BlockSpec((None, MOE_CHUNK, E), lambda b, k, s: (b, k, 0)),
                pl.BlockSpec(expand.shape, lambda b, k, s: (0, 0)),
            ],
            out_specs=pl.BlockSpec((None, MOE_CHUNK, D), lambda b, k, s: (b, k, 0)),
        ),
        out_shape=jax.ShapeDtypeStruct((B, T, D), f32),
        compiler_params=_params("arbitrary", "arbitrary"),
        name="moe_combine",
    )(bnd, y.reshape(B, E * rows, D), cp.transpose(0, 2, 1), expand)


def _final_kernel(x_ref, moe_ref, mod_ref, o_ref):
    o_ref[...] = x_ref[...] + mod_ref[5:6, :] * moe_ref[...]


def _final(x, moe, mod, S):
    B, T, D = x.shape
    return pl.pallas_call(
        _final_kernel,
        grid=(B, S // TT),
        in_specs=[
            pl.BlockSpec((None, TT, D), lambda b, i: (b, i, 0)),
            pl.BlockSpec((None, TT, D), lambda b, i: (b, i, 0)),
            pl.BlockSpec((None, None, 6, D), lambda b, i: (b, 0, 0, 0)),
        ],
        out_specs=pl.BlockSpec((None, TT, D), lambda b, i: (b, i, 0)),
        out_shape=jax.ShapeDtypeStruct((B, S, D), f32),
        compiler_params=_params("arbitrary", "arbitrary"),
        name="final_residual",
    )(x, moe, mod)


def _rope_tables(S, L):
    t = jnp.arange(S)
    row, col = t // GRID_W, t % GRID_W
    f = 16
    inv = ROPE_BASE ** (-jnp.arange(f, dtype=f32) / f)
    ar = row.astype(f32)[:, None] * inv
    ac = col.astype(f32)[:, None] * inv
    cos = jnp.concatenate([jnp.cos(ar), jnp.cos(ar), jnp.cos(ac), jnp.cos(ac)], axis=1)
    sin = jnp.concatenate([-jnp.sin(ar), jnp.sin(ar), -jnp.sin(ac), jnp.sin(ac)], axis=1)
    cos = jnp.concatenate([jnp.tile(cos, (1, 2)), jnp.ones((L, LANES), f32)], axis=0)
    sin = jnp.concatenate([jnp.tile(sin, (1, 2)), jnp.zeros((L, LANES), f32)], axis=0)
    return cos, sin


def _pad_heads(w, n_heads, width, padded):
    lead = w.shape[:-1]
    w = w.reshape(lead + (n_heads, width))
    w = jnp.pad(w, [(0, 0)] * len(lead) + [(0, 0), (0, padded - width)])
    return w.reshape(lead + (n_heads * padded,))


def _layer_weights(l, D, w_in, norm2_g, na_q_g, na_k_g, mla_q_norm, mla_w_uq, mla_kv_norm, mla_w_ukv,
                   mla_q_g, mla_k_g, swa_q_g, swa_k_g, w_na_o, w_mla_o, w_swa_o, w_o, router,
                   w_gate, w_up, w_down):
    o = np.cumsum((0, NA_W, NA_W, NA_W, MLA_Q_RANK, MLA_KV_RANK, MLA_ROPE, SWA_W, SWA_KV_W, SWA_KV_W))
    o_naq, o_nak, o_nav, o_cq, o_ckv, o_kr, o_swq, o_swk, o_swv, o_gl = (int(v) for v in o)
    wi = w_in[l]
    swq = wi[:, o_swq:o_swq + SWA_W].reshape(D, SWA_HEADS, SWA_DIM)[:, SWA_HEAD_ORDER, :].reshape(D, SWA_W)
    kr = jnp.pad(wi[:, o_kr:o_kr + MLA_ROPE], ((0, 0), (0, LANES - MLA_ROPE)))
    w_all = jnp.concatenate([
        wi[:, o_naq:o_naq + 3 * NA_W], swq, wi[:, o_cq:o_cq + MLA_Q_RANK], wi[:, o_ckv:o_ckv + MLA_KV_RANK], kr,
        wi[:, o_swk:o_swk + 2 * SWA_KV_W], wi[:, o_gl:]], axis=1).astype(bf16)
    ukv = mla_w_ukv[l].reshape(MLA_KV_RANK, MLA_HEADS, MLA_NOPE + MLA_V)
    swa_o = w_swa_o[l].reshape(SWA_HEADS, SWA_DIM, D)[SWA_HEAD_ORDER, :, :].reshape(SWA_W, D)
    r_hi = router[l].astype(bf16)
    r_lo = (router[l] - r_hi.astype(f32)).astype(bf16)
    return dict(
        w_all=w_all,
        naq_g=jnp.tile(na_q_g[l], NA_HEADS)[None], nak_g=jnp.tile(na_k_g[l], NA_HEADS)[None],
        swq_g=jnp.tile(swa_q_g[l], SWA_HEADS)[None], swk_g=jnp.tile(swa_k_g[l], SWA_KV_HEADS)[None],
        qn=mla_q_norm[l][None], kvn=mla_kv_norm[l][None],
        wuq=_pad_heads(mla_w_uq[l], MLA_HEADS, MLA_QK, MLA_PAD).astype(bf16),
        wuk=ukv[:, :, :MLA_NOPE].reshape(MLA_KV_RANK, MLA_HEADS * MLA_NOPE).astype(bf16),
        wuv=ukv[:, :, MLA_NOPE:].reshape(MLA_KV_RANK, MLA_W).astype(bf16),
        qg=jnp.tile(jnp.pad(mla_q_g[l], (0, MLA_PAD - MLA_QK)), MLA_HEADS)[None],
        kg=jnp.tile(jnp.pad(mla_k_g[l], (0, MLA_PAD - MLA_QK)), MLA_HEADS)[None],
        wna=w_na_o[l].astype(bf16), wmla=w_mla_o[l].astype(bf16), wswa=swa_o.astype(bf16),
        wo=w_o[l].astype(bf16), g2=norm2_g[l][None], router=jnp.concatenate([r_hi, r_lo], axis=1),
        wg=w_gate[l].astype(bf16), wu=w_up[l].astype(bf16), wd=w_down[l].astype(bf16),
    )


def kernel(x, c, ctx, c_ctx, norm1_g, norm2_g, w_ada, b_ada, w_in, na_q_g, na_k_g, na_rpb, mla_q_norm, mla_w_uq, mla_kv_norm, mla_w_ukv, mla_q_g, mla_k_g, swa_q_g, swa_k_g, swa_sink, w_na_o, w_mla_o, w_swa_o, w_o, router, w_gate, w_up, w_down):
    B, S, D = x.shape
    L = ctx.shape[1]
    depth = w_in.shape[0]
    E = router.shape[2]
    T = S + L
    n_lat = S // TT
    rows = S // GRID_W
    assert S % TT == 0 and L % TM == 0 and L <= TT and rows >= NA_BAND_ROWS and S >= SWA_KEYS and B < 16
    cap_s = CAPACITY_FACTOR * S // E
    cap_l = CAPACITY_FACTOR * L // E

    cpad = jnp.zeros((16, D), f32).at[:B].set(c).at[B].set(c_ctx)
    mod = _ada(cpad, w_ada, b_ada).reshape(depth, 16, 6, D)
    mod = jnp.stack([mod[:, :B], jnp.broadcast_to(mod[:, B:B + 1], (depth, B, 6, D))], axis=2)

    cos, sin = _rope_tables(S, L)
    gmat = jnp.asarray(np.kron(np.eye(NA_HEADS), np.full((NA_DIM, NA_DIM), 1.0 / NA_DIM)), bf16)

    xs = jnp.concatenate([x, ctx], axis=1)
    moe = None
    for l in range(depth):
        lw = _layer_weights(l, D, w_in, norm2_g, na_q_g, na_k_g, mla_q_norm, mla_w_uq, mla_kv_norm, mla_w_ukv,
                            mla_q_g, mla_k_g, swa_q_g, swa_k_g, w_na_o, w_mla_o, w_swa_o, w_o, router,
                            w_gate, w_up, w_down)
        modp = mod[l - 1] if l > 0 else None
        xs, proj, gl = _in_proj(xs, moe, modp, mod[l], norm1_g[l][None], lw["w_all"], n_lat)
        qka, mq, mk, mv = _prep(proj, cos, sin, gmat, lw)
        ona = _na(qka, proj, _na_bias(na_rpb[l], rows), S)
        omla = _mla(mq, mk, mv, S)
        oswa = _swa(qka, proj, swa_sink[l], S)
        xs, h2, aff = _post(ona, omla, oswa, gl, xs, mod[l], lw, n_lat)
        aff = aff.transpose(0, 2, 1)
        cp, bnd = _topk(aff, S, cap_s, cap_l)
        bnd = bnd.astype(jnp.int32).reshape(-1)
        y = _moe(h2, cp, bnd, aff, lw["wg"], lw["wu"], lw["wd"], cap_s + cap_l)
        moe = _combine(y, cp, bnd)
    return _final(xs, moe, mod[depth - 1], S)
```

```python
import functools

import numpy as np
import jax
import jax.numpy as jnp
from jax import lax
from jax.experimental import pallas as pl
from jax.experimental.pallas import tpu as pltpu

GRID_W = 64
NA_HEADS, NA_DIM, NA_WIN_ROWS, NA_WIN_COLS = 6, 64, 8, 16
MLA_HEADS, MLA_Q_RANK, MLA_KV_RANK, MLA_NOPE, MLA_ROPE, MLA_V = 4, 256, 128, 128, 64, 128
MLA_QK = MLA_NOPE + MLA_ROPE
MLA_PAD = 256
SWA_HEADS, SWA_KV_HEADS, SWA_DIM, SWA_WINDOW = 6, 2, 64, 128
CAPACITY_FACTOR = 2
N_BRANCH = 3
ROPE_BASE = 10000.0
EPS = 1e-6

NA_W = NA_HEADS * NA_DIM
SWA_W = SWA_HEADS * SWA_DIM
SWA_KV_W = SWA_KV_HEADS * SWA_DIM
MLA_W = MLA_HEADS * MLA_V

LANES = 128
TM = 256
TT = 512
MLA_TQ = 1024
MLA_TK = 1024
LOG2E = 1.4426950408889634
NA_TILE_ROWS = TM // GRID_W
NA_BAND_ROWS = NA_TILE_ROWS + NA_WIN_ROWS - 1
NA_BAND = NA_BAND_ROWS * GRID_W
SWA_KEYS = TM + 2 * SWA_WINDOW
MOE_CHUNK = 256
MOE_WIN = 64
NEG = -1e30
VMEM_LIMIT = 56 * 1024 * 1024

P_NAQ, P_NAK, P_NAV, P_SWQ = 0, 384, 768, 1152
P_CQ, P_CKV, P_KR, P_SWK, P_SWV = 1536, 1792, 1920, 2048, 2176
PROJ_W = 2304
N_CHUNK = 768
A_NAQ, A_NAK, A_SWQ, A_SWK = 0, 384, 768, 1152
QKA_W = 1280
SWA_HEAD_ORDER = (0, 3, 1, 4, 2, 5)

bf16 = jnp.bfloat16
f32 = jnp.float32


def _mm(a, b):
    return jnp.dot(a, b, preferred_element_type=f32)


def _nt(a, b):
    return lax.dot_general(a, b, (((1,), (1,)), ((), ())), preferred_element_type=f32)


def _params(*sem):
    return pltpu.CompilerParams(dimension_semantics=sem, vmem_limit_bytes=VMEM_LIMIT)


def _ada_kernel(c_ref, w_ref, b_ref, o_ref):
    a = c_ref[...]
    a = (a * jax.nn.sigmoid(a)).astype(bf16)
    o_ref[...] = _mm(a, w_ref[...].astype(bf16)) + b_ref[...]


def _ada(cpad, w_ada, b_ada):
    L, D, N = w_ada.shape
    tn = 1536
    return pl.pallas_call(
        _ada_kernel,
        grid=(L, N // tn),
        in_specs=[
            pl.BlockSpec((16, D), lambda l, j: (0, 0)),
            pl.BlockSpec((None, D, tn), lambda l, j: (l, 0, j)),
            pl.BlockSpec((None, 1, tn), lambda l, j: (l, 0, j)),
        ],
        out_specs=pl.BlockSpec((None, 16, tn), lambda l, j: (l, 0, j)),
        out_shape=jax.ShapeDtypeStruct((L, 16, N), f32),
        compiler_params=_params("arbitrary", "arbitrary"),
        name="ada",
    )(cpad, w_ada, b_ada.reshape(L, 1, N))


def _in_proj_kernel(has_moe, n_gl_chunks, *refs):
    if has_moe:
        x_ref, moe_ref, modp_ref, mod_ref, g_ref, w_ref, xo_ref, proj_ref, gl_ref = refs
        x = x_ref[...] + modp_ref[5:6, :] * moe_ref[...]
        xo_ref[...] = x
    else:
        x_ref, mod_ref, g_ref, w_ref, proj_ref, gl_ref = refs
        x = x_ref[...]
    ms = jnp.mean(x * x, axis=-1, keepdims=True)
    h = x * lax.rsqrt(ms + EPS) * g_ref[...]
    h = (h * (1.0 + mod_ref[1:2, :]) + mod_ref[0:1, :]).astype(bf16)
    for c in range(PROJ_W // N_CHUNK):
        sl = slice(c * N_CHUNK, (c + 1) * N_CHUNK)
        proj_ref[:, sl] = _mm(h, w_ref[:, sl]).astype(bf16)
    for c in range(n_gl_chunks):
        sl = slice(c * N_CHUNK, (c + 1) * N_CHUNK)
        gl_ref[:, sl] = _mm(h, w_ref[:, PROJ_W + c * N_CHUNK:PROJ_W + (c + 1) * N_CHUNK]).astype(bf16)


def _mod_spec(n_lat, D):
    return pl.BlockSpec((None, None, 6, D), lambda b, i: (b, jnp.where(i >= n_lat, 1, 0), 0, 0))


def _in_proj(x, moe, modp, mod, g1, w, n_lat):
    B, T, D = x.shape
    nt = pl.cdiv(T, TT)
    has_moe = moe is not None
    n_gl_chunks = N_BRANCH * D // N_CHUNK
    tok = pl.BlockSpec((None, TT, D), lambda b, i: (b, i, 0))
    in_specs = [tok]
    args = [x]
    if has_moe:
        in_specs += [tok, _mod_spec(n_lat, D)]
        args += [moe, modp]
    in_specs += [
        _mod_spec(n_lat, D),
        pl.BlockSpec((1, D), lambda b, i: (0, 0)),
        pl.BlockSpec(w.shape, lambda b, i: (0, 0), pipeline_mode=pl.Buffered(1)),
    ]
    args += [mod, g1, w]
    out_specs = [
        pl.BlockSpec((None, TT, PROJ_W), lambda b, i: (b, i, 0)),
        pl.BlockSpec((None, TT, N_BRANCH * D), lambda b, i: (b, i, 0)),
    ]
    out_shape = [
        jax.ShapeDtypeStruct((B, T, PROJ_W), bf16),
        jax.ShapeDtypeStruct((B, T, N_BRANCH * D), bf16),
    ]
    if has_moe:
        out_specs = [tok] + out_specs
        out_shape = [jax.ShapeDtypeStruct((B, T, D), f32)] + out_shape
    res = pl.pallas_call(
        functools.partial(_in_proj_kernel, has_moe, n_gl_chunks),
        grid=(B, nt),
        in_specs=in_specs,
        out_specs=out_specs,
        out_shape=out_shape,
        compiler_params=_params("arbitrary", "arbitrary"),
        name="in_proj",
    )(*args)
    if has_moe:
        return res
    return [x] + list(res)


def _prep_kernel(proj_ref, cos_ref, sin_ref, gmat_ref, naq_g, nak_g, swq_g, swk_g,
                 qn_ref, wuq_ref, qg_ref, kvn_ref, wuk_ref, wuv_ref, kg_ref,
                 qka_ref, mq_ref, mk_ref, mv_ref):
    cos = cos_ref[...]
    sin = sin_ref[...]
    lane = lax.broadcasted_iota(jnp.int32, cos.shape, 1)
    first = (lane & 16) == 0

    def rope(x):
        partner = jnp.where(first, pltpu.roll(x, LANES - 16, 1), pltpu.roll(x, 16, 1))
        return x * cos + partner * sin

    def headnorm(x, gain):
        w = x.shape[1]
        ms = _mm((x * x).astype(bf16), gmat_ref[:w, :w])
        return x * lax.rsqrt(ms + EPS) * gain

    def rmsnorm(x, gain):
        ms = jnp.mean(x * x, axis=-1, keepdims=True)
        return x * lax.rsqrt(ms + EPS) * gain

    na_scale = NA_DIM ** -0.5
    naq = headnorm(proj_ref[:, P_NAQ:P_NAQ + NA_W].astype(f32), naq_g[...])
    qka_ref[:, A_NAQ:A_NAQ + NA_W] = (naq * na_scale).astype(bf16)
    nak = headnorm(proj_ref[:, P_NAK:P_NAK + NA_W].astype(f32), nak_g[...])
    qka_ref[:, A_NAK:A_NAK + NA_W] = nak.astype(bf16)

    sw_scale = SWA_DIM ** -0.5
    swq = headnorm(proj_ref[:, P_SWQ:P_SWQ + SWA_W].astype(f32), swq_g[...])
    for p in range(SWA_W // LANES):
        sl = slice(p * LANES, (p + 1) * LANES)
        qka_ref[:, A_SWQ + p * LANES:A_SWQ + (p + 1) * LANES] = (rope(swq[:, sl]) * sw_scale).astype(bf16)
    swk = headnorm(proj_ref[:, P_SWK:P_SWK + SWA_KV_W].astype(f32), swk_g[...])
    qka_ref[:, A_SWK:A_SWK + SWA_KV_W] = rope(swk).astype(bf16)

    mla_scale = MLA_QK ** -0.5 * LOG2E
    cq = rmsnorm(proj_ref[:, P_CQ:P_CQ + MLA_Q_RANK].astype(f32), qn_ref[...]).astype(bf16)
    q = _mm(cq, wuq_ref[...])
    ckv = rmsnorm(proj_ref[:, P_CKV:P_CKV + MLA_KV_RANK].astype(f32), kvn_ref[...]).astype(bf16)
    kn = _mm(ckv, wuk_ref[...])
    v = _mm(ckv, wuv_ref[...]).astype(bf16)
    ones_col = jnp.where(lax.broadcasted_iota(jnp.int32, (v.shape[0], LANES), 1) == 0, 1.0, 0.0).astype(bf16)
    for h in range(MLA_HEADS):
        mv_ref[:, h * MLA_PAD:h * MLA_PAD + MLA_V] = v[:, h * MLA_V:(h + 1) * MLA_V]
        mv_ref[:, h * MLA_PAD + MLA_V:(h + 1) * MLA_PAD] = ones_col
    kr = proj_ref[:, P_KR:P_KR + LANES].astype(f32)
    kr_ss = jnp.sum(kr * kr, axis=-1, keepdims=True)
    for h in range(MLA_HEADS):
        o = h * MLA_PAD
        qh = q[:, o:o + MLA_PAD]
        r = lax.rsqrt(jnp.sum(qh * qh, axis=-1, keepdims=True) * (1.0 / MLA_QK) + EPS)
        qh = qh * r * qg_ref[:, o:o + MLA_PAD]
        mq_ref[:, o:o + LANES] = (qh[:, :LANES] * mla_scale).astype(bf16)
        mq_ref[:, o + LANES:o + MLA_PAD] = (rope(qh[:, LANES:]) * mla_scale).astype(bf16)
        kh = kn[:, h * MLA_NOPE:(h + 1) * MLA_NOPE]
        r = lax.rsqrt((jnp.sum(kh * kh, axis=-1, keepdims=True) + kr_ss) * (1.0 / MLA_QK) + EPS)
        mk_ref[:, o:o + LANES] = (kh * r * kg_ref[:, o:o + LANES]).astype(bf16)
        mk_ref[:, o + LANES:o + MLA_PAD] = rope(kr * r * kg_ref[:, o + LANES:o + MLA_PAD]).astype(bf16)


def _prep(proj, cos, sin, gmat, lw):
    B, T, _ = proj.shape
    nt = pl.cdiv(T, TT)

    def const(a):
        return pl.BlockSpec(a.shape, lambda b, i: (0,) * a.ndim)

    consts = [gmat, lw["naq_g"], lw["nak_g"], lw["swq_g"], lw["swk_g"], lw["qn"], lw["wuq"], lw["qg"],
              lw["kvn"], lw["wuk"], lw["wuv"], lw["kg"]]
    widths = (QKA_W, MLA_HEADS * MLA_PAD, MLA_HEADS * MLA_PAD, MLA_HEADS * MLA_PAD)
    return pl.pallas_call(
        _prep_kernel,
        grid=(B, nt),
        in_specs=[
            pl.BlockSpec((None, TT, PROJ_W), lambda b, i: (b, i, 0)),
            pl.BlockSpec((TT, LANES), lambda b, i: (i, 0)),
            pl.BlockSpec((TT, LANES), lambda b, i: (i, 0)),
        ] + [const(a) for a in consts],
        out_specs=[pl.BlockSpec((None, TT, w), lambda b, i: (b, i, 0)) for w in widths],
        out_shape=[jax.ShapeDtypeStruct((B, T, w), bf16) for w in widths],
        compiler_params=_params("arbitrary", "arbitrary"),
        name="prep",
    )(proj, cos, sin, *consts)


def _half_masks():
    lane = lax.broadcasted_iota(jnp.int32, (1, LANES), 1)
    lo = jnp.where(lane < 64, 1.0, 0.0).astype(bf16)
    return lo, (1.0 - lo.astype(f32)).astype(bf16)


def _pair_attend(qp, parts, masks, sinks=None):
    n_q = qp.shape[0]
    q2 = jnp.concatenate([qp * masks[0], qp * masks[1]], axis=0)
    scores = []
    for k, _, bias in parts:
        s = _nt(q2, k)
        scores.append(s if bias is None else s + bias)
    m = functools.reduce(jnp.maximum, [jnp.max(s, axis=-1, keepdims=True) for s in scores])
    if sinks is not None:
        first = lax.broadcasted_iota(jnp.int32, (2 * n_q, 1), 0) < n_q
        sink = jnp.where(first, sinks[0], sinks[1])
        m = jnp.maximum(m, sink)
    ps = [jnp.exp(s - m) for s in scores]
    l = functools.reduce(jnp.add, [jnp.sum(p, axis=-1, keepdims=True) for p in ps])
    if sinks is not None:
        l = l + jnp.exp(sink - m)
    o = functools.reduce(jnp.add, [_mm(p.astype(bf16), v) for p, (_, v, _) in zip(ps, parts)]) / l
    lane = lax.broadcasted_iota(jnp.int32, (1, LANES), 1)
    return jnp.where(lane < 64, o[:n_q], o[n_q:])


def _na_kernel(n_lat, S, rows, q_ref, k_ref, v_ref, bias_ref, o_ref):
    i = pl.program_id(1)
    masks = _half_masks()
    T = k_ref.shape[0]

    @pl.when(i < n_lat)
    def _():
        start = GRID_W * jnp.clip(NA_TILE_ROWS * i - NA_WIN_ROWS // 2, 0, rows - NA_BAND_ROWS)
        start = pl.multiple_of(start, GRID_W)
        for p in range(NA_W // LANES):
            sl = slice(p * LANES, (p + 1) * LANES)
            band = (k_ref[pl.ds(start, NA_BAND), sl], v_ref[pl.ds(start, NA_BAND), sl],
                    bias_ref[2 * p:2 * p + 2].reshape(2 * TM, NA_BAND))
            ctx = (k_ref[S:T, sl], v_ref[S:T, sl], None)
            o_ref[:, sl] = _pair_attend(q_ref[:, sl], [band, ctx], masks).astype(bf16)

    @pl.when(i >= n_lat)
    def _():
        for p in range(NA_W // LANES):
            sl = slice(p * LANES, (p + 1) * LANES)
            ctx = (k_ref[S:T, sl], v_ref[S:T, sl], None)
            o_ref[:, sl] = _pair_attend(q_ref[:, sl], [ctx], masks).astype(bf16)


def _na(qka, proj, bias, S):
    B, T, _ = qka.shape
    nt, n_lat, rows = T // TM, S // TM, S // GRID_W

    def bias_idx(b, i):
        return (jnp.where(i == 0, 0, jnp.where(i >= n_lat - 1, 2, 1)), 0, 0, 0)

    return pl.pallas_call(
        functools.partial(_na_kernel, n_lat, S, rows),
        grid=(B, nt),
        in_specs=[
            pl.BlockSpec((None, TM, NA_W), lambda b, i: (b, i, A_NAQ // NA_W)),
            pl.BlockSpec((None, T, NA_W), lambda b, i: (b, 0, A_NAK // NA_W)),
            pl.BlockSpec((None, T, NA_W), lambda b, i: (b, 0, P_NAV // NA_W)),
            pl.BlockSpec((None, NA_HEADS, TM, NA_BAND), bias_idx),
        ],
        out_specs=pl.BlockSpec((None, TM, NA_W), lambda b, i: (b, i, 0)),
        out_shape=jax.ShapeDtypeStruct((B, T, NA_W), bf16),
        compiler_params=_params("arbitrary", "arbitrary"),
        name="na_attn",
    )(qka, qka, proj, bias)


def _na_bias(rpb, rows):
    n_tiles = rows // NA_TILE_ROWS
    col = np.arange(GRID_W)
    c0 = np.clip(col - NA_WIN_COLS // 2, 0, GRID_W - NA_WIN_COLS)
    col_ok = (col[None, :] >= c0[:, None]) & (col[None, :] < c0[:, None] + NA_WIN_COLS)
    dc = np.clip(col[None, :] - col[:, None] + NA_WIN_COLS - 1, 0, 2 * NA_WIN_COLS - 2)
    tz = jnp.where(col_ok, rpb[:, :, dc], NEG)
    dr = np.zeros((3, NA_TILE_ROWS, NA_BAND_ROWS), np.int32)
    ok = np.zeros((3, NA_TILE_ROWS, NA_BAND_ROWS), bool)
    for ty, rt in enumerate((0, 1, n_tiles - 1)):
        bs = int(np.clip(NA_TILE_ROWS * rt - NA_WIN_ROWS // 2, 0, rows - NA_BAND_ROWS))
        for qr in range(NA_TILE_ROWS):
            r = NA_TILE_ROWS * rt + qr
            s_r = int(np.clip(r - NA_WIN_ROWS // 2, 0, rows - NA_WIN_ROWS))
            for w in range(NA_BAND_ROWS):
                kr = bs + w
                ok[ty, qr, w] = s_r <= kr < s_r + NA_WIN_ROWS
                dr[ty, qr, w] = np.clip(kr - r + NA_WIN_ROWS - 1, 0, 2 * NA_WIN_ROWS - 2)
    blk = tz[:, dr]
    blk = jnp.where(ok[None, :, :, :, None, None], blk, NEG)
    blk = blk.transpose(1, 0, 2, 4, 3, 5)
    return blk.reshape(3, NA_HEADS, TM, NA_BAND)


def _mla_kernel(n_q, S, q_ref, k_ref, v_ref, o_ref):
    i = pl.program_id(2)
    T = k_ref.shape[0]

    def attend(q, bounds):
        m_run = acc = None
        for lo, hi in bounds:
            s = _nt(q, k_ref[lo:hi, :])
            m_new = jnp.max(s, axis=-1, keepdims=True)
            if m_run is not None:
                m_new = jnp.maximum(m_run, m_new)
            pv = _mm(jnp.exp2(s - m_new).astype(bf16), v_ref[lo:hi, :])
            acc = pv if acc is None else acc * jnp.exp2(m_run - m_new) + pv
            m_run = m_new
        return (acc[:, :MLA_V] / acc[:, MLA_V:MLA_V + 1]).astype(bf16)

    @pl.when(i < n_q)
    def _():
        chunks = [(lo, lo + MLA_TK) for lo in range(0, S, MLA_TK)] + [(S, T)]
        o_ref[...] = attend(q_ref[...], chunks)

    @pl.when(i >= n_q)
    def _():
        o_ref[0:T - S, :] = attend(q_ref[0:T - S, :], [(S, T)])


def _mla(mq, mk, mv, S):
    B, T, _ = mq.shape
    n_q = S // MLA_TQ
    assert S % MLA_TQ == 0 and S % MLA_TK == 0 and T - S <= MLA_TQ
    return pl.pallas_call(
        functools.partial(_mla_kernel, n_q, S),
        grid=(B, MLA_HEADS, n_q + 1),
        in_specs=[
            pl.BlockSpec((None, MLA_TQ, MLA_PAD), lambda b, h, i: (b, i, h)),
            pl.BlockSpec((None, T, MLA_PAD), lambda b, h, i: (b, 0, h)),
            pl.BlockSpec((None, T, MLA_PAD), lambda b, h, i: (b, 0, h)),
        ],
        out_specs=pl.BlockSpec((None, MLA_TQ, MLA_V), lambda b, h, i: (b, i, h)),
        out_shape=jax.ShapeDtypeStruct((B, T, MLA_W), bf16),
        compiler_params=_params("arbitrary", "arbitrary", "arbitrary"),
        name="mla_attn",
    )(mq, mk, mv)


def _swa_kernel(n_lat, S, sink_ref, q_ref, k_ref, v_ref, o_ref):
    i = pl.program_id(1)
    masks = _half_masks()
    T = k_ref.shape[0]
    kc, vc = k_ref[S:T, :], v_ref[S:T, :]

    def sinks(p):
        return (sink_ref[SWA_HEAD_ORDER[2 * p]], sink_ref[SWA_HEAD_ORDER[2 * p + 1]])

    @pl.when(i < n_lat)
    def _():
        start = pl.multiple_of(jnp.clip(i * TM - SWA_WINDOW, 0, S - SWA_KEYS), SWA_WINDOW)
        row = lax.broadcasted_iota(jnp.int32, (2 * TM, SWA_KEYS), 0)
        qpos = i * TM + jnp.where(row >= TM, row - TM, row)
        kpos = start + lax.broadcasted_iota(jnp.int32, (2 * TM, SWA_KEYS), 1)
        band = jnp.where(jnp.abs(qpos - kpos) <= SWA_WINDOW, 0.0, NEG)
        kb, vb = k_ref[pl.ds(start, SWA_KEYS), :], v_ref[pl.ds(start, SWA_KEYS), :]
        for p in range(SWA_W // LANES):
            sl = slice(p * LANES, (p + 1) * LANES)
            parts = [(kb, vb, band), (kc, vc, None)]
            o_ref[:, sl] = _pair_attend(q_ref[:, sl], parts, masks, sinks(p)).astype(bf16)

    @pl.when(i >= n_lat)
    def _():
        for p in range(SWA_W // LANES):
            sl = slice(p * LANES, (p + 1) * LANES)
            o_ref[:, sl] = _pair_attend(q_ref[:, sl], [(kc, vc, None)], masks, sinks(p)).astype(bf16)


def _swa(qka, proj, sink, S):
    B, T, _ = qka.shape
    nt, n_lat = T // TM, S // TM
    return pl.pallas_call(
        functools.partial(_swa_kernel, n_lat, S),
        grid=(B, nt),
        in_specs=[
            pl.BlockSpec(memory_space=pltpu.SMEM),
            pl.BlockSpec((None, TM, SWA_W), lambda b, i: (b, i, A_SWQ // SWA_W)),
            pl.BlockSpec((None, T, SWA_KV_W), lambda b, i: (b, 0, A_SWK // SWA_KV_W)),
            pl.BlockSpec((None, T, SWA_KV_W), lambda b, i: (b, 0, P_SWV // SWA_KV_W)),
        ],
        out_specs=pl.BlockSpec((None, TM, SWA_W), lambda b, i: (b, i, 0)),
        out_shape=jax.ShapeDtypeStruct((B, T, SWA_W), bf16),
        compiler_params=_params("arbitrary", "arbitrary"),
        name="swa_attn",
    )(sink, qka, qka, proj)


def _post_kernel(ona_ref, omla_ref, oswa_ref, gl_ref, x_ref, mod_ref, wna_ref, wmla_ref, wswa_ref, wo_ref,
                 g2_ref, rt_ref, xo_ref, h2_ref, aff_ref, affx_ref):
    D = x_ref.shape[1]
    merged = None
    for j, (o_ref, w_ref) in enumerate(((ona_ref, wna_ref), (omla_ref, wmla_ref), (oswa_ref, wswa_ref))):
        y = _mm(o_ref[...], w_ref[...])
        g = 0.5 * jnp.tanh(0.5 * gl_ref[:, j * D:(j + 1) * D].astype(f32)) + 0.5
        merged = g * y if merged is None else merged + g * y
    res = _mm(merged.astype(bf16), wo_ref[...])
    x = x_ref[...] + mod_ref[2:3, :] * res
    xo_ref[...] = x
    ms = jnp.mean(x * x, axis=-1, keepdims=True)
    h2 = x * lax.rsqrt(ms + EPS) * g2_ref[...]
    h2 = h2 * (1.0 + mod_ref[4:5, :]) + mod_ref[3:4, :]
    h_hi = h2.astype(bf16)
    h2_ref[...] = h_hi
    h_lo = (h2 - h_hi.astype(f32)).astype(bf16)
    parts = _mm(h_hi, rt_ref[...]) + _mm(h_lo, rt_ref[...])
    n_e = aff_ref.shape[1]
    logits = parts[:, :n_e] + parts[:, n_e:]
    e = jnp.exp(logits - jnp.max(logits, axis=1, keepdims=True))
    aff = e / jnp.sum(e, axis=1, keepdims=True)
    aff_ref[...] = aff
    p0 = aff.astype(bf16)
    r1 = aff - p0.astype(f32)
    p1 = r1.astype(bf16)
    p2 = (r1 - p1.astype(f32)).astype(bf16)
    pad = jnp.zeros((aff.shape[0], LANES - 3 * n_e), bf16)
    affx_ref[...] = jnp.concatenate([p0, p1, p2, pad], axis=1)


def _post(ona, omla, oswa, gl, x, mod, lw, n_lat):
    B, T, D = x.shape
    nt = pl.cdiv(T, TT)
    E = lw["router"].shape[1] // 2

    def tok(w):
        return pl.BlockSpec((None, TT, w), lambda b, i: (b, i, 0))

    def const(a):
        return pl.BlockSpec(a.shape, lambda b, i: (0,) * a.ndim)

    consts = [lw["wna"], lw["wmla"], lw["wswa"], lw["wo"], lw["g2"], lw["router"]]
    return pl.pallas_call(
        _post_kernel,
        grid=(B, nt),
        in_specs=[tok(NA_W), tok(MLA_W), tok(SWA_W), tok(N_BRANCH * D), tok(D), _mod_spec(n_lat, D)]
        + [const(a) for a in consts],
        out_specs=[tok(D), tok(D), tok(E), tok(LANES)],
        out_shape=[jax.ShapeDtypeStruct((B, T, D), f32), jax.ShapeDtypeStruct((B, T, D), bf16),
                   jax.ShapeDtypeStruct((B, T, E), f32), jax.ShapeDtypeStruct((B, T, LANES), bf16)],
        compiler_params=_params("arbitrary", "arbitrary"),
        name="post_attn",
    )(ona, omla, oswa, gl, x, mod, *consts)


def _lane_cumsum(mask, tri):
    E, n = mask.shape
    carry = jnp.zeros((E, 1), f32)
    outs = []
    for k in range(n // LANES):
        w = _mm(mask[:, k * LANES:(k + 1) * LANES].astype(bf16), tri) + carry
        outs.append(w)
        carry = w[:, LANES - 1:LANES]
    return jnp.concatenate(outs, axis=1)


def _select_slots(aff, cap, base, tri):
    bits = lax.bitcast_convert_type(aff, jnp.int32)
    thr = jnp.zeros((aff.shape[0], 1), jnp.int32)
    for bit in range(30, -1, -1):
        cand = thr | (1 << bit)
        cnt = jnp.sum(jnp.where(bits >= cand, 1.0, 0.0), axis=1, keepdims=True)
        thr = jnp.where(cnt >= cap, cand, thr)
    gt = jnp.where(bits > thr, 1.0, 0.0)
    eq = jnp.where(bits == thr, 1.0, 0.0)
    need = cap - jnp.sum(gt, axis=1, keepdims=True)
    sel = jnp.maximum(gt, jnp.where(_lane_cumsum(eq, tri) <= need, eq, 0.0))
    return jnp.where(sel > 0.0, _lane_cumsum(sel, tri) + base, 0.0)


def _topk_kernel(S, cap_s, cap_l, aff_ref, tri_ref, before_ref, cp_ref, bnd_ref):
    T = aff_ref.shape[1]
    tri = tri_ref[...]
    cp_ref[:, 0:S] = _select_slots(aff_ref[:, 0:S], cap_s, 0.0, tri)
    cp_ref[:, S:T] = _select_slots(aff_ref[:, S:T], cap_l, float(cap_s), tri)
    bnd_ref[...] = _mm(jnp.where(cp_ref[...] > 0.0, 1.0, 0.0).astype(bf16), before_ref[...])


def _topk(aff, S, cap_s, cap_l):
    B, E, T = aff.shape
    tri = jnp.asarray(np.triu(np.ones((LANES, LANES), np.float32)), bf16)
    before = jnp.asarray(np.arange(T)[:, None] < MOE_CHUNK * np.arange(LANES)[None, :], bf16)
    return pl.pallas_call(
        functools.partial(_topk_kernel, S, cap_s, cap_l),
        grid=(B,),
        in_specs=[pl.BlockSpec((None, E, T), lambda b: (b, 0, 0)),
                  pl.BlockSpec((LANES, LANES), lambda b: (0, 0)),
                  pl.BlockSpec((T, LANES), lambda b: (0, 0))],
        out_specs=[pl.BlockSpec((None, E, T), lambda b: (b, 0, 0)),
                   pl.BlockSpec((None, E, LANES), lambda b: (b, 0, 0))],
        out_shape=[jax.ShapeDtypeStruct((B, E, T), f32), jax.ShapeDtypeStruct((B, E, LANES), f32)],
        compiler_params=_params("arbitrary"),
        name="expert_select",
    )(aff, tri, before)


def _window_count(lo, hi, r0):
    return jnp.where(hi > lo, lax.div(hi - r0 + (MOE_WIN - 1), MOE_WIN), 0)


def _gather_kernel(n_e, rows, bnd_ref, h2_ref, affx_ref, cp_ref, xg_ref):
    b, k = pl.program_id(0), pl.program_id(1)
    D = h2_ref.shape[1]

    @pl.when(k == 0)
    def _():
        xg_ref[...] = jnp.zeros_like(xg_ref)

    ex = lax.broadcasted_iota(jnp.int32, (n_e, 1), 0)
    starts = []
    start_col = jnp.zeros((n_e, 1), f32)
    for e in range(n_e):
        r0 = lax.div(bnd_ref[(b * n_e + e) * LANES + k], 16) * 16
        starts.append(r0)
        start_col = start_col + jnp.where(ex == e, r0.astype(f32), 0.0)
    rel = cp_ref[...] - start_col
    rel_rows = jnp.broadcast_to(rel[:, None, :], (n_e, MOE_WIN, MOE_CHUNK)).reshape(n_e * MOE_WIN, MOE_CHUNK)
    j1 = lax.broadcasted_iota(jnp.int32, (n_e, MOE_WIN, 1), 1).reshape(n_e * MOE_WIN, 1) + 1
    onehot = jnp.where(rel_rows == j1.astype(f32), 1.0, 0.0).astype(bf16)
    tok, tok_aff = h2_ref[...], affx_ref[...]
    moved = _mm(onehot, tok).astype(bf16)
    moved_aff = _mm(onehot, tok_aff).astype(bf16)
    for e in range(n_e):
        dst = pl.ds(pl.multiple_of(e * rows + starts[e], 16), MOE_WIN)
        xg_ref[dst, 0:D] += moved[e * MOE_WIN:(e + 1) * MOE_WIN]
        xg_ref[dst, D:D + LANES] += moved_aff[e * MOE_WIN:(e + 1) * MOE_WIN]

    sub1 = lax.broadcasted_iota(jnp.int32, (MOE_WIN, 1), 0) + 1
    for e in range(n_e):
        lo, hi = bnd_ref[(b * n_e + e) * LANES + k], bnd_ref[(b * n_e + e) * LANES + k + 1]
        n_win = _window_count(lo, hi, starts[e])

        @pl.when(n_win > 1)
        def _(e=e, n_win=n_win):
            def extra(w, c):
                hit = rel[e:e + 1, :] == (sub1 + w * MOE_WIN).astype(f32)
                oh = jnp.where(hit, 1.0, 0.0).astype(bf16)
                dst = pl.ds(pl.multiple_of(e * rows + starts[e] + w * MOE_WIN, 16), MOE_WIN)
                xg_ref[dst, 0:D] += _mm(oh, tok).astype(bf16)
                xg_ref[dst, D:D + LANES] += _mm(oh, tok_aff).astype(bf16)
                return c

            lax.fori_loop(1, n_win, extra, 0)


def _gather(h2, affx, cp, bnd, n_slots):
    B, T, D = h2.shape
    E = cp.shape[1]
    nc = T // MOE_CHUNK
    rows = n_slots + MOE_WIN
    assert n_slots % 16 == 0 and nc + 1 <= LANES and 3 * E <= LANES
    return pl.pallas_call(
        functools.partial(_gather_kernel, E, rows),
        grid_spec=pltpu.PrefetchScalarGridSpec(
            num_scalar_prefetch=1,
            grid=(B, nc),
            in_specs=[
                pl.BlockSpec((None, MOE_CHUNK, D), lambda b, k, s: (b, k, 0)),
                pl.BlockSpec((None, MOE_CHUNK, LANES), lambda b, k, s: (b, k, 0)),
                pl.BlockSpec((None, E, MOE_CHUNK), lambda b, k, s: (b, 0, k)),
            ],
            out_specs=pl.BlockSpec((None, E * rows, D + LANES), lambda b, k, s: (b, 0, 0),
                                   pipeline_mode=pl.Buffered(1)),
        ),
        out_shape=jax.ShapeDtypeStruct((B, E * rows, D + LANES), bf16),
        compiler_params=_params("arbitrary", "arbitrary"),
        name="moe_gather",
    )(bnd, h2, affx, cp)


def _ffn_kernel(n_slots, n_e, xg_ref, wg_ref, wu_ref, wd_ref, y_ref):
    e = pl.program_id(0)
    D = wg_ref.shape[0]
    xg = xg_ref[0:n_slots, 0:D]
    a = _mm(xg, wg_ref[...])
    u = _mm(xg, wu_ref[...])
    act = (a * jax.nn.sigmoid(a) * u).astype(bf16)
    lane = lax.broadcasted_iota(jnp.int32, (1, LANES), 1)
    mine = jnp.logical_and(lax.rem(lane, n_e) == e, lane < 3 * n_e)
    gate = jnp.sum(jnp.where(mine, xg_ref[0:n_slots, D:D + LANES].astype(f32), 0.0), axis=1, keepdims=True)
    y_ref[0:n_slots, :] = (_mm(act, wd_ref[...]) * gate).astype(bf16)
    y_ref[n_slots:, :] = jnp.zeros((y_ref.shape[0] - n_slots, y_ref.shape[1]), bf16)


def _ffn(xg, wg, wu, wd, n_slots):
    B = xg.shape[0]
    E, D, F = wg.shape
    rows = xg.shape[1] // E
    return pl.pallas_call(
        functools.partial(_ffn_kernel, n_slots, E),
        grid=(E, B),
        in_specs=[
            pl.BlockSpec((None, rows, D + LANES), lambda e, b: (b, e, 0)),
            pl.BlockSpec((None, D, F), lambda e, b: (e, 0, 0)),
            pl.BlockSpec((None, D, F), lambda e, b: (e, 0, 0)),
            pl.BlockSpec((None, F, D), lambda e, b: (e, 0, 0)),
        ],
        out_specs=pl.BlockSpec((None, None, rows, D), lambda e, b: (b, e, 0, 0)),
        out_shape=jax.ShapeDtypeStruct((B, E, rows, D), bf16),
        compiler_params=_params("arbitrary", "arbitrary"),
        name="moe_ffn",
    )(xg, wg, wu, wd)


def _combine_kernel(n_e, rows, bnd_ref, y_ref, cpt_ref, expand_ref, o_ref):
    b, k = pl.program_id(0), pl.program_id(1)
    cpt = cpt_ref[...]
    hi_part = jnp.floor(cpt * (1.0 / 32.0))
    lo_part = cpt - 32.0 * hi_part
    expand = expand_ref[...]
    rank = 32.0 * _mm(hi_part.astype(bf16), expand) + _mm(lo_part.astype(bf16), expand)
    lane = lax.broadcasted_iota(jnp.int32, (1, n_e * MOE_WIN), 1)
    lane_e = lax.div(lane, MOE_WIN)
    starts = []
    tgt = (lane - lane_e * MOE_WIN + 1).astype(f32)
    for e in range(n_e):
        r0 = lax.div(bnd_ref[(b * n_e + e) * LANES + k], 16) * 16
        starts.append(r0)
        tgt = tgt + jnp.where(lane_e == e, r0.astype(f32), 0.0)
    onehot = jnp.where(rank == tgt, 1.0, 0.0).astype(bf16)
    ycat = jnp.concatenate(
        [y_ref[pl.ds(pl.multiple_of(e * rows + starts[e], 16), MOE_WIN), :] for e in range(n_e)], axis=0)
    o_ref[...] = _mm(onehot, ycat)

    lane_w = lax.broadcasted_iota(jnp.int32, (1, MOE_WIN), 1)
    for e in range(n_e):
        lo, hi = bnd_ref[(b * n_e + e) * LANES + k], bnd_ref[(b * n_e + e) * LANES + k + 1]
        n_win = _window_count(lo, hi, starts[e])

        @pl.when(n_win > 1)
        def _(e=e, n_win=n_win):
            rank_e = rank[:, e * MOE_WIN:(e + 1) * MOE_WIN]

            def extra(w, c):
                r = pl.multiple_of(starts[e] + w * MOE_WIN, 16)
                hit = rank_e == (lane_w + (r + 1)).astype(f32)
                o_ref[...] += _mm(jnp.where(hit, 1.0, 0.0).astype(bf16),
                                  y_ref[pl.ds(pl.multiple_of(e * rows + r, 16), MOE_WIN), :])
                return c

            lax.fori_loop(1, n_win, extra, 0)


def _combine(y, cp, bnd):
    B, E, rows, D = y.shape
    T = cp.shape[2]
    nc = T // MOE_CHUNK
    expand = jnp.asarray(np.kron(np.eye(E), np.ones((1, MOE_WIN))), bf16)
    return pl.pallas_call(
        functools.partial(_combine_kernel, E, rows),
        grid_spec=pltpu.PrefetchScalarGridSpec(
            num_scalar_prefetch=1,
            grid=(B, nc),
            in_specs=[
                pl.BlockSpec((None, E * rows, D), lambda b, k, s: (b, 0, 0), pipeline_mode=pl.Buffered(1)),
                pl.BlockSpec((None, MOE_CHUNK, E), lambda b, k, s: (b, k, 0)),
                pl.BlockSpec(expand.shape, lambda b, k, s: (0, 0)),
            ],
            out_specs=pl.BlockSpec((None, MOE_CHUNK, D), lambda b, k, s: (b, k, 0)),
        ),
        out_shape=jax.ShapeDtypeStruct((B, T, D), f32),
        compiler_params=_params("arbitrary", "arbitrary"),
        name="moe_combine",
    )(bnd, y.reshape(B, E * rows, D), cp.transpose(0, 2, 1), expand)


def _final_kernel(x_ref, moe_ref, mod_ref, o_ref):
    o_ref[...] = x_ref[...] + mod_ref[5:6, :] * moe_ref[...]


def _final(x, moe, mod, S):
    B, T, D = x.shape
    return pl.pallas_call(
        _final_kernel,
        grid=(B, S // TT),
        in_specs=[
            pl.BlockSpec((None, TT, D), lambda b, i: (b, i, 0)),
            pl.BlockSpec((None, TT, D), lambda b, i: (b, i, 0)),
            pl.BlockSpec((None, None, 6, D), lambda b, i: (b, 0, 0, 0)),
        ],
        out_specs=pl.BlockSpec((None, TT, D), lambda b, i: (b, i, 0)),
        out_shape=jax.ShapeDtypeStruct((B, S, D), f32),
        compiler_params=_params("arbitrary", "arbitrary"),
        name="final_residual",
    )(x, moe, mod)


def _rope_tables(S, L):
    t = jnp.arange(S)
    row, col = t // GRID_W, t % GRID_W
    f = 16
    inv = ROPE_BASE ** (-jnp.arange(f, dtype=f32) / f)
    ar = row.astype(f32)[:, None] * inv
    ac = col.astype(f32)[:, None] * inv
    cos = jnp.concatenate([jnp.cos(ar), jnp.cos(ar), jnp.cos(ac), jnp.cos(ac)], axis=1)
    sin = jnp.concatenate([-jnp.sin(ar), jnp.sin(ar), -jnp.sin(ac), jnp.sin(ac)], axis=1)
    cos = jnp.concatenate([jnp.tile(cos, (1, 2)), jnp.ones((L, LANES), f32)], axis=0)
    sin = jnp.concatenate([jnp.tile(sin, (1, 2)), jnp.zeros((L, LANES), f32)], axis=0)
    return cos, sin


def _pad_heads(w, n_heads, width, padded):
    lead = w.shape[:-1]
    w = w.reshape(lead + (n_heads, width))
    w = jnp.pad(w, [(0, 0)] * len(lead) + [(0, 0), (0, padded - width)])
    return w.reshape(lead + (n_heads * padded,))


def _layer_weights(l, D, w_in, norm2_g, na_q_g, na_k_g, mla_q_norm, mla_w_uq, mla_kv_norm, mla_w_ukv,
                   mla_q_g, mla_k_g, swa_q_g, swa_k_g, w_na_o, w_mla_o, w_swa_o, w_o, router,
                   w_gate, w_up, w_down):
    o = np.cumsum((0, NA_W, NA_W, NA_W, MLA_Q_RANK, MLA_KV_RANK, MLA_ROPE, SWA_W, SWA_KV_W, SWA_KV_W))
    o_naq, o_nak, o_nav, o_cq, o_ckv, o_kr, o_swq, o_swk, o_swv, o_gl = (int(v) for v in o)
    wi = w_in[l]
    swq = wi[:, o_swq:o_swq + SWA_W].reshape(D, SWA_HEADS, SWA_DIM)[:, SWA_HEAD_ORDER, :].reshape(D, SWA_W)
    kr = jnp.pad(wi[:, o_kr:o_kr + MLA_ROPE], ((0, 0), (0, LANES - MLA_ROPE)))
    w_all = jnp.concatenate([
        wi[:, o_naq:o_naq + 3 * NA_W], swq, wi[:, o_cq:o_cq + MLA_Q_RANK], wi[:, o_ckv:o_ckv + MLA_KV_RANK], kr,
        wi[:, o_swk:o_swk + 2 * SWA_KV_W], wi[:, o_gl:]], axis=1).astype(bf16)
    ukv = mla_w_ukv[l].reshape(MLA_KV_RANK, MLA_HEADS, MLA_NOPE + MLA_V)
    swa_o = w_swa_o[l].reshape(SWA_HEADS, SWA_DIM, D)[SWA_HEAD_ORDER, :, :].reshape(SWA_W, D)
    r_hi = router[l].astype(bf16)
    r_lo = (router[l] - r_hi.astype(f32)).astype(bf16)
    return dict(
        w_all=w_all,
        naq_g=jnp.tile(na_q_g[l], NA_HEADS)[None], nak_g=jnp.tile(na_k_g[l], NA_HEADS)[None],
        swq_g=jnp.tile(swa_q_g[l], SWA_HEADS)[None], swk_g=jnp.tile(swa_k_g[l], SWA_KV_HEADS)[None],
        qn=mla_q_norm[l][None], kvn=mla_kv_norm[l][None],
        wuq=_pad_heads(mla_w_uq[l], MLA_HEADS, MLA_QK, MLA_PAD).astype(bf16),
        wuk=ukv[:, :, :MLA_NOPE].reshape(MLA_KV_RANK, MLA_HEADS * MLA_NOPE).astype(bf16),
        wuv=ukv[:, :, MLA_NOPE:].reshape(MLA_KV_RANK, MLA_W).astype(bf16),
        qg=jnp.tile(jnp.pad(mla_q_g[l], (0, MLA_PAD - MLA_QK)), MLA_HEADS)[None],
        kg=jnp.tile(jnp.pad(mla_k_g[l], (0, MLA_PAD - MLA_QK)), MLA_HEADS)[None],
        wna=w_na_o[l].astype(bf16), wmla=w_mla_o[l].astype(bf16), wswa=swa_o.astype(bf16),
        wo=w_o[l].astype(bf16), g2=norm2_g[l][None], router=jnp.concatenate([r_hi, r_lo], axis=1),
        wg=w_gate[l].astype(bf16), wu=w_up[l].astype(bf16), wd=w_down[l].astype(bf16),
    )


def kernel(x, c, ctx, c_ctx, norm1_g, norm2_g, w_ada, b_ada, w_in, na_q_g, na_k_g, na_rpb, mla_q_norm, mla_w_uq, mla_kv_norm, mla_w_ukv, mla_q_g, mla_k_g, swa_q_g, swa_k_g, swa_sink, w_na_o, w_mla_o, w_swa_o, w_o, router, w_gate, w_up, w_down):
    B, S, D = x.shape
    L = ctx.shape[1]
    depth = w_in.shape[0]
    E = router.shape[2]
    T = S + L
    n_lat = S // TT
    rows = S // GRID_W
    assert S % TT == 0 and L % TM == 0 and L <= TT and rows >= NA_BAND_ROWS and S >= SWA_KEYS and B < 16
    cap_s = CAPACITY_FACTOR * S // E
    cap_l = CAPACITY_FACTOR * L // E

    cpad = jnp.zeros((16, D), f32).at[:B].set(c).at[B].set(c_ctx)
    mod = _ada(cpad, w_ada, b_ada).reshape(depth, 16, 6, D)
    mod = jnp.stack([mod[:, :B], jnp.broadcast_to(mod[:, B:B + 1], (depth, B, 6, D))], axis=2)

    cos, sin = _rope_tables(S, L)
    gmat = jnp.asarray(np.kron(np.eye(NA_HEADS), np.full((NA_DIM, NA_DIM), 1.0 / NA_DIM)), bf16)

    xs = jnp.concatenate([x, ctx], axis=1)
    moe = None
    for l in range(depth):
        lw = _layer_weights(l, D, w_in, norm2_g, na_q_g, na_k_g, mla_q_norm, mla_w_uq, mla_kv_norm, mla_w_ukv,
                            mla_q_g, mla_k_g, swa_q_g, swa_k_g, w_na_o, w_mla_o, w_swa_o, w_o, router,
                            w_gate, w_up, w_down)
        modp = mod[l - 1] if l > 0 else None
        xs, proj, gl = _in_proj(xs, moe, modp, mod[l], norm1_g[l][None], lw["w_all"], n_lat)
        qka, mq, mk, mv = _prep(proj, cos, sin, gmat, lw)
        ona = _na(qka, proj, _na_bias(na_rpb[l], rows), S)
        omla = _mla(mq, mk, mv, S)
        oswa = _swa(qka, proj, swa_sink[l], S)
        xs, h2, aff, affx = _post(ona, omla, oswa, gl, xs, mod[l], lw, n_lat)
        cp, bnd = _topk(aff.transpose(0, 2, 1), S, cap_s, cap_l)
        bnd = bnd.astype(jnp.int32).reshape(-1)
        xg = _gather(h2, affx, cp, bnd, cap_s + cap_l)
        y = _ffn(xg, lw["wg"], lw["wu"], lw["wd"], cap_s + cap_l)
        moe = _combine(y, cp, bnd)
    return _final(xs, moe, mod[depth - 1], S)
```

```python
import functools

import numpy as np
import jax
import jax.numpy as jnp
from jax import lax
from jax.experimental import pallas as pl
from jax.experimental.pallas import tpu as pltpu

GRID_W = 64
NA_HEADS, NA_DIM, NA_WIN_ROWS, NA_WIN_COLS = 6, 64, 8, 16
MLA_HEADS, MLA_Q_RANK, MLA_KV_RANK, MLA_NOPE, MLA_ROPE, MLA_V = 4, 256, 128, 128, 64, 128
MLA_QK = MLA_NOPE + MLA_ROPE
MLA_PAD = 256
SWA_HEADS, SWA_KV_HEADS, SWA_DIM, SWA_WINDOW = 6, 2, 64, 128
CAPACITY_FACTOR = 2
N_BRANCH = 3
ROPE_BASE = 10000.0
EPS = 1e-6

NA_W = NA_HEADS * NA_DIM
SWA_W = SWA_HEADS * SWA_DIM
SWA_KV_W = SWA_KV_HEADS * SWA_DIM
MLA_W = MLA_HEADS * MLA_V

LANES = 128
TM = 256
TT = 512
MLA_TQ = 1024
MLA_TK = 1024
LOG2E = 1.4426950408889634
NA_TILE_ROWS = TM // GRID_W
NA_BAND_ROWS = NA_TILE_ROWS + NA_WIN_ROWS - 1
NA_BAND = NA_BAND_ROWS * GRID_W
SWA_KEYS = TM + 2 * SWA_WINDOW
MOE_CHUNK = 256
MOE_WIN = 64
NEG = -1e30
VMEM_LIMIT = 56 * 1024 * 1024

P_NAQ, P_NAK, P_NAV, P_SWQ = 0, 384, 768, 1152
P_CQ, P_CKV, P_KR, P_SWK, P_SWV = 1536, 1792, 1920, 2048, 2176
PROJ_W = 2304
N_CHUNK = 768
A_NAQ, A_NAK, A_SWQ, A_SWK = 0, 384, 768, 1152
QKA_W = 1280
SWA_HEAD_ORDER = (0, 3, 1, 4, 2, 5)

bf16 = jnp.bfloat16
f32 = jnp.float32


def _mm(a, b):
    return jnp.dot(a, b, preferred_element_type=f32)


def _nt(a, b):
    return lax.dot_general(a, b, (((1,), (1,)), ((), ())), preferred_element_type=f32)


def _params(*sem):
    return pltpu.CompilerParams(dimension_semantics=sem, vmem_limit_bytes=VMEM_LIMIT)


def _ada_kernel(c_ref, w_ref, b_ref, o_ref):
    a = c_ref[...]
    a = (a * jax.nn.sigmoid(a)).astype(bf16)
    o_ref[...] = _mm(a, w_ref[...].astype(bf16)) + b_ref[...]


def _ada(cpad, w_ada, b_ada):
    L, D, N = w_ada.shape
    tn = 1536
    return pl.pallas_call(
        _ada_kernel,
        grid=(L, N // tn),
        in_specs=[
            pl.BlockSpec((16, D), lambda l, j: (0, 0)),
            pl.BlockSpec((None, D, tn), lambda l, j: (l, 0, j)),
            pl.BlockSpec((None, 1, tn), lambda l, j: (l, 0, j)),
        ],
        out_specs=pl.BlockSpec((None, 16, tn), lambda l, j: (l, 0, j)),
        out_shape=jax.ShapeDtypeStruct((L, 16, N), f32),
        compiler_params=_params("arbitrary", "arbitrary"),
        name="ada",
    )(cpad, w_ada, b_ada.reshape(L, 1, N))


def _in_proj_kernel(has_moe, n_gl_chunks, *refs):
    if has_moe:
        x_ref, moe_ref, modp_ref, mod_ref, g_ref, w_ref, xo_ref, proj_ref, gl_ref = refs
        x = x_ref[...] + modp_ref[5:6, :] * moe_ref[...]
        xo_ref[...] = x
    else:
        x_ref, mod_ref, g_ref, w_ref, proj_ref, gl_ref = refs
        x = x_ref[...]
    ms = jnp.mean(x * x, axis=-1, keepdims=True)
    h = x * lax.rsqrt(ms + EPS) * g_ref[...]
    h = (h * (1.0 + mod_ref[1:2, :]) + mod_ref[0:1, :]).astype(bf16)
    for c in range(PROJ_W // N_CHUNK):
        sl = slice(c * N_CHUNK, (c + 1) * N_CHUNK)
        proj_ref[:, sl] = _mm(h, w_ref[:, sl]).astype(bf16)
    for c in range(n_gl_chunks):
        sl = slice(c * N_CHUNK, (c + 1) * N_CHUNK)
        gl_ref[:, sl] = _mm(h, w_ref[:, PROJ_W + c * N_CHUNK:PROJ_W + (c + 1) * N_CHUNK]).astype(bf16)


def _mod_spec(n_lat, D):
    return pl.BlockSpec((None, None, 6, D), lambda b, i: (b, jnp.where(i >= n_lat, 1, 0), 0, 0))


def _in_proj(x, moe, modp, mod, g1, w, n_lat):
    B, T, D = x.shape
    nt = pl.cdiv(T, TT)
    has_moe = moe is not None
    n_gl_chunks = N_BRANCH * D // N_CHUNK
    tok = pl.BlockSpec((None, TT, D), lambda b, i: (b, i, 0))
    in_specs = [tok]
    args = [x]
    if has_moe:
        in_specs += [tok, _mod_spec(n_lat, D)]
        args += [moe, modp]
    in_specs += [
        _mod_spec(n_lat, D),
        pl.BlockSpec((1, D), lambda b, i: (0, 0)),
        pl.BlockSpec(w.shape, lambda b, i: (0, 0), pipeline_mode=pl.Buffered(1)),
    ]
    args += [mod, g1, w]
    out_specs = [
        pl.BlockSpec((None, TT, PROJ_W), lambda b, i: (b, i, 0)),
        pl.BlockSpec((None, TT, N_BRANCH * D), lambda b, i: (b, i, 0)),
    ]
    out_shape = [
        jax.ShapeDtypeStruct((B, T, PROJ_W), bf16),
        jax.ShapeDtypeStruct((B, T, N_BRANCH * D), bf16),
    ]
    if has_moe:
        out_specs = [tok] + out_specs
        out_shape = [jax.ShapeDtypeStruct((B, T, D), f32)] + out_shape
    res = pl.pallas_call(
        functools.partial(_in_proj_kernel, has_moe, n_gl_chunks),
        grid=(B, nt),
        in_specs=in_specs,
        out_specs=out_specs,
        out_shape=out_shape,
        compiler_params=_params("arbitrary", "arbitrary"),
        name="in_proj",
    )(*args)
    if has_moe:
        return res
    return [x] + list(res)


def _prep_kernel(proj_ref, cos_ref, sin_ref, gmat_ref, naq_g, nak_g, swq_g, swk_g,
                 qn_ref, wuq_ref, qg_ref, kvn_ref, wuk_ref, wuv_ref, kg_ref,
                 qka_ref, mq_ref, mk_ref, mv_ref):
    cos = cos_ref[...]
    sin = sin_ref[...]
    lane = lax.broadcasted_iota(jnp.int32, cos.shape, 1)
    first = (lane & 16) == 0

    def rope(x):
        partner = jnp.where(first, pltpu.roll(x, LANES - 16, 1), pltpu.roll(x, 16, 1))
        return x * cos + partner * sin

    def headnorm(x, gain):
        w = x.shape[1]
        ms = _mm((x * x).astype(bf16), gmat_ref[:w, :w])
        return x * lax.rsqrt(ms + EPS) * gain

    def rmsnorm(x, gain):
        ms = jnp.mean(x * x, axis=-1, keepdims=True)
        return x * lax.rsqrt(ms + EPS) * gain

    na_scale = NA_DIM ** -0.5 * LOG2E
    naq = headnorm(proj_ref[:, P_NAQ:P_NAQ + NA_W].astype(f32), naq_g[...])
    qka_ref[:, A_NAQ:A_NAQ + NA_W] = (naq * na_scale).astype(bf16)
    nak = headnorm(proj_ref[:, P_NAK:P_NAK + NA_W].astype(f32), nak_g[...])
    qka_ref[:, A_NAK:A_NAK + NA_W] = nak.astype(bf16)

    sw_scale = SWA_DIM ** -0.5 * LOG2E
    swq = headnorm(proj_ref[:, P_SWQ:P_SWQ + SWA_W].astype(f32), swq_g[...])
    for p in range(SWA_W // LANES):
        sl = slice(p * LANES, (p + 1) * LANES)
        qka_ref[:, A_SWQ + p * LANES:A_SWQ + (p + 1) * LANES] = (rope(swq[:, sl]) * sw_scale).astype(bf16)
    swk = headnorm(proj_ref[:, P_SWK:P_SWK + SWA_KV_W].astype(f32), swk_g[...])
    qka_ref[:, A_SWK:A_SWK + SWA_KV_W] = rope(swk).astype(bf16)

    mla_scale = MLA_QK ** -0.5 * LOG2E
    cq = rmsnorm(proj_ref[:, P_CQ:P_CQ + MLA_Q_RANK].astype(f32), qn_ref[...]).astype(bf16)
    q = _mm(cq, wuq_ref[...])
    ckv = rmsnorm(proj_ref[:, P_CKV:P_CKV + MLA_KV_RANK].astype(f32), kvn_ref[...]).astype(bf16)
    kn = _mm(ckv, wuk_ref[...])
    v = _mm(ckv, wuv_ref[...]).astype(bf16)
    ones_col = jnp.where(lax.broadcasted_iota(jnp.int32, (v.shape[0], LANES), 1) == 0, 1.0, 0.0).astype(bf16)
    for h in range(MLA_HEADS):
        mv_ref[:, h * MLA_PAD:h * MLA_PAD + MLA_V] = v[:, h * MLA_V:(h + 1) * MLA_V]
        mv_ref[:, h * MLA_PAD + MLA_V:(h + 1) * MLA_PAD] = ones_col
    kr = proj_ref[:, P_KR:P_KR + LANES].astype(f32)
    kr_ss = jnp.sum(kr * kr, axis=-1, keepdims=True)
    for h in range(MLA_HEADS):
        o = h * MLA_PAD
        qh = q[:, o:o + MLA_PAD]
        r = lax.rsqrt(jnp.sum(qh * qh, axis=-1, keepdims=True) * (1.0 / MLA_QK) + EPS)
        qh = qh * r * qg_ref[:, o:o + MLA_PAD]
        mq_ref[:, o:o + LANES] = (qh[:, :LANES] * mla_scale).astype(bf16)
        mq_ref[:, o + LANES:o + MLA_PAD] = (rope(qh[:, LANES:]) * mla_scale).astype(bf16)
        kh = kn[:, h * MLA_NOPE:(h + 1) * MLA_NOPE]
        r = lax.rsqrt((jnp.sum(kh * kh, axis=-1, keepdims=True) + kr_ss) * (1.0 / MLA_QK) + EPS)
        mk_ref[:, o:o + LANES] = (kh * r * kg_ref[:, o:o + LANES]).astype(bf16)
        mk_ref[:, o + LANES:o + MLA_PAD] = rope(kr * r * kg_ref[:, o + LANES:o + MLA_PAD]).astype(bf16)


def _prep(proj, cos, sin, gmat, lw):
    B, T, _ = proj.shape
    nt = pl.cdiv(T, TT)

    def const(a):
        return pl.BlockSpec(a.shape, lambda b, i: (0,) * a.ndim)

    consts = [gmat, lw["naq_g"], lw["nak_g"], lw["swq_g"], lw["swk_g"], lw["qn"], lw["wuq"], lw["qg"],
              lw["kvn"], lw["wuk"], lw["wuv"], lw["kg"]]
    widths = (QKA_W, MLA_HEADS * MLA_PAD, MLA_HEADS * MLA_PAD, MLA_HEADS * MLA_PAD)
    return pl.pallas_call(
        _prep_kernel,
        grid=(B, nt),
        in_specs=[
            pl.BlockSpec((None, TT, PROJ_W), lambda b, i: (b, i, 0)),
            pl.BlockSpec((TT, LANES), lambda b, i: (i, 0)),
            pl.BlockSpec((TT, LANES), lambda b, i: (i, 0)),
        ] + [const(a) for a in consts],
        out_specs=[pl.BlockSpec((None, TT, w), lambda b, i: (b, i, 0)) for w in widths],
        out_shape=[jax.ShapeDtypeStruct((B, T, w), bf16) for w in widths],
        compiler_params=_params("arbitrary", "arbitrary"),
        name="prep",
    )(proj, cos, sin, *consts)


def _half_masks():
    lane = lax.broadcasted_iota(jnp.int32, (1, LANES), 1)
    lo = jnp.where(lane < 64, 1.0, 0.0).astype(bf16)
    return lo, (1.0 - lo.astype(f32)).astype(bf16)


def _pair_attend(qp, parts, masks, sinks=None):
    n_q = qp.shape[0]
    q2 = jnp.concatenate([qp * masks[0], qp * masks[1]], axis=0)
    scores = []
    for k, _, bias in parts:
        s = _nt(q2, k)
        scores.append(s if bias is None else s + bias)
    m = functools.reduce(jnp.maximum, [jnp.max(s, axis=-1, keepdims=True) for s in scores])
    if sinks is not None:
        first = lax.broadcasted_iota(jnp.int32, (2 * n_q, 1), 0) < n_q
        sink = jnp.where(first, sinks[0], sinks[1])
        m = jnp.maximum(m, sink)
    ps = [jnp.exp2(s - m) for s in scores]
    l = functools.reduce(jnp.add, [jnp.sum(p, axis=-1, keepdims=True) for p in ps])
    if sinks is not None:
        l = l + jnp.exp2(sink - m)
    o = functools.reduce(jnp.add, [_mm(p.astype(bf16), v) for p, (_, v, _) in zip(ps, parts)]) / l
    lane = lax.broadcasted_iota(jnp.int32, (1, LANES), 1)
    return jnp.where(lane < 64, o[:n_q], o[n_q:])


def _na_kernel(n_lat, S, rows, q_ref, k_ref, v_ref, bias_ref, o_ref):
    i = pl.program_id(1)
    masks = _half_masks()
    T = k_ref.shape[0]

    @pl.when(i < n_lat)
    def _():
        start = GRID_W * jnp.clip(NA_TILE_ROWS * i - NA_WIN_ROWS // 2, 0, rows - NA_BAND_ROWS)
        start = pl.multiple_of(start, GRID_W)
        for p in range(NA_W // LANES):
            sl = slice(p * LANES, (p + 1) * LANES)
            band = (k_ref[pl.ds(start, NA_BAND), sl], v_ref[pl.ds(start, NA_BAND), sl],
                    bias_ref[2 * p:2 * p + 2].reshape(2 * TM, NA_BAND))
            ctx = (k_ref[S:T, sl], v_ref[S:T, sl], None)
            o_ref[:, sl] = _pair_attend(q_ref[:, sl], [band, ctx], masks).astype(bf16)

    @pl.when(i >= n_lat)
    def _():
        for p in range(NA_W // LANES):
            sl = slice(p * LANES, (p + 1) * LANES)
            ctx = (k_ref[S:T, sl], v_ref[S:T, sl], None)
            o_ref[:, sl] = _pair_attend(q_ref[:, sl], [ctx], masks).astype(bf16)


def _na(qka, proj, bias, S):
    B, T, _ = qka.shape
    nt, n_lat, rows = T // TM, S // TM, S // GRID_W

    def bias_idx(b, i):
        return (jnp.where(i == 0, 0, jnp.where(i >= n_lat - 1, 2, 1)), 0, 0, 0)

    return pl.pallas_call(
        functools.partial(_na_kernel, n_lat, S, rows),
        grid=(B, nt),
        in_specs=[
            pl.BlockSpec((None, TM, NA_W), lambda b, i: (b, i, A_NAQ // NA_W)),
            pl.BlockSpec((None, T, NA_W), lambda b, i: (b, 0, A_NAK // NA_W)),
            pl.BlockSpec((None, T, NA_W), lambda b, i: (b, 0, P_NAV // NA_W)),
            pl.BlockSpec((None, NA_HEADS, TM, NA_BAND), bias_idx),
        ],
        out_specs=pl.BlockSpec((None, TM, NA_W), lambda b, i: (b, i, 0)),
        out_shape=jax.ShapeDtypeStruct((B, T, NA_W), bf16),
        compiler_params=_params("arbitrary", "arbitrary"),
        name="na_attn",
    )(qka, qka, proj, bias)


def _na_bias(rpb, rows):
    n_tiles = rows // NA_TILE_ROWS
    col = np.arange(GRID_W)
    c0 = np.clip(col - NA_WIN_COLS // 2, 0, GRID_W - NA_WIN_COLS)
    col_ok = (col[None, :] >= c0[:, None]) & (col[None, :] < c0[:, None] + NA_WIN_COLS)
    dc = np.clip(col[None, :] - col[:, None] + NA_WIN_COLS - 1, 0, 2 * NA_WIN_COLS - 2)
    tz = jnp.where(col_ok, rpb[:, :, dc] * LOG2E, NEG)
    dr = np.zeros((3, NA_TILE_ROWS, NA_BAND_ROWS), np.int32)
    ok = np.zeros((3, NA_TILE_ROWS, NA_BAND_ROWS), bool)
    for ty, rt in enumerate((0, 1, n_tiles - 1)):
        bs = int(np.clip(NA_TILE_ROWS * rt - NA_WIN_ROWS // 2, 0, rows - NA_BAND_ROWS))
        for qr in range(NA_TILE_ROWS):
            r = NA_TILE_ROWS * rt + qr
            s_r = int(np.clip(r - NA_WIN_ROWS // 2, 0, rows - NA_WIN_ROWS))
            for w in range(NA_BAND_ROWS):
                kr = bs + w
                ok[ty, qr, w] = s_r <= kr < s_r + NA_WIN_ROWS
                dr[ty, qr, w] = np.clip(kr - r + NA_WIN_ROWS - 1, 0, 2 * NA_WIN_ROWS - 2)
    blk = tz[:, dr]
    blk = jnp.where(ok[None, :, :, :, None, None], blk, NEG)
    blk = blk.transpose(1, 0, 2, 4, 3, 5)
    return blk.reshape(3, NA_HEADS, TM, NA_BAND)


def _mla_kernel(n_q, S, q_ref, k_ref, v_ref, o_ref):
    i = pl.program_id(2)
    T = k_ref.shape[0]

    def attend(q, bounds):
        m_run = acc = None
        for lo, hi in bounds:
            s = _nt(q, k_ref[lo:hi, :])
            m_new = jnp.max(s, axis=-1, keepdims=True)
            if m_run is not None:
                m_new = jnp.maximum(m_run, m_new)
            pv = _mm(jnp.exp2(s - m_new).astype(bf16), v_ref[lo:hi, :])
            acc = pv if acc is None else acc * jnp.exp2(m_run - m_new) + pv
            m_run = m_new
        return (acc[:, :MLA_V] / acc[:, MLA_V:MLA_V + 1]).astype(bf16)

    @pl.when(i < n_q)
    def _():
        chunks = [(lo, lo + MLA_TK) for lo in range(0, S, MLA_TK)] + [(S, T)]
        o_ref[...] = attend(q_ref[...], chunks)

    @pl.when(i >= n_q)
    def _():
        o_ref[0:T - S, :] = attend(q_ref[0:T - S, :], [(S, T)])


def _mla(mq, mk, mv, S):
    B, T, _ = mq.shape
    n_q = S // MLA_TQ
    assert S % MLA_TQ == 0 and S % MLA_TK == 0 and T - S <= MLA_TQ
    return pl.pallas_call(
        functools.partial(_mla_kernel, n_q, S),
        grid=(B, MLA_HEADS, n_q + 1),
        in_specs=[
            pl.BlockSpec((None, MLA_TQ, MLA_PAD), lambda b, h, i: (b, i, h)),
            pl.BlockSpec((None, T, MLA_PAD), lambda b, h, i: (b, 0, h)),
            pl.BlockSpec((None, T, MLA_PAD), lambda b, h, i: (b, 0, h)),
        ],
        out_specs=pl.BlockSpec((None, MLA_TQ, MLA_V), lambda b, h, i: (b, i, h)),
        out_shape=jax.ShapeDtypeStruct((B, T, MLA_W), bf16),
        compiler_params=_params("arbitrary", "arbitrary", "arbitrary"),
        name="mla_attn",
    )(mq, mk, mv)


def _swa_kernel(n_lat, S, sink_ref, q_ref, k_ref, v_ref, o_ref):
    i = pl.program_id(1)
    masks = _half_masks()
    T = k_ref.shape[0]
    kc, vc = k_ref[S:T, :], v_ref[S:T, :]

    def sinks(p):
        return (sink_ref[SWA_HEAD_ORDER[2 * p]], sink_ref[SWA_HEAD_ORDER[2 * p + 1]])

    @pl.when(i < n_lat)
    def _():
        start = pl.multiple_of(jnp.clip(i * TM - SWA_WINDOW, 0, S - SWA_KEYS), SWA_WINDOW)
        row = lax.broadcasted_iota(jnp.int32, (2 * TM, SWA_KEYS), 0)
        qpos = i * TM + jnp.where(row >= TM, row - TM, row)
        kpos = start + lax.broadcasted_iota(jnp.int32, (2 * TM, SWA_KEYS), 1)
        band = jnp.where(jnp.abs(qpos - kpos) <= SWA_WINDOW, 0.0, NEG)
        kb, vb = k_ref[pl.ds(start, SWA_KEYS), :], v_ref[pl.ds(start, SWA_KEYS), :]
        for p in range(SWA_W // LANES):
            sl = slice(p * LANES, (p + 1) * LANES)
            parts = [(kb, vb, band), (kc, vc, None)]
            o_ref[:, sl] = _pair_attend(q_ref[:, sl], parts, masks, sinks(p)).astype(bf16)

    @pl.when(i >= n_lat)
    def _():
        for p in range(SWA_W // LANES):
            sl = slice(p * LANES, (p + 1) * LANES)
            o_ref[:, sl] = _pair_attend(q_ref[:, sl], [(kc, vc, None)], masks, sinks(p)).astype(bf16)


def _swa(qka, proj, sink, S):
    B, T, _ = qka.shape
    nt, n_lat = T // TM, S // TM
    return pl.pallas_call(
        functools.partial(_swa_kernel, n_lat, S),
        grid=(B, nt),
        in_specs=[
            pl.BlockSpec(memory_space=pltpu.SMEM),
            pl.BlockSpec((None, TM, SWA_W), lambda b, i: (b, i, A_SWQ // SWA_W)),
            pl.BlockSpec((None, T, SWA_KV_W), lambda b, i: (b, 0, A_SWK // SWA_KV_W)),
            pl.BlockSpec((None, T, SWA_KV_W), lambda b, i: (b, 0, P_SWV // SWA_KV_W)),
        ],
        out_specs=pl.BlockSpec((None, TM, SWA_W), lambda b, i: (b, i, 0)),
        out_shape=jax.ShapeDtypeStruct((B, T, SWA_W), bf16),
        compiler_params=_params("arbitrary", "arbitrary"),
        name="swa_attn",
    )(sink, qka, qka, proj)


def _post_kernel(ona_ref, omla_ref, oswa_ref, gl_ref, x_ref, mod_ref, wna_ref, wmla_ref, wswa_ref, wo_ref,
                 g2_ref, rt_ref, xo_ref, h2_ref, aff_ref, affx_ref):
    D = x_ref.shape[1]
    merged = None
    for j, (o_ref, w_ref) in enumerate(((ona_ref, wna_ref), (omla_ref, wmla_ref), (oswa_ref, wswa_ref))):
        y = _mm(o_ref[...], w_ref[...])
        g = 0.5 * jnp.tanh(0.5 * gl_ref[:, j * D:(j + 1) * D].astype(f32)) + 0.5
        merged = g * y if merged is None else merged + g * y
    res = _mm(merged.astype(bf16), wo_ref[...])
    x = x_ref[...] + mod_ref[2:3, :] * res
    xo_ref[...] = x
    ms = jnp.mean(x * x, axis=-1, keepdims=True)
    h2 = x * lax.rsqrt(ms + EPS) * g2_ref[...]
    h2 = h2 * (1.0 + mod_ref[4:5, :]) + mod_ref[3:4, :]
    h_hi = h2.astype(bf16)
    h2_ref[...] = h_hi
    h_lo = (h2 - h_hi.astype(f32)).astype(bf16)
    parts = _mm(h_hi, rt_ref[...]) + _mm(h_lo, rt_ref[...])
    n_e = aff_ref.shape[1]
    logits = parts[:, :n_e] + parts[:, n_e:]
    e = jnp.exp(logits - jnp.max(logits, axis=1, keepdims=True))
    aff = e / jnp.sum(e, axis=1, keepdims=True)
    aff_ref[...] = aff
    p0 = aff.astype(bf16)
    r1 = aff - p0.astype(f32)
    p1 = r1.astype(bf16)
    p2 = (r1 - p1.astype(f32)).astype(bf16)
    pad = jnp.zeros((aff.shape[0], LANES - 3 * n_e), bf16)
    affx_ref[...] = jnp.concatenate([p0, p1, p2, pad], axis=1)


def _post(ona, omla, oswa, gl, x, mod, lw, n_lat):
    B, T, D = x.shape
    nt = pl.cdiv(T, TT)
    E = lw["router"].shape[1] // 2

    def tok(w):
        return pl.BlockSpec((None, TT, w), lambda b, i: (b, i, 0))

    def const(a):
        return pl.BlockSpec(a.shape, lambda b, i: (0,) * a.ndim)

    consts = [lw["wna"], lw["wmla"], lw["wswa"], lw["wo"], lw["g2"], lw["router"]]
    return pl.pallas_call(
        _post_kernel,
        grid=(B, nt),
        in_specs=[tok(NA_W), tok(MLA_W), tok(SWA_W), tok(N_BRANCH * D), tok(D), _mod_spec(n_lat, D)]
        + [const(a) for a in consts],
        out_specs=[tok(D), tok(D), tok(E), tok(LANES)],
        out_shape=[jax.ShapeDtypeStruct((B, T, D), f32), jax.ShapeDtypeStruct((B, T, D), bf16),
                   jax.ShapeDtypeStruct((B, T, E), f32), jax.ShapeDtypeStruct((B, T, LANES), bf16)],
        compiler_params=_params("arbitrary", "arbitrary"),
        name="post_attn",
    )(ona, omla, oswa, gl, x, mod, *consts)


def _lane_cumsum(mask, tri):
    E, n = mask.shape
    carry = jnp.zeros((E, 1), f32)
    outs = []
    for k in range(n // LANES):
        w = _mm(mask[:, k * LANES:(k + 1) * LANES].astype(bf16), tri) + carry
        outs.append(w)
        carry = w[:, LANES - 1:LANES]
    return jnp.concatenate(outs, axis=1)


def _select_slots(aff, cap, base, tri):
    bits = lax.bitcast_convert_type(aff, jnp.int32)
    thr = jnp.zeros((aff.shape[0], 1), jnp.int32)
    for bit in range(30, -1, -1):
        cand = thr | (1 << bit)
        cnt = jnp.sum(jnp.where(bits >= cand, 1.0, 0.0), axis=1, keepdims=True)
        thr = jnp.where(cnt >= cap, cand, thr)
    gt = jnp.where(bits > thr, 1.0, 0.0)
    eq = jnp.where(bits == thr, 1.0, 0.0)
    need = cap - jnp.sum(gt, axis=1, keepdims=True)
    sel = jnp.maximum(gt, jnp.where(_lane_cumsum(eq, tri) <= need, eq, 0.0))
    return jnp.where(sel > 0.0, _lane_cumsum(sel, tri) + base, 0.0)


def _topk_kernel(S, cap_s, cap_l, aff_ref, tri_ref, before_ref, cp_ref, bnd_ref):
    T = aff_ref.shape[1]
    tri = tri_ref[...]
    cp_ref[:, 0:S] = _select_slots(aff_ref[:, 0:S], cap_s, 0.0, tri)
    cp_ref[:, S:T] = _select_slots(aff_ref[:, S:T], cap_l, float(cap_s), tri)
    bnd_ref[...] = _mm(jnp.where(cp_ref[...] > 0.0, 1.0, 0.0).astype(bf16), before_ref[...])


def _topk(aff, S, cap_s, cap_l):
    B, E, T = aff.shape
    tri = jnp.asarray(np.triu(np.ones((LANES, LANES), np.float32)), bf16)
    before = jnp.asarray(np.arange(T)[:, None] < MOE_CHUNK * np.arange(LANES)[None, :], bf16)
    return pl.pallas_call(
        functools.partial(_topk_kernel, S, cap_s, cap_l),
        grid=(B,),
        in_specs=[pl.BlockSpec((None, E, T), lambda b: (b, 0, 0)),
                  pl.BlockSpec((LANES, LANES), lambda b: (0, 0)),
                  pl.BlockSpec((T, LANES), lambda b: (0, 0))],
        out_specs=[pl.BlockSpec((None, E, T), lambda b: (b, 0, 0)),
                   pl.BlockSpec((None, E, LANES), lambda b: (b, 0, 0))],
        out_shape=[jax.ShapeDtypeStruct((B, E, T), f32), jax.ShapeDtypeStruct((B, E, LANES), f32)],
        compiler_params=_params("arbitrary"),
        name="expert_select",
    )(aff, tri, before)


def _window_count(lo, hi, r0):
    return jnp.where(hi > lo, lax.div(hi - r0 + (MOE_WIN - 1), MOE_WIN), 0)


def _gather_kernel(n_e, rows, bnd_ref, h2_ref, affx_ref, cp_ref, xg_ref):
    b, k = pl.program_id(0), pl.program_id(1)
    D = h2_ref.shape[1]

    @pl.when(k == 0)
    def _():
        xg_ref[...] = jnp.zeros_like(xg_ref)

    ex = lax.broadcasted_iota(jnp.int32, (n_e, 1), 0)
    starts = []
    start_col = jnp.zeros((n_e, 1), f32)
    for e in range(n_e):
        r0 = lax.div(bnd_ref[(b * n_e + e) * LANES + k], 16) * 16
        starts.append(r0)
        start_col = start_col + jnp.where(ex == e, r0.astype(f32), 0.0)
    rel = cp_ref[...] - start_col
    rel_rows = jnp.broadcast_to(rel[:, None, :], (n_e, MOE_WIN, MOE_CHUNK)).reshape(n_e * MOE_WIN, MOE_CHUNK)
    j1 = lax.broadcasted_iota(jnp.int32, (n_e, MOE_WIN, 1), 1).reshape(n_e * MOE_WIN, 1) + 1
    onehot = jnp.where(rel_rows == j1.astype(f32), 1.0, 0.0).astype(bf16)
    tok, tok_aff = h2_ref[...], affx_ref[...]
    moved = _mm(onehot, tok).astype(bf16)
    moved_aff = _mm(onehot, tok_aff).astype(bf16)
    for e in range(n_e):
        dst = pl.ds(pl.multiple_of(e * rows + starts[e], 16), MOE_WIN)
        xg_ref[dst, 0:D] += moved[e * MOE_WIN:(e + 1) * MOE_WIN]
        xg_ref[dst, D:D + LANES] += moved_aff[e * MOE_WIN:(e + 1) * MOE_WIN]

    sub1 = lax.broadcasted_iota(jnp.int32, (MOE_WIN, 1), 0) + 1
    for e in range(n_e):
        lo, hi = bnd_ref[(b * n_e + e) * LANES + k], bnd_ref[(b * n_e + e) * LANES + k + 1]
        n_win = _window_count(lo, hi, starts[e])

        @pl.when(n_win > 1)
        def _(e=e, n_win=n_win):
            def extra(w, c):
                hit = rel[e:e + 1, :] == (sub1 + w * MOE_WIN).astype(f32)
                oh = jnp.where(hit, 1.0, 0.0).astype(bf16)
                dst = pl.ds(pl.multiple_of(e * rows + starts[e] + w * MOE_WIN, 16), MOE_WIN)
                xg_ref[dst, 0:D] += _mm(oh, tok).astype(bf16)
                xg_ref[dst, D:D + LANES] += _mm(oh, tok_aff).astype(bf16)
                return c

            lax.fori_loop(1, n_win, extra, 0)


def _gather(h2, affx, cp, bnd, n_slots):
    B, T, D = h2.shape
    E = cp.shape[1]
    nc = T // MOE_CHUNK
    rows = n_slots + MOE_WIN
    assert n_slots % 16 == 0 and nc + 1 <= LANES and 3 * E <= LANES
    return pl.pallas_call(
        functools.partial(_gather_kernel, E, rows),
        grid_spec=pltpu.PrefetchScalarGridSpec(
            num_scalar_prefetch=1,
            grid=(B, nc),
            in_specs=[
                pl.BlockSpec((None, MOE_CHUNK, D), lambda b, k, s: (b, k, 0)),
                pl.BlockSpec((None, MOE_CHUNK, LANES), lambda b, k, s: (b, k, 0)),
                pl.BlockSpec((None, E, MOE_CHUNK), lambda b, k, s: (b, 0, k)),
            ],
            out_specs=pl.BlockSpec((None, E * rows, D + LANES), lambda b, k, s: (b, 0, 0),
                                   pipeline_mode=pl.Buffered(1)),
        ),
        out_shape=jax.ShapeDtypeStruct((B, E * rows, D + LANES), bf16),
        compiler_params=_params("arbitrary", "arbitrary"),
        name="moe_gather",
    )(bnd, h2, affx, cp)


def _ffn_kernel(n_slots, n_e, xg_ref, wg_ref, wu_ref, wd_ref, y_ref, wg_s, wu_s, wd_s):
    e, b = pl.program_id(0), pl.program_id(1)
    D = wg_ref.shape[0]

    @pl.when(b == 0)
    def _():
        wg_s[...] = wg_ref[...].astype(bf16)
        wu_s[...] = wu_ref[...].astype(bf16)
        wd_s[...] = wd_ref[...].astype(bf16)

    xg = xg_ref[0:n_slots, 0:D]
    a = _mm(xg, wg_s[...])
    u = _mm(xg, wu_s[...])
    act = (a * jax.nn.sigmoid(a) * u).astype(bf16)
    lane = lax.broadcasted_iota(jnp.int32, (1, LANES), 1)
    mine = jnp.logical_and(lax.rem(lane, n_e) == e, lane < 3 * n_e)
    gate = jnp.sum(jnp.where(mine, xg_ref[0:n_slots, D:D + LANES].astype(f32), 0.0), axis=1, keepdims=True)
    y_ref[0:n_slots, :] = (_mm(act, wd_s[...]) * gate).astype(bf16)
    y_ref[n_slots:, :] = jnp.zeros((y_ref.shape[0] - n_slots, y_ref.shape[1]), bf16)


def _ffn(xg, w_gate, w_up, w_down, layer, n_slots):
    B = xg.shape[0]
    _, E, D, F = w_gate.shape
    rows = xg.shape[1] // E
    return pl.pallas_call(
        functools.partial(_ffn_kernel, n_slots, E),
        grid=(E, B),
        in_specs=[
            pl.BlockSpec((None, rows, D + LANES), lambda e, b: (b, e, 0)),
            pl.BlockSpec((None, None, D, F), lambda e, b: (layer, e, 0, 0)),
            pl.BlockSpec((None, None, D, F), lambda e, b: (layer, e, 0, 0)),
            pl.BlockSpec((None, None, F, D), lambda e, b: (layer, e, 0, 0)),
        ],
        out_specs=pl.BlockSpec((None, None, rows, D), lambda e, b: (b, e, 0, 0)),
        out_shape=jax.ShapeDtypeStruct((B, E, rows, D), bf16),
        scratch_shapes=[pltpu.VMEM((D, F), bf16), pltpu.VMEM((D, F), bf16), pltpu.VMEM((F, D), bf16)],
        compiler_params=_params("arbitrary", "arbitrary"),
        name="moe_ffn",
    )(xg, w_gate, w_up, w_down)


def _combine_kernel(n_e, rows, bnd_ref, y_ref, cpt_ref, expand_ref, o_ref):
    b, k = pl.program_id(0), pl.program_id(1)
    cpt = cpt_ref[...]
    hi_part = jnp.floor(cpt * (1.0 / 32.0))
    lo_part = cpt - 32.0 * hi_part
    expand = expand_ref[...]
    rank = 32.0 * _mm(hi_part.astype(bf16), expand) + _mm(lo_part.astype(bf16), expand)
    lane = lax.broadcasted_iota(jnp.int32, (1, n_e * MOE_WIN), 1)
    lane_e = lax.div(lane, MOE_WIN)
    starts = []
    tgt = (lane - lane_e * MOE_WIN + 1).astype(f32)
    for e in range(n_e):
        r0 = lax.div(bnd_ref[(b * n_e + e) * LANES + k], 16) * 16
        starts.append(r0)
        tgt = tgt + jnp.where(lane_e == e, r0.astype(f32), 0.0)
    onehot = jnp.where(rank == tgt, 1.0, 0.0).astype(bf16)
    ycat = jnp.concatenate(
        [y_ref[pl.ds(pl.multiple_of(e * rows + starts[e], 16), MOE_WIN), :] for e in range(n_e)], axis=0)
    o_ref[...] = _mm(onehot, ycat)

    lane_w = lax.broadcasted_iota(jnp.int32, (1, MOE_WIN), 1)
    for e in range(n_e):
        lo, hi = bnd_ref[(b * n_e + e) * LANES + k], bnd_ref[(b * n_e + e) * LANES + k + 1]
        n_win = _window_count(lo, hi, starts[e])

        @pl.when(n_win > 1)
        def _(e=e, n_win=n_win):
            rank_e = rank[:, e * MOE_WIN:(e + 1) * MOE_WIN]

            def extra(w, c):
                r = pl.multiple_of(starts[e] + w * MOE_WIN, 16)
                hit = rank_e == (lane_w + (r + 1)).astype(f32)
                o_ref[...] += _mm(jnp.where(hit, 1.0, 0.0).astype(bf16),
                                  y_ref[pl.ds(pl.multiple_of(e * rows + r, 16), MOE_WIN), :])
                return c

            lax.fori_loop(1, n_win, extra, 0)


def _combine(y, cp, bnd):
    B, E, rows, D = y.shape
    T = cp.shape[2]
    nc = T // MOE_CHUNK
    expand = jnp.asarray(np.kron(np.eye(E), np.ones((1, MOE_WIN))), bf16)
    return pl.pallas_call(
        functools.partial(_combine_kernel, E, rows),
        grid_spec=pltpu.PrefetchScalarGridSpec(
            num_scalar_prefetch=1,
            grid=(B, nc),
            in_specs=[
                pl.BlockSpec((None, E * rows, D), lambda b, k, s: (b, 0, 0), pipeline_mode=pl.Buffered(1)),
                pl.BlockSpec((None, MOE_CHUNK, E), lambda b, k, s: (b, k, 0)),
                pl.BlockSpec(expand.shape, lambda b, k, s: (0, 0)),
            ],
            out_specs=pl.BlockSpec((None, MOE_CHUNK, D), lambda b, k, s: (b, k, 0)),
        ),
        out_shape=jax.ShapeDtypeStruct((B, T, D), f32),
        compiler_params=_params("arbitrary", "arbitrary"),
        name="moe_combine",
    )(bnd, y.reshape(B, E * rows, D), cp.transpose(0, 2, 1), expand)


def _final_kernel(x_ref, moe_ref, mod_ref, o_ref):
    o_ref[...] = x_ref[...] + mod_ref[5:6, :] * moe_ref[...]


def _final(x, moe, mod, S):
    B, T, D = x.shape
    return pl.pallas_call(
        _final_kernel,
        grid=(B, S // TT),
        in_specs=[
            pl.BlockSpec((None, TT, D), lambda b, i: (b, i, 0)),
            pl.BlockSpec((None, TT, D), lambda b, i: (b, i, 0)),
            pl.BlockSpec((None, None, 6, D), lambda b, i: (b, 0, 0, 0)),
        ],
        out_specs=pl.BlockSpec((None, TT, D), lambda b, i: (b, i, 0)),
        out_shape=jax.ShapeDtypeStruct((B, S, D), f32),
        compiler_params=_params("arbitrary", "arbitrary"),
        name="final_residual",
    )(x, moe, mod)


def _rope_tables(S, L):
    t = jnp.arange(S)
    row, col = t // GRID_W, t % GRID_W
    f = 16
    inv = ROPE_BASE ** (-jnp.arange(f, dtype=f32) / f)
    ar = row.astype(f32)[:, None] * inv
    ac = col.astype(f32)[:, None] * inv
    cos = jnp.concatenate([jnp.cos(ar), jnp.cos(ar), jnp.cos(ac), jnp.cos(ac)], axis=1)
    sin = jnp.concatenate([-jnp.sin(ar), jnp.sin(ar), -jnp.sin(ac), jnp.sin(ac)], axis=1)
    cos = jnp.concatenate([jnp.tile(cos, (1, 2)), jnp.ones((L, LANES), f32)], axis=0)
    sin = jnp.concatenate([jnp.tile(sin, (1, 2)), jnp.zeros((L, LANES), f32)], axis=0)
    return cos, sin


def _pad_heads(w, n_heads, width, padded):
    lead = w.shape[:-1]
    w = w.reshape(lead + (n_heads, width))
    w = jnp.pad(w, [(0, 0)] * len(lead) + [(0, 0), (0, padded - width)])
    return w.reshape(lead + (n_heads * padded,))


def _layer_weights(l, D, w_in, norm2_g, na_q_g, na_k_g, mla_q_norm, mla_w_uq, mla_kv_norm, mla_w_ukv,
                   mla_q_g, mla_k_g, swa_q_g, swa_k_g, w_na_o, w_mla_o, w_swa_o, w_o, router):
    o = np.cumsum((0, NA_W, NA_W, NA_W, MLA_Q_RANK, MLA_KV_RANK, MLA_ROPE, SWA_W, SWA_KV_W, SWA_KV_W))
    o_naq, o_nak, o_nav, o_cq, o_ckv, o_kr, o_swq, o_swk, o_swv, o_gl = (int(v) for v in o)
    wi = w_in[l]
    swq = wi[:, o_swq:o_swq + SWA_W].reshape(D, SWA_HEADS, SWA_DIM)[:, SWA_HEAD_ORDER, :].reshape(D, SWA_W)
    kr = jnp.pad(wi[:, o_kr:o_kr + MLA_ROPE], ((0, 0), (0, LANES - MLA_ROPE)))
    w_all = jnp.concatenate([
        wi[:, o_naq:o_naq + 3 * NA_W], swq, wi[:, o_cq:o_cq + MLA_Q_RANK], wi[:, o_ckv:o_ckv + MLA_KV_RANK], kr,
        wi[:, o_swk:o_swk + 2 * SWA_KV_W], wi[:, o_gl:]], axis=1).astype(bf16)
    ukv = mla_w_ukv[l].reshape(MLA_KV_RANK, MLA_HEADS, MLA_NOPE + MLA_V)
    swa_o = w_swa_o[l].reshape(SWA_HEADS, SWA_DIM, D)[SWA_HEAD_ORDER, :, :].reshape(SWA_W, D)
    r_hi = router[l].astype(bf16)
    r_lo = (router[l] - r_hi.astype(f32)).astype(bf16)
    return dict(
        w_all=w_all,
        naq_g=jnp.tile(na_q_g[l], NA_HEADS)[None], nak_g=jnp.tile(na_k_g[l], NA_HEADS)[None],
        swq_g=jnp.tile(swa_q_g[l], SWA_HEADS)[None], swk_g=jnp.tile(swa_k_g[l], SWA_KV_HEADS)[None],
        qn=mla_q_norm[l][None], kvn=mla_kv_norm[l][None],
        wuq=_pad_heads(mla_w_uq[l], MLA_HEADS, MLA_QK, MLA_PAD).astype(bf16),
        wuk=ukv[:, :, :MLA_NOPE].reshape(MLA_KV_RANK, MLA_HEADS * MLA_NOPE).astype(bf16),
        wuv=ukv[:, :, MLA_NOPE:].reshape(MLA_KV_RANK, MLA_W).astype(bf16),
        qg=jnp.tile(jnp.pad(mla_q_g[l], (0, MLA_PAD - MLA_QK)), MLA_HEADS)[None],
        kg=jnp.tile(jnp.pad(mla_k_g[l], (0, MLA_PAD - MLA_QK)), MLA_HEADS)[None],
        wna=w_na_o[l].astype(bf16), wmla=w_mla_o[l].astype(bf16), wswa=swa_o.astype(bf16),
        wo=w_o[l].astype(bf16), g2=norm2_g[l][None], router=jnp.concatenate([r_hi, r_lo], axis=1),
    )


def kernel(x, c, ctx, c_ctx, norm1_g, norm2_g, w_ada, b_ada, w_in, na_q_g, na_k_g, na_rpb, mla_q_norm, mla_w_uq, mla_kv_norm, mla_w_ukv, mla_q_g, mla_k_g, swa_q_g, swa_k_g, swa_sink, w_na_o, w_mla_o, w_swa_o, w_o, router, w_gate, w_up, w_down):
    B, S, D = x.shape
    L = ctx.shape[1]
    depth = w_in.shape[0]
    E = router.shape[2]
    T = S + L
    n_lat = S // TT
    rows = S // GRID_W
    assert S % TT == 0 and L % TM == 0 and L <= TT and rows >= NA_BAND_ROWS and S >= SWA_KEYS and B < 16
    cap_s = CAPACITY_FACTOR * S // E
    cap_l = CAPACITY_FACTOR * L // E

    cpad = jnp.zeros((16, D), f32).at[:B].set(c).at[B].set(c_ctx)
    mod = _ada(cpad, w_ada, b_ada).reshape(depth, 16, 6, D)
    mod = jnp.stack([mod[:, :B], jnp.broadcast_to(mod[:, B:B + 1], (depth, B, 6, D))], axis=2)

    cos, sin = _rope_tables(S, L)
    gmat = jnp.asarray(np.kron(np.eye(NA_HEADS), np.full((NA_DIM, NA_DIM), 1.0 / NA_DIM)), bf16)

    xs = jnp.concatenate([x, ctx], axis=1)
    moe = None
    for l in range(depth):
        lw = _layer_weights(l, D, w_in, norm2_g, na_q_g, na_k_g, mla_q_norm, mla_w_uq, mla_kv_norm, mla_w_ukv,
                            mla_q_g, mla_k_g, swa_q_g, swa_k_g, w_na_o, w_mla_o, w_swa_o, w_o, router)
        modp = mod[l - 1] if l > 0 else None
        xs, proj, gl = _in_proj(xs, moe, modp, mod[l], norm1_g[l][None], lw["w_all"], n_lat)
        qka, mq, mk, mv = _prep(proj, cos, sin, gmat, lw)
        ona = _na(qka, proj, _na_bias(na_rpb[l], rows), S)
        omla = _mla(mq, mk, mv, S)
        oswa = _swa(qka, proj, swa_sink[l] * LOG2E, S)
        xs, h2, aff, affx = _post(ona, omla, oswa, gl, xs, mod[l], lw, n_lat)
        cp, bnd = _topk(aff.transpose(0, 2, 1), S, cap_s, cap_l)
        bnd = bnd.astype(jnp.int32).reshape(-1)
        xg = _gather(h2, affx, cp, bnd, cap_s + cap_l)
        y = _ffn(xg, w_gate, w_up, w_down, l, cap_s + cap_l)
        moe = _combine(y, cp, bnd)
    return _final(xs, moe, mod[depth - 1], S)
```

```python
import functools

import numpy as np
import jax
import jax.numpy as jnp
from jax import lax
from jax.experimental import pallas as pl
from jax.experimental.pallas import tpu as pltpu

GRID_W = 64
NA_HEADS, NA_DIM, NA_WIN_ROWS, NA_WIN_COLS = 6, 64, 8, 16
MLA_HEADS, MLA_Q_RANK, MLA_KV_RANK, MLA_NOPE, MLA_ROPE, MLA_V = 4, 256, 128, 128, 64, 128
MLA_QK = MLA_NOPE + MLA_ROPE
MLA_PAD = 256
SWA_HEADS, SWA_KV_HEADS, SWA_DIM, SWA_WINDOW = 6, 2, 64, 128
CAPACITY_FACTOR = 2
N_BRANCH = 3
ROPE_BASE = 10000.0
EPS = 1e-6

NA_W = NA_HEADS * NA_DIM
SWA_W = SWA_HEADS * SWA_DIM
SWA_KV_W = SWA_KV_HEADS * SWA_DIM
MLA_W = MLA_HEADS * MLA_V

LANES = 128
TM = 256
TT = 512
MLA_TQ = 1024
MLA_TK = 1024
LOG2E = 1.4426950408889634
NA_TILE_ROWS = TM // GRID_W
NA_BAND_ROWS = NA_TILE_ROWS + NA_WIN_ROWS - 1
NA_BAND = NA_BAND_ROWS * GRID_W
SWA_KEYS = TM + 2 * SWA_WINDOW
MOE_CHUNK = 256
MOE_WIN = 64
NEG = -1e30
VMEM_LIMIT = 56 * 1024 * 1024

P_NAQ, P_NAK, P_NAV, P_SWQ = 0, 384, 768, 1152
P_CQ, P_CKV, P_KR, P_SWK, P_SWV = 1536, 1792, 1920, 2048, 2176
PROJ_W = 2304
N_CHUNK = 768
A_NAQ, A_NAK, A_SWQ, A_SWK = 0, 384, 768, 1152
QKA_W = 1280
SWA_HEAD_ORDER = (0, 3, 1, 4, 2, 5)

bf16 = jnp.bfloat16
f32 = jnp.float32


def _mm(a, b):
    return jnp.dot(a, b, preferred_element_type=f32)


def _nt(a, b):
    return lax.dot_general(a, b, (((1,), (1,)), ((), ())), preferred_element_type=f32)


def _params(*sem):
    return pltpu.CompilerParams(dimension_semantics=sem, vmem_limit_bytes=VMEM_LIMIT)


def _ada_kernel(c_ref, w_ref, b_ref, o_ref):
    a = c_ref[...]
    a = (a * jax.nn.sigmoid(a)).astype(bf16)
    o_ref[...] = _mm(a, w_ref[...].astype(bf16)) + b_ref[...]


def _ada(cpad, w_ada, b_ada):
    L, D, N = w_ada.shape
    tn = 1536
    return pl.pallas_call(
        _ada_kernel,
        grid=(L, N // tn),
        in_specs=[
            pl.BlockSpec((16, D), lambda l, j: (0, 0)),
            pl.BlockSpec((None, D, tn), lambda l, j: (l, 0, j)),
            pl.BlockSpec((None, 1, tn), lambda l, j: (l, 0, j)),
        ],
        out_specs=pl.BlockSpec((None, 16, tn), lambda l, j: (l, 0, j)),
        out_shape=jax.ShapeDtypeStruct((L, 16, N), f32),
        compiler_params=_params("arbitrary", "arbitrary"),
        name="ada",
    )(cpad, w_ada, b_ada.reshape(L, 1, N))


def _in_proj_kernel(has_moe, n_gl_chunks, *refs):
    if has_moe:
        x_ref, moe_ref, modp_ref = refs[:3]
        refs = refs[3:]
        xo_ref = refs[-7]
        x = x_ref[...] + modp_ref[5:6, :] * moe_ref[...]
        xo_ref[...] = x
    else:
        x_ref = refs[0]
        refs = refs[1:]
        x = x_ref[...]
    mod_ref, g_ref, w_ref = refs[:3]
    qk_consts = refs[3:17]
    qka_ref, mq_ref, mk_ref, mv_ref, val_ref, gl_ref = refs[-6:]
    ms = jnp.mean(x * x, axis=-1, keepdims=True)
    h = x * lax.rsqrt(ms + EPS) * g_ref[...]
    h = (h * (1.0 + mod_ref[1:2, :]) + mod_ref[0:1, :]).astype(bf16)

    def proj(lo, width):
        return _mm(h, w_ref[:, lo:lo + width])

    val_ref[:, 0:NA_W] = proj(P_NAV, NA_W).astype(bf16)
    val_ref[:, NA_W:NA_W + SWA_KV_W] = proj(P_SWV, SWA_KV_W).astype(bf16)
    _qk_prep(proj, *qk_consts, qka_ref, mq_ref, mk_ref, mv_ref)
    for c in range(n_gl_chunks):
        sl = slice(c * N_CHUNK, (c + 1) * N_CHUNK)
        gl_ref[:, sl] = proj(PROJ_W + c * N_CHUNK, N_CHUNK).astype(bf16)


def _mod_spec(n_lat, D):
    return pl.BlockSpec((None, None, 6, D), lambda b, i: (b, jnp.where(i >= n_lat, 1, 0), 0, 0))


def _in_proj(x, moe, modp, mod, g1, w, cos, sin, gmat, lw, n_lat):
    B, T, D = x.shape
    nt = pl.cdiv(T, TT)
    has_moe = moe is not None
    n_gl_chunks = N_BRANCH * D // N_CHUNK
    tok = pl.BlockSpec((None, TT, D), lambda b, i: (b, i, 0))

    def const(a):
        return pl.BlockSpec(a.shape, lambda b, i: (0,) * a.ndim)

    in_specs = [tok]
    args = [x]
    if has_moe:
        in_specs += [tok, _mod_spec(n_lat, D)]
        args += [moe, modp]
    consts = [gmat, lw["naq_g"], lw["nak_g"], lw["swq_g"], lw["swk_g"], lw["qn"], lw["wuq"], lw["qg"],
              lw["kvn"], lw["wuk"], lw["wuv"], lw["kg"]]
    in_specs += [
        _mod_spec(n_lat, D),
        pl.BlockSpec((1, D), lambda b, i: (0, 0)),
        pl.BlockSpec(w.shape, lambda b, i: (0, 0), pipeline_mode=pl.Buffered(1)),
        pl.BlockSpec((TT, LANES), lambda b, i: (i, 0)),
        pl.BlockSpec((TT, LANES), lambda b, i: (i, 0)),
    ] + [const(a) for a in consts]
    args += [mod, g1, w, cos, sin] + consts
    widths = (QKA_W, MLA_HEADS * MLA_PAD, MLA_HEADS * MLA_PAD, MLA_HEADS * MLA_PAD, NA_W + SWA_KV_W, N_BRANCH * D)
    out_specs = [pl.BlockSpec((None, TT, wd), lambda b, i: (b, i, 0)) for wd in widths]
    out_shape = [jax.ShapeDtypeStruct((B, T, wd), bf16) for wd in widths]
    if has_moe:
        out_specs = [tok] + out_specs
        out_shape = [jax.ShapeDtypeStruct((B, T, D), f32)] + out_shape
    res = pl.pallas_call(
        functools.partial(_in_proj_kernel, has_moe, n_gl_chunks),
        grid=(B, nt),
        in_specs=in_specs,
        out_specs=out_specs,
        out_shape=out_shape,
        compiler_params=_params("arbitrary", "arbitrary"),
        name="in_proj",
    )(*args)
    if has_moe:
        return res
    return [x] + list(res)


def _qk_prep(proj, cos_ref, sin_ref, gmat_ref, naq_g, nak_g, swq_g, swk_g,
             qn_ref, wuq_ref, qg_ref, kvn_ref, wuk_ref, wuv_ref, kg_ref,
             qka_ref, mq_ref, mk_ref, mv_ref):
    cos = cos_ref[...]
    sin = sin_ref[...]
    lane = lax.broadcasted_iota(jnp.int32, cos.shape, 1)
    first = (lane & 16) == 0

    def rope(x):
        partner = jnp.where(first, pltpu.roll(x, LANES - 16, 1), pltpu.roll(x, 16, 1))
        return x * cos + partner * sin

    def headnorm(x, gain):
        w = x.shape[1]
        ms = _mm((x * x).astype(bf16), gmat_ref[:w, :w])
        return x * lax.rsqrt(ms + EPS) * gain

    def rmsnorm(x, gain):
        ms = jnp.mean(x * x, axis=-1, keepdims=True)
        return x * lax.rsqrt(ms + EPS) * gain

    na_scale = NA_DIM ** -0.5 * LOG2E
    naq = headnorm(proj(P_NAQ, NA_W), naq_g[...])
    qka_ref[:, A_NAQ:A_NAQ + NA_W] = (naq * na_scale).astype(bf16)
    nak = headnorm(proj(P_NAK, NA_W), nak_g[...])
    qka_ref[:, A_NAK:A_NAK + NA_W] = nak.astype(bf16)

    sw_scale = SWA_DIM ** -0.5 * LOG2E
    swq = headnorm(proj(P_SWQ, SWA_W), swq_g[...])
    for p in range(SWA_W // LANES):
        sl = slice(p * LANES, (p + 1) * LANES)
        qka_ref[:, A_SWQ + p * LANES:A_SWQ + (p + 1) * LANES] = (rope(swq[:, sl]) * sw_scale).astype(bf16)
    swk = headnorm(proj(P_SWK, SWA_KV_W), swk_g[...])
    qka_ref[:, A_SWK:A_SWK + SWA_KV_W] = rope(swk).astype(bf16)

    mla_scale = MLA_QK ** -0.5 * LOG2E
    lat = proj(P_CQ, MLA_Q_RANK + MLA_KV_RANK + LANES)
    cq = rmsnorm(lat[:, :MLA_Q_RANK], qn_ref[...]).astype(bf16)
    q = _mm(cq, wuq_ref[...])
    ckv = rmsnorm(lat[:, MLA_Q_RANK:MLA_Q_RANK + MLA_KV_RANK], kvn_ref[...]).astype(bf16)
    kn = _mm(ckv, wuk_ref[...])
    v = _mm(ckv, wuv_ref[...]).astype(bf16)
    ones_col = jnp.where(lax.broadcasted_iota(jnp.int32, (v.shape[0], LANES), 1) == 0, 1.0, 0.0).astype(bf16)
    for h in range(MLA_HEADS):
        mv_ref[:, h * MLA_PAD:h * MLA_PAD + MLA_V] = v[:, h * MLA_V:(h + 1) * MLA_V]
        mv_ref[:, h * MLA_PAD + MLA_V:(h + 1) * MLA_PAD] = ones_col
    kr = lat[:, MLA_Q_RANK + MLA_KV_RANK:]
    kr_ss = jnp.sum(kr * kr, axis=-1, keepdims=True)
    for h in range(MLA_HEADS):
        o = h * MLA_PAD
        qh = q[:, o:o + MLA_PAD]
        r = lax.rsqrt(jnp.sum(qh * qh, axis=-1, keepdims=True) * (1.0 / MLA_QK) + EPS)
        qh = qh * r * qg_ref[:, o:o + MLA_PAD]
        mq_ref[:, o:o + LANES] = (qh[:, :LANES] * mla_scale).astype(bf16)
        mq_ref[:, o + LANES:o + MLA_PAD] = (rope(qh[:, LANES:]) * mla_scale).astype(bf16)
        kh = kn[:, h * MLA_NOPE:(h + 1) * MLA_NOPE]
        r = lax.rsqrt((jnp.sum(kh * kh, axis=-1, keepdims=True) + kr_ss) * (1.0 / MLA_QK) + EPS)
        mk_ref[:, o:o + LANES] = (kh * r * kg_ref[:, o:o + LANES]).astype(bf16)
        mk_ref[:, o + LANES:o + MLA_PAD] = rope(kr * r * kg_ref[:, o + LANES:o + MLA_PAD]).astype(bf16)


def _half_masks():
    lane = lax.broadcasted_iota(jnp.int32, (1, LANES), 1)
    lo = jnp.where(lane < 64, 1.0, 0.0).astype(bf16)
    return lo, (1.0 - lo.astype(f32)).astype(bf16)


def _pair_attend(qp, parts, masks, sinks=None):
    n_q = qp.shape[0]
    q2 = jnp.concatenate([qp * masks[0], qp * masks[1]], axis=0)
    scores = []
    for k, _, bias in parts:
        s = _nt(q2, k)
        scores.append(s if bias is None else s + bias)
    m = functools.reduce(jnp.maximum, [jnp.max(s, axis=-1, keepdims=True) for s in scores])
    if sinks is not None:
        first = lax.broadcasted_iota(jnp.int32, (2 * n_q, 1), 0) < n_q
        sink = jnp.where(first, sinks[0], sinks[1])
        m = jnp.maximum(m, sink)
    ps = [jnp.exp2(s - m) for s in scores]
    l = functools.reduce(jnp.add, [jnp.sum(p, axis=-1, keepdims=True) for p in ps])
    if sinks is not None:
        l = l + jnp.exp2(sink - m)
    o = functools.reduce(jnp.add, [_mm(p.astype(bf16), v) for p, (_, v, _) in zip(ps, parts)]) / l
    lane = lax.broadcasted_iota(jnp.int32, (1, LANES), 1)
    return jnp.where(lane < 64, o[:n_q], o[n_q:])


def _na_kernel(n_lat, S, rows, q_ref, k_ref, v_ref, bias_ref, o_ref):
    i = pl.program_id(1)
    masks = _half_masks()
    T = k_ref.shape[0]

    @pl.when(i < n_lat)
    def _():
        start = GRID_W * jnp.clip(NA_TILE_ROWS * i - NA_WIN_ROWS // 2, 0, rows - NA_BAND_ROWS)
        start = pl.multiple_of(start, GRID_W)
        for p in range(NA_W // LANES):
            sl = slice(p * LANES, (p + 1) * LANES)
            band = (k_ref[pl.ds(start, NA_BAND), sl], v_ref[pl.ds(start, NA_BAND), sl],
                    bias_ref[2 * p:2 * p + 2].reshape(2 * TM, NA_BAND))
            ctx = (k_ref[S:T, sl], v_ref[S:T, sl], None)
            o_ref[:, sl] = _pair_attend(q_ref[:, sl], [band, ctx], masks).astype(bf16)

    @pl.when(i >= n_lat)
    def _():
        for p in range(NA_W // LANES):
            sl = slice(p * LANES, (p + 1) * LANES)
            ctx = (k_ref[S:T, sl], v_ref[S:T, sl], None)
            o_ref[:, sl] = _pair_attend(q_ref[:, sl], [ctx], masks).astype(bf16)


def _na(qka, proj, bias, S):
    B, T, _ = qka.shape
    nt, n_lat, rows = T // TM, S // TM, S // GRID_W

    def bias_idx(b, i):
        return (jnp.where(i == 0, 0, jnp.where(i >= n_lat - 1, 2, 1)), 0, 0, 0)

    return pl.pallas_call(
        functools.partial(_na_kernel, n_lat, S, rows),
        grid=(B, nt),
        in_specs=[
            pl.BlockSpec((None, TM, NA_W), lambda b, i: (b, i, A_NAQ // NA_W)),
            pl.BlockSpec((None, T, NA_W), lambda b, i: (b, 0, A_NAK // NA_W)),
            pl.BlockSpec((None, T, NA_W), lambda b, i: (b, 0, 0)),
            pl.BlockSpec((None, NA_HEADS, TM, NA_BAND), bias_idx),
        ],
        out_specs=pl.BlockSpec((None, TM, NA_W), lambda b, i: (b, i, 0)),
        out_shape=jax.ShapeDtypeStruct((B, T, NA_W), bf16),
        compiler_params=_params("arbitrary", "arbitrary"),
        name="na_attn",
    )(qka, qka, proj, bias)


def _na_bias(rpb, rows):
    n_tiles = rows // NA_TILE_ROWS
    col = np.arange(GRID_W)
    c0 = np.clip(col - NA_WIN_COLS // 2, 0, GRID_W - NA_WIN_COLS)
    col_ok = (col[None, :] >= c0[:, None]) & (col[None, :] < c0[:, None] + NA_WIN_COLS)
    dc = np.clip(col[None, :] - col[:, None] + NA_WIN_COLS - 1, 0, 2 * NA_WIN_COLS - 2)
    tz = jnp.where(col_ok, rpb[:, :, dc] * LOG2E, NEG)
    dr0 = np.zeros((3, NA_TILE_ROWS), np.int64)
    ok = np.zeros((3, NA_TILE_ROWS, NA_BAND_ROWS), bool)
    for ty, rt in enumerate((0, 1, n_tiles - 1)):
        bs = int(np.clip(NA_TILE_ROWS * rt - NA_WIN_ROWS // 2, 0, rows - NA_BAND_ROWS))
        for qr in range(NA_TILE_ROWS):
            r = NA_TILE_ROWS * rt + qr
            s_r = int(np.clip(r - NA_WIN_ROWS // 2, 0, rows - NA_WIN_ROWS))
            dr0[ty, qr] = bs - r + NA_WIN_ROWS - 1
            ok[ty, qr] = (bs + np.arange(NA_BAND_ROWS) >= s_r) & (bs + np.arange(NA_BAND_ROWS) < s_r + NA_WIN_ROWS)
    front = int(max(0, -dr0.min()))
    back = int(max(0, dr0.max() + NA_BAND_ROWS - (2 * NA_WIN_ROWS - 1)))
    tzp = jnp.pad(tz, ((0, 0), (front, back), (0, 0), (0, 0)), constant_values=NEG)
    blk = jnp.stack([jnp.stack([tzp[:, int(dr0[ty, qr]) + front:int(dr0[ty, qr]) + front + NA_BAND_ROWS]
                                for qr in range(NA_TILE_ROWS)], axis=1) for ty in range(3)], axis=0)
    blk = jnp.where(ok[:, None, :, :, None, None], blk, NEG)
    blk = blk.transpose(0, 1, 2, 4, 3, 5)
    return blk.reshape(3, NA_HEADS, TM, NA_BAND)


def _mla_kernel(n_q, S, q_ref, k_ref, v_ref, o_ref):
    i = pl.program_id(2)
    T = k_ref.shape[0]

    def attend(q, bounds):
        m_run = acc = None
        for lo, hi in bounds:
            s = _nt(q, k_ref[lo:hi, :])
            m_new = jnp.max(s, axis=-1, keepdims=True)
            if m_run is not None:
                m_new = jnp.maximum(m_run, m_new)
            pv = _mm(jnp.exp2(s - m_new).astype(bf16), v_ref[lo:hi, :])
            acc = pv if acc is None else acc * jnp.exp2(m_run - m_new) + pv
            m_run = m_new
        return (acc[:, :MLA_V] / acc[:, MLA_V:MLA_V + 1]).astype(bf16)

    @pl.when(i < n_q)
    def _():
        chunks = [(lo, lo + MLA_TK) for lo in range(0, S, MLA_TK)] + [(S, T)]
        o_ref[...] = attend(q_ref[...], chunks)

    @pl.when(i >= n_q)
    def _():
        o_ref[0:T - S, :] = attend(q_ref[0:T - S, :], [(S, T)])


def _mla(mq, mk, mv, S):
    B, T, _ = mq.shape
    n_q = S // MLA_TQ
    assert S % MLA_TQ == 0 and S % MLA_TK == 0 and T - S <= MLA_TQ
    return pl.pallas_call(
        functools.partial(_mla_kernel, n_q, S),
        grid=(B, MLA_HEADS, n_q + 1),
        in_specs=[
            pl.BlockSpec((None, MLA_TQ, MLA_PAD), lambda b, h, i: (b, i, h)),
            pl.BlockSpec((None, T, MLA_PAD), lambda b, h, i: (b, 0, h)),
            pl.BlockSpec((None, T, MLA_PAD), lambda b, h, i: (b, 0, h)),
        ],
        out_specs=pl.BlockSpec((None, MLA_TQ, MLA_V), lambda b, h, i: (b, i, h)),
        out_shape=jax.ShapeDtypeStruct((B, T, MLA_W), bf16),
        compiler_params=_params("arbitrary", "arbitrary", "arbitrary"),
        name="mla_attn",
    )(mq, mk, mv)


def _swa_kernel(n_lat, S, sink_ref, q_ref, k_ref, v_ref, o_ref):
    i = pl.program_id(1)
    masks = _half_masks()
    T = k_ref.shape[0]
    kc, vc = k_ref[S:T, :], v_ref[S:T, :]

    def sinks(p):
        return (sink_ref[SWA_HEAD_ORDER[2 * p]], sink_ref[SWA_HEAD_ORDER[2 * p + 1]])

    @pl.when(i < n_lat)
    def _():
        start = pl.multiple_of(jnp.clip(i * TM - SWA_WINDOW, 0, S - SWA_KEYS), SWA_WINDOW)
        row = lax.broadcasted_iota(jnp.int32, (2 * TM, SWA_KEYS), 0)
        qpos = i * TM + jnp.where(row >= TM, row - TM, row)
        kpos = start + lax.broadcasted_iota(jnp.int32, (2 * TM, SWA_KEYS), 1)
        band = jnp.where(jnp.abs(qpos - kpos) <= SWA_WINDOW, 0.0, NEG)
        kb, vb = k_ref[pl.ds(start, SWA_KEYS), :], v_ref[pl.ds(start, SWA_KEYS), :]
        for p in range(SWA_W // LANES):
            sl = slice(p * LANES, (p + 1) * LANES)
            parts = [(kb, vb, band), (kc, vc, None)]
            o_ref[:, sl] = _pair_attend(q_ref[:, sl], parts, masks, sinks(p)).astype(bf16)

    @pl.when(i >= n_lat)
    def _():
        for p in range(SWA_W // LANES):
            sl = slice(p * LANES, (p + 1) * LANES)
            o_ref[:, sl] = _pair_attend(q_ref[:, sl], [(kc, vc, None)], masks, sinks(p)).astype(bf16)


def _swa(qka, proj, sink, S):
    B, T, _ = qka.shape
    nt, n_lat = T // TM, S // TM
    return pl.pallas_call(
        functools.partial(_swa_kernel, n_lat, S),
        grid=(B, nt),
        in_specs=[
            pl.BlockSpec(memory_space=pltpu.SMEM),
            pl.BlockSpec((None, TM, SWA_W), lambda b, i: (b, i, A_SWQ // SWA_W)),
            pl.BlockSpec((None, T, SWA_KV_W), lambda b, i: (b, 0, A_SWK // SWA_KV_W)),
            pl.BlockSpec((None, T, SWA_KV_W), lambda b, i: (b, 0, NA_W // SWA_KV_W)),
        ],
        out_specs=pl.BlockSpec((None, TM, SWA_W), lambda b, i: (b, i, 0)),
        out_shape=jax.ShapeDtypeStruct((B, T, SWA_W), bf16),
        compiler_params=_params("arbitrary", "arbitrary"),
        name="swa_attn",
    )(sink, qka, qka, proj)


def _post_kernel(ona_ref, omla_ref, oswa_ref, gl_ref, x_ref, mod_ref, wna_ref, wmla_ref, wswa_ref, wo_ref,
                 g2_ref, rt_ref, xo_ref, h2_ref, aff_ref, affx_ref):
    D = x_ref.shape[1]
    merged = None
    for j, (o_ref, w_ref) in enumerate(((ona_ref, wna_ref), (omla_ref, wmla_ref), (oswa_ref, wswa_ref))):
        y = _mm(o_ref[...], w_ref[...])
        g = 0.5 * jnp.tanh(0.5 * gl_ref[:, j * D:(j + 1) * D].astype(f32)) + 0.5
        merged = g * y if merged is None else merged + g * y
    res = _mm(merged.astype(bf16), wo_ref[...])
    x = x_ref[...] + mod_ref[2:3, :] * res
    xo_ref[...] = x
    ms = jnp.mean(x * x, axis=-1, keepdims=True)
    h2 = x * lax.rsqrt(ms + EPS) * g2_ref[...]
    h2 = h2 * (1.0 + mod_ref[4:5, :]) + mod_ref[3:4, :]
    h_hi = h2.astype(bf16)
    h2_ref[...] = h_hi
    h_lo = (h2 - h_hi.astype(f32)).astype(bf16)
    parts = _mm(h_hi, rt_ref[...]) + _mm(h_lo, rt_ref[...])
    n_e = aff_ref.shape[1]
    logits = parts[:, :n_e] + parts[:, n_e:]
    e = jnp.exp(logits - jnp.max(logits, axis=1, keepdims=True))
    aff = e / jnp.sum(e, axis=1, keepdims=True)
    aff_ref[...] = aff
    p0 = aff.astype(bf16)
    r1 = aff - p0.astype(f32)
    p1 = r1.astype(bf16)
    p2 = (r1 - p1.astype(f32)).astype(bf16)
    pad = jnp.zeros((aff.shape[0], LANES - 3 * n_e), bf16)
    affx_ref[...] = jnp.concatenate([p0, p1, p2, pad], axis=1)


def _post(ona, omla, oswa, gl, x, mod, lw, n_lat):
    B, T, D = x.shape
    nt = pl.cdiv(T, TT)
    E = lw["router"].shape[1] // 2

    def tok(w):
        return pl.BlockSpec((None, TT, w), lambda b, i: (b, i, 0))

    def const(a):
        return pl.BlockSpec(a.shape, lambda b, i: (0,) * a.ndim)

    consts = [lw["wna"], lw["wmla"], lw["wswa"], lw["wo"], lw["g2"], lw["router"]]
    return pl.pallas_call(
        _post_kernel,
        grid=(B, nt),
        in_specs=[tok(NA_W), tok(MLA_W), tok(SWA_W), tok(N_BRANCH * D), tok(D), _mod_spec(n_lat, D)]
        + [const(a) for a in consts],
        out_specs=[tok(D), tok(D), tok(E), tok(LANES)],
        out_shape=[jax.ShapeDtypeStruct((B, T, D), f32), jax.ShapeDtypeStruct((B, T, D), bf16),
                   jax.ShapeDtypeStruct((B, T, E), f32), jax.ShapeDtypeStruct((B, T, LANES), bf16)],
        compiler_params=_params("arbitrary", "arbitrary"),
        name="post_attn",
    )(ona, omla, oswa, gl, x, mod, *consts)


def _lane_cumsum(mask, tri):
    E, n = mask.shape
    carry = jnp.zeros((E, 1), f32)
    outs = []
    for k in range(n // LANES):
        w = _mm(mask[:, k * LANES:(k + 1) * LANES].astype(bf16), tri) + carry
        outs.append(w)
        carry = w[:, LANES - 1:LANES]
    return jnp.concatenate(outs, axis=1)


def _select_slots(aff, cap, base, tri):
    bits = lax.bitcast_convert_type(aff, jnp.int32)
    thr = jnp.zeros((aff.shape[0], 1), jnp.int32)
    for bit in range(30, -1, -1):
        cand = thr | (1 << bit)
        cnt = jnp.sum(jnp.where(bits >= cand, 1.0, 0.0), axis=1, keepdims=True)
        thr = jnp.where(cnt >= cap, cand, thr)
    gt = jnp.where(bits > thr, 1.0, 0.0)
    eq = jnp.where(bits == thr, 1.0, 0.0)
    need = cap - jnp.sum(gt, axis=1, keepdims=True)
    sel = jnp.maximum(gt, jnp.where(_lane_cumsum(eq, tri) <= need, eq, 0.0))
    return jnp.where(sel > 0.0, _lane_cumsum(sel, tri) + base, 0.0)


def _topk_kernel(S, cap_s, cap_l, aff_ref, tri_ref, before_ref, cp_ref, bnd_ref):
    T = aff_ref.shape[1]
    tri = tri_ref[...]
    cp_ref[:, 0:S] = _select_slots(aff_ref[:, 0:S], cap_s, 0.0, tri)
    cp_ref[:, S:T] = _select_slots(aff_ref[:, S:T], cap_l, float(cap_s), tri)
    bnd_ref[...] = _mm(jnp.where(cp_ref[...] > 0.0, 1.0, 0.0).astype(bf16), before_ref[...])


def _topk(aff, S, cap_s, cap_l):
    B, E, T = aff.shape
    tri = jnp.asarray(np.triu(np.ones((LANES, LANES), np.float32)), bf16)
    before = jnp.asarray(np.arange(T)[:, None] < MOE_CHUNK * np.arange(LANES)[None, :], bf16)
    return pl.pallas_call(
        functools.partial(_topk_kernel, S, cap_s, cap_l),
        grid=(B,),
        in_specs=[pl.BlockSpec((None, E, T), lambda b: (b, 0, 0)),
                  pl.BlockSpec((LANES, LANES), lambda b: (0, 0)),
                  pl.BlockSpec((T, LANES), lambda b: (0, 0))],
        out_specs=[pl.BlockSpec((None, E, T), lambda b: (b, 0, 0)),
                   pl.BlockSpec((None, E, LANES), lambda b: (b, 0, 0))],
        out_shape=[jax.ShapeDtypeStruct((B, E, T), f32), jax.ShapeDtypeStruct((B, E, LANES), f32)],
        compiler_params=_params("arbitrary"),
        name="expert_select",
    )(aff, tri, before)


def _window_count(lo, hi, r0):
    return jnp.where(hi > lo, lax.div(hi - r0 + (MOE_WIN - 1), MOE_WIN), 0)


def _gather_kernel(n_e, rows, bnd_ref, h2_ref, affx_ref, cp_ref, xg_ref):
    b, k = pl.program_id(0), pl.program_id(1)
    D = h2_ref.shape[1]

    @pl.when(k == 0)
    def _():
        xg_ref[...] = jnp.zeros_like(xg_ref)

    ex = lax.broadcasted_iota(jnp.int32, (n_e, 1), 0)
    starts = []
    start_col = jnp.zeros((n_e, 1), f32)
    for e in range(n_e):
        r0 = lax.div(bnd_ref[(b * n_e + e) * LANES + k], 16) * 16
        starts.append(r0)
        start_col = start_col + jnp.where(ex == e, r0.astype(f32), 0.0)
    rel = cp_ref[...] - start_col
    rel_rows = jnp.broadcast_to(rel[:, None, :], (n_e, MOE_WIN, MOE_CHUNK)).reshape(n_e * MOE_WIN, MOE_CHUNK)
    j1 = lax.broadcasted_iota(jnp.int32, (n_e, MOE_WIN, 1), 1).reshape(n_e * MOE_WIN, 1) + 1
    onehot = jnp.where(rel_rows == j1.astype(f32), 1.0, 0.0).astype(bf16)
    tok, tok_aff = h2_ref[...], affx_ref[...]
    moved = _mm(onehot, tok).astype(bf16)
    moved_aff = _mm(onehot, tok_aff).astype(bf16)
    for e in range(n_e):
        dst = pl.ds(pl.multiple_of(e * rows + starts[e], 16), MOE_WIN)
        xg_ref[dst, 0:D] += moved[e * MOE_WIN:(e + 1) * MOE_WIN]
        xg_ref[dst, D:D + LANES] += moved_aff[e * MOE_WIN:(e + 1) * MOE_WIN]

    sub1 = lax.broadcasted_iota(jnp.int32, (MOE_WIN, 1), 0) + 1
    for e in range(n_e):
        lo, hi = bnd_ref[(b * n_e + e) * LANES + k], bnd_ref[(b * n_e + e) * LANES + k + 1]
        n_win = _window_count(lo, hi, starts[e])

        @pl.when(n_win > 1)
        def _(e=e, n_win=n_win):
            def extra(w, c):
                hit = rel[e:e + 1, :] == (sub1 + w * MOE_WIN).astype(f32)
                oh = jnp.where(hit, 1.0, 0.0).astype(bf16)
                dst = pl.ds(pl.multiple_of(e * rows + starts[e] + w * MOE_WIN, 16), MOE_WIN)
                xg_ref[dst, 0:D] += _mm(oh, tok).astype(bf16)
                xg_ref[dst, D:D + LANES] += _mm(oh, tok_aff).astype(bf16)
                return c

            lax.fori_loop(1, n_win, extra, 0)


def _gather(h2, affx, cp, bnd, n_slots):
    B, T, D = h2.shape
    E = cp.shape[1]
    nc = T // MOE_CHUNK
    rows = n_slots + MOE_WIN
    assert n_slots % 16 == 0 and nc + 1 <= LANES and 3 * E <= LANES
    return pl.pallas_call(
        functools.partial(_gather_kernel, E, rows),
        grid_spec=pltpu.PrefetchScalarGridSpec(
            num_scalar_prefetch=1,
            grid=(B, nc),
            in_specs=[
                pl.BlockSpec((None, MOE_CHUNK, D), lambda b, k, s: (b, k, 0)),
                pl.BlockSpec((None, MOE_CHUNK, LANES), lambda b, k, s: (b, k, 0)),
                pl.BlockSpec((None, E, MOE_CHUNK), lambda b, k, s: (b, 0, k)),
            ],
            out_specs=pl.BlockSpec((None, E * rows, D + LANES), lambda b, k, s: (b, 0, 0),
                                   pipeline_mode=pl.Buffered(1)),
        ),
        out_shape=jax.ShapeDtypeStruct((B, E * rows, D + LANES), bf16),
        compiler_params=_params("arbitrary", "arbitrary"),
        name="moe_gather",
    )(bnd, h2, affx, cp)


def _ffn_kernel(n_slots, n_e, xg_ref, wg_ref, wu_ref, wd_ref, y_ref, wg_s, wu_s, wd_s):
    e, b = pl.program_id(0), pl.program_id(1)
    D = wg_ref.shape[0]

    @pl.when(b == 0)
    def _():
        wg_s[...] = wg_ref[...].astype(bf16)
        wu_s[...] = wu_ref[...].astype(bf16)
        wd_s[...] = wd_ref[...].astype(bf16)

    xg = xg_ref[0:n_slots, 0:D]
    a = _mm(xg, wg_s[...])
    u = _mm(xg, wu_s[...])
    act = (a * jax.nn.sigmoid(a) * u).astype(bf16)
    lane = lax.broadcasted_iota(jnp.int32, (1, LANES), 1)
    mine = jnp.logical_and(lax.rem(lane, n_e) == e, lane < 3 * n_e)
    gate = jnp.sum(jnp.where(mine, xg_ref[0:n_slots, D:D + LANES].astype(f32), 0.0), axis=1, keepdims=True)
    y_ref[0:n_slots, :] = (_mm(act, wd_s[...]) * gate).astype(bf16)
    y_ref[n_slots:, :] = jnp.zeros((y_ref.shape[0] - n_slots, y_ref.shape[1]), bf16)


def _ffn(xg, w_gate, w_up, w_down, layer, n_slots):
    B = xg.shape[0]
    _, E, D, F = w_gate.shape
    rows = xg.shape[1] // E
    return pl.pallas_call(
        functools.partial(_ffn_kernel, n_slots, E),
        grid=(E, B),
        in_specs=[
            pl.BlockSpec((None, rows, D + LANES), lambda e, b: (b, e, 0)),
            pl.BlockSpec((None, None, D, F), lambda e, b: (layer, e, 0, 0)),
            pl.BlockSpec((None, None, D, F), lambda e, b: (layer, e, 0, 0)),
            pl.BlockSpec((None, None, F, D), lambda e, b: (layer, e, 0, 0)),
        ],
        out_specs=pl.BlockSpec((None, None, rows, D), lambda e, b: (b, e, 0, 0)),
        out_shape=jax.ShapeDtypeStruct((B, E, rows, D), bf16),
        scratch_shapes=[pltpu.VMEM((D, F), bf16), pltpu.VMEM((D, F), bf16), pltpu.VMEM((F, D), bf16)],
        compiler_params=_params("arbitrary", "arbitrary"),
        name="moe_ffn",
    )(xg, w_gate, w_up, w_down)


def _combine_kernel(n_e, rows, bnd_ref, y_ref, cpt_ref, expand_ref, o_ref):
    b, k = pl.program_id(0), pl.program_id(1)
    cpt = cpt_ref[...]
    hi_part = jnp.floor(cpt * (1.0 / 32.0))
    lo_part = cpt - 32.0 * hi_part
    expand = expand_ref[...]
    rank = 32.0 * _mm(hi_part.astype(bf16), expand) + _mm(lo_part.astype(bf16), expand)
    lane = lax.broadcasted_iota(jnp.int32, (1, n_e * MOE_WIN), 1)
    lane_e = lax.div(lane, MOE_WIN)
    starts = []
    tgt = (lane - lane_e * MOE_WIN + 1).astype(f32)
    for e in range(n_e):
        r0 = lax.div(bnd_ref[(b * n_e + e) * LANES + k], 16) * 16
        starts.append(r0)
        tgt = tgt + jnp.where(lane_e == e, r0.astype(f32), 0.0)
    onehot = jnp.where(rank == tgt, 1.0, 0.0).astype(bf16)
    ycat = jnp.concatenate(
        [y_ref[pl.ds(pl.multiple_of(e * rows + starts[e], 16), MOE_WIN), :] for e in range(n_e)], axis=0)
    o_ref[...] = _mm(onehot, ycat)

    lane_w = lax.broadcasted_iota(jnp.int32, (1, MOE_WIN), 1)
    for e in range(n_e):
        lo, hi = bnd_ref[(b * n_e + e) * LANES + k], bnd_ref[(b * n_e + e) * LANES + k + 1]
        n_win = _window_count(lo, hi, starts[e])

        @pl.when(n_win > 1)
        def _(e=e, n_win=n_win):
            rank_e = rank[:, e * MOE_WIN:(e + 1) * MOE_WIN]

            def extra(w, c):
                r = pl.multiple_of(starts[e] + w * MOE_WIN, 16)
                hit = rank_e == (lane_w + (r + 1)).astype(f32)
                o_ref[...] += _mm(jnp.where(hit, 1.0, 0.0).astype(bf16),
                                  y_ref[pl.ds(pl.multiple_of(e * rows + r, 16), MOE_WIN), :])
                return c

            lax.fori_loop(1, n_win, extra, 0)


def _combine(y, cp, bnd):
    B, E, rows, D = y.shape
    T = cp.shape[2]
    nc = T // MOE_CHUNK
    expand = jnp.asarray(np.kron(np.eye(E), np.ones((1, MOE_WIN))), bf16)
    return pl.pallas_call(
        functools.partial(_combine_kernel, E, rows),
        grid_spec=pltpu.PrefetchScalarGridSpec(
            num_scalar_prefetch=1,
            grid=(B, nc),
            in_specs=[
                pl.BlockSpec((None, E * rows, D), lambda b, k, s: (b, 0, 0), pipeline_mode=pl.Buffered(1)),
                pl.BlockSpec((None, MOE_CHUNK, E), lambda b, k, s: (b, k, 0)),
                pl.BlockSpec(expand.shape, lambda b, k, s: (0, 0)),
            ],
            out_specs=pl.BlockSpec((None, MOE_CHUNK, D), lambda b, k, s: (b, k, 0)),
        ),
        out_shape=jax.ShapeDtypeStruct((B, T, D), f32),
        compiler_params=_params("arbitrary", "arbitrary"),
        name="moe_combine",
    )(bnd, y.reshape(B, E * rows, D), cp.transpose(0, 2, 1), expand)


def _final_kernel(x_ref, moe_ref, mod_ref, o_ref):
    o_ref[...] = x_ref[...] + mod_ref[5:6, :] * moe_ref[...]


def _final(x, moe, mod, S):
    B, T, D = x.shape
    return pl.pallas_call(
        _final_kernel,
        grid=(B, S // TT),
        in_specs=[
            pl.BlockSpec((None, TT, D), lambda b, i: (b, i, 0)),
            pl.BlockSpec((None, TT, D), lambda b, i: (b, i, 0)),
            pl.BlockSpec((None, None, 6, D), lambda b, i: (b, 0, 0, 0)),
        ],
        out_specs=pl.BlockSpec((None, TT, D), lambda b, i: (b, i, 0)),
        out_shape=jax.ShapeDtypeStruct((B, S, D), f32),
        compiler_params=_params("arbitrary", "arbitrary"),
        name="final_residual",
    )(x, moe, mod)


def _rope_tables(S, L):
    t = jnp.arange(S)
    row, col = t // GRID_W, t % GRID_W
    f = 16
    inv = ROPE_BASE ** (-jnp.arange(f, dtype=f32) / f)
    ar = row.astype(f32)[:, None] * inv
    ac = col.astype(f32)[:, None] * inv
    cos = jnp.concatenate([jnp.cos(ar), jnp.cos(ar), jnp.cos(ac), jnp.cos(ac)], axis=1)
    sin = jnp.concatenate([-jnp.sin(ar), jnp.sin(ar), -jnp.sin(ac), jnp.sin(ac)], axis=1)
    cos = jnp.concatenate([jnp.tile(cos, (1, 2)), jnp.ones((L, LANES), f32)], axis=0)
    sin = jnp.concatenate([jnp.tile(sin, (1, 2)), jnp.zeros((L, LANES), f32)], axis=0)
    return cos, sin


def _pad_heads(w, n_heads, width, padded):
    lead = w.shape[:-1]
    w = w.reshape(lead + (n_heads, width))
    w = jnp.pad(w, [(0, 0)] * len(lead) + [(0, 0), (0, padded - width)])
    return w.reshape(lead + (n_heads * padded,))


def _layer_weights(l, D, w_in, norm2_g, na_q_g, na_k_g, mla_q_norm, mla_w_uq, mla_kv_norm, mla_w_ukv,
                   mla_q_g, mla_k_g, swa_q_g, swa_k_g, w_na_o, w_mla_o, w_swa_o, w_o, router):
    o = np.cumsum((0, NA_W, NA_W, NA_W, MLA_Q_RANK, MLA_KV_RANK, MLA_ROPE, SWA_W, SWA_KV_W, SWA_KV_W))
    o_naq, o_nak, o_nav, o_cq, o_ckv, o_kr, o_swq, o_swk, o_swv, o_gl = (int(v) for v in o)
    wi = w_in[l]
    swq = wi[:, o_swq:o_swq + SWA_W].reshape(D, SWA_HEADS, SWA_DIM)[:, SWA_HEAD_ORDER, :].reshape(D, SWA_W)
    kr = jnp.pad(wi[:, o_kr:o_kr + MLA_ROPE], ((0, 0), (0, LANES - MLA_ROPE)))
    w_all = jnp.concatenate([
        wi[:, o_naq:o_naq + 3 * NA_W], swq, wi[:, o_cq:o_cq + MLA_Q_RANK], wi[:, o_ckv:o_ckv + MLA_KV_RANK], kr,
        wi[:, o_swk:o_swk + 2 * SWA_KV_W], wi[:, o_gl:]], axis=1).astype(bf16)
    ukv = mla_w_ukv[l].reshape(MLA_KV_RANK, MLA_HEADS, MLA_NOPE + MLA_V)
    swa_o = w_swa_o[l].reshape(SWA_HEADS, SWA_DIM, D)[SWA_HEAD_ORDER, :, :].reshape(SWA_W, D)
    r_hi = router[l].astype(bf16)
    r_lo = (router[l] - r_hi.astype(f32)).astype(bf16)
    return dict(
        w_all=w_all,
        naq_g=jnp.tile(na_q_g[l], NA_HEADS)[None], nak_g=jnp.tile(na_k_g[l], NA_HEADS)[None],
        swq_g=jnp.tile(swa_q_g[l], SWA_HEADS)[None], swk_g=jnp.tile(swa_k_g[l], SWA_KV_HEADS)[None],
        qn=mla_q_norm[l][None], kvn=mla_kv_norm[l][None],
        wuq=_pad_heads(mla_w_uq[l], MLA_HEADS, MLA_QK, MLA_PAD).astype(bf16),
        wuk=ukv[:, :, :MLA_NOPE].reshape(MLA_KV_RANK, MLA_HEADS * MLA_NOPE).astype(bf16),
        wuv=ukv[:, :, MLA_NOPE:].reshape(MLA_KV_RANK, MLA_W).astype(bf16),
        qg=jnp.tile(jnp.pad(mla_q_g[l], (0, MLA_PAD - MLA_QK)), MLA_HEADS)[None],
        kg=jnp.tile(jnp.pad(mla_k_g[l], (0, MLA_PAD - MLA_QK)), MLA_HEADS)[None],
        wna=w_na_o[l].astype(bf16), wmla=w_mla_o[l].astype(bf16), wswa=swa_o.astype(bf16),
        wo=w_o[l].astype(bf16), g2=norm2_g[l][None], router=jnp.concatenate([r_hi, r_lo], axis=1),
    )


def kernel(x, c, ctx, c_ctx, norm1_g, norm2_g, w_ada, b_ada, w_in, na_q_g, na_k_g, na_rpb, mla_q_norm, mla_w_uq, mla_kv_norm, mla_w_ukv, mla_q_g, mla_k_g, swa_q_g, swa_k_g, swa_sink, w_na_o, w_mla_o, w_swa_o, w_o, router, w_gate, w_up, w_down):
    B, S, D = x.shape
    L = ctx.shape[1]
    depth = w_in.shape[0]
    E = router.shape[2]
    T = S + L
    n_lat = S // TT
    rows = S // GRID_W
    assert S % TT == 0 and L % TM == 0 and L <= TT and rows >= NA_BAND_ROWS and S >= SWA_KEYS and B < 16
    cap_s = CAPACITY_FACTOR * S // E
    cap_l = CAPACITY_FACTOR * L // E

    cpad = jnp.zeros((16, D), f32).at[:B].set(c).at[B].set(c_ctx)
    mod = _ada(cpad, w_ada, b_ada).reshape(depth, 16, 6, D)
    mod = jnp.stack([mod[:, :B], jnp.broadcast_to(mod[:, B:B + 1], (depth, B, 6, D))], axis=2)

    cos, sin = _rope_tables(S, L)
    gmat = jnp.asarray(np.kron(np.eye(NA_HEADS), np.full((NA_DIM, NA_DIM), 1.0 / NA_DIM)), bf16)

    xs = jnp.concatenate([x, ctx], axis=1)
    moe = None
    for l in range(depth):
        lw = _layer_weights(l, D, w_in, norm2_g, na_q_g, na_k_g, mla_q_norm, mla_w_uq, mla_kv_norm, mla_w_ukv,
                            mla_q_g, mla_k_g, swa_q_g, swa_k_g, w_na_o, w_mla_o, w_swa_o, w_o, router)
        modp = mod[l - 1] if l > 0 else None
        xs, qka, mq, mk, mv, vals, gl = _in_proj(xs, moe, modp, mod[l], norm1_g[l][None], lw["w_all"],
                                                 cos, sin, gmat, lw, n_lat)
        ona = _na(qka, vals, _na_bias(na_rpb[l], rows), S)
        omla = _mla(mq, mk, mv, S)
        oswa = _swa(qka, vals, swa_sink[l] * LOG2E, S)
        xs, h2, aff, affx = _post(ona, omla, oswa, gl, xs, mod[l], lw, n_lat)
        cp, bnd = _topk(aff.transpose(0, 2, 1), S, cap_s, cap_l)
        bnd = bnd.astype(jnp.int32).reshape(-1)
        xg = _gather(h2, affx, cp, bnd, cap_s + cap_l)
        y = _ffn(xg, w_gate, w_up, w_down, l, cap_s + cap_l)
        moe = _combine(y, cp, bnd)
    return _final(xs, moe, mod[depth - 1], S)
```

```python
import functools

import numpy as np
import jax
import jax.numpy as jnp
from jax import lax
from jax.experimental import pallas as pl
from jax.experimental.pallas import tpu as pltpu

GRID_W = 64
NA_HEADS, NA_DIM, NA_WIN_ROWS, NA_WIN_COLS = 6, 64, 8, 16
MLA_HEADS, MLA_Q_RANK, MLA_KV_RANK, MLA_NOPE, MLA_ROPE, MLA_V = 4, 256, 128, 128, 64, 128
MLA_QK = MLA_NOPE + MLA_ROPE
MLA_PAD = 256
SWA_HEADS, SWA_KV_HEADS, SWA_DIM, SWA_WINDOW = 6, 2, 64, 128
CAPACITY_FACTOR = 2
N_BRANCH = 3
ROPE_BASE = 10000.0
EPS = 1e-6

NA_W = NA_HEADS * NA_DIM
SWA_W = SWA_HEADS * SWA_DIM
SWA_KV_W = SWA_KV_HEADS * SWA_DIM
MLA_W = MLA_HEADS * MLA_V

LANES = 128
TM = 256
TT = 512
MLA_TQ = 1024
MLA_TK = 1024
LOG2E = 1.4426950408889634
NA_TILE_ROWS = TM // GRID_W
NA_BAND_ROWS = NA_TILE_ROWS + NA_WIN_ROWS - 1
NA_BAND = NA_BAND_ROWS * GRID_W
SWA_KEYS = TM + 2 * SWA_WINDOW
MOE_CHUNK = 256
MOE_WIN = 64
NEG = -1e30
VMEM_LIMIT = 56 * 1024 * 1024

P_NAQ, P_NAK, P_NAV, P_SWQ = 0, 384, 768, 1152
P_CQ, P_CKV, P_KR, P_SWK, P_SWV = 1536, 1792, 1920, 2048, 2176
PROJ_W = 2304
N_CHUNK = 768
A_NAQ, A_NAK, A_SWQ, A_SWK = 0, 384, 768, 1152
QKA_W = 1280
SWA_HEAD_ORDER = (0, 3, 1, 4, 2, 5)

bf16 = jnp.bfloat16
f32 = jnp.float32


def _mm(a, b):
    return jnp.dot(a, b, preferred_element_type=f32)


def _nt(a, b):
    return lax.dot_general(a, b, (((1,), (1,)), ((), ())), preferred_element_type=f32)


def _params(*sem):
    return pltpu.CompilerParams(dimension_semantics=sem, vmem_limit_bytes=VMEM_LIMIT)


def _ada_kernel(c_ref, w_ref, b_ref, o_ref):
    a = c_ref[...]
    a = (a * jax.nn.sigmoid(a)).astype(bf16)
    o_ref[...] = _mm(a, w_ref[...].astype(bf16)) + b_ref[...]


def _ada(cpad, w_ada, b_ada):
    L, D, N = w_ada.shape
    tn = 1536
    return pl.pallas_call(
        _ada_kernel,
        grid=(L, N // tn),
        in_specs=[
            pl.BlockSpec((16, D), lambda l, j: (0, 0)),
            pl.BlockSpec((None, D, tn), lambda l, j: (l, 0, j)),
            pl.BlockSpec((None, 1, tn), lambda l, j: (l, 0, j)),
        ],
        out_specs=pl.BlockSpec((None, 16, tn), lambda l, j: (l, 0, j)),
        out_shape=jax.ShapeDtypeStruct((L, 16, N), f32),
        compiler_params=_params("arbitrary", "arbitrary"),
        name="ada",
    )(cpad, w_ada, b_ada.reshape(L, 1, N))


def _in_proj_kernel(has_moe, n_gl_chunks, *refs):
    if has_moe:
        x_ref, moe_ref, modp_ref, mod_ref, g_ref, w_ref, xo_ref, proj_ref, gl_ref = refs
    else:
        x_ref, mod_ref, g_ref, w_ref, proj_ref, gl_ref = refs
    n_sub = 2
    sub = x_ref.shape[0] // n_sub
    for r in range(n_sub):
        rows = slice(r * sub, (r + 1) * sub)
        x = x_ref[rows, :]
        if has_moe:
            x = x + modp_ref[5:6, :] * moe_ref[rows, :]
            xo_ref[rows, :] = x
        ms = jnp.mean(x * x, axis=-1, keepdims=True)
        h = x * lax.rsqrt(ms + EPS) * (g_ref[...] * (1.0 + mod_ref[1:2, :])) + mod_ref[0:1, :]
        h = h.astype(bf16)
        for c in range(PROJ_W // N_CHUNK):
            sl = slice(c * N_CHUNK, (c + 1) * N_CHUNK)
            proj_ref[rows, sl] = _mm(h, w_ref[:, sl]).astype(bf16)
        for c in range(n_gl_chunks):
            sl = slice(c * N_CHUNK, (c + 1) * N_CHUNK)
            gl_ref[rows, sl] = _mm(h, w_ref[:, PROJ_W + c * N_CHUNK:PROJ_W + (c + 1) * N_CHUNK]).astype(bf16)


def _mod_spec(n_lat, D):
    return pl.BlockSpec((None, None, 6, D), lambda b, i: (b, jnp.where(i >= n_lat, 1, 0), 0, 0))


def _in_proj(x, moe, modp, mod, g1, w, n_lat):
    B, T, D = x.shape
    nt = pl.cdiv(T, TT)
    has_moe = moe is not None
    n_gl_chunks = N_BRANCH * D // N_CHUNK
    tok = pl.BlockSpec((None, TT, D), lambda b, i: (b, i, 0))
    in_specs = [tok]
    args = [x]
    if has_moe:
        in_specs += [tok, _mod_spec(n_lat, D)]
        args += [moe, modp]
    in_specs += [
        _mod_spec(n_lat, D),
        pl.BlockSpec((1, D), lambda b, i: (0, 0)),
        pl.BlockSpec(w.shape, lambda b, i: (0, 0), pipeline_mode=pl.Buffered(1)),
    ]
    args += [mod, g1, w]
    out_specs = [
        pl.BlockSpec((None, TT, PROJ_W), lambda b, i: (b, i, 0)),
        pl.BlockSpec((None, TT, N_BRANCH * D), lambda b, i: (b, i, 0)),
    ]
    out_shape = [
        jax.ShapeDtypeStruct((B, T, PROJ_W), bf16),
        jax.ShapeDtypeStruct((B, T, N_BRANCH * D), bf16),
    ]
    if has_moe:
        out_specs = [tok] + out_specs
        out_shape = [jax.ShapeDtypeStruct((B, T, D), f32)] + out_shape
    res = pl.pallas_call(
        functools.partial(_in_proj_kernel, has_moe, n_gl_chunks),
        grid=(B, nt),
        in_specs=in_specs,
        out_specs=out_specs,
        out_shape=out_shape,
        compiler_params=_params("arbitrary", "arbitrary"),
        name="in_proj",
    )(*args)
    if has_moe:
        return res
    return [x] + list(res)


def _prep_kernel(proj_ref, cos_ref, sin_ref, gmat_ref, naq_g, nak_g, swq_g, swk_g,
                 qn_ref, wuq_ref, qg_ref, kvn_ref, wuk_ref, wuv_ref, kg_ref,
                 qka_ref, mq_ref, mk_ref, mv_ref):
    cos = cos_ref[...]
    sin = sin_ref[...]
    lane = lax.broadcasted_iota(jnp.int32, cos.shape, 1)
    first = (lane & 16) == 0

    def rope(x):
        partner = jnp.where(first, pltpu.roll(x, LANES - 16, 1), pltpu.roll(x, 16, 1))
        return x * cos + partner * sin

    def headnorm(x, gain):
        w = x.shape[1]
        ms = _mm((x * x).astype(bf16), gmat_ref[:w, :w])
        return x * lax.rsqrt(ms + EPS) * gain

    def rmsnorm(x, gain):
        ms = jnp.mean(x * x, axis=-1, keepdims=True)
        return x * lax.rsqrt(ms + EPS) * gain

    na_scale = NA_DIM ** -0.5 * LOG2E
    naq = headnorm(proj_ref[:, P_NAQ:P_NAQ + NA_W].astype(f32), naq_g[...])
    qka_ref[:, A_NAQ:A_NAQ + NA_W] = (naq * na_scale).astype(bf16)
    nak = headnorm(proj_ref[:, P_NAK:P_NAK + NA_W].astype(f32), nak_g[...])
    qka_ref[:, A_NAK:A_NAK + NA_W] = nak.astype(bf16)

    sw_scale = SWA_DIM ** -0.5 * LOG2E
    swq = headnorm(proj_ref[:, P_SWQ:P_SWQ + SWA_W].astype(f32), swq_g[...])
    for p in range(SWA_W // LANES):
        sl = slice(p * LANES, (p + 1) * LANES)
        qka_ref[:, A_SWQ + p * LANES:A_SWQ + (p + 1) * LANES] = (rope(swq[:, sl]) * sw_scale).astype(bf16)
    swk = headnorm(proj_ref[:, P_SWK:P_SWK + SWA_KV_W].astype(f32), swk_g[...])
    qka_ref[:, A_SWK:A_SWK + SWA_KV_W] = rope(swk).astype(bf16)

    mla_scale = MLA_QK ** -0.5 * LOG2E
    cq = rmsnorm(proj_ref[:, P_CQ:P_CQ + MLA_Q_RANK].astype(f32), qn_ref[...]).astype(bf16)
    q = _mm(cq, wuq_ref[...])
    ckv = rmsnorm(proj_ref[:, P_CKV:P_CKV + MLA_KV_RANK].astype(f32), kvn_ref[...]).astype(bf16)
    kn = _mm(ckv, wuk_ref[...])
    v = _mm(ckv, wuv_ref[...]).astype(bf16)
    ones_col = jnp.where(lax.broadcasted_iota(jnp.int32, (v.shape[0], LANES), 1) == 0, 1.0, 0.0).astype(bf16)
    for h in range(MLA_HEADS):
        mv_ref[:, h * MLA_PAD:h * MLA_PAD + MLA_V] = v[:, h * MLA_V:(h + 1) * MLA_V]
        mv_ref[:, h * MLA_PAD + MLA_V:(h + 1) * MLA_PAD] = ones_col
    kr = proj_ref[:, P_KR:P_KR + LANES].astype(f32)
    kr_ss = jnp.sum(kr * kr, axis=-1, keepdims=True)
    for h in range(MLA_HEADS):
        o = h * MLA_PAD
        qh = q[:, o:o + MLA_PAD]
        r = lax.rsqrt(jnp.sum(qh * qh, axis=-1, keepdims=True) * (1.0 / MLA_QK) + EPS)
        qh = qh * r * qg_ref[:, o:o + MLA_PAD]
        mq_ref[:, o:o + LANES] = (qh[:, :LANES] * mla_scale).astype(bf16)
        mq_ref[:, o + LANES:o + MLA_PAD] = (rope(qh[:, LANES:]) * mla_scale).astype(bf16)
        kh = kn[:, h * MLA_NOPE:(h + 1) * MLA_NOPE]
        r = lax.rsqrt((jnp.sum(kh * kh, axis=-1, keepdims=True) + kr_ss) * (1.0 / MLA_QK) + EPS)
        mk_ref[:, o:o + LANES] = (kh * r * kg_ref[:, o:o + LANES]).astype(bf16)
        mk_ref[:, o + LANES:o + MLA_PAD] = rope(kr * r * kg_ref[:, o + LANES:o + MLA_PAD]).astype(bf16)


def _prep(proj, cos, sin, gmat, lw):
    B, T, _ = proj.shape
    nt = pl.cdiv(T, TT)

    def const(a):
        return pl.BlockSpec(a.shape, lambda b, i: (0,) * a.ndim)

    consts = [gmat, lw["naq_g"], lw["nak_g"], lw["swq_g"], lw["swk_g"], lw["qn"], lw["wuq"], lw["qg"],
              lw["kvn"], lw["wuk"], lw["wuv"], lw["kg"]]
    widths = (QKA_W, MLA_HEADS * MLA_PAD, MLA_HEADS * MLA_PAD, MLA_HEADS * MLA_PAD)
    return pl.pallas_call(
        _prep_kernel,
        grid=(B, nt),
        in_specs=[
            pl.BlockSpec((None, TT, PROJ_W), lambda b, i: (b, i, 0)),
            pl.BlockSpec((TT, LANES), lambda b, i: (i, 0)),
            pl.BlockSpec((TT, LANES), lambda b, i: (i, 0)),
        ] + [const(a) for a in consts],
        out_specs=[pl.BlockSpec((None, TT, w), lambda b, i: (b, i, 0)) for w in widths],
        out_shape=[jax.ShapeDtypeStruct((B, T, w), bf16) for w in widths],
        compiler_params=_params("arbitrary", "arbitrary"),
        name="prep",
    )(proj, cos, sin, *consts)


def _half_masks():
    lane = lax.broadcasted_iota(jnp.int32, (1, LANES), 1)
    lo = jnp.where(lane < 64, 1.0, 0.0).astype(bf16)
    return lo, (1.0 - lo.astype(f32)).astype(bf16)


def _pair_attend(qp, parts, masks, sinks=None):
    n_q = qp.shape[0]
    q2 = jnp.concatenate([qp * masks[0], qp * masks[1]], axis=0)
    scores = []
    for k, _, bias in parts:
        s = _nt(q2, k)
        scores.append(s if bias is None else s + bias)
    m = functools.reduce(jnp.maximum, [jnp.max(s, axis=-1, keepdims=True) for s in scores])
    if sinks is not None:
        first = lax.broadcasted_iota(jnp.int32, (2 * n_q, 1), 0) < n_q
        sink = jnp.where(first, sinks[0], sinks[1])
        m = jnp.maximum(m, sink)
    ps = [jnp.exp2(s - m) for s in scores]
    l = functools.reduce(jnp.add, [jnp.sum(p, axis=-1, keepdims=True) for p in ps])
    if sinks is not None:
        l = l + jnp.exp2(sink - m)
    o = functools.reduce(jnp.add, [_mm(p.astype(bf16), v) for p, (_, v, _) in zip(ps, parts)]) / l
    lane = lax.broadcasted_iota(jnp.int32, (1, LANES), 1)
    return jnp.where(lane < 64, o[:n_q], o[n_q:])


def _na_kernel(n_lat, S, rows, q_ref, k_ref, v_ref, bias_ref, o_ref):
    i = pl.program_id(1)
    masks = _half_masks()
    T = k_ref.shape[0]

    @pl.when(i < n_lat)
    def _():
        start = GRID_W * jnp.clip(NA_TILE_ROWS * i - NA_WIN_ROWS // 2, 0, rows - NA_BAND_ROWS)
        start = pl.multiple_of(start, GRID_W)
        for p in range(NA_W // LANES):
            sl = slice(p * LANES, (p + 1) * LANES)
            band = (k_ref[pl.ds(start, NA_BAND), sl], v_ref[pl.ds(start, NA_BAND), sl],
                    bias_ref[2 * p:2 * p + 2].reshape(2 * TM, NA_BAND))
            ctx = (k_ref[S:T, sl], v_ref[S:T, sl], None)
            o_ref[:, sl] = _pair_attend(q_ref[:, sl], [band, ctx], masks).astype(bf16)

    @pl.when(i >= n_lat)
    def _():
        for p in range(NA_W // LANES):
            sl = slice(p * LANES, (p + 1) * LANES)
            ctx = (k_ref[S:T, sl], v_ref[S:T, sl], None)
            o_ref[:, sl] = _pair_attend(q_ref[:, sl], [ctx], masks).astype(bf16)


def _na(qka, proj, bias, S):
    B, T, _ = qka.shape
    nt, n_lat, rows = T // TM, S // TM, S // GRID_W

    def bias_idx(b, i):
        return (jnp.where(i == 0, 0, jnp.where(i >= n_lat - 1, 2, 1)), 0, 0, 0)

    return pl.pallas_call(
        functools.partial(_na_kernel, n_lat, S, rows),
        grid=(B, nt),
        in_specs=[
            pl.BlockSpec((None, TM, NA_W), lambda b, i: (b, i, A_NAQ // NA_W)),
            pl.BlockSpec((None, T, NA_W), lambda b, i: (b, 0, A_NAK // NA_W)),
            pl.BlockSpec((None, T, NA_W), lambda b, i: (b, 0, P_NAV // NA_W)),
            pl.BlockSpec((None, NA_HEADS, TM, NA_BAND), bias_idx),
        ],
        out_specs=pl.BlockSpec((None, TM, NA_W), lambda b, i: (b, i, 0)),
        out_shape=jax.ShapeDtypeStruct((B, T, NA_W), bf16),
        compiler_params=_params("arbitrary", "arbitrary"),
        name="na_attn",
    )(qka, qka, proj, bias)


def _na_bias(rpb, rows):
    n_tiles = rows // NA_TILE_ROWS
    col = np.arange(GRID_W)
    c0 = np.clip(col - NA_WIN_COLS // 2, 0, GRID_W - NA_WIN_COLS)
    col_ok = (col[None, :] >= c0[:, None]) & (col[None, :] < c0[:, None] + NA_WIN_COLS)
    dc = np.clip(col[None, :] - col[:, None] + NA_WIN_COLS - 1, 0, 2 * NA_WIN_COLS - 2)
    tz = jnp.where(col_ok, rpb[:, :, dc] * LOG2E, NEG)
    dr = np.zeros((3, NA_TILE_ROWS, NA_BAND_ROWS), np.int32)
    ok = np.zeros((3, NA_TILE_ROWS, NA_BAND_ROWS), bool)
    for ty, rt in enumerate((0, 1, n_tiles - 1)):
        bs = int(np.clip(NA_TILE_ROWS * rt - NA_WIN_ROWS // 2, 0, rows - NA_BAND_ROWS))
        for qr in range(NA_TILE_ROWS):
            r = NA_TILE_ROWS * rt + qr
            s_r = int(np.clip(r - NA_WIN_ROWS // 2, 0, rows - NA_WIN_ROWS))
            for w in range(NA_BAND_ROWS):
                kr = bs + w
                ok[ty, qr, w] = s_r <= kr < s_r + NA_WIN_ROWS
                dr[ty, qr, w] = np.clip(kr - r + NA_WIN_ROWS - 1, 0, 2 * NA_WIN_ROWS - 2)
    blk = tz[:, dr]
    blk = jnp.where(ok[None, :, :, :, None, None], blk, NEG)
    blk = blk.transpose(1, 0, 2, 4, 3, 5)
    return blk.reshape(3, NA_HEADS, TM, NA_BAND)


def _mla_kernel(n_q, S, q_ref, k_ref, v_ref, o_ref):
    i = pl.program_id(2)
    T = k_ref.shape[0]

    def attend(q, bounds):
        m_run = acc = None
        for lo, hi in bounds:
            s = _nt(q, k_ref[lo:hi, :])
            m_new = jnp.max(s, axis=-1, keepdims=True)
            if m_run is not None:
                m_new = jnp.maximum(m_run, m_new)
            pv = _mm(jnp.exp2(s - m_new).astype(bf16), v_ref[lo:hi, :])
            acc = pv if acc is None else acc * jnp.exp2(m_run - m_new) + pv
            m_run = m_new
        return (acc[:, :MLA_V] / acc[:, MLA_V:MLA_V + 1]).astype(bf16)

    @pl.when(i < n_q)
    def _():
        chunks = [(lo, lo + MLA_TK) for lo in range(0, S, MLA_TK)] + [(S, T)]
        o_ref[...] = attend(q_ref[...], chunks)

    @pl.when(i >= n_q)
    def _():
        o_ref[0:T - S, :] = attend(q_ref[0:T - S, :], [(S, T)])


def _mla(mq, mk, mv, S):
    B, T, _ = mq.shape
    n_q = S // MLA_TQ
    assert S % MLA_TQ == 0 and S % MLA_TK == 0 and T - S <= MLA_TQ
    return pl.pallas_call(
        functools.partial(_mla_kernel, n_q, S),
        grid=(B, MLA_HEADS, n_q + 1),
        in_specs=[
            pl.BlockSpec((None, MLA_TQ, MLA_PAD), lambda b, h, i: (b, i, h)),
            pl.BlockSpec((None, T, MLA_PAD), lambda b, h, i: (b, 0, h)),
            pl.BlockSpec((None, T, MLA_PAD), lambda b, h, i: (b, 0, h)),
        ],
        out_specs=pl.BlockSpec((None, MLA_TQ, MLA_V), lambda b, h, i: (b, i, h)),
        out_shape=jax.ShapeDtypeStruct((B, T, MLA_W), bf16),
        compiler_params=_params("arbitrary", "arbitrary", "arbitrary"),
        name="mla_attn",
    )(mq, mk, mv)


def _swa_kernel(n_lat, S, sink_ref, q_ref, k_ref, v_ref, o_ref):
    i = pl.program_id(1)
    masks = _half_masks()
    T = k_ref.shape[0]
    kc, vc = k_ref[S:T, :], v_ref[S:T, :]

    def sinks(p):
        return (sink_ref[SWA_HEAD_ORDER[2 * p]], sink_ref[SWA_HEAD_ORDER[2 * p + 1]])

    @pl.when(i < n_lat)
    def _():
        start = pl.multiple_of(jnp.clip(i * TM - SWA_WINDOW, 0, S - SWA_KEYS), SWA_WINDOW)
        row = lax.broadcasted_iota(jnp.int32, (2 * TM, SWA_KEYS), 0)
        qpos = i * TM + jnp.where(row >= TM, row - TM, row)
        kpos = start + lax.broadcasted_iota(jnp.int32, (2 * TM, SWA_KEYS), 1)
        band = jnp.where(jnp.abs(qpos - kpos) <= SWA_WINDOW, 0.0, NEG)
        kb, vb = k_ref[pl.ds(start, SWA_KEYS), :], v_ref[pl.ds(start, SWA_KEYS), :]
        for p in range(SWA_W // LANES):
            sl = slice(p * LANES, (p + 1) * LANES)
            parts = [(kb, vb, band), (kc, vc, None)]
            o_ref[:, sl] = _pair_attend(q_ref[:, sl], parts, masks, sinks(p)).astype(bf16)

    @pl.when(i >= n_lat)
    def _():
        for p in range(SWA_W // LANES):
            sl = slice(p * LANES, (p + 1) * LANES)
            o_ref[:, sl] = _pair_attend(q_ref[:, sl], [(kc, vc, None)], masks, sinks(p)).astype(bf16)


def _swa(qka, proj, sink, S):
    B, T, _ = qka.shape
    nt, n_lat = T // TM, S // TM
    return pl.pallas_call(
        functools.partial(_swa_kernel, n_lat, S),
        grid=(B, nt),
        in_specs=[
            pl.BlockSpec(memory_space=pltpu.SMEM),
            pl.BlockSpec((None, TM, SWA_W), lambda b, i: (b, i, A_SWQ // SWA_W)),
            pl.BlockSpec((None, T, SWA_KV_W), lambda b, i: (b, 0, A_SWK // SWA_KV_W)),
            pl.BlockSpec((None, T, SWA_KV_W), lambda b, i: (b, 0, P_SWV // SWA_KV_W)),
        ],
        out_specs=pl.BlockSpec((None, TM, SWA_W), lambda b, i: (b, i, 0)),
        out_shape=jax.ShapeDtypeStruct((B, T, SWA_W), bf16),
        compiler_params=_params("arbitrary", "arbitrary"),
        name="swa_attn",
    )(sink, qka, qka, proj)


def _post_kernel(ona_ref, omla_ref, oswa_ref, gl_ref, x_ref, mod_ref, wna_ref, wmla_ref, wswa_ref, wo_ref,
                 g2_ref, rt_ref, xo_ref, h2_ref, aff_ref, affx_ref):
    D = x_ref.shape[1]
    n_e = aff_ref.shape[1]
    n_sub = 2
    sub = x_ref.shape[0] // n_sub
    for r in range(n_sub):
        rows = slice(r * sub, (r + 1) * sub)
        merged = None
        for j, (o_ref, w_ref) in enumerate(((ona_ref, wna_ref), (omla_ref, wmla_ref), (oswa_ref, wswa_ref))):
            y = _mm(o_ref[rows, :], w_ref[...])
            gy = (jnp.tanh(gl_ref[rows, j * D:(j + 1) * D].astype(f32)) + 1.0) * y
            merged = gy if merged is None else merged + gy
        res = _mm(merged.astype(bf16), wo_ref[...])
        x = x_ref[rows, :] + mod_ref[2:3, :] * res
        xo_ref[rows, :] = x
        ms = jnp.mean(x * x, axis=-1, keepdims=True)
        h2 = x * lax.rsqrt(ms + EPS) * (g2_ref[...] * (1.0 + mod_ref[4:5, :])) + mod_ref[3:4, :]
        h_hi = h2.astype(bf16)
        h2_ref[rows, :] = h_hi
        h_lo = (h2 - h_hi.astype(f32)).astype(bf16)
        parts = _mm(h_hi, rt_ref[...]) + _mm(h_lo, rt_ref[...])
        logits = parts[:, :n_e] + parts[:, n_e:]
        e = jnp.exp(logits - jnp.max(logits, axis=1, keepdims=True))
        aff = e / jnp.sum(e, axis=1, keepdims=True)
        aff_ref[rows, :] = aff
        p0 = aff.astype(bf16)
        r1 = aff - p0.astype(f32)
        p1 = r1.astype(bf16)
        p2 = (r1 - p1.astype(f32)).astype(bf16)
        pad = jnp.zeros((sub, LANES - 3 * n_e), bf16)
        affx_ref[rows, :] = jnp.concatenate([p0, p1, p2, pad], axis=1)


def _post(ona, omla, oswa, gl, x, mod, lw, n_lat):
    B, T, D = x.shape
    nt = pl.cdiv(T, TT)
    E = lw["router"].shape[1] // 2

    def tok(w):
        return pl.BlockSpec((None, TT, w), lambda b, i: (b, i, 0))

    def const(a):
        return pl.BlockSpec(a.shape, lambda b, i: (0,) * a.ndim)

    consts = [lw["wna"], lw["wmla"], lw["wswa"], lw["wo"], lw["g2"], lw["router"]]
    return pl.pallas_call(
        _post_kernel,
        grid=(B, nt),
        in_specs=[tok(NA_W), tok(MLA_W), tok(SWA_W), tok(N_BRANCH * D), tok(D), _mod_spec(n_lat, D)]
        + [const(a) for a in consts],
        out_specs=[tok(D), tok(D), tok(E), tok(LANES)],
        out_shape=[jax.ShapeDtypeStruct((B, T, D), f32), jax.ShapeDtypeStruct((B, T, D), bf16),
                   jax.ShapeDtypeStruct((B, T, E), f32), jax.ShapeDtypeStruct((B, T, LANES), bf16)],
        compiler_params=_params("arbitrary", "arbitrary"),
        name="post_attn",
    )(ona, omla, oswa, gl, x, mod, *consts)


def _lane_cumsum(mask, tri):
    E, n = mask.shape
    carry = jnp.zeros((E, 1), f32)
    outs = []
    for k in range(n // LANES):
        w = _mm(mask[:, k * LANES:(k + 1) * LANES].astype(bf16), tri) + carry
        outs.append(w)
        carry = w[:, LANES - 1:LANES]
    return jnp.concatenate(outs, axis=1)


def _select_slots(aff, cap, base, tri):
    bits = lax.bitcast_convert_type(aff, jnp.int32)
    thr = jnp.zeros((aff.shape[0], 1), jnp.int32)
    for bit in range(30, -1, -1):
        cand = thr | (1 << bit)
        cnt = jnp.sum(jnp.where(bits >= cand, 1.0, 0.0), axis=1, keepdims=True)
        thr = jnp.where(cnt >= cap, cand, thr)
    gt = jnp.where(bits > thr, 1.0, 0.0)
    eq = jnp.where(bits == thr, 1.0, 0.0)
    need = cap - jnp.sum(gt, axis=1, keepdims=True)
    sel = jnp.maximum(gt, jnp.where(_lane_cumsum(eq, tri) <= need, eq, 0.0))
    return jnp.where(sel > 0.0, _lane_cumsum(sel, tri) + base, 0.0)


def _topk_kernel(S, cap_s, cap_l, aff_ref, tri_ref, before_ref, cp_ref, bnd_ref):
    T = aff_ref.shape[1]
    tri = tri_ref[...]
    cp_ref[:, 0:S] = _select_slots(aff_ref[:, 0:S], cap_s, 0.0, tri)
    cp_ref[:, S:T] = _select_slots(aff_ref[:, S:T], cap_l, float(cap_s), tri)
    bnd_ref[...] = _mm(jnp.where(cp_ref[...] > 0.0, 1.0, 0.0).astype(bf16), before_ref[...])


def _topk(aff, S, cap_s, cap_l):
    B, E, T = aff.shape
    tri = jnp.asarray(np.triu(np.ones((LANES, LANES), np.float32)), bf16)
    before = jnp.asarray(np.arange(T)[:, None] < MOE_CHUNK * np.arange(LANES)[None, :], bf16)
    return pl.pallas_call(
        functools.partial(_topk_kernel, S, cap_s, cap_l),
        grid=(B,),
        in_specs=[pl.BlockSpec((None, E, T), lambda b: (b, 0, 0)),
                  pl.BlockSpec((LANES, LANES), lambda b: (0, 0)),
                  pl.BlockSpec((T, LANES), lambda b: (0, 0))],
        out_specs=[pl.BlockSpec((None, E, T), lambda b: (b, 0, 0)),
                   pl.BlockSpec((None, E, LANES), lambda b: (b, 0, 0))],
        out_shape=[jax.ShapeDtypeStruct((B, E, T), f32), jax.ShapeDtypeStruct((B, E, LANES), f32)],
        compiler_params=_params("arbitrary"),
        name="expert_select",
    )(aff, tri, before)


def _window_count(lo, hi, r0):
    return jnp.where(hi > lo, lax.div(hi - r0 + (MOE_WIN - 1), MOE_WIN), 0)


def _gather_kernel(n_e, rows, bnd_ref, h2_ref, affx_ref, cp_ref, xg_ref):
    b, k = pl.program_id(0), pl.program_id(1)
    D = h2_ref.shape[1]

    @pl.when(k == 0)
    def _():
        xg_ref[...] = jnp.zeros_like(xg_ref)

    ex = lax.broadcasted_iota(jnp.int32, (n_e, 1), 0)
    starts = []
    start_col = jnp.zeros((n_e, 1), f32)
    for e in range(n_e):
        r0 = lax.div(bnd_ref[(b * n_e + e) * LANES + k], 16) * 16
        starts.append(r0)
        start_col = start_col + jnp.where(ex == e, r0.astype(f32), 0.0)
    rel = cp_ref[...] - start_col
    rel_rows = jnp.broadcast_to(rel[:, None, :], (n_e, MOE_WIN, MOE_CHUNK)).reshape(n_e * MOE_WIN, MOE_CHUNK)
    j1 = lax.broadcasted_iota(jnp.int32, (n_e, MOE_WIN, 1), 1).reshape(n_e * MOE_WIN, 1) + 1
    onehot = jnp.where(rel_rows == j1.astype(f32), 1.0, 0.0).astype(bf16)
    tok, tok_aff = h2_ref[...], affx_ref[...]
    moved = _mm(onehot, tok).astype(bf16)
    moved_aff = _mm(onehot, tok_aff).astype(bf16)
    for e in range(n_e):
        dst = pl.ds(pl.multiple_of(e * rows + starts[e], 16), MOE_WIN)
        xg_ref[dst, 0:D] += moved[e * MOE_WIN:(e + 1) * MOE_WIN]
        xg_ref[dst, D:D + LANES] += moved_aff[e * MOE_WIN:(e + 1) * MOE_WIN]

    sub1 = lax.broadcasted_iota(jnp.int32, (MOE_WIN, 1), 0) + 1
    for e in range(n_e):
        lo, hi = bnd_ref[(b * n_e + e) * LANES + k], bnd_ref[(b * n_e + e) * LANES + k + 1]
        n_win = _window_count(lo, hi, starts[e])

        @pl.when(n_win > 1)
        def _(e=e, n_win=n_win):
            def extra(w, c):
                hit = rel[e:e + 1, :] == (sub1 + w * MOE_WIN).astype(f32)
                oh = jnp.where(hit, 1.0, 0.0).astype(bf16)
                dst = pl.ds(pl.multiple_of(e * rows + starts[e] + w * MOE_WIN, 16), MOE_WIN)
                xg_ref[dst, 0:D] += _mm(oh, tok).astype(bf16)
                xg_ref[dst, D:D + LANES] += _mm(oh, tok_aff).astype(bf16)
                return c

            lax.fori_loop(1, n_win, extra, 0)


def _gather(h2, affx, cp, bnd, n_slots):
    B, T, D = h2.shape
    E = cp.shape[1]
    nc = T // MOE_CHUNK
    rows = n_slots + MOE_WIN
    assert n_slots % 16 == 0 and nc + 1 <= LANES and 3 * E <= LANES
    return pl.pallas_call(
        functools.partial(_gather_kernel, E, rows),
        grid_spec=pltpu.PrefetchScalarGridSpec(
            num_scalar_prefetch=1,
            grid=(B, nc),
            in_specs=[
                pl.BlockSpec((None, MOE_CHUNK, D), lambda b, k, s: (b, k, 0)),
                pl.BlockSpec((None, MOE_CHUNK, LANES), lambda b, k, s: (b, k, 0)),
                pl.BlockSpec((None, E, MOE_CHUNK), lambda b, k, s: (b, 0, k)),
            ],
            out_specs=pl.BlockSpec((None, E * rows, D + LANES), lambda b, k, s: (b, 0, 0),
                                   pipeline_mode=pl.Buffered(1)),
        ),
        out_shape=jax.ShapeDtypeStruct((B, E * rows, D + LANES), bf16),
        compiler_params=_params("arbitrary", "arbitrary"),
        name="moe_gather",
    )(bnd, h2, affx, cp)


def _ffn_kernel(n_slots, n_e, xg_ref, wg_ref, wu_ref, wd_ref, y_ref, wg_s, wu_s, wd_s):
    e, b = pl.program_id(0), pl.program_id(1)
    D = wg_ref.shape[0]

    @pl.when(b == 0)
    def _():
        wg_s[...] = wg_ref[...].astype(bf16)
        wu_s[...] = wu_ref[...].astype(bf16)
        wd_s[...] = wd_ref[...].astype(bf16)

    n_b = xg_ref.shape[0]
    xg = jnp.concatenate([xg_ref[s, 0:n_slots, 0:D] for s in range(n_b)], axis=0)
    a = _mm(xg, wg_s[...])
    u = _mm(xg, wu_s[...])
    act = (a * jax.nn.sigmoid(a) * u).astype(bf16)
    lane = lax.broadcasted_iota(jnp.int32, (1, LANES), 1)
    mine = jnp.logical_and(lax.rem(lane, n_e) == e, lane < 3 * n_e)
    parts = jnp.concatenate([xg_ref[s, 0:n_slots, D:D + LANES] for s in range(n_b)], axis=0).astype(f32)
    gate = jnp.sum(jnp.where(mine, parts, 0.0), axis=1, keepdims=True)
    y = (_mm(act, wd_s[...]) * gate).astype(bf16)
    for s in range(n_b):
        y_ref[s, 0:n_slots, :] = y[s * n_slots:(s + 1) * n_slots]
        y_ref[s, n_slots:, :] = jnp.zeros((y_ref.shape[1] - n_slots, y_ref.shape[2]), bf16)


def _ffn(xg, w_gate, w_up, w_down, layer, n_slots):
    B = xg.shape[0]
    _, E, D, F = w_gate.shape
    rows = xg.shape[1] // E
    n_b = 2 if B % 2 == 0 else 1
    return pl.pallas_call(
        functools.partial(_ffn_kernel, n_slots, E),
        grid=(E, B // n_b),
        in_specs=[
            pl.BlockSpec((n_b, None, rows, D + LANES), lambda e, b: (b, e, 0, 0)),
            pl.BlockSpec((None, None, D, F), lambda e, b: (layer, e, 0, 0)),
            pl.BlockSpec((None, None, D, F), lambda e, b: (layer, e, 0, 0)),
            pl.BlockSpec((None, None, F, D), lambda e, b: (layer, e, 0, 0)),
        ],
        out_specs=pl.BlockSpec((n_b, None, rows, D), lambda e, b: (b, e, 0, 0)),
        out_shape=jax.ShapeDtypeStruct((B, E, rows, D), bf16),
        scratch_shapes=[pltpu.VMEM((D, F), bf16), pltpu.VMEM((D, F), bf16), pltpu.VMEM((F, D), bf16)],
        compiler_params=_params("arbitrary", "arbitrary"),
        name="moe_ffn",
    )(xg.reshape(B, E, rows, D + LANES), w_gate, w_up, w_down)


def _combine_kernel(n_e, rows, bnd_ref, y_ref, cpt_ref, expand_ref, o_ref):
    b, k = pl.program_id(0), pl.program_id(1)
    cpt = cpt_ref[...]
    hi_part = jnp.floor(cpt * (1.0 / 32.0))
    lo_part = cpt - 32.0 * hi_part
    expand = expand_ref[...]
    rank = 32.0 * _mm(hi_part.astype(bf16), expand) + _mm(lo_part.astype(bf16), expand)
    lane = lax.broadcasted_iota(jnp.int32, (1, n_e * MOE_WIN), 1)
    lane_e = lax.div(lane, MOE_WIN)
    starts = []
    tgt = (lane - lane_e * MOE_WIN + 1).astype(f32)
    for e in range(n_e):
        r0 = lax.div(bnd_ref[(b * n_e + e) * LANES + k], 16) * 16
        starts.append(r0)
        tgt = tgt + jnp.where(lane_e == e, r0.astype(f32), 0.0)
    onehot = jnp.where(rank == tgt, 1.0, 0.0).astype(bf16)
    ycat = jnp.concatenate(
        [y_ref[pl.ds(pl.multiple_of(e * rows + starts[e], 16), MOE_WIN), :] for e in range(n_e)], axis=0)
    o_ref[...] = _mm(onehot, ycat)

    lane_w = lax.broadcasted_iota(jnp.int32, (1, MOE_WIN), 1)
    for e in range(n_e):
        lo, hi = bnd_ref[(b * n_e + e) * LANES + k], bnd_ref[(b * n_e + e) * LANES + k + 1]
        n_win = _window_count(lo, hi, starts[e])

        @pl.when(n_win > 1)
        def _(e=e, n_win=n_win):
            rank_e = rank[:, e * MOE_WIN:(e + 1) * MOE_WIN]

            def extra(w, c):
                r = pl.multiple_of(starts[e] + w * MOE_WIN, 16)
                hit = rank_e == (lane_w + (r + 1)).astype(f32)
                o_ref[...] += _mm(jnp.where(hit, 1.0, 0.0).astype(bf16),
                                  y_ref[pl.ds(pl.multiple_of(e * rows + r, 16), MOE_WIN), :])
                return c

            lax.fori_loop(1, n_win, extra, 0)


def _combine(y, cp, bnd):
    B, E, rows, D = y.shape
    T = cp.shape[2]
    nc = T // MOE_CHUNK
    expand = jnp.asarray(np.kron(np.eye(E), np.ones((1, MOE_WIN))), bf16)
    return pl.pallas_call(
        functools.partial(_combine_kernel, E, rows),
        grid_spec=pltpu.PrefetchScalarGridSpec(
            num_scalar_prefetch=1,
            grid=(B, nc),
            in_specs=[
                pl.BlockSpec((None, E * rows, D), lambda b, k, s: (b, 0, 0), pipeline_mode=pl.Buffered(1)),
                pl.BlockSpec((None, MOE_CHUNK, E), lambda b, k, s: (b, k, 0)),
                pl.BlockSpec(expand.shape, lambda b, k, s: (0, 0)),
            ],
            out_specs=pl.BlockSpec((None, MOE_CHUNK, D), lambda b, k, s: (b, k, 0)),
        ),
        out_shape=jax.ShapeDtypeStruct((B, T, D), f32),
        compiler_params=_params("arbitrary", "arbitrary"),
        name="moe_combine",
    )(bnd, y.reshape(B, E * rows, D), cp.transpose(0, 2, 1), expand)


def _final_kernel(x_ref, moe_ref, mod_ref, o_ref):
    o_ref[...] = x_ref[...] + mod_ref[5:6, :] * moe_ref[...]


def _final(x, moe, mod, S):
    B, T, D = x.shape
    return pl.pallas_call(
        _final_kernel,
        grid=(B, S // TT),
        in_specs=[
            pl.BlockSpec((None, TT, D), lambda b, i: (b, i, 0)),
            pl.BlockSpec((None, TT, D), lambda b, i: (b, i, 0)),
            pl.BlockSpec((None, None, 6, D), lambda b, i: (b, 0, 0, 0)),
        ],
        out_specs=pl.BlockSpec((None, TT, D), lambda b, i: (b, i, 0)),
        out_shape=jax.ShapeDtypeStruct((B, S, D), f32),
        compiler_params=_params("arbitrary", "arbitrary"),
        name="final_residual",
    )(x, moe, mod)


def _rope_tables(S, L):
    t = jnp.arange(S)
    row, col = t // GRID_W, t % GRID_W
    f = 16
    inv = ROPE_BASE ** (-jnp.arange(f, dtype=f32) / f)
    ar = row.astype(f32)[:, None] * inv
    ac = col.astype(f32)[:, None] * inv
    cos = jnp.concatenate([jnp.cos(ar), jnp.cos(ar), jnp.cos(ac), jnp.cos(ac)], axis=1)
    sin = jnp.concatenate([-jnp.sin(ar), jnp.sin(ar), -jnp.sin(ac), jnp.sin(ac)], axis=1)
    cos = jnp.concatenate([jnp.tile(cos, (1, 2)), jnp.ones((L, LANES), f32)], axis=0)
    sin = jnp.concatenate([jnp.tile(sin, (1, 2)), jnp.zeros((L, LANES), f32)], axis=0)
    return cos, sin


def _pad_heads(w, n_heads, width, padded):
    lead = w.shape[:-1]
    w = w.reshape(lead + (n_heads, width))
    w = jnp.pad(w, [(0, 0)] * len(lead) + [(0, 0), (0, padded - width)])
    return w.reshape(lead + (n_heads * padded,))


def _layer_weights(l, D, w_in, norm2_g, na_q_g, na_k_g, mla_q_norm, mla_w_uq, mla_kv_norm, mla_w_ukv,
                   mla_q_g, mla_k_g, swa_q_g, swa_k_g, w_na_o, w_mla_o, w_swa_o, w_o, router):
    o = np.cumsum((0, NA_W, NA_W, NA_W, MLA_Q_RANK, MLA_KV_RANK, MLA_ROPE, SWA_W, SWA_KV_W, SWA_KV_W))
    o_naq, o_nak, o_nav, o_cq, o_ckv, o_kr, o_swq, o_swk, o_swv, o_gl = (int(v) for v in o)
    wi = w_in[l]
    swq = wi[:, o_swq:o_swq + SWA_W].reshape(D, SWA_HEADS, SWA_DIM)[:, SWA_HEAD_ORDER, :].reshape(D, SWA_W)
    kr = jnp.pad(wi[:, o_kr:o_kr + MLA_ROPE], ((0, 0), (0, LANES - MLA_ROPE)))
    w_all = jnp.concatenate([
        wi[:, o_naq:o_naq + 3 * NA_W], swq, wi[:, o_cq:o_cq + MLA_Q_RANK], wi[:, o_ckv:o_ckv + MLA_KV_RANK], kr,
        wi[:, o_swk:o_swk + 2 * SWA_KV_W], 0.5 * wi[:, o_gl:]], axis=1).astype(bf16)
    ukv = mla_w_ukv[l].reshape(MLA_KV_RANK, MLA_HEADS, MLA_NOPE + MLA_V)
    swa_o = w_swa_o[l].reshape(SWA_HEADS, SWA_DIM, D)[SWA_HEAD_ORDER, :, :].reshape(SWA_W, D)
    r_hi = router[l].astype(bf16)
    r_lo = (router[l] - r_hi.astype(f32)).astype(bf16)
    return dict(
        w_all=w_all,
        naq_g=jnp.tile(na_q_g[l], NA_HEADS)[None], nak_g=jnp.tile(na_k_g[l], NA_HEADS)[None],
        swq_g=jnp.tile(swa_q_g[l], SWA_HEADS)[None], swk_g=jnp.tile(swa_k_g[l], SWA_KV_HEADS)[None],
        qn=mla_q_norm[l][None], kvn=mla_kv_norm[l][None],
        wuq=_pad_heads(mla_w_uq[l], MLA_HEADS, MLA_QK, MLA_PAD).astype(bf16),
        wuk=ukv[:, :, :MLA_NOPE].reshape(MLA_KV_RANK, MLA_HEADS * MLA_NOPE).astype(bf16),
        wuv=ukv[:, :, MLA_NOPE:].reshape(MLA_KV_RANK, MLA_W).astype(bf16),
        qg=jnp.tile(jnp.pad(mla_q_g[l], (0, MLA_PAD - MLA_QK)), MLA_HEADS)[None],
        kg=jnp.tile(jnp.pad(mla_k_g[l], (0, MLA_PAD - MLA_QK)), MLA_HEADS)[None],
        wna=w_na_o[l].astype(bf16), wmla=w_mla_o[l].astype(bf16), wswa=swa_o.astype(bf16),
        wo=(0.5 * w_o[l]).astype(bf16), g2=norm2_g[l][None], router=jnp.concatenate([r_hi, r_lo], axis=1),
    )


def kernel(x, c, ctx, c_ctx, norm1_g, norm2_g, w_ada, b_ada, w_in, na_q_g, na_k_g, na_rpb, mla_q_norm, mla_w_uq, mla_kv_norm, mla_w_ukv, mla_q_g, mla_k_g, swa_q_g, swa_k_g, swa_sink, w_na_o, w_mla_o, w_swa_o, w_o, router, w_gate, w_up, w_down):
    B, S, D = x.shape
    L = ctx.shape[1]
    depth = w_in.shape[0]
    E = router.shape[2]
    T = S + L
    n_lat = S // TT
    rows = S // GRID_W
    assert S % TT == 0 and L % TM == 0 and L <= TT and rows >= NA_BAND_ROWS and S >= SWA_KEYS and B < 16
    cap_s = CAPACITY_FACTOR * S // E
    cap_l = CAPACITY_FACTOR * L // E

    cpad = jnp.zeros((16, D), f32).at[:B].set(c).at[B].set(c_ctx)
    mod = _ada(cpad, w_ada, b_ada).reshape(depth, 16, 6, D)
    mod = jnp.stack([mod[:, :B], jnp.broadcast_to(mod[:, B:B + 1], (depth, B, 6, D))], axis=2)

    cos, sin = _rope_tables(S, L)
    gmat = jnp.asarray(np.kron(np.eye(NA_HEADS), np.full((NA_DIM, NA_DIM), 1.0 / NA_DIM)), bf16)

    xs = jnp.concatenate([x, ctx], axis=1)
    moe = None
    for l in range(depth):
        lw = _layer_weights(l, D, w_in, norm2_g, na_q_g, na_k_g, mla_q_norm, mla_w_uq, mla_kv_norm, mla_w_ukv,
                            mla_q_g, mla_k_g, swa_q_g, swa_k_g, w_na_o, w_mla_o, w_swa_o, w_o, router)
        modp = mod[l - 1] if l > 0 else None
        xs, proj, gl = _in_proj(xs, moe, modp, mod[l], norm1_g[l][None], lw["w_all"], n_lat)
        qka, mq, mk, mv = _prep(proj, cos, sin, gmat, lw)
        ona = _na(qka, proj, _na_bias(na_rpb[l], rows), S)
        omla = _mla(mq, mk, mv, S)
        oswa = _swa(qka, proj, swa_sink[l] * LOG2E, S)
        xs, h2, aff, affx = _post(ona, omla, oswa, gl, xs, mod[l], lw, n_lat)
        cp, bnd = _topk(aff.transpose(0, 2, 1), S, cap_s, cap_l)
        bnd = bnd.astype(jnp.int32).reshape(-1)
        xg = _gather(h2, affx, cp, bnd, cap_s + cap_l)
        y = _ffn(xg, w_gate, w_up, w_down, l, cap_s + cap_l)
        moe = _combine(y, cp, bnd)
    return _final(xs, moe, mod[depth - 1], S)
```

```python
import functools

import numpy as np
import jax
import jax.numpy as jnp
from jax import lax
from jax.experimental import pallas as pl
from jax.experimental.pallas import tpu as pltpu

GRID_W = 64
NA_HEADS, NA_DIM, NA_WIN_ROWS, NA_WIN_COLS = 6, 64, 8, 16
MLA_HEADS, MLA_Q_RANK, MLA_KV_RANK, MLA_NOPE, MLA_ROPE, MLA_V = 4, 256, 128, 128, 64, 128
MLA_QK = MLA_NOPE + MLA_ROPE
MLA_PAD = 256
SWA_HEADS, SWA_KV_HEADS, SWA_DIM, SWA_WINDOW = 6, 2, 64, 128
CAPACITY_FACTOR = 2
N_BRANCH = 3
ROPE_BASE = 10000.0
EPS = 1e-6

NA_W = NA_HEADS * NA_DIM
SWA_W = SWA_HEADS * SWA_DIM
SWA_KV_W = SWA_KV_HEADS * SWA_DIM
MLA_W = MLA_HEADS * MLA_V

LANES = 128
TM = 256
TT = 512
MLA_TQ = 1024
MLA_TK = 1024
LOG2E = 1.4426950408889634
NA_TILE_ROWS = TM // GRID_W
NA_BAND_ROWS = NA_TILE_ROWS + NA_WIN_ROWS - 1
NA_BAND = NA_BAND_ROWS * GRID_W
SWA_KEYS = TM + 2 * SWA_WINDOW
MOE_CHUNK = 256
MOE_WIN = 64
NEG = -1e30
VMEM_LIMIT = 56 * 1024 * 1024

P_NAQ, P_NAK, P_NAV, P_SWQ = 0, 384, 768, 1152
P_CQ, P_CKV, P_KR, P_SWK, P_SWV = 1536, 1792, 1920, 2048, 2176
PROJ_W = 2304
N_CHUNK = 768
A_NAQ, A_NAK, A_SWQ, A_SWK = 0, 384, 768, 1152
QKA_W = 1280
SWA_HEAD_ORDER = (0, 3, 1, 4, 2, 5)

bf16 = jnp.bfloat16
f32 = jnp.float32


def _mm(a, b):
    return jnp.dot(a, b, preferred_element_type=f32)


def _nt(a, b):
    return lax.dot_general(a, b, (((1,), (1,)), ((), ())), preferred_element_type=f32)


def _params(*sem):
    return pltpu.CompilerParams(dimension_semantics=sem, vmem_limit_bytes=VMEM_LIMIT)


def _ada_kernel(c_ref, w_ref, b_ref, o_ref):
    a = c_ref[...]
    a = (a * jax.nn.sigmoid(a)).astype(bf16)
    o_ref[...] = _mm(a, w_ref[...].astype(bf16)) + b_ref[...]


def _ada(cpad, w_ada, b_ada):
    L, D, N = w_ada.shape
    tn = 1536
    return pl.pallas_call(
        _ada_kernel,
        grid=(L, N // tn),
        in_specs=[
            pl.BlockSpec((16, D), lambda l, j: (0, 0)),
            pl.BlockSpec((None, D, tn), lambda l, j: (l, 0, j)),
            pl.BlockSpec((None, 1, tn), lambda l, j: (l, 0, j)),
        ],
        out_specs=pl.BlockSpec((None, 16, tn), lambda l, j: (l, 0, j)),
        out_shape=jax.ShapeDtypeStruct((L, 16, N), f32),
        compiler_params=_params("arbitrary", "arbitrary"),
        name="ada",
    )(cpad, w_ada, b_ada.reshape(L, 1, N))


def _in_proj_kernel(has_moe, n_gl_chunks, *refs):
    if has_moe:
        x_ref, moe_ref, modp_ref, mod_ref, g_ref, w_ref, xo_ref, proj_ref, gl_ref = refs
    else:
        x_ref, mod_ref, g_ref, w_ref, proj_ref, gl_ref = refs
    n_sub = 2
    sub = x_ref.shape[0] // n_sub
    for r in range(n_sub):
        rows = slice(r * sub, (r + 1) * sub)
        x = x_ref[rows, :]
        if has_moe:
            x = x + modp_ref[5:6, :] * moe_ref[rows, :]
            xo_ref[rows, :] = x
        ms = jnp.mean(x * x, axis=-1, keepdims=True)
        h = x * lax.rsqrt(ms + EPS) * (g_ref[...] * (1.0 + mod_ref[1:2, :])) + mod_ref[0:1, :]
        h = h.astype(bf16)
        for c in range(PROJ_W // N_CHUNK):
            sl = slice(c * N_CHUNK, (c + 1) * N_CHUNK)
            proj_ref[rows, sl] = _mm(h, w_ref[:, sl]).astype(bf16)
        for c in range(n_gl_chunks):
            sl = slice(c * N_CHUNK, (c + 1) * N_CHUNK)
            gl_ref[rows, sl] = _mm(h, w_ref[:, PROJ_W + c * N_CHUNK:PROJ_W + (c + 1) * N_CHUNK]).astype(bf16)


def _mod_spec(n_lat, D):
    return pl.BlockSpec((None, None, 6, D), lambda b, i: (b, jnp.where(i >= n_lat, 1, 0), 0, 0))


def _in_proj(x, moe, modp, mod, g1, w, n_lat):
    B, T, D = x.shape
    nt = pl.cdiv(T, TT)
    has_moe = moe is not None
    n_gl_chunks = N_BRANCH * D // N_CHUNK
    tok = pl.BlockSpec((None, TT, D), lambda b, i: (b, i, 0))
    in_specs = [tok]
    args = [x]
    if has_moe:
        in_specs += [tok, _mod_spec(n_lat, D)]
        args += [moe, modp]
    in_specs += [
        _mod_spec(n_lat, D),
        pl.BlockSpec((1, D), lambda b, i: (0, 0)),
        pl.BlockSpec(w.shape, lambda b, i: (0, 0), pipeline_mode=pl.Buffered(1)),
    ]
    args += [mod, g1, w]
    out_specs = [
        pl.BlockSpec((None, TT, PROJ_W), lambda b, i: (b, i, 0)),
        pl.BlockSpec((None, TT, N_BRANCH * D), lambda b, i: (b, i, 0)),
    ]
    out_shape = [
        jax.ShapeDtypeStruct((B, T, PROJ_W), bf16),
        jax.ShapeDtypeStruct((B, T, N_BRANCH * D), bf16),
    ]
    if has_moe:
        out_specs = [tok] + out_specs
        out_shape = [jax.ShapeDtypeStruct((B, T, D), f32)] + out_shape
    res = pl.pallas_call(
        functools.partial(_in_proj_kernel, has_moe, n_gl_chunks),
        grid=(B, nt),
        in_specs=in_specs,
        out_specs=out_specs,
        out_shape=out_shape,
        compiler_params=_params("arbitrary", "arbitrary"),
        name="in_proj",
    )(*args)
    if has_moe:
        return res
    return [x] + list(res)


def _prep_kernel(proj_ref, cos_ref, sin_ref, perm_ref, gmat_ref, naq_g, nak_g, swq_g, swk_g,
                 qn_ref, wuq_ref, qg_ref, kvn_ref, wuk_ref, wuv_ref, kg_ref,
                 qka_ref, mq_ref, mk_ref, mv_ref):
    cos = cos_ref[...]
    sin = sin_ref[...]
    perm = perm_ref[...]

    def rope(x):
        hi = x.astype(bf16)
        lo = (x - hi.astype(f32)).astype(bf16)
        partner = _mm(hi, perm) + _mm(lo, perm)
        return x * cos + partner * sin

    def headnorm(x, gain):
        w = x.shape[1]
        ms = _mm((x * x).astype(bf16), gmat_ref[:w, :w])
        return x * lax.rsqrt(ms + EPS) * gain

    def rmsnorm(x, gain):
        ms = jnp.mean(x * x, axis=-1, keepdims=True)
        return x * lax.rsqrt(ms + EPS) * gain

    na_scale = NA_DIM ** -0.5 * LOG2E
    naq = headnorm(proj_ref[:, P_NAQ:P_NAQ + NA_W].astype(f32), naq_g[...])
    qka_ref[:, A_NAQ:A_NAQ + NA_W] = (naq * na_scale).astype(bf16)
    nak = headnorm(proj_ref[:, P_NAK:P_NAK + NA_W].astype(f32), nak_g[...])
    qka_ref[:, A_NAK:A_NAK + NA_W] = nak.astype(bf16)

    sw_scale = SWA_DIM ** -0.5 * LOG2E
    swq = headnorm(proj_ref[:, P_SWQ:P_SWQ + SWA_W].astype(f32), swq_g[...])
    for p in range(SWA_W // LANES):
        sl = slice(p * LANES, (p + 1) * LANES)
        qka_ref[:, A_SWQ + p * LANES:A_SWQ + (p + 1) * LANES] = (rope(swq[:, sl]) * sw_scale).astype(bf16)
    swk = headnorm(proj_ref[:, P_SWK:P_SWK + SWA_KV_W].astype(f32), swk_g[...])
    qka_ref[:, A_SWK:A_SWK + SWA_KV_W] = rope(swk).astype(bf16)

    mla_scale = MLA_QK ** -0.5 * LOG2E
    cq = rmsnorm(proj_ref[:, P_CQ:P_CQ + MLA_Q_RANK].astype(f32), qn_ref[...]).astype(bf16)
    q = _mm(cq, wuq_ref[...])
    ckv = rmsnorm(proj_ref[:, P_CKV:P_CKV + MLA_KV_RANK].astype(f32), kvn_ref[...]).astype(bf16)
    kn = _mm(ckv, wuk_ref[...])
    v = _mm(ckv, wuv_ref[...]).astype(bf16)
    ones_col = jnp.where(lax.broadcasted_iota(jnp.int32, (v.shape[0], LANES), 1) == 0, 1.0, 0.0).astype(bf16)
    for h in range(MLA_HEADS):
        mv_ref[:, h * MLA_PAD:h * MLA_PAD + MLA_V] = v[:, h * MLA_V:(h + 1) * MLA_V]
        mv_ref[:, h * MLA_PAD + MLA_V:(h + 1) * MLA_PAD] = ones_col
    kr = proj_ref[:, P_KR:P_KR + LANES].astype(f32)
    kr_ss = jnp.sum(kr * kr, axis=-1, keepdims=True)
    for h in range(MLA_HEADS):
        o = h * MLA_PAD
        qh = q[:, o:o + MLA_PAD]
        r = lax.rsqrt(jnp.sum(qh * qh, axis=-1, keepdims=True) * (1.0 / MLA_QK) + EPS)
        qh = qh * r * qg_ref[:, o:o + MLA_PAD]
        mq_ref[:, o:o + LANES] = (qh[:, :LANES] * mla_scale).astype(bf16)
        mq_ref[:, o + LANES:o + MLA_PAD] = (rope(qh[:, LANES:]) * mla_scale).astype(bf16)
        kh = kn[:, h * MLA_NOPE:(h + 1) * MLA_NOPE]
        r = lax.rsqrt((jnp.sum(kh * kh, axis=-1, keepdims=True) + kr_ss) * (1.0 / MLA_QK) + EPS)
        mk_ref[:, o:o + LANES] = (kh * r * kg_ref[:, o:o + LANES]).astype(bf16)
        mk_ref[:, o + LANES:o + MLA_PAD] = rope(kr * r * kg_ref[:, o + LANES:o + MLA_PAD]).astype(bf16)


def _prep(proj, cos, sin, gmat, lw):
    B, T, _ = proj.shape
    nt = pl.cdiv(T, TT)

    def const(a):
        return pl.BlockSpec(a.shape, lambda b, i: (0,) * a.ndim)

    lane = np.arange(LANES)
    perm = jnp.asarray(lane[:, None] == (lane[None, :] ^ 16), bf16)
    consts = [perm, gmat, lw["naq_g"], lw["nak_g"], lw["swq_g"], lw["swk_g"], lw["qn"], lw["wuq"], lw["qg"],
              lw["kvn"], lw["wuk"], lw["wuv"], lw["kg"]]
    widths = (QKA_W, MLA_HEADS * MLA_PAD, MLA_HEADS * MLA_PAD, MLA_HEADS * MLA_PAD)
    return pl.pallas_call(
        _prep_kernel,
        grid=(B, nt),
        in_specs=[
            pl.BlockSpec((None, TT, PROJ_W), lambda b, i: (b, i, 0)),
            pl.BlockSpec((TT, LANES), lambda b, i: (i, 0)),
            pl.BlockSpec((TT, LANES), lambda b, i: (i, 0)),
        ] + [const(a) for a in consts],
        out_specs=[pl.BlockSpec((None, TT, w), lambda b, i: (b, i, 0)) for w in widths],
        out_shape=[jax.ShapeDtypeStruct((B, T, w), bf16) for w in widths],
        compiler_params=_params("arbitrary", "arbitrary"),
        name="prep",
    )(proj, cos, sin, *consts)


def _half_masks():
    lane = lax.broadcasted_iota(jnp.int32, (1, LANES), 1)
    lo = jnp.where(lane < 64, 1.0, 0.0).astype(bf16)
    return lo, (1.0 - lo.astype(f32)).astype(bf16)


def _pair_attend(qp, parts, masks, sinks=None):
    n_q = qp.shape[0]
    q2 = jnp.concatenate([qp * masks[0], qp * masks[1]], axis=0)
    scores = []
    for k, _, bias in parts:
        s = _nt(q2, k)
        scores.append(s if bias is None else s + bias)
    m = functools.reduce(jnp.maximum, [jnp.max(s, axis=-1, keepdims=True) for s in scores])
    if sinks is not None:
        first = lax.broadcasted_iota(jnp.int32, (2 * n_q, 1), 0) < n_q
        sink = jnp.where(first, sinks[0], sinks[1])
        m = jnp.maximum(m, sink)
    ps = [jnp.exp2(s - m) for s in scores]
    l = functools.reduce(jnp.add, [jnp.sum(p, axis=-1, keepdims=True) for p in ps])
    if sinks is not None:
        l = l + jnp.exp2(sink - m)
    o = functools.reduce(jnp.add, [_mm(p.astype(bf16), v) for p, (_, v, _) in zip(ps, parts)]) / l
    lane = lax.broadcasted_iota(jnp.int32, (1, LANES), 1)
    return jnp.where(lane < 64, o[:n_q], o[n_q:])


def _na_kernel(n_lat, S, rows, q_ref, k_ref, v_ref, bias_ref, o_ref):
    i = pl.program_id(1)
    masks = _half_masks()
    T = k_ref.shape[0]

    @pl.when(i < n_lat)
    def _():
        start = GRID_W * jnp.clip(NA_TILE_ROWS * i - NA_WIN_ROWS // 2, 0, rows - NA_BAND_ROWS)
        start = pl.multiple_of(start, GRID_W)
        for p in range(NA_W // LANES):
            sl = slice(p * LANES, (p + 1) * LANES)
            band = (k_ref[pl.ds(start, NA_BAND), sl], v_ref[pl.ds(start, NA_BAND), sl],
                    bias_ref[2 * p:2 * p + 2].reshape(2 * TM, NA_BAND))
            ctx = (k_ref[S:T, sl], v_ref[S:T, sl], None)
            o_ref[:, sl] = _pair_attend(q_ref[:, sl], [band, ctx], masks).astype(bf16)

    @pl.when(i >= n_lat)
    def _():
        for p in range(NA_W // LANES):
            sl = slice(p * LANES, (p + 1) * LANES)
            ctx = (k_ref[S:T, sl], v_ref[S:T, sl], None)
            o_ref[:, sl] = _pair_attend(q_ref[:, sl], [ctx], masks).astype(bf16)


def _na(qka, proj, bias, S):
    B, T, _ = qka.shape
    nt, n_lat, rows = T // TM, S // TM, S // GRID_W

    def bias_idx(b, i):
        return (jnp.where(i == 0, 0, jnp.where(i >= n_lat - 1, 2, 1)), 0, 0, 0)

    return pl.pallas_call(
        functools.partial(_na_kernel, n_lat, S, rows),
        grid=(B, nt),
        in_specs=[
            pl.BlockSpec((None, TM, NA_W), lambda b, i: (b, i, A_NAQ // NA_W)),
            pl.BlockSpec((None, T, NA_W), lambda b, i: (b, 0, A_NAK // NA_W)),
            pl.BlockSpec((None, T, NA_W), lambda b, i: (b, 0, P_NAV // NA_W)),
            pl.BlockSpec((None, NA_HEADS, TM, NA_BAND), bias_idx),
        ],
        out_specs=pl.BlockSpec((None, TM, NA_W), lambda b, i: (b, i, 0)),
        out_shape=jax.ShapeDtypeStruct((B, T, NA_W), bf16),
        compiler_params=_params("arbitrary", "arbitrary"),
        name="na_attn",
    )(qka, qka, proj, bias)


def _na_bias(rpb, rows):
    n_tiles = rows // NA_TILE_ROWS
    col = np.arange(GRID_W)
    c0 = np.clip(col - NA_WIN_COLS // 2, 0, GRID_W - NA_WIN_COLS)
    col_ok = (col[None, :] >= c0[:, None]) & (col[None, :] < c0[:, None] + NA_WIN_COLS)
    dc = np.clip(col[None, :] - col[:, None] + NA_WIN_COLS - 1, 0, 2 * NA_WIN_COLS - 2)
    tz = jnp.where(col_ok, rpb[:, :, dc] * LOG2E, NEG)
    dr = np.zeros((3, NA_TILE_ROWS, NA_BAND_ROWS), np.int32)
    ok = np.zeros((3, NA_TILE_ROWS, NA_BAND_ROWS), bool)
    for ty, rt in enumerate((0, 1, n_tiles - 1)):
        bs = int(np.clip(NA_TILE_ROWS * rt - NA_WIN_ROWS // 2, 0, rows - NA_BAND_ROWS))
        for qr in range(NA_TILE_ROWS):
            r = NA_TILE_ROWS * rt + qr
            s_r = int(np.clip(r - NA_WIN_ROWS // 2, 0, rows - NA_WIN_ROWS))
            for w in range(NA_BAND_ROWS):
                kr = bs + w
                ok[ty, qr, w] = s_r <= kr < s_r + NA_WIN_ROWS
                dr[ty, qr, w] = np.clip(kr - r + NA_WIN_ROWS - 1, 0, 2 * NA_WIN_ROWS - 2)
    blk = tz[:, dr]
    blk = jnp.where(ok[None, :, :, :, None, None], blk, NEG)
    blk = blk.transpose(1, 0, 2, 4, 3, 5)
    return blk.reshape(3, NA_HEADS, TM, NA_BAND)


def _mla_kernel(n_q, S, q_ref, k_ref, v_ref, o_ref):
    i = pl.program_id(2)
    T = k_ref.shape[0]

    def attend(q, bounds):
        m_run = acc = None
        for lo, hi in bounds:
            s = _nt(q, k_ref[lo:hi, :])
            m_new = jnp.max(s, axis=-1, keepdims=True)
            if m_run is not None:
                m_new = jnp.maximum(m_run, m_new)
            pv = _mm(jnp.exp2(s - m_new).astype(bf16), v_ref[lo:hi, :])
            acc = pv if acc is None else acc * jnp.exp2(m_run - m_new) + pv
            m_run = m_new
        return (acc[:, :MLA_V] / acc[:, MLA_V:MLA_V + 1]).astype(bf16)

    @pl.when(i < n_q)
    def _():
        chunks = [(lo, lo + MLA_TK) for lo in range(0, S, MLA_TK)] + [(S, T)]
        o_ref[...] = attend(q_ref[...], chunks)

    @pl.when(i >= n_q)
    def _():
        o_ref[0:T - S, :] = attend(q_ref[0:T - S, :], [(S, T)])


def _mla(mq, mk, mv, S):
    B, T, _ = mq.shape
    n_q = S // MLA_TQ
    assert S % MLA_TQ == 0 and S % MLA_TK == 0 and T - S <= MLA_TQ
    return pl.pallas_call(
        functools.partial(_mla_kernel, n_q, S),
        grid=(B, MLA_HEADS, n_q + 1),
        in_specs=[
            pl.BlockSpec((None, MLA_TQ, MLA_PAD), lambda b, h, i: (b, i, h)),
            pl.BlockSpec((None, T, MLA_PAD), lambda b, h, i: (b, 0, h)),
            pl.BlockSpec((None, T, MLA_PAD), lambda b, h, i: (b, 0, h)),
        ],
        out_specs=pl.BlockSpec((None, MLA_TQ, MLA_V), lambda b, h, i: (b, i, h)),
        out_shape=jax.ShapeDtypeStruct((B, T, MLA_W), bf16),
        compiler_params=_params("arbitrary", "arbitrary", "arbitrary"),
        name="mla_attn",
    )(mq, mk, mv)


def _swa_kernel(n_lat, S, sink_ref, q_ref, k_ref, v_ref, o_ref):
    i = pl.program_id(1)
    masks = _half_masks()
    T = k_ref.shape[0]
    kc, vc = k_ref[S:T, :], v_ref[S:T, :]

    def sinks(p):
        return (sink_ref[SWA_HEAD_ORDER[2 * p]], sink_ref[SWA_HEAD_ORDER[2 * p + 1]])

    @pl.when(i < n_lat)
    def _():
        start = pl.multiple_of(jnp.clip(i * TM - SWA_WINDOW, 0, S - SWA_KEYS), SWA_WINDOW)
        row = lax.broadcasted_iota(jnp.int32, (2 * TM, SWA_KEYS), 0)
        qpos = i * TM + jnp.where(row >= TM, row - TM, row)
        kpos = start + lax.broadcasted_iota(jnp.int32, (2 * TM, SWA_KEYS), 1)
        band = jnp.where(jnp.abs(qpos - kpos) <= SWA_WINDOW, 0.0, NEG)
        kb, vb = k_ref[pl.ds(start, SWA_KEYS), :], v_ref[pl.ds(start, SWA_KEYS), :]
        for p in range(SWA_W // LANES):
            sl = slice(p * LANES, (p + 1) * LANES)
            parts = [(kb, vb, band), (kc, vc, None)]
            o_ref[:, sl] = _pair_attend(q_ref[:, sl], parts, masks, sinks(p)).astype(bf16)

    @pl.when(i >= n_lat)
    def _():
        for p in range(SWA_W // LANES):
            sl = slice(p * LANES, (p + 1) * LANES)
            o_ref[:, sl] = _pair_attend(q_ref[:, sl], [(kc, vc, None)], masks, sinks(p)).astype(bf16)


def _swa(qka, proj, sink, S):
    B, T, _ = qka.shape
    nt, n_lat = T // TM, S // TM
    return pl.pallas_call(
        functools.partial(_swa_kernel, n_lat, S),
        grid=(B, nt),
        in_specs=[
            pl.BlockSpec(memory_space=pltpu.SMEM),
            pl.BlockSpec((None, TM, SWA_W), lambda b, i: (b, i, A_SWQ // SWA_W)),
            pl.BlockSpec((None, T, SWA_KV_W), lambda b, i: (b, 0, A_SWK // SWA_KV_W)),
            pl.BlockSpec((None, T, SWA_KV_W), lambda b, i: (b, 0, P_SWV // SWA_KV_W)),
        ],
        out_specs=pl.BlockSpec((None, TM, SWA_W), lambda b, i: (b, i, 0)),
        out_shape=jax.ShapeDtypeStruct((B, T, SWA_W), bf16),
        compiler_params=_params("arbitrary", "arbitrary"),
        name="swa_attn",
    )(sink, qka, qka, proj)


def _post_kernel(ona_ref, omla_ref, oswa_ref, gl_ref, x_ref, mod_ref, wna_ref, wmla_ref, wswa_ref, wo_ref,
                 g2_ref, rt_ref, xo_ref, h2_ref, aff_ref, affx_ref):
    D = x_ref.shape[1]
    n_e = aff_ref.shape[1]
    n_sub = 2
    sub = x_ref.shape[0] // n_sub
    for r in range(n_sub):
        rows = slice(r * sub, (r + 1) * sub)
        merged = None
        for j, (o_ref, w_ref) in enumerate(((ona_ref, wna_ref), (omla_ref, wmla_ref), (oswa_ref, wswa_ref))):
            y = _mm(o_ref[rows, :], w_ref[...])
            gy = (jnp.tanh(gl_ref[rows, j * D:(j + 1) * D].astype(f32)) + 1.0) * y
            merged = gy if merged is None else merged + gy
        res = _mm(merged.astype(bf16), wo_ref[...])
        x = x_ref[rows, :] + mod_ref[2:3, :] * res
        xo_ref[rows, :] = x
        ms = jnp.mean(x * x, axis=-1, keepdims=True)
        h2 = x * lax.rsqrt(ms + EPS) * (g2_ref[...] * (1.0 + mod_ref[4:5, :])) + mod_ref[3:4, :]
        h_hi = h2.astype(bf16)
        h2_ref[rows, :] = h_hi
        h_lo = (h2 - h_hi.astype(f32)).astype(bf16)
        parts = _mm(h_hi, rt_ref[...]) + _mm(h_lo, rt_ref[...])
        logits = parts[:, :n_e] + parts[:, n_e:]
        e = jnp.exp(logits - jnp.max(logits, axis=1, keepdims=True))
        aff = e / jnp.sum(e, axis=1, keepdims=True)
        aff_ref[rows, :] = aff
        p0 = aff.astype(bf16)
        r1 = aff - p0.astype(f32)
        p1 = r1.astype(bf16)
        p2 = (r1 - p1.astype(f32)).astype(bf16)
        pad = jnp.zeros((sub, LANES - 3 * n_e), bf16)
        affx_ref[rows, :] = jnp.concatenate([p0, p1, p2, pad], axis=1)


def _post(ona, omla, oswa, gl, x, mod, lw, n_lat):
    B, T, D = x.shape
    nt = pl.cdiv(T, TT)
    E = lw["router"].shape[1] // 2

    def tok(w):
        return pl.BlockSpec((None, TT, w), lambda b, i: (b, i, 0))

    def const(a):
        return pl.BlockSpec(a.shape, lambda b, i: (0,) * a.ndim)

    consts = [lw["wna"], lw["wmla"], lw["wswa"], lw["wo"], lw["g2"], lw["router"]]
    return pl.pallas_call(
        _post_kernel,
        grid=(B, nt),
        in_specs=[tok(NA_W), tok(MLA_W), tok(SWA_W), tok(N_BRANCH * D), tok(D), _mod_spec(n_lat, D)]
        + [const(a) for a in consts],
        out_specs=[tok(D), tok(D), tok(E), tok(LANES)],
        out_shape=[jax.ShapeDtypeStruct((B, T, D), f32), jax.ShapeDtypeStruct((B, T, D), bf16),
                   jax.ShapeDtypeStruct((B, T, E), f32), jax.ShapeDtypeStruct((B, T, LANES), bf16)],
        compiler_params=_params("arbitrary", "arbitrary"),
        name="post_attn",
    )(ona, omla, oswa, gl, x, mod, *consts)


def _lane_cumsum(mask, tri):
    E, n = mask.shape
    carry = jnp.zeros((E, 1), f32)
    outs = []
    for k in range(n // LANES):
        w = _mm(mask[:, k * LANES:(k + 1) * LANES].astype(bf16), tri) + carry
        outs.append(w)
        carry = w[:, LANES - 1:LANES]
    return jnp.concatenate(outs, axis=1)


def _select_slots(aff, cap, base, tri):
    bits = lax.bitcast_convert_type(aff, jnp.int32)
    thr = jnp.zeros((aff.shape[0], 1), jnp.int32)
    for bit in range(30, -1, -1):
        cand = thr | (1 << bit)
        cnt = jnp.sum(jnp.where(bits >= cand, 1.0, 0.0), axis=1, keepdims=True)
        thr = jnp.where(cnt >= cap, cand, thr)
    gt = jnp.where(bits > thr, 1.0, 0.0)
    eq = jnp.where(bits == thr, 1.0, 0.0)
    need = cap - jnp.sum(gt, axis=1, keepdims=True)
    sel = jnp.maximum(gt, jnp.where(_lane_cumsum(eq, tri) <= need, eq, 0.0))
    return jnp.where(sel > 0.0, _lane_cumsum(sel, tri) + base, 0.0)


def _topk_kernel(S, cap_s, cap_l, aff_ref, tri_ref, before_ref, cp_ref, bnd_ref):
    T = aff_ref.shape[1]
    tri = tri_ref[...]
    cp_ref[:, 0:S] = _select_slots(aff_ref[:, 0:S], cap_s, 0.0, tri)
    cp_ref[:, S:T] = _select_slots(aff_ref[:, S:T], cap_l, float(cap_s), tri)
    bnd_ref[...] = _mm(jnp.where(cp_ref[...] > 0.0, 1.0, 0.0).astype(bf16), before_ref[...])


def _topk(aff, S, cap_s, cap_l):
    B, E, T = aff.shape
    tri = jnp.asarray(np.triu(np.ones((LANES, LANES), np.float32)), bf16)
    before = jnp.asarray(np.arange(T)[:, None] < MOE_CHUNK * np.arange(LANES)[None, :], bf16)
    return pl.pallas_call(
        functools.partial(_topk_kernel, S, cap_s, cap_l),
        grid=(B,),
        in_specs=[pl.BlockSpec((None, E, T), lambda b: (b, 0, 0)),
                  pl.BlockSpec((LANES, LANES), lambda b: (0, 0)),
                  pl.BlockSpec((T, LANES), lambda b: (0, 0))],
        out_specs=[pl.BlockSpec((None, E, T), lambda b: (b, 0, 0)),
                   pl.BlockSpec((None, E, LANES), lambda b: (b, 0, 0))],
        out_shape=[jax.ShapeDtypeStruct((B, E, T), f32), jax.ShapeDtypeStruct((B, E, LANES), f32)],
        compiler_params=_params("arbitrary"),
        name="expert_select",
    )(aff, tri, before)


def _window_count(lo, hi, r0):
    return jnp.where(hi > lo, lax.div(hi - r0 + (MOE_WIN - 1), MOE_WIN), 0)


def _gather_kernel(n_e, rows, bnd_ref, h2_ref, affx_ref, cp_ref, xg_ref):
    b, k = pl.program_id(0), pl.program_id(1)
    D = h2_ref.shape[1]

    @pl.when(k == 0)
    def _():
        xg_ref[...] = jnp.zeros_like(xg_ref)

    ex = lax.broadcasted_iota(jnp.int32, (n_e, 1), 0)
    starts = []
    start_col = jnp.zeros((n_e, 1), f32)
    for e in range(n_e):
        r0 = lax.div(bnd_ref[(b * n_e + e) * LANES + k], 16) * 16
        starts.append(r0)
        start_col = start_col + jnp.where(ex == e, r0.astype(f32), 0.0)
    rel = cp_ref[...] - start_col
    rel_rows = jnp.broadcast_to(rel[:, None, :], (n_e, MOE_WIN, MOE_CHUNK)).reshape(n_e * MOE_WIN, MOE_CHUNK)
    j1 = lax.broadcasted_iota(jnp.int32, (n_e, MOE_WIN, 1), 1).reshape(n_e * MOE_WIN, 1) + 1
    onehot = jnp.where(rel_rows == j1.astype(f32), 1.0, 0.0).astype(bf16)
    tok, tok_aff = h2_ref[...], affx_ref[...]
    moved = _mm(onehot, tok).astype(bf16)
    moved_aff = _mm(onehot, tok_aff).astype(bf16)
    for e in range(n_e):
        dst = pl.ds(pl.multiple_of(e * rows + starts[e], 16), MOE_WIN)
        xg_ref[dst, 0:D] += moved[e * MOE_WIN:(e + 1) * MOE_WIN]
        xg_ref[dst, D:D + LANES] += moved_aff[e * MOE_WIN:(e + 1) * MOE_WIN]

    sub1 = lax.broadcasted_iota(jnp.int32, (MOE_WIN, 1), 0) + 1
    for e in range(n_e):
        lo, hi = bnd_ref[(b * n_e + e) * LANES + k], bnd_ref[(b * n_e + e) * LANES + k + 1]
        n_win = _window_count(lo, hi, starts[e])

        @pl.when(n_win > 1)
        def _(e=e, n_win=n_win):
            def extra(w, c):
                hit = rel[e:e + 1, :] == (sub1 + w * MOE_WIN).astype(f32)
                oh = jnp.where(hit, 1.0, 0.0).astype(bf16)
                dst = pl.ds(pl.multiple_of(e * rows + starts[e] + w * MOE_WIN, 16), MOE_WIN)
                xg_ref[dst, 0:D] += _mm(oh, tok).astype(bf16)
                xg_ref[dst, D:D + LANES] += _mm(oh, tok_aff).astype(bf16)
                return c

            lax.fori_loop(1, n_win, extra, 0)


def _gather(h2, affx, cp, bnd, n_slots):
    B, T, D = h2.shape
    E = cp.shape[1]
    nc = T // MOE_CHUNK
    rows = n_slots + MOE_WIN
    assert n_slots % 16 == 0 and nc + 1 <= LANES and 3 * E <= LANES
    return pl.pallas_call(
        functools.partial(_gather_kernel, E, rows),
        grid_spec=pltpu.PrefetchScalarGridSpec(
            num_scalar_prefetch=1,
            grid=(B, nc),
            in_specs=[
                pl.BlockSpec((None, MOE_CHUNK, D), lambda b, k, s: (b, k, 0)),
                pl.BlockSpec((None, MOE_CHUNK, LANES), lambda b, k, s: (b, k, 0)),
                pl.BlockSpec((None, E, MOE_CHUNK), lambda b, k, s: (b, 0, k)),
            ],
            out_specs=pl.BlockSpec((None, E * rows, D + LANES), lambda b, k, s: (b, 0, 0),
                                   pipeline_mode=pl.Buffered(1)),
        ),
        out_shape=jax.ShapeDtypeStruct((B, E * rows, D + LANES), bf16),
        compiler_params=_params("arbitrary", "arbitrary"),
        name="moe_gather",
    )(bnd, h2, affx, cp)


def _ffn_kernel(n_slots, n_e, xg_ref, wg_ref, wu_ref, wd_ref, y_ref, wg_s, wu_s, wd_s):
    e, b = pl.program_id(0), pl.program_id(1)
    D = wg_ref.shape[0]

    @pl.when(b == 0)
    def _():
        wg_s[...] = wg_ref[...].astype(bf16)
        wu_s[...] = wu_ref[...].astype(bf16)
        wd_s[...] = wd_ref[...].astype(bf16)

    n_b = xg_ref.shape[0]
    xg = jnp.concatenate([xg_ref[s, 0:n_slots, 0:D] for s in range(n_b)], axis=0)
    a = _mm(xg, wg_s[...])
    u = _mm(xg, wu_s[...])
    act = (a * jax.nn.sigmoid(a) * u).astype(bf16)
    lane = lax.broadcasted_iota(jnp.int32, (1, LANES), 1)
    mine = jnp.logical_and(lax.rem(lane, n_e) == e, lane < 3 * n_e)
    parts = jnp.concatenate([xg_ref[s, 0:n_slots, D:D + LANES] for s in range(n_b)], axis=0).astype(f32)
    gate = jnp.sum(jnp.where(mine, parts, 0.0), axis=1, keepdims=True)
    y = (_mm(act, wd_s[...]) * gate).astype(bf16)
    for s in range(n_b):
        y_ref[s, 0:n_slots, :] = y[s * n_slots:(s + 1) * n_slots]
        y_ref[s, n_slots:, :] = jnp.zeros((y_ref.shape[1] - n_slots, y_ref.shape[2]), bf16)


def _ffn(xg, w_gate, w_up, w_down, layer, n_slots):
    B = xg.shape[0]
    _, E, D, F = w_gate.shape
    rows = xg.shape[1] // E
    n_b = 2 if B % 2 == 0 else 1
    return pl.pallas_call(
        functools.partial(_ffn_kernel, n_slots, E),
        grid=(E, B // n_b),
        in_specs=[
            pl.BlockSpec((n_b, None, rows, D + LANES), lambda e, b: (b, e, 0, 0)),
            pl.BlockSpec((None, None, D, F), lambda e, b: (layer, e, 0, 0)),
            pl.BlockSpec((None, None, D, F), lambda e, b: (layer, e, 0, 0)),
            pl.BlockSpec((None, None, F, D), lambda e, b: (layer, e, 0, 0)),
        ],
        out_specs=pl.BlockSpec((n_b, None, rows, D), lambda e, b: (b, e, 0, 0)),
        out_shape=jax.ShapeDtypeStruct((B, E, rows, D), bf16),
        scratch_shapes=[pltpu.VMEM((D, F), bf16), pltpu.VMEM((D, F), bf16), pltpu.VMEM((F, D), bf16)],
        compiler_params=_params("arbitrary", "arbitrary"),
        name="moe_ffn",
    )(xg.reshape(B, E, rows, D + LANES), w_gate, w_up, w_down)


def _combine_kernel(n_e, rows, residual, bnd_ref, y_ref, cpt_ref, expand_ref, *refs):
    o_ref = refs[-1]
    b, k = pl.program_id(0), pl.program_id(1)
    cpt = cpt_ref[...]
    hi_part = jnp.floor(cpt * (1.0 / 32.0))
    lo_part = cpt - 32.0 * hi_part
    expand = expand_ref[...]
    rank = 32.0 * _mm(hi_part.astype(bf16), expand) + _mm(lo_part.astype(bf16), expand)
    lane = lax.broadcasted_iota(jnp.int32, (1, n_e * MOE_WIN), 1)
    lane_e = lax.div(lane, MOE_WIN)
    starts = []
    tgt = (lane - lane_e * MOE_WIN + 1).astype(f32)
    for e in range(n_e):
        r0 = lax.div(bnd_ref[(b * n_e + e) * LANES + k], 16) * 16
        starts.append(r0)
        tgt = tgt + jnp.where(lane_e == e, r0.astype(f32), 0.0)
    onehot = jnp.where(rank == tgt, 1.0, 0.0).astype(bf16)
    ycat = jnp.concatenate(
        [y_ref[pl.ds(pl.multiple_of(e * rows + starts[e], 16), MOE_WIN), :] for e in range(n_e)], axis=0)
    o_ref[...] = _mm(onehot, ycat)

    lane_w = lax.broadcasted_iota(jnp.int32, (1, MOE_WIN), 1)
    for e in range(n_e):
        lo, hi = bnd_ref[(b * n_e + e) * LANES + k], bnd_ref[(b * n_e + e) * LANES + k + 1]
        n_win = _window_count(lo, hi, starts[e])

        @pl.when(n_win > 1)
        def _(e=e, n_win=n_win):
            rank_e = rank[:, e * MOE_WIN:(e + 1) * MOE_WIN]

            def extra(w, c):
                r = pl.multiple_of(starts[e] + w * MOE_WIN, 16)
                hit = rank_e == (lane_w + (r + 1)).astype(f32)
                o_ref[...] += _mm(jnp.where(hit, 1.0, 0.0).astype(bf16),
                                  y_ref[pl.ds(pl.multiple_of(e * rows + r, 16), MOE_WIN), :])
                return c

            lax.fori_loop(1, n_win, extra, 0)

    if residual:
        x_ref, mod_ref = refs[0], refs[1]
        o_ref[...] = x_ref[...] + mod_ref[5:6, :] * o_ref[...]


def _combine(y, cp, bnd, x=None, mod=None, n_tok=None):
    B, E, rows, D = y.shape
    residual = x is not None
    n_tok = n_tok if residual else cp.shape[2]
    expand = jnp.asarray(np.kron(np.eye(E), np.ones((1, MOE_WIN))), bf16)
    chunk = pl.BlockSpec((None, MOE_CHUNK, D), lambda b, k, s: (b, k, 0))
    in_specs = [
        pl.BlockSpec((None, E * rows, D), lambda b, k, s: (b, 0, 0), pipeline_mode=pl.Buffered(1)),
        pl.BlockSpec((None, MOE_CHUNK, E), lambda b, k, s: (b, k, 0)),
        pl.BlockSpec(expand.shape, lambda b, k, s: (0, 0)),
    ]
    args = [bnd, y.reshape(B, E * rows, D), cp.transpose(0, 2, 1), expand]
    if residual:
        in_specs += [chunk, pl.BlockSpec((None, None, 6, D), lambda b, k, s: (b, 0, 0, 0))]
        args += [x, mod]
    return pl.pallas_call(
        functools.partial(_combine_kernel, E, rows, residual),
        grid_spec=pltpu.PrefetchScalarGridSpec(
            num_scalar_prefetch=1,
            grid=(B, n_tok // MOE_CHUNK),
            in_specs=in_specs,
            out_specs=chunk,
        ),
        out_shape=jax.ShapeDtypeStruct((B, n_tok, D), f32),
        compiler_params=_params("arbitrary", "arbitrary"),
        name="moe_combine",
    )(*args)


def _rope_tables(S, L):
    t = jnp.arange(S)
    row, col = t // GRID_W, t % GRID_W
    f = 16
    inv = ROPE_BASE ** (-jnp.arange(f, dtype=f32) / f)
    ar = row.astype(f32)[:, None] * inv
    ac = col.astype(f32)[:, None] * inv
    cos = jnp.concatenate([jnp.cos(ar), jnp.cos(ar), jnp.cos(ac), jnp.cos(ac)], axis=1)
    sin = jnp.concatenate([-jnp.sin(ar), jnp.sin(ar), -jnp.sin(ac), jnp.sin(ac)], axis=1)
    cos = jnp.concatenate([jnp.tile(cos, (1, 2)), jnp.ones((L, LANES), f32)], axis=0)
    sin = jnp.concatenate([jnp.tile(sin, (1, 2)), jnp.zeros((L, LANES), f32)], axis=0)
    return cos, sin


def _pad_heads(w, n_heads, width, padded):
    lead = w.shape[:-1]
    w = w.reshape(lead + (n_heads, width))
    w = jnp.pad(w, [(0, 0)] * len(lead) + [(0, 0), (0, padded - width)])
    return w.reshape(lead + (n_heads * padded,))


def _layer_weights(l, D, w_in, norm2_g, na_q_g, na_k_g, mla_q_norm, mla_w_uq, mla_kv_norm, mla_w_ukv,
                   mla_q_g, mla_k_g, swa_q_g, swa_k_g, w_na_o, w_mla_o, w_swa_o, w_o, router):
    o = np.cumsum((0, NA_W, NA_W, NA_W, MLA_Q_RANK, MLA_KV_RANK, MLA_ROPE, SWA_W, SWA_KV_W, SWA_KV_W))
    o_naq, o_nak, o_nav, o_cq, o_ckv, o_kr, o_swq, o_swk, o_swv, o_gl = (int(v) for v in o)
    wi = w_in[l]
    swq = wi[:, o_swq:o_swq + SWA_W].reshape(D, SWA_HEADS, SWA_DIM)[:, SWA_HEAD_ORDER, :].reshape(D, SWA_W)
    kr = jnp.pad(wi[:, o_kr:o_kr + MLA_ROPE], ((0, 0), (0, LANES - MLA_ROPE)))
    w_all = jnp.concatenate([
        wi[:, o_naq:o_naq + 3 * NA_W], swq, wi[:, o_cq:o_cq + MLA_Q_RANK], wi[:, o_ckv:o_ckv + MLA_KV_RANK], kr,
        wi[:, o_swk:o_swk + 2 * SWA_KV_W], 0.5 * wi[:, o_gl:]], axis=1).astype(bf16)
    ukv = mla_w_ukv[l].reshape(MLA_KV_RANK, MLA_HEADS, MLA_NOPE + MLA_V)
    swa_o = w_swa_o[l].reshape(SWA_HEADS, SWA_DIM, D)[SWA_HEAD_ORDER, :, :].reshape(SWA_W, D)
    r_hi = router[l].astype(bf16)
    r_lo = (router[l] - r_hi.astype(f32)).astype(bf16)
    return dict(
        w_all=w_all,
        naq_g=jnp.tile(na_q_g[l], NA_HEADS)[None], nak_g=jnp.tile(na_k_g[l], NA_HEADS)[None],
        swq_g=jnp.tile(swa_q_g[l], SWA_HEADS)[None], swk_g=jnp.tile(swa_k_g[l], SWA_KV_HEADS)[None],
        qn=mla_q_norm[l][None], kvn=mla_kv_norm[l][None],
        wuq=_pad_heads(mla_w_uq[l], MLA_HEADS, MLA_QK, MLA_PAD).astype(bf16),
        wuk=ukv[:, :, :MLA_NOPE].reshape(MLA_KV_RANK, MLA_HEADS * MLA_NOPE).astype(bf16),
        wuv=ukv[:, :, MLA_NOPE:].reshape(MLA_KV_RANK, MLA_W).astype(bf16),
        qg=jnp.tile(jnp.pad(mla_q_g[l], (0, MLA_PAD - MLA_QK)), MLA_HEADS)[None],
        kg=jnp.tile(jnp.pad(mla_k_g[l], (0, MLA_PAD - MLA_QK)), MLA_HEADS)[None],
        wna=w_na_o[l].astype(bf16), wmla=w_mla_o[l].astype(bf16), wswa=swa_o.astype(bf16),
        wo=(0.5 * w_o[l]).astype(bf16), g2=norm2_g[l][None], router=jnp.concatenate([r_hi, r_lo], axis=1),
    )


def kernel(x, c, ctx, c_ctx, norm1_g, norm2_g, w_ada, b_ada, w_in, na_q_g, na_k_g, na_rpb, mla_q_norm, mla_w_uq, mla_kv_norm, mla_w_ukv, mla_q_g, mla_k_g, swa_q_g, swa_k_g, swa_sink, w_na_o, w_mla_o, w_swa_o, w_o, router, w_gate, w_up, w_down):
    B, S, D = x.shape
    L = ctx.shape[1]
    depth = w_in.shape[0]
    E = router.shape[2]
    T = S + L
    n_lat = S // TT
    rows = S // GRID_W
    assert S % TT == 0 and L % TM == 0 and L <= TT and rows >= NA_BAND_ROWS and S >= SWA_KEYS and B < 16
    cap_s = CAPACITY_FACTOR * S // E
    cap_l = CAPACITY_FACTOR * L // E

    cpad = jnp.zeros((16, D), f32).at[:B].set(c).at[B].set(c_ctx)
    mod = _ada(cpad, w_ada, b_ada).reshape(depth, 16, 6, D)
    mod = jnp.stack([mod[:, :B], jnp.broadcast_to(mod[:, B:B + 1], (depth, B, 6, D))], axis=2)

    cos, sin = _rope_tables(S, L)
    gmat = jnp.asarray(np.kron(np.eye(NA_HEADS), np.full((NA_DIM, NA_DIM), 1.0 / NA_DIM)), bf16)

    xs = jnp.concatenate([x, ctx], axis=1)
    moe = None
    for l in range(depth):
        lw = _layer_weights(l, D, w_in, norm2_g, na_q_g, na_k_g, mla_q_norm, mla_w_uq, mla_kv_norm, mla_w_ukv,
                            mla_q_g, mla_k_g, swa_q_g, swa_k_g, w_na_o, w_mla_o, w_swa_o, w_o, router)
        modp = mod[l - 1] if l > 0 else None
        xs, proj, gl = _in_proj(xs, moe, modp, mod[l], norm1_g[l][None], lw["w_all"], n_lat)
        qka, mq, mk, mv = _prep(proj, cos, sin, gmat, lw)
        ona = _na(qka, proj, _na_bias(na_rpb[l], rows), S)
        omla = _mla(mq, mk, mv, S)
        oswa = _swa(qka, proj, swa_sink[l] * LOG2E, S)
        xs, h2, aff, affx = _post(ona, omla, oswa, gl, xs, mod[l], lw, n_lat)
        cp, bnd = _topk(aff.transpose(0, 2, 1), S, cap_s, cap_l)
        bnd = bnd.astype(jnp.int32).reshape(-1)
        xg = _gather(h2, affx, cp, bnd, cap_s + cap_l)
        y = _ffn(xg, w_gate, w_up, w_down, l, cap_s + cap_l)
        if l == depth - 1:
            return _combine(y, cp, bnd, xs, mod[l], S)
        moe = _combine(y, cp, bnd)
```

```python
import functools

import numpy as np
import jax
import jax.numpy as jnp
from jax import lax
from jax.experimental import pallas as pl
from jax.experimental.pallas import tpu as pltpu

GRID_W = 64
NA_HEADS, NA_DIM, NA_WIN_ROWS, NA_WIN_COLS = 6, 64, 8, 16
MLA_HEADS, MLA_Q_RANK, MLA_KV_RANK, MLA_NOPE, MLA_ROPE, MLA_V = 4, 256, 128, 128, 64, 128
MLA_QK = MLA_NOPE + MLA_ROPE
MLA_PAD = 256
SWA_HEADS, SWA_KV_HEADS, SWA_DIM, SWA_WINDOW = 6, 2, 64, 128
CAPACITY_FACTOR = 2
N_BRANCH = 3
ROPE_BASE = 10000.0
EPS = 1e-6

NA_W = NA_HEADS * NA_DIM
SWA_W = SWA_HEADS * SWA_DIM
SWA_KV_W = SWA_KV_HEADS * SWA_DIM
MLA_W = MLA_HEADS * MLA_V

LANES = 128
TM = 256
TT = 512
MLA_TQ = 1024
MLA_TK = 1024
LOG2E = 1.4426950408889634
NA_TILE_ROWS = TM // GRID_W
NA_BAND_ROWS = NA_TILE_ROWS + NA_WIN_ROWS - 1
NA_BAND = NA_BAND_ROWS * GRID_W
SWA_KEYS = TM + 2 * SWA_WINDOW
MOE_CHUNK = 256
MOE_WIN = 64
NEG = -1e30
VMEM_LIMIT = 56 * 1024 * 1024

P_NAQ, P_NAK, P_NAV, P_SWQ = 0, 384, 768, 1152
P_CQ, P_CKV, P_KR, P_SWK, P_SWV = 1536, 1792, 1920, 2048, 2176
PROJ_W = 2304
N_CHUNK = 768
A_NAQ, A_NAK, A_SWQ, A_SWK = 0, 384, 768, 1152
QKA_W = 1280
SWA_HEAD_ORDER = (0, 3, 1, 4, 2, 5)

bf16 = jnp.bfloat16
f32 = jnp.float32


def _mm(a, b):
    return jnp.dot(a, b, preferred_element_type=f32)


def _nt(a, b):
    return lax.dot_general(a, b, (((1,), (1,)), ((), ())), preferred_element_type=f32)


def _params(*sem):
    return pltpu.CompilerParams(dimension_semantics=sem, vmem_limit_bytes=VMEM_LIMIT)


def _ada_kernel(c_ref, w_ref, b_ref, o_ref):
    a = c_ref[...]
    a = (a * jax.nn.sigmoid(a)).astype(bf16)
    o_ref[...] = _mm(a, w_ref[...].astype(bf16)) + b_ref[...]


def _ada(cpad, w_ada, b_ada):
    L, D, N = w_ada.shape
    tn = 1536
    return pl.pallas_call(
        _ada_kernel,
        grid=(L, N // tn),
        in_specs=[
            pl.BlockSpec((16, D), lambda l, j: (0, 0)),
            pl.BlockSpec((None, D, tn), lambda l, j: (l, 0, j)),
            pl.BlockSpec((None, 1, tn), lambda l, j: (l, 0, j)),
        ],
        out_specs=pl.BlockSpec((None, 16, tn), lambda l, j: (l, 0, j)),
        out_shape=jax.ShapeDtypeStruct((L, 16, N), f32),
        compiler_params=_params("arbitrary", "arbitrary"),
        name="ada",
    )(cpad, w_ada, b_ada.reshape(L, 1, N))


def _in_proj_kernel(has_moe, n_gl_chunks, *refs):
    if has_moe:
        x_ref, moe_ref, modp_ref, mod_ref, g_ref, w_ref, xo_ref, proj_ref, gl_ref = refs
    else:
        x_ref, mod_ref, g_ref, w_ref, proj_ref, gl_ref = refs
    n_sub = 2
    sub = x_ref.shape[0] // n_sub
    for r in range(n_sub):
        rows = slice(r * sub, (r + 1) * sub)
        x = x_ref[rows, :]
        if has_moe:
            x = x + modp_ref[5:6, :] * moe_ref[rows, :]
            xo_ref[rows, :] = x
        ms = jnp.mean(x * x, axis=-1, keepdims=True)
        h = x * lax.rsqrt(ms + EPS) * (g_ref[...] * (1.0 + mod_ref[1:2, :])) + mod_ref[0:1, :]
        h = h.astype(bf16)
        for c in range(PROJ_W // N_CHUNK):
            sl = slice(c * N_CHUNK, (c + 1) * N_CHUNK)
            proj_ref[rows, sl] = _mm(h, w_ref[:, sl]).astype(bf16)
        for c in range(n_gl_chunks):
            sl = slice(c * N_CHUNK, (c + 1) * N_CHUNK)
            gl_ref[rows, sl] = _mm(h, w_ref[:, PROJ_W + c * N_CHUNK:PROJ_W + (c + 1) * N_CHUNK]).astype(bf16)


def _mod_spec(n_lat, D):
    return pl.BlockSpec((None, None, 6, D), lambda b, i: (b, jnp.where(i >= n_lat, 1, 0), 0, 0))


def _in_proj(x, moe, modp, mod, g1, w, n_lat):
    B, T, D = x.shape
    nt = pl.cdiv(T, TT)
    has_moe = moe is not None
    n_gl_chunks = N_BRANCH * D // N_CHUNK
    tok = pl.BlockSpec((None, TT, D), lambda b, i: (b, i, 0))
    in_specs = [tok]
    args = [x]
    if has_moe:
        in_specs += [tok, _mod_spec(n_lat, D)]
        args += [moe, modp]
    in_specs += [
        _mod_spec(n_lat, D),
        pl.BlockSpec((1, D), lambda b, i: (0, 0)),
        pl.BlockSpec(w.shape, lambda b, i: (0, 0), pipeline_mode=pl.Buffered(1)),
    ]
    args += [mod, g1, w]
    out_specs = [
        pl.BlockSpec((None, TT, PROJ_W), lambda b, i: (b, i, 0)),
        pl.BlockSpec((None, TT, N_BRANCH * D), lambda b, i: (b, i, 0)),
    ]
    out_shape = [
        jax.ShapeDtypeStruct((B, T, PROJ_W), bf16),
        jax.ShapeDtypeStruct((B, T, N_BRANCH * D), bf16),
    ]
    if has_moe:
        out_specs = [tok] + out_specs
        out_shape = [jax.ShapeDtypeStruct((B, T, D), f32)] + out_shape
    res = pl.pallas_call(
        functools.partial(_in_proj_kernel, has_moe, n_gl_chunks),
        grid=(B, nt),
        in_specs=in_specs,
        out_specs=out_specs,
        out_shape=out_shape,
        compiler_params=_params("arbitrary", "arbitrary"),
        name="in_proj",
    )(*args)
    if has_moe:
        return res
    return [x] + list(res)


def _prep_kernel(proj_ref, cos_ref, sin_ref, perm_ref, gmat_ref, naq_g, nak_g, swq_g, swk_g,
                 qn_ref, wuq_ref, qg_ref, kvn_ref, wuk_ref, wuv_ref, kg_ref,
                 qka_ref, mq_ref, mk_ref, mv_ref):
    cos = cos_ref[...]
    sin = sin_ref[...]
    perm = perm_ref[...]

    def rope(x):
        hi = x.astype(bf16)
        lo = (x - hi.astype(f32)).astype(bf16)
        partner = _mm(hi, perm) + _mm(lo, perm)
        return x * cos + partner * sin

    def headnorm(x, gain):
        w = x.shape[1]
        ms = _mm((x * x).astype(bf16), gmat_ref[:w, :w])
        return x * lax.rsqrt(ms + EPS) * gain

    def rmsnorm(x, gain):
        ms = jnp.mean(x * x, axis=-1, keepdims=True)
        return x * lax.rsqrt(ms + EPS) * gain

    na_scale = NA_DIM ** -0.5 * LOG2E
    naq = headnorm(proj_ref[:, P_NAQ:P_NAQ + NA_W].astype(f32), naq_g[...])
    qka_ref[:, A_NAQ:A_NAQ + NA_W] = (naq * na_scale).astype(bf16)
    nak = headnorm(proj_ref[:, P_NAK:P_NAK + NA_W].astype(f32), nak_g[...])
    qka_ref[:, A_NAK:A_NAK + NA_W] = nak.astype(bf16)

    sw_scale = SWA_DIM ** -0.5 * LOG2E
    swq = headnorm(proj_ref[:, P_SWQ:P_SWQ + SWA_W].astype(f32), swq_g[...])
    for p in range(SWA_W // LANES):
        sl = slice(p * LANES, (p + 1) * LANES)
        qka_ref[:, A_SWQ + p * LANES:A_SWQ + (p + 1) * LANES] = (rope(swq[:, sl]) * sw_scale).astype(bf16)
    swk = headnorm(proj_ref[:, P_SWK:P_SWK + SWA_KV_W].astype(f32), swk_g[...])
    qka_ref[:, A_SWK:A_SWK + SWA_KV_W] = rope(swk).astype(bf16)

    mla_scale = MLA_QK ** -0.5 * LOG2E
    cq = rmsnorm(proj_ref[:, P_CQ:P_CQ + MLA_Q_RANK].astype(f32), qn_ref[...]).astype(bf16)
    q = _mm(cq, wuq_ref[...])
    ckv = rmsnorm(proj_ref[:, P_CKV:P_CKV + MLA_KV_RANK].astype(f32), kvn_ref[...]).astype(bf16)
    kn = _mm(ckv, wuk_ref[...])
    v = _mm(ckv, wuv_ref[...]).astype(bf16)
    ones_col = jnp.where(lax.broadcasted_iota(jnp.int32, (v.shape[0], LANES), 1) == 0, 1.0, 0.0).astype(bf16)
    for h in range(MLA_HEADS):
        mv_ref[:, h * MLA_PAD:h * MLA_PAD + MLA_V] = v[:, h * MLA_V:(h + 1) * MLA_V]
        mv_ref[:, h * MLA_PAD + MLA_V:(h + 1) * MLA_PAD] = ones_col
    kr = proj_ref[:, P_KR:P_KR + LANES].astype(f32)
    kr_ss = jnp.sum(kr * kr, axis=-1, keepdims=True)
    for h in range(MLA_HEADS):
        o = h * MLA_PAD
        qh = q[:, o:o + MLA_PAD]
        r = lax.rsqrt(jnp.sum(qh * qh, axis=-1, keepdims=True) * (1.0 / MLA_QK) + EPS)
        qh = qh * r * qg_ref[:, o:o + MLA_PAD]
        mq_ref[:, o:o + LANES] = (qh[:, :LANES] * mla_scale).astype(bf16)
        mq_ref[:, o + LANES:o + MLA_PAD] = (rope(qh[:, LANES:]) * mla_scale).astype(bf16)
        kh = kn[:, h * MLA_NOPE:(h + 1) * MLA_NOPE]
        r = lax.rsqrt((jnp.sum(kh * kh, axis=-1, keepdims=True) + kr_ss) * (1.0 / MLA_QK) + EPS)
        mk_ref[:, o:o + LANES] = (kh * r * kg_ref[:, o:o + LANES]).astype(bf16)
        mk_ref[:, o + LANES:o + MLA_PAD] = rope(kr * r * kg_ref[:, o + LANES:o + MLA_PAD]).astype(bf16)


def _prep(proj, cos, sin, gmat, lw):
    B, T, _ = proj.shape
    nt = pl.cdiv(T, TT)

    def const(a):
        return pl.BlockSpec(a.shape, lambda b, i: (0,) * a.ndim)

    lane = np.arange(LANES)
    perm = jnp.asarray(lane[:, None] == (lane[None, :] ^ 16), bf16)
    consts = [perm, gmat, lw["naq_g"], lw["nak_g"], lw["swq_g"], lw["swk_g"], lw["qn"], lw["wuq"], lw["qg"],
              lw["kvn"], lw["wuk"], lw["wuv"], lw["kg"]]
    widths = (QKA_W, MLA_HEADS * MLA_PAD, MLA_HEADS * MLA_PAD, MLA_HEADS * MLA_PAD)
    return pl.pallas_call(
        _prep_kernel,
        grid=(B, nt),
        in_specs=[
            pl.BlockSpec((None, TT, PROJ_W), lambda b, i: (b, i, 0)),
            pl.BlockSpec((TT, LANES), lambda b, i: (i, 0)),
            pl.BlockSpec((TT, LANES), lambda b, i: (i, 0)),
        ] + [const(a) for a in consts],
        out_specs=[pl.BlockSpec((None, TT, w), lambda b, i: (b, i, 0)) for w in widths],
        out_shape=[jax.ShapeDtypeStruct((B, T, w), bf16) for w in widths],
        compiler_params=_params("arbitrary", "arbitrary"),
        name="prep",
    )(proj, cos, sin, *consts)


def _half_masks():
    lane = lax.broadcasted_iota(jnp.int32, (1, LANES), 1)
    lo = jnp.where(lane < 64, 1.0, 0.0).astype(bf16)
    return lo, (1.0 - lo.astype(f32)).astype(bf16)


def _pair_attend(qp, parts, masks, sinks=None):
    n_q = qp.shape[0]
    q2 = jnp.concatenate([qp * masks[0], qp * masks[1]], axis=0)
    scores = []
    for k, _, bias in parts:
        s = _nt(q2, k)
        scores.append(s if bias is None else s + bias)
    m = functools.reduce(jnp.maximum, [jnp.max(s, axis=-1, keepdims=True) for s in scores])
    if sinks is not None:
        first = lax.broadcasted_iota(jnp.int32, (2 * n_q, 1), 0) < n_q
        sink = jnp.where(first, sinks[0], sinks[1])
        m = jnp.maximum(m, sink)
    ps = [jnp.exp2(s - m) for s in scores]
    l = functools.reduce(jnp.add, [jnp.sum(p, axis=-1, keepdims=True) for p in ps])
    if sinks is not None:
        l = l + jnp.exp2(sink - m)
    o = functools.reduce(jnp.add, [_mm(p.astype(bf16), v) for p, (_, v, _) in zip(ps, parts)]) / l
    lane = lax.broadcasted_iota(jnp.int32, (1, LANES), 1)
    return jnp.where(lane < 64, o[:n_q], o[n_q:])


def _na_kernel(n_lat, S, rows, offs, q_ref, k_ref, v_ref, tall_ref, rmask_ref, o_ref):
    i = pl.program_id(1)
    masks = _half_masks()
    T = k_ref.shape[0]

    def latent(ty):
        start = GRID_W * jnp.clip(NA_TILE_ROWS * i - NA_WIN_ROWS // 2, 0, rows - NA_BAND_ROWS)
        start = pl.multiple_of(start, GRID_W)
        for p in range(NA_W // LANES):
            sl = slice(p * LANES, (p + 1) * LANES)
            slabs = []
            for head in (2 * p, 2 * p + 1):
                for qr in range(NA_TILE_ROWS):
                    odd = int(offs[ty, qr]) % 2
                    lo = (int(offs[ty, qr]) - odd) * GRID_W
                    slabs.append(tall_ref[head, odd, :, lo:lo + NA_BAND] + rmask_ref[ty, qr:qr + 1, :])
            band = (k_ref[pl.ds(start, NA_BAND), sl], v_ref[pl.ds(start, NA_BAND), sl],
                    jnp.concatenate(slabs, axis=0))
            ctx = (k_ref[S:T, sl], v_ref[S:T, sl], None)
            o_ref[:, sl] = _pair_attend(q_ref[:, sl], [band, ctx], masks).astype(bf16)

    pl.when(i == 0)(functools.partial(latent, 0))
    pl.when(jnp.logical_and(i > 0, i < n_lat - 1))(functools.partial(latent, 1))
    pl.when(i == n_lat - 1)(functools.partial(latent, 2))

    @pl.when(i >= n_lat)
    def _():
        for p in range(NA_W // LANES):
            sl = slice(p * LANES, (p + 1) * LANES)
            ctx = (k_ref[S:T, sl], v_ref[S:T, sl], None)
            o_ref[:, sl] = _pair_attend(q_ref[:, sl], [ctx], masks).astype(bf16)


def _na(qka, proj, rpb, S):
    B, T, _ = qka.shape
    nt, n_lat, rows = T // TM, S // TM, S // GRID_W
    assert n_lat >= 3
    offs, front, back, row_mask = _na_geometry(rows)
    tall = _na_bias(rpb, front, back)
    row_mask = jnp.asarray(row_mask)
    return pl.pallas_call(
        functools.partial(_na_kernel, n_lat, S, rows, offs),
        grid=(B, nt),
        in_specs=[
            pl.BlockSpec((None, TM, NA_W), lambda b, i: (b, i, A_NAQ // NA_W)),
            pl.BlockSpec((None, T, NA_W), lambda b, i: (b, 0, A_NAK // NA_W)),
            pl.BlockSpec((None, T, NA_W), lambda b, i: (b, 0, P_NAV // NA_W)),
            pl.BlockSpec(tall.shape, lambda b, i: (0, 0, 0, 0)),
            pl.BlockSpec(row_mask.shape, lambda b, i: (0, 0, 0)),
        ],
        out_specs=pl.BlockSpec((None, TM, NA_W), lambda b, i: (b, i, 0)),
        out_shape=jax.ShapeDtypeStruct((B, T, NA_W), bf16),
        compiler_params=_params("arbitrary", "arbitrary"),
        name="na_attn",
    )(qka, qka, proj, tall, row_mask)


def _na_geometry(rows):
    n_tiles = rows // NA_TILE_ROWS
    band = np.arange(NA_BAND_ROWS)
    dr0 = np.zeros((3, NA_TILE_ROWS), np.int64)
    valid = np.zeros((3, NA_TILE_ROWS, NA_BAND_ROWS), bool)
    for ty, rt in enumerate((0, 1, n_tiles - 1)):
        bs = int(np.clip(NA_TILE_ROWS * rt - NA_WIN_ROWS // 2, 0, rows - NA_BAND_ROWS))
        for qr in range(NA_TILE_ROWS):
            r = NA_TILE_ROWS * rt + qr
            s_r = int(np.clip(r - NA_WIN_ROWS // 2, 0, rows - NA_WIN_ROWS))
            dr0[ty, qr] = bs - r + NA_WIN_ROWS - 1
            valid[ty, qr] = (bs + band >= s_r) & (bs + band < s_r + NA_WIN_ROWS)
    front = int(max(0, -dr0.min()))
    back = int(max(0, dr0.max() + NA_BAND_ROWS - (2 * NA_WIN_ROWS - 1)))
    row_mask = np.where(np.repeat(valid, GRID_W, axis=2), 0.0, NEG).astype(np.float32)
    return dr0 + front, front, back, row_mask


def _na_bias(rpb, front, back):
    col = np.arange(GRID_W)
    c0 = np.clip(col - NA_WIN_COLS // 2, 0, GRID_W - NA_WIN_COLS)
    col_ok = (col[None, :] >= c0[:, None]) & (col[None, :] < c0[:, None] + NA_WIN_COLS)
    dc = np.clip(col[None, :] - col[:, None] + NA_WIN_COLS - 1, 0, 2 * NA_WIN_COLS - 2)
    tz = jnp.where(col_ok, rpb[:, :, dc] * LOG2E, NEG)
    n_blocks = front + tz.shape[1] + back
    width = pl.cdiv(n_blocks * GRID_W, LANES) * LANES
    tall = tz.transpose(0, 2, 1, 3).reshape(NA_HEADS, GRID_W, -1)
    tall = jnp.pad(tall, ((0, 0), (0, 0), (front * GRID_W, width + GRID_W - (front + tz.shape[1]) * GRID_W)),
                   constant_values=NEG)
    return jnp.stack([tall[:, :, :width], tall[:, :, GRID_W:GRID_W + width]], axis=1)


def _mla_kernel(n_q, S, q_ref, k_ref, v_ref, o_ref):
    i = pl.program_id(2)
    T = k_ref.shape[0]

    def attend(q, bounds):
        m_run = acc = None
        for lo, hi in bounds:
            s = _nt(q, k_ref[lo:hi, :])
            m_new = jnp.max(s, axis=-1, keepdims=True)
            if m_run is not None:
                m_new = jnp.maximum(m_run, m_new)
            pv = _mm(jnp.exp2(s - m_new).astype(bf16), v_ref[lo:hi, :])
            acc = pv if acc is None else acc * jnp.exp2(m_run - m_new) + pv
            m_run = m_new
        return (acc[:, :MLA_V] / acc[:, MLA_V:MLA_V + 1]).astype(bf16)

    @pl.when(i < n_q)
    def _():
        chunks = [(lo, lo + MLA_TK) for lo in range(0, S, MLA_TK)] + [(S, T)]
        o_ref[...] = attend(q_ref[...], chunks)

    @pl.when(i >= n_q)
    def _():
        o_ref[0:T - S, :] = attend(q_ref[0:T - S, :], [(S, T)])


def _mla(mq, mk, mv, S):
    B, T, _ = mq.shape
    tq = min(MLA_TQ, S)
    n_q = S // tq
    assert S % tq == 0 and S % MLA_TK == 0 and T - S <= tq
    return pl.pallas_call(
        functools.partial(_mla_kernel, n_q, S),
        grid=(B, MLA_HEADS, n_q + 1),
        in_specs=[
            pl.BlockSpec((None, tq, MLA_PAD), lambda b, h, i: (b, i, h)),
            pl.BlockSpec((None, T, MLA_PAD), lambda b, h, i: (b, 0, h)),
            pl.BlockSpec((None, T, MLA_PAD), lambda b, h, i: (b, 0, h)),
        ],
        out_specs=pl.BlockSpec((None, tq, MLA_V), lambda b, h, i: (b, i, h)),
        out_shape=jax.ShapeDtypeStruct((B, T, MLA_W), bf16),
        compiler_params=_params("arbitrary", "arbitrary", "arbitrary"),
        name="mla_attn",
    )(mq, mk, mv)


def _swa_kernel(n_lat, S, sink_ref, q_ref, k_ref, v_ref, o_ref):
    i = pl.program_id(1)
    masks = _half_masks()
    T = k_ref.shape[0]
    kc, vc = k_ref[S:T, :], v_ref[S:T, :]

    def sinks(p):
        return (sink_ref[SWA_HEAD_ORDER[2 * p]], sink_ref[SWA_HEAD_ORDER[2 * p + 1]])

    @pl.when(i < n_lat)
    def _():
        start = pl.multiple_of(jnp.clip(i * TM - SWA_WINDOW, 0, S - SWA_KEYS), SWA_WINDOW)
        row = lax.broadcasted_iota(jnp.int32, (2 * TM, SWA_KEYS), 0)
        qpos = i * TM + jnp.where(row >= TM, row - TM, row)
        kpos = start + lax.broadcasted_iota(jnp.int32, (2 * TM, SWA_KEYS), 1)
        band = jnp.where(jnp.abs(qpos - kpos) <= SWA_WINDOW, 0.0, NEG)
        kb, vb = k_ref[pl.ds(start, SWA_KEYS), :], v_ref[pl.ds(start, SWA_KEYS), :]
        for p in range(SWA_W // LANES):
            sl = slice(p * LANES, (p + 1) * LANES)
            parts = [(kb, vb, band), (kc, vc, None)]
            o_ref[:, sl] = _pair_attend(q_ref[:, sl], parts, masks, sinks(p)).astype(bf16)

    @pl.when(i >= n_lat)
    def _():
        for p in range(SWA_W // LANES):
            sl = slice(p * LANES, (p + 1) * LANES)
            o_ref[:, sl] = _pair_attend(q_ref[:, sl], [(kc, vc, None)], masks, sinks(p)).astype(bf16)


def _swa(qka, proj, sink, S):
    B, T, _ = qka.shape
    nt, n_lat = T // TM, S // TM
    return pl.pallas_call(
        functools.partial(_swa_kernel, n_lat, S),
        grid=(B, nt),
        in_specs=[
            pl.BlockSpec(memory_space=pltpu.SMEM),
            pl.BlockSpec((None, TM, SWA_W), lambda b, i: (b, i, A_SWQ // SWA_W)),
            pl.BlockSpec((None, T, SWA_KV_W), lambda b, i: (b, 0, A_SWK // SWA_KV_W)),
            pl.BlockSpec((None, T, SWA_KV_W), lambda b, i: (b, 0, P_SWV // SWA_KV_W)),
        ],
        out_specs=pl.BlockSpec((None, TM, SWA_W), lambda b, i: (b, i, 0)),
        out_shape=jax.ShapeDtypeStruct((B, T, SWA_W), bf16),
        compiler_params=_params("arbitrary", "arbitrary"),
        name="swa_attn",
    )(sink, qka, qka, proj)


def _post_kernel(ona_ref, omla_ref, oswa_ref, gl_ref, x_ref, mod_ref, wna_ref, wmla_ref, wswa_ref, wo_ref,
                 g2_ref, rt_ref, xo_ref, h2_ref, aff_ref, affx_ref):
    D = x_ref.shape[1]
    n_e = aff_ref.shape[1]
    n_sub = 2
    sub = x_ref.shape[0] // n_sub
    for r in range(n_sub):
        rows = slice(r * sub, (r + 1) * sub)
        merged = None
        for j, (o_ref, w_ref) in enumerate(((ona_ref, wna_ref), (omla_ref, wmla_ref), (oswa_ref, wswa_ref))):
            y = _mm(o_ref[rows, :], w_ref[...])
            gy = (jnp.tanh(gl_ref[rows, j * D:(j + 1) * D].astype(f32)) + 1.0) * y
            merged = gy if merged is None else merged + gy
        res = _mm(merged.astype(bf16), wo_ref[...])
        x = x_ref[rows, :] + mod_ref[2:3, :] * res
        xo_ref[rows, :] = x
        ms = jnp.mean(x * x, axis=-1, keepdims=True)
        h2 = x * lax.rsqrt(ms + EPS) * (g2_ref[...] * (1.0 + mod_ref[4:5, :])) + mod_ref[3:4, :]
        h_hi = h2.astype(bf16)
        h2_ref[rows, :] = h_hi
        h_lo = (h2 - h_hi.astype(f32)).astype(bf16)
        parts = _mm(h_hi, rt_ref[...]) + _mm(h_lo, rt_ref[...])
        logits = parts[:, :n_e] + parts[:, n_e:]
        e = jnp.exp(logits - jnp.max(logits, axis=1, keepdims=True))
        aff = e / jnp.sum(e, axis=1, keepdims=True)
        aff_ref[rows, :] = aff
        p0 = aff.astype(bf16)
        r1 = aff - p0.astype(f32)
        p1 = r1.astype(bf16)
        p2 = (r1 - p1.astype(f32)).astype(bf16)
        pad = jnp.zeros((sub, LANES - 3 * n_e), bf16)
        affx_ref[rows, :] = jnp.concatenate([p0, p1, p2, pad], axis=1)


def _post(ona, omla, oswa, gl, x, mod, lw, n_lat):
    B, T, D = x.shape
    nt = pl.cdiv(T, TT)
    E = lw["router"].shape[1] // 2

    def tok(w):
        return pl.BlockSpec((None, TT, w), lambda b, i: (b, i, 0))

    def const(a):
        return pl.BlockSpec(a.shape, lambda b, i: (0,) * a.ndim)

    consts = [lw["wna"], lw["wmla"], lw["wswa"], lw["wo"], lw["g2"], lw["router"]]
    return pl.pallas_call(
        _post_kernel,
        grid=(B, nt),
        in_specs=[tok(NA_W), tok(MLA_W), tok(SWA_W), tok(N_BRANCH * D), tok(D), _mod_spec(n_lat, D)]
        + [const(a) for a in consts],
        out_specs=[tok(D), tok(D), tok(E), tok(LANES)],
        out_shape=[jax.ShapeDtypeStruct((B, T, D), f32), jax.ShapeDtypeStruct((B, T, D), bf16),
                   jax.ShapeDtypeStruct((B, T, E), f32), jax.ShapeDtypeStruct((B, T, LANES), bf16)],
        compiler_params=_params("arbitrary", "arbitrary"),
        name="post_attn",
    )(ona, omla, oswa, gl, x, mod, *consts)


def _lane_cumsum(mask, tri):
    E, n = mask.shape
    carry = jnp.zeros((E, 1), f32)
    outs = []
    for k in range(n // LANES):
        w = _mm(mask[:, k * LANES:(k + 1) * LANES].astype(bf16), tri) + carry
        outs.append(w)
        carry = w[:, LANES - 1:LANES]
    return jnp.concatenate(outs, axis=1)


def _select_slots(aff, cap, base, tri):
    bits = lax.bitcast_convert_type(aff, jnp.int32)
    thr = jnp.zeros((aff.shape[0], 1), jnp.int32)
    for bit in range(30, -1, -1):
        cand = thr | (1 << bit)
        cnt = jnp.sum(jnp.where(bits >= cand, 1.0, 0.0), axis=1, keepdims=True)
        thr = jnp.where(cnt >= cap, cand, thr)
    gt = jnp.where(bits > thr, 1.0, 0.0)
    eq = jnp.where(bits == thr, 1.0, 0.0)
    need = cap - jnp.sum(gt, axis=1, keepdims=True)
    sel = jnp.maximum(gt, jnp.where(_lane_cumsum(eq, tri) <= need, eq, 0.0))
    return jnp.where(sel > 0.0, _lane_cumsum(sel, tri) + base, 0.0)


def _topk_kernel(S, cap_s, cap_l, aff_ref, tri_ref, before_ref, cp_ref, bnd_ref):
    T = aff_ref.shape[1]
    tri = tri_ref[...]
    cp_ref[:, 0:S] = _select_slots(aff_ref[:, 0:S], cap_s, 0.0, tri)
    cp_ref[:, S:T] = _select_slots(aff_ref[:, S:T], cap_l, float(cap_s), tri)
    bnd_ref[...] = _mm(jnp.where(cp_ref[...] > 0.0, 1.0, 0.0).astype(bf16), before_ref[...])


def _topk(aff, S, cap_s, cap_l):
    B, E, T = aff.shape
    tri = jnp.asarray(np.triu(np.ones((LANES, LANES), np.float32)), bf16)
    before = jnp.asarray(np.arange(T)[:, None] < MOE_CHUNK * np.arange(LANES)[None, :], bf16)
    return pl.pallas_call(
        functools.partial(_topk_kernel, S, cap_s, cap_l),
        grid=(B,),
        in_specs=[pl.BlockSpec((None, E, T), lambda b: (b, 0, 0)),
                  pl.BlockSpec((LANES, LANES), lambda b: (0, 0)),
                  pl.BlockSpec((T, LANES), lambda b: (0, 0))],
        out_specs=[pl.BlockSpec((None, E, T), lambda b: (b, 0, 0)),
                   pl.BlockSpec((None, E, LANES), lambda b: (b, 0, 0))],
        out_shape=[jax.ShapeDtypeStruct((B, E, T), f32), jax.ShapeDtypeStruct((B, E, LANES), f32)],
        compiler_params=_params("arbitrary"),
        name="expert_select",
    )(aff, tri, before)


def _window_count(lo, hi, r0):
    return jnp.where(hi > lo, lax.div(hi - r0 + (MOE_WIN - 1), MOE_WIN), 0)


def _gather_kernel(n_e, rows, bnd_ref, h2_ref, affx_ref, cp_ref, xg_ref):
    b, k = pl.program_id(0), pl.program_id(1)
    D = h2_ref.shape[1]

    @pl.when(k == 0)
    def _():
        xg_ref[...] = jnp.zeros_like(xg_ref)

    ex = lax.broadcasted_iota(jnp.int32, (n_e, 1), 0)
    starts = []
    start_col = jnp.zeros((n_e, 1), f32)
    for e in range(n_e):
        r0 = lax.div(bnd_ref[(b * n_e + e) * LANES + k], 16) * 16
        starts.append(r0)
        start_col = start_col + jnp.where(ex == e, r0.astype(f32), 0.0)
    rel = cp_ref[...] - start_col
    rel_rows = jnp.broadcast_to(rel[:, None, :], (n_e, MOE_WIN, MOE_CHUNK)).reshape(n_e * MOE_WIN, MOE_CHUNK)
    j1 = lax.broadcasted_iota(jnp.int32, (n_e, MOE_WIN, 1), 1).reshape(n_e * MOE_WIN, 1) + 1
    onehot = jnp.where(rel_rows == j1.astype(f32), 1.0, 0.0).astype(bf16)
    tok, tok_aff = h2_ref[...], affx_ref[...]
    moved = _mm(onehot, tok).astype(bf16)
    moved_aff = _mm(onehot, tok_aff).astype(bf16)
    for e in range(n_e):
        dst = pl.ds(pl.multiple_of(e * rows + starts[e], 16), MOE_WIN)
        xg_ref[dst, 0:D] += moved[e * MOE_WIN:(e + 1) * MOE_WIN]
        xg_ref[dst, D:D + LANES] += moved_aff[e * MOE_WIN:(e + 1) * MOE_WIN]

    sub1 = lax.broadcasted_iota(jnp.int32, (MOE_WIN, 1), 0) + 1
    for e in range(n_e):
        lo, hi = bnd_ref[(b * n_e + e) * LANES + k], bnd_ref[(b * n_e + e) * LANES + k + 1]
        n_win = _window_count(lo, hi, starts[e])

        @pl.when(n_win > 1)
        def _(e=e, n_win=n_win):
            def extra(w, c):
                hit = rel[e:e + 1, :] == (sub1 + w * MOE_WIN).astype(f32)
                oh = jnp.where(hit, 1.0, 0.0).astype(bf16)
                dst = pl.ds(pl.multiple_of(e * rows + starts[e] + w * MOE_WIN, 16), MOE_WIN)
                xg_ref[dst, 0:D] += _mm(oh, tok).astype(bf16)
                xg_ref[dst, D:D + LANES] += _mm(oh, tok_aff).astype(bf16)
                return c

            lax.fori_loop(1, n_win, extra, 0)


def _gather(h2, affx, cp, bnd, n_slots):
    B, T, D = h2.shape
    E = cp.shape[1]
    nc = T // MOE_CHUNK
    rows = n_slots + MOE_WIN
    assert n_slots % 16 == 0 and nc + 1 <= LANES and 3 * E <= LANES
    return pl.pallas_call(
        functools.partial(_gather_kernel, E, rows),
        grid_spec=pltpu.PrefetchScalarGridSpec(
            num_scalar_prefetch=1,
            grid=(B, nc),
            in_specs=[
                pl.BlockSpec((None, MOE_CHUNK, D), lambda b, k, s: (b, k, 0)),
                pl.BlockSpec((None, MOE_CHUNK, LANES), lambda b, k, s: (b, k, 0)),
                pl.BlockSpec((None, E, MOE_CHUNK), lambda b, k, s: (b, 0, k)),
            ],
            out_specs=pl.BlockSpec((None, E * rows, D + LANES), lambda b, k, s: (b, 0, 0),
                                   pipeline_mode=pl.Buffered(1)),
        ),
        out_shape=jax.ShapeDtypeStruct((B, E * rows, D + LANES), bf16),
        compiler_params=_params("arbitrary", "arbitrary"),
        name="moe_gather",
    )(bnd, h2, affx, cp)


def _ffn_kernel(n_slots, n_e, xg_ref, wg_ref, wu_ref, wd_ref, y_ref, wg_s, wu_s, wd_s):
    e, b = pl.program_id(0), pl.program_id(1)
    D = wg_ref.shape[0]

    @pl.when(b == 0)
    def _():
        wg_s[...] = wg_ref[...].astype(bf16)
        wu_s[...] = wu_ref[...].astype(bf16)
        wd_s[...] = wd_ref[...].astype(bf16)

    n_b = xg_ref.shape[0]
    xg = jnp.concatenate([xg_ref[s, 0:n_slots, 0:D] for s in range(n_b)], axis=0)
    a = _mm(xg, wg_s[...])
    u = _mm(xg, wu_s[...])
    act = (a * jax.nn.sigmoid(a) * u).astype(bf16)
    lane = lax.broadcasted_iota(jnp.int32, (1, LANES), 1)
    mine = jnp.logical_and(lax.rem(lane, n_e) == e, lane < 3 * n_e)
    parts = jnp.concatenate([xg_ref[s, 0:n_slots, D:D + LANES] for s in range(n_b)], axis=0).astype(f32)
    gate = jnp.sum(jnp.where(mine, parts, 0.0), axis=1, keepdims=True)
    y = (_mm(act, wd_s[...]) * gate).astype(bf16)
    for s in range(n_b):
        y_ref[s, 0:n_slots, :] = y[s * n_slots:(s + 1) * n_slots]
        y_ref[s, n_slots:, :] = jnp.zeros((y_ref.shape[1] - n_slots, y_ref.shape[2]), bf16)


def _ffn(xg, w_gate, w_up, w_down, layer, n_slots):
    B = xg.shape[0]
    _, E, D, F = w_gate.shape
    rows = xg.shape[1] // E
    n_b = 2 if B % 2 == 0 else 1
    return pl.pallas_call(
        functools.partial(_ffn_kernel, n_slots, E),
        grid=(E, B // n_b),
        in_specs=[
            pl.BlockSpec((n_b, None, rows, D + LANES), lambda e, b: (b, e, 0, 0)),
            pl.BlockSpec((None, None, D, F), lambda e, b: (layer, e, 0, 0)),
            pl.BlockSpec((None, None, D, F), lambda e, b: (layer, e, 0, 0)),
            pl.BlockSpec((None, None, F, D), lambda e, b: (layer, e, 0, 0)),
        ],
        out_specs=pl.BlockSpec((n_b, None, rows, D), lambda e, b: (b, e, 0, 0)),
        out_shape=jax.ShapeDtypeStruct((B, E, rows, D), bf16),
        scratch_shapes=[pltpu.VMEM((D, F), bf16), pltpu.VMEM((D, F), bf16), pltpu.VMEM((F, D), bf16)],
        compiler_params=_params("arbitrary", "arbitrary"),
        name="moe_ffn",
    )(xg.reshape(B, E, rows, D + LANES), w_gate, w_up, w_down)


def _combine_kernel(n_e, rows, residual, bnd_ref, y_ref, cpt_ref, expand_ref, *refs):
    o_ref = refs[-1]
    b, k = pl.program_id(0), pl.program_id(1)
    cpt = cpt_ref[...]
    hi_part = jnp.floor(cpt * (1.0 / 32.0))
    lo_part = cpt - 32.0 * hi_part
    expand = expand_ref[...]
    rank = 32.0 * _mm(hi_part.astype(bf16), expand) + _mm(lo_part.astype(bf16), expand)
    lane = lax.broadcasted_iota(jnp.int32, (1, n_e * MOE_WIN), 1)
    lane_e = lax.div(lane, MOE_WIN)
    starts = []
    tgt = (lane - lane_e * MOE_WIN + 1).astype(f32)
    for e in range(n_e):
        r0 = lax.div(bnd_ref[(b * n_e + e) * LANES + k], 16) * 16
        starts.append(r0)
        tgt = tgt + jnp.where(lane_e == e, r0.astype(f32), 0.0)
    onehot = jnp.where(rank == tgt, 1.0, 0.0).astype(bf16)
    ycat = jnp.concatenate(
        [y_ref[pl.ds(pl.multiple_of(e * rows + starts[e], 16), MOE_WIN), :] for e in range(n_e)], axis=0)
    o_ref[...] = _mm(onehot, ycat)

    lane_w = lax.broadcasted_iota(jnp.int32, (1, MOE_WIN), 1)
    for e in range(n_e):
        lo, hi = bnd_ref[(b * n_e + e) * LANES + k], bnd_ref[(b * n_e + e) * LANES + k + 1]
        n_win = _window_count(lo, hi, starts[e])

        @pl.when(n_win > 1)
        def _(e=e, n_win=n_win):
            rank_e = rank[:, e * MOE_WIN:(e + 1) * MOE_WIN]

            def extra(w, c):
                r = pl.multiple_of(starts[e] + w * MOE_WIN, 16)
                hit = rank_e == (lane_w + (r + 1)).astype(f32)
                o_ref[...] += _mm(jnp.where(hit, 1.0, 0.0).astype(bf16),
                                  y_ref[pl.ds(pl.multiple_of(e * rows + r, 16), MOE_WIN), :])
                return c

            lax.fori_loop(1, n_win, extra, 0)

    if residual:
        x_ref, mod_ref = refs[0], refs[1]
        o_ref[...] = x_ref[...] + mod_ref[5:6, :] * o_ref[...]


def _combine(y, cp, bnd, x=None, mod=None, n_tok=None):
    B, E, rows, D = y.shape
    residual = x is not None
    n_tok = n_tok if residual else cp.shape[2]
    expand = jnp.asarray(np.kron(np.eye(E), np.ones((1, MOE_WIN))), bf16)
    chunk = pl.BlockSpec((None, MOE_CHUNK, D), lambda b, k, s: (b, k, 0))
    in_specs = [
        pl.BlockSpec((None, E * rows, D), lambda b, k, s: (b, 0, 0), pipeline_mode=pl.Buffered(1)),
        pl.BlockSpec((None, MOE_CHUNK, E), lambda b, k, s: (b, k, 0)),
        pl.BlockSpec(expand.shape, lambda b, k, s: (0, 0)),
    ]
    args = [bnd, y.reshape(B, E * rows, D), cp.transpose(0, 2, 1), expand]
    if residual:
        in_specs += [chunk, pl.BlockSpec((None, None, 6, D), lambda b, k, s: (b, 0, 0, 0))]
        args += [x, mod]
    return pl.pallas_call(
        functools.partial(_combine_kernel, E, rows, residual),
        grid_spec=pltpu.PrefetchScalarGridSpec(
            num_scalar_prefetch=1,
            grid=(B, n_tok // MOE_CHUNK),
            in_specs=in_specs,
            out_specs=chunk,
        ),
        out_shape=jax.ShapeDtypeStruct((B, n_tok, D), f32),
        compiler_params=_params("arbitrary", "arbitrary"),
        name="moe_combine",
    )(*args)


def _rope_tables(S, L):
    t = jnp.arange(S)
    row, col = t // GRID_W, t % GRID_W
    f = 16
    inv = ROPE_BASE ** (-jnp.arange(f, dtype=f32) / f)
    ar = row.astype(f32)[:, None] * inv
    ac = col.astype(f32)[:, None] * inv
    cos = jnp.concatenate([jnp.cos(ar), jnp.cos(ar), jnp.cos(ac), jnp.cos(ac)], axis=1)
    sin = jnp.concatenate([-jnp.sin(ar), jnp.sin(ar), -jnp.sin(ac), jnp.sin(ac)], axis=1)
    cos = jnp.concatenate([jnp.tile(cos, (1, 2)), jnp.ones((L, LANES), f32)], axis=0)
    sin = jnp.concatenate([jnp.tile(sin, (1, 2)), jnp.zeros((L, LANES), f32)], axis=0)
    return cos, sin


def _pad_heads(w, n_heads, width, padded):
    lead = w.shape[:-1]
    w = w.reshape(lead + (n_heads, width))
    w = jnp.pad(w, [(0, 0)] * len(lead) + [(0, 0), (0, padded - width)])
    return w.reshape(lead + (n_heads * padded,))


def _layer_weights(l, D, w_in, norm2_g, na_q_g, na_k_g, mla_q_norm, mla_w_uq, mla_kv_norm, mla_w_ukv,
                   mla_q_g, mla_k_g, swa_q_g, swa_k_g, w_na_o, w_mla_o, w_swa_o, w_o, router):
    o = np.cumsum((0, NA_W, NA_W, NA_W, MLA_Q_RANK, MLA_KV_RANK, MLA_ROPE, SWA_W, SWA_KV_W, SWA_KV_W))
    o_naq, o_nak, o_nav, o_cq, o_ckv, o_kr, o_swq, o_swk, o_swv, o_gl = (int(v) for v in o)
    wi = w_in[l]
    swq = wi[:, o_swq:o_swq + SWA_W].reshape(D, SWA_HEADS, SWA_DIM)[:, SWA_HEAD_ORDER, :].reshape(D, SWA_W)
    kr = jnp.pad(wi[:, o_kr:o_kr + MLA_ROPE], ((0, 0), (0, LANES - MLA_ROPE)))
    w_all = jnp.concatenate([
        wi[:, o_naq:o_naq + 3 * NA_W], swq, wi[:, o_cq:o_cq + MLA_Q_RANK], wi[:, o_ckv:o_ckv + MLA_KV_RANK], kr,
        wi[:, o_swk:o_swk + 2 * SWA_KV_W], 0.5 * wi[:, o_gl:]], axis=1).astype(bf16)
    ukv = mla_w_ukv[l].reshape(MLA_KV_RANK, MLA_HEADS, MLA_NOPE + MLA_V)
    swa_o = w_swa_o[l].reshape(SWA_HEADS, SWA_DIM, D)[SWA_HEAD_ORDER, :, :].reshape(SWA_W, D)
    r_hi = router[l].astype(bf16)
    r_lo = (router[l] - r_hi.astype(f32)).astype(bf16)
    return dict(
        w_all=w_all,
        naq_g=jnp.tile(na_q_g[l], NA_HEADS)[None], nak_g=jnp.tile(na_k_g[l], NA_HEADS)[None],
        swq_g=jnp.tile(swa_q_g[l], SWA_HEADS)[None], swk_g=jnp.tile(swa_k_g[l], SWA_KV_HEADS)[None],
        qn=mla_q_norm[l][None], kvn=mla_kv_norm[l][None],
        wuq=_pad_heads(mla_w_uq[l], MLA_HEADS, MLA_QK, MLA_PAD).astype(bf16),
        wuk=ukv[:, :, :MLA_NOPE].reshape(MLA_KV_RANK, MLA_HEADS * MLA_NOPE).astype(bf16),
        wuv=ukv[:, :, MLA_NOPE:].reshape(MLA_KV_RANK, MLA_W).astype(bf16),
        qg=jnp.tile(jnp.pad(mla_q_g[l], (0, MLA_PAD - MLA_QK)), MLA_HEADS)[None],
        kg=jnp.tile(jnp.pad(mla_k_g[l], (0, MLA_PAD - MLA_QK)), MLA_HEADS)[None],
        wna=w_na_o[l].astype(bf16), wmla=w_mla_o[l].astype(bf16), wswa=swa_o.astype(bf16),
        wo=(0.5 * w_o[l]).astype(bf16), g2=norm2_g[l][None], router=jnp.concatenate([r_hi, r_lo], axis=1),
    )


def kernel(x, c, ctx, c_ctx, norm1_g, norm2_g, w_ada, b_ada, w_in, na_q_g, na_k_g, na_rpb, mla_q_norm, mla_w_uq, mla_kv_norm, mla_w_ukv, mla_q_g, mla_k_g, swa_q_g, swa_k_g, swa_sink, w_na_o, w_mla_o, w_swa_o, w_o, router, w_gate, w_up, w_down):
    B, S, D = x.shape
    L = ctx.shape[1]
    depth = w_in.shape[0]
    E = router.shape[2]
    T = S + L
    n_lat = S // TT
    rows = S // GRID_W
    assert S % TT == 0 and L % TM == 0 and L <= TT and rows >= NA_BAND_ROWS and S >= SWA_KEYS and B < 16
    cap_s = CAPACITY_FACTOR * S // E
    cap_l = CAPACITY_FACTOR * L // E

    cpad = jnp.zeros((16, D), f32).at[:B].set(c).at[B].set(c_ctx)
    mod = _ada(cpad, w_ada, b_ada).reshape(depth, 16, 6, D)
    mod = jnp.stack([mod[:, :B], jnp.broadcast_to(mod[:, B:B + 1], (depth, B, 6, D))], axis=2)

    cos, sin = _rope_tables(S, L)
    gmat = jnp.asarray(np.kron(np.eye(NA_HEADS), np.full((NA_DIM, NA_DIM), 1.0 / NA_DIM)), bf16)

    xs = jnp.concatenate([x, ctx], axis=1)
    moe = None
    for l in range(depth):
        lw = _layer_weights(l, D, w_in, norm2_g, na_q_g, na_k_g, mla_q_norm, mla_w_uq, mla_kv_norm, mla_w_ukv,
                            mla_q_g, mla_k_g, swa_q_g, swa_k_g, w_na_o, w_mla_o, w_swa_o, w_o, router)
        modp = mod[l - 1] if l > 0 else None
        xs, proj, gl = _in_proj(xs, moe, modp, mod[l], norm1_g[l][None], lw["w_all"], n_lat)
        qka, mq, mk, mv = _prep(proj, cos, sin, gmat, lw)
        ona = _na(qka, proj, na_rpb[l], S)
        omla = _mla(mq, mk, mv, S)
        oswa = _swa(qka, proj, swa_sink[l] * LOG2E, S)
        xs, h2, aff, affx = _post(ona, omla, oswa, gl, xs, mod[l], lw, n_lat)
        cp, bnd = _topk(aff.transpose(0, 2, 1), S, cap_s, cap_l)
        bnd = bnd.astype(jnp.int32).reshape(-1)
        xg = _gather(h2, affx, cp, bnd, cap_s + cap_l)
        y = _ffn(xg, w_gate, w_up, w_down, l, cap_s + cap_l)
        if l == depth - 1:
            return _combine(y, cp, bnd, xs, mod[l], S)
        moe = _combine(y, cp, bnd)
```

```python
import functools

import numpy as np
import jax
import jax.numpy as jnp
from jax import lax
from jax.experimental import pallas as pl
from jax.experimental.pallas import tpu as pltpu

GRID_W = 64
NA_HEADS, NA_DIM, NA_WIN_ROWS, NA_WIN_COLS = 6, 64, 8, 16
MLA_HEADS, MLA_Q_RANK, MLA_KV_RANK, MLA_NOPE, MLA_ROPE, MLA_V = 4, 256, 128, 128, 64, 128
MLA_QK = MLA_NOPE + MLA_ROPE
MLA_PAD = 256
SWA_HEADS, SWA_KV_HEADS, SWA_DIM, SWA_WINDOW = 6, 2, 64, 128
CAPACITY_FACTOR = 2
N_BRANCH = 3
ROPE_BASE = 10000.0
EPS = 1e-6

NA_W = NA_HEADS * NA_DIM
SWA_W = SWA_HEADS * SWA_DIM
SWA_KV_W = SWA_KV_HEADS * SWA_DIM
MLA_W = MLA_HEADS * MLA_V

LANES = 128
TM = 256
TT = 512
MLA_TQ = 1024
MLA_TK = 512
LOG2E = 1.4426950408889634
NA_TILE_ROWS = TM // GRID_W
NA_BAND_ROWS = NA_TILE_ROWS + NA_WIN_ROWS - 1
NA_BAND = NA_BAND_ROWS * GRID_W
SWA_KEYS = TM + 2 * SWA_WINDOW
MOE_CHUNK = 256
MOE_WIN = 64
NEG = -1e30
VMEM_LIMIT = 56 * 1024 * 1024

P_NAQ, P_NAK, P_NAV, P_SWQ = 0, 384, 768, 1152
P_CQ, P_CKV, P_KR, P_SWK, P_SWV = 1536, 1792, 1920, 2048, 2176
PROJ_W = 2304
N_CHUNK = 768
A_NAQ, A_NAK, A_SWQ, A_SWK = 0, 384, 768, 1152
QKA_W = 1280
SWA_HEAD_ORDER = (0, 3, 1, 4, 2, 5)

bf16 = jnp.bfloat16
f32 = jnp.float32


def _mm(a, b):
    return jnp.dot(a, b, preferred_element_type=f32)


def _nt(a, b):
    return lax.dot_general(a, b, (((1,), (1,)), ((), ())), preferred_element_type=f32)


def _params(*sem):
    return pltpu.CompilerParams(dimension_semantics=sem, vmem_limit_bytes=VMEM_LIMIT)


def _ada_kernel(c_ref, w_ref, b_ref, o_ref):
    a = c_ref[...]
    a = (a * jax.nn.sigmoid(a)).astype(bf16)
    o_ref[...] = _mm(a, w_ref[...].astype(bf16)) + b_ref[...]


def _ada(cpad, w_ada, b_ada):
    L, D, N = w_ada.shape
    tn = 1536
    return pl.pallas_call(
        _ada_kernel,
        grid=(L, N // tn),
        in_specs=[
            pl.BlockSpec((16, D), lambda l, j: (0, 0)),
            pl.BlockSpec((None, D, tn), lambda l, j: (l, 0, j)),
            pl.BlockSpec((None, 1, tn), lambda l, j: (l, 0, j)),
        ],
        out_specs=pl.BlockSpec((None, 16, tn), lambda l, j: (l, 0, j)),
        out_shape=jax.ShapeDtypeStruct((L, 16, N), f32),
        compiler_params=_params("arbitrary", "arbitrary"),
        name="ada",
    )(cpad, w_ada, b_ada.reshape(L, 1, N))


def _in_proj_kernel(has_moe, n_gl_chunks, *refs):
    if has_moe:
        x_ref, moe_ref, modp_ref, mod_ref, g_ref, w_ref, xo_ref, proj_ref, gl_ref = refs
    else:
        x_ref, mod_ref, g_ref, w_ref, proj_ref, gl_ref = refs
    n_sub = 2
    sub = x_ref.shape[0] // n_sub
    for r in range(n_sub):
        rows = slice(r * sub, (r + 1) * sub)
        x = x_ref[rows, :]
        if has_moe:
            x = x + modp_ref[5:6, :] * moe_ref[rows, :]
            xo_ref[rows, :] = x
        ms = jnp.mean(x * x, axis=-1, keepdims=True)
        h = x * lax.rsqrt(ms + EPS) * (g_ref[...] * (1.0 + mod_ref[1:2, :])) + mod_ref[0:1, :]
        h = h.astype(bf16)
        for c in range(PROJ_W // N_CHUNK):
            sl = slice(c * N_CHUNK, (c + 1) * N_CHUNK)
            proj_ref[rows, sl] = _mm(h, w_ref[:, sl]).astype(bf16)
        for c in range(n_gl_chunks):
            sl = slice(c * N_CHUNK, (c + 1) * N_CHUNK)
            gl_ref[rows, sl] = _mm(h, w_ref[:, PROJ_W + c * N_CHUNK:PROJ_W + (c + 1) * N_CHUNK]).astype(bf16)


def _mod_spec(n_lat, D):
    return pl.BlockSpec((None, None, 6, D), lambda b, i: (b, jnp.where(i >= n_lat, 1, 0), 0, 0))


def _in_proj(x, moe, modp, mod, g1, w, n_lat):
    B, T, D = x.shape
    nt = pl.cdiv(T, TT)
    has_moe = moe is not None
    n_gl_chunks = N_BRANCH * D // N_CHUNK
    tok = pl.BlockSpec((None, TT, D), lambda b, i: (b, i, 0))
    in_specs = [tok]
    args = [x]
    if has_moe:
        in_specs += [tok, _mod_spec(n_lat, D)]
        args += [moe, modp]
    in_specs += [
        _mod_spec(n_lat, D),
        pl.BlockSpec((1, D), lambda b, i: (0, 0)),
        pl.BlockSpec(w.shape, lambda b, i: (0, 0), pipeline_mode=pl.Buffered(1)),
    ]
    args += [mod, g1, w]
    out_specs = [
        pl.BlockSpec((None, TT, PROJ_W), lambda b, i: (b, i, 0)),
        pl.BlockSpec((None, TT, N_BRANCH * D), lambda b, i: (b, i, 0)),
    ]
    out_shape = [
        jax.ShapeDtypeStruct((B, T, PROJ_W), bf16),
        jax.ShapeDtypeStruct((B, T, N_BRANCH * D), bf16),
    ]
    if has_moe:
        out_specs = [tok] + out_specs
        out_shape = [jax.ShapeDtypeStruct((B, T, D), f32)] + out_shape
    res = pl.pallas_call(
        functools.partial(_in_proj_kernel, has_moe, n_gl_chunks),
        grid=(B, nt),
        in_specs=in_specs,
        out_specs=out_specs,
        out_shape=out_shape,
        compiler_params=_params("arbitrary", "arbitrary"),
        name="in_proj",
    )(*args)
    if has_moe:
        return res
    return [x] + list(res)


def _prep_kernel(proj_ref, cos_ref, sin_ref, perm_ref, gmat_ref, naq_g, nak_g, swq_g, swk_g,
                 qn_ref, wuq_ref, qg_ref, kvn_ref, wuk_ref, wuv_ref, kg_ref,
                 qka_ref, mq_ref, mk_ref, mv_ref):
    cos = cos_ref[...]
    sin = sin_ref[...]
    perm = perm_ref[...]

    def rope(x):
        hi = x.astype(bf16)
        lo = (x - hi.astype(f32)).astype(bf16)
        partner = _mm(hi, perm) + _mm(lo, perm)
        return x * cos + partner * sin

    def headnorm(x, gain):
        w = x.shape[1]
        ms = _mm((x * x).astype(bf16), gmat_ref[:w, :w])
        return x * lax.rsqrt(ms + EPS) * gain

    def rmsnorm(x, gain):
        ms = jnp.mean(x * x, axis=-1, keepdims=True)
        return x * lax.rsqrt(ms + EPS) * gain

    na_scale = NA_DIM ** -0.5 * LOG2E
    naq = headnorm(proj_ref[:, P_NAQ:P_NAQ + NA_W].astype(f32), naq_g[...])
    qka_ref[:, A_NAQ:A_NAQ + NA_W] = (naq * na_scale).astype(bf16)
    nak = headnorm(proj_ref[:, P_NAK:P_NAK + NA_W].astype(f32), nak_g[...])
    qka_ref[:, A_NAK:A_NAK + NA_W] = nak.astype(bf16)

    sw_scale = SWA_DIM ** -0.5 * LOG2E
    swq = headnorm(proj_ref[:, P_SWQ:P_SWQ + SWA_W].astype(f32), swq_g[...])
    for p in range(SWA_W // LANES):
        sl = slice(p * LANES, (p + 1) * LANES)
        qka_ref[:, A_SWQ + p * LANES:A_SWQ + (p + 1) * LANES] = (rope(swq[:, sl]) * sw_scale).astype(bf16)
    swk = headnorm(proj_ref[:, P_SWK:P_SWK + SWA_KV_W].astype(f32), swk_g[...])
    qka_ref[:, A_SWK:A_SWK + SWA_KV_W] = rope(swk).astype(bf16)

    mla_scale = MLA_QK ** -0.5 * LOG2E
    cq = rmsnorm(proj_ref[:, P_CQ:P_CQ + MLA_Q_RANK].astype(f32), qn_ref[...]).astype(bf16)
    q = _mm(cq, wuq_ref[...])
    ckv = rmsnorm(proj_ref[:, P_CKV:P_CKV + MLA_KV_RANK].astype(f32), kvn_ref[...]).astype(bf16)
    kn = _mm(ckv, wuk_ref[...])
    v = _mm(ckv, wuv_ref[...]).astype(bf16)
    ones_col = jnp.where(lax.broadcasted_iota(jnp.int32, (v.shape[0], LANES), 1) == 0, 1.0, 0.0).astype(bf16)
    for h in range(MLA_HEADS):
        mv_ref[:, h * MLA_PAD:h * MLA_PAD + MLA_V] = v[:, h * MLA_V:(h + 1) * MLA_V]
        mv_ref[:, h * MLA_PAD + MLA_V:(h + 1) * MLA_PAD] = ones_col
    kr = proj_ref[:, P_KR:P_KR + LANES].astype(f32)
    kr_ss = jnp.sum(kr * kr, axis=-1, keepdims=True)
    for h in range(MLA_HEADS):
        o = h * MLA_PAD
        qh = q[:, o:o + MLA_PAD]
        r = lax.rsqrt(jnp.sum(qh * qh, axis=-1, keepdims=True) * (1.0 / MLA_QK) + EPS)
        qh = qh * r * qg_ref[:, o:o + MLA_PAD]
        mq_ref[:, o:o + LANES] = (qh[:, :LANES] * mla_scale).astype(bf16)
        mq_ref[:, o + LANES:o + MLA_PAD] = (rope(qh[:, LANES:]) * mla_scale).astype(bf16)
        kh = kn[:, h * MLA_NOPE:(h + 1) * MLA_NOPE]
        r = lax.rsqrt((jnp.sum(kh * kh, axis=-1, keepdims=True) + kr_ss) * (1.0 / MLA_QK) + EPS)
        mk_ref[:, o:o + LANES] = (kh * r * kg_ref[:, o:o + LANES]).astype(bf16)
        mk_ref[:, o + LANES:o + MLA_PAD] = rope(kr * r * kg_ref[:, o + LANES:o + MLA_PAD]).astype(bf16)


def _prep(proj, cos, sin, gmat, lw):
    B, T, _ = proj.shape
    nt = pl.cdiv(T, TT)

    def const(a):
        return pl.BlockSpec(a.shape, lambda b, i: (0,) * a.ndim)

    lane = np.arange(LANES)
    perm = jnp.asarray(lane[:, None] == (lane[None, :] ^ 16), bf16)
    consts = [perm, gmat, lw["naq_g"], lw["nak_g"], lw["swq_g"], lw["swk_g"], lw["qn"], lw["wuq"], lw["qg"],
              lw["kvn"], lw["wuk"], lw["wuv"], lw["kg"]]
    widths = (QKA_W, MLA_HEADS * MLA_PAD, MLA_HEADS * MLA_PAD, MLA_HEADS * MLA_PAD)
    return pl.pallas_call(
        _prep_kernel,
        grid=(B, nt),
        in_specs=[
            pl.BlockSpec((None, TT, PROJ_W), lambda b, i: (b, i, 0)),
            pl.BlockSpec((TT, LANES), lambda b, i: (i, 0)),
            pl.BlockSpec((TT, LANES), lambda b, i: (i, 0)),
        ] + [const(a) for a in consts],
        out_specs=[pl.BlockSpec((None, TT, w), lambda b, i: (b, i, 0)) for w in widths],
        out_shape=[jax.ShapeDtypeStruct((B, T, w), bf16) for w in widths],
        compiler_params=_params("arbitrary", "arbitrary"),
        name="prep",
    )(proj, cos, sin, *consts)


def _half_masks():
    lane = lax.broadcasted_iota(jnp.int32, (1, LANES), 1)
    lo = jnp.where(lane < 64, 1.0, 0.0).astype(bf16)
    return lo, (1.0 - lo.astype(f32)).astype(bf16)


def _pair_attend(qp, parts, masks, sinks=None):
    n_q = qp.shape[0]
    q2 = jnp.concatenate([qp * masks[0], qp * masks[1]], axis=0)
    scores = []
    for k, _, bias in parts:
        s = _nt(q2, k)
        scores.append(s if bias is None else s + bias)
    m = functools.reduce(jnp.maximum, [jnp.max(s, axis=-1, keepdims=True) for s in scores])
    if sinks is not None:
        first = lax.broadcasted_iota(jnp.int32, (2 * n_q, 1), 0) < n_q
        sink = jnp.where(first, sinks[0], sinks[1])
        m = jnp.maximum(m, sink)
    ps = [jnp.exp2(s - m) for s in scores]
    l = functools.reduce(jnp.add, [jnp.sum(p, axis=-1, keepdims=True) for p in ps])
    if sinks is not None:
        l = l + jnp.exp2(sink - m)
    o = functools.reduce(jnp.add, [_mm(p.astype(bf16), v) for p, (_, v, _) in zip(ps, parts)]) / l
    lane = lax.broadcasted_iota(jnp.int32, (1, LANES), 1)
    return jnp.where(lane < 64, o[:n_q], o[n_q:])


def _na_kernel(n_lat, S, rows, offs, q_ref, k_ref, v_ref, tall_ref, rmask_ref, o_ref):
    i = pl.program_id(1)
    masks = _half_masks()
    T = k_ref.shape[0]

    def latent(ty):
        start = GRID_W * jnp.clip(NA_TILE_ROWS * i - NA_WIN_ROWS // 2, 0, rows - NA_BAND_ROWS)
        start = pl.multiple_of(start, GRID_W)
        for p in range(NA_W // LANES):
            sl = slice(p * LANES, (p + 1) * LANES)
            slabs = []
            for head in (2 * p, 2 * p + 1):
                for qr in range(NA_TILE_ROWS):
                    odd = int(offs[ty, qr]) % 2
                    lo = (int(offs[ty, qr]) - odd) * GRID_W
                    slabs.append(tall_ref[head, odd, :, lo:lo + NA_BAND] + rmask_ref[ty, qr:qr + 1, :])
            band = (k_ref[pl.ds(start, NA_BAND), sl], v_ref[pl.ds(start, NA_BAND), sl],
                    jnp.concatenate(slabs, axis=0))
            ctx = (k_ref[S:T, sl], v_ref[S:T, sl], None)
            o_ref[:, sl] = _pair_attend(q_ref[:, sl], [band, ctx], masks).astype(bf16)

    pl.when(i == 0)(functools.partial(latent, 0))
    pl.when(jnp.logical_and(i > 0, i < n_lat - 1))(functools.partial(latent, 1))
    pl.when(i == n_lat - 1)(functools.partial(latent, 2))

    @pl.when(i >= n_lat)
    def _():
        for p in range(NA_W // LANES):
            sl = slice(p * LANES, (p + 1) * LANES)
            ctx = (k_ref[S:T, sl], v_ref[S:T, sl], None)
            o_ref[:, sl] = _pair_attend(q_ref[:, sl], [ctx], masks).astype(bf16)


def _na(qka, proj, rpb, S):
    B, T, _ = qka.shape
    nt, n_lat, rows = T // TM, S // TM, S // GRID_W
    assert n_lat >= 3
    offs, front, back, row_mask = _na_geometry(rows)
    tall = _na_bias(rpb, front, back)
    row_mask = jnp.asarray(row_mask)
    return pl.pallas_call(
        functools.partial(_na_kernel, n_lat, S, rows, offs),
        grid=(B, nt),
        in_specs=[
            pl.BlockSpec((None, TM, NA_W), lambda b, i: (b, i, A_NAQ // NA_W)),
            pl.BlockSpec((None, T, NA_W), lambda b, i: (b, 0, A_NAK // NA_W)),
            pl.BlockSpec((None, T, NA_W), lambda b, i: (b, 0, P_NAV // NA_W)),
            pl.BlockSpec(tall.shape, lambda b, i: (0, 0, 0, 0)),
            pl.BlockSpec(row_mask.shape, lambda b, i: (0, 0, 0)),
        ],
        out_specs=pl.BlockSpec((None, TM, NA_W), lambda b, i: (b, i, 0)),
        out_shape=jax.ShapeDtypeStruct((B, T, NA_W), bf16),
        compiler_params=_params("arbitrary", "arbitrary"),
        name="na_attn",
    )(qka, qka, proj, tall, row_mask)


def _na_geometry(rows):
    n_tiles = rows // NA_TILE_ROWS
    band = np.arange(NA_BAND_ROWS)
    dr0 = np.zeros((3, NA_TILE_ROWS), np.int64)
    valid = np.zeros((3, NA_TILE_ROWS, NA_BAND_ROWS), bool)
    for ty, rt in enumerate((0, 1, n_tiles - 1)):
        bs = int(np.clip(NA_TILE_ROWS * rt - NA_WIN_ROWS // 2, 0, rows - NA_BAND_ROWS))
        for qr in range(NA_TILE_ROWS):
            r = NA_TILE_ROWS * rt + qr
            s_r = int(np.clip(r - NA_WIN_ROWS // 2, 0, rows - NA_WIN_ROWS))
            dr0[ty, qr] = bs - r + NA_WIN_ROWS - 1
            valid[ty, qr] = (bs + band >= s_r) & (bs + band < s_r + NA_WIN_ROWS)
    front = int(max(0, -dr0.min()))
    back = int(max(0, dr0.max() + NA_BAND_ROWS - (2 * NA_WIN_ROWS - 1)))
    row_mask = np.where(np.repeat(valid, GRID_W, axis=2), 0.0, NEG).astype(np.float32)
    return dr0 + front, front, back, row_mask


def _na_bias(rpb, front, back):
    col = np.arange(GRID_W)
    c0 = np.clip(col - NA_WIN_COLS // 2, 0, GRID_W - NA_WIN_COLS)
    col_ok = (col[None, :] >= c0[:, None]) & (col[None, :] < c0[:, None] + NA_WIN_COLS)
    dc = np.clip(col[None, :] - col[:, None] + NA_WIN_COLS - 1, 0, 2 * NA_WIN_COLS - 2)
    pick = jnp.asarray(np.arange(2 * NA_WIN_COLS - 1)[:, None, None] == dc[None], f32)
    tz = jnp.einsum("hrd,dqk->hrqk", rpb, pick, precision=lax.Precision.HIGHEST)
    tz = jnp.where(col_ok, tz * LOG2E, NEG)
    n_blocks = front + tz.shape[1] + back
    width = pl.cdiv(n_blocks * GRID_W, LANES) * LANES
    tall = tz.transpose(0, 2, 1, 3).reshape(NA_HEADS, GRID_W, -1)
    tall = jnp.pad(tall, ((0, 0), (0, 0), (front * GRID_W, width + GRID_W - (front + tz.shape[1]) * GRID_W)),
                   constant_values=NEG)
    return jnp.stack([tall[:, :, :width], tall[:, :, GRID_W:GRID_W + width]], axis=1)


def _mla_kernel(n_q, S, q_ref, k_ref, v_ref, o_ref):
    i = pl.program_id(2)
    T = k_ref.shape[0]

    def attend(q, bounds):
        m_run = acc = None
        for lo, hi in bounds:
            s = _nt(q, k_ref[lo:hi, :])
            m_new = jnp.max(s, axis=-1, keepdims=True)
            if m_run is not None:
                m_new = jnp.maximum(m_run, m_new)
            pv = _mm(jnp.exp2(s - m_new).astype(bf16), v_ref[lo:hi, :])
            acc = pv if acc is None else acc * jnp.exp2(m_run - m_new) + pv
            m_run = m_new
        return (acc[:, :MLA_V] / acc[:, MLA_V:MLA_V + 1]).astype(bf16)

    @pl.when(i < n_q)
    def _():
        chunks = [(lo, lo + MLA_TK) for lo in range(0, S, MLA_TK)] + [(S, T)]
        o_ref[...] = attend(q_ref[...], chunks)

    @pl.when(i >= n_q)
    def _():
        o_ref[0:T - S, :] = attend(q_ref[0:T - S, :], [(S, T)])


def _mla(mq, mk, mv, S):
    B, T, _ = mq.shape
    tq = min(MLA_TQ, S)
    n_q = S // tq
    assert S % tq == 0 and S % MLA_TK == 0 and T - S <= tq
    return pl.pallas_call(
        functools.partial(_mla_kernel, n_q, S),
        grid=(B, MLA_HEADS, n_q + 1),
        in_specs=[
            pl.BlockSpec((None, tq, MLA_PAD), lambda b, h, i: (b, i, h)),
            pl.BlockSpec((None, T, MLA_PAD), lambda b, h, i: (b, 0, h)),
            pl.BlockSpec((None, T, MLA_PAD), lambda b, h, i: (b, 0, h)),
        ],
        out_specs=pl.BlockSpec((None, tq, MLA_V), lambda b, h, i: (b, i, h)),
        out_shape=jax.ShapeDtypeStruct((B, T, MLA_W), bf16),
        compiler_params=_params("arbitrary", "arbitrary", "arbitrary"),
        name="mla_attn",
    )(mq, mk, mv)


def _swa_kernel(n_lat, S, sink_ref, q_ref, k_ref, v_ref, band_ref, o_ref):
    i = pl.program_id(1)
    masks = _half_masks()
    T = k_ref.shape[0]
    kc, vc = k_ref[S:T, :], v_ref[S:T, :]

    def sinks(p):
        return (sink_ref[SWA_HEAD_ORDER[2 * p]], sink_ref[SWA_HEAD_ORDER[2 * p + 1]])

    @pl.when(i < n_lat)
    def _():
        start = pl.multiple_of(jnp.clip(i * TM - SWA_WINDOW, 0, S - SWA_KEYS), SWA_WINDOW)
        band = band_ref[jnp.where(i == 0, 0, jnp.where(i == n_lat - 1, 2, 1))]
        kb, vb = k_ref[pl.ds(start, SWA_KEYS), :], v_ref[pl.ds(start, SWA_KEYS), :]
        for p in range(SWA_W // LANES):
            sl = slice(p * LANES, (p + 1) * LANES)
            parts = [(kb, vb, band), (kc, vc, None)]
            o_ref[:, sl] = _pair_attend(q_ref[:, sl], parts, masks, sinks(p)).astype(bf16)

    @pl.when(i >= n_lat)
    def _():
        for p in range(SWA_W // LANES):
            sl = slice(p * LANES, (p + 1) * LANES)
            o_ref[:, sl] = _pair_attend(q_ref[:, sl], [(kc, vc, None)], masks, sinks(p)).astype(bf16)


def _swa_band_masks(S):
    n_lat = S // TM
    out = np.zeros((3, 2 * TM, SWA_KEYS), np.float32)
    for ty, i in enumerate((0, 1, n_lat - 1)):
        start = int(np.clip(i * TM - SWA_WINDOW, 0, S - SWA_KEYS))
        qpos = i * TM + np.tile(np.arange(TM), 2)
        kpos = start + np.arange(SWA_KEYS)
        out[ty] = np.where(np.abs(qpos[:, None] - kpos[None, :]) <= SWA_WINDOW, 0.0, NEG)
    return jnp.asarray(out)


def _swa(qka, proj, sink, S):
    B, T, _ = qka.shape
    nt, n_lat = T // TM, S // TM
    assert n_lat >= 3
    band = _swa_band_masks(S)
    return pl.pallas_call(
        functools.partial(_swa_kernel, n_lat, S),
        grid=(B, nt),
        in_specs=[
            pl.BlockSpec(memory_space=pltpu.SMEM),
            pl.BlockSpec((None, TM, SWA_W), lambda b, i: (b, i, A_SWQ // SWA_W)),
            pl.BlockSpec((None, T, SWA_KV_W), lambda b, i: (b, 0, A_SWK // SWA_KV_W)),
            pl.BlockSpec((None, T, SWA_KV_W), lambda b, i: (b, 0, P_SWV // SWA_KV_W)),
            pl.BlockSpec(band.shape, lambda b, i: (0, 0, 0)),
        ],
        out_specs=pl.BlockSpec((None, TM, SWA_W), lambda b, i: (b, i, 0)),
        out_shape=jax.ShapeDtypeStruct((B, T, SWA_W), bf16),
        compiler_params=_params("arbitrary", "arbitrary"),
        name="swa_attn",
    )(sink, qka, qka, proj, band)


def _post_kernel(ona_ref, omla_ref, oswa_ref, gl_ref, x_ref, mod_ref, wna_ref, wmla_ref, wswa_ref, wo_ref,
                 g2_ref, rt_ref, xo_ref, h2_ref, aff_ref, affx_ref):
    D = x_ref.shape[1]
    n_e = aff_ref.shape[1]
    n_sub = 2
    sub = x_ref.shape[0] // n_sub
    for r in range(n_sub):
        rows = slice(r * sub, (r + 1) * sub)
        merged = None
        for j, (o_ref, w_ref) in enumerate(((ona_ref, wna_ref), (omla_ref, wmla_ref), (oswa_ref, wswa_ref))):
            y = _mm(o_ref[rows, :], w_ref[...])
            gy = (jnp.tanh(gl_ref[rows, j * D:(j + 1) * D].astype(f32)) + 1.0) * y
            merged = gy if merged is None else merged + gy
        res = _mm(merged.astype(bf16), wo_ref[...])
        x = x_ref[rows, :] + mod_ref[2:3, :] * res
        xo_ref[rows, :] = x
        ms = jnp.mean(x * x, axis=-1, keepdims=True)
        h2 = x * lax.rsqrt(ms + EPS) * (g2_ref[...] * (1.0 + mod_ref[4:5, :])) + mod_ref[3:4, :]
        h_hi = h2.astype(bf16)
        h2_ref[rows, :] = h_hi
        h_lo = (h2 - h_hi.astype(f32)).astype(bf16)
        parts = _mm(h_hi, rt_ref[...]) + _mm(h_lo, rt_ref[...])
        logits = parts[:, :n_e] + parts[:, n_e:]
        e = jnp.exp(logits - jnp.max(logits, axis=1, keepdims=True))
        aff = e / jnp.sum(e, axis=1, keepdims=True)
        aff_ref[rows, :] = aff
        p0 = aff.astype(bf16)
        r1 = aff - p0.astype(f32)
        p1 = r1.astype(bf16)
        p2 = (r1 - p1.astype(f32)).astype(bf16)
        pad = jnp.zeros((sub, LANES - 3 * n_e), bf16)
        affx_ref[rows, :] = jnp.concatenate([p0, p1, p2, pad], axis=1)


def _post(ona, omla, oswa, gl, x, mod, lw, n_lat):
    B, T, D = x.shape
    nt = pl.cdiv(T, TT)
    E = lw["router"].shape[1] // 2

    def tok(w):
        return pl.BlockSpec((None, TT, w), lambda b, i: (b, i, 0))

    def const(a):
        return pl.BlockSpec(a.shape, lambda b, i: (0,) * a.ndim)

    consts = [lw["wna"], lw["wmla"], lw["wswa"], lw["wo"], lw["g2"], lw["router"]]
    return pl.pallas_call(
        _post_kernel,
        grid=(B, nt),
        in_specs=[tok(NA_W), tok(MLA_W), tok(SWA_W), tok(N_BRANCH * D), tok(D), _mod_spec(n_lat, D)]
        + [const(a) for a in consts],
        out_specs=[tok(D), tok(D), tok(E), tok(LANES)],
        out_shape=[jax.ShapeDtypeStruct((B, T, D), f32), jax.ShapeDtypeStruct((B, T, D), bf16),
                   jax.ShapeDtypeStruct((B, T, E), f32), jax.ShapeDtypeStruct((B, T, LANES), bf16)],
        compiler_params=_params("arbitrary", "arbitrary"),
        name="post_attn",
    )(ona, omla, oswa, gl, x, mod, *consts)


def _lane_cumsum(mask, tri):
    E, n = mask.shape
    carry = jnp.zeros((E, 1), f32)
    outs = []
    for k in range(n // LANES):
        w = _mm(mask[:, k * LANES:(k + 1) * LANES].astype(bf16), tri) + carry
        outs.append(w)
        carry = w[:, LANES - 1:LANES]
    return jnp.concatenate(outs, axis=1)


def _select_slots(aff, cap, base, tri):
    bits = lax.bitcast_convert_type(aff, jnp.int32)
    thr = jnp.zeros((aff.shape[0], 1), jnp.int32)
    for bit in range(30, -1, -1):
        cand = thr | (1 << bit)
        cnt = jnp.sum(jnp.where(bits >= cand, 1.0, 0.0), axis=1, keepdims=True)
        thr = jnp.where(cnt >= cap, cand, thr)
    gt = jnp.where(bits > thr, 1.0, 0.0)
    eq = jnp.where(bits == thr, 1.0, 0.0)
    need = cap - jnp.sum(gt, axis=1, keepdims=True)
    sel = jnp.maximum(gt, jnp.where(_lane_cumsum(eq, tri) <= need, eq, 0.0))
    return jnp.where(sel > 0.0, _lane_cumsum(sel, tri) + base, 0.0)


def _topk_kernel(S, cap_s, cap_l, aff_ref, tri_ref, before_ref, cp_ref, bnd_ref):
    T = aff_ref.shape[1]
    tri = tri_ref[...]
    cp_ref[:, 0:S] = _select_slots(aff_ref[:, 0:S], cap_s, 0.0, tri)
    cp_ref[:, S:T] = _select_slots(aff_ref[:, S:T], cap_l, float(cap_s), tri)
    bnd_ref[...] = _mm(jnp.where(cp_ref[...] > 0.0, 1.0, 0.0).astype(bf16), before_ref[...])


def _topk(aff, S, cap_s, cap_l):
    B, E, T = aff.shape
    tri = jnp.asarray(np.triu(np.ones((LANES, LANES), np.float32)), bf16)
    before = jnp.asarray(np.arange(T)[:, None] < MOE_CHUNK * np.arange(LANES)[None, :], bf16)
    return pl.pallas_call(
        functools.partial(_topk_kernel, S, cap_s, cap_l),
        grid=(B,),
        in_specs=[pl.BlockSpec((None, E, T), lambda b: (b, 0, 0)),
                  pl.BlockSpec((LANES, LANES), lambda b: (0, 0)),
                  pl.BlockSpec((T, LANES), lambda b: (0, 0))],
        out_specs=[pl.BlockSpec((None, E, T), lambda b: (b, 0, 0)),
                   pl.BlockSpec((None, E, LANES), lambda b: (b, 0, 0))],
        out_shape=[jax.ShapeDtypeStruct((B, E, T), f32), jax.ShapeDtypeStruct((B, E, LANES), f32)],
        compiler_params=_params("arbitrary"),
        name="expert_select",
    )(aff, tri, before)


def _window_count(lo, hi, r0):
    return jnp.where(hi > lo, lax.div(hi - r0 + (MOE_WIN - 1), MOE_WIN), 0)


def _gather_kernel(n_e, rows, bnd_ref, h2_ref, affx_ref, cp_ref, xg_ref):
    b, k = pl.program_id(0), pl.program_id(1)
    D = h2_ref.shape[1]

    @pl.when(k == 0)
    def _():
        xg_ref[...] = jnp.zeros_like(xg_ref)

    ex = lax.broadcasted_iota(jnp.int32, (n_e, 1), 0)
    starts = []
    start_col = jnp.zeros((n_e, 1), f32)
    for e in range(n_e):
        r0 = lax.div(bnd_ref[(b * n_e + e) * LANES + k], 16) * 16
        starts.append(r0)
        start_col = start_col + jnp.where(ex == e, r0.astype(f32), 0.0)
    rel = cp_ref[...] - start_col
    rel_rows = jnp.broadcast_to(rel[:, None, :], (n_e, MOE_WIN, MOE_CHUNK)).reshape(n_e * MOE_WIN, MOE_CHUNK)
    j1 = lax.broadcasted_iota(jnp.int32, (n_e, MOE_WIN, 1), 1).reshape(n_e * MOE_WIN, 1) + 1
    onehot = jnp.where(rel_rows == j1.astype(f32), 1.0, 0.0).astype(bf16)
    tok, tok_aff = h2_ref[...], affx_ref[...]
    moved = _mm(onehot, tok).astype(bf16)
    moved_aff = _mm(onehot, tok_aff).astype(bf16)
    for e in range(n_e):
        dst = pl.ds(pl.multiple_of(e * rows + starts[e], 16), MOE_WIN)
        xg_ref[dst, 0:D] += moved[e * MOE_WIN:(e + 1) * MOE_WIN]
        xg_ref[dst, D:D + LANES] += moved_aff[e * MOE_WIN:(e + 1) * MOE_WIN]

    sub1 = lax.broadcasted_iota(jnp.int32, (MOE_WIN, 1), 0) + 1
    for e in range(n_e):
        lo, hi = bnd_ref[(b * n_e + e) * LANES + k], bnd_ref[(b * n_e + e) * LANES + k + 1]
        n_win = _window_count(lo, hi, starts[e])

        @pl.when(n_win > 1)
        def _(e=e, n_win=n_win):
            def extra(w, c):
                hit = rel[e:e + 1, :] == (sub1 + w * MOE_WIN).astype(f32)
                oh = jnp.where(hit, 1.0, 0.0).astype(bf16)
                dst = pl.ds(pl.multiple_of(e * rows + starts[e] + w * MOE_WIN, 16), MOE_WIN)
                xg_ref[dst, 0:D] += _mm(oh, tok).astype(bf16)
                xg_ref[dst, D:D + LANES] += _mm(oh, tok_aff).astype(bf16)
                return c

            lax.fori_loop(1, n_win, extra, 0)


def _gather(h2, affx, cp, bnd, n_slots):
    B, T, D = h2.shape
    E = cp.shape[1]
    nc = T // MOE_CHUNK
    rows = n_slots + MOE_WIN
    assert n_slots % 16 == 0 and nc + 1 <= LANES and 3 * E <= LANES
    return pl.pallas_call(
        functools.partial(_gather_kernel, E, rows),
        grid_spec=pltpu.PrefetchScalarGridSpec(
            num_scalar_prefetch=1,
            grid=(B, nc),
            in_specs=[
                pl.BlockSpec((None, MOE_CHUNK, D), lambda b, k, s: (b, k, 0)),
                pl.BlockSpec((None, MOE_CHUNK, LANES), lambda b, k, s: (b, k, 0)),
                pl.BlockSpec((None, E, MOE_CHUNK), lambda b, k, s: (b, 0, k)),
            ],
            out_specs=pl.BlockSpec((None, E * rows, D + LANES), lambda b, k, s: (b, 0, 0),
                                   pipeline_mode=pl.Buffered(1)),
        ),
        out_shape=jax.ShapeDtypeStruct((B, E * rows, D + LANES), bf16),
        compiler_params=_params("arbitrary", "arbitrary"),
        name="moe_gather",
    )(bnd, h2, affx, cp)


def _ffn_kernel(n_slots, n_e, xg_ref, wg_ref, wu_ref, wd_ref, y_ref, wg_s, wu_s, wd_s):
    e, b = pl.program_id(0), pl.program_id(1)
    D = wg_ref.shape[0]

    @pl.when(b == 0)
    def _():
        wg_s[...] = wg_ref[...].astype(bf16)
        wu_s[...] = wu_ref[...].astype(bf16)
        wd_s[...] = wd_ref[...].astype(bf16)

    n_b = xg_ref.shape[0]
    xg = jnp.concatenate([xg_ref[s, 0:n_slots, 0:D] for s in range(n_b)], axis=0)
    a = _mm(xg, wg_s[...])
    u = _mm(xg, wu_s[...])
    act = (a * jax.nn.sigmoid(a) * u).astype(bf16)
    lane = lax.broadcasted_iota(jnp.int32, (1, LANES), 1)
    mine = jnp.logical_and(lax.rem(lane, n_e) == e, lane < 3 * n_e)
    parts = jnp.concatenate([xg_ref[s, 0:n_slots, D:D + LANES] for s in range(n_b)], axis=0).astype(f32)
    gate = jnp.sum(jnp.where(mine, parts, 0.0), axis=1, keepdims=True)
    y = (_mm(act, wd_s[...]) * gate).astype(bf16)
    for s in range(n_b):
        y_ref[s, 0:n_slots, :] = y[s * n_slots:(s + 1) * n_slots]
        y_ref[s, n_slots:, :] = jnp.zeros((y_ref.shape[1] - n_slots, y_ref.shape[2]), bf16)


def _ffn(xg, w_gate, w_up, w_down, layer, n_slots):
    B = xg.shape[0]
    _, E, D, F = w_gate.shape
    rows = xg.shape[1] // E
    n_b = 2 if B % 2 == 0 else 1
    return pl.pallas_call(
        functools.partial(_ffn_kernel, n_slots, E),
        grid=(E, B // n_b),
        in_specs=[
            pl.BlockSpec((n_b, None, rows, D + LANES), lambda e, b: (b, e, 0, 0)),
            pl.BlockSpec((None, None, D, F), lambda e, b: (layer, e, 0, 0)),
            pl.BlockSpec((None, None, D, F), lambda e, b: (layer, e, 0, 0)),
            pl.BlockSpec((None, None, F, D), lambda e, b: (layer, e, 0, 0)),
        ],
        out_specs=pl.BlockSpec((n_b, None, rows, D), lambda e, b: (b, e, 0, 0)),
        out_shape=jax.ShapeDtypeStruct((B, E, rows, D), bf16),
        scratch_shapes=[pltpu.VMEM((D, F), bf16), pltpu.VMEM((D, F), bf16), pltpu.VMEM((F, D), bf16)],
        compiler_params=_params("arbitrary", "arbitrary"),
        name="moe_ffn",
    )(xg.reshape(B, E, rows, D + LANES), w_gate, w_up, w_down)


def _combine_kernel(n_e, rows, residual, bnd_ref, y_ref, cpt_ref, expand_ref, *refs):
    o_ref = refs[-1]
    b, k = pl.program_id(0), pl.program_id(1)
    cpt = cpt_ref[...]
    hi_part = jnp.floor(cpt * (1.0 / 32.0))
    lo_part = cpt - 32.0 * hi_part
    expand = expand_ref[...]
    rank = 32.0 * _mm(hi_part.astype(bf16), expand) + _mm(lo_part.astype(bf16), expand)
    lane = lax.broadcasted_iota(jnp.int32, (1, n_e * MOE_WIN), 1)
    lane_e = lax.div(lane, MOE_WIN)
    starts = []
    tgt = (lane - lane_e * MOE_WIN + 1).astype(f32)
    for e in range(n_e):
        r0 = lax.div(bnd_ref[(b * n_e + e) * LANES + k], 16) * 16
        starts.append(r0)
        tgt = tgt + jnp.where(lane_e == e, r0.astype(f32), 0.0)
    onehot = jnp.where(rank == tgt, 1.0, 0.0).astype(bf16)
    ycat = jnp.concatenate(
        [y_ref[pl.ds(pl.multiple_of(e * rows + starts[e], 16), MOE_WIN), :] for e in range(n_e)], axis=0)
    o_ref[...] = _mm(onehot, ycat)

    lane_w = lax.broadcasted_iota(jnp.int32, (1, MOE_WIN), 1)
    for e in range(n_e):
        lo, hi = bnd_ref[(b * n_e + e) * LANES + k], bnd_ref[(b * n_e + e) * LANES + k + 1]
        n_win = _window_count(lo, hi, starts[e])

        @pl.when(n_win > 1)
        def _(e=e, n_win=n_win):
            rank_e = rank[:, e * MOE_WIN:(e + 1) * MOE_WIN]

            def extra(w, c):
                r = pl.multiple_of(starts[e] + w * MOE_WIN, 16)
                hit = rank_e == (lane_w + (r + 1)).astype(f32)
                o_ref[...] += _mm(jnp.where(hit, 1.0, 0.0).astype(bf16),
                                  y_ref[pl.ds(pl.multiple_of(e * rows + r, 16), MOE_WIN), :])
                return c

            lax.fori_loop(1, n_win, extra, 0)

    if residual:
        x_ref, mod_ref = refs[0], refs[1]
        o_ref[...] = x_ref[...] + mod_ref[5:6, :] * o_ref[...]


def _combine(y, cp, bnd, x=None, mod=None, n_tok=None):
    B, E, rows, D = y.shape
    residual = x is not None
    n_tok = n_tok if residual else cp.shape[2]
    expand = jnp.asarray(np.kron(np.eye(E), np.ones((1, MOE_WIN))), bf16)
    chunk = pl.BlockSpec((None, MOE_CHUNK, D), lambda b, k, s: (b, k, 0))
    in_specs = [
        pl.BlockSpec((None, E * rows, D), lambda b, k, s: (b, 0, 0), pipeline_mode=pl.Buffered(1)),
        pl.BlockSpec((None, MOE_CHUNK, E), lambda b, k, s: (b, k, 0)),
        pl.BlockSpec(expand.shape, lambda b, k, s: (0, 0)),
    ]
    args = [bnd, y.reshape(B, E * rows, D), cp.transpose(0, 2, 1), expand]
    if residual:
        in_specs += [chunk, pl.BlockSpec((None, None, 6, D), lambda b, k, s: (b, 0, 0, 0))]
        args += [x, mod]
    return pl.pallas_call(
        functools.partial(_combine_kernel, E, rows, residual),
        grid_spec=pltpu.PrefetchScalarGridSpec(
            num_scalar_prefetch=1,
            grid=(B, n_tok // MOE_CHUNK),
            in_specs=in_specs,
            out_specs=chunk,
        ),
        out_shape=jax.ShapeDtypeStruct((B, n_tok, D), f32),
        compiler_params=_params("arbitrary", "arbitrary"),
        name="moe_combine",
    )(*args)


def _rope_tables(S, L):
    t = jnp.arange(S)
    row, col = t // GRID_W, t % GRID_W
    f = 16
    inv = ROPE_BASE ** (-jnp.arange(f, dtype=f32) / f)
    ar = row.astype(f32)[:, None] * inv
    ac = col.astype(f32)[:, None] * inv
    cos = jnp.concatenate([jnp.cos(ar), jnp.cos(ar), jnp.cos(ac), jnp.cos(ac)], axis=1)
    sin = jnp.concatenate([-jnp.sin(ar), jnp.sin(ar), -jnp.sin(ac), jnp.sin(ac)], axis=1)
    cos = jnp.concatenate([jnp.tile(cos, (1, 2)), jnp.ones((L, LANES), f32)], axis=0)
    sin = jnp.concatenate([jnp.tile(sin, (1, 2)), jnp.zeros((L, LANES), f32)], axis=0)
    return cos, sin


def _pad_heads(w, n_heads, width, padded):
    lead = w.shape[:-1]
    w = w.reshape(lead + (n_heads, width))
    w = jnp.pad(w, [(0, 0)] * len(lead) + [(0, 0), (0, padded - width)])
    return w.reshape(lead + (n_heads * padded,))


def _layer_weights(l, D, w_in, norm2_g, na_q_g, na_k_g, mla_q_norm, mla_w_uq, mla_kv_norm, mla_w_ukv,
                   mla_q_g, mla_k_g, swa_q_g, swa_k_g, w_na_o, w_mla_o, w_swa_o, w_o, router):
    o = np.cumsum((0, NA_W, NA_W, NA_W, MLA_Q_RANK, MLA_KV_RANK, MLA_ROPE, SWA_W, SWA_KV_W, SWA_KV_W))
    o_naq, o_nak, o_nav, o_cq, o_ckv, o_kr, o_swq, o_swk, o_swv, o_gl = (int(v) for v in o)
    wi = w_in[l]
    swq = wi[:, o_swq:o_swq + SWA_W].reshape(D, SWA_HEADS, SWA_DIM)[:, SWA_HEAD_ORDER, :].reshape(D, SWA_W)
    kr = jnp.pad(wi[:, o_kr:o_kr + MLA_ROPE], ((0, 0), (0, LANES - MLA_ROPE)))
    w_all = jnp.concatenate([
        wi[:, o_naq:o_naq + 3 * NA_W], swq, wi[:, o_cq:o_cq + MLA_Q_RANK], wi[:, o_ckv:o_ckv + MLA_KV_RANK], kr,
        wi[:, o_swk:o_swk + 2 * SWA_KV_W], 0.5 * wi[:, o_gl:]], axis=1).astype(bf16)
    ukv = mla_w_ukv[l].reshape(MLA_KV_RANK, MLA_HEADS, MLA_NOPE + MLA_V)
    swa_o = w_swa_o[l].reshape(SWA_HEADS, SWA_DIM, D)[SWA_HEAD_ORDER, :, :].reshape(SWA_W, D)
    r_hi = router[l].astype(bf16)
    r_lo = (router[l] - r_hi.astype(f32)).astype(bf16)
    return dict(
        w_all=w_all,
        naq_g=jnp.tile(na_q_g[l], NA_HEADS)[None], nak_g=jnp.tile(na_k_g[l], NA_HEADS)[None],
        swq_g=jnp.tile(swa_q_g[l], SWA_HEADS)[None], swk_g=jnp.tile(swa_k_g[l], SWA_KV_HEADS)[None],
        qn=mla_q_norm[l][None], kvn=mla_kv_norm[l][None],
        wuq=_pad_heads(mla_w_uq[l], MLA_HEADS, MLA_QK, MLA_PAD).astype(bf16),
        wuk=ukv[:, :, :MLA_NOPE].reshape(MLA_KV_RANK, MLA_HEADS * MLA_NOPE).astype(bf16),
        wuv=ukv[:, :, MLA_NOPE:].reshape(MLA_KV_RANK, MLA_W).astype(bf16),
        qg=jnp.tile(jnp.pad(mla_q_g[l], (0, MLA_PAD - MLA_QK)), MLA_HEADS)[None],
        kg=jnp.tile(jnp.pad(mla_k_g[l], (0, MLA_PAD - MLA_QK)), MLA_HEADS)[None],
        wna=w_na_o[l].astype(bf16), wmla=w_mla_o[l].astype(bf16), wswa=swa_o.astype(bf16),
        wo=(0.5 * w_o[l]).astype(bf16), g2=norm2_g[l][None], router=jnp.concatenate([r_hi, r_lo], axis=1),
    )


def kernel(x, c, ctx, c_ctx, norm1_g, norm2_g, w_ada, b_ada, w_in, na_q_g, na_k_g, na_rpb, mla_q_norm, mla_w_uq, mla_kv_norm, mla_w_ukv, mla_q_g, mla_k_g, swa_q_g, swa_k_g, swa_sink, w_na_o, w_mla_o, w_swa_o, w_o, router, w_gate, w_up, w_down):
    B, S, D = x.shape
    L = ctx.shape[1]
    depth = w_in.shape[0]
    E = router.shape[2]
    T = S + L
    n_lat = S // TT
    rows = S // GRID_W
    assert S % TT == 0 and L % TM == 0 and L <= TT and rows >= NA_BAND_ROWS and S >= SWA_KEYS and B < 16
    cap_s = CAPACITY_FACTOR * S // E
    cap_l = CAPACITY_FACTOR * L // E

    cpad = jnp.zeros((16, D), f32).at[:B].set(c).at[B].set(c_ctx)
    mod = _ada(cpad, w_ada, b_ada).reshape(depth, 16, 6, D)
    mod = jnp.stack([mod[:, :B], jnp.broadcast_to(mod[:, B:B + 1], (depth, B, 6, D))], axis=2)

    cos, sin = _rope_tables(S, L)
    gmat = jnp.asarray(np.kron(np.eye(NA_HEADS), np.full((NA_DIM, NA_DIM), 1.0 / NA_DIM)), bf16)

    xs = jnp.concatenate([x, ctx], axis=1)
    moe = None
    for l in range(depth):
        lw = _layer_weights(l, D, w_in, norm2_g, na_q_g, na_k_g, mla_q_norm, mla_w_uq, mla_kv_norm, mla_w_ukv,
                            mla_q_g, mla_k_g, swa_q_g, swa_k_g, w_na_o, w_mla_o, w_swa_o, w_o, router)
        modp = mod[l - 1] if l > 0 else None
        xs, proj, gl = _in_proj(xs, moe, modp, mod[l], norm1_g[l][None], lw["w_all"], n_lat)
        qka, mq, mk, mv = _prep(proj, cos, sin, gmat, lw)
        ona = _na(qka, proj, na_rpb[l], S)
        omla = _mla(mq, mk, mv, S)
        oswa = _swa(qka, proj, swa_sink[l] * LOG2E, S)
        xs, h2, aff, affx = _post(ona, omla, oswa, gl, xs, mod[l], lw, n_lat)
        cp, bnd = _topk(aff.transpose(0, 2, 1), S, cap_s, cap_l)
        bnd = bnd.astype(jnp.int32).reshape(-1)
        xg = _gather(h2, affx, cp, bnd, cap_s + cap_l)
        y = _ffn(xg, w_gate, w_up, w_down, l, cap_s + cap_l)
        if l == depth - 1:
            return _combine(y, cp, bnd, xs, mod[l], S)
        moe = _combine(y, cp, bnd)
```

```python
import functools

import numpy as np
import jax
import jax.numpy as jnp
from jax import lax
from jax.experimental import pallas as pl
from jax.experimental.pallas import tpu as pltpu

GRID_W = 64
NA_HEADS, NA_DIM, NA_WIN_ROWS, NA_WIN_COLS = 6, 64, 8, 16
MLA_HEADS, MLA_Q_RANK, MLA_KV_RANK, MLA_NOPE, MLA_ROPE, MLA_V = 4, 256, 128, 128, 64, 128
MLA_QK = MLA_NOPE + MLA_ROPE
MLA_PAD = 256
SWA_HEADS, SWA_KV_HEADS, SWA_DIM, SWA_WINDOW = 6, 2, 64, 128
CAPACITY_FACTOR = 2
N_BRANCH = 3
ROPE_BASE = 10000.0
EPS = 1e-6

NA_W = NA_HEADS * NA_DIM
SWA_W = SWA_HEADS * SWA_DIM
SWA_KV_W = SWA_KV_HEADS * SWA_DIM
MLA_W = MLA_HEADS * MLA_V

LANES = 128
TM = 256
TT = 512
MLA_TQ = 1024
MLA_TK = 512
LOG2E = 1.4426950408889634
NA_TILE_ROWS = TM // GRID_W
NA_BAND_ROWS = NA_TILE_ROWS + NA_WIN_ROWS - 1
NA_BAND = NA_BAND_ROWS * GRID_W
SWA_KEYS = TM + 2 * SWA_WINDOW
MOE_CHUNK = 256
MOE_WIN = 64
NEG = -1e30
VMEM_LIMIT = 56 * 1024 * 1024

P_NAQ, P_NAK, P_NAV, P_SWQ = 0, 384, 768, 1152
P_CQ, P_CKV, P_KR, P_SWK, P_SWV = 1536, 1792, 1920, 2048, 2176
PROJ_W = 2304
N_CHUNK = 768
A_NAQ, A_NAK, A_SWQ, A_SWK = 0, 384, 768, 1152
QKA_W = 1280
SWA_HEAD_ORDER = (0, 3, 1, 4, 2, 5)

bf16 = jnp.bfloat16
f32 = jnp.float32


def _mm(a, b):
    return jnp.dot(a, b, preferred_element_type=f32)


def _nt(a, b):
    return lax.dot_general(a, b, (((1,), (1,)), ((), ())), preferred_element_type=f32)


def _params(*sem):
    return pltpu.CompilerParams(dimension_semantics=sem, vmem_limit_bytes=VMEM_LIMIT)


def _ada_kernel(c_ref, w_ref, b_ref, o_ref):
    a = c_ref[...]
    a = (a * jax.nn.sigmoid(a)).astype(bf16)
    o_ref[...] = _mm(a, w_ref[...].astype(bf16)) + b_ref[...]


def _ada(cpad, w_ada, b_ada):
    L, D, N = w_ada.shape
    tn = 1536
    return pl.pallas_call(
        _ada_kernel,
        grid=(L, N // tn),
        in_specs=[
            pl.BlockSpec((16, D), lambda l, j: (0, 0)),
            pl.BlockSpec((None, D, tn), lambda l, j: (l, 0, j)),
            pl.BlockSpec((None, 1, tn), lambda l, j: (l, 0, j)),
        ],
        out_specs=pl.BlockSpec((None, 16, tn), lambda l, j: (l, 0, j)),
        out_shape=jax.ShapeDtypeStruct((L, 16, N), f32),
        compiler_params=_params("arbitrary", "arbitrary"),
        name="ada",
    )(cpad, w_ada, b_ada.reshape(L, 1, N))


def _in_proj_kernel(has_moe, n_gl_chunks, *refs):
    if has_moe:
        x_ref, moe_ref, modp_ref, mod_ref, g_ref, w_ref, xo_ref, proj_ref, gl_ref = refs
    else:
        x_ref, mod_ref, g_ref, w_ref, proj_ref, gl_ref = refs
    n_sub = 2
    sub = x_ref.shape[0] // n_sub
    for r in range(n_sub):
        rows = slice(r * sub, (r + 1) * sub)
        x = x_ref[rows, :]
        if has_moe:
            x = x + modp_ref[5:6, :] * moe_ref[rows, :]
            xo_ref[rows, :] = x
        ms = jnp.mean(x * x, axis=-1, keepdims=True)
        h = x * lax.rsqrt(ms + EPS) * (g_ref[...] * (1.0 + mod_ref[1:2, :])) + mod_ref[0:1, :]
        h = h.astype(bf16)
        for c in range(PROJ_W // N_CHUNK):
            sl = slice(c * N_CHUNK, (c + 1) * N_CHUNK)
            proj_ref[rows, sl] = _mm(h, w_ref[:, sl]).astype(bf16)
        for c in range(n_gl_chunks):
            sl = slice(c * N_CHUNK, (c + 1) * N_CHUNK)
            gl_ref[rows, sl] = _mm(h, w_ref[:, PROJ_W + c * N_CHUNK:PROJ_W + (c + 1) * N_CHUNK]).astype(bf16)


def _mod_spec(n_lat, D):
    return pl.BlockSpec((None, None, 6, D), lambda b, i: (b, jnp.where(i >= n_lat, 1, 0), 0, 0))


def _in_proj(x, moe, modp, mod, g1, w, n_lat):
    B, T, D = x.shape
    nt = pl.cdiv(T, TT)
    has_moe = moe is not None
    n_gl_chunks = N_BRANCH * D // N_CHUNK
    tok = pl.BlockSpec((None, TT, D), lambda b, i: (b, i, 0))
    in_specs = [tok]
    args = [x]
    if has_moe:
        in_specs += [tok, _mod_spec(n_lat, D)]
        args += [moe, modp]
    in_specs += [
        _mod_spec(n_lat, D),
        pl.BlockSpec((1, D), lambda b, i: (0, 0)),
        pl.BlockSpec(w.shape, lambda b, i: (0, 0), pipeline_mode=pl.Buffered(1)),
    ]
    args += [mod, g1, w]
    out_specs = [
        pl.BlockSpec((None, TT, PROJ_W), lambda b, i: (b, i, 0)),
        pl.BlockSpec((None, TT, N_BRANCH * D), lambda b, i: (b, i, 0)),
    ]
    out_shape = [
        jax.ShapeDtypeStruct((B, T, PROJ_W), bf16),
        jax.ShapeDtypeStruct((B, T, N_BRANCH * D), bf16),
    ]
    if has_moe:
        out_specs = [tok] + out_specs
        out_shape = [jax.ShapeDtypeStruct((B, T, D), f32)] + out_shape
    res = pl.pallas_call(
        functools.partial(_in_proj_kernel, has_moe, n_gl_chunks),
        grid=(B, nt),
        in_specs=in_specs,
        out_specs=out_specs,
        out_shape=out_shape,
        compiler_params=_params("arbitrary", "arbitrary"),
        name="in_proj",
    )(*args)
    if has_moe:
        return res
    return [x] + list(res)


def _prep_kernel(proj_ref, cos_ref, sin_ref, perm_ref, gmat_ref, naq_g, nak_g, swq_g, swk_g,
                 qn_ref, wuq_ref, qg_ref, kvn_ref, wuk_ref, wuv_ref, kg_ref,
                 qka_ref, mq_ref, mk_ref, mv_ref):
    cos = cos_ref[...]
    sin = sin_ref[...]
    perm = perm_ref[...]

    def rope(x):
        hi = x.astype(bf16)
        lo = (x - hi.astype(f32)).astype(bf16)
        partner = _mm(hi, perm) + _mm(lo, perm)
        return x * cos + partner * sin

    def headnorm(x, gain):
        w = x.shape[1]
        ms = _mm((x * x).astype(bf16), gmat_ref[:w, :w])
        return x * lax.rsqrt(ms + EPS) * gain

    def rmsnorm(x, gain):
        ms = jnp.mean(x * x, axis=-1, keepdims=True)
        return x * lax.rsqrt(ms + EPS) * gain

    na_scale = NA_DIM ** -0.5 * LOG2E
    naq = headnorm(proj_ref[:, P_NAQ:P_NAQ + NA_W].astype(f32), naq_g[...])
    qka_ref[:, A_NAQ:A_NAQ + NA_W] = (naq * na_scale).astype(bf16)
    nak = headnorm(proj_ref[:, P_NAK:P_NAK + NA_W].astype(f32), nak_g[...])
    qka_ref[:, A_NAK:A_NAK + NA_W] = nak.astype(bf16)

    sw_scale = SWA_DIM ** -0.5 * LOG2E
    swq = headnorm(proj_ref[:, P_SWQ:P_SWQ + SWA_W].astype(f32), swq_g[...])
    for p in range(SWA_W // LANES):
        sl = slice(p * LANES, (p + 1) * LANES)
        qka_ref[:, A_SWQ + p * LANES:A_SWQ + (p + 1) * LANES] = (rope(swq[:, sl]) * sw_scale).astype(bf16)
    swk = headnorm(proj_ref[:, P_SWK:P_SWK + SWA_KV_W].astype(f32), swk_g[...])
    qka_ref[:, A_SWK:A_SWK + SWA_KV_W] = rope(swk).astype(bf16)

    mla_scale = MLA_QK ** -0.5 * LOG2E
    cq = rmsnorm(proj_ref[:, P_CQ:P_CQ + MLA_Q_RANK].astype(f32), qn_ref[...]).astype(bf16)
    q = _mm(cq, wuq_ref[...])
    ckv = rmsnorm(proj_ref[:, P_CKV:P_CKV + MLA_KV_RANK].astype(f32), kvn_ref[...]).astype(bf16)
    kn = _mm(ckv, wuk_ref[...])
    v = _mm(ckv, wuv_ref[...]).astype(bf16)
    ones_col = jnp.where(lax.broadcasted_iota(jnp.int32, (v.shape[0], LANES), 1) == 0, 1.0, 0.0).astype(bf16)
    for h in range(MLA_HEADS):
        mv_ref[:, h * MLA_PAD:h * MLA_PAD + MLA_V] = v[:, h * MLA_V:(h + 1) * MLA_V]
        mv_ref[:, h * MLA_PAD + MLA_V:(h + 1) * MLA_PAD] = ones_col
    kr = proj_ref[:, P_KR:P_KR + LANES].astype(f32)
    kr_ss = jnp.sum(kr * kr, axis=-1, keepdims=True)
    for h in range(MLA_HEADS):
        o = h * MLA_PAD
        qh = q[:, o:o + MLA_PAD]
        r = lax.rsqrt(jnp.sum(qh * qh, axis=-1, keepdims=True) * (1.0 / MLA_QK) + EPS)
        qh = qh * r * qg_ref[:, o:o + MLA_PAD]
        mq_ref[:, o:o + LANES] = (qh[:, :LANES] * mla_scale).astype(bf16)
        mq_ref[:, o + LANES:o + MLA_PAD] = (rope(qh[:, LANES:]) * mla_scale).astype(bf16)
        kh = kn[:, h * MLA_NOPE:(h + 1) * MLA_NOPE]
        r = lax.rsqrt((jnp.sum(kh * kh, axis=-1, keepdims=True) + kr_ss) * (1.0 / MLA_QK) + EPS)
        mk_ref[:, o:o + LANES] = (kh * r * kg_ref[:, o:o + LANES]).astype(bf16)
        mk_ref[:, o + LANES:o + MLA_PAD] = rope(kr * r * kg_ref[:, o + LANES:o + MLA_PAD]).astype(bf16)


def _prep(proj, cos, sin, gmat, lw):
    B, T, _ = proj.shape
    nt = pl.cdiv(T, TT)

    def const(a):
        return pl.BlockSpec(a.shape, lambda b, i: (0,) * a.ndim)

    lane = np.arange(LANES)
    perm = jnp.asarray(lane[:, None] == (lane[None, :] ^ 16), bf16)
    consts = [perm, gmat, lw["naq_g"], lw["nak_g"], lw["swq_g"], lw["swk_g"], lw["qn"], lw["wuq"], lw["qg"],
              lw["kvn"], lw["wuk"], lw["wuv"], lw["kg"]]
    widths = (QKA_W, MLA_HEADS * MLA_PAD, MLA_HEADS * MLA_PAD, MLA_HEADS * MLA_PAD)
    return pl.pallas_call(
        _prep_kernel,
        grid=(B, nt),
        in_specs=[
            pl.BlockSpec((None, TT, PROJ_W), lambda b, i: (b, i, 0)),
            pl.BlockSpec((TT, LANES), lambda b, i: (i, 0)),
            pl.BlockSpec((TT, LANES), lambda b, i: (i, 0)),
        ] + [const(a) for a in consts],
        out_specs=[pl.BlockSpec((None, TT, w), lambda b, i: (b, i, 0)) for w in widths],
        out_shape=[jax.ShapeDtypeStruct((B, T, w), bf16) for w in widths],
        compiler_params=_params("arbitrary", "arbitrary"),
        name="prep",
    )(proj, cos, sin, *consts)


def _half_masks():
    lane = lax.broadcasted_iota(jnp.int32, (1, LANES), 1)
    lo = jnp.where(lane < 64, 1.0, 0.0).astype(bf16)
    return lo, (1.0 - lo.astype(f32)).astype(bf16)


def _pairs_attend(jobs, masks):
    scored = []
    for qp, parts, _ in jobs:
        q2 = jnp.concatenate([qp * masks[0], qp * masks[1]], axis=0)
        scores = []
        for k, _, bias in parts:
            s = _nt(q2, k)
            scores.append(s if bias is None else s + bias)
        scored.append(scores)
    normed = []
    for (qp, _, sinks), scores in zip(jobs, scored):
        n_q = qp.shape[0]
        m = functools.reduce(jnp.maximum, [jnp.max(s, axis=-1, keepdims=True) for s in scores])
        if sinks is not None:
            first = lax.broadcasted_iota(jnp.int32, (2 * n_q, 1), 0) < n_q
            sink = jnp.where(first, sinks[0], sinks[1])
            m = jnp.maximum(m, sink)
        ps = [jnp.exp2(s - m) for s in scores]
        l = functools.reduce(jnp.add, [jnp.sum(p, axis=-1, keepdims=True) for p in ps])
        if sinks is not None:
            l = l + jnp.exp2(sink - m)
        normed.append((ps, l))
    lane = lax.broadcasted_iota(jnp.int32, (1, LANES), 1)
    outs = []
    for (qp, parts, _), (ps, l) in zip(jobs, normed):
        n_q = qp.shape[0]
        o = functools.reduce(jnp.add, [_mm(p.astype(bf16), v) for p, (_, v, _) in zip(ps, parts)]) / l
        outs.append(jnp.where(lane < 64, o[:n_q], o[n_q:]))
    return outs


def _pair_attend(qp, parts, masks, sinks=None):
    return _pairs_attend([(qp, parts, sinks)], masks)[0]


def _na_kernel(n_lat, S, rows, offs, q_ref, k_ref, v_ref, tall_ref, rmask_ref, o_ref):
    i = pl.program_id(1)
    masks = _half_masks()
    T = k_ref.shape[0]

    def latent(ty):
        start = GRID_W * jnp.clip(NA_TILE_ROWS * i - NA_WIN_ROWS // 2, 0, rows - NA_BAND_ROWS)
        start = pl.multiple_of(start, GRID_W)
        jobs = []
        for p in range(NA_W // LANES):
            sl = slice(p * LANES, (p + 1) * LANES)
            slabs = []
            for head in (2 * p, 2 * p + 1):
                for qr in range(NA_TILE_ROWS):
                    odd = int(offs[ty, qr]) % 2
                    lo = (int(offs[ty, qr]) - odd) * GRID_W
                    slabs.append(tall_ref[head, odd, :, lo:lo + NA_BAND] + rmask_ref[ty, qr:qr + 1, :])
            band = (k_ref[pl.ds(start, NA_BAND), sl], v_ref[pl.ds(start, NA_BAND), sl],
                    jnp.concatenate(slabs, axis=0))
            ctx = (k_ref[S:T, sl], v_ref[S:T, sl], None)
            jobs.append((q_ref[:, sl], [band, ctx], None))
        for p, o in enumerate(_pairs_attend(jobs, masks)):
            o_ref[:, p * LANES:(p + 1) * LANES] = o.astype(bf16)

    pl.when(i == 0)(functools.partial(latent, 0))
    pl.when(jnp.logical_and(i > 0, i < n_lat - 1))(functools.partial(latent, 1))
    pl.when(i == n_lat - 1)(functools.partial(latent, 2))

    @pl.when(i >= n_lat)
    def _():
        for p in range(NA_W // LANES):
            sl = slice(p * LANES, (p + 1) * LANES)
            ctx = (k_ref[S:T, sl], v_ref[S:T, sl], None)
            o_ref[:, sl] = _pair_attend(q_ref[:, sl], [ctx], masks).astype(bf16)


def _na(qka, proj, rpb, S):
    B, T, _ = qka.shape
    nt, n_lat, rows = T // TM, S // TM, S // GRID_W
    assert n_lat >= 3
    offs, front, back, row_mask = _na_geometry(rows)
    tall = _na_bias(rpb, front, back)
    row_mask = jnp.asarray(row_mask)
    return pl.pallas_call(
        functools.partial(_na_kernel, n_lat, S, rows, offs),
        grid=(B, nt),
        in_specs=[
            pl.BlockSpec((None, TM, NA_W), lambda b, i: (b, i, A_NAQ // NA_W)),
            pl.BlockSpec((None, T, NA_W), lambda b, i: (b, 0, A_NAK // NA_W)),
            pl.BlockSpec((None, T, NA_W), lambda b, i: (b, 0, P_NAV // NA_W)),
            pl.BlockSpec(tall.shape, lambda b, i: (0, 0, 0, 0)),
            pl.BlockSpec(row_mask.shape, lambda b, i: (0, 0, 0)),
        ],
        out_specs=pl.BlockSpec((None, TM, NA_W), lambda b, i: (b, i, 0)),
        out_shape=jax.ShapeDtypeStruct((B, T, NA_W), bf16),
        compiler_params=_params("arbitrary", "arbitrary"),
        name="na_attn",
    )(qka, qka, proj, tall, row_mask)


def _na_geometry(rows):
    n_tiles = rows // NA_TILE_ROWS
    band = np.arange(NA_BAND_ROWS)
    dr0 = np.zeros((3, NA_TILE_ROWS), np.int64)
    valid = np.zeros((3, NA_TILE_ROWS, NA_BAND_ROWS), bool)
    for ty, rt in enumerate((0, 1, n_tiles - 1)):
        bs = int(np.clip(NA_TILE_ROWS * rt - NA_WIN_ROWS // 2, 0, rows - NA_BAND_ROWS))
        for qr in range(NA_TILE_ROWS):
            r = NA_TILE_ROWS * rt + qr
            s_r = int(np.clip(r - NA_WIN_ROWS // 2, 0, rows - NA_WIN_ROWS))
            dr0[ty, qr] = bs - r + NA_WIN_ROWS - 1
            valid[ty, qr] = (bs + band >= s_r) & (bs + band < s_r + NA_WIN_ROWS)
    front = int(max(0, -dr0.min()))
    back = int(max(0, dr0.max() + NA_BAND_ROWS - (2 * NA_WIN_ROWS - 1)))
    row_mask = np.where(np.repeat(valid, GRID_W, axis=2), 0.0, NEG).astype(np.float32)
    return dr0 + front, front, back, row_mask


def _na_bias(rpb, front, back):
    col = np.arange(GRID_W)
    c0 = np.clip(col - NA_WIN_COLS // 2, 0, GRID_W - NA_WIN_COLS)
    col_ok = (col[None, :] >= c0[:, None]) & (col[None, :] < c0[:, None] + NA_WIN_COLS)
    dc = np.clip(col[None, :] - col[:, None] + NA_WIN_COLS - 1, 0, 2 * NA_WIN_COLS - 2)
    pick = jnp.asarray(np.arange(2 * NA_WIN_COLS - 1)[:, None, None] == dc[None], f32)
    tz = jnp.einsum("hrd,dqk->hrqk", rpb, pick, precision=lax.Precision.HIGHEST)
    tz = jnp.where(col_ok, tz * LOG2E, NEG)
    n_blocks = front + tz.shape[1] + back
    width = pl.cdiv(n_blocks * GRID_W, LANES) * LANES
    tall = tz.transpose(0, 2, 1, 3).reshape(NA_HEADS, GRID_W, -1)
    tall = jnp.pad(tall, ((0, 0), (0, 0), (front * GRID_W, width + GRID_W - (front + tz.shape[1]) * GRID_W)),
                   constant_values=NEG)
    return jnp.stack([tall[:, :, :width], tall[:, :, GRID_W:GRID_W + width]], axis=1)


def _mla_kernel(n_q, S, q_ref, k_ref, v_ref, o_ref):
    i = pl.program_id(2)
    T = k_ref.shape[0]

    def attend(q, bounds):
        m_run = acc = None
        for lo, hi in bounds:
            s = _nt(q, k_ref[lo:hi, :])
            m_new = jnp.max(s, axis=-1, keepdims=True)
            if m_run is not None:
                m_new = jnp.maximum(m_run, m_new)
            pv = _mm(jnp.exp2(s - m_new).astype(bf16), v_ref[lo:hi, :])
            acc = pv if acc is None else acc * jnp.exp2(m_run - m_new) + pv
            m_run = m_new
        return (acc[:, :MLA_V] / acc[:, MLA_V:MLA_V + 1]).astype(bf16)

    @pl.when(i < n_q)
    def _():
        chunks = [(lo, lo + MLA_TK) for lo in range(0, S, MLA_TK)] + [(S, T)]
        o_ref[...] = attend(q_ref[...], chunks)

    @pl.when(i >= n_q)
    def _():
        o_ref[0:T - S, :] = attend(q_ref[0:T - S, :], [(S, T)])


def _mla(mq, mk, mv, S):
    B, T, _ = mq.shape
    tq = min(MLA_TQ, S)
    n_q = S // tq
    assert S % tq == 0 and S % MLA_TK == 0 and T - S <= tq
    return pl.pallas_call(
        functools.partial(_mla_kernel, n_q, S),
        grid=(B, MLA_HEADS, n_q + 1),
        in_specs=[
            pl.BlockSpec((None, tq, MLA_PAD), lambda b, h, i: (b, i, h)),
            pl.BlockSpec((None, T, MLA_PAD), lambda b, h, i: (b, 0, h)),
            pl.BlockSpec((None, T, MLA_PAD), lambda b, h, i: (b, 0, h)),
        ],
        out_specs=pl.BlockSpec((None, tq, MLA_V), lambda b, h, i: (b, i, h)),
        out_shape=jax.ShapeDtypeStruct((B, T, MLA_W), bf16),
        compiler_params=_params("arbitrary", "arbitrary", "arbitrary"),
        name="mla_attn",
    )(mq, mk, mv)


def _swa_kernel(n_lat, S, sink_ref, q_ref, k_ref, v_ref, band_ref, o_ref):
    i = pl.program_id(1)
    masks = _half_masks()
    T = k_ref.shape[0]
    kc, vc = k_ref[S:T, :], v_ref[S:T, :]

    def sinks(p):
        return (sink_ref[SWA_HEAD_ORDER[2 * p]], sink_ref[SWA_HEAD_ORDER[2 * p + 1]])

    @pl.when(i < n_lat)
    def _():
        start = pl.multiple_of(jnp.clip(i * TM - SWA_WINDOW, 0, S - SWA_KEYS), SWA_WINDOW)
        band = band_ref[jnp.where(i == 0, 0, jnp.where(i == n_lat - 1, 2, 1))]
        kb, vb = k_ref[pl.ds(start, SWA_KEYS), :], v_ref[pl.ds(start, SWA_KEYS), :]
        jobs = [(q_ref[:, p * LANES:(p + 1) * LANES], [(kb, vb, band), (kc, vc, None)], sinks(p))
                for p in range(SWA_W // LANES)]
        for p, o in enumerate(_pairs_attend(jobs, masks)):
            o_ref[:, p * LANES:(p + 1) * LANES] = o.astype(bf16)

    @pl.when(i >= n_lat)
    def _():
        for p in range(SWA_W // LANES):
            sl = slice(p * LANES, (p + 1) * LANES)
            o_ref[:, sl] = _pair_attend(q_ref[:, sl], [(kc, vc, None)], masks, sinks(p)).astype(bf16)


def _swa_band_masks(S):
    n_lat = S // TM
    out = np.zeros((3, 2 * TM, SWA_KEYS), np.float32)
    for ty, i in enumerate((0, 1, n_lat - 1)):
        start = int(np.clip(i * TM - SWA_WINDOW, 0, S - SWA_KEYS))
        qpos = i * TM + np.tile(np.arange(TM), 2)
        kpos = start + np.arange(SWA_KEYS)
        out[ty] = np.where(np.abs(qpos[:, None] - kpos[None, :]) <= SWA_WINDOW, 0.0, NEG)
    return jnp.asarray(out)


def _swa(qka, proj, sink, S):
    B, T, _ = qka.shape
    nt, n_lat = T // TM, S // TM
    assert n_lat >= 3
    band = _swa_band_masks(S)
    return pl.pallas_call(
        functools.partial(_swa_kernel, n_lat, S),
        grid=(B, nt),
        in_specs=[
            pl.BlockSpec(memory_space=pltpu.SMEM),
            pl.BlockSpec((None, TM, SWA_W), lambda b, i: (b, i, A_SWQ // SWA_W)),
            pl.BlockSpec((None, T, SWA_KV_W), lambda b, i: (b, 0, A_SWK // SWA_KV_W)),
            pl.BlockSpec((None, T, SWA_KV_W), lambda b, i: (b, 0, P_SWV // SWA_KV_W)),
            pl.BlockSpec(band.shape, lambda b, i: (0, 0, 0)),
        ],
        out_specs=pl.BlockSpec((None, TM, SWA_W), lambda b, i: (b, i, 0)),
        out_shape=jax.ShapeDtypeStruct((B, T, SWA_W), bf16),
        compiler_params=_params("arbitrary", "arbitrary"),
        name="swa_attn",
    )(sink, qka, qka, proj, band)


def _post_kernel(ona_ref, omla_ref, oswa_ref, gl_ref, x_ref, mod_ref, wna_ref, wmla_ref, wswa_ref, wo_ref,
                 g2_ref, rt_ref, xo_ref, h2_ref, aff_ref, affx_ref):
    D = x_ref.shape[1]
    n_e = aff_ref.shape[1]
    n_sub = 2
    sub = x_ref.shape[0] // n_sub
    halves = [slice(r * sub, (r + 1) * sub) for r in range(n_sub)]
    gated = []
    for rows in halves:
        merged = None
        for j, (o_ref, w_ref) in enumerate(((ona_ref, wna_ref), (omla_ref, wmla_ref), (oswa_ref, wswa_ref))):
            y = _mm(o_ref[rows, :], w_ref[...])
            gy = (jnp.tanh(gl_ref[rows, j * D:(j + 1) * D].astype(f32)) + 1.0) * y
            merged = gy if merged is None else merged + gy
        gated.append(merged.astype(bf16))
    mixed = [_mm(m, wo_ref[...]) for m in gated]
    for rows, res in zip(halves, mixed):
        x = x_ref[rows, :] + mod_ref[2:3, :] * res
        xo_ref[rows, :] = x
        ms = jnp.mean(x * x, axis=-1, keepdims=True)
        h2 = x * lax.rsqrt(ms + EPS) * (g2_ref[...] * (1.0 + mod_ref[4:5, :])) + mod_ref[3:4, :]
        h_hi = h2.astype(bf16)
        h2_ref[rows, :] = h_hi
        h_lo = (h2 - h_hi.astype(f32)).astype(bf16)
        parts = _mm(h_hi, rt_ref[...]) + _mm(h_lo, rt_ref[...])
        logits = parts[:, :n_e] + parts[:, n_e:]
        e = jnp.exp(logits - jnp.max(logits, axis=1, keepdims=True))
        aff = e / jnp.sum(e, axis=1, keepdims=True)
        aff_ref[rows, :] = aff
        p0 = aff.astype(bf16)
        r1 = aff - p0.astype(f32)
        p1 = r1.astype(bf16)
        p2 = (r1 - p1.astype(f32)).astype(bf16)
        pad = jnp.zeros((sub, LANES - 3 * n_e), bf16)
        affx_ref[rows, :] = jnp.concatenate([p0, p1, p2, pad], axis=1)


def _post(ona, omla, oswa, gl, x, mod, lw, n_lat):
    B, T, D = x.shape
    nt = pl.cdiv(T, TT)
    E = lw["router"].shape[1] // 2

    def tok(w):
        return pl.BlockSpec((None, TT, w), lambda b, i: (b, i, 0))

    def const(a):
        return pl.BlockSpec(a.shape, lambda b, i: (0,) * a.ndim)

    consts = [lw["wna"], lw["wmla"], lw["wswa"], lw["wo"], lw["g2"], lw["router"]]
    return pl.pallas_call(
        _post_kernel,
        grid=(B, nt),
        in_specs=[tok(NA_W), tok(MLA_W), tok(SWA_W), tok(N_BRANCH * D), tok(D), _mod_spec(n_lat, D)]
        + [const(a) for a in consts],
        out_specs=[tok(D), tok(D), tok(E), tok(LANES)],
        out_shape=[jax.ShapeDtypeStruct((B, T, D), f32), jax.ShapeDtypeStruct((B, T, D), bf16),
                   jax.ShapeDtypeStruct((B, T, E), f32), jax.ShapeDtypeStruct((B, T, LANES), bf16)],
        compiler_params=_params("arbitrary", "arbitrary"),
        name="post_attn",
    )(ona, omla, oswa, gl, x, mod, *consts)


def _lane_cumsum(mask, tri):
    E, n = mask.shape
    carry = jnp.zeros((E, 1), f32)
    outs = []
    for k in range(n // LANES):
        w = _mm(mask[:, k * LANES:(k + 1) * LANES].astype(bf16), tri) + carry
        outs.append(w)
        carry = w[:, LANES - 1:LANES]
    return jnp.concatenate(outs, axis=1)


def _select_slots(aff, cap, base, tri):
    bits = lax.bitcast_convert_type(aff, jnp.int32)
    thr = jnp.zeros((aff.shape[0], 1), jnp.int32)
    for bit in range(30, -1, -1):
        cand = thr | (1 << bit)
        cnt = jnp.sum(jnp.where(bits >= cand, 1.0, 0.0), axis=1, keepdims=True)
        thr = jnp.where(cnt >= cap, cand, thr)
    gt = jnp.where(bits > thr, 1.0, 0.0)
    eq = jnp.where(bits == thr, 1.0, 0.0)
    need = cap - jnp.sum(gt, axis=1, keepdims=True)
    sel = jnp.maximum(gt, jnp.where(_lane_cumsum(eq, tri) <= need, eq, 0.0))
    return jnp.where(sel > 0.0, _lane_cumsum(sel, tri) + base, 0.0)


def _topk_kernel(S, cap_s, cap_l, aff_ref, tri_ref, before_ref, cp_ref, bnd_ref):
    T = aff_ref.shape[1]
    tri = tri_ref[...]
    cp_ref[:, 0:S] = _select_slots(aff_ref[:, 0:S], cap_s, 0.0, tri)
    cp_ref[:, S:T] = _select_slots(aff_ref[:, S:T], cap_l, float(cap_s), tri)
    bnd_ref[...] = _mm(jnp.where(cp_ref[...] > 0.0, 1.0, 0.0).astype(bf16), before_ref[...])


def _topk(aff, S, cap_s, cap_l):
    B, E, T = aff.shape
    tri = jnp.asarray(np.triu(np.ones((LANES, LANES), np.float32)), bf16)
    before = jnp.asarray(np.arange(T)[:, None] < MOE_CHUNK * np.arange(LANES)[None, :], bf16)
    return pl.pallas_call(
        functools.partial(_topk_kernel, S, cap_s, cap_l),
        grid=(B,),
        in_specs=[pl.BlockSpec((None, E, T), lambda b: (b, 0, 0)),
                  pl.BlockSpec((LANES, LANES), lambda b: (0, 0)),
                  pl.BlockSpec((T, LANES), lambda b: (0, 0))],
        out_specs=[pl.BlockSpec((None, E, T), lambda b: (b, 0, 0)),
                   pl.BlockSpec((None, E, LANES), lambda b: (b, 0, 0))],
        out_shape=[jax.ShapeDtypeStruct((B, E, T), f32), jax.ShapeDtypeStruct((B, E, LANES), f32)],
        compiler_params=_params("arbitrary"),
        name="expert_select",
    )(aff, tri, before)


def _window_count(lo, hi, r0):
    return jnp.where(hi > lo, lax.div(hi - r0 + (MOE_WIN - 1), MOE_WIN), 0)


def _gather_kernel(n_e, rows, bnd_ref, h2_ref, affx_ref, cp_ref, xg_ref):
    b, k = pl.program_id(0), pl.program_id(1)
    D = h2_ref.shape[1]

    @pl.when(k == 0)
    def _():
        xg_ref[...] = jnp.zeros_like(xg_ref)

    ex = lax.broadcasted_iota(jnp.int32, (n_e, 1), 0)
    starts = []
    start_col = jnp.zeros((n_e, 1), f32)
    for e in range(n_e):
        r0 = lax.div(bnd_ref[(b * n_e + e) * LANES + k], 16) * 16
        starts.append(r0)
        start_col = start_col + jnp.where(ex == e, r0.astype(f32), 0.0)
    rel = cp_ref[...] - start_col
    rel_rows = jnp.broadcast_to(rel[:, None, :], (n_e, MOE_WIN, MOE_CHUNK)).reshape(n_e * MOE_WIN, MOE_CHUNK)
    j1 = lax.broadcasted_iota(jnp.int32, (n_e, MOE_WIN, 1), 1).reshape(n_e * MOE_WIN, 1) + 1
    onehot = jnp.where(rel_rows == j1.astype(f32), 1.0, 0.0).astype(bf16)
    tok, tok_aff = h2_ref[...], affx_ref[...]
    moved = _mm(onehot, tok).astype(bf16)
    moved_aff = _mm(onehot, tok_aff).astype(bf16)
    for e in range(n_e):
        dst = pl.ds(pl.multiple_of(e * rows + starts[e], 16), MOE_WIN)
        xg_ref[dst, 0:D] += moved[e * MOE_WIN:(e + 1) * MOE_WIN]
        xg_ref[dst, D:D + LANES] += moved_aff[e * MOE_WIN:(e + 1) * MOE_WIN]

    sub1 = lax.broadcasted_iota(jnp.int32, (MOE_WIN, 1), 0) + 1
    for e in range(n_e):
        lo, hi = bnd_ref[(b * n_e + e) * LANES + k], bnd_ref[(b * n_e + e) * LANES + k + 1]
        n_win = _window_count(lo, hi, starts[e])

        @pl.when(n_win > 1)
        def _(e=e, n_win=n_win):
            def extra(w, c):
                hit = rel[e:e + 1, :] == (sub1 + w * MOE_WIN).astype(f32)
                oh = jnp.where(hit, 1.0, 0.0).astype(bf16)
                dst = pl.ds(pl.multiple_of(e * rows + starts[e] + w * MOE_WIN, 16), MOE_WIN)
                xg_ref[dst, 0:D] += _mm(oh, tok).astype(bf16)
                xg_ref[dst, D:D + LANES] += _mm(oh, tok_aff).astype(bf16)
                return c

            lax.fori_loop(1, n_win, extra, 0)


def _gather(h2, affx, cp, bnd, n_slots):
    B, T, D = h2.shape
    E = cp.shape[1]
    nc = T // MOE_CHUNK
    rows = n_slots + MOE_WIN
    assert n_slots % 16 == 0 and nc + 1 <= LANES and 3 * E <= LANES
    return pl.pallas_call(
        functools.partial(_gather_kernel, E, rows),
        grid_spec=pltpu.PrefetchScalarGridSpec(
            num_scalar_prefetch=1,
            grid=(B, nc),
            in_specs=[
                pl.BlockSpec((None, MOE_CHUNK, D), lambda b, k, s: (b, k, 0)),
                pl.BlockSpec((None, MOE_CHUNK, LANES), lambda b, k, s: (b, k, 0)),
                pl.BlockSpec((None, E, MOE_CHUNK), lambda b, k, s: (b, 0, k)),
            ],
            out_specs=pl.BlockSpec((None, E * rows, D + LANES), lambda b, k, s: (b, 0, 0),
                                   pipeline_mode=pl.Buffered(1)),
        ),
        out_shape=jax.ShapeDtypeStruct((B, E * rows, D + LANES), bf16),
        compiler_params=_params("arbitrary", "arbitrary"),
        name="moe_gather",
    )(bnd, h2, affx, cp)


def _ffn_kernel(n_slots, n_e, xg_ref, wg_ref, wu_ref, wd_ref, y_ref, wg_s, wu_s, wd_s):
    e, b = pl.program_id(0), pl.program_id(1)
    D = wg_ref.shape[0]

    @pl.when(b == 0)
    def _():
        wg_s[...] = wg_ref[...].astype(bf16)
        wu_s[...] = wu_ref[...].astype(bf16)
        wd_s[...] = wd_ref[...].astype(bf16)

    n_b = xg_ref.shape[0]
    xg = jnp.concatenate([xg_ref[s, 0:n_slots, 0:D] for s in range(n_b)], axis=0)
    a = _mm(xg, wg_s[...])
    u = _mm(xg, wu_s[...])
    act = (a * jax.nn.sigmoid(a) * u).astype(bf16)
    lane = lax.broadcasted_iota(jnp.int32, (1, LANES), 1)
    mine = jnp.logical_and(lax.rem(lane, n_e) == e, lane < 3 * n_e)
    parts = jnp.concatenate([xg_ref[s, 0:n_slots, D:D + LANES] for s in range(n_b)], axis=0).astype(f32)
    gate = jnp.sum(jnp.where(mine, parts, 0.0), axis=1, keepdims=True)
    y = (_mm(act, wd_s[...]) * gate).astype(bf16)
    for s in range(n_b):
        y_ref[s, 0:n_slots, :] = y[s * n_slots:(s + 1) * n_slots]
        y_ref[s, n_slots:, :] = jnp.zeros((y_ref.shape[1] - n_slots, y_ref.shape[2]), bf16)


def _ffn(xg, w_gate, w_up, w_down, layer, n_slots):
    B = xg.shape[0]
    _, E, D, F = w_gate.shape
    rows = xg.shape[1] // E
    n_b = 2 if B % 2 == 0 else 1
    return pl.pallas_call(
        functools.partial(_ffn_kernel, n_slots, E),
        grid=(E, B // n_b),
        in_specs=[
            pl.BlockSpec((n_b, None, rows, D + LANES), lambda e, b: (b, e, 0, 0)),
            pl.BlockSpec((None, None, D, F), lambda e, b: (layer, e, 0, 0)),
            pl.BlockSpec((None, None, D, F), lambda e, b: (layer, e, 0, 0)),
            pl.BlockSpec((None, None, F, D), lambda e, b: (layer, e, 0, 0)),
        ],
        out_specs=pl.BlockSpec((n_b, None, rows, D), lambda e, b: (b, e, 0, 0)),
        out_shape=jax.ShapeDtypeStruct((B, E, rows, D), bf16),
        scratch_shapes=[pltpu.VMEM((D, F), bf16), pltpu.VMEM((D, F), bf16), pltpu.VMEM((F, D), bf16)],
        compiler_params=_params("arbitrary", "arbitrary"),
        name="moe_ffn",
    )(xg.reshape(B, E, rows, D + LANES), w_gate, w_up, w_down)


def _combine_kernel(n_e, rows, residual, bnd_ref, y_ref, cpt_ref, expand_ref, *refs):
    o_ref = refs[-1]
    b, k = pl.program_id(0), pl.program_id(1)
    cpt = cpt_ref[...]
    hi_part = jnp.floor(cpt * (1.0 / 32.0))
    lo_part = cpt - 32.0 * hi_part
    expand = expand_ref[...]
    rank = 32.0 * _mm(hi_part.astype(bf16), expand) + _mm(lo_part.astype(bf16), expand)
    lane = lax.broadcasted_iota(jnp.int32, (1, n_e * MOE_WIN), 1)
    lane_e = lax.div(lane, MOE_WIN)
    starts = []
    tgt = (lane - lane_e * MOE_WIN + 1).astype(f32)
    for e in range(n_e):
        r0 = lax.div(bnd_ref[(b * n_e + e) * LANES + k], 16) * 16
        starts.append(r0)
        tgt = tgt + jnp.where(lane_e == e, r0.astype(f32), 0.0)
    onehot = jnp.where(rank == tgt, 1.0, 0.0).astype(bf16)
    ycat = jnp.concatenate(
        [y_ref[pl.ds(pl.multiple_of(e * rows + starts[e], 16), MOE_WIN), :] for e in range(n_e)], axis=0)
    o_ref[...] = _mm(onehot, ycat)

    lane_w = lax.broadcasted_iota(jnp.int32, (1, MOE_WIN), 1)
    for e in range(n_e):
        lo, hi = bnd_ref[(b * n_e + e) * LANES + k], bnd_ref[(b * n_e + e) * LANES + k + 1]
        n_win = _window_count(lo, hi, starts[e])

        @pl.when(n_win > 1)
        def _(e=e, n_win=n_win):
            rank_e = rank[:, e * MOE_WIN:(e + 1) * MOE_WIN]

            def extra(w, c):
                r = pl.multiple_of(starts[e] + w * MOE_WIN, 16)
                hit = rank_e == (lane_w + (r + 1)).astype(f32)
                o_ref[...] += _mm(jnp.where(hit, 1.0, 0.0).astype(bf16),
                                  y_ref[pl.ds(pl.multiple_of(e * rows + r, 16), MOE_WIN), :])
                return c

            lax.fori_loop(1, n_win, extra, 0)

    if residual:
        x_ref, mod_ref = refs[0], refs[1]
        o_ref[...] = x_ref[...] + mod_ref[5:6, :] * o_ref[...]


def _combine(y, cp, bnd, x=None, mod=None, n_tok=None):
    B, E, rows, D = y.shape
    residual = x is not None
    n_tok = n_tok if residual else cp.shape[2]
    expand = jnp.asarray(np.kron(np.eye(E), np.ones((1, MOE_WIN))), bf16)
    chunk = pl.BlockSpec((None, MOE_CHUNK, D), lambda b, k, s: (b, k, 0))
    in_specs = [
        pl.BlockSpec((None, E * rows, D), lambda b, k, s: (b, 0, 0), pipeline_mode=pl.Buffered(1)),
        pl.BlockSpec((None, MOE_CHUNK, E), lambda b, k, s: (b, k, 0)),
        pl.BlockSpec(expand.shape, lambda b, k, s: (0, 0)),
    ]
    args = [bnd, y.reshape(B, E * rows, D), cp.transpose(0, 2, 1), expand]
    if residual:
        in_specs += [chunk, pl.BlockSpec((None, None, 6, D), lambda b, k, s: (b, 0, 0, 0))]
        args += [x, mod]
    return pl.pallas_call(
        functools.partial(_combine_kernel, E, rows, residual),
        grid_spec=pltpu.PrefetchScalarGridSpec(
            num_scalar_prefetch=1,
            grid=(B, n_tok // MOE_CHUNK),
            in_specs=in_specs,
            out_specs=chunk,
        ),
        out_shape=jax.ShapeDtypeStruct((B, n_tok, D), f32),
        compiler_params=_params("arbitrary", "arbitrary"),
        name="moe_combine",
    )(*args)


def _rope_tables(S, L):
    t = jnp.arange(S)
    row, col = t // GRID_W, t % GRID_W
    f = 16
    inv = ROPE_BASE ** (-jnp.arange(f, dtype=f32) / f)
    ar = row.astype(f32)[:, None] * inv
    ac = col.astype(f32)[:, None] * inv
    cos = jnp.concatenate([jnp.cos(ar), jnp.cos(ar), jnp.cos(ac), jnp.cos(ac)], axis=1)
    sin = jnp.concatenate([-jnp.sin(ar), jnp.sin(ar), -jnp.sin(ac), jnp.sin(ac)], axis=1)
    cos = jnp.concatenate([jnp.tile(cos, (1, 2)), jnp.ones((L, LANES), f32)], axis=0)
    sin = jnp.concatenate([jnp.tile(sin, (1, 2)), jnp.zeros((L, LANES), f32)], axis=0)
    return cos, sin


def _pad_heads(w, n_heads, width, padded):
    lead = w.shape[:-1]
    w = w.reshape(lead + (n_heads, width))
    w = jnp.pad(w, [(0, 0)] * len(lead) + [(0, 0), (0, padded - width)])
    return w.reshape(lead + (n_heads * padded,))


def _layer_weights(l, D, w_in, norm2_g, na_q_g, na_k_g, mla_q_norm, mla_w_uq, mla_kv_norm, mla_w_ukv,
                   mla_q_g, mla_k_g, swa_q_g, swa_k_g, w_na_o, w_mla_o, w_swa_o, w_o, router):
    o = np.cumsum((0, NA_W, NA_W, NA_W, MLA_Q_RANK, MLA_KV_RANK, MLA_ROPE, SWA_W, SWA_KV_W, SWA_KV_W))
    o_naq, o_nak, o_nav, o_cq, o_ckv, o_kr, o_swq, o_swk, o_swv, o_gl = (int(v) for v in o)
    wi = w_in[l]
    swq = wi[:, o_swq:o_swq + SWA_W].reshape(D, SWA_HEADS, SWA_DIM)[:, SWA_HEAD_ORDER, :].reshape(D, SWA_W)
    kr = jnp.pad(wi[:, o_kr:o_kr + MLA_ROPE], ((0, 0), (0, LANES - MLA_ROPE)))
    w_all = jnp.concatenate([
        wi[:, o_naq:o_naq + 3 * NA_W], swq, wi[:, o_cq:o_cq + MLA_Q_RANK], wi[:, o_ckv:o_ckv + MLA_KV_RANK], kr,
        wi[:, o_swk:o_swk + 2 * SWA_KV_W], 0.5 * wi[:, o_gl:]], axis=1).astype(bf16)
    ukv = mla_w_ukv[l].reshape(MLA_KV_RANK, MLA_HEADS, MLA_NOPE + MLA_V)
    swa_o = w_swa_o[l].reshape(SWA_HEADS, SWA_DIM, D)[SWA_HEAD_ORDER, :, :].reshape(SWA_W, D)
    r_hi = router[l].astype(bf16)
    r_lo = (router[l] - r_hi.astype(f32)).astype(bf16)
    return dict(
        w_all=w_all,
        naq_g=jnp.tile(na_q_g[l], NA_HEADS)[None], nak_g=jnp.tile(na_k_g[l], NA_HEADS)[None],
        swq_g=jnp.tile(swa_q_g[l], SWA_HEADS)[None], swk_g=jnp.tile(swa_k_g[l], SWA_KV_HEADS)[None],
        qn=mla_q_norm[l][None], kvn=mla_kv_norm[l][None],
        wuq=_pad_heads(mla_w_uq[l], MLA_HEADS, MLA_QK, MLA_PAD).astype(bf16),
        wuk=ukv[:, :, :MLA_NOPE].reshape(MLA_KV_RANK, MLA_HEADS * MLA_NOPE).astype(bf16),
        wuv=ukv[:, :, MLA_NOPE:].reshape(MLA_KV_RANK, MLA_W).astype(bf16),
        qg=jnp.tile(jnp.pad(mla_q_g[l], (0, MLA_PAD - MLA_QK)), MLA_HEADS)[None],
        kg=jnp.tile(jnp.pad(mla_k_g[l], (0, MLA_PAD - MLA_QK)), MLA_HEADS)[None],
        wna=w_na_o[l].astype(bf16), wmla=w_mla_o[l].astype(bf16), wswa=swa_o.astype(bf16),
        wo=(0.5 * w_o[l]).astype(bf16), g2=norm2_g[l][None], router=jnp.concatenate([r_hi, r_lo], axis=1),
    )


def kernel(x, c, ctx, c_ctx, norm1_g, norm2_g, w_ada, b_ada, w_in, na_q_g, na_k_g, na_rpb, mla_q_norm, mla_w_uq, mla_kv_norm, mla_w_ukv, mla_q_g, mla_k_g, swa_q_g, swa_k_g, swa_sink, w_na_o, w_mla_o, w_swa_o, w_o, router, w_gate, w_up, w_down):
    B, S, D = x.shape
    L = ctx.shape[1]
    depth = w_in.shape[0]
    E = router.shape[2]
    T = S + L
    n_lat = S // TT
    rows = S // GRID_W
    assert S % TT == 0 and L % TM == 0 and L <= TT and rows >= NA_BAND_ROWS and S >= SWA_KEYS and B < 16
    cap_s = CAPACITY_FACTOR * S // E
    cap_l = CAPACITY_FACTOR * L // E

    cpad = jnp.zeros((16, D), f32).at[:B].set(c).at[B].set(c_ctx)
    mod = _ada(cpad, w_ada, b_ada).reshape(depth, 16, 6, D)
    mod = jnp.stack([mod[:, :B], jnp.broadcast_to(mod[:, B:B + 1], (depth, B, 6, D))], axis=2)

    cos, sin = _rope_tables(S, L)
    gmat = jnp.asarray(np.kron(np.eye(NA_HEADS), np.full((NA_DIM, NA_DIM), 1.0 / NA_DIM)), bf16)

    xs = jnp.concatenate([x, ctx], axis=1)
    moe = None
    for l in range(depth):
        lw = _layer_weights(l, D, w_in, norm2_g, na_q_g, na_k_g, mla_q_norm, mla_w_uq, mla_kv_norm, mla_w_ukv,
                            mla_q_g, mla_k_g, swa_q_g, swa_k_g, w_na_o, w_mla_o, w_swa_o, w_o, router)
        modp = mod[l - 1] if l > 0 else None
        xs, proj, gl = _in_proj(xs, moe, modp, mod[l], norm1_g[l][None], lw["w_all"], n_lat)
        qka, mq, mk, mv = _prep(proj, cos, sin, gmat, lw)
        ona = _na(qka, proj, na_rpb[l], S)
        omla = _mla(mq, mk, mv, S)
        oswa = _swa(qka, proj, swa_sink[l] * LOG2E, S)
        xs, h2, aff, affx = _post(ona, omla, oswa, gl, xs, mod[l], lw, n_lat)
        cp, bnd = _topk(aff.transpose(0, 2, 1), S, cap_s, cap_l)
        bnd = bnd.astype(jnp.int32).reshape(-1)
        xg = _gather(h2, affx, cp, bnd, cap_s + cap_l)
        y = _ffn(xg, w_gate, w_up, w_down, l, cap_s + cap_l)
        if l == depth - 1:
            return _combine(y, cp, bnd, xs, mod[l], S)
        moe = _combine(y, cp, bnd)
```

```python
import functools

import numpy as np
import jax
import jax.numpy as jnp
from jax import lax
from jax.experimental import pallas as pl
from jax.experimental.pallas import tpu as pltpu

GRID_W = 64
NA_HEADS, NA_DIM, NA_WIN_ROWS, NA_WIN_COLS = 6, 64, 8, 16
MLA_HEADS, MLA_Q_RANK, MLA_KV_RANK, MLA_NOPE, MLA_ROPE, MLA_V = 4, 256, 128, 128, 64, 128
MLA_QK = MLA_NOPE + MLA_ROPE
MLA_PAD = 256
SWA_HEADS, SWA_KV_HEADS, SWA_DIM, SWA_WINDOW = 6, 2, 64, 128
CAPACITY_FACTOR = 2
N_BRANCH = 3
ROPE_BASE = 10000.0
EPS = 1e-6

NA_W = NA_HEADS * NA_DIM
SWA_W = SWA_HEADS * SWA_DIM
SWA_KV_W = SWA_KV_HEADS * SWA_DIM
MLA_W = MLA_HEADS * MLA_V

LANES = 128
TM = 256
TT = 512
MLA_TQ = 1024
MLA_TK = 512
LOG2E = 1.4426950408889634
NA_TILE_ROWS = TM // GRID_W
NA_BAND_ROWS = NA_TILE_ROWS + NA_WIN_ROWS - 1
NA_BAND = NA_BAND_ROWS * GRID_W
SWA_KEYS = TM + 2 * SWA_WINDOW
MOE_CHUNK = 256
MOE_WIN = 64
NEG = -1e30
VMEM_LIMIT = 56 * 1024 * 1024

P_NAQ, P_NAK, P_NAV, P_SWQ = 0, 384, 768, 1152
P_CQ, P_CKV, P_KR, P_SWK, P_SWV = 1536, 1792, 1920, 2048, 2176
PROJ_W = 2304
N_CHUNK = 768
A_NAQ, A_NAK, A_SWQ, A_SWK = 0, 384, 768, 1152
QKA_W = 1280
SWA_HEAD_ORDER = (0, 3, 1, 4, 2, 5)

bf16 = jnp.bfloat16
f32 = jnp.float32


def _mm(a, b):
    return jnp.dot(a, b, preferred_element_type=f32)


def _nt(a, b):
    return lax.dot_general(a, b, (((1,), (1,)), ((), ())), preferred_element_type=f32)


def _params(*sem):
    return pltpu.CompilerParams(dimension_semantics=sem, vmem_limit_bytes=VMEM_LIMIT)


def _ada_kernel(c_ref, w_ref, b_ref, o_ref):
    a = c_ref[...]
    a = (a * jax.nn.sigmoid(a)).astype(bf16)
    o_ref[...] = _mm(a, w_ref[...].astype(bf16)) + b_ref[...]


def _ada(cpad, w_ada, b_ada):
    L, D, N = w_ada.shape
    tn = 1536
    return pl.pallas_call(
        _ada_kernel,
        grid=(L, N // tn),
        in_specs=[
            pl.BlockSpec((16, D), lambda l, j: (0, 0)),
            pl.BlockSpec((None, D, tn), lambda l, j: (l, 0, j)),
            pl.BlockSpec((None, 1, tn), lambda l, j: (l, 0, j)),
        ],
        out_specs=pl.BlockSpec((None, 16, tn), lambda l, j: (l, 0, j)),
        out_shape=jax.ShapeDtypeStruct((L, 16, N), f32),
        compiler_params=_params("arbitrary", "arbitrary"),
        name="ada",
    )(cpad, w_ada, b_ada.reshape(L, 1, N))


def _in_proj_kernel(has_moe, n_gl_chunks, *refs):
    if has_moe:
        x_ref, moe_ref, modp_ref, mod_ref, g_ref, w_ref, xo_ref, proj_ref, gl_ref = refs
    else:
        x_ref, mod_ref, g_ref, w_ref, proj_ref, gl_ref = refs
    n_sub = 2
    sub = x_ref.shape[0] // n_sub
    for r in range(n_sub):
        rows = slice(r * sub, (r + 1) * sub)
        x = x_ref[rows, :]
        if has_moe:
            x = x + modp_ref[5:6, :] * moe_ref[rows, :]
            xo_ref[rows, :] = x
        ms = jnp.mean(x * x, axis=-1, keepdims=True)
        h = x * lax.rsqrt(ms + EPS) * (g_ref[...] * (1.0 + mod_ref[1:2, :])) + mod_ref[0:1, :]
        h = h.astype(bf16)
        for c in range(PROJ_W // N_CHUNK):
            sl = slice(c * N_CHUNK, (c + 1) * N_CHUNK)
            proj_ref[rows, sl] = _mm(h, w_ref[:, sl]).astype(bf16)
        for c in range(n_gl_chunks):
            sl = slice(c * N_CHUNK, (c + 1) * N_CHUNK)
            gl_ref[rows, sl] = _mm(h, w_ref[:, PROJ_W + c * N_CHUNK:PROJ_W + (c + 1) * N_CHUNK]).astype(bf16)


def _mod_spec(n_lat, D):
    return pl.BlockSpec((None, None, 6, D), lambda b, i: (b, jnp.where(i >= n_lat, 1, 0), 0, 0))


def _in_proj(x, moe, modp, mod, g1, w, n_lat):
    B, T, D = x.shape
    nt = pl.cdiv(T, TT)
    has_moe = moe is not None
    n_gl_chunks = N_BRANCH * D // N_CHUNK
    tok = pl.BlockSpec((None, TT, D), lambda b, i: (b, i, 0))
    in_specs = [tok]
    args = [x]
    if has_moe:
        in_specs += [tok, _mod_spec(n_lat, D)]
        args += [moe, modp]
    in_specs += [
        _mod_spec(n_lat, D),
        pl.BlockSpec((1, D), lambda b, i: (0, 0)),
        pl.BlockSpec(w.shape, lambda b, i: (0, 0), pipeline_mode=pl.Buffered(1)),
    ]
    args += [mod, g1, w]
    out_specs = [
        pl.BlockSpec((None, TT, PROJ_W), lambda b, i: (b, i, 0)),
        pl.BlockSpec((None, TT, N_BRANCH * D), lambda b, i: (b, i, 0)),
    ]
    out_shape = [
        jax.ShapeDtypeStruct((B, T, PROJ_W), bf16),
        jax.ShapeDtypeStruct((B, T, N_BRANCH * D), bf16),
    ]
    if has_moe:
        out_specs = [tok] + out_specs
        out_shape = [jax.ShapeDtypeStruct((B, T, D), f32)] + out_shape
    res = pl.pallas_call(
        functools.partial(_in_proj_kernel, has_moe, n_gl_chunks),
        grid=(B, nt),
        in_specs=in_specs,
        out_specs=out_specs,
        out_shape=out_shape,
        compiler_params=_params("arbitrary", "arbitrary"),
        name="in_proj",
    )(*args)
    if has_moe:
        return res
    return [x] + list(res)


def _prep_kernel(proj_ref, cos_ref, sin_ref, perm_ref, gmat_ref, naq_g, nak_g, swq_g, swk_g,
                 qn_ref, wuq_ref, qg_ref, kvn_ref, wuk_ref, wuv_ref, kg_ref,
                 qka_ref, mq_ref, mk_ref, mv_ref):
    cos = cos_ref[...]
    sin = sin_ref[...]
    perm = perm_ref[...]

    def rope(x):
        hi = x.astype(bf16)
        lo = (x - hi.astype(f32)).astype(bf16)
        partner = _mm(hi, perm) + _mm(lo, perm)
        return x * cos + partner * sin

    def headnorm(x, gain):
        w = x.shape[1]
        ms = _mm((x * x).astype(bf16), gmat_ref[:w, :w])
        return x * lax.rsqrt(ms + EPS) * gain

    def rmsnorm(x, gain):
        ms = jnp.mean(x * x, axis=-1, keepdims=True)
        return x * lax.rsqrt(ms + EPS) * gain

    na_scale = NA_DIM ** -0.5 * LOG2E
    naq = headnorm(proj_ref[:, P_NAQ:P_NAQ + NA_W].astype(f32), naq_g[...])
    qka_ref[:, A_NAQ:A_NAQ + NA_W] = (naq * na_scale).astype(bf16)
    nak = headnorm(proj_ref[:, P_NAK:P_NAK + NA_W].astype(f32), nak_g[...])
    qka_ref[:, A_NAK:A_NAK + NA_W] = nak.astype(bf16)

    sw_scale = SWA_DIM ** -0.5 * LOG2E
    swq = headnorm(proj_ref[:, P_SWQ:P_SWQ + SWA_W].astype(f32), swq_g[...])
    for p in range(SWA_W // LANES):
        sl = slice(p * LANES, (p + 1) * LANES)
        qka_ref[:, A_SWQ + p * LANES:A_SWQ + (p + 1) * LANES] = (rope(swq[:, sl]) * sw_scale).astype(bf16)
    swk = headnorm(proj_ref[:, P_SWK:P_SWK + SWA_KV_W].astype(f32), swk_g[...])
    qka_ref[:, A_SWK:A_SWK + SWA_KV_W] = rope(swk).astype(bf16)

    mla_scale = MLA_QK ** -0.5 * LOG2E
    cq = rmsnorm(proj_ref[:, P_CQ:P_CQ + MLA_Q_RANK].astype(f32), qn_ref[...]).astype(bf16)
    q = _mm(cq, wuq_ref[...])
    ckv = rmsnorm(proj_ref[:, P_CKV:P_CKV + MLA_KV_RANK].astype(f32), kvn_ref[...]).astype(bf16)
    kn = _mm(ckv, wuk_ref[...])
    v = _mm(ckv, wuv_ref[...]).astype(bf16)
    ones_col = jnp.where(lax.broadcasted_iota(jnp.int32, (v.shape[0], LANES), 1) == 0, 1.0, 0.0).astype(bf16)
    for h in range(MLA_HEADS):
        mv_ref[:, h * MLA_PAD:h * MLA_PAD + MLA_V] = v[:, h * MLA_V:(h + 1) * MLA_V]
        mv_ref[:, h * MLA_PAD + MLA_V:(h + 1) * MLA_PAD] = ones_col
    kr = proj_ref[:, P_KR:P_KR + LANES].astype(f32)
    kr_ss = jnp.sum(kr * kr, axis=-1, keepdims=True)
    for h in range(MLA_HEADS):
        o = h * MLA_PAD
        qh = q[:, o:o + MLA_PAD]
        r = lax.rsqrt(jnp.sum(qh * qh, axis=-1, keepdims=True) * (1.0 / MLA_QK) + EPS)
        qh = qh * r * qg_ref[:, o:o + MLA_PAD]
        mq_ref[:, o:o + LANES] = (qh[:, :LANES] * mla_scale).astype(bf16)
        mq_ref[:, o + LANES:o + MLA_PAD] = (rope(qh[:, LANES:]) * mla_scale).astype(bf16)
        kh = kn[:, h * MLA_NOPE:(h + 1) * MLA_NOPE]
        r = lax.rsqrt((jnp.sum(kh * kh, axis=-1, keepdims=True) + kr_ss) * (1.0 / MLA_QK) + EPS)
        mk_ref[:, o:o + LANES] = (kh * r * kg_ref[:, o:o + LANES]).astype(bf16)
        mk_ref[:, o + LANES:o + MLA_PAD] = rope(kr * r * kg_ref[:, o + LANES:o + MLA_PAD]).astype(bf16)


def _prep(proj, cos, sin, gmat, lw):
    B, T, _ = proj.shape
    nt = pl.cdiv(T, TT)

    def const(a):
        return pl.BlockSpec(a.shape, lambda b, i: (0,) * a.ndim)

    lane = np.arange(LANES)
    perm = jnp.asarray(lane[:, None] == (lane[None, :] ^ 16), bf16)
    consts = [perm, gmat, lw["naq_g"], lw["nak_g"], lw["swq_g"], lw["swk_g"], lw["qn"], lw["wuq"], lw["qg"],
              lw["kvn"], lw["wuk"], lw["wuv"], lw["kg"]]
    widths = (QKA_W, MLA_HEADS * MLA_PAD, MLA_HEADS * MLA_PAD, MLA_HEADS * MLA_PAD)
    return pl.pallas_call(
        _prep_kernel,
        grid=(B, nt),
        in_specs=[
            pl.BlockSpec((None, TT, PROJ_W), lambda b, i: (b, i, 0)),
            pl.BlockSpec((TT, LANES), lambda b, i: (i, 0)),
            pl.BlockSpec((TT, LANES), lambda b, i: (i, 0)),
        ] + [const(a) for a in consts],
        out_specs=[pl.BlockSpec((None, TT, w), lambda b, i: (b, i, 0)) for w in widths],
        out_shape=[jax.ShapeDtypeStruct((B, T, w), bf16) for w in widths],
        compiler_params=_params("arbitrary", "arbitrary"),
        name="prep",
    )(proj, cos, sin, *consts)


def _half_masks():
    lane = lax.broadcasted_iota(jnp.int32, (1, LANES), 1)
    lo = jnp.where(lane < 64, 1.0, 0.0).astype(bf16)
    return lo, (1.0 - lo.astype(f32)).astype(bf16)


def _pairs_attend(jobs, masks):
    scored = []
    for qp, parts, _ in jobs:
        q2 = jnp.concatenate([qp * masks[0], qp * masks[1]], axis=0)
        scores = []
        for k, _, bias in parts:
            s = _nt(q2, k)
            scores.append(s if bias is None else s + bias)
        scored.append(scores)
    normed = []
    for (qp, _, sinks), scores in zip(jobs, scored):
        n_q = qp.shape[0]
        m = functools.reduce(jnp.maximum, [jnp.max(s, axis=-1, keepdims=True) for s in scores])
        if sinks is not None:
            first = lax.broadcasted_iota(jnp.int32, (2 * n_q, 1), 0) < n_q
            sink = jnp.where(first, sinks[0], sinks[1])
            m = jnp.maximum(m, sink)
        ps = [jnp.exp2(s - m) for s in scores]
        l = functools.reduce(jnp.add, [jnp.sum(p, axis=-1, keepdims=True) for p in ps])
        if sinks is not None:
            l = l + jnp.exp2(sink - m)
        normed.append((ps, l))
    lane = lax.broadcasted_iota(jnp.int32, (1, LANES), 1)
    outs = []
    for (qp, parts, _), (ps, l) in zip(jobs, normed):
        n_q = qp.shape[0]
        o = functools.reduce(jnp.add, [_mm(p.astype(bf16), v) for p, (_, v, _) in zip(ps, parts)]) / l
        outs.append(jnp.where(lane < 64, o[:n_q], o[n_q:]))
    return outs


def _pair_attend(qp, parts, masks, sinks=None):
    return _pairs_attend([(qp, parts, sinks)], masks)[0]


def _na_kernel(n_lat, S, rows, offs, q_ref, k_ref, v_ref, tall_ref, rmask_ref, o_ref):
    i = pl.program_id(1)
    masks = _half_masks()
    T = k_ref.shape[0]

    def latent(ty):
        start = GRID_W * jnp.clip(NA_TILE_ROWS * i - NA_WIN_ROWS // 2, 0, rows - NA_BAND_ROWS)
        start = pl.multiple_of(start, GRID_W)
        jobs = []
        for p in range(NA_W // LANES):
            sl = slice(p * LANES, (p + 1) * LANES)
            slabs = []
            for head in (2 * p, 2 * p + 1):
                for qr in range(NA_TILE_ROWS):
                    odd = int(offs[ty, qr]) % 2
                    lo = (int(offs[ty, qr]) - odd) * GRID_W
                    slabs.append(tall_ref[head, odd, :, lo:lo + NA_BAND] + rmask_ref[ty, qr:qr + 1, :])
            band = (k_ref[pl.ds(start, NA_BAND), sl], v_ref[pl.ds(start, NA_BAND), sl],
                    jnp.concatenate(slabs, axis=0))
            ctx = (k_ref[S:T, sl], v_ref[S:T, sl], None)
            jobs.append((q_ref[:, sl], [band, ctx], None))
        for p, o in enumerate(_pairs_attend(jobs, masks)):
            o_ref[:, p * LANES:(p + 1) * LANES] = o.astype(bf16)

    pl.when(i == 0)(functools.partial(latent, 0))
    pl.when(jnp.logical_and(i > 0, i < n_lat - 1))(functools.partial(latent, 1))
    pl.when(i == n_lat - 1)(functools.partial(latent, 2))

    @pl.when(i >= n_lat)
    def _():
        for p in range(NA_W // LANES):
            sl = slice(p * LANES, (p + 1) * LANES)
            ctx = (k_ref[S:T, sl], v_ref[S:T, sl], None)
            o_ref[:, sl] = _pair_attend(q_ref[:, sl], [ctx], masks).astype(bf16)


def _na(qka, proj, rpb, S):
    B, T, _ = qka.shape
    nt, n_lat, rows = T // TM, S // TM, S // GRID_W
    assert n_lat >= 3
    offs, front, back, row_mask = _na_geometry(rows)
    tall = _na_bias(rpb, front, back)
    row_mask = jnp.asarray(row_mask)
    return pl.pallas_call(
        functools.partial(_na_kernel, n_lat, S, rows, offs),
        grid=(B, nt),
        in_specs=[
            pl.BlockSpec((None, TM, NA_W), lambda b, i: (b, i, A_NAQ // NA_W)),
            pl.BlockSpec((None, T, NA_W), lambda b, i: (b, 0, A_NAK // NA_W)),
            pl.BlockSpec((None, T, NA_W), lambda b, i: (b, 0, P_NAV // NA_W)),
            pl.BlockSpec(tall.shape, lambda b, i: (0, 0, 0, 0)),
            pl.BlockSpec(row_mask.shape, lambda b, i: (0, 0, 0)),
        ],
        out_specs=pl.BlockSpec((None, TM, NA_W), lambda b, i: (b, i, 0)),
        out_shape=jax.ShapeDtypeStruct((B, T, NA_W), bf16),
        compiler_params=_params("arbitrary", "arbitrary"),
        name="na_attn",
    )(qka, qka, proj, tall, row_mask)


def _na_geometry(rows):
    n_tiles = rows // NA_TILE_ROWS
    band = np.arange(NA_BAND_ROWS)
    dr0 = np.zeros((3, NA_TILE_ROWS), np.int64)
    valid = np.zeros((3, NA_TILE_ROWS, NA_BAND_ROWS), bool)
    for ty, rt in enumerate((0, 1, n_tiles - 1)):
        bs = int(np.clip(NA_TILE_ROWS * rt - NA_WIN_ROWS // 2, 0, rows - NA_BAND_ROWS))
        for qr in range(NA_TILE_ROWS):
            r = NA_TILE_ROWS * rt + qr
            s_r = int(np.clip(r - NA_WIN_ROWS // 2, 0, rows - NA_WIN_ROWS))
            dr0[ty, qr] = bs - r + NA_WIN_ROWS - 1
            valid[ty, qr] = (bs + band >= s_r) & (bs + band < s_r + NA_WIN_ROWS)
    front = int(max(0, -dr0.min()))
    back = int(max(0, dr0.max() + NA_BAND_ROWS - (2 * NA_WIN_ROWS - 1)))
    row_mask = np.where(np.repeat(valid, GRID_W, axis=2), 0.0, NEG).astype(np.float32)
    return dr0 + front, front, back, row_mask


def _na_bias(rpb, front, back):
    col = np.arange(GRID_W)
    c0 = np.clip(col - NA_WIN_COLS // 2, 0, GRID_W - NA_WIN_COLS)
    col_ok = (col[None, :] >= c0[:, None]) & (col[None, :] < c0[:, None] + NA_WIN_COLS)
    dc = np.clip(col[None, :] - col[:, None] + NA_WIN_COLS - 1, 0, 2 * NA_WIN_COLS - 2)
    pick = jnp.asarray(np.arange(2 * NA_WIN_COLS - 1)[:, None, None] == dc[None], f32)
    tz = jnp.einsum("hrd,dqk->hrqk", rpb, pick, precision=lax.Precision.HIGHEST)
    tz = jnp.where(col_ok, tz * LOG2E, NEG)
    n_blocks = front + tz.shape[1] + back
    width = pl.cdiv(n_blocks * GRID_W, LANES) * LANES
    tall = tz.transpose(0, 2, 1, 3).reshape(NA_HEADS, GRID_W, -1)
    tall = jnp.pad(tall, ((0, 0), (0, 0), (front * GRID_W, width + GRID_W - (front + tz.shape[1]) * GRID_W)),
                   constant_values=NEG)
    return jnp.stack([tall[:, :, :width], tall[:, :, GRID_W:GRID_W + width]], axis=1)


def _mla_kernel(n_q, S, q_ref, k_ref, v_ref, o_ref):
    i = pl.program_id(2)
    T = k_ref.shape[0]

    def attend(q, bounds):
        m_run = acc = None
        for lo, hi in bounds:
            s = _nt(q, k_ref[lo:hi, :])
            m_new = jnp.max(s, axis=-1, keepdims=True)
            if m_run is not None:
                m_new = jnp.maximum(m_run, m_new)
            pv = _mm(jnp.exp2(s - m_new).astype(bf16), v_ref[lo:hi, :])
            acc = pv if acc is None else acc * jnp.exp2(m_run - m_new) + pv
            m_run = m_new
        return (acc[:, :MLA_V] / acc[:, MLA_V:MLA_V + 1]).astype(bf16)

    @pl.when(i < n_q)
    def _():
        chunks = [(lo, lo + MLA_TK) for lo in range(0, S, MLA_TK)] + [(S, T)]
        o_ref[...] = attend(q_ref[...], chunks)

    @pl.when(i >= n_q)
    def _():
        o_ref[0:T - S, :] = attend(q_ref[0:T - S, :], [(S, T)])


def _mla(mq, mk, mv, S):
    B, T, _ = mq.shape
    tq = min(MLA_TQ, S)
    n_q = S // tq
    assert S % tq == 0 and S % MLA_TK == 0 and T - S <= tq
    return pl.pallas_call(
        functools.partial(_mla_kernel, n_q, S),
        grid=(B, MLA_HEADS, n_q + 1),
        in_specs=[
            pl.BlockSpec((None, tq, MLA_PAD), lambda b, h, i: (b, i, h)),
            pl.BlockSpec((None, T, MLA_PAD), lambda b, h, i: (b, 0, h)),
            pl.BlockSpec((None, T, MLA_PAD), lambda b, h, i: (b, 0, h)),
        ],
        out_specs=pl.BlockSpec((None, tq, MLA_V), lambda b, h, i: (b, i, h)),
        out_shape=jax.ShapeDtypeStruct((B, T, MLA_W), bf16),
        compiler_params=_params("arbitrary", "arbitrary", "arbitrary"),
        name="mla_attn",
    )(mq, mk, mv)


def _swa_kernel(n_lat, S, sink_ref, q_ref, k_ref, v_ref, band_ref, o_ref):
    i = pl.program_id(1)
    masks = _half_masks()
    T = k_ref.shape[0]
    kc, vc = k_ref[S:T, :], v_ref[S:T, :]

    def sinks(p):
        return (sink_ref[SWA_HEAD_ORDER[2 * p]], sink_ref[SWA_HEAD_ORDER[2 * p + 1]])

    @pl.when(i < n_lat)
    def _():
        start = pl.multiple_of(jnp.clip(i * TM - SWA_WINDOW, 0, S - SWA_KEYS), SWA_WINDOW)
        band = band_ref[jnp.where(i == 0, 0, jnp.where(i == n_lat - 1, 2, 1))]
        kb, vb = k_ref[pl.ds(start, SWA_KEYS), :], v_ref[pl.ds(start, SWA_KEYS), :]
        jobs = [(q_ref[:, p * LANES:(p + 1) * LANES], [(kb, vb, band), (kc, vc, None)], sinks(p))
                for p in range(SWA_W // LANES)]
        for p, o in enumerate(_pairs_attend(jobs, masks)):
            o_ref[:, p * LANES:(p + 1) * LANES] = o.astype(bf16)

    @pl.when(i >= n_lat)
    def _():
        for p in range(SWA_W // LANES):
            sl = slice(p * LANES, (p + 1) * LANES)
            o_ref[:, sl] = _pair_attend(q_ref[:, sl], [(kc, vc, None)], masks, sinks(p)).astype(bf16)


def _swa_band_masks(S):
    n_lat = S // TM
    out = np.zeros((3, 2 * TM, SWA_KEYS), np.float32)
    for ty, i in enumerate((0, 1, n_lat - 1)):
        start = int(np.clip(i * TM - SWA_WINDOW, 0, S - SWA_KEYS))
        qpos = i * TM + np.tile(np.arange(TM), 2)
        kpos = start + np.arange(SWA_KEYS)
        out[ty] = np.where(np.abs(qpos[:, None] - kpos[None, :]) <= SWA_WINDOW, 0.0, NEG)
    return jnp.asarray(out)


def _swa(qka, proj, sink, S):
    B, T, _ = qka.shape
    nt, n_lat = T // TM, S // TM
    assert n_lat >= 3
    band = _swa_band_masks(S)
    return pl.pallas_call(
        functools.partial(_swa_kernel, n_lat, S),
        grid=(B, nt),
        in_specs=[
            pl.BlockSpec(memory_space=pltpu.SMEM),
            pl.BlockSpec((None, TM, SWA_W), lambda b, i: (b, i, A_SWQ // SWA_W)),
            pl.BlockSpec((None, T, SWA_KV_W), lambda b, i: (b, 0, A_SWK // SWA_KV_W)),
            pl.BlockSpec((None, T, SWA_KV_W), lambda b, i: (b, 0, P_SWV // SWA_KV_W)),
            pl.BlockSpec(band.shape, lambda b, i: (0, 0, 0)),
        ],
        out_specs=pl.BlockSpec((None, TM, SWA_W), lambda b, i: (b, i, 0)),
        out_shape=jax.ShapeDtypeStruct((B, T, SWA_W), bf16),
        compiler_params=_params("arbitrary", "arbitrary"),
        name="swa_attn",
    )(sink, qka, qka, proj, band)


def _post_kernel(ona_ref, omla_ref, oswa_ref, gl_ref, x_ref, mod_ref, wna_ref, wmla_ref, wswa_ref, wo_ref,
                 g2_ref, rt_ref, xo_ref, h2_ref, aff_ref, affx_ref):
    D = x_ref.shape[1]
    n_e = aff_ref.shape[1]
    n_sub = 2
    sub = x_ref.shape[0] // n_sub
    halves = [slice(r * sub, (r + 1) * sub) for r in range(n_sub)]
    gated = []
    for rows in halves:
        merged = None
        for j, (o_ref, w_ref) in enumerate(((ona_ref, wna_ref), (omla_ref, wmla_ref), (oswa_ref, wswa_ref))):
            y = _mm(o_ref[rows, :], w_ref[...])
            gy = (jnp.tanh(gl_ref[rows, j * D:(j + 1) * D].astype(f32)) + 1.0) * y
            merged = gy if merged is None else merged + gy
        gated.append(merged.astype(bf16))
    mixed = [_mm(m, wo_ref[...]) for m in gated]
    for rows, res in zip(halves, mixed):
        x = x_ref[rows, :] + mod_ref[2:3, :] * res
        xo_ref[rows, :] = x
        ms = jnp.mean(x * x, axis=-1, keepdims=True)
        h2 = x * lax.rsqrt(ms + EPS) * (g2_ref[...] * (1.0 + mod_ref[4:5, :])) + mod_ref[3:4, :]
        h_hi = h2.astype(bf16)
        h2_ref[rows, :] = h_hi
        h_lo = (h2 - h_hi.astype(f32)).astype(bf16)
        parts = _mm(h_hi, rt_ref[...]) + _mm(h_lo, rt_ref[...])
        logits = parts[:, :n_e] + parts[:, n_e:]
        e = jnp.exp(logits - jnp.max(logits, axis=1, keepdims=True))
        aff = e / jnp.sum(e, axis=1, keepdims=True)
        aff_ref[rows, :] = aff
        p0 = aff.astype(bf16)
        r1 = aff - p0.astype(f32)
        p1 = r1.astype(bf16)
        p2 = (r1 - p1.astype(f32)).astype(bf16)
        pad = jnp.zeros((sub, LANES - 3 * n_e), bf16)
        affx_ref[rows, :] = jnp.concatenate([p0, p1, p2, pad], axis=1)


def _post(ona, omla, oswa, gl, x, mod, lw, n_lat):
    B, T, D = x.shape
    nt = pl.cdiv(T, TT)
    E = lw["router"].shape[1] // 2

    def tok(w):
        return pl.BlockSpec((None, TT, w), lambda b, i: (b, i, 0))

    def const(a):
        return pl.BlockSpec(a.shape, lambda b, i: (0,) * a.ndim)

    consts = [lw["wna"], lw["wmla"], lw["wswa"], lw["wo"], lw["g2"], lw["router"]]
    return pl.pallas_call(
        _post_kernel,
        grid=(B, nt),
        in_specs=[tok(NA_W), tok(MLA_W), tok(SWA_W), tok(N_BRANCH * D), tok(D), _mod_spec(n_lat, D)]
        + [const(a) for a in consts],
        out_specs=[tok(D), tok(D), tok(E), tok(LANES)],
        out_shape=[jax.ShapeDtypeStruct((B, T, D), f32), jax.ShapeDtypeStruct((B, T, D), bf16),
                   jax.ShapeDtypeStruct((B, T, E), f32), jax.ShapeDtypeStruct((B, T, LANES), bf16)],
        compiler_params=_params("arbitrary", "arbitrary"),
        name="post_attn",
    )(ona, omla, oswa, gl, x, mod, *consts)


def _lane_cumsum(mask, tri):
    E, n = mask.shape
    carry = jnp.zeros((E, 1), f32)
    outs = []
    for k in range(n // LANES):
        w = _mm(mask[:, k * LANES:(k + 1) * LANES].astype(bf16), tri) + carry
        outs.append(w)
        carry = w[:, LANES - 1:LANES]
    return jnp.concatenate(outs, axis=1)


def _select_slots(aff, cap, base, tri):
    bits = lax.bitcast_convert_type(aff, jnp.int32)
    thr = jnp.zeros((aff.shape[0], 1), jnp.int32)
    for bit in range(30, -1, -1):
        cand = thr | (1 << bit)
        cnt = jnp.sum(jnp.where(bits >= cand, 1.0, 0.0), axis=1, keepdims=True)
        thr = jnp.where(cnt >= cap, cand, thr)
    gt = jnp.where(bits > thr, 1.0, 0.0)
    eq = jnp.where(bits == thr, 1.0, 0.0)
    need = cap - jnp.sum(gt, axis=1, keepdims=True)
    sel = jnp.maximum(gt, jnp.where(_lane_cumsum(eq, tri) <= need, eq, 0.0))
    return jnp.where(sel > 0.0, _lane_cumsum(sel, tri) + base, 0.0)


def _topk_kernel(S, cap_s, cap_l, aff_ref, tri_ref, before_ref, cp_ref, bnd_ref):
    T = aff_ref.shape[1]
    tri = tri_ref[...]
    cp_ref[:, 0:S] = _select_slots(aff_ref[:, 0:S], cap_s, 0.0, tri)
    cp_ref[:, S:T] = _select_slots(aff_ref[:, S:T], cap_l, float(cap_s), tri)
    bnd_ref[...] = _mm(jnp.where(cp_ref[...] > 0.0, 1.0, 0.0).astype(bf16), before_ref[...])


def _topk(aff, S, cap_s, cap_l):
    B, E, T = aff.shape
    tri = jnp.asarray(np.triu(np.ones((LANES, LANES), np.float32)), bf16)
    before = jnp.asarray(np.arange(T)[:, None] < MOE_CHUNK * np.arange(LANES)[None, :], bf16)
    return pl.pallas_call(
        functools.partial(_topk_kernel, S, cap_s, cap_l),
        grid=(B,),
        in_specs=[pl.BlockSpec((None, E, T), lambda b: (b, 0, 0)),
                  pl.BlockSpec((LANES, LANES), lambda b: (0, 0)),
                  pl.BlockSpec((T, LANES), lambda b: (0, 0))],
        out_specs=[pl.BlockSpec((None, E, T), lambda b: (b, 0, 0)),
                   pl.BlockSpec((None, E, LANES), lambda b: (b, 0, 0))],
        out_shape=[jax.ShapeDtypeStruct((B, E, T), f32), jax.ShapeDtypeStruct((B, E, LANES), f32)],
        compiler_params=_params("arbitrary"),
        name="expert_select",
    )(aff, tri, before)


def _window_count(lo, hi, r0):
    return jnp.where(hi > lo, lax.div(hi - r0 + (MOE_WIN - 1), MOE_WIN), 0)


def _gather_kernel(n_e, rows, bnd_ref, h2_ref, affx_ref, cp_ref, xg_ref):
    b, k = pl.program_id(0), pl.program_id(1)
    D = h2_ref.shape[1]

    @pl.when(k == 0)
    def _():
        xg_ref[...] = jnp.zeros_like(xg_ref)

    ex = lax.broadcasted_iota(jnp.int32, (n_e, 1), 0)
    starts = []
    start_col = jnp.zeros((n_e, 1), f32)
    for e in range(n_e):
        r0 = lax.div(bnd_ref[(b * n_e + e) * LANES + k], 16) * 16
        starts.append(r0)
        start_col = start_col + jnp.where(ex == e, r0.astype(f32), 0.0)
    rel = cp_ref[...] - start_col
    rel_rows = jnp.broadcast_to(rel[:, None, :], (n_e, MOE_WIN, MOE_CHUNK)).reshape(n_e * MOE_WIN, MOE_CHUNK)
    j1 = lax.broadcasted_iota(jnp.int32, (n_e, MOE_WIN, 1), 1).reshape(n_e * MOE_WIN, 1) + 1
    onehot = jnp.where(rel_rows == j1.astype(f32), 1.0, 0.0).astype(bf16)
    tok, tok_aff = h2_ref[...], affx_ref[...]
    moved = _mm(onehot, tok).astype(bf16)
    moved_aff = _mm(onehot, tok_aff).astype(bf16)
    for e in range(n_e):
        dst = pl.ds(pl.multiple_of(e * rows + starts[e], 16), MOE_WIN)
        xg_ref[dst, 0:D] += moved[e * MOE_WIN:(e + 1) * MOE_WIN]
        xg_ref[dst, D:D + LANES] += moved_aff[e * MOE_WIN:(e + 1) * MOE_WIN]

    sub1 = lax.broadcasted_iota(jnp.int32, (MOE_WIN, 1), 0) + 1
    for e in range(n_e):
        lo, hi = bnd_ref[(b * n_e + e) * LANES + k], bnd_ref[(b * n_e + e) * LANES + k + 1]
        n_win = _window_count(lo, hi, starts[e])

        @pl.when(n_win > 1)
        def _(e=e, n_win=n_win):
            def extra(w, c):
                hit = rel[e:e + 1, :] == (sub1 + w * MOE_WIN).astype(f32)
                oh = jnp.where(hit, 1.0, 0.0).astype(bf16)
                dst = pl.ds(pl.multiple_of(e * rows + starts[e] + w * MOE_WIN, 16), MOE_WIN)
                xg_ref[dst, 0:D] += _mm(oh, tok).astype(bf16)
                xg_ref[dst, D:D + LANES] += _mm(oh, tok_aff).astype(bf16)
                return c

            lax.fori_loop(1, n_win, extra, 0)


def _gather(h2, affx, cp, bnd, n_slots):
    B, T, D = h2.shape
    E = cp.shape[1]
    nc = T // MOE_CHUNK
    rows = n_slots + MOE_WIN
    assert n_slots % 16 == 0 and nc + 1 <= LANES and 3 * E <= LANES
    return pl.pallas_call(
        functools.partial(_gather_kernel, E, rows),
        grid_spec=pltpu.PrefetchScalarGridSpec(
            num_scalar_prefetch=1,
            grid=(B, nc),
            in_specs=[
                pl.BlockSpec((None, MOE_CHUNK, D), lambda b, k, s: (b, k, 0)),
                pl.BlockSpec((None, MOE_CHUNK, LANES), lambda b, k, s: (b, k, 0)),
                pl.BlockSpec((None, E, MOE_CHUNK), lambda b, k, s: (b, 0, k)),
            ],
            out_specs=pl.BlockSpec((None, E * rows, D + LANES), lambda b, k, s: (b, 0, 0)),
        ),
        out_shape=jax.ShapeDtypeStruct((B, E * rows, D + LANES), bf16),
        compiler_params=_params("arbitrary", "arbitrary"),
        name="moe_gather",
    )(bnd, h2, affx, cp)


def _ffn_kernel(n_slots, n_e, xg_ref, wg_ref, wu_ref, wd_ref, y_ref, wg_s, wu_s, wd_s):
    e, b = pl.program_id(0), pl.program_id(1)
    D = wg_ref.shape[0]

    @pl.when(b == 0)
    def _():
        wg_s[...] = wg_ref[...].astype(bf16)
        wu_s[...] = wu_ref[...].astype(bf16)
        wd_s[...] = wd_ref[...].astype(bf16)

    n_b = xg_ref.shape[0]
    xg = jnp.concatenate([xg_ref[s, 0:n_slots, 0:D] for s in range(n_b)], axis=0)
    a = _mm(xg, wg_s[...])
    u = _mm(xg, wu_s[...])
    act = (a * jax.nn.sigmoid(a) * u).astype(bf16)
    lane = lax.broadcasted_iota(jnp.int32, (1, LANES), 1)
    mine = jnp.logical_and(lax.rem(lane, n_e) == e, lane < 3 * n_e)
    parts = jnp.concatenate([xg_ref[s, 0:n_slots, D:D + LANES] for s in range(n_b)], axis=0).astype(f32)
    gate = jnp.sum(jnp.where(mine, parts, 0.0), axis=1, keepdims=True)
    y = (_mm(act, wd_s[...]) * gate).astype(bf16)
    for s in range(n_b):
        y_ref[s, 0:n_slots, :] = y[s * n_slots:(s + 1) * n_slots]
        y_ref[s, n_slots:, :] = jnp.zeros((y_ref.shape[1] - n_slots, y_ref.shape[2]), bf16)


def _ffn(xg, w_gate, w_up, w_down, layer, n_slots):
    B = xg.shape[0]
    _, E, D, F = w_gate.shape
    rows = xg.shape[1] // E
    n_b = 2 if B % 2 == 0 else 1
    return pl.pallas_call(
        functools.partial(_ffn_kernel, n_slots, E),
        grid=(E, B // n_b),
        in_specs=[
            pl.BlockSpec((n_b, None, rows, D + LANES), lambda e, b: (b, e, 0, 0)),
            pl.BlockSpec((None, None, D, F), lambda e, b: (layer, e, 0, 0)),
            pl.BlockSpec((None, None, D, F), lambda e, b: (layer, e, 0, 0)),
            pl.BlockSpec((None, None, F, D), lambda e, b: (layer, e, 0, 0)),
        ],
        out_specs=pl.BlockSpec((n_b, None, rows, D), lambda e, b: (b, e, 0, 0)),
        out_shape=jax.ShapeDtypeStruct((B, E, rows, D), bf16),
        scratch_shapes=[pltpu.VMEM((D, F), bf16), pltpu.VMEM((D, F), bf16), pltpu.VMEM((F, D), bf16)],
        compiler_params=_params("arbitrary", "arbitrary"),
        name="moe_ffn",
    )(xg.reshape(B, E, rows, D + LANES), w_gate, w_up, w_down)


def _combine_kernel(n_e, rows, residual, bnd_ref, y_ref, cpt_ref, expand_ref, *refs):
    o_ref = refs[-1]
    b, k = pl.program_id(0), pl.program_id(1)
    cpt = cpt_ref[...]
    hi_part = jnp.floor(cpt * (1.0 / 32.0))
    lo_part = cpt - 32.0 * hi_part
    expand = expand_ref[...]
    rank = 32.0 * _mm(hi_part.astype(bf16), expand) + _mm(lo_part.astype(bf16), expand)
    lane = lax.broadcasted_iota(jnp.int32, (1, n_e * MOE_WIN), 1)
    lane_e = lax.div(lane, MOE_WIN)
    starts = []
    tgt = (lane - lane_e * MOE_WIN + 1).astype(f32)
    for e in range(n_e):
        r0 = lax.div(bnd_ref[(b * n_e + e) * LANES + k], 16) * 16
        starts.append(r0)
        tgt = tgt + jnp.where(lane_e == e, r0.astype(f32), 0.0)
    onehot = jnp.where(rank == tgt, 1.0, 0.0).astype(bf16)
    ycat = jnp.concatenate(
        [y_ref[pl.ds(pl.multiple_of(e * rows + starts[e], 16), MOE_WIN), :] for e in range(n_e)], axis=0)
    o_ref[...] = _mm(onehot, ycat)

    lane_w = lax.broadcasted_iota(jnp.int32, (1, MOE_WIN), 1)
    for e in range(n_e):
        lo, hi = bnd_ref[(b * n_e + e) * LANES + k], bnd_ref[(b * n_e + e) * LANES + k + 1]
        n_win = _window_count(lo, hi, starts[e])

        @pl.when(n_win > 1)
        def _(e=e, n_win=n_win):
            rank_e = rank[:, e * MOE_WIN:(e + 1) * MOE_WIN]

            def extra(w, c):
                r = pl.multiple_of(starts[e] + w * MOE_WIN, 16)
                hit = rank_e == (lane_w + (r + 1)).astype(f32)
                o_ref[...] += _mm(jnp.where(hit, 1.0, 0.0).astype(bf16),
                                  y_ref[pl.ds(pl.multiple_of(e * rows + r, 16), MOE_WIN), :])
                return c

            lax.fori_loop(1, n_win, extra, 0)

    if residual:
        x_ref, mod_ref = refs[0], refs[1]
        o_ref[...] = x_ref[...] + mod_ref[5:6, :] * o_ref[...]


def _combine(y, cp, bnd, x=None, mod=None, n_tok=None):
    B, E, rows, D = y.shape
    residual = x is not None
    n_tok = n_tok if residual else cp.shape[2]
    expand = jnp.asarray(np.kron(np.eye(E), np.ones((1, MOE_WIN))), bf16)
    chunk = pl.BlockSpec((None, MOE_CHUNK, D), lambda b, k, s: (b, k, 0))
    in_specs = [
        pl.BlockSpec((None, E * rows, D), lambda b, k, s: (b, 0, 0)),
        pl.BlockSpec((None, MOE_CHUNK, E), lambda b, k, s: (b, k, 0)),
        pl.BlockSpec(expand.shape, lambda b, k, s: (0, 0)),
    ]
    args = [bnd, y.reshape(B, E * rows, D), cp.transpose(0, 2, 1), expand]
    if residual:
        in_specs += [chunk, pl.BlockSpec((None, None, 6, D), lambda b, k, s: (b, 0, 0, 0))]
        args += [x, mod]
    return pl.pallas_call(
        functools.partial(_combine_kernel, E, rows, residual),
        grid_spec=pltpu.PrefetchScalarGridSpec(
            num_scalar_prefetch=1,
            grid=(B, n_tok // MOE_CHUNK),
            in_specs=in_specs,
            out_specs=chunk,
        ),
        out_shape=jax.ShapeDtypeStruct((B, n_tok, D), f32),
        compiler_params=_params("arbitrary", "arbitrary"),
        name="moe_combine",
    )(*args)


def _rope_tables(S, L):
    t = jnp.arange(S)
    row, col = t // GRID_W, t % GRID_W
    f = 16
    inv = ROPE_BASE ** (-jnp.arange(f, dtype=f32) / f)
    ar = row.astype(f32)[:, None] * inv
    ac = col.astype(f32)[:, None] * inv
    cos = jnp.concatenate([jnp.cos(ar), jnp.cos(ar), jnp.cos(ac), jnp.cos(ac)], axis=1)
    sin = jnp.concatenate([-jnp.sin(ar), jnp.sin(ar), -jnp.sin(ac), jnp.sin(ac)], axis=1)
    cos = jnp.concatenate([jnp.tile(cos, (1, 2)), jnp.ones((L, LANES), f32)], axis=0)
    sin = jnp.concatenate([jnp.tile(sin, (1, 2)), jnp.zeros((L, LANES), f32)], axis=0)
    return cos, sin


def _pad_heads(w, n_heads, width, padded):
    lead = w.shape[:-1]
    w = w.reshape(lead + (n_heads, width))
    w = jnp.pad(w, [(0, 0)] * len(lead) + [(0, 0), (0, padded - width)])
    return w.reshape(lead + (n_heads * padded,))


def _layer_weights(l, D, w_in, norm2_g, na_q_g, na_k_g, mla_q_norm, mla_w_uq, mla_kv_norm, mla_w_ukv,
                   mla_q_g, mla_k_g, swa_q_g, swa_k_g, w_na_o, w_mla_o, w_swa_o, w_o, router):
    o = np.cumsum((0, NA_W, NA_W, NA_W, MLA_Q_RANK, MLA_KV_RANK, MLA_ROPE, SWA_W, SWA_KV_W, SWA_KV_W))
    o_naq, o_nak, o_nav, o_cq, o_ckv, o_kr, o_swq, o_swk, o_swv, o_gl = (int(v) for v in o)
    wi = w_in[l]
    swq = wi[:, o_swq:o_swq + SWA_W].reshape(D, SWA_HEADS, SWA_DIM)[:, SWA_HEAD_ORDER, :].reshape(D, SWA_W)
    kr = jnp.pad(wi[:, o_kr:o_kr + MLA_ROPE], ((0, 0), (0, LANES - MLA_ROPE)))
    w_all = jnp.concatenate([
        wi[:, o_naq:o_naq + 3 * NA_W], swq, wi[:, o_cq:o_cq + MLA_Q_RANK], wi[:, o_ckv:o_ckv + MLA_KV_RANK], kr,
        wi[:, o_swk:o_swk + 2 * SWA_KV_W], 0.5 * wi[:, o_gl:]], axis=1).astype(bf16)
    ukv = mla_w_ukv[l].reshape(MLA_KV_RANK, MLA_HEADS, MLA_NOPE + MLA_V)
    swa_o = w_swa_o[l].reshape(SWA_HEADS, SWA_DIM, D)[SWA_HEAD_ORDER, :, :].reshape(SWA_W, D)
    r_hi = router[l].astype(bf16)
    r_lo = (router[l] - r_hi.astype(f32)).astype(bf16)
    return dict(
        w_all=w_all,
        naq_g=jnp.tile(na_q_g[l], NA_HEADS)[None], nak_g=jnp.tile(na_k_g[l], NA_HEADS)[None],
        swq_g=jnp.tile(swa_q_g[l], SWA_HEADS)[None], swk_g=jnp.tile(swa_k_g[l], SWA_KV_HEADS)[None],
        qn=mla_q_norm[l][None], kvn=mla_kv_norm[l][None],
        wuq=_pad_heads(mla_w_uq[l], MLA_HEADS, MLA_QK, MLA_PAD).astype(bf16),
        wuk=ukv[:, :, :MLA_NOPE].reshape(MLA_KV_RANK, MLA_HEADS * MLA_NOPE).astype(bf16),
        wuv=ukv[:, :, MLA_NOPE:].reshape(MLA_KV_RANK, MLA_W).astype(bf16),
        qg=jnp.tile(jnp.pad(mla_q_g[l], (0, MLA_PAD - MLA_QK)), MLA_HEADS)[None],
        kg=jnp.tile(jnp.pad(mla_k_g[l], (0, MLA_PAD - MLA_QK)), MLA_HEADS)[None],
        wna=w_na_o[l].astype(bf16), wmla=w_mla_o[l].astype(bf16), wswa=swa_o.astype(bf16),
        wo=(0.5 * w_o[l]).astype(bf16), g2=norm2_g[l][None], router=jnp.concatenate([r_hi, r_lo], axis=1),
    )


def kernel(x, c, ctx, c_ctx, norm1_g, norm2_g, w_ada, b_ada, w_in, na_q_g, na_k_g, na_rpb, mla_q_norm, mla_w_uq, mla_kv_norm, mla_w_ukv, mla_q_g, mla_k_g, swa_q_g, swa_k_g, swa_sink, w_na_o, w_mla_o, w_swa_o, w_o, router, w_gate, w_up, w_down):
    B, S, D = x.shape
    L = ctx.shape[1]
    depth = w_in.shape[0]
    E = router.shape[2]
    T = S + L
    n_lat = S // TT
    rows = S // GRID_W
    assert S % TT == 0 and L % TM == 0 and L <= TT and rows >= NA_BAND_ROWS and S >= SWA_KEYS and B < 16
    cap_s = CAPACITY_FACTOR * S // E
    cap_l = CAPACITY_FACTOR * L // E

    cpad = jnp.zeros((16, D), f32).at[:B].set(c).at[B].set(c_ctx)
    mod = _ada(cpad, w_ada, b_ada).reshape(depth, 16, 6, D)
    mod = jnp.stack([mod[:, :B], jnp.broadcast_to(mod[:, B:B + 1], (depth, B, 6, D))], axis=2)

    cos, sin = _rope_tables(S, L)
    gmat = jnp.asarray(np.kron(np.eye(NA_HEADS), np.full((NA_DIM, NA_DIM), 1.0 / NA_DIM)), bf16)

    xs = jnp.concatenate([x, ctx], axis=1)
    moe = None
    for l in range(depth):
        lw = _layer_weights(l, D, w_in, norm2_g, na_q_g, na_k_g, mla_q_norm, mla_w_uq, mla_kv_norm, mla_w_ukv,
                            mla_q_g, mla_k_g, swa_q_g, swa_k_g, w_na_o, w_mla_o, w_swa_o, w_o, router)
        modp = mod[l - 1] if l > 0 else None
        xs, proj, gl = _in_proj(xs, moe, modp, mod[l], norm1_g[l][None], lw["w_all"], n_lat)
        qka, mq, mk, mv = _prep(proj, cos, sin, gmat, lw)
        ona = _na(qka, proj, na_rpb[l], S)
        omla = _mla(mq, mk, mv, S)
        oswa = _swa(qka, proj, swa_sink[l] * LOG2E, S)
        xs, h2, aff, affx = _post(ona, omla, oswa, gl, xs, mod[l], lw, n_lat)
        cp, bnd = _topk(aff.transpose(0, 2, 1), S, cap_s, cap_l)
        bnd = bnd.astype(jnp.int32).reshape(-1)
        xg = _gather(h2, affx, cp, bnd, cap_s + cap_l)
        y = _ffn(xg, w_gate, w_up, w_down, l, cap_s + cap_l)
        if l == depth - 1:
            return _combine(y, cp, bnd, xs, mod[l], S)
        moe = _combine(y, cp, bnd)
```

```python
import functools

import numpy as np
import jax
import jax.numpy as jnp
from jax import lax
from jax.experimental import pallas as pl
from jax.experimental.pallas import tpu as pltpu

GRID_W = 64
NA_HEADS, NA_DIM, NA_WIN_ROWS, NA_WIN_COLS = 6, 64, 8, 16
MLA_HEADS, MLA_Q_RANK, MLA_KV_RANK, MLA_NOPE, MLA_ROPE, MLA_V = 4, 256, 128, 128, 64, 128
MLA_QK = MLA_NOPE + MLA_ROPE
MLA_PAD = 256
SWA_HEADS, SWA_KV_HEADS, SWA_DIM, SWA_WINDOW = 6, 2, 64, 128
CAPACITY_FACTOR = 2
N_BRANCH = 3
ROPE_BASE = 10000.0
EPS = 1e-6

NA_W = NA_HEADS * NA_DIM
SWA_W = SWA_HEADS * SWA_DIM
SWA_KV_W = SWA_KV_HEADS * SWA_DIM
MLA_W = MLA_HEADS * MLA_V

LANES = 128
TM = 256
TT = 512
MLA_TQ = 1024
MLA_TK = 512
LOG2E = 1.4426950408889634
NA_TILE_ROWS = TM // GRID_W
NA_BAND_ROWS = NA_TILE_ROWS + NA_WIN_ROWS - 1
NA_BAND = NA_BAND_ROWS * GRID_W
SWA_KEYS = TM + 2 * SWA_WINDOW
MOE_CHUNK = 256
MOE_WIN = 64
NEG = -1e30
VMEM_LIMIT = 56 * 1024 * 1024

P_NAQ, P_NAK, P_NAV, P_SWQ = 0, 384, 768, 1152
P_CQ, P_CKV, P_KR, P_SWK, P_SWV = 1536, 1792, 1920, 2048, 2176
PROJ_W = 2304
N_CHUNK = 768
A_NAQ, A_NAK, A_SWQ, A_SWK = 0, 384, 768, 1152
QKA_W = 1280
SWA_HEAD_ORDER = (0, 3, 1, 4, 2, 5)

bf16 = jnp.bfloat16
f32 = jnp.float32


def _mm(a, b):
    return jnp.dot(a, b, preferred_element_type=f32)


def _nt(a, b):
    return lax.dot_general(a, b, (((1,), (1,)), ((), ())), preferred_element_type=f32)


def _params(*sem):
    return pltpu.CompilerParams(dimension_semantics=sem, vmem_limit_bytes=VMEM_LIMIT)


def _ada_kernel(c_ref, w_ref, b_ref, o_ref):
    a = c_ref[...]
    a = (a * jax.nn.sigmoid(a)).astype(bf16)
    o_ref[...] = _mm(a, w_ref[...].astype(bf16)) + b_ref[...]


def _ada(cpad, w_ada, b_ada):
    L, D, N = w_ada.shape
    tn = 1536
    return pl.pallas_call(
        _ada_kernel,
        grid=(L, N // tn),
        in_specs=[
            pl.BlockSpec((16, D), lambda l, j: (0, 0)),
            pl.BlockSpec((None, D, tn), lambda l, j: (l, 0, j)),
            pl.BlockSpec((None, 1, tn), lambda l, j: (l, 0, j)),
        ],
        out_specs=pl.BlockSpec((None, 16, tn), lambda l, j: (l, 0, j)),
        out_shape=jax.ShapeDtypeStruct((L, 16, N), f32),
        compiler_params=_params("arbitrary", "arbitrary"),
        name="ada",
    )(cpad, w_ada, b_ada.reshape(L, 1, N))


def _in_proj_kernel(has_moe, n_gl_chunks, *refs):
    if has_moe:
        x_ref, moe_ref, modp_ref, mod_ref, g_ref, w_ref, xo_ref, proj_ref, gl_ref = refs
    else:
        x_ref, mod_ref, g_ref, w_ref, proj_ref, gl_ref = refs
    n_sub = 2
    sub = x_ref.shape[0] // n_sub
    for r in range(n_sub):
        rows = slice(r * sub, (r + 1) * sub)
        x = x_ref[rows, :]
        if has_moe:
            x = x + modp_ref[5:6, :] * moe_ref[rows, :]
            xo_ref[rows, :] = x
        ms = jnp.mean(x * x, axis=-1, keepdims=True)
        h = x * lax.rsqrt(ms + EPS) * (g_ref[...] * (1.0 + mod_ref[1:2, :])) + mod_ref[0:1, :]
        h = h.astype(bf16)
        for c in range(PROJ_W // N_CHUNK):
            sl = slice(c * N_CHUNK, (c + 1) * N_CHUNK)
            proj_ref[rows, sl] = _mm(h, w_ref[:, sl]).astype(bf16)
        for c in range(n_gl_chunks):
            sl = slice(c * N_CHUNK, (c + 1) * N_CHUNK)
            gl_ref[rows, sl] = _mm(h, w_ref[:, PROJ_W + c * N_CHUNK:PROJ_W + (c + 1) * N_CHUNK]).astype(bf16)


def _mod_spec(n_lat, D):
    return pl.BlockSpec((None, None, 6, D), lambda b, i: (b, jnp.where(i >= n_lat, 1, 0), 0, 0))


def _in_proj(x, moe, modp, mod, g1, w, n_lat):
    B, T, D = x.shape
    nt = pl.cdiv(T, TT)
    has_moe = moe is not None
    n_gl_chunks = N_BRANCH * D // N_CHUNK
    tok = pl.BlockSpec((None, TT, D), lambda b, i: (b, i, 0))
    in_specs = [tok]
    args = [x]
    if has_moe:
        in_specs += [tok, _mod_spec(n_lat, D)]
        args += [moe, modp]
    in_specs += [
        _mod_spec(n_lat, D),
        pl.BlockSpec((1, D), lambda b, i: (0, 0)),
        pl.BlockSpec(w.shape, lambda b, i: (0, 0), pipeline_mode=pl.Buffered(1)),
    ]
    args += [mod, g1, w]
    out_specs = [
        pl.BlockSpec((None, TT, PROJ_W), lambda b, i: (b, i, 0)),
        pl.BlockSpec((None, TT, N_BRANCH * D), lambda b, i: (b, i, 0)),
    ]
    out_shape = [
        jax.ShapeDtypeStruct((B, T, PROJ_W), bf16),
        jax.ShapeDtypeStruct((B, T, N_BRANCH * D), bf16),
    ]
    if has_moe:
        out_specs = [tok] + out_specs
        out_shape = [jax.ShapeDtypeStruct((B, T, D), f32)] + out_shape
    res = pl.pallas_call(
        functools.partial(_in_proj_kernel, has_moe, n_gl_chunks),
        grid=(B, nt),
        in_specs=in_specs,
        out_specs=out_specs,
        out_shape=out_shape,
        compiler_params=_params("arbitrary", "arbitrary"),
        name="in_proj",
    )(*args)
    if has_moe:
        return res
    return [x] + list(res)


def _prep_kernel(proj_ref, cos_ref, sin_ref, perm_ref, gmat_ref, naq_g, nak_g, swq_g, swk_g,
                 qn_ref, wuq_ref, qg_ref, kvn_ref, wuk_ref, wuv_ref, kg_ref,
                 qka_ref, mq_ref, mk_ref, mv_ref):
    cos = cos_ref[...]
    sin = sin_ref[...]
    perm = perm_ref[...]

    def rope(x):
        hi = x.astype(bf16)
        lo = (x - hi.astype(f32)).astype(bf16)
        partner = _mm(hi, perm) + _mm(lo, perm)
        return x * cos + partner * sin

    def headnorm(x, gain):
        w = x.shape[1]
        ms = _mm((x * x).astype(bf16), gmat_ref[:w, :w])
        return x * lax.rsqrt(ms + EPS) * gain

    def rmsnorm(x, gain):
        ms = jnp.mean(x * x, axis=-1, keepdims=True)
        return x * lax.rsqrt(ms + EPS) * gain

    naq = headnorm(proj_ref[:, P_NAQ:P_NAQ + NA_W].astype(f32), naq_g[...])
    qka_ref[:, A_NAQ:A_NAQ + NA_W] = naq.astype(bf16)
    nak = headnorm(proj_ref[:, P_NAK:P_NAK + NA_W].astype(f32), nak_g[...])
    qka_ref[:, A_NAK:A_NAK + NA_W] = nak.astype(bf16)

    swq = headnorm(proj_ref[:, P_SWQ:P_SWQ + SWA_W].astype(f32), swq_g[...])
    for p in range(SWA_W // LANES):
        sl = slice(p * LANES, (p + 1) * LANES)
        qka_ref[:, A_SWQ + p * LANES:A_SWQ + (p + 1) * LANES] = rope(swq[:, sl]).astype(bf16)
    swk = headnorm(proj_ref[:, P_SWK:P_SWK + SWA_KV_W].astype(f32), swk_g[...])
    qka_ref[:, A_SWK:A_SWK + SWA_KV_W] = rope(swk).astype(bf16)

    cq = rmsnorm(proj_ref[:, P_CQ:P_CQ + MLA_Q_RANK].astype(f32), qn_ref[...]).astype(bf16)
    q = _mm(cq, wuq_ref[...])
    ckv = rmsnorm(proj_ref[:, P_CKV:P_CKV + MLA_KV_RANK].astype(f32), kvn_ref[...]).astype(bf16)
    kn = _mm(ckv, wuk_ref[...])
    v = _mm(ckv, wuv_ref[...]).astype(bf16)
    ones_col = jnp.where(lax.broadcasted_iota(jnp.int32, (v.shape[0], LANES), 1) == 0, 1.0, 0.0).astype(bf16)
    for h in range(MLA_HEADS):
        mv_ref[:, h * MLA_PAD:h * MLA_PAD + MLA_V] = v[:, h * MLA_V:(h + 1) * MLA_V]
        mv_ref[:, h * MLA_PAD + MLA_V:(h + 1) * MLA_PAD] = ones_col
    kr = proj_ref[:, P_KR:P_KR + LANES].astype(f32)
    kr_ss = jnp.sum(kr * kr, axis=-1, keepdims=True)
    for h in range(MLA_HEADS):
        o = h * MLA_PAD
        qh = q[:, o:o + MLA_PAD]
        r = lax.rsqrt(jnp.sum(qh * qh, axis=-1, keepdims=True) * (1.0 / MLA_QK) + EPS)
        qh = qh * r * qg_ref[:, o:o + MLA_PAD]
        mq_ref[:, o:o + LANES] = qh[:, :LANES].astype(bf16)
        mq_ref[:, o + LANES:o + MLA_PAD] = rope(qh[:, LANES:]).astype(bf16)
        kh = kn[:, h * MLA_NOPE:(h + 1) * MLA_NOPE]
        r = lax.rsqrt((jnp.sum(kh * kh, axis=-1, keepdims=True) + kr_ss) * (1.0 / MLA_QK) + EPS)
        mk_ref[:, o:o + LANES] = (kh * r * kg_ref[:, o:o + LANES]).astype(bf16)
        mk_ref[:, o + LANES:o + MLA_PAD] = rope(kr * r * kg_ref[:, o + LANES:o + MLA_PAD]).astype(bf16)


def _prep(proj, cos, sin, gmat, lw):
    B, T, _ = proj.shape
    nt = pl.cdiv(T, TT)

    def const(a):
        return pl.BlockSpec(a.shape, lambda b, i: (0,) * a.ndim)

    lane = np.arange(LANES)
    perm = jnp.asarray(lane[:, None] == (lane[None, :] ^ 16), bf16)
    consts = [perm, gmat, lw["naq_g"], lw["nak_g"], lw["swq_g"], lw["swk_g"], lw["qn"], lw["wuq"], lw["qg"],
              lw["kvn"], lw["wuk"], lw["wuv"], lw["kg"]]
    widths = (QKA_W, MLA_HEADS * MLA_PAD, MLA_HEADS * MLA_PAD, MLA_HEADS * MLA_PAD)
    return pl.pallas_call(
        _prep_kernel,
        grid=(B, nt),
        in_specs=[
            pl.BlockSpec((None, TT, PROJ_W), lambda b, i: (b, i, 0)),
            pl.BlockSpec((TT, LANES), lambda b, i: (i, 0)),
            pl.BlockSpec((TT, LANES), lambda b, i: (i, 0)),
        ] + [const(a) for a in consts],
        out_specs=[pl.BlockSpec((None, TT, w), lambda b, i: (b, i, 0)) for w in widths],
        out_shape=[jax.ShapeDtypeStruct((B, T, w), bf16) for w in widths],
        compiler_params=_params("arbitrary", "arbitrary"),
        name="prep",
    )(proj, cos, sin, *consts)


def _half_masks():
    lane = lax.broadcasted_iota(jnp.int32, (1, LANES), 1)
    lo = jnp.where(lane < 64, 1.0, 0.0).astype(bf16)
    return lo, (1.0 - lo.astype(f32)).astype(bf16)


def _pairs_attend(jobs, masks):
    scored = []
    for qp, parts, _ in jobs:
        q2 = jnp.concatenate([qp * masks[0], qp * masks[1]], axis=0)
        scores = []
        for k, _, bias in parts:
            s = _nt(q2, k)
            scores.append(s if bias is None else s + bias)
        scored.append(scores)
    normed = []
    for (qp, _, sinks), scores in zip(jobs, scored):
        n_q = qp.shape[0]
        m = functools.reduce(jnp.maximum, [jnp.max(s, axis=-1, keepdims=True) for s in scores])
        if sinks is not None:
            first = lax.broadcasted_iota(jnp.int32, (2 * n_q, 1), 0) < n_q
            sink = jnp.where(first, sinks[0], sinks[1])
            m = jnp.maximum(m, sink)
        ps = [jnp.exp2(s - m) for s in scores]
        l = functools.reduce(jnp.add, [jnp.sum(p, axis=-1, keepdims=True) for p in ps])
        if sinks is not None:
            l = l + jnp.exp2(sink - m)
        normed.append((ps, l))
    lane = lax.broadcasted_iota(jnp.int32, (1, LANES), 1)
    outs = []
    for (qp, parts, _), (ps, l) in zip(jobs, normed):
        n_q = qp.shape[0]
        o = functools.reduce(jnp.add, [_mm(p.astype(bf16), v) for p, (_, v, _) in zip(ps, parts)]) / l
        outs.append(jnp.where(lane < 64, o[:n_q], o[n_q:]))
    return outs


def _pair_attend(qp, parts, masks, sinks=None):
    return _pairs_attend([(qp, parts, sinks)], masks)[0]


def _na_kernel(n_lat, S, rows, offs, q_ref, k_ref, v_ref, tall_ref, rmask_ref, o_ref):
    i = pl.program_id(1)
    masks = _half_masks()
    T = k_ref.shape[0]

    def latent(ty):
        start = GRID_W * jnp.clip(NA_TILE_ROWS * i - NA_WIN_ROWS // 2, 0, rows - NA_BAND_ROWS)
        start = pl.multiple_of(start, GRID_W)
        jobs = []
        for p in range(NA_W // LANES):
            sl = slice(p * LANES, (p + 1) * LANES)
            slabs = []
            for head in (2 * p, 2 * p + 1):
                for qr in range(NA_TILE_ROWS):
                    odd = int(offs[ty, qr]) % 2
                    lo = (int(offs[ty, qr]) - odd) * GRID_W
                    slabs.append(tall_ref[head, odd, :, lo:lo + NA_BAND] + rmask_ref[ty, qr:qr + 1, :])
            band = (k_ref[pl.ds(start, NA_BAND), sl], v_ref[pl.ds(start, NA_BAND), sl],
                    jnp.concatenate(slabs, axis=0))
            ctx = (k_ref[S:T, sl], v_ref[S:T, sl], None)
            jobs.append((q_ref[:, sl], [band, ctx], None))
        for p, o in enumerate(_pairs_attend(jobs, masks)):
            o_ref[:, p * LANES:(p + 1) * LANES] = o.astype(bf16)

    pl.when(i == 0)(functools.partial(latent, 0))
    pl.when(jnp.logical_and(i > 0, i < n_lat - 1))(functools.partial(latent, 1))
    pl.when(i == n_lat - 1)(functools.partial(latent, 2))

    @pl.when(i >= n_lat)
    def _():
        for p in range(NA_W // LANES):
            sl = slice(p * LANES, (p + 1) * LANES)
            ctx = (k_ref[S:T, sl], v_ref[S:T, sl], None)
            o_ref[:, sl] = _pair_attend(q_ref[:, sl], [ctx], masks).astype(bf16)


def _na(qka, proj, rpb, S):
    B, T, _ = qka.shape
    nt, n_lat, rows = T // TM, S // TM, S // GRID_W
    assert n_lat >= 3
    offs, front, back, row_mask = _na_geometry(rows)
    tall = _na_bias(rpb, front, back)
    row_mask = jnp.asarray(row_mask)
    return pl.pallas_call(
        functools.partial(_na_kernel, n_lat, S, rows, offs),
        grid=(B, nt),
        in_specs=[
            pl.BlockSpec((None, TM, NA_W), lambda b, i: (b, i, A_NAQ // NA_W)),
            pl.BlockSpec((None, T, NA_W), lambda b, i: (b, 0, A_NAK // NA_W)),
            pl.BlockSpec((None, T, NA_W), lambda b, i: (b, 0, P_NAV // NA_W)),
            pl.BlockSpec(tall.shape, lambda b, i: (0, 0, 0, 0)),
            pl.BlockSpec(row_mask.shape, lambda b, i: (0, 0, 0)),
        ],
        out_specs=pl.BlockSpec((None, TM, NA_W), lambda b, i: (b, i, 0)),
        out_shape=jax.ShapeDtypeStruct((B, T, NA_W), bf16),
        compiler_params=_params("arbitrary", "arbitrary"),
        name="na_attn",
    )(qka, qka, proj, tall, row_mask)


def _na_geometry(rows):
    n_tiles = rows // NA_TILE_ROWS
    band = np.arange(NA_BAND_ROWS)
    dr0 = np.zeros((3, NA_TILE_ROWS), np.int64)
    valid = np.zeros((3, NA_TILE_ROWS, NA_BAND_ROWS), bool)
    for ty, rt in enumerate((0, 1, n_tiles - 1)):
        bs = int(np.clip(NA_TILE_ROWS * rt - NA_WIN_ROWS // 2, 0, rows - NA_BAND_ROWS))
        for qr in range(NA_TILE_ROWS):
            r = NA_TILE_ROWS * rt + qr
            s_r = int(np.clip(r - NA_WIN_ROWS // 2, 0, rows - NA_WIN_ROWS))
            dr0[ty, qr] = bs - r + NA_WIN_ROWS - 1
            valid[ty, qr] = (bs + band >= s_r) & (bs + band < s_r + NA_WIN_ROWS)
    front = int(max(0, -dr0.min()))
    back = int(max(0, dr0.max() + NA_BAND_ROWS - (2 * NA_WIN_ROWS - 1)))
    row_mask = np.where(np.repeat(valid, GRID_W, axis=2), 0.0, NEG).astype(np.float32)
    return dr0 + front, front, back, row_mask


def _na_bias(rpb, front, back):
    col = np.arange(GRID_W)
    c0 = np.clip(col - NA_WIN_COLS // 2, 0, GRID_W - NA_WIN_COLS)
    col_ok = (col[None, :] >= c0[:, None]) & (col[None, :] < c0[:, None] + NA_WIN_COLS)
    dc = np.clip(col[None, :] - col[:, None] + NA_WIN_COLS - 1, 0, 2 * NA_WIN_COLS - 2)
    pick = jnp.asarray(np.arange(2 * NA_WIN_COLS - 1)[:, None, None] == dc[None], f32)
    tz = jnp.einsum("hrd,dqk->hrqk", rpb, pick, precision=lax.Precision.HIGHEST)
    tz = jnp.where(col_ok, tz * LOG2E, NEG)
    n_blocks = front + tz.shape[1] + back
    width = pl.cdiv(n_blocks * GRID_W, LANES) * LANES
    tall = tz.transpose(0, 2, 1, 3).reshape(NA_HEADS, GRID_W, -1)
    tall = jnp.pad(tall, ((0, 0), (0, 0), (front * GRID_W, width + GRID_W - (front + tz.shape[1]) * GRID_W)),
                   constant_values=NEG)
    return jnp.stack([tall[:, :, :width], tall[:, :, GRID_W:GRID_W + width]], axis=1)


def _mla_kernel(n_q, S, q_ref, k_ref, v_ref, o_ref):
    i = pl.program_id(2)
    T = k_ref.shape[0]

    def attend(q, bounds):
        m_run = acc = None
        for lo, hi in bounds:
            s = _nt(q, k_ref[lo:hi, :])
            m_new = jnp.max(s, axis=-1, keepdims=True)
            if m_run is not None:
                m_new = jnp.maximum(m_run, m_new)
            pv = _mm(jnp.exp2(s - m_new).astype(bf16), v_ref[lo:hi, :])
            acc = pv if acc is None else acc * jnp.exp2(m_run - m_new) + pv
            m_run = m_new
        return (acc[:, :MLA_V] / acc[:, MLA_V:MLA_V + 1]).astype(bf16)

    @pl.when(i < n_q)
    def _():
        chunks = [(lo, lo + MLA_TK) for lo in range(0, S, MLA_TK)] + [(S, T)]
        o_ref[...] = attend(q_ref[...], chunks)

    @pl.when(i >= n_q)
    def _():
        o_ref[0:T - S, :] = attend(q_ref[0:T - S, :], [(S, T)])


def _mla(mq, mk, mv, S):
    B, T, _ = mq.shape
    tq = min(MLA_TQ, S)
    n_q = S // tq
    assert S % tq == 0 and S % MLA_TK == 0 and T - S <= tq
    return pl.pallas_call(
        functools.partial(_mla_kernel, n_q, S),
        grid=(B, MLA_HEADS, n_q + 1),
        in_specs=[
            pl.BlockSpec((None, tq, MLA_PAD), lambda b, h, i: (b, i, h)),
            pl.BlockSpec((None, T, MLA_PAD), lambda b, h, i: (b, 0, h)),
            pl.BlockSpec((None, T, MLA_PAD), lambda b, h, i: (b, 0, h)),
        ],
        out_specs=pl.BlockSpec((None, tq, MLA_V), lambda b, h, i: (b, i, h)),
        out_shape=jax.ShapeDtypeStruct((B, T, MLA_W), bf16),
        compiler_params=_params("arbitrary", "arbitrary", "arbitrary"),
        name="mla_attn",
    )(mq, mk, mv)


def _swa_kernel(n_lat, S, sink_ref, q_ref, k_ref, v_ref, band_ref, o_ref):
    i = pl.program_id(1)
    masks = _half_masks()
    T = k_ref.shape[0]
    kc, vc = k_ref[S:T, :], v_ref[S:T, :]

    def sinks(p):
        return (sink_ref[SWA_HEAD_ORDER[2 * p]], sink_ref[SWA_HEAD_ORDER[2 * p + 1]])

    @pl.when(i < n_lat)
    def _():
        start = pl.multiple_of(jnp.clip(i * TM - SWA_WINDOW, 0, S - SWA_KEYS), SWA_WINDOW)
        band = band_ref[jnp.where(i == 0, 0, jnp.where(i == n_lat - 1, 2, 1))]
        kb, vb = k_ref[pl.ds(start, SWA_KEYS), :], v_ref[pl.ds(start, SWA_KEYS), :]
        jobs = [(q_ref[:, p * LANES:(p + 1) * LANES], [(kb, vb, band), (kc, vc, None)], sinks(p))
                for p in range(SWA_W // LANES)]
        for p, o in enumerate(_pairs_attend(jobs, masks)):
            o_ref[:, p * LANES:(p + 1) * LANES] = o.astype(bf16)

    @pl.when(i >= n_lat)
    def _():
        for p in range(SWA_W // LANES):
            sl = slice(p * LANES, (p + 1) * LANES)
            o_ref[:, sl] = _pair_attend(q_ref[:, sl], [(kc, vc, None)], masks, sinks(p)).astype(bf16)


def _swa_band_masks(S):
    n_lat = S // TM
    out = np.zeros((3, 2 * TM, SWA_KEYS), np.float32)
    for ty, i in enumerate((0, 1, n_lat - 1)):
        start = int(np.clip(i * TM - SWA_WINDOW, 0, S - SWA_KEYS))
        qpos = i * TM + np.tile(np.arange(TM), 2)
        kpos = start + np.arange(SWA_KEYS)
        out[ty] = np.where(np.abs(qpos[:, None] - kpos[None, :]) <= SWA_WINDOW, 0.0, NEG)
    return jnp.asarray(out)


def _swa(qka, proj, sink, S):
    B, T, _ = qka.shape
    nt, n_lat = T // TM, S // TM
    assert n_lat >= 3
    band = _swa_band_masks(S)
    return pl.pallas_call(
        functools.partial(_swa_kernel, n_lat, S),
        grid=(B, nt),
        in_specs=[
            pl.BlockSpec(memory_space=pltpu.SMEM),
            pl.BlockSpec((None, TM, SWA_W), lambda b, i: (b, i, A_SWQ // SWA_W)),
            pl.BlockSpec((None, T, SWA_KV_W), lambda b, i: (b, 0, A_SWK // SWA_KV_W)),
            pl.BlockSpec((None, T, SWA_KV_W), lambda b, i: (b, 0, P_SWV // SWA_KV_W)),
            pl.BlockSpec(band.shape, lambda b, i: (0, 0, 0)),
        ],
        out_specs=pl.BlockSpec((None, TM, SWA_W), lambda b, i: (b, i, 0)),
        out_shape=jax.ShapeDtypeStruct((B, T, SWA_W), bf16),
        compiler_params=_params("arbitrary", "arbitrary"),
        name="swa_attn",
    )(sink, qka, qka, proj, band)


def _post_kernel(ona_ref, omla_ref, oswa_ref, gl_ref, x_ref, mod_ref, wna_ref, wmla_ref, wswa_ref, wo_ref,
                 g2_ref, rt_ref, xo_ref, h2_ref, aff_ref, affx_ref):
    D = x_ref.shape[1]
    n_e = aff_ref.shape[1]
    n_sub = 2
    sub = x_ref.shape[0] // n_sub
    halves = [slice(r * sub, (r + 1) * sub) for r in range(n_sub)]
    gated = []
    for rows in halves:
        merged = None
        for j, (o_ref, w_ref) in enumerate(((ona_ref, wna_ref), (omla_ref, wmla_ref), (oswa_ref, wswa_ref))):
            y = _mm(o_ref[rows, :], w_ref[...])
            gy = (jnp.tanh(gl_ref[rows, j * D:(j + 1) * D].astype(f32)) + 1.0) * y
            merged = gy if merged is None else merged + gy
        gated.append(merged.astype(bf16))
    mixed = [_mm(m, wo_ref[...]) for m in gated]
    for rows, res in zip(halves, mixed):
        x = x_ref[rows, :] + mod_ref[2:3, :] * res
        xo_ref[rows, :] = x
        ms = jnp.mean(x * x, axis=-1, keepdims=True)
        h2 = x * lax.rsqrt(ms + EPS) * (g2_ref[...] * (1.0 + mod_ref[4:5, :])) + mod_ref[3:4, :]
        h_hi = h2.astype(bf16)
        h2_ref[rows, :] = h_hi
        h_lo = (h2 - h_hi.astype(f32)).astype(bf16)
        parts = _mm(h_hi, rt_ref[...]) + _mm(h_lo, rt_ref[...])
        logits = parts[:, :n_e] + parts[:, n_e:]
        e = jnp.exp(logits - jnp.max(logits, axis=1, keepdims=True))
        aff = e / jnp.sum(e, axis=1, keepdims=True)
        aff_ref[rows, :] = aff
        p0 = aff.astype(bf16)
        r1 = aff - p0.astype(f32)
        p1 = r1.astype(bf16)
        p2 = (r1 - p1.astype(f32)).astype(bf16)
        pad = jnp.zeros((sub, LANES - 3 * n_e), bf16)
        affx_ref[rows, :] = jnp.concatenate([p0, p1, p2, pad], axis=1)


def _post(ona, omla, oswa, gl, x, mod, lw, n_lat):
    B, T, D = x.shape
    nt = pl.cdiv(T, TT)
    E = lw["router"].shape[1] // 2

    def tok(w):
        return pl.BlockSpec((None, TT, w), lambda b, i: (b, i, 0))

    def const(a):
        return pl.BlockSpec(a.shape, lambda b, i: (0,) * a.ndim)

    consts = [lw["wna"], lw["wmla"], lw["wswa"], lw["wo"], lw["g2"], lw["router"]]
    return pl.pallas_call(
        _post_kernel,
        grid=(B, nt),
        in_specs=[tok(NA_W), tok(MLA_W), tok(SWA_W), tok(N_BRANCH * D), tok(D), _mod_spec(n_lat, D)]
        + [const(a) for a in consts],
        out_specs=[tok(D), tok(D), tok(E), tok(LANES)],
        out_shape=[jax.ShapeDtypeStruct((B, T, D), f32), jax.ShapeDtypeStruct((B, T, D), bf16),
                   jax.ShapeDtypeStruct((B, T, E), f32), jax.ShapeDtypeStruct((B, T, LANES), bf16)],
        compiler_params=_params("arbitrary", "arbitrary"),
        name="post_attn",
    )(ona, omla, oswa, gl, x, mod, *consts)


def _lane_cumsum(mask, tri):
    E, n = mask.shape
    carry = jnp.zeros((E, 1), f32)
    outs = []
    for k in range(n // LANES):
        w = _mm(mask[:, k * LANES:(k + 1) * LANES].astype(bf16), tri) + carry
        outs.append(w)
        carry = w[:, LANES - 1:LANES]
    return jnp.concatenate(outs, axis=1)


def _select_slots(aff, cap, base, tri):
    bits = lax.bitcast_convert_type(aff, jnp.int32)
    thr = jnp.zeros((aff.shape[0], 1), jnp.int32)
    for bit in range(30, -1, -1):
        cand = thr | (1 << bit)
        cnt = jnp.sum(jnp.where(bits >= cand, 1.0, 0.0), axis=1, keepdims=True)
        thr = jnp.where(cnt >= cap, cand, thr)
    gt = jnp.where(bits > thr, 1.0, 0.0)
    eq = jnp.where(bits == thr, 1.0, 0.0)
    need = cap - jnp.sum(gt, axis=1, keepdims=True)
    sel = jnp.maximum(gt, jnp.where(_lane_cumsum(eq, tri) <= need, eq, 0.0))
    return jnp.where(sel > 0.0, _lane_cumsum(sel, tri) + base, 0.0)


def _topk_kernel(S, cap_s, cap_l, aff_ref, tri_ref, before_ref, cp_ref, bnd_ref):
    T = aff_ref.shape[1]
    tri = tri_ref[...]
    cp_ref[:, 0:S] = _select_slots(aff_ref[:, 0:S], cap_s, 0.0, tri)
    cp_ref[:, S:T] = _select_slots(aff_ref[:, S:T], cap_l, float(cap_s), tri)
    bnd_ref[...] = _mm(jnp.where(cp_ref[...] > 0.0, 1.0, 0.0).astype(bf16), before_ref[...])


def _topk(aff, S, cap_s, cap_l):
    B, E, T = aff.shape
    tri = jnp.asarray(np.triu(np.ones((LANES, LANES), np.float32)), bf16)
    before = jnp.asarray(np.arange(T)[:, None] < MOE_CHUNK * np.arange(LANES)[None, :], bf16)
    return pl.pallas_call(
        functools.partial(_topk_kernel, S, cap_s, cap_l),
        grid=(B,),
        in_specs=[pl.BlockSpec((None, E, T), lambda b: (b, 0, 0)),
                  pl.BlockSpec((LANES, LANES), lambda b: (0, 0)),
                  pl.BlockSpec((T, LANES), lambda b: (0, 0))],
        out_specs=[pl.BlockSpec((None, E, T), lambda b: (b, 0, 0)),
                   pl.BlockSpec((None, E, LANES), lambda b: (b, 0, 0))],
        out_shape=[jax.ShapeDtypeStruct((B, E, T), f32), jax.ShapeDtypeStruct((B, E, LANES), f32)],
        compiler_params=_params("arbitrary"),
        name="expert_select",
    )(aff, tri, before)


def _window_count(lo, hi, r0):
    return jnp.where(hi > lo, lax.div(hi - r0 + (MOE_WIN - 1), MOE_WIN), 0)


def _gather_kernel(n_e, rows, bnd_ref, h2_ref, affx_ref, cp_ref, xg_ref):
    b, k = pl.program_id(0), pl.program_id(1)
    D = h2_ref.shape[1]

    @pl.when(k == 0)
    def _():
        xg_ref[...] = jnp.zeros_like(xg_ref)

    ex = lax.broadcasted_iota(jnp.int32, (n_e, 1), 0)
    starts = []
    start_col = jnp.zeros((n_e, 1), f32)
    for e in range(n_e):
        r0 = lax.div(bnd_ref[(b * n_e + e) * LANES + k], 16) * 16
        starts.append(r0)
        start_col = start_col + jnp.where(ex == e, r0.astype(f32), 0.0)
    rel = cp_ref[...] - start_col
    rel_rows = jnp.broadcast_to(rel[:, None, :], (n_e, MOE_WIN, MOE_CHUNK)).reshape(n_e * MOE_WIN, MOE_CHUNK)
    j1 = lax.broadcasted_iota(jnp.int32, (n_e, MOE_WIN, 1), 1).reshape(n_e * MOE_WIN, 1) + 1
    onehot = jnp.where(rel_rows == j1.astype(f32), 1.0, 0.0).astype(bf16)
    tok, tok_aff = h2_ref[...], affx_ref[...]
    moved = _mm(onehot, tok).astype(bf16)
    moved_aff = _mm(onehot, tok_aff).astype(bf16)
    for e in range(n_e):
        dst = pl.ds(pl.multiple_of(e * rows + starts[e], 16), MOE_WIN)
        xg_ref[dst, 0:D] += moved[e * MOE_WIN:(e + 1) * MOE_WIN]
        xg_ref[dst, D:D + LANES] += moved_aff[e * MOE_WIN:(e + 1) * MOE_WIN]

    sub1 = lax.broadcasted_iota(jnp.int32, (MOE_WIN, 1), 0) + 1
    for e in range(n_e):
        lo, hi = bnd_ref[(b * n_e + e) * LANES + k], bnd_ref[(b * n_e + e) * LANES + k + 1]
        n_win = _window_count(lo, hi, starts[e])

        @pl.when(n_win > 1)
        def _(e=e, n_win=n_win):
            def extra(w, c):
                hit = rel[e:e + 1, :] == (sub1 + w * MOE_WIN).astype(f32)
                oh = jnp.where(hit, 1.0, 0.0).astype(bf16)
                dst = pl.ds(pl.multiple_of(e * rows + starts[e] + w * MOE_WIN, 16), MOE_WIN)
                xg_ref[dst, 0:D] += _mm(oh, tok).astype(bf16)
                xg_ref[dst, D:D + LANES] += _mm(oh, tok_aff).astype(bf16)
                return c

            lax.fori_loop(1, n_win, extra, 0)


def _gather(h2, affx, cp, bnd, n_slots):
    B, T, D = h2.shape
    E = cp.shape[1]
    nc = T // MOE_CHUNK
    rows = n_slots + MOE_WIN
    assert n_slots % 16 == 0 and nc + 1 <= LANES and 3 * E <= LANES
    return pl.pallas_call(
        functools.partial(_gather_kernel, E, rows),
        grid_spec=pltpu.PrefetchScalarGridSpec(
            num_scalar_prefetch=1,
            grid=(B, nc),
            in_specs=[
                pl.BlockSpec((None, MOE_CHUNK, D), lambda b, k, s: (b, k, 0)),
                pl.BlockSpec((None, MOE_CHUNK, LANES), lambda b, k, s: (b, k, 0)),
                pl.BlockSpec((None, E, MOE_CHUNK), lambda b, k, s: (b, 0, k)),
            ],
            out_specs=pl.BlockSpec((None, E * rows, D + LANES), lambda b, k, s: (b, 0, 0)),
        ),
        out_shape=jax.ShapeDtypeStruct((B, E * rows, D + LANES), bf16),
        compiler_params=_params("arbitrary", "arbitrary"),
        name="moe_gather",
    )(bnd, h2, affx, cp)


def _ffn_kernel(n_slots, n_e, xg_ref, wg_ref, wu_ref, wd_ref, y_ref, wg_s, wu_s, wd_s):
    e, b = pl.program_id(0), pl.program_id(1)
    D = wg_ref.shape[0]

    @pl.when(b == 0)
    def _():
        wg_s[...] = wg_ref[...].astype(bf16)
        wu_s[...] = wu_ref[...].astype(bf16)
        wd_s[...] = wd_ref[...].astype(bf16)

    n_b = xg_ref.shape[0]
    xg = jnp.concatenate([xg_ref[s, 0:n_slots, 0:D] for s in range(n_b)], axis=0)
    a = _mm(xg, wg_s[...])
    u = _mm(xg, wu_s[...])
    act = (a * jax.nn.sigmoid(a) * u).astype(bf16)
    lane = lax.broadcasted_iota(jnp.int32, (1, LANES), 1)
    mine = jnp.logical_and(lax.rem(lane, n_e) == e, lane < 3 * n_e)
    parts = jnp.concatenate([xg_ref[s, 0:n_slots, D:D + LANES] for s in range(n_b)], axis=0).astype(f32)
    gate = jnp.sum(jnp.where(mine, parts, 0.0), axis=1, keepdims=True)
    y = (_mm(act, wd_s[...]) * gate).astype(bf16)
    for s in range(n_b):
        y_ref[s, 0:n_slots, :] = y[s * n_slots:(s + 1) * n_slots]
        y_ref[s, n_slots:, :] = jnp.zeros((y_ref.shape[1] - n_slots, y_ref.shape[2]), bf16)


def _ffn(xg, w_gate, w_up, w_down, layer, n_slots):
    B = xg.shape[0]
    _, E, D, F = w_gate.shape
    rows = xg.shape[1] // E
    n_b = 2 if B % 2 == 0 else 1
    return pl.pallas_call(
        functools.partial(_ffn_kernel, n_slots, E),
        grid=(E, B // n_b),
        in_specs=[
            pl.BlockSpec((n_b, None, rows, D + LANES), lambda e, b: (b, e, 0, 0)),
            pl.BlockSpec((None, None, D, F), lambda e, b: (layer, e, 0, 0)),
            pl.BlockSpec((None, None, D, F), lambda e, b: (layer, e, 0, 0)),
            pl.BlockSpec((None, None, F, D), lambda e, b: (layer, e, 0, 0)),
        ],
        out_specs=pl.BlockSpec((n_b, None, rows, D), lambda e, b: (b, e, 0, 0)),
        out_shape=jax.ShapeDtypeStruct((B, E, rows, D), bf16),
        scratch_shapes=[pltpu.VMEM((D, F), bf16), pltpu.VMEM((D, F), bf16), pltpu.VMEM((F, D), bf16)],
        compiler_params=_params("arbitrary", "arbitrary"),
        name="moe_ffn",
    )(xg.reshape(B, E, rows, D + LANES), w_gate, w_up, w_down)


def _combine_kernel(n_e, rows, residual, bnd_ref, y_ref, cpt_ref, expand_ref, *refs):
    o_ref = refs[-1]
    b, k = pl.program_id(0), pl.program_id(1)
    cpt = cpt_ref[...]
    hi_part = jnp.floor(cpt * (1.0 / 32.0))
    lo_part = cpt - 32.0 * hi_part
    expand = expand_ref[...]
    rank = 32.0 * _mm(hi_part.astype(bf16), expand) + _mm(lo_part.astype(bf16), expand)
    lane = lax.broadcasted_iota(jnp.int32, (1, n_e * MOE_WIN), 1)
    lane_e = lax.div(lane, MOE_WIN)
    starts = []
    tgt = (lane - lane_e * MOE_WIN + 1).astype(f32)
    for e in range(n_e):
        r0 = lax.div(bnd_ref[(b * n_e + e) * LANES + k], 16) * 16
        starts.append(r0)
        tgt = tgt + jnp.where(lane_e == e, r0.astype(f32), 0.0)
    onehot = jnp.where(rank == tgt, 1.0, 0.0).astype(bf16)
    ycat = jnp.concatenate(
        [y_ref[pl.ds(pl.multiple_of(e * rows + starts[e], 16), MOE_WIN), :] for e in range(n_e)], axis=0)
    o_ref[...] = _mm(onehot, ycat)

    lane_w = lax.broadcasted_iota(jnp.int32, (1, MOE_WIN), 1)
    for e in range(n_e):
        lo, hi = bnd_ref[(b * n_e + e) * LANES + k], bnd_ref[(b * n_e + e) * LANES + k + 1]
        n_win = _window_count(lo, hi, starts[e])

        @pl.when(n_win > 1)
        def _(e=e, n_win=n_win):
            rank_e = rank[:, e * MOE_WIN:(e + 1) * MOE_WIN]

            def extra(w, c):
                r = pl.multiple_of(starts[e] + w * MOE_WIN, 16)
                hit = rank_e == (lane_w + (r + 1)).astype(f32)
                o_ref[...] += _mm(jnp.where(hit, 1.0, 0.0).astype(bf16),
                                  y_ref[pl.ds(pl.multiple_of(e * rows + r, 16), MOE_WIN), :])
                return c

            lax.fori_loop(1, n_win, extra, 0)

    if residual:
        x_ref, mod_ref = refs[0], refs[1]
        o_ref[...] = x_ref[...] + mod_ref[5:6, :] * o_ref[...]


def _combine(y, cp, bnd, x=None, mod=None, n_tok=None):
    B, E, rows, D = y.shape
    assert rows <= 32 * 256
    residual = x is not None
    n_tok = n_tok if residual else cp.shape[2]
    expand = jnp.asarray(np.kron(np.eye(E), np.ones((1, MOE_WIN))), bf16)
    chunk = pl.BlockSpec((None, MOE_CHUNK, D), lambda b, k, s: (b, k, 0))
    in_specs = [
        pl.BlockSpec((None, E * rows, D), lambda b, k, s: (b, 0, 0)),
        pl.BlockSpec((None, MOE_CHUNK, E), lambda b, k, s: (b, k, 0)),
        pl.BlockSpec(expand.shape, lambda b, k, s: (0, 0)),
    ]
    args = [bnd, y.reshape(B, E * rows, D), cp.transpose(0, 2, 1), expand]
    if residual:
        in_specs += [chunk, pl.BlockSpec((None, None, 6, D), lambda b, k, s: (b, 0, 0, 0))]
        args += [x, mod]
    return pl.pallas_call(
        functools.partial(_combine_kernel, E, rows, residual),
        grid_spec=pltpu.PrefetchScalarGridSpec(
            num_scalar_prefetch=1,
            grid=(B, n_tok // MOE_CHUNK),
            in_specs=in_specs,
            out_specs=chunk,
        ),
        out_shape=jax.ShapeDtypeStruct((B, n_tok, D), f32),
        compiler_params=_params("arbitrary", "arbitrary"),
        name="moe_combine",
    )(*args)


def _rope_tables(S, L):
    t = jnp.arange(S)
    row, col = t // GRID_W, t % GRID_W
    f = 16
    inv = ROPE_BASE ** (-jnp.arange(f, dtype=f32) / f)
    ar = row.astype(f32)[:, None] * inv
    ac = col.astype(f32)[:, None] * inv
    cos = jnp.concatenate([jnp.cos(ar), jnp.cos(ar), jnp.cos(ac), jnp.cos(ac)], axis=1)
    sin = jnp.concatenate([-jnp.sin(ar), jnp.sin(ar), -jnp.sin(ac), jnp.sin(ac)], axis=1)
    cos = jnp.concatenate([jnp.tile(cos, (1, 2)), jnp.ones((L, LANES), f32)], axis=0)
    sin = jnp.concatenate([jnp.tile(sin, (1, 2)), jnp.zeros((L, LANES), f32)], axis=0)
    return cos, sin


def _pad_heads(w, n_heads, width, padded):
    lead = w.shape[:-1]
    w = w.reshape(lead + (n_heads, width))
    w = jnp.pad(w, [(0, 0)] * len(lead) + [(0, 0), (0, padded - width)])
    return w.reshape(lead + (n_heads * padded,))


def _layer_weights(l, D, w_in, norm2_g, na_q_g, na_k_g, mla_q_norm, mla_w_uq, mla_kv_norm, mla_w_ukv,
                   mla_q_g, mla_k_g, swa_q_g, swa_k_g, w_na_o, w_mla_o, w_swa_o, w_o, router):
    o = np.cumsum((0, NA_W, NA_W, NA_W, MLA_Q_RANK, MLA_KV_RANK, MLA_ROPE, SWA_W, SWA_KV_W, SWA_KV_W))
    o_naq, o_nak, o_nav, o_cq, o_ckv, o_kr, o_swq, o_swk, o_swv, o_gl = (int(v) for v in o)
    wi = w_in[l]
    swq = wi[:, o_swq:o_swq + SWA_W].reshape(D, SWA_HEADS, SWA_DIM)[:, SWA_HEAD_ORDER, :].reshape(D, SWA_W)
    kr = jnp.pad(wi[:, o_kr:o_kr + MLA_ROPE], ((0, 0), (0, LANES - MLA_ROPE)))
    w_all = jnp.concatenate([
        wi[:, o_naq:o_naq + 3 * NA_W], swq, wi[:, o_cq:o_cq + MLA_Q_RANK], wi[:, o_ckv:o_ckv + MLA_KV_RANK], kr,
        wi[:, o_swk:o_swk + 2 * SWA_KV_W], 0.5 * wi[:, o_gl:]], axis=1).astype(bf16)
    ukv = mla_w_ukv[l].reshape(MLA_KV_RANK, MLA_HEADS, MLA_NOPE + MLA_V)
    swa_o = w_swa_o[l].reshape(SWA_HEADS, SWA_DIM, D)[SWA_HEAD_ORDER, :, :].reshape(SWA_W, D)
    r_hi = router[l].astype(bf16)
    r_lo = (router[l] - r_hi.astype(f32)).astype(bf16)
    return dict(
        w_all=w_all,
        naq_g=jnp.tile(na_q_g[l] * (NA_DIM ** -0.5 * LOG2E), NA_HEADS)[None],
        nak_g=jnp.tile(na_k_g[l], NA_HEADS)[None],
        swq_g=jnp.tile(swa_q_g[l] * (SWA_DIM ** -0.5 * LOG2E), SWA_HEADS)[None],
        swk_g=jnp.tile(swa_k_g[l], SWA_KV_HEADS)[None],
        qn=mla_q_norm[l][None], kvn=mla_kv_norm[l][None],
        wuq=_pad_heads(mla_w_uq[l], MLA_HEADS, MLA_QK, MLA_PAD).astype(bf16),
        wuk=ukv[:, :, :MLA_NOPE].reshape(MLA_KV_RANK, MLA_HEADS * MLA_NOPE).astype(bf16),
        wuv=ukv[:, :, MLA_NOPE:].reshape(MLA_KV_RANK, MLA_W).astype(bf16),
        qg=jnp.tile(jnp.pad(mla_q_g[l] * (MLA_QK ** -0.5 * LOG2E), (0, MLA_PAD - MLA_QK)), MLA_HEADS)[None],
        kg=jnp.tile(jnp.pad(mla_k_g[l], (0, MLA_PAD - MLA_QK)), MLA_HEADS)[None],
        wna=w_na_o[l].astype(bf16), wmla=w_mla_o[l].astype(bf16), wswa=swa_o.astype(bf16),
        wo=(0.5 * w_o[l]).astype(bf16), g2=norm2_g[l][None], router=jnp.concatenate([r_hi, r_lo], axis=1),
    )


def kernel(x, c, ctx, c_ctx, norm1_g, norm2_g, w_ada, b_ada, w_in, na_q_g, na_k_g, na_rpb, mla_q_norm, mla_w_uq, mla_kv_norm, mla_w_ukv, mla_q_g, mla_k_g, swa_q_g, swa_k_g, swa_sink, w_na_o, w_mla_o, w_swa_o, w_o, router, w_gate, w_up, w_down):
    B, S, D = x.shape
    L = ctx.shape[1]
    depth = w_in.shape[0]
    E = router.shape[2]
    T = S + L
    n_lat = S // TT
    rows = S // GRID_W
    assert S % TT == 0 and L % TM == 0 and L <= TT and rows >= NA_BAND_ROWS and S >= SWA_KEYS and B < 16
    cap_s = CAPACITY_FACTOR * S // E
    cap_l = CAPACITY_FACTOR * L // E

    cpad = jnp.zeros((16, D), f32).at[:B].set(c).at[B].set(c_ctx)
    mod = _ada(cpad, w_ada, b_ada).reshape(depth, 16, 6, D)
    mod = jnp.stack([mod[:, :B], jnp.broadcast_to(mod[:, B:B + 1], (depth, B, 6, D))], axis=2)

    cos, sin = _rope_tables(S, L)
    gmat = jnp.asarray(np.kron(np.eye(NA_HEADS), np.full((NA_DIM, NA_DIM), 1.0 / NA_DIM)), bf16)

    xs = jnp.concatenate([x, ctx], axis=1)
    moe = None
    for l in range(depth):
        lw = _layer_weights(l, D, w_in, norm2_g, na_q_g, na_k_g, mla_q_norm, mla_w_uq, mla_kv_norm, mla_w_ukv,
                            mla_q_g, mla_k_g, swa_q_g, swa_k_g, w_na_o, w_mla_o, w_swa_o, w_o, router)
        modp = mod[l - 1] if l > 0 else None
        xs, proj, gl = _in_proj(xs, moe, modp, mod[l], norm1_g[l][None], lw["w_all"], n_lat)
        qka, mq, mk, mv = _prep(proj, cos, sin, gmat, lw)
        ona = _na(qka, proj, na_rpb[l], S)
        omla = _mla(mq, mk, mv, S)
        oswa = _swa(qka, proj, swa_sink[l] * LOG2E, S)
        xs, h2, aff, affx = _post(ona, omla, oswa, gl, xs, mod[l], lw, n_lat)
        cp, bnd = _topk(aff.transpose(0, 2, 1), S, cap_s, cap_l)
        bnd = bnd.astype(jnp.int32).reshape(-1)
        xg = _gather(h2, affx, cp, bnd, cap_s + cap_l)
        y = _ffn(xg, w_gate, w_up, w_down, l, cap_s + cap_l)
        if l == depth - 1:
            return _combine(y, cp, bnd, xs, mod[l], S)
        moe = _combine(y, cp, bnd)
```

```python
import functools

import numpy as np
import jax
import jax.numpy as jnp
from jax import lax
from jax.experimental import pallas as pl
from jax.experimental.pallas import tpu as pltpu

GRID_W = 64
NA_HEADS, NA_DIM, NA_WIN_ROWS, NA_WIN_COLS = 6, 64, 8, 16
MLA_HEADS, MLA_Q_RANK, MLA_KV_RANK, MLA_NOPE, MLA_ROPE, MLA_V = 4, 256, 128, 128, 64, 128
MLA_QK = MLA_NOPE + MLA_ROPE
MLA_PAD = 256
SWA_HEADS, SWA_KV_HEADS, SWA_DIM, SWA_WINDOW = 6, 2, 64, 128
CAPACITY_FACTOR = 2
N_BRANCH = 3
ROPE_BASE = 10000.0
EPS = 1e-6

NA_W = NA_HEADS * NA_DIM
SWA_W = SWA_HEADS * SWA_DIM
SWA_KV_W = SWA_KV_HEADS * SWA_DIM
MLA_W = MLA_HEADS * MLA_V

LANES = 128
TM = 256
TT = 512
MLA_TQ = 1024
MLA_TK = 512
LOG2E = 1.4426950408889634
NA_TILE_ROWS = TM // GRID_W
NA_BAND_ROWS = NA_TILE_ROWS + NA_WIN_ROWS - 1
NA_BAND = NA_BAND_ROWS * GRID_W
SWA_KEYS = TM + 2 * SWA_WINDOW
MOE_CHUNK = 256
MOE_WIN = 64
NEG = -1e30
VMEM_LIMIT = 56 * 1024 * 1024

P_NAQ, P_NAK, P_NAV, P_SWQ = 0, 384, 768, 1152
P_CQ, P_CKV, P_KR, P_SWK, P_SWV = 1536, 1792, 1920, 2048, 2176
PROJ_W = 2304
N_CHUNK = 768
A_NAQ, A_NAK, A_SWQ, A_SWK = 0, 384, 768, 1152
QKA_W = 1280
SWA_HEAD_ORDER = (0, 3, 1, 4, 2, 5)

bf16 = jnp.bfloat16
f32 = jnp.float32


def _mm(a, b):
    return jnp.dot(a, b, preferred_element_type=f32)


def _nt(a, b):
    return lax.dot_general(a, b, (((1,), (1,)), ((), ())), preferred_element_type=f32)


def _params(*sem):
    return pltpu.CompilerParams(dimension_semantics=sem, vmem_limit_bytes=VMEM_LIMIT)


def _ada_kernel(c_ref, w_ref, b_ref, o_ref):
    a = c_ref[...]
    a = (a * jax.nn.sigmoid(a)).astype(bf16)
    o_ref[...] = _mm(a, w_ref[...].astype(bf16)) + b_ref[...]


def _ada(cpad, w_ada, b_ada):
    L, D, N = w_ada.shape
    tn = 1536
    return pl.pallas_call(
        _ada_kernel,
        grid=(L, N // tn),
        in_specs=[
            pl.BlockSpec((16, D), lambda l, j: (0, 0)),
            pl.BlockSpec((None, D, tn), lambda l, j: (l, 0, j)),
            pl.BlockSpec((None, 1, tn), lambda l, j: (l, 0, j)),
        ],
        out_specs=pl.BlockSpec((None, 16, tn), lambda l, j: (l, 0, j)),
        out_shape=jax.ShapeDtypeStruct((L, 16, N), f32),
        compiler_params=_params("arbitrary", "arbitrary"),
        name="ada",
    )(cpad, w_ada, b_ada.reshape(L, 1, N))


def _in_proj_kernel(has_moe, n_gl_chunks, *refs):
    if has_moe:
        x_ref, moe_ref, modp_ref, mod_ref, g_ref, w_ref, xo_ref, proj_ref, gl_ref = refs
    else:
        x_ref, mod_ref, g_ref, w_ref, proj_ref, gl_ref = refs
    n_sub = 2
    sub = x_ref.shape[0] // n_sub
    for r in range(n_sub):
        rows = slice(r * sub, (r + 1) * sub)
        x = x_ref[rows, :]
        if has_moe:
            x = x + modp_ref[5:6, :] * moe_ref[rows, :]
            xo_ref[rows, :] = x
        ms = jnp.mean(x * x, axis=-1, keepdims=True)
        h = x * lax.rsqrt(ms + EPS) * (g_ref[...] * (1.0 + mod_ref[1:2, :])) + mod_ref[0:1, :]
        h = h.astype(bf16)
        for c in range(PROJ_W // N_CHUNK):
            sl = slice(c * N_CHUNK, (c + 1) * N_CHUNK)
            proj_ref[rows, sl] = _mm(h, w_ref[:, sl]).astype(bf16)
        for c in range(n_gl_chunks):
            sl = slice(c * N_CHUNK, (c + 1) * N_CHUNK)
            gl_ref[rows, sl] = _mm(h, w_ref[:, PROJ_W + c * N_CHUNK:PROJ_W + (c + 1) * N_CHUNK]).astype(bf16)


def _mod_spec(n_lat, D):
    return pl.BlockSpec((None, None, 6, D), lambda b, i: (b, jnp.where(i >= n_lat, 1, 0), 0, 0))


def _in_proj(x, moe, modp, mod, g1, w, n_lat):
    B, T, D = x.shape
    nt = pl.cdiv(T, TT)
    has_moe = moe is not None
    n_gl_chunks = N_BRANCH * D // N_CHUNK
    tok = pl.BlockSpec((None, TT, D), lambda b, i: (b, i, 0))
    in_specs = [tok]
    args = [x]
    if has_moe:
        in_specs += [tok, _mod_spec(n_lat, D)]
        args += [moe, modp]
    in_specs += [
        _mod_spec(n_lat, D),
        pl.BlockSpec((1, D), lambda b, i: (0, 0)),
        pl.BlockSpec(w.shape, lambda b, i: (0, 0), pipeline_mode=pl.Buffered(1)),
    ]
    args += [mod, g1, w]
    out_specs = [
        pl.BlockSpec((None, TT, PROJ_W), lambda b, i: (b, i, 0)),
        pl.BlockSpec((None, TT, N_BRANCH * D), lambda b, i: (b, i, 0)),
    ]
    out_shape = [
        jax.ShapeDtypeStruct((B, T, PROJ_W), bf16),
        jax.ShapeDtypeStruct((B, T, N_BRANCH * D), bf16),
    ]
    if has_moe:
        out_specs = [tok] + out_specs
        out_shape = [jax.ShapeDtypeStruct((B, T, D), f32)] + out_shape
    res = pl.pallas_call(
        functools.partial(_in_proj_kernel, has_moe, n_gl_chunks),
        grid=(B, nt),
        in_specs=in_specs,
        out_specs=out_specs,
        out_shape=out_shape,
        compiler_params=_params("arbitrary", "arbitrary"),
        name="in_proj",
    )(*args)
    if has_moe:
        return res
    return [x] + list(res)


def _prep_kernel(proj_ref, cos_ref, sin_ref, perm_ref, gmat_ref, naq_g, nak_g, swq_g, swk_g,
                 qn_ref, wuq_ref, qg_ref, kvn_ref, wuk_ref, wuv_ref, kg_ref,
                 qka_ref, mq_ref, mk_ref, mv_ref):
    cos = cos_ref[...]
    sin = sin_ref[...]
    perm = perm_ref[...]

    def rope(x):
        hi = x.astype(bf16)
        lo = (x - hi.astype(f32)).astype(bf16)
        partner = _mm(hi, perm) + _mm(lo, perm)
        return x * cos + partner * sin

    def headnorm(x, gain):
        w = x.shape[1]
        ms = _mm((x * x).astype(bf16), gmat_ref[:w, :w])
        return x * lax.rsqrt(ms + EPS) * gain

    def rmsnorm(x, gain):
        ms = jnp.mean(x * x, axis=-1, keepdims=True)
        return x * lax.rsqrt(ms + EPS) * gain

    naq = headnorm(proj_ref[:, P_NAQ:P_NAQ + NA_W].astype(f32), naq_g[...])
    qka_ref[:, A_NAQ:A_NAQ + NA_W] = naq.astype(bf16)
    nak = headnorm(proj_ref[:, P_NAK:P_NAK + NA_W].astype(f32), nak_g[...])
    qka_ref[:, A_NAK:A_NAK + NA_W] = nak.astype(bf16)

    swq = headnorm(proj_ref[:, P_SWQ:P_SWQ + SWA_W].astype(f32), swq_g[...])
    for p in range(SWA_W // LANES):
        sl = slice(p * LANES, (p + 1) * LANES)
        qka_ref[:, A_SWQ + p * LANES:A_SWQ + (p + 1) * LANES] = rope(swq[:, sl]).astype(bf16)
    swk = headnorm(proj_ref[:, P_SWK:P_SWK + SWA_KV_W].astype(f32), swk_g[...])
    qka_ref[:, A_SWK:A_SWK + SWA_KV_W] = rope(swk).astype(bf16)

    cq = rmsnorm(proj_ref[:, P_CQ:P_CQ + MLA_Q_RANK].astype(f32), qn_ref[...]).astype(bf16)
    q = _mm(cq, wuq_ref[...])
    ckv = rmsnorm(proj_ref[:, P_CKV:P_CKV + MLA_KV_RANK].astype(f32), kvn_ref[...]).astype(bf16)
    kn = _mm(ckv, wuk_ref[...])
    v = _mm(ckv, wuv_ref[...]).astype(bf16)
    ones_col = jnp.where(lax.broadcasted_iota(jnp.int32, (v.shape[0], LANES), 1) == 0, 1.0, 0.0).astype(bf16)
    for h in range(MLA_HEADS):
        mv_ref[:, h * MLA_PAD:h * MLA_PAD + MLA_V] = v[:, h * MLA_V:(h + 1) * MLA_V]
        mv_ref[:, h * MLA_PAD + MLA_V:(h + 1) * MLA_PAD] = ones_col
    kr = proj_ref[:, P_KR:P_KR + LANES].astype(f32)
    kr_ss = jnp.sum(kr * kr, axis=-1, keepdims=True)
    for h in range(MLA_HEADS):
        o = h * MLA_PAD
        qh = q[:, o:o + MLA_PAD]
        r = lax.rsqrt(jnp.sum(qh * qh, axis=-1, keepdims=True) * (1.0 / MLA_QK) + EPS)
        qh = qh * r * qg_ref[:, o:o + MLA_PAD]
        mq_ref[:, o:o + LANES] = qh[:, :LANES].astype(bf16)
        mq_ref[:, o + LANES:o + MLA_PAD] = rope(qh[:, LANES:]).astype(bf16)
        kh = kn[:, h * MLA_NOPE:(h + 1) * MLA_NOPE]
        r = lax.rsqrt((jnp.sum(kh * kh, axis=-1, keepdims=True) + kr_ss) * (1.0 / MLA_QK) + EPS)
        mk_ref[:, o:o + LANES] = (kh * r * kg_ref[:, o:o + LANES]).astype(bf16)
        mk_ref[:, o + LANES:o + MLA_PAD] = rope(kr * r * kg_ref[:, o + LANES:o + MLA_PAD]).astype(bf16)


def _prep(proj, cos, sin, gmat, lw):
    B, T, _ = proj.shape
    nt = pl.cdiv(T, TT)

    def const(a):
        return pl.BlockSpec(a.shape, lambda b, i: (0,) * a.ndim)

    lane = np.arange(LANES)
    perm = jnp.asarray(lane[:, None] == (lane[None, :] ^ 16), bf16)
    consts = [perm, gmat, lw["naq_g"], lw["nak_g"], lw["swq_g"], lw["swk_g"], lw["qn"], lw["wuq"], lw["qg"],
              lw["kvn"], lw["wuk"], lw["wuv"], lw["kg"]]
    widths = (QKA_W, MLA_HEADS * MLA_PAD, MLA_HEADS * MLA_PAD, MLA_HEADS * MLA_PAD)
    return pl.pallas_call(
        _prep_kernel,
        grid=(B, nt),
        in_specs=[
            pl.BlockSpec((None, TT, PROJ_W), lambda b, i: (b, i, 0)),
            pl.BlockSpec((TT, LANES), lambda b, i: (i, 0)),
            pl.BlockSpec((TT, LANES), lambda b, i: (i, 0)),
        ] + [const(a) for a in consts],
        out_specs=[pl.BlockSpec((None, TT, w), lambda b, i: (b, i, 0)) for w in widths],
        out_shape=[jax.ShapeDtypeStruct((B, T, w), bf16) for w in widths],
        compiler_params=_params("arbitrary", "arbitrary"),
        name="prep",
    )(proj, cos, sin, *consts)


def _half_masks():
    lane = lax.broadcasted_iota(jnp.int32, (1, LANES), 1)
    lo = jnp.where(lane < 64, 1.0, 0.0).astype(bf16)
    return lo, (1.0 - lo.astype(f32)).astype(bf16)


def _pairs_attend(jobs, masks):
    scored = []
    for qp, parts, _ in jobs:
        q2 = jnp.concatenate([qp * masks[0], qp * masks[1]], axis=0)
        scores = []
        for k, _, bias in parts:
            s = _nt(q2, k)
            scores.append(s if bias is None else s + bias)
        scored.append(scores)
    normed = []
    for (qp, _, sinks), scores in zip(jobs, scored):
        n_q = qp.shape[0]
        m = functools.reduce(jnp.maximum, [jnp.max(s, axis=-1, keepdims=True) for s in scores])
        if sinks is not None:
            first = lax.broadcasted_iota(jnp.int32, (2 * n_q, 1), 0) < n_q
            sink = jnp.where(first, sinks[0], sinks[1])
            m = jnp.maximum(m, sink)
        ps = [jnp.exp2(s - m) for s in scores]
        l = functools.reduce(jnp.add, [jnp.sum(p, axis=-1, keepdims=True) for p in ps])
        if sinks is not None:
            l = l + jnp.exp2(sink - m)
        normed.append((ps, l))
    lane = lax.broadcasted_iota(jnp.int32, (1, LANES), 1)
    outs = []
    for (qp, parts, _), (ps, l) in zip(jobs, normed):
        n_q = qp.shape[0]
        o = functools.reduce(jnp.add, [_mm(p.astype(bf16), v) for p, (_, v, _) in zip(ps, parts)]) / l
        outs.append(jnp.where(lane < 64, o[:n_q], o[n_q:]))
    return outs


def _pair_attend(qp, parts, masks, sinks=None):
    return _pairs_attend([(qp, parts, sinks)], masks)[0]


def _na_kernel(n_lat, S, rows, offs, q_ref, k_ref, v_ref, tall_ref, rmask_ref, o_ref):
    i = pl.program_id(1)
    masks = _half_masks()
    T = k_ref.shape[0]

    def latent(ty):
        start = GRID_W * jnp.clip(NA_TILE_ROWS * i - NA_WIN_ROWS // 2, 0, rows - NA_BAND_ROWS)
        start = pl.multiple_of(start, GRID_W)
        jobs = []
        for p in range(NA_W // LANES):
            sl = slice(p * LANES, (p + 1) * LANES)
            slabs = []
            for head in (2 * p, 2 * p + 1):
                for qr in range(NA_TILE_ROWS):
                    odd = int(offs[ty, qr]) % 2
                    lo = (int(offs[ty, qr]) - odd) * GRID_W
                    slabs.append(tall_ref[head, odd, :, lo:lo + NA_BAND] + rmask_ref[ty, qr:qr + 1, :])
            band = (k_ref[pl.ds(start, NA_BAND), sl], v_ref[pl.ds(start, NA_BAND), sl],
                    jnp.concatenate(slabs, axis=0))
            ctx = (k_ref[S:T, sl], v_ref[S:T, sl], None)
            jobs.append((q_ref[:, sl], [band, ctx], None))
        for p, o in enumerate(_pairs_attend(jobs, masks)):
            o_ref[:, p * LANES:(p + 1) * LANES] = o.astype(bf16)

    pl.when(i == 0)(functools.partial(latent, 0))
    pl.when(jnp.logical_and(i > 0, i < n_lat - 1))(functools.partial(latent, 1))
    pl.when(i == n_lat - 1)(functools.partial(latent, 2))

    @pl.when(i >= n_lat)
    def _():
        for p in range(NA_W // LANES):
            sl = slice(p * LANES, (p + 1) * LANES)
            ctx = (k_ref[S:T, sl], v_ref[S:T, sl], None)
            o_ref[:, sl] = _pair_attend(q_ref[:, sl], [ctx], masks).astype(bf16)


def _na(qka, proj, rpb, S):
    B, T, _ = qka.shape
    nt, n_lat, rows = T // TM, S // TM, S // GRID_W
    assert n_lat >= 3
    offs, front, back, row_mask = _na_geometry(rows)
    tall = _na_bias(rpb, front, back)
    row_mask = jnp.asarray(row_mask)
    return pl.pallas_call(
        functools.partial(_na_kernel, n_lat, S, rows, offs),
        grid=(B, nt),
        in_specs=[
            pl.BlockSpec((None, TM, NA_W), lambda b, i: (b, i, A_NAQ // NA_W)),
            pl.BlockSpec((None, T, NA_W), lambda b, i: (b, 0, A_NAK // NA_W)),
            pl.BlockSpec((None, T, NA_W), lambda b, i: (b, 0, P_NAV // NA_W)),
            pl.BlockSpec(tall.shape, lambda b, i: (0, 0, 0, 0)),
            pl.BlockSpec(row_mask.shape, lambda b, i: (0, 0, 0)),
        ],
        out_specs=pl.BlockSpec((None, TM, NA_W), lambda b, i: (b, i, 0)),
        out_shape=jax.ShapeDtypeStruct((B, T, NA_W), bf16),
        compiler_params=_params("arbitrary", "arbitrary"),
        name="na_attn",
    )(qka, qka, proj, tall, row_mask)


def _na_geometry(rows):
    n_tiles = rows // NA_TILE_ROWS
    band = np.arange(NA_BAND_ROWS)
    dr0 = np.zeros((3, NA_TILE_ROWS), np.int64)
    valid = np.zeros((3, NA_TILE_ROWS, NA_BAND_ROWS), bool)
    for ty, rt in enumerate((0, 1, n_tiles - 1)):
        bs = int(np.clip(NA_TILE_ROWS * rt - NA_WIN_ROWS // 2, 0, rows - NA_BAND_ROWS))
        for qr in range(NA_TILE_ROWS):
            r = NA_TILE_ROWS * rt + qr
            s_r = int(np.clip(r - NA_WIN_ROWS // 2, 0, rows - NA_WIN_ROWS))
            dr0[ty, qr] = bs - r + NA_WIN_ROWS - 1
            valid[ty, qr] = (bs + band >= s_r) & (bs + band < s_r + NA_WIN_ROWS)
    front = int(max(0, -dr0.min()))
    back = int(max(0, dr0.max() + NA_BAND_ROWS - (2 * NA_WIN_ROWS - 1)))
    row_mask = np.where(np.repeat(valid, GRID_W, axis=2), 0.0, NEG).astype(np.float32)
    return dr0 + front, front, back, row_mask


def _na_bias(rpb, front, back):
    col = np.arange(GRID_W)
    c0 = np.clip(col - NA_WIN_COLS // 2, 0, GRID_W - NA_WIN_COLS)
    col_ok = (col[None, :] >= c0[:, None]) & (col[None, :] < c0[:, None] + NA_WIN_COLS)
    dc = np.clip(col[None, :] - col[:, None] + NA_WIN_COLS - 1, 0, 2 * NA_WIN_COLS - 2)
    pick = jnp.asarray(np.arange(2 * NA_WIN_COLS - 1)[:, None, None] == dc[None], f32)
    tz = jnp.einsum("hrd,dqk->hrqk", rpb, pick, precision=lax.Precision.HIGHEST)
    tz = jnp.where(col_ok, tz * LOG2E, NEG)
    n_blocks = front + tz.shape[1] + back
    width = pl.cdiv(n_blocks * GRID_W, LANES) * LANES
    tall = tz.transpose(0, 2, 1, 3).reshape(NA_HEADS, GRID_W, -1)
    tall = jnp.pad(tall, ((0, 0), (0, 0), (front * GRID_W, width + GRID_W - (front + tz.shape[1]) * GRID_W)),
                   constant_values=NEG)
    return jnp.stack([tall[:, :, :width], tall[:, :, GRID_W:GRID_W + width]], axis=1)


def _mla_kernel(n_q, S, q_ref, k_ref, v_ref, o_ref):
    i = pl.program_id(2)
    T = k_ref.shape[0]

    def attend(q, bounds):
        m_run = acc = None
        for lo, hi in bounds:
            s = _nt(q, k_ref[lo:hi, :])
            m_new = jnp.max(s, axis=-1, keepdims=True)
            if m_run is not None:
                m_new = jnp.maximum(m_run, m_new)
            pv = _mm(jnp.exp2(s - m_new).astype(bf16), v_ref[lo:hi, :])
            acc = pv if acc is None else acc * jnp.exp2(m_run - m_new) + pv
            m_run = m_new
        return (acc[:, :MLA_V] / acc[:, MLA_V:MLA_V + 1]).astype(bf16)

    @pl.when(i < n_q)
    def _():
        chunks = [(lo, lo + MLA_TK) for lo in range(0, S, MLA_TK)] + [(S, T)]
        o_ref[...] = attend(q_ref[...], chunks)

    @pl.when(i >= n_q)
    def _():
        o_ref[0:T - S, :] = attend(q_ref[0:T - S, :], [(S, T)])


def _mla(mq, mk, mv, S):
    B, T, _ = mq.shape
    tq = min(MLA_TQ, S)
    n_q = S // tq
    assert S % tq == 0 and S % MLA_TK == 0 and T - S <= tq
    return pl.pallas_call(
        functools.partial(_mla_kernel, n_q, S),
        grid=(B, MLA_HEADS, n_q + 1),
        in_specs=[
            pl.BlockSpec((None, tq, MLA_PAD), lambda b, h, i: (b, i, h)),
            pl.BlockSpec((None, T, MLA_PAD), lambda b, h, i: (b, 0, h)),
            pl.BlockSpec((None, T, MLA_PAD), lambda b, h, i: (b, 0, h)),
        ],
        out_specs=pl.BlockSpec((None, tq, MLA_V), lambda b, h, i: (b, i, h)),
        out_shape=jax.ShapeDtypeStruct((B, T, MLA_W), bf16),
        compiler_params=_params("arbitrary", "arbitrary", "arbitrary"),
        name="mla_attn",
    )(mq, mk, mv)


def _swa_kernel(n_lat, S, sink_ref, q_ref, k_ref, v_ref, band_ref, o_ref):
    i = pl.program_id(1)
    masks = _half_masks()
    T = k_ref.shape[0]
    kc, vc = k_ref[S:T, :], v_ref[S:T, :]

    def sinks(p):
        return (sink_ref[SWA_HEAD_ORDER[2 * p]], sink_ref[SWA_HEAD_ORDER[2 * p + 1]])

    @pl.when(i < n_lat)
    def _():
        start = pl.multiple_of(jnp.clip(i * TM - SWA_WINDOW, 0, S - SWA_KEYS), SWA_WINDOW)
        band = band_ref[jnp.where(i == 0, 0, jnp.where(i == n_lat - 1, 2, 1))]
        kb, vb = k_ref[pl.ds(start, SWA_KEYS), :], v_ref[pl.ds(start, SWA_KEYS), :]
        jobs = [(q_ref[:, p * LANES:(p + 1) * LANES], [(kb, vb, band), (kc, vc, None)], sinks(p))
                for p in range(SWA_W // LANES)]
        for p, o in enumerate(_pairs_attend(jobs, masks)):
            o_ref[:, p * LANES:(p + 1) * LANES] = o.astype(bf16)

    @pl.when(i >= n_lat)
    def _():
        for p in range(SWA_W // LANES):
            sl = slice(p * LANES, (p + 1) * LANES)
            o_ref[:, sl] = _pair_attend(q_ref[:, sl], [(kc, vc, None)], masks, sinks(p)).astype(bf16)


def _swa_band_masks(S):
    n_lat = S // TM
    out = np.zeros((3, 2 * TM, SWA_KEYS), np.float32)
    for ty, i in enumerate((0, 1, n_lat - 1)):
        start = int(np.clip(i * TM - SWA_WINDOW, 0, S - SWA_KEYS))
        qpos = i * TM + np.tile(np.arange(TM), 2)
        kpos = start + np.arange(SWA_KEYS)
        out[ty] = np.where(np.abs(qpos[:, None] - kpos[None, :]) <= SWA_WINDOW, 0.0, NEG)
    return jnp.asarray(out)


def _swa(qka, proj, sink, S):
    B, T, _ = qka.shape
    nt, n_lat = T // TM, S // TM
    assert n_lat >= 3
    band = _swa_band_masks(S)
    return pl.pallas_call(
        functools.partial(_swa_kernel, n_lat, S),
        grid=(B, nt),
        in_specs=[
            pl.BlockSpec(memory_space=pltpu.SMEM),
            pl.BlockSpec((None, TM, SWA_W), lambda b, i: (b, i, A_SWQ // SWA_W)),
            pl.BlockSpec((None, T, SWA_KV_W), lambda b, i: (b, 0, A_SWK // SWA_KV_W)),
            pl.BlockSpec((None, T, SWA_KV_W), lambda b, i: (b, 0, P_SWV // SWA_KV_W)),
            pl.BlockSpec(band.shape, lambda b, i: (0, 0, 0)),
        ],
        out_specs=pl.BlockSpec((None, TM, SWA_W), lambda b, i: (b, i, 0)),
        out_shape=jax.ShapeDtypeStruct((B, T, SWA_W), bf16),
        compiler_params=_params("arbitrary", "arbitrary"),
        name="swa_attn",
    )(sink, qka, qka, proj, band)


def _post_kernel(ona_ref, omla_ref, oswa_ref, gl_ref, x_ref, mod_ref, wna_ref, wmla_ref, wswa_ref, wo_ref,
                 g2_ref, rt_ref, xo_ref, h2_ref, aff_ref, affx_ref):
    D = x_ref.shape[1]
    n_e = aff_ref.shape[1]
    n_sub = 2
    sub = x_ref.shape[0] // n_sub
    halves = [slice(r * sub, (r + 1) * sub) for r in range(n_sub)]
    gated = []
    for rows in halves:
        merged = None
        for j, (o_ref, w_ref) in enumerate(((ona_ref, wna_ref), (omla_ref, wmla_ref), (oswa_ref, wswa_ref))):
            y = _mm(o_ref[rows, :], w_ref[...])
            gy = (jnp.tanh(gl_ref[rows, j * D:(j + 1) * D].astype(f32)) + 1.0) * y
            merged = gy if merged is None else merged + gy
        gated.append(merged.astype(bf16))
    mixed = [_mm(m, wo_ref[...]) for m in gated]
    for rows, res in zip(halves, mixed):
        x = x_ref[rows, :] + mod_ref[2:3, :] * res
        xo_ref[rows, :] = x
        ms = jnp.mean(x * x, axis=-1, keepdims=True)
        h2 = x * lax.rsqrt(ms + EPS) * (g2_ref[...] * (1.0 + mod_ref[4:5, :])) + mod_ref[3:4, :]
        h_hi = h2.astype(bf16)
        h2_ref[rows, :] = h_hi
        h_lo = (h2 - h_hi.astype(f32)).astype(bf16)
        parts = _mm(h_hi, rt_ref[...]) + _mm(h_lo, rt_ref[...])
        logits = parts[:, :n_e] + parts[:, n_e:]
        e = jnp.exp(logits - jnp.max(logits, axis=1, keepdims=True))
        aff = e / jnp.sum(e, axis=1, keepdims=True)
        aff_ref[rows, :] = aff
        p0 = aff.astype(bf16)
        r1 = aff - p0.astype(f32)
        p1 = r1.astype(bf16)
        p2 = (r1 - p1.astype(f32)).astype(bf16)
        pad = jnp.zeros((sub, LANES - 3 * n_e), bf16)
        affx_ref[rows, :] = jnp.concatenate([p0, p1, p2, pad], axis=1)


def _post(ona, omla, oswa, gl, x, mod, lw, n_lat):
    B, T, D = x.shape
    nt = pl.cdiv(T, TT)
    E = lw["router"].shape[1] // 2

    def tok(w):
        return pl.BlockSpec((None, TT, w), lambda b, i: (b, i, 0))

    def const(a):
        return pl.BlockSpec(a.shape, lambda b, i: (0,) * a.ndim)

    consts = [lw["wna"], lw["wmla"], lw["wswa"], lw["wo"], lw["g2"], lw["router"]]
    return pl.pallas_call(
        _post_kernel,
        grid=(B, nt),
        in_specs=[tok(NA_W), tok(MLA_W), tok(SWA_W), tok(N_BRANCH * D), tok(D), _mod_spec(n_lat, D)]
        + [const(a) for a in consts],
        out_specs=[tok(D), tok(D), tok(E), tok(LANES)],
        out_shape=[jax.ShapeDtypeStruct((B, T, D), f32), jax.ShapeDtypeStruct((B, T, D), bf16),
                   jax.ShapeDtypeStruct((B, T, E), f32), jax.ShapeDtypeStruct((B, T, LANES), bf16)],
        compiler_params=_params("arbitrary", "arbitrary"),
        name="post_attn",
    )(ona, omla, oswa, gl, x, mod, *consts)


def _lane_cumsum(mask, tri):
    E, n = mask.shape
    carry = jnp.zeros((E, 1), f32)
    outs = []
    for k in range(n // LANES):
        w = _mm(mask[:, k * LANES:(k + 1) * LANES].astype(bf16), tri) + carry
        outs.append(w)
        carry = w[:, LANES - 1:LANES]
    return jnp.concatenate(outs, axis=1)


def _select_slots(aff, cap, base, tri):
    bits = lax.bitcast_convert_type(aff, jnp.int32)
    thr = jnp.zeros((aff.shape[0], 1), jnp.int32)
    for bit in range(30, -1, -1):
        cand = thr | (1 << bit)
        cnt = jnp.sum(jnp.where(bits >= cand, 1.0, 0.0), axis=1, keepdims=True)
        thr = jnp.where(cnt >= cap, cand, thr)
    gt = jnp.where(bits > thr, 1.0, 0.0)
    eq = jnp.where(bits == thr, 1.0, 0.0)
    need = cap - jnp.sum(gt, axis=1, keepdims=True)
    sel = jnp.maximum(gt, jnp.where(_lane_cumsum(eq, tri) <= need, eq, 0.0))
    return jnp.where(sel > 0.0, _lane_cumsum(sel, tri) + base, 0.0)


def _topk_kernel(S, cap_s, cap_l, aff_ref, tri_ref, before_ref, cp_ref, bnd_ref):
    T = aff_ref.shape[1]
    tri = tri_ref[...]
    cp_ref[:, 0:S] = _select_slots(aff_ref[:, 0:S], cap_s, 0.0, tri)
    cp_ref[:, S:T] = _select_slots(aff_ref[:, S:T], cap_l, float(cap_s), tri)
    bnd_ref[...] = _mm(jnp.where(cp_ref[...] > 0.0, 1.0, 0.0).astype(bf16), before_ref[...])


def _topk(aff, S, cap_s, cap_l):
    B, E, T = aff.shape
    tri = jnp.asarray(np.triu(np.ones((LANES, LANES), np.float32)), bf16)
    before = jnp.asarray(np.arange(T)[:, None] < MOE_CHUNK * np.arange(LANES)[None, :], bf16)
    return pl.pallas_call(
        functools.partial(_topk_kernel, S, cap_s, cap_l),
        grid=(B,),
        in_specs=[pl.BlockSpec((None, E, T), lambda b: (b, 0, 0)),
                  pl.BlockSpec((LANES, LANES), lambda b: (0, 0)),
                  pl.BlockSpec((T, LANES), lambda b: (0, 0))],
        out_specs=[pl.BlockSpec((None, E, T), lambda b: (b, 0, 0)),
                   pl.BlockSpec((None, E, LANES), lambda b: (b, 0, 0))],
        out_shape=[jax.ShapeDtypeStruct((B, E, T), f32), jax.ShapeDtypeStruct((B, E, LANES), f32)],
        compiler_params=_params("arbitrary"),
        name="expert_select",
    )(aff, tri, before)


def _window_count(lo, hi, r0):
    return jnp.where(hi > lo, lax.div(hi - r0 + (MOE_WIN - 1), MOE_WIN), 0)


def _gather_kernel(n_e, rows, bnd_ref, h2_ref, affx_ref, cp_ref, xg_ref):
    b, k = pl.program_id(0), pl.program_id(1)
    D = h2_ref.shape[1]

    @pl.when(k == 0)
    def _():
        xg_ref[...] = jnp.zeros_like(xg_ref)

    ex = lax.broadcasted_iota(jnp.int32, (n_e, 1), 0)
    starts = []
    start_col = jnp.zeros((n_e, 1), f32)
    for e in range(n_e):
        r0 = lax.div(bnd_ref[(b * n_e + e) * LANES + k], 16) * 16
        starts.append(r0)
        start_col = start_col + jnp.where(ex == e, r0.astype(f32), 0.0)
    rel = cp_ref[...] - start_col
    rel_rows = jnp.broadcast_to(rel[:, None, :], (n_e, MOE_WIN, MOE_CHUNK)).reshape(n_e * MOE_WIN, MOE_CHUNK)
    j1 = lax.broadcasted_iota(jnp.int32, (n_e, MOE_WIN, 1), 1).reshape(n_e * MOE_WIN, 1) + 1
    onehot = jnp.where(rel_rows == j1.astype(f32), 1.0, 0.0).astype(bf16)
    tok, tok_aff = h2_ref[...], affx_ref[...]
    moved = _mm(onehot, tok).astype(bf16)
    moved_aff = _mm(onehot, tok_aff).astype(bf16)
    for e in range(n_e):
        dst = pl.ds(pl.multiple_of(e * rows + starts[e], 16), MOE_WIN)
        xg_ref[dst, 0:D] += moved[e * MOE_WIN:(e + 1) * MOE_WIN]
        xg_ref[dst, D:D + LANES] += moved_aff[e * MOE_WIN:(e + 1) * MOE_WIN]

    sub1 = lax.broadcasted_iota(jnp.int32, (MOE_WIN, 1), 0) + 1
    n_wins = [_window_count(bnd_ref[(b * n_e + e) * LANES + k], bnd_ref[(b * n_e + e) * LANES + k + 1], starts[e])
              for e in range(n_e)]

    @pl.when(functools.reduce(jnp.maximum, n_wins) > 1)
    def _():
        for e in range(n_e):
            def extra(w, c, e=e):
                hit = rel[e:e + 1, :] == (sub1 + w * MOE_WIN).astype(f32)
                oh = jnp.where(hit, 1.0, 0.0).astype(bf16)
                dst = pl.ds(pl.multiple_of(e * rows + starts[e] + w * MOE_WIN, 16), MOE_WIN)
                xg_ref[dst, 0:D] += _mm(oh, tok).astype(bf16)
                xg_ref[dst, D:D + LANES] += _mm(oh, tok_aff).astype(bf16)
                return c

            lax.fori_loop(1, n_wins[e], extra, 0)


def _gather(h2, affx, cp, bnd, n_slots):
    B, T, D = h2.shape
    E = cp.shape[1]
    nc = T // MOE_CHUNK
    rows = n_slots + MOE_WIN
    assert n_slots % 16 == 0 and nc + 1 <= LANES and 3 * E <= LANES
    return pl.pallas_call(
        functools.partial(_gather_kernel, E, rows),
        grid_spec=pltpu.PrefetchScalarGridSpec(
            num_scalar_prefetch=1,
            grid=(B, nc),
            in_specs=[
                pl.BlockSpec((None, MOE_CHUNK, D), lambda b, k, s: (b, k, 0)),
                pl.BlockSpec((None, MOE_CHUNK, LANES), lambda b, k, s: (b, k, 0)),
                pl.BlockSpec((None, E, MOE_CHUNK), lambda b, k, s: (b, 0, k)),
            ],
            out_specs=pl.BlockSpec((None, E * rows, D + LANES), lambda b, k, s: (b, 0, 0)),
        ),
        out_shape=jax.ShapeDtypeStruct((B, E * rows, D + LANES), bf16),
        compiler_params=_params("arbitrary", "arbitrary"),
        name="moe_gather",
    )(bnd, h2, affx, cp)


def _ffn_kernel(n_slots, n_e, xg_ref, wg_ref, wu_ref, wd_ref, y_ref, wg_s, wu_s, wd_s):
    e, b = pl.program_id(0), pl.program_id(1)
    D = wg_ref.shape[0]

    @pl.when(b == 0)
    def _():
        wg_s[...] = wg_ref[...].astype(bf16)
        wu_s[...] = wu_ref[...].astype(bf16)
        wd_s[...] = wd_ref[...].astype(bf16)

    n_b = xg_ref.shape[0]
    xg = jnp.concatenate([xg_ref[s, 0:n_slots, 0:D] for s in range(n_b)], axis=0)
    a = _mm(xg, wg_s[...])
    u = _mm(xg, wu_s[...])
    act = (a * jax.nn.sigmoid(a) * u).astype(bf16)
    lane = lax.broadcasted_iota(jnp.int32, (1, LANES), 1)
    mine = jnp.logical_and(lax.rem(lane, n_e) == e, lane < 3 * n_e)
    parts = jnp.concatenate([xg_ref[s, 0:n_slots, D:D + LANES] for s in range(n_b)], axis=0).astype(f32)
    gate = jnp.sum(jnp.where(mine, parts, 0.0), axis=1, keepdims=True)
    y = (_mm(act, wd_s[...]) * gate).astype(bf16)
    for s in range(n_b):
        y_ref[s, 0:n_slots, :] = y[s * n_slots:(s + 1) * n_slots]
        y_ref[s, n_slots:, :] = jnp.zeros((y_ref.shape[1] - n_slots, y_ref.shape[2]), bf16)


def _ffn(xg, w_gate, w_up, w_down, layer, n_slots):
    B = xg.shape[0]
    _, E, D, F = w_gate.shape
    rows = xg.shape[1] // E
    n_b = 2 if B % 2 == 0 else 1
    return pl.pallas_call(
        functools.partial(_ffn_kernel, n_slots, E),
        grid=(E, B // n_b),
        in_specs=[
            pl.BlockSpec((n_b, None, rows, D + LANES), lambda e, b: (b, e, 0, 0)),
            pl.BlockSpec((None, None, D, F), lambda e, b: (layer, e, 0, 0)),
            pl.BlockSpec((None, None, D, F), lambda e, b: (layer, e, 0, 0)),
            pl.BlockSpec((None, None, F, D), lambda e, b: (layer, e, 0, 0)),
        ],
        out_specs=pl.BlockSpec((n_b, None, rows, D), lambda e, b: (b, e, 0, 0)),
        out_shape=jax.ShapeDtypeStruct((B, E, rows, D), bf16),
        scratch_shapes=[pltpu.VMEM((D, F), bf16), pltpu.VMEM((D, F), bf16), pltpu.VMEM((F, D), bf16)],
        compiler_params=_params("arbitrary", "arbitrary"),
        name="moe_ffn",
    )(xg.reshape(B, E, rows, D + LANES), w_gate, w_up, w_down)


def _combine_kernel(n_e, rows, residual, bnd_ref, y_ref, cpt_ref, expand_ref, *refs):
    o_ref = refs[-1]
    b, k = pl.program_id(0), pl.program_id(1)
    cpt = cpt_ref[...]
    hi_part = jnp.floor(cpt * (1.0 / 32.0))
    lo_part = cpt - 32.0 * hi_part
    expand = expand_ref[...]
    rank = 32.0 * _mm(hi_part.astype(bf16), expand) + _mm(lo_part.astype(bf16), expand)
    lane = lax.broadcasted_iota(jnp.int32, (1, n_e * MOE_WIN), 1)
    lane_e = lax.div(lane, MOE_WIN)
    starts = []
    tgt = (lane - lane_e * MOE_WIN + 1).astype(f32)
    for e in range(n_e):
        r0 = lax.div(bnd_ref[(b * n_e + e) * LANES + k], 16) * 16
        starts.append(r0)
        tgt = tgt + jnp.where(lane_e == e, r0.astype(f32), 0.0)
    onehot = jnp.where(rank == tgt, 1.0, 0.0).astype(bf16)
    ycat = jnp.concatenate(
        [y_ref[pl.ds(pl.multiple_of(e * rows + starts[e], 16), MOE_WIN), :] for e in range(n_e)], axis=0)
    o_ref[...] = _mm(onehot, ycat)

    lane_w = lax.broadcasted_iota(jnp.int32, (1, MOE_WIN), 1)
    n_wins = [_window_count(bnd_ref[(b * n_e + e) * LANES + k], bnd_ref[(b * n_e + e) * LANES + k + 1], starts[e])
              for e in range(n_e)]

    @pl.when(functools.reduce(jnp.maximum, n_wins) > 1)
    def _():
        for e in range(n_e):
            rank_e = rank[:, e * MOE_WIN:(e + 1) * MOE_WIN]

            def extra(w, c, e=e, rank_e=rank_e):
                r = pl.multiple_of(starts[e] + w * MOE_WIN, 16)
                hit = rank_e == (lane_w + (r + 1)).astype(f32)
                o_ref[...] += _mm(jnp.where(hit, 1.0, 0.0).astype(bf16),
                                  y_ref[pl.ds(pl.multiple_of(e * rows + r, 16), MOE_WIN), :])
                return c

            lax.fori_loop(1, n_wins[e], extra, 0)

    if residual:
        x_ref, mod_ref = refs[0], refs[1]
        o_ref[...] = x_ref[...] + mod_ref[5:6, :] * o_ref[...]


def _combine(y, cp, bnd, x=None, mod=None, n_tok=None):
    B, E, rows, D = y.shape
    assert rows <= 32 * 256
    residual = x is not None
    n_tok = n_tok if residual else cp.shape[2]
    expand = jnp.asarray(np.kron(np.eye(E), np.ones((1, MOE_WIN))), bf16)
    chunk = pl.BlockSpec((None, MOE_CHUNK, D), lambda b, k, s: (b, k, 0))
    in_specs = [
        pl.BlockSpec((None, E * rows, D), lambda b, k, s: (b, 0, 0)),
        pl.BlockSpec((None, MOE_CHUNK, E), lambda b, k, s: (b, k, 0)),
        pl.BlockSpec(expand.shape, lambda b, k, s: (0, 0)),
    ]
    args = [bnd, y.reshape(B, E * rows, D), cp.transpose(0, 2, 1), expand]
    if residual:
        in_specs += [chunk, pl.BlockSpec((None, None, 6, D), lambda b, k, s: (b, 0, 0, 0))]
        args += [x, mod]
    return pl.pallas_call(
        functools.partial(_combine_kernel, E, rows, residual),
        grid_spec=pltpu.PrefetchScalarGridSpec(
            num_scalar_prefetch=1,
            grid=(B, n_tok // MOE_CHUNK),
            in_specs=in_specs,
            out_specs=chunk,
        ),
        out_shape=jax.ShapeDtypeStruct((B, n_tok, D), f32),
        compiler_params=_params("arbitrary", "arbitrary"),
        name="moe_combine",
    )(*args)


def _rope_tables(S, L):
    t = jnp.arange(S)
    row, col = t // GRID_W, t % GRID_W
    f = 16
    inv = ROPE_BASE ** (-jnp.arange(f, dtype=f32) / f)
    ar = row.astype(f32)[:, None] * inv
    ac = col.astype(f32)[:, None] * inv
    cos = jnp.concatenate([jnp.cos(ar), jnp.cos(ar), jnp.cos(ac), jnp.cos(ac)], axis=1)
    sin = jnp.concatenate([-jnp.sin(ar), jnp.sin(ar), -jnp.sin(ac), jnp.sin(ac)], axis=1)
    cos = jnp.concatenate([jnp.tile(cos, (1, 2)), jnp.ones((L, LANES), f32)], axis=0)
    sin = jnp.concatenate([jnp.tile(sin, (1, 2)), jnp.zeros((L, LANES), f32)], axis=0)
    return cos, sin


def _pad_heads(w, n_heads, width, padded):
    lead = w.shape[:-1]
    w = w.reshape(lead + (n_heads, width))
    w = jnp.pad(w, [(0, 0)] * len(lead) + [(0, 0), (0, padded - width)])
    return w.reshape(lead + (n_heads * padded,))


def _layer_weights(l, D, w_in, norm2_g, na_q_g, na_k_g, mla_q_norm, mla_w_uq, mla_kv_norm, mla_w_ukv,
                   mla_q_g, mla_k_g, swa_q_g, swa_k_g, w_na_o, w_mla_o, w_swa_o, w_o, router):
    o = np.cumsum((0, NA_W, NA_W, NA_W, MLA_Q_RANK, MLA_KV_RANK, MLA_ROPE, SWA_W, SWA_KV_W, SWA_KV_W))
    o_naq, o_nak, o_nav, o_cq, o_ckv, o_kr, o_swq, o_swk, o_swv, o_gl = (int(v) for v in o)
    wi = w_in[l]
    swq = wi[:, o_swq:o_swq + SWA_W].reshape(D, SWA_HEADS, SWA_DIM)[:, SWA_HEAD_ORDER, :].reshape(D, SWA_W)
    kr = jnp.pad(wi[:, o_kr:o_kr + MLA_ROPE], ((0, 0), (0, LANES - MLA_ROPE)))
    w_all = jnp.concatenate([
        wi[:, o_naq:o_naq + 3 * NA_W], swq, wi[:, o_cq:o_cq + MLA_Q_RANK], wi[:, o_ckv:o_ckv + MLA_KV_RANK], kr,
        wi[:, o_swk:o_swk + 2 * SWA_KV_W], 0.5 * wi[:, o_gl:]], axis=1).astype(bf16)
    ukv = mla_w_ukv[l].reshape(MLA_KV_RANK, MLA_HEADS, MLA_NOPE + MLA_V)
    swa_o = w_swa_o[l].reshape(SWA_HEADS, SWA_DIM, D)[SWA_HEAD_ORDER, :, :].reshape(SWA_W, D)
    r_hi = router[l].astype(bf16)
    r_lo = (router[l] - r_hi.astype(f32)).astype(bf16)
    return dict(
        w_all=w_all,
        naq_g=jnp.tile(na_q_g[l] * (NA_DIM ** -0.5 * LOG2E), NA_HEADS)[None],
        nak_g=jnp.tile(na_k_g[l], NA_HEADS)[None],
        swq_g=jnp.tile(swa_q_g[l] * (SWA_DIM ** -0.5 * LOG2E), SWA_HEADS)[None],
        swk_g=jnp.tile(swa_k_g[l], SWA_KV_HEADS)[None],
        qn=mla_q_norm[l][None], kvn=mla_kv_norm[l][None],
        wuq=_pad_heads(mla_w_uq[l], MLA_HEADS, MLA_QK, MLA_PAD).astype(bf16),
        wuk=ukv[:, :, :MLA_NOPE].reshape(MLA_KV_RANK, MLA_HEADS * MLA_NOPE).astype(bf16),
        wuv=ukv[:, :, MLA_NOPE:].reshape(MLA_KV_RANK, MLA_W).astype(bf16),
        qg=jnp.tile(jnp.pad(mla_q_g[l] * (MLA_QK ** -0.5 * LOG2E), (0, MLA_PAD - MLA_QK)), MLA_HEADS)[None],
        kg=jnp.tile(jnp.pad(mla_k_g[l], (0, MLA_PAD - MLA_QK)), MLA_HEADS)[None],
        wna=w_na_o[l].astype(bf16), wmla=w_mla_o[l].astype(bf16), wswa=swa_o.astype(bf16),
        wo=(0.5 * w_o[l]).astype(bf16), g2=norm2_g[l][None], router=jnp.concatenate([r_hi, r_lo], axis=1),
    )


def kernel(x, c, ctx, c_ctx, norm1_g, norm2_g, w_ada, b_ada, w_in, na_q_g, na_k_g, na_rpb, mla_q_norm, mla_w_uq, mla_kv_norm, mla_w_ukv, mla_q_g, mla_k_g, swa_q_g, swa_k_g, swa_sink, w_na_o, w_mla_o, w_swa_o, w_o, router, w_gate, w_up, w_down):
    B, S, D = x.shape
    L = ctx.shape[1]
    depth = w_in.shape[0]
    E = router.shape[2]
    T = S + L
    n_lat = S // TT
    rows = S // GRID_W
    assert S % TT == 0 and L % TM == 0 and L <= TT and rows >= NA_BAND_ROWS and S >= SWA_KEYS and B < 16
    cap_s = CAPACITY_FACTOR * S // E
    cap_l = CAPACITY_FACTOR * L // E

    cpad = jnp.zeros((16, D), f32).at[:B].set(c).at[B].set(c_ctx)
    mod = _ada(cpad, w_ada, b_ada).reshape(depth, 16, 6, D)
    mod = jnp.stack([mod[:, :B], jnp.broadcast_to(mod[:, B:B + 1], (depth, B, 6, D))], axis=2)

    cos, sin = _rope_tables(S, L)
    gmat = jnp.asarray(np.kron(np.eye(NA_HEADS), np.full((NA_DIM, NA_DIM), 1.0 / NA_DIM)), bf16)

    xs = jnp.concatenate([x, ctx], axis=1)
    moe = None
    for l in range(depth):
        lw = _layer_weights(l, D, w_in, norm2_g, na_q_g, na_k_g, mla_q_norm, mla_w_uq, mla_kv_norm, mla_w_ukv,
                            mla_q_g, mla_k_g, swa_q_g, swa_k_g, w_na_o, w_mla_o, w_swa_o, w_o, router)
        modp = mod[l - 1] if l > 0 else None
        xs, proj, gl = _in_proj(xs, moe, modp, mod[l], norm1_g[l][None], lw["w_all"], n_lat)
        qka, mq, mk, mv = _prep(proj, cos, sin, gmat, lw)
        ona = _na(qka, proj, na_rpb[l], S)
        omla = _mla(mq, mk, mv, S)
        oswa = _swa(qka, proj, swa_sink[l] * LOG2E, S)
        xs, h2, aff, affx = _post(ona, omla, oswa, gl, xs, mod[l], lw, n_lat)
        cp, bnd = _topk(aff.transpose(0, 2, 1), S, cap_s, cap_l)
        bnd = bnd.astype(jnp.int32).reshape(-1)
        xg = _gather(h2, affx, cp, bnd, cap_s + cap_l)
        y = _ffn(xg, w_gate, w_up, w_down, l, cap_s + cap_l)
        if l == depth - 1:
            return _combine(y, cp, bnd, xs, mod[l], S)
        moe = _combine(y, cp, bnd)
```

```python
import functools

import numpy as np
import jax
import jax.numpy as jnp
from jax import lax
from jax.experimental import pallas as pl
from jax.experimental.pallas import tpu as pltpu

GRID_W = 64
NA_HEADS, NA_DIM, NA_WIN_ROWS, NA_WIN_COLS = 6, 64, 8, 16
MLA_HEADS, MLA_Q_RANK, MLA_KV_RANK, MLA_NOPE, MLA_ROPE, MLA_V = 4, 256, 128, 128, 64, 128
MLA_QK = MLA_NOPE + MLA_ROPE
MLA_PAD = 256
SWA_HEADS, SWA_KV_HEADS, SWA_DIM, SWA_WINDOW = 6, 2, 64, 128
CAPACITY_FACTOR = 2
N_BRANCH = 3
ROPE_BASE = 10000.0
EPS = 1e-6

NA_W = NA_HEADS * NA_DIM
SWA_W = SWA_HEADS * SWA_DIM
SWA_KV_W = SWA_KV_HEADS * SWA_DIM
MLA_W = MLA_HEADS * MLA_V

LANES = 128
PACK = 16
COND_ROWS = 16
ROPE_F = SWA_DIM // 4
TM = 256
TT = 512
MLA_TQ = 1024
MLA_TK = 512
LOG2E = 1.4426950408889634
NA_TILE_ROWS = TM // GRID_W
NA_BAND_ROWS = NA_TILE_ROWS + NA_WIN_ROWS - 1
NA_BAND = NA_BAND_ROWS * GRID_W
SWA_KEYS = TM + 2 * SWA_WINDOW
MOE_CHUNK = 256
MOE_WIN = 64
NEG = -1e30
VMEM_LIMIT = 56 * 1024 * 1024

P_NAQ, P_NAK, P_NAV, P_SWQ = 0, 384, 768, 1152
P_CQ, P_CKV, P_KR, P_SWK, P_SWV = 1536, 1792, 1920, 2048, 2176
PROJ_W = 2304
N_CHUNK = 768
A_NAQ, A_NAK, A_SWQ, A_SWK = 0, 384, 768, 1152
QKA_W = 1280
SWA_HEAD_ORDER = (0, 3, 1, 4, 2, 5)

bf16 = jnp.bfloat16
f32 = jnp.float32


def _mm(a, b):
    return jnp.dot(a, b, preferred_element_type=f32)


def _nt(a, b):
    return lax.dot_general(a, b, (((1,), (1,)), ((), ())), preferred_element_type=f32)


def _params(*sem):
    return pltpu.CompilerParams(dimension_semantics=sem, vmem_limit_bytes=VMEM_LIMIT)


def _ada_kernel(c_ref, w_ref, b_ref, o_ref):
    a = c_ref[...]
    a = (a * jax.nn.sigmoid(a)).astype(bf16)
    o_ref[...] = _mm(a, w_ref[...].astype(bf16)) + b_ref[...]


def _ada(cpad, w_ada, b_ada):
    L, D, N = w_ada.shape
    tn = 1536
    return pl.pallas_call(
        _ada_kernel,
        grid=(L, N // tn),
        in_specs=[
            pl.BlockSpec((COND_ROWS, D), lambda l, j: (0, 0)),
            pl.BlockSpec((None, D, tn), lambda l, j: (l, 0, j)),
            pl.BlockSpec((None, 1, tn), lambda l, j: (l, 0, j)),
        ],
        out_specs=pl.BlockSpec((None, COND_ROWS, tn), lambda l, j: (l, 0, j)),
        out_shape=jax.ShapeDtypeStruct((L, COND_ROWS, N), f32),
        compiler_params=_params("arbitrary", "arbitrary"),
        name="ada",
    )(cpad, w_ada, b_ada.reshape(L, 1, N))


def _in_proj_kernel(has_moe, n_gl_chunks, *refs):
    if has_moe:
        x_ref, moe_ref, modp_ref, mod_ref, g_ref, w_ref, xo_ref, proj_ref, gl_ref = refs
    else:
        x_ref, mod_ref, g_ref, w_ref, proj_ref, gl_ref = refs
    n_sub = 2
    sub = x_ref.shape[0] // n_sub
    for r in range(n_sub):
        rows = slice(r * sub, (r + 1) * sub)
        x = x_ref[rows, :]
        if has_moe:
            x = x + modp_ref[5:6, :] * moe_ref[rows, :]
            xo_ref[rows, :] = x
        ms = jnp.mean(x * x, axis=-1, keepdims=True)
        h = x * lax.rsqrt(ms + EPS) * (g_ref[...] * (1.0 + mod_ref[1:2, :])) + mod_ref[0:1, :]
        h = h.astype(bf16)
        for c in range(PROJ_W // N_CHUNK):
            sl = slice(c * N_CHUNK, (c + 1) * N_CHUNK)
            proj_ref[rows, sl] = _mm(h, w_ref[:, sl]).astype(bf16)
        for c in range(n_gl_chunks):
            sl = slice(c * N_CHUNK, (c + 1) * N_CHUNK)
            gl_ref[rows, sl] = _mm(h, w_ref[:, PROJ_W + c * N_CHUNK:PROJ_W + (c + 1) * N_CHUNK]).astype(bf16)


def _mod_spec(n_lat, D):
    return pl.BlockSpec((None, None, 6, D), lambda b, i: (b, jnp.where(i >= n_lat, 1, 0), 0, 0))


def _in_proj(x, moe, modp, mod, g1, w, n_lat):
    B, T, D = x.shape
    nt = pl.cdiv(T, TT)
    has_moe = moe is not None
    n_gl_chunks = N_BRANCH * D // N_CHUNK
    tok = pl.BlockSpec((None, TT, D), lambda b, i: (b, i, 0))
    in_specs = [tok]
    args = [x]
    if has_moe:
        in_specs += [tok, _mod_spec(n_lat, D)]
        args += [moe, modp]
    in_specs += [
        _mod_spec(n_lat, D),
        pl.BlockSpec((1, D), lambda b, i: (0, 0)),
        pl.BlockSpec(w.shape, lambda b, i: (0, 0), pipeline_mode=pl.Buffered(1)),
    ]
    args += [mod, g1, w]
    out_specs = [
        pl.BlockSpec((None, TT, PROJ_W), lambda b, i: (b, i, 0)),
        pl.BlockSpec((None, TT, N_BRANCH * D), lambda b, i: (b, i, 0)),
    ]
    out_shape = [
        jax.ShapeDtypeStruct((B, T, PROJ_W), bf16),
        jax.ShapeDtypeStruct((B, T, N_BRANCH * D), bf16),
    ]
    if has_moe:
        out_specs = [tok] + out_specs
        out_shape = [jax.ShapeDtypeStruct((B, T, D), f32)] + out_shape
    res = pl.pallas_call(
        functools.partial(_in_proj_kernel, has_moe, n_gl_chunks),
        grid=(B, nt),
        in_specs=in_specs,
        out_specs=out_specs,
        out_shape=out_shape,
        compiler_params=_params("arbitrary", "arbitrary"),
        name="in_proj",
    )(*args)
    if has_moe:
        return res
    return [x] + list(res)


def _prep_kernel(proj_ref, cos_ref, sin_ref, perm_ref, gmat_ref, naq_g, nak_g, swq_g, swk_g,
                 qn_ref, wuq_ref, qg_ref, kvn_ref, wuk_ref, wuv_ref, kg_ref,
                 qka_ref, mq_ref, mk_ref, mv_ref):
    cos = cos_ref[...]
    sin = sin_ref[...]
    perm = perm_ref[...]

    def rope(x):
        hi = x.astype(bf16)
        lo = (x - hi.astype(f32)).astype(bf16)
        partner = _mm(hi, perm) + _mm(lo, perm)
        return x * cos + partner * sin

    def headnorm(x, gain):
        w = x.shape[1]
        ms = _mm((x * x).astype(bf16), gmat_ref[:w, :w])
        return x * lax.rsqrt(ms + EPS) * gain

    def rmsnorm(x, gain):
        ms = jnp.mean(x * x, axis=-1, keepdims=True)
        return x * lax.rsqrt(ms + EPS) * gain

    naq = headnorm(proj_ref[:, P_NAQ:P_NAQ + NA_W].astype(f32), naq_g[...])
    qka_ref[:, A_NAQ:A_NAQ + NA_W] = naq.astype(bf16)
    nak = headnorm(proj_ref[:, P_NAK:P_NAK + NA_W].astype(f32), nak_g[...])
    qka_ref[:, A_NAK:A_NAK + NA_W] = nak.astype(bf16)

    swq = headnorm(proj_ref[:, P_SWQ:P_SWQ + SWA_W].astype(f32), swq_g[...])
    for p in range(SWA_W // LANES):
        sl = slice(p * LANES, (p + 1) * LANES)
        qka_ref[:, A_SWQ + p * LANES:A_SWQ + (p + 1) * LANES] = rope(swq[:, sl]).astype(bf16)
    swk = headnorm(proj_ref[:, P_SWK:P_SWK + SWA_KV_W].astype(f32), swk_g[...])
    qka_ref[:, A_SWK:A_SWK + SWA_KV_W] = rope(swk).astype(bf16)

    cq = rmsnorm(proj_ref[:, P_CQ:P_CQ + MLA_Q_RANK].astype(f32), qn_ref[...]).astype(bf16)
    q = _mm(cq, wuq_ref[...])
    ckv = rmsnorm(proj_ref[:, P_CKV:P_CKV + MLA_KV_RANK].astype(f32), kvn_ref[...]).astype(bf16)
    kn = _mm(ckv, wuk_ref[...])
    v = _mm(ckv, wuv_ref[...]).astype(bf16)
    ones_col = jnp.where(lax.broadcasted_iota(jnp.int32, (v.shape[0], LANES), 1) == 0, 1.0, 0.0).astype(bf16)
    for h in range(MLA_HEADS):
        mv_ref[:, h * MLA_PAD:h * MLA_PAD + MLA_V] = v[:, h * MLA_V:(h + 1) * MLA_V]
        mv_ref[:, h * MLA_PAD + MLA_V:(h + 1) * MLA_PAD] = ones_col
    kr = proj_ref[:, P_KR:P_KR + LANES].astype(f32)
    kr_ss = jnp.sum(kr * kr, axis=-1, keepdims=True)
    for h in range(MLA_HEADS):
        o = h * MLA_PAD
        qh = q[:, o:o + MLA_PAD]
        r = lax.rsqrt(jnp.sum(qh * qh, axis=-1, keepdims=True) * (1.0 / MLA_QK) + EPS)
        qh = qh * r * qg_ref[:, o:o + MLA_PAD]
        mq_ref[:, o:o + LANES] = qh[:, :LANES].astype(bf16)
        mq_ref[:, o + LANES:o + MLA_PAD] = rope(qh[:, LANES:]).astype(bf16)
        kh = kn[:, h * MLA_NOPE:(h + 1) * MLA_NOPE]
        r = lax.rsqrt((jnp.sum(kh * kh, axis=-1, keepdims=True) + kr_ss) * (1.0 / MLA_QK) + EPS)
        mk_ref[:, o:o + LANES] = (kh * r * kg_ref[:, o:o + LANES]).astype(bf16)
        mk_ref[:, o + LANES:o + MLA_PAD] = rope(kr * r * kg_ref[:, o + LANES:o + MLA_PAD]).astype(bf16)


def _prep(proj, cos, sin, gmat, lw):
    B, T, _ = proj.shape
    nt = pl.cdiv(T, TT)

    def const(a):
        return pl.BlockSpec(a.shape, lambda b, i: (0,) * a.ndim)

    lane = np.arange(LANES)
    perm = jnp.asarray(lane[:, None] == (lane[None, :] ^ ROPE_F), bf16)
    consts = [perm, gmat, lw["naq_g"], lw["nak_g"], lw["swq_g"], lw["swk_g"], lw["qn"], lw["wuq"], lw["qg"],
              lw["kvn"], lw["wuk"], lw["wuv"], lw["kg"]]
    widths = (QKA_W, MLA_HEADS * MLA_PAD, MLA_HEADS * MLA_PAD, MLA_HEADS * MLA_PAD)
    return pl.pallas_call(
        _prep_kernel,
        grid=(B, nt),
        in_specs=[
            pl.BlockSpec((None, TT, PROJ_W), lambda b, i: (b, i, 0)),
            pl.BlockSpec((TT, LANES), lambda b, i: (i, 0)),
            pl.BlockSpec((TT, LANES), lambda b, i: (i, 0)),
        ] + [const(a) for a in consts],
        out_specs=[pl.BlockSpec((None, TT, w), lambda b, i: (b, i, 0)) for w in widths],
        out_shape=[jax.ShapeDtypeStruct((B, T, w), bf16) for w in widths],
        compiler_params=_params("arbitrary", "arbitrary"),
        name="prep",
    )(proj, cos, sin, *consts)


def _half_masks():
    lane = lax.broadcasted_iota(jnp.int32, (1, LANES), 1)
    lo = jnp.where(lane < 64, 1.0, 0.0).astype(bf16)
    return lo, (1.0 - lo.astype(f32)).astype(bf16)


def _pairs_attend(jobs, masks):
    scored = []
    for qp, parts, _ in jobs:
        q2 = jnp.concatenate([qp * masks[0], qp * masks[1]], axis=0)
        scores = []
        for k, _, bias in parts:
            s = _nt(q2, k)
            scores.append(s if bias is None else s + bias)
        scored.append(scores)
    normed = []
    for (qp, _, sinks), scores in zip(jobs, scored):
        n_q = qp.shape[0]
        m = functools.reduce(jnp.maximum, [jnp.max(s, axis=-1, keepdims=True) for s in scores])
        if sinks is not None:
            first = lax.broadcasted_iota(jnp.int32, (2 * n_q, 1), 0) < n_q
            sink = jnp.where(first, sinks[0], sinks[1])
            m = jnp.maximum(m, sink)
        ps = [jnp.exp2(s - m) for s in scores]
        l = functools.reduce(jnp.add, [jnp.sum(p, axis=-1, keepdims=True) for p in ps])
        if sinks is not None:
            l = l + jnp.exp2(sink - m)
        normed.append((ps, l))
    lane = lax.broadcasted_iota(jnp.int32, (1, LANES), 1)
    outs = []
    for (qp, parts, _), (ps, l) in zip(jobs, normed):
        n_q = qp.shape[0]
        o = functools.reduce(jnp.add, [_mm(p.astype(bf16), v) for p, (_, v, _) in zip(ps, parts)]) / l
        outs.append(jnp.where(lane < 64, o[:n_q], o[n_q:]))
    return outs


def _pair_attend(qp, parts, masks, sinks=None):
    return _pairs_attend([(qp, parts, sinks)], masks)[0]


def _na_kernel(n_lat, S, rows, offs, q_ref, k_ref, v_ref, tall_ref, rmask_ref, o_ref):
    i = pl.program_id(1)
    masks = _half_masks()
    T = k_ref.shape[0]

    def latent(ty):
        start = GRID_W * jnp.clip(NA_TILE_ROWS * i - NA_WIN_ROWS // 2, 0, rows - NA_BAND_ROWS)
        start = pl.multiple_of(start, GRID_W)
        jobs = []
        for p in range(NA_W // LANES):
            sl = slice(p * LANES, (p + 1) * LANES)
            slabs = []
            for head in (2 * p, 2 * p + 1):
                for qr in range(NA_TILE_ROWS):
                    odd = int(offs[ty, qr]) % 2
                    lo = (int(offs[ty, qr]) - odd) * GRID_W
                    slabs.append(tall_ref[head, odd, :, lo:lo + NA_BAND] + rmask_ref[ty, qr:qr + 1, :])
            band = (k_ref[pl.ds(start, NA_BAND), sl], v_ref[pl.ds(start, NA_BAND), sl],
                    jnp.concatenate(slabs, axis=0))
            ctx = (k_ref[S:T, sl], v_ref[S:T, sl], None)
            jobs.append((q_ref[:, sl], [band, ctx], None))
        for p, o in enumerate(_pairs_attend(jobs, masks)):
            o_ref[:, p * LANES:(p + 1) * LANES] = o.astype(bf16)

    pl.when(i == 0)(functools.partial(latent, 0))
    pl.when(jnp.logical_and(i > 0, i < n_lat - 1))(functools.partial(latent, 1))
    pl.when(i == n_lat - 1)(functools.partial(latent, 2))

    @pl.when(i >= n_lat)
    def _():
        for p in range(NA_W // LANES):
            sl = slice(p * LANES, (p + 1) * LANES)
            ctx = (k_ref[S:T, sl], v_ref[S:T, sl], None)
            o_ref[:, sl] = _pair_attend(q_ref[:, sl], [ctx], masks).astype(bf16)


def _na(qka, proj, rpb, S):
    B, T, _ = qka.shape
    nt, n_lat, rows = T // TM, S // TM, S // GRID_W
    assert n_lat >= 3
    offs, front, back, row_mask = _na_geometry(rows)
    tall = _na_bias(rpb, front, back)
    row_mask = jnp.asarray(row_mask)
    return pl.pallas_call(
        functools.partial(_na_kernel, n_lat, S, rows, offs),
        grid=(B, nt),
        in_specs=[
            pl.BlockSpec((None, TM, NA_W), lambda b, i: (b, i, A_NAQ // NA_W)),
            pl.BlockSpec((None, T, NA_W), lambda b, i: (b, 0, A_NAK // NA_W)),
            pl.BlockSpec((None, T, NA_W), lambda b, i: (b, 0, P_NAV // NA_W)),
            pl.BlockSpec(tall.shape, lambda b, i: (0, 0, 0, 0)),
            pl.BlockSpec(row_mask.shape, lambda b, i: (0, 0, 0)),
        ],
        out_specs=pl.BlockSpec((None, TM, NA_W), lambda b, i: (b, i, 0)),
        out_shape=jax.ShapeDtypeStruct((B, T, NA_W), bf16),
        compiler_params=_params("arbitrary", "arbitrary"),
        name="na_attn",
    )(qka, qka, proj, tall, row_mask)


def _na_geometry(rows):
    n_tiles = rows // NA_TILE_ROWS
    band = np.arange(NA_BAND_ROWS)
    dr0 = np.zeros((3, NA_TILE_ROWS), np.int64)
    valid = np.zeros((3, NA_TILE_ROWS, NA_BAND_ROWS), bool)
    for ty, rt in enumerate((0, 1, n_tiles - 1)):
        bs = int(np.clip(NA_TILE_ROWS * rt - NA_WIN_ROWS // 2, 0, rows - NA_BAND_ROWS))
        for qr in range(NA_TILE_ROWS):
            r = NA_TILE_ROWS * rt + qr
            s_r = int(np.clip(r - NA_WIN_ROWS // 2, 0, rows - NA_WIN_ROWS))
            dr0[ty, qr] = bs - r + NA_WIN_ROWS - 1
            valid[ty, qr] = (bs + band >= s_r) & (bs + band < s_r + NA_WIN_ROWS)
    front = int(max(0, -dr0.min()))
    back = int(max(0, dr0.max() + NA_BAND_ROWS - (2 * NA_WIN_ROWS - 1)))
    row_mask = np.where(np.repeat(valid, GRID_W, axis=2), 0.0, NEG).astype(np.float32)
    return dr0 + front, front, back, row_mask


def _na_bias(rpb, front, back):
    col = np.arange(GRID_W)
    c0 = np.clip(col - NA_WIN_COLS // 2, 0, GRID_W - NA_WIN_COLS)
    col_ok = (col[None, :] >= c0[:, None]) & (col[None, :] < c0[:, None] + NA_WIN_COLS)
    dc = np.clip(col[None, :] - col[:, None] + NA_WIN_COLS - 1, 0, 2 * NA_WIN_COLS - 2)
    pick = jnp.asarray(np.arange(2 * NA_WIN_COLS - 1)[:, None, None] == dc[None], f32)
    tz = jnp.einsum("hrd,dqk->hrqk", rpb, pick, precision=lax.Precision.HIGHEST)
    tz = jnp.where(col_ok, tz * LOG2E, NEG)
    n_blocks = front + tz.shape[1] + back
    width = pl.cdiv(n_blocks * GRID_W, LANES) * LANES
    tall = tz.transpose(0, 2, 1, 3).reshape(NA_HEADS, GRID_W, -1)
    tall = jnp.pad(tall, ((0, 0), (0, 0), (front * GRID_W, width + GRID_W - (front + tz.shape[1]) * GRID_W)),
                   constant_values=NEG)
    return jnp.stack([tall[:, :, :width], tall[:, :, GRID_W:GRID_W + width]], axis=1)


def _mla_kernel(n_q, S, q_ref, k_ref, v_ref, o_ref):
    i = pl.program_id(2)
    T = k_ref.shape[0]

    def attend(q, bounds):
        m_run = acc = None
        for lo, hi in bounds:
            s = _nt(q, k_ref[lo:hi, :])
            m_new = jnp.max(s, axis=-1, keepdims=True)
            if m_run is not None:
                m_new = jnp.maximum(m_run, m_new)
            pv = _mm(jnp.exp2(s - m_new).astype(bf16), v_ref[lo:hi, :])
            acc = pv if acc is None else acc * jnp.exp2(m_run - m_new) + pv
            m_run = m_new
        return (acc[:, :MLA_V] / acc[:, MLA_V:MLA_V + 1]).astype(bf16)

    @pl.when(i < n_q)
    def _():
        chunks = [(lo, lo + MLA_TK) for lo in range(0, S, MLA_TK)] + [(S, T)]
        o_ref[...] = attend(q_ref[...], chunks)

    @pl.when(i >= n_q)
    def _():
        o_ref[0:T - S, :] = attend(q_ref[0:T - S, :], [(S, T)])


def _mla(mq, mk, mv, S):
    B, T, _ = mq.shape
    tq = min(MLA_TQ, S)
    n_q = S // tq
    assert S % tq == 0 and S % MLA_TK == 0 and T - S <= tq
    return pl.pallas_call(
        functools.partial(_mla_kernel, n_q, S),
        grid=(B, MLA_HEADS, n_q + 1),
        in_specs=[
            pl.BlockSpec((None, tq, MLA_PAD), lambda b, h, i: (b, i, h)),
            pl.BlockSpec((None, T, MLA_PAD), lambda b, h, i: (b, 0, h)),
            pl.BlockSpec((None, T, MLA_PAD), lambda b, h, i: (b, 0, h)),
        ],
        out_specs=pl.BlockSpec((None, tq, MLA_V), lambda b, h, i: (b, i, h)),
        out_shape=jax.ShapeDtypeStruct((B, T, MLA_W), bf16),
        compiler_params=_params("arbitrary", "arbitrary", "arbitrary"),
        name="mla_attn",
    )(mq, mk, mv)


def _swa_kernel(n_lat, S, sink_ref, q_ref, k_ref, v_ref, band_ref, o_ref):
    i = pl.program_id(1)
    masks = _half_masks()
    T = k_ref.shape[0]
    kc, vc = k_ref[S:T, :], v_ref[S:T, :]

    def sinks(p):
        return (sink_ref[SWA_HEAD_ORDER[2 * p]], sink_ref[SWA_HEAD_ORDER[2 * p + 1]])

    @pl.when(i < n_lat)
    def _():
        start = pl.multiple_of(jnp.clip(i * TM - SWA_WINDOW, 0, S - SWA_KEYS), SWA_WINDOW)
        band = band_ref[jnp.where(i == 0, 0, jnp.where(i == n_lat - 1, 2, 1))]
        kb, vb = k_ref[pl.ds(start, SWA_KEYS), :], v_ref[pl.ds(start, SWA_KEYS), :]
        jobs = [(q_ref[:, p * LANES:(p + 1) * LANES], [(kb, vb, band), (kc, vc, None)], sinks(p))
                for p in range(SWA_W // LANES)]
        for p, o in enumerate(_pairs_attend(jobs, masks)):
            o_ref[:, p * LANES:(p + 1) * LANES] = o.astype(bf16)

    @pl.when(i >= n_lat)
    def _():
        for p in range(SWA_W // LANES):
            sl = slice(p * LANES, (p + 1) * LANES)
            o_ref[:, sl] = _pair_attend(q_ref[:, sl], [(kc, vc, None)], masks, sinks(p)).astype(bf16)


def _swa_band_masks(S):
    n_lat = S // TM
    out = np.zeros((3, 2 * TM, SWA_KEYS), np.float32)
    for ty, i in enumerate((0, 1, n_lat - 1)):
        start = int(np.clip(i * TM - SWA_WINDOW, 0, S - SWA_KEYS))
        qpos = i * TM + np.tile(np.arange(TM), 2)
        kpos = start + np.arange(SWA_KEYS)
        out[ty] = np.where(np.abs(qpos[:, None] - kpos[None, :]) <= SWA_WINDOW, 0.0, NEG)
    return jnp.asarray(out)


def _swa(qka, proj, sink, S):
    B, T, _ = qka.shape
    nt, n_lat = T // TM, S // TM
    assert n_lat >= 3
    band = _swa_band_masks(S)
    return pl.pallas_call(
        functools.partial(_swa_kernel, n_lat, S),
        grid=(B, nt),
        in_specs=[
            pl.BlockSpec(memory_space=pltpu.SMEM),
            pl.BlockSpec((None, TM, SWA_W), lambda b, i: (b, i, A_SWQ // SWA_W)),
            pl.BlockSpec((None, T, SWA_KV_W), lambda b, i: (b, 0, A_SWK // SWA_KV_W)),
            pl.BlockSpec((None, T, SWA_KV_W), lambda b, i: (b, 0, P_SWV // SWA_KV_W)),
            pl.BlockSpec(band.shape, lambda b, i: (0, 0, 0)),
        ],
        out_specs=pl.BlockSpec((None, TM, SWA_W), lambda b, i: (b, i, 0)),
        out_shape=jax.ShapeDtypeStruct((B, T, SWA_W), bf16),
        compiler_params=_params("arbitrary", "arbitrary"),
        name="swa_attn",
    )(sink, qka, qka, proj, band)


def _post_kernel(ona_ref, omla_ref, oswa_ref, gl_ref, x_ref, mod_ref, wna_ref, wmla_ref, wswa_ref, wo_ref,
                 g2_ref, rt_ref, xo_ref, h2_ref, aff_ref, affx_ref):
    D = x_ref.shape[1]
    n_e = aff_ref.shape[1]
    n_sub = 2
    sub = x_ref.shape[0] // n_sub
    halves = [slice(r * sub, (r + 1) * sub) for r in range(n_sub)]
    gated = []
    for rows in halves:
        merged = None
        for j, (o_ref, w_ref) in enumerate(((ona_ref, wna_ref), (omla_ref, wmla_ref), (oswa_ref, wswa_ref))):
            y = _mm(o_ref[rows, :], w_ref[...])
            gy = (jnp.tanh(gl_ref[rows, j * D:(j + 1) * D].astype(f32)) + 1.0) * y
            merged = gy if merged is None else merged + gy
        gated.append(merged.astype(bf16))
    mixed = [_mm(m, wo_ref[...]) for m in gated]
    for rows, res in zip(halves, mixed):
        x = x_ref[rows, :] + mod_ref[2:3, :] * res
        xo_ref[rows, :] = x
        ms = jnp.mean(x * x, axis=-1, keepdims=True)
        h2 = x * lax.rsqrt(ms + EPS) * (g2_ref[...] * (1.0 + mod_ref[4:5, :])) + mod_ref[3:4, :]
        h_hi = h2.astype(bf16)
        h2_ref[rows, :] = h_hi
        h_lo = (h2 - h_hi.astype(f32)).astype(bf16)
        parts = _mm(h_hi, rt_ref[...]) + _mm(h_lo, rt_ref[...])
        logits = parts[:, :n_e] + parts[:, n_e:]
        e = jnp.exp(logits - jnp.max(logits, axis=1, keepdims=True))
        aff = e / jnp.sum(e, axis=1, keepdims=True)
        aff_ref[rows, :] = aff
        p0 = aff.astype(bf16)
        r1 = aff - p0.astype(f32)
        p1 = r1.astype(bf16)
        p2 = (r1 - p1.astype(f32)).astype(bf16)
        pad = jnp.zeros((sub, LANES - 3 * n_e), bf16)
        affx_ref[rows, :] = jnp.concatenate([p0, p1, p2, pad], axis=1)


def _post(ona, omla, oswa, gl, x, mod, lw, n_lat):
    B, T, D = x.shape
    nt = pl.cdiv(T, TT)
    E = lw["router"].shape[1] // 2

    def tok(w):
        return pl.BlockSpec((None, TT, w), lambda b, i: (b, i, 0))

    def const(a):
        return pl.BlockSpec(a.shape, lambda b, i: (0,) * a.ndim)

    consts = [lw["wna"], lw["wmla"], lw["wswa"], lw["wo"], lw["g2"], lw["router"]]
    return pl.pallas_call(
        _post_kernel,
        grid=(B, nt),
        in_specs=[tok(NA_W), tok(MLA_W), tok(SWA_W), tok(N_BRANCH * D), tok(D), _mod_spec(n_lat, D)]
        + [const(a) for a in consts],
        out_specs=[tok(D), tok(D), tok(E), tok(LANES)],
        out_shape=[jax.ShapeDtypeStruct((B, T, D), f32), jax.ShapeDtypeStruct((B, T, D), bf16),
                   jax.ShapeDtypeStruct((B, T, E), f32), jax.ShapeDtypeStruct((B, T, LANES), bf16)],
        compiler_params=_params("arbitrary", "arbitrary"),
        name="post_attn",
    )(ona, omla, oswa, gl, x, mod, *consts)


def _lane_cumsum(mask, tri):
    E, n = mask.shape
    carry = jnp.zeros((E, 1), f32)
    outs = []
    for k in range(n // LANES):
        w = _mm(mask[:, k * LANES:(k + 1) * LANES].astype(bf16), tri) + carry
        outs.append(w)
        carry = w[:, LANES - 1:LANES]
    return jnp.concatenate(outs, axis=1)


def _select_slots(aff, cap, base, tri):
    bits = lax.bitcast_convert_type(aff, jnp.int32)
    thr = jnp.zeros((aff.shape[0], 1), jnp.int32)
    for bit in range(30, -1, -1):
        cand = thr | (1 << bit)
        cnt = jnp.sum(jnp.where(bits >= cand, 1.0, 0.0), axis=1, keepdims=True)
        thr = jnp.where(cnt >= cap, cand, thr)
    gt = jnp.where(bits > thr, 1.0, 0.0)
    eq = jnp.where(bits == thr, 1.0, 0.0)
    need = cap - jnp.sum(gt, axis=1, keepdims=True)
    sel = jnp.maximum(gt, jnp.where(_lane_cumsum(eq, tri) <= need, eq, 0.0))
    return jnp.where(sel > 0.0, _lane_cumsum(sel, tri) + base, 0.0)


def _topk_kernel(S, cap_s, cap_l, aff_ref, tri_ref, before_ref, cp_ref, bnd_ref):
    T = aff_ref.shape[1]
    tri = tri_ref[...]
    cp_ref[:, 0:S] = _select_slots(aff_ref[:, 0:S], cap_s, 0.0, tri)
    cp_ref[:, S:T] = _select_slots(aff_ref[:, S:T], cap_l, float(cap_s), tri)
    bnd_ref[...] = _mm(jnp.where(cp_ref[...] > 0.0, 1.0, 0.0).astype(bf16), before_ref[...])


def _topk(aff, S, cap_s, cap_l):
    B, E, T = aff.shape
    tri = jnp.asarray(np.triu(np.ones((LANES, LANES), np.float32)), bf16)
    before = jnp.asarray(np.arange(T)[:, None] < MOE_CHUNK * np.arange(LANES)[None, :], bf16)
    return pl.pallas_call(
        functools.partial(_topk_kernel, S, cap_s, cap_l),
        grid=(B,),
        in_specs=[pl.BlockSpec((None, E, T), lambda b: (b, 0, 0)),
                  pl.BlockSpec((LANES, LANES), lambda b: (0, 0)),
                  pl.BlockSpec((T, LANES), lambda b: (0, 0))],
        out_specs=[pl.BlockSpec((None, E, T), lambda b: (b, 0, 0)),
                   pl.BlockSpec((None, E, LANES), lambda b: (b, 0, 0))],
        out_shape=[jax.ShapeDtypeStruct((B, E, T), f32), jax.ShapeDtypeStruct((B, E, LANES), f32)],
        compiler_params=_params("arbitrary"),
        name="expert_select",
    )(aff, tri, before)


def _window_count(lo, hi, r0):
    return jnp.where(hi > lo, lax.div(hi - r0 + (MOE_WIN - 1), MOE_WIN), 0)


def _gather_kernel(n_e, rows, bnd_ref, h2_ref, affx_ref, cp_ref, xg_ref):
    b, k = pl.program_id(0), pl.program_id(1)
    D = h2_ref.shape[1]

    @pl.when(k == 0)
    def _():
        xg_ref[...] = jnp.zeros_like(xg_ref)

    ex = lax.broadcasted_iota(jnp.int32, (n_e, 1), 0)
    starts = []
    start_col = jnp.zeros((n_e, 1), f32)
    for e in range(n_e):
        r0 = lax.div(bnd_ref[(b * n_e + e) * LANES + k], PACK) * PACK
        starts.append(r0)
        start_col = start_col + jnp.where(ex == e, r0.astype(f32), 0.0)
    rel = cp_ref[...] - start_col
    rel_rows = jnp.broadcast_to(rel[:, None, :], (n_e, MOE_WIN, MOE_CHUNK)).reshape(n_e * MOE_WIN, MOE_CHUNK)
    j1 = lax.broadcasted_iota(jnp.int32, (n_e, MOE_WIN, 1), 1).reshape(n_e * MOE_WIN, 1) + 1
    onehot = jnp.where(rel_rows == j1.astype(f32), 1.0, 0.0).astype(bf16)
    tok, tok_aff = h2_ref[...], affx_ref[...]
    moved = _mm(onehot, tok).astype(bf16)
    moved_aff = _mm(onehot, tok_aff).astype(bf16)
    for e in range(n_e):
        dst = pl.ds(pl.multiple_of(e * rows + starts[e], PACK), MOE_WIN)
        xg_ref[dst, 0:D] += moved[e * MOE_WIN:(e + 1) * MOE_WIN]
        xg_ref[dst, D:D + LANES] += moved_aff[e * MOE_WIN:(e + 1) * MOE_WIN]

    sub1 = lax.broadcasted_iota(jnp.int32, (MOE_WIN, 1), 0) + 1
    n_wins = [_window_count(bnd_ref[(b * n_e + e) * LANES + k], bnd_ref[(b * n_e + e) * LANES + k + 1], starts[e])
              for e in range(n_e)]

    @pl.when(functools.reduce(jnp.maximum, n_wins) > 1)
    def _():
        for e in range(n_e):
            def extra(w, c, e=e):
                hit = rel[e:e + 1, :] == (sub1 + w * MOE_WIN).astype(f32)
                oh = jnp.where(hit, 1.0, 0.0).astype(bf16)
                dst = pl.ds(pl.multiple_of(e * rows + starts[e] + w * MOE_WIN, PACK), MOE_WIN)
                xg_ref[dst, 0:D] += _mm(oh, tok).astype(bf16)
                xg_ref[dst, D:D + LANES] += _mm(oh, tok_aff).astype(bf16)
                return c

            lax.fori_loop(1, n_wins[e], extra, 0)


def _gather(h2, affx, cp, bnd, n_slots):
    B, T, D = h2.shape
    E = cp.shape[1]
    nc = T // MOE_CHUNK
    rows = n_slots + MOE_WIN
    assert n_slots % PACK == 0 and MOE_WIN % PACK == 0 and nc + 1 <= LANES and 3 * E <= LANES
    return pl.pallas_call(
        functools.partial(_gather_kernel, E, rows),
        grid_spec=pltpu.PrefetchScalarGridSpec(
            num_scalar_prefetch=1,
            grid=(B, nc),
            in_specs=[
                pl.BlockSpec((None, MOE_CHUNK, D), lambda b, k, s: (b, k, 0)),
                pl.BlockSpec((None, MOE_CHUNK, LANES), lambda b, k, s: (b, k, 0)),
                pl.BlockSpec((None, E, MOE_CHUNK), lambda b, k, s: (b, 0, k)),
            ],
            out_specs=pl.BlockSpec((None, E * rows, D + LANES), lambda b, k, s: (b, 0, 0)),
        ),
        out_shape=jax.ShapeDtypeStruct((B, E * rows, D + LANES), bf16),
        compiler_params=_params("arbitrary", "arbitrary"),
        name="moe_gather",
    )(bnd, h2, affx, cp)


def _ffn_kernel(n_slots, n_e, xg_ref, wg_ref, wu_ref, wd_ref, y_ref, wg_s, wu_s, wd_s):
    e, b = pl.program_id(0), pl.program_id(1)
    D = wg_ref.shape[0]

    @pl.when(b == 0)
    def _():
        wg_s[...] = wg_ref[...].astype(bf16)
        wu_s[...] = wu_ref[...].astype(bf16)
        wd_s[...] = wd_ref[...].astype(bf16)

    n_b = xg_ref.shape[0]
    xg = jnp.concatenate([xg_ref[s, 0:n_slots, 0:D] for s in range(n_b)], axis=0)
    a = _mm(xg, wg_s[...])
    u = _mm(xg, wu_s[...])
    act = (a * jax.nn.sigmoid(a) * u).astype(bf16)
    lane = lax.broadcasted_iota(jnp.int32, (1, LANES), 1)
    mine = jnp.logical_and(lax.rem(lane, n_e) == e, lane < 3 * n_e)
    parts = jnp.concatenate([xg_ref[s, 0:n_slots, D:D + LANES] for s in range(n_b)], axis=0).astype(f32)
    gate = jnp.sum(jnp.where(mine, parts, 0.0), axis=1, keepdims=True)
    y = (_mm(act, wd_s[...]) * gate).astype(bf16)
    for s in range(n_b):
        y_ref[s, 0:n_slots, :] = y[s * n_slots:(s + 1) * n_slots]
        y_ref[s, n_slots:, :] = jnp.zeros((y_ref.shape[1] - n_slots, y_ref.shape[2]), bf16)


def _ffn(xg, w_gate, w_up, w_down, layer, n_slots):
    B = xg.shape[0]
    _, E, D, F = w_gate.shape
    rows = xg.shape[1] // E
    n_b = 2 if B % 2 == 0 else 1
    return pl.pallas_call(
        functools.partial(_ffn_kernel, n_slots, E),
        grid=(E, B // n_b),
        in_specs=[
            pl.BlockSpec((n_b, None, rows, D + LANES), lambda e, b: (b, e, 0, 0)),
            pl.BlockSpec((None, None, D, F), lambda e, b: (layer, e, 0, 0)),
            pl.BlockSpec((None, None, D, F), lambda e, b: (layer, e, 0, 0)),
            pl.BlockSpec((None, None, F, D), lambda e, b: (layer, e, 0, 0)),
        ],
        out_specs=pl.BlockSpec((n_b, None, rows, D), lambda e, b: (b, e, 0, 0)),
        out_shape=jax.ShapeDtypeStruct((B, E, rows, D), bf16),
        scratch_shapes=[pltpu.VMEM((D, F), bf16), pltpu.VMEM((D, F), bf16), pltpu.VMEM((F, D), bf16)],
        compiler_params=_params("arbitrary", "arbitrary"),
        name="moe_ffn",
    )(xg.reshape(B, E, rows, D + LANES), w_gate, w_up, w_down)


def _combine_kernel(n_e, rows, residual, bnd_ref, y_ref, cpt_ref, expand_ref, *refs):
    o_ref = refs[-1]
    b, k = pl.program_id(0), pl.program_id(1)
    cpt = cpt_ref[...]
    hi_part = jnp.floor(cpt * (1.0 / 32.0))
    lo_part = cpt - 32.0 * hi_part
    expand = expand_ref[...]
    rank = 32.0 * _mm(hi_part.astype(bf16), expand) + _mm(lo_part.astype(bf16), expand)
    lane = lax.broadcasted_iota(jnp.int32, (1, n_e * MOE_WIN), 1)
    lane_e = lax.div(lane, MOE_WIN)
    starts = []
    tgt = (lane - lane_e * MOE_WIN + 1).astype(f32)
    for e in range(n_e):
        r0 = lax.div(bnd_ref[(b * n_e + e) * LANES + k], PACK) * PACK
        starts.append(r0)
        tgt = tgt + jnp.where(lane_e == e, r0.astype(f32), 0.0)
    onehot = jnp.where(rank == tgt, 1.0, 0.0).astype(bf16)
    ycat = jnp.concatenate(
        [y_ref[pl.ds(pl.multiple_of(e * rows + starts[e], PACK), MOE_WIN), :] for e in range(n_e)], axis=0)
    o_ref[...] = _mm(onehot, ycat)

    lane_w = lax.broadcasted_iota(jnp.int32, (1, MOE_WIN), 1)
    n_wins = [_window_count(bnd_ref[(b * n_e + e) * LANES + k], bnd_ref[(b * n_e + e) * LANES + k + 1], starts[e])
              for e in range(n_e)]

    @pl.when(functools.reduce(jnp.maximum, n_wins) > 1)
    def _():
        for e in range(n_e):
            rank_e = rank[:, e * MOE_WIN:(e + 1) * MOE_WIN]

            def extra(w, c, e=e, rank_e=rank_e):
                r = pl.multiple_of(starts[e] + w * MOE_WIN, PACK)
                hit = rank_e == (lane_w + (r + 1)).astype(f32)
                o_ref[...] += _mm(jnp.where(hit, 1.0, 0.0).astype(bf16),
                                  y_ref[pl.ds(pl.multiple_of(e * rows + r, PACK), MOE_WIN), :])
                return c

            lax.fori_loop(1, n_wins[e], extra, 0)

    if residual:
        x_ref, mod_ref = refs[0], refs[1]
        o_ref[...] = x_ref[...] + mod_ref[5:6, :] * o_ref[...]


def _combine(y, cp, bnd, x=None, mod=None, n_tok=None):
    B, E, rows, D = y.shape
    assert rows <= 32 * 256
    residual = x is not None
    n_tok = n_tok if residual else cp.shape[2]
    expand = jnp.asarray(np.kron(np.eye(E), np.ones((1, MOE_WIN))), bf16)
    chunk = pl.BlockSpec((None, MOE_CHUNK, D), lambda b, k, s: (b, k, 0))
    in_specs = [
        pl.BlockSpec((None, E * rows, D), lambda b, k, s: (b, 0, 0)),
        pl.BlockSpec((None, MOE_CHUNK, E), lambda b, k, s: (b, k, 0)),
        pl.BlockSpec(expand.shape, lambda b, k, s: (0, 0)),
    ]
    args = [bnd, y.reshape(B, E * rows, D), cp.transpose(0, 2, 1), expand]
    if residual:
        in_specs += [chunk, pl.BlockSpec((None, None, 6, D), lambda b, k, s: (b, 0, 0, 0))]
        args += [x, mod]
    return pl.pallas_call(
        functools.partial(_combine_kernel, E, rows, residual),
        grid_spec=pltpu.PrefetchScalarGridSpec(
            num_scalar_prefetch=1,
            grid=(B, n_tok // MOE_CHUNK),
            in_specs=in_specs,
            out_specs=chunk,
        ),
        out_shape=jax.ShapeDtypeStruct((B, n_tok, D), f32),
        compiler_params=_params("arbitrary", "arbitrary"),
        name="moe_combine",
    )(*args)


def _rope_tables(S, L):
    t = jnp.arange(S)
    row, col = t // GRID_W, t % GRID_W
    f = ROPE_F
    inv = ROPE_BASE ** (-jnp.arange(f, dtype=f32) / f)
    ar = row.astype(f32)[:, None] * inv
    ac = col.astype(f32)[:, None] * inv
    cos = jnp.concatenate([jnp.cos(ar), jnp.cos(ar), jnp.cos(ac), jnp.cos(ac)], axis=1)
    sin = jnp.concatenate([-jnp.sin(ar), jnp.sin(ar), -jnp.sin(ac), jnp.sin(ac)], axis=1)
    cos = jnp.concatenate([jnp.tile(cos, (1, 2)), jnp.ones((L, LANES), f32)], axis=0)
    sin = jnp.concatenate([jnp.tile(sin, (1, 2)), jnp.zeros((L, LANES), f32)], axis=0)
    return cos, sin


def _pad_heads(w, n_heads, width, padded):
    lead = w.shape[:-1]
    w = w.reshape(lead + (n_heads, width))
    w = jnp.pad(w, [(0, 0)] * len(lead) + [(0, 0), (0, padded - width)])
    return w.reshape(lead + (n_heads * padded,))


def _layer_weights(l, D, w_in, norm2_g, na_q_g, na_k_g, mla_q_norm, mla_w_uq, mla_kv_norm, mla_w_ukv,
                   mla_q_g, mla_k_g, swa_q_g, swa_k_g, w_na_o, w_mla_o, w_swa_o, w_o, router):
    o = np.cumsum((0, NA_W, NA_W, NA_W, MLA_Q_RANK, MLA_KV_RANK, MLA_ROPE, SWA_W, SWA_KV_W, SWA_KV_W))
    o_naq, o_nak, o_nav, o_cq, o_ckv, o_kr, o_swq, o_swk, o_swv, o_gl = (int(v) for v in o)
    wi = w_in[l]
    swq = wi[:, o_swq:o_swq + SWA_W].reshape(D, SWA_HEADS, SWA_DIM)[:, SWA_HEAD_ORDER, :].reshape(D, SWA_W)
    kr = jnp.pad(wi[:, o_kr:o_kr + MLA_ROPE], ((0, 0), (0, LANES - MLA_ROPE)))
    w_all = jnp.concatenate([
        wi[:, o_naq:o_naq + 3 * NA_W], swq, wi[:, o_cq:o_cq + MLA_Q_RANK], wi[:, o_ckv:o_ckv + MLA_KV_RANK], kr,
        wi[:, o_swk:o_swk + 2 * SWA_KV_W], 0.5 * wi[:, o_gl:]], axis=1).astype(bf16)
    ukv = mla_w_ukv[l].reshape(MLA_KV_RANK, MLA_HEADS, MLA_NOPE + MLA_V)
    swa_o = w_swa_o[l].reshape(SWA_HEADS, SWA_DIM, D)[SWA_HEAD_ORDER, :, :].reshape(SWA_W, D)
    r_hi = router[l].astype(bf16)
    r_lo = (router[l] - r_hi.astype(f32)).astype(bf16)
    return dict(
        w_all=w_all,
        naq_g=jnp.tile(na_q_g[l] * (NA_DIM ** -0.5 * LOG2E), NA_HEADS)[None],
        nak_g=jnp.tile(na_k_g[l], NA_HEADS)[None],
        swq_g=jnp.tile(swa_q_g[l] * (SWA_DIM ** -0.5 * LOG2E), SWA_HEADS)[None],
        swk_g=jnp.tile(swa_k_g[l], SWA_KV_HEADS)[None],
        qn=mla_q_norm[l][None], kvn=mla_kv_norm[l][None],
        wuq=_pad_heads(mla_w_uq[l], MLA_HEADS, MLA_QK, MLA_PAD).astype(bf16),
        wuk=ukv[:, :, :MLA_NOPE].reshape(MLA_KV_RANK, MLA_HEADS * MLA_NOPE).astype(bf16),
        wuv=ukv[:, :, MLA_NOPE:].reshape(MLA_KV_RANK, MLA_W).astype(bf16),
        qg=jnp.tile(jnp.pad(mla_q_g[l] * (MLA_QK ** -0.5 * LOG2E), (0, MLA_PAD - MLA_QK)), MLA_HEADS)[None],
        kg=jnp.tile(jnp.pad(mla_k_g[l], (0, MLA_PAD - MLA_QK)), MLA_HEADS)[None],
        wna=w_na_o[l].astype(bf16), wmla=w_mla_o[l].astype(bf16), wswa=swa_o.astype(bf16),
        wo=(0.5 * w_o[l]).astype(bf16), g2=norm2_g[l][None], router=jnp.concatenate([r_hi, r_lo], axis=1),
    )


def kernel(x, c, ctx, c_ctx, norm1_g, norm2_g, w_ada, b_ada, w_in, na_q_g, na_k_g, na_rpb, mla_q_norm, mla_w_uq, mla_kv_norm, mla_w_ukv, mla_q_g, mla_k_g, swa_q_g, swa_k_g, swa_sink, w_na_o, w_mla_o, w_swa_o, w_o, router, w_gate, w_up, w_down):
    B, S, D = x.shape
    L = ctx.shape[1]
    depth = w_in.shape[0]
    E = router.shape[2]
    T = S + L
    n_lat = S // TT
    rows = S // GRID_W
    assert S % TT == 0 and L % TM == 0 and L <= TT and rows >= NA_BAND_ROWS and S >= SWA_KEYS and B < COND_ROWS
    cap_s = CAPACITY_FACTOR * S // E
    cap_l = CAPACITY_FACTOR * L // E

    cpad = jnp.zeros((COND_ROWS, D), f32).at[:B].set(c).at[B].set(c_ctx)
    mod = _ada(cpad, w_ada, b_ada).reshape(depth, COND_ROWS, 6, D)
    mod = jnp.stack([mod[:, :B], jnp.broadcast_to(mod[:, B:B + 1], (depth, B, 6, D))], axis=2)

    cos, sin = _rope_tables(S, L)
    gmat = jnp.asarray(np.kron(np.eye(NA_HEADS), np.full((NA_DIM, NA_DIM), 1.0 / NA_DIM)), bf16)

    xs = jnp.concatenate([x, ctx], axis=1)
    moe = None
    for l in range(depth):
        lw = _layer_weights(l, D, w_in, norm2_g, na_q_g, na_k_g, mla_q_norm, mla_w_uq, mla_kv_norm, mla_w_ukv,
                            mla_q_g, mla_k_g, swa_q_g, swa_k_g, w_na_o, w_mla_o, w_swa_o, w_o, router)
        modp = mod[l - 1] if l > 0 else None
        xs, proj, gl = _in_proj(xs, moe, modp, mod[l], norm1_g[l][None], lw["w_all"], n_lat)
        qka, mq, mk, mv = _prep(proj, cos, sin, gmat, lw)
        ona = _na(qka, proj, na_rpb[l], S)
        omla = _mla(mq, mk, mv, S)
        oswa = _swa(qka, proj, swa_sink[l] * LOG2E, S)
        xs, h2, aff, affx = _post(ona, omla, oswa, gl, xs, mod[l], lw, n_lat)
        cp, bnd = _topk(aff.transpose(0, 2, 1), S, cap_s, cap_l)
        bnd = bnd.astype(jnp.int32).reshape(-1)
        xg = _gather(h2, affx, cp, bnd, cap_s + cap_l)
        y = _ffn(xg, w_gate, w_up, w_down, l, cap_s + cap_l)
        if l == depth - 1:
            return _combine(y, cp, bnd, xs, mod[l], S)
        moe = _combine(y, cp, bnd)
```

```python
import functools

import numpy as np
import jax
import jax.numpy as jnp
from jax import lax
from jax.experimental import pallas as pl
from jax.experimental.pallas import tpu as pltpu

GRID_W = 64
NA_HEADS, NA_DIM, NA_WIN_ROWS, NA_WIN_COLS = 6, 64, 8, 16
MLA_HEADS, MLA_Q_RANK, MLA_KV_RANK, MLA_NOPE, MLA_ROPE, MLA_V = 4, 256, 128, 128, 64, 128
MLA_QK = MLA_NOPE + MLA_ROPE
MLA_PAD = 256
SWA_HEADS, SWA_KV_HEADS, SWA_DIM, SWA_WINDOW = 6, 2, 64, 128
CAPACITY_FACTOR = 2
N_BRANCH = 3
ROPE_BASE = 10000.0
EPS = 1e-6

NA_W = NA_HEADS * NA_DIM
SWA_W = SWA_HEADS * SWA_DIM
SWA_KV_W = SWA_KV_HEADS * SWA_DIM
MLA_W = MLA_HEADS * MLA_V

LANES = 128
PACK = 16
COND_ROWS = 16
ROPE_F = SWA_DIM // 4
TM = 256
TT = 512
MLA_TQ = 1024
MLA_TK = 256
LOG2E = 1.4426950408889634
NA_TILE_ROWS = TM // GRID_W
NA_BAND_ROWS = NA_TILE_ROWS + NA_WIN_ROWS - 1
NA_BAND = NA_BAND_ROWS * GRID_W
SWA_KEYS = TM + 2 * SWA_WINDOW
MOE_CHUNK = 256
MOE_WIN = 64
NEG = -1e30
VMEM_LIMIT = 56 * 1024 * 1024

P_NAQ, P_NAK, P_NAV, P_SWQ = 0, 384, 768, 1152
P_CQ, P_CKV, P_KR, P_SWK, P_SWV = 1536, 1792, 1920, 2048, 2176
PROJ_W = 2304
N_CHUNK = 768
A_NAQ, A_NAK, A_SWQ, A_SWK = 0, 384, 768, 1152
QKA_W = 1280
SWA_HEAD_ORDER = (0, 3, 1, 4, 2, 5)

bf16 = jnp.bfloat16
f32 = jnp.float32


def _mm(a, b):
    return jnp.dot(a, b, preferred_element_type=f32)


def _nt(a, b):
    return lax.dot_general(a, b, (((1,), (1,)), ((), ())), preferred_element_type=f32)


def _params(*sem):
    return pltpu.CompilerParams(dimension_semantics=sem, vmem_limit_bytes=VMEM_LIMIT)


def _ada_kernel(c_ref, w_ref, b_ref, o_ref):
    a = c_ref[...]
    a = (a * jax.nn.sigmoid(a)).astype(bf16)
    o_ref[...] = _mm(a, w_ref[...].astype(bf16)) + b_ref[...]


def _ada(cpad, w_ada, b_ada):
    L, D, N = w_ada.shape
    tn = 1536
    return pl.pallas_call(
        _ada_kernel,
        grid=(L, N // tn),
        in_specs=[
            pl.BlockSpec((COND_ROWS, D), lambda l, j: (0, 0)),
            pl.BlockSpec((None, D, tn), lambda l, j: (l, 0, j)),
            pl.BlockSpec((None, 1, tn), lambda l, j: (l, 0, j)),
        ],
        out_specs=pl.BlockSpec((None, COND_ROWS, tn), lambda l, j: (l, 0, j)),
        out_shape=jax.ShapeDtypeStruct((L, COND_ROWS, N), f32),
        compiler_params=_params("arbitrary", "arbitrary"),
        name="ada",
    )(cpad, w_ada, b_ada.reshape(L, 1, N))


def _in_proj_kernel(has_moe, n_gl_chunks, *refs):
    if has_moe:
        x_ref, moe_ref, modp_ref, mod_ref, g_ref, w_ref, xo_ref, proj_ref, gl_ref = refs
    else:
        x_ref, mod_ref, g_ref, w_ref, proj_ref, gl_ref = refs
    n_sub = 2
    sub = x_ref.shape[0] // n_sub
    for r in range(n_sub):
        rows = slice(r * sub, (r + 1) * sub)
        x = x_ref[rows, :]
        if has_moe:
            x = x + modp_ref[5:6, :] * moe_ref[rows, :]
            xo_ref[rows, :] = x
        ms = jnp.mean(x * x, axis=-1, keepdims=True)
        h = x * lax.rsqrt(ms + EPS) * (g_ref[...] * (1.0 + mod_ref[1:2, :])) + mod_ref[0:1, :]
        h = h.astype(bf16)
        for c in range(PROJ_W // N_CHUNK):
            sl = slice(c * N_CHUNK, (c + 1) * N_CHUNK)
            proj_ref[rows, sl] = _mm(h, w_ref[:, sl]).astype(bf16)
        for c in range(n_gl_chunks):
            sl = slice(c * N_CHUNK, (c + 1) * N_CHUNK)
            gl_ref[rows, sl] = _mm(h, w_ref[:, PROJ_W + c * N_CHUNK:PROJ_W + (c + 1) * N_CHUNK]).astype(bf16)


def _mod_spec(n_lat, D):
    return pl.BlockSpec((None, None, 6, D), lambda b, i: (b, jnp.where(i >= n_lat, 1, 0), 0, 0))


def _in_proj(x, moe, modp, mod, g1, w, n_lat):
    B, T, D = x.shape
    nt = pl.cdiv(T, TT)
    has_moe = moe is not None
    n_gl_chunks = N_BRANCH * D // N_CHUNK
    tok = pl.BlockSpec((None, TT, D), lambda b, i: (b, i, 0))
    in_specs = [tok]
    args = [x]
    if has_moe:
        in_specs += [tok, _mod_spec(n_lat, D)]
        args += [moe, modp]
    in_specs += [
        _mod_spec(n_lat, D),
        pl.BlockSpec((1, D), lambda b, i: (0, 0)),
        pl.BlockSpec(w.shape, lambda b, i: (0, 0), pipeline_mode=pl.Buffered(1)),
    ]
    args += [mod, g1, w]
    out_specs = [
        pl.BlockSpec((None, TT, PROJ_W), lambda b, i: (b, i, 0)),
        pl.BlockSpec((None, TT, N_BRANCH * D), lambda b, i: (b, i, 0)),
    ]
    out_shape = [
        jax.ShapeDtypeStruct((B, T, PROJ_W), bf16),
        jax.ShapeDtypeStruct((B, T, N_BRANCH * D), bf16),
    ]
    if has_moe:
        out_specs = [tok] + out_specs
        out_shape = [jax.ShapeDtypeStruct((B, T, D), f32)] + out_shape
    res = pl.pallas_call(
        functools.partial(_in_proj_kernel, has_moe, n_gl_chunks),
        grid=(B, nt),
        in_specs=in_specs,
        out_specs=out_specs,
        out_shape=out_shape,
        compiler_params=_params("arbitrary", "arbitrary"),
        name="in_proj",
    )(*args)
    if has_moe:
        return res
    return [x] + list(res)


def _prep_kernel(proj_ref, cos_ref, sin_ref, perm_ref, gmat_ref, naq_g, nak_g, swq_g, swk_g,
                 qn_ref, wuq_ref, qg_ref, kvn_ref, wuk_ref, wuv_ref, kg_ref,
                 qka_ref, mq_ref, mk_ref, mv_ref):
    cos = cos_ref[...]
    sin = sin_ref[...]
    perm = perm_ref[...]

    def rope(x):
        hi = x.astype(bf16)
        lo = (x - hi.astype(f32)).astype(bf16)
        partner = _mm(hi, perm) + _mm(lo, perm)
        return x * cos + partner * sin

    def headnorm(x, gain):
        w = x.shape[1]
        ms = _mm((x * x).astype(bf16), gmat_ref[:w, :w])
        return x * lax.rsqrt(ms + EPS) * gain

    def rmsnorm(x, gain):
        ms = jnp.mean(x * x, axis=-1, keepdims=True)
        return x * lax.rsqrt(ms + EPS) * gain

    naq = headnorm(proj_ref[:, P_NAQ:P_NAQ + NA_W].astype(f32), naq_g[...])
    qka_ref[:, A_NAQ:A_NAQ + NA_W] = naq.astype(bf16)
    nak = headnorm(proj_ref[:, P_NAK:P_NAK + NA_W].astype(f32), nak_g[...])
    qka_ref[:, A_NAK:A_NAK + NA_W] = nak.astype(bf16)

    swq = headnorm(proj_ref[:, P_SWQ:P_SWQ + SWA_W].astype(f32), swq_g[...])
    for p in range(SWA_W // LANES):
        sl = slice(p * LANES, (p + 1) * LANES)
        qka_ref[:, A_SWQ + p * LANES:A_SWQ + (p + 1) * LANES] = rope(swq[:, sl]).astype(bf16)
    swk = headnorm(proj_ref[:, P_SWK:P_SWK + SWA_KV_W].astype(f32), swk_g[...])
    qka_ref[:, A_SWK:A_SWK + SWA_KV_W] = rope(swk).astype(bf16)

    cq = rmsnorm(proj_ref[:, P_CQ:P_CQ + MLA_Q_RANK].astype(f32), qn_ref[...]).astype(bf16)
    q = _mm(cq, wuq_ref[...])
    ckv = rmsnorm(proj_ref[:, P_CKV:P_CKV + MLA_KV_RANK].astype(f32), kvn_ref[...]).astype(bf16)
    kn = _mm(ckv, wuk_ref[...])
    v = _mm(ckv, wuv_ref[...]).astype(bf16)
    ones_col = jnp.where(lax.broadcasted_iota(jnp.int32, (v.shape[0], LANES), 1) == 0, 1.0, 0.0).astype(bf16)
    for h in range(MLA_HEADS):
        mv_ref[:, h * MLA_PAD:h * MLA_PAD + MLA_V] = v[:, h * MLA_V:(h + 1) * MLA_V]
        mv_ref[:, h * MLA_PAD + MLA_V:(h + 1) * MLA_PAD] = ones_col
    kr = proj_ref[:, P_KR:P_KR + LANES].astype(f32)
    kr_ss = jnp.sum(kr * kr, axis=-1, keepdims=True)
    for h in range(MLA_HEADS):
        o = h * MLA_PAD
        qh = q[:, o:o + MLA_PAD]
        r = lax.rsqrt(jnp.sum(qh * qh, axis=-1, keepdims=True) * (1.0 / MLA_QK) + EPS)
        qh = qh * r * qg_ref[:, o:o + MLA_PAD]
        mq_ref[:, o:o + LANES] = qh[:, :LANES].astype(bf16)
        mq_ref[:, o + LANES:o + MLA_PAD] = rope(qh[:, LANES:]).astype(bf16)
        kh = kn[:, h * MLA_NOPE:(h + 1) * MLA_NOPE]
        r = lax.rsqrt((jnp.sum(kh * kh, axis=-1, keepdims=True) + kr_ss) * (1.0 / MLA_QK) + EPS)
        mk_ref[:, o:o + LANES] = (kh * r * kg_ref[:, o:o + LANES]).astype(bf16)
        mk_ref[:, o + LANES:o + MLA_PAD] = rope(kr * r * kg_ref[:, o + LANES:o + MLA_PAD]).astype(bf16)


def _prep(proj, cos, sin, gmat, lw):
    B, T, _ = proj.shape
    nt = pl.cdiv(T, TT)

    def const(a):
        return pl.BlockSpec(a.shape, lambda b, i: (0,) * a.ndim)

    lane = np.arange(LANES)
    perm = jnp.asarray(lane[:, None] == (lane[None, :] ^ ROPE_F), bf16)
    consts = [perm, gmat, lw["naq_g"], lw["nak_g"], lw["swq_g"], lw["swk_g"], lw["qn"], lw["wuq"], lw["qg"],
              lw["kvn"], lw["wuk"], lw["wuv"], lw["kg"]]
    widths = (QKA_W, MLA_HEADS * MLA_PAD, MLA_HEADS * MLA_PAD, MLA_HEADS * MLA_PAD)
    return pl.pallas_call(
        _prep_kernel,
        grid=(B, nt),
        in_specs=[
            pl.BlockSpec((None, TT, PROJ_W), lambda b, i: (b, i, 0)),
            pl.BlockSpec((TT, LANES), lambda b, i: (i, 0)),
            pl.BlockSpec((TT, LANES), lambda b, i: (i, 0)),
        ] + [const(a) for a in consts],
        out_specs=[pl.BlockSpec((None, TT, w), lambda b, i: (b, i, 0)) for w in widths],
        out_shape=[jax.ShapeDtypeStruct((B, T, w), bf16) for w in widths],
        compiler_params=_params("arbitrary", "arbitrary"),
        name="prep",
    )(proj, cos, sin, *consts)


def _half_masks():
    lane = lax.broadcasted_iota(jnp.int32, (1, LANES), 1)
    lo = jnp.where(lane < 64, 1.0, 0.0).astype(bf16)
    return lo, (1.0 - lo.astype(f32)).astype(bf16)


def _pairs_attend(jobs, masks):
    scored = []
    for qp, parts, _ in jobs:
        q2 = jnp.concatenate([qp * masks[0], qp * masks[1]], axis=0)
        scores = []
        for k, _, bias in parts:
            s = _nt(q2, k)
            scores.append(s if bias is None else s + bias)
        scored.append(scores)
    normed = []
    for (qp, _, sinks), scores in zip(jobs, scored):
        n_q = qp.shape[0]
        m = functools.reduce(jnp.maximum, [jnp.max(s, axis=-1, keepdims=True) for s in scores])
        if sinks is not None:
            first = lax.broadcasted_iota(jnp.int32, (2 * n_q, 1), 0) < n_q
            sink = jnp.where(first, sinks[0], sinks[1])
            m = jnp.maximum(m, sink)
        ps = [jnp.exp2(s - m) for s in scores]
        l = functools.reduce(jnp.add, [jnp.sum(p, axis=-1, keepdims=True) for p in ps])
        if sinks is not None:
            l = l + jnp.exp2(sink - m)
        normed.append((ps, l))
    lane = lax.broadcasted_iota(jnp.int32, (1, LANES), 1)
    outs = []
    for (qp, parts, _), (ps, l) in zip(jobs, normed):
        n_q = qp.shape[0]
        o = functools.reduce(jnp.add, [_mm(p.astype(bf16), v) for p, (_, v, _) in zip(ps, parts)]) / l
        outs.append(jnp.where(lane < 64, o[:n_q], o[n_q:]))
    return outs


def _pair_attend(qp, parts, masks, sinks=None):
    return _pairs_attend([(qp, parts, sinks)], masks)[0]


def _na_kernel(n_lat, S, rows, offs, q_ref, k_ref, v_ref, tall_ref, rmask_ref, o_ref):
    i = pl.program_id(1)
    masks = _half_masks()
    T = k_ref.shape[0]

    def latent(ty):
        start = GRID_W * jnp.clip(NA_TILE_ROWS * i - NA_WIN_ROWS // 2, 0, rows - NA_BAND_ROWS)
        start = pl.multiple_of(start, GRID_W)
        jobs = []
        for p in range(NA_W // LANES):
            sl = slice(p * LANES, (p + 1) * LANES)
            slabs = []
            for head in (2 * p, 2 * p + 1):
                for qr in range(NA_TILE_ROWS):
                    odd = int(offs[ty, qr]) % 2
                    lo = (int(offs[ty, qr]) - odd) * GRID_W
                    slabs.append(tall_ref[head, odd, :, lo:lo + NA_BAND] + rmask_ref[ty, qr:qr + 1, :])
            band = (k_ref[pl.ds(start, NA_BAND), sl], v_ref[pl.ds(start, NA_BAND), sl],
                    jnp.concatenate(slabs, axis=0))
            ctx = (k_ref[S:T, sl], v_ref[S:T, sl], None)
            jobs.append((q_ref[:, sl], [band, ctx], None))
        for p, o in enumerate(_pairs_attend(jobs, masks)):
            o_ref[:, p * LANES:(p + 1) * LANES] = o.astype(bf16)

    pl.when(i == 0)(functools.partial(latent, 0))
    pl.when(jnp.logical_and(i > 0, i < n_lat - 1))(functools.partial(latent, 1))
    pl.when(i == n_lat - 1)(functools.partial(latent, 2))

    @pl.when(i >= n_lat)
    def _():
        for p in range(NA_W // LANES):
            sl = slice(p * LANES, (p + 1) * LANES)
            ctx = (k_ref[S:T, sl], v_ref[S:T, sl], None)
            o_ref[:, sl] = _pair_attend(q_ref[:, sl], [ctx], masks).astype(bf16)


def _na(qka, proj, rpb, S):
    B, T, _ = qka.shape
    nt, n_lat, rows = T // TM, S // TM, S // GRID_W
    assert n_lat >= 3
    offs, front, back, row_mask = _na_geometry(rows)
    tall = _na_bias(rpb, front, back)
    row_mask = jnp.asarray(row_mask)
    return pl.pallas_call(
        functools.partial(_na_kernel, n_lat, S, rows, offs),
        grid=(B, nt),
        in_specs=[
            pl.BlockSpec((None, TM, NA_W), lambda b, i: (b, i, A_NAQ // NA_W)),
            pl.BlockSpec((None, T, NA_W), lambda b, i: (b, 0, A_NAK // NA_W)),
            pl.BlockSpec((None, T, NA_W), lambda b, i: (b, 0, P_NAV // NA_W)),
            pl.BlockSpec(tall.shape, lambda b, i: (0, 0, 0, 0)),
            pl.BlockSpec(row_mask.shape, lambda b, i: (0, 0, 0)),
        ],
        out_specs=pl.BlockSpec((None, TM, NA_W), lambda b, i: (b, i, 0)),
        out_shape=jax.ShapeDtypeStruct((B, T, NA_W), bf16),
        compiler_params=_params("arbitrary", "arbitrary"),
        name="na_attn",
    )(qka, qka, proj, tall, row_mask)


def _na_geometry(rows):
    n_tiles = rows // NA_TILE_ROWS
    band = np.arange(NA_BAND_ROWS)
    dr0 = np.zeros((3, NA_TILE_ROWS), np.int64)
    valid = np.zeros((3, NA_TILE_ROWS, NA_BAND_ROWS), bool)
    for ty, rt in enumerate((0, 1, n_tiles - 1)):
        bs = int(np.clip(NA_TILE_ROWS * rt - NA_WIN_ROWS // 2, 0, rows - NA_BAND_ROWS))
        for qr in range(NA_TILE_ROWS):
            r = NA_TILE_ROWS * rt + qr
            s_r = int(np.clip(r - NA_WIN_ROWS // 2, 0, rows - NA_WIN_ROWS))
            dr0[ty, qr] = bs - r + NA_WIN_ROWS - 1
            valid[ty, qr] = (bs + band >= s_r) & (bs + band < s_r + NA_WIN_ROWS)
    front = int(max(0, -dr0.min()))
    back = int(max(0, dr0.max() + NA_BAND_ROWS - (2 * NA_WIN_ROWS - 1)))
    row_mask = np.where(np.repeat(valid, GRID_W, axis=2), 0.0, NEG).astype(np.float32)
    return dr0 + front, front, back, row_mask


def _na_bias(rpb, front, back):
    col = np.arange(GRID_W)
    c0 = np.clip(col - NA_WIN_COLS // 2, 0, GRID_W - NA_WIN_COLS)
    col_ok = (col[None, :] >= c0[:, None]) & (col[None, :] < c0[:, None] + NA_WIN_COLS)
    dc = np.clip(col[None, :] - col[:, None] + NA_WIN_COLS - 1, 0, 2 * NA_WIN_COLS - 2)
    pick = jnp.asarray(np.arange(2 * NA_WIN_COLS - 1)[:, None, None] == dc[None], f32)
    tz = jnp.einsum("hrd,dqk->hrqk", rpb, pick, precision=lax.Precision.HIGHEST)
    tz = jnp.where(col_ok, tz * LOG2E, NEG)
    n_blocks = front + tz.shape[1] + back
    width = pl.cdiv(n_blocks * GRID_W, LANES) * LANES
    tall = tz.transpose(0, 2, 1, 3).reshape(NA_HEADS, GRID_W, -1)
    tall = jnp.pad(tall, ((0, 0), (0, 0), (front * GRID_W, width + GRID_W - (front + tz.shape[1]) * GRID_W)),
                   constant_values=NEG)
    return jnp.stack([tall[:, :, :width], tall[:, :, GRID_W:GRID_W + width]], axis=1)


def _mla_kernel(n_q, S, q_ref, k_ref, v_ref, o_ref):
    i = pl.program_id(2)
    T = k_ref.shape[0]

    def attend(q, bounds):
        m_run = acc = None
        for lo, hi in bounds:
            s = _nt(q, k_ref[lo:hi, :])
            m_new = jnp.max(s, axis=-1, keepdims=True)
            if m_run is not None:
                m_new = jnp.maximum(m_run, m_new)
            pv = _mm(jnp.exp2(s - m_new).astype(bf16), v_ref[lo:hi, :])
            acc = pv if acc is None else acc * jnp.exp2(m_run - m_new) + pv
            m_run = m_new
        return (acc[:, :MLA_V] / acc[:, MLA_V:MLA_V + 1]).astype(bf16)

    @pl.when(i < n_q)
    def _():
        chunks = [(lo, lo + MLA_TK) for lo in range(0, S, MLA_TK)] + [(S, T)]
        o_ref[...] = attend(q_ref[...], chunks)

    @pl.when(i >= n_q)
    def _():
        o_ref[0:T - S, :] = attend(q_ref[0:T - S, :], [(S, T)])


def _mla(mq, mk, mv, S):
    B, T, _ = mq.shape
    tq = min(MLA_TQ, S)
    n_q = S // tq
    assert S % tq == 0 and S % MLA_TK == 0 and T - S <= tq
    return pl.pallas_call(
        functools.partial(_mla_kernel, n_q, S),
        grid=(B, MLA_HEADS, n_q + 1),
        in_specs=[
            pl.BlockSpec((None, tq, MLA_PAD), lambda b, h, i: (b, i, h)),
            pl.BlockSpec((None, T, MLA_PAD), lambda b, h, i: (b, 0, h)),
            pl.BlockSpec((None, T, MLA_PAD), lambda b, h, i: (b, 0, h)),
        ],
        out_specs=pl.BlockSpec((None, tq, MLA_V), lambda b, h, i: (b, i, h)),
        out_shape=jax.ShapeDtypeStruct((B, T, MLA_W), bf16),
        compiler_params=_params("arbitrary", "arbitrary", "arbitrary"),
        name="mla_attn",
    )(mq, mk, mv)


def _swa_kernel(n_lat, S, sink_ref, q_ref, k_ref, v_ref, band_ref, o_ref):
    i = pl.program_id(1)
    masks = _half_masks()
    T = k_ref.shape[0]
    kc, vc = k_ref[S:T, :], v_ref[S:T, :]

    def sinks(p):
        return (sink_ref[SWA_HEAD_ORDER[2 * p]], sink_ref[SWA_HEAD_ORDER[2 * p + 1]])

    @pl.when(i < n_lat)
    def _():
        start = pl.multiple_of(jnp.clip(i * TM - SWA_WINDOW, 0, S - SWA_KEYS), SWA_WINDOW)
        band = band_ref[jnp.where(i == 0, 0, jnp.where(i == n_lat - 1, 2, 1))]
        kb, vb = k_ref[pl.ds(start, SWA_KEYS), :], v_ref[pl.ds(start, SWA_KEYS), :]
        jobs = [(q_ref[:, p * LANES:(p + 1) * LANES], [(kb, vb, band), (kc, vc, None)], sinks(p))
                for p in range(SWA_W // LANES)]
        for p, o in enumerate(_pairs_attend(jobs, masks)):
            o_ref[:, p * LANES:(p + 1) * LANES] = o.astype(bf16)

    @pl.when(i >= n_lat)
    def _():
        for p in range(SWA_W // LANES):
            sl = slice(p * LANES, (p + 1) * LANES)
            o_ref[:, sl] = _pair_attend(q_ref[:, sl], [(kc, vc, None)], masks, sinks(p)).astype(bf16)


def _swa_band_masks(S):
    n_lat = S // TM
    out = np.zeros((3, 2 * TM, SWA_KEYS), np.float32)
    for ty, i in enumerate((0, 1, n_lat - 1)):
        start = int(np.clip(i * TM - SWA_WINDOW, 0, S - SWA_KEYS))
        qpos = i * TM + np.tile(np.arange(TM), 2)
        kpos = start + np.arange(SWA_KEYS)
        out[ty] = np.where(np.abs(qpos[:, None] - kpos[None, :]) <= SWA_WINDOW, 0.0, NEG)
    return jnp.asarray(out)


def _swa(qka, proj, sink, S):
    B, T, _ = qka.shape
    nt, n_lat = T // TM, S // TM
    assert n_lat >= 3
    band = _swa_band_masks(S)
    return pl.pallas_call(
        functools.partial(_swa_kernel, n_lat, S),
        grid=(B, nt),
        in_specs=[
            pl.BlockSpec(memory_space=pltpu.SMEM),
            pl.BlockSpec((None, TM, SWA_W), lambda b, i: (b, i, A_SWQ // SWA_W)),
            pl.BlockSpec((None, T, SWA_KV_W), lambda b, i: (b, 0, A_SWK // SWA_KV_W)),
            pl.BlockSpec((None, T, SWA_KV_W), lambda b, i: (b, 0, P_SWV // SWA_KV_W)),
            pl.BlockSpec(band.shape, lambda b, i: (0, 0, 0)),
        ],
        out_specs=pl.BlockSpec((None, TM, SWA_W), lambda b, i: (b, i, 0)),
        out_shape=jax.ShapeDtypeStruct((B, T, SWA_W), bf16),
        compiler_params=_params("arbitrary", "arbitrary"),
        name="swa_attn",
    )(sink, qka, qka, proj, band)


def _post_kernel(ona_ref, omla_ref, oswa_ref, gl_ref, x_ref, mod_ref, wna_ref, wmla_ref, wswa_ref, wo_ref,
                 g2_ref, rt_ref, xo_ref, h2_ref, aff_ref, affx_ref):
    D = x_ref.shape[1]
    n_e = aff_ref.shape[1]
    n_sub = 2
    sub = x_ref.shape[0] // n_sub
    halves = [slice(r * sub, (r + 1) * sub) for r in range(n_sub)]
    gated = []
    for rows in halves:
        merged = None
        for j, (o_ref, w_ref) in enumerate(((ona_ref, wna_ref), (omla_ref, wmla_ref), (oswa_ref, wswa_ref))):
            y = _mm(o_ref[rows, :], w_ref[...])
            gy = (jnp.tanh(gl_ref[rows, j * D:(j + 1) * D].astype(f32)) + 1.0) * y
            merged = gy if merged is None else merged + gy
        gated.append(merged.astype(bf16))
    mixed = [_mm(m, wo_ref[...]) for m in gated]
    for rows, res in zip(halves, mixed):
        x = x_ref[rows, :] + mod_ref[2:3, :] * res
        xo_ref[rows, :] = x
        ms = jnp.mean(x * x, axis=-1, keepdims=True)
        h2 = x * lax.rsqrt(ms + EPS) * (g2_ref[...] * (1.0 + mod_ref[4:5, :])) + mod_ref[3:4, :]
        h_hi = h2.astype(bf16)
        h2_ref[rows, :] = h_hi
        h_lo = (h2 - h_hi.astype(f32)).astype(bf16)
        parts = _mm(h_hi, rt_ref[...]) + _mm(h_lo, rt_ref[...])
        logits = parts[:, :n_e] + parts[:, n_e:]
        e = jnp.exp(logits - jnp.max(logits, axis=1, keepdims=True))
        aff = e / jnp.sum(e, axis=1, keepdims=True)
        aff_ref[rows, :] = aff
        p0 = aff.astype(bf16)
        r1 = aff - p0.astype(f32)
        p1 = r1.astype(bf16)
        p2 = (r1 - p1.astype(f32)).astype(bf16)
        pad = jnp.zeros((sub, LANES - 3 * n_e), bf16)
        affx_ref[rows, :] = jnp.concatenate([p0, p1, p2, pad], axis=1)


def _post(ona, omla, oswa, gl, x, mod, lw, n_lat):
    B, T, D = x.shape
    nt = pl.cdiv(T, TT)
    E = lw["router"].shape[1] // 2

    def tok(w):
        return pl.BlockSpec((None, TT, w), lambda b, i: (b, i, 0))

    def const(a):
        return pl.BlockSpec(a.shape, lambda b, i: (0,) * a.ndim)

    consts = [lw["wna"], lw["wmla"], lw["wswa"], lw["wo"], lw["g2"], lw["router"]]
    return pl.pallas_call(
        _post_kernel,
        grid=(B, nt),
        in_specs=[tok(NA_W), tok(MLA_W), tok(SWA_W), tok(N_BRANCH * D), tok(D), _mod_spec(n_lat, D)]
        + [const(a) for a in consts],
        out_specs=[tok(D), tok(D), tok(E), tok(LANES)],
        out_shape=[jax.ShapeDtypeStruct((B, T, D), f32), jax.ShapeDtypeStruct((B, T, D), bf16),
                   jax.ShapeDtypeStruct((B, T, E), f32), jax.ShapeDtypeStruct((B, T, LANES), bf16)],
        compiler_params=_params("arbitrary", "arbitrary"),
        name="post_attn",
    )(ona, omla, oswa, gl, x, mod, *consts)


def _lane_cumsum(mask, tri):
    E, n = mask.shape
    carry = jnp.zeros((E, 1), f32)
    outs = []
    for k in range(n // LANES):
        w = _mm(mask[:, k * LANES:(k + 1) * LANES].astype(bf16), tri) + carry
        outs.append(w)
        carry = w[:, LANES - 1:LANES]
    return jnp.concatenate(outs, axis=1)


def _select_slots(aff, cap, base, tri):
    bits = lax.bitcast_convert_type(aff, jnp.int32)
    thr = jnp.zeros((aff.shape[0], 1), jnp.int32)
    for bit in range(30, -1, -1):
        cand = thr | (1 << bit)
        cnt = jnp.sum(jnp.where(bits >= cand, 1.0, 0.0), axis=1, keepdims=True)
        thr = jnp.where(cnt >= cap, cand, thr)
    gt = jnp.where(bits > thr, 1.0, 0.0)
    eq = jnp.where(bits == thr, 1.0, 0.0)
    need = cap - jnp.sum(gt, axis=1, keepdims=True)
    sel = jnp.maximum(gt, jnp.where(_lane_cumsum(eq, tri) <= need, eq, 0.0))
    return jnp.where(sel > 0.0, _lane_cumsum(sel, tri) + base, 0.0)


def _topk_kernel(S, cap_s, cap_l, aff_ref, tri_ref, before_ref, cp_ref, bnd_ref):
    T = aff_ref.shape[1]
    tri = tri_ref[...]
    cp_ref[:, 0:S] = _select_slots(aff_ref[:, 0:S], cap_s, 0.0, tri)
    cp_ref[:, S:T] = _select_slots(aff_ref[:, S:T], cap_l, float(cap_s), tri)
    bnd_ref[...] = _mm(jnp.where(cp_ref[...] > 0.0, 1.0, 0.0).astype(bf16), before_ref[...])


def _topk(aff, S, cap_s, cap_l):
    B, E, T = aff.shape
    tri = jnp.asarray(np.triu(np.ones((LANES, LANES), np.float32)), bf16)
    before = jnp.asarray(np.arange(T)[:, None] < MOE_CHUNK * np.arange(LANES)[None, :], bf16)
    return pl.pallas_call(
        functools.partial(_topk_kernel, S, cap_s, cap_l),
        grid=(B,),
        in_specs=[pl.BlockSpec((None, E, T), lambda b: (b, 0, 0)),
                  pl.BlockSpec((LANES, LANES), lambda b: (0, 0)),
                  pl.BlockSpec((T, LANES), lambda b: (0, 0))],
        out_specs=[pl.BlockSpec((None, E, T), lambda b: (b, 0, 0)),
                   pl.BlockSpec((None, E, LANES), lambda b: (b, 0, 0))],
        out_shape=[jax.ShapeDtypeStruct((B, E, T), f32), jax.ShapeDtypeStruct((B, E, LANES), f32)],
        compiler_params=_params("arbitrary"),
        name="expert_select",
    )(aff, tri, before)


def _window_count(lo, hi, r0):
    return jnp.where(hi > lo, lax.div(hi - r0 + (MOE_WIN - 1), MOE_WIN), 0)


def _gather_kernel(n_e, rows, bnd_ref, h2_ref, affx_ref, cp_ref, xg_ref):
    b, k = pl.program_id(0), pl.program_id(1)
    D = h2_ref.shape[1]

    @pl.when(k == 0)
    def _():
        xg_ref[...] = jnp.zeros_like(xg_ref)

    ex = lax.broadcasted_iota(jnp.int32, (n_e, 1), 0)
    starts = []
    start_col = jnp.zeros((n_e, 1), f32)
    for e in range(n_e):
        r0 = lax.div(bnd_ref[(b * n_e + e) * LANES + k], PACK) * PACK
        starts.append(r0)
        start_col = start_col + jnp.where(ex == e, r0.astype(f32), 0.0)
    rel = cp_ref[...] - start_col
    rel_rows = jnp.broadcast_to(rel[:, None, :], (n_e, MOE_WIN, MOE_CHUNK)).reshape(n_e * MOE_WIN, MOE_CHUNK)
    j1 = lax.broadcasted_iota(jnp.int32, (n_e, MOE_WIN, 1), 1).reshape(n_e * MOE_WIN, 1) + 1
    onehot = jnp.where(rel_rows == j1.astype(f32), 1.0, 0.0).astype(bf16)
    tok, tok_aff = h2_ref[...], affx_ref[...]
    moved = _mm(onehot, tok).astype(bf16)
    moved_aff = _mm(onehot, tok_aff).astype(bf16)
    for e in range(n_e):
        dst = pl.ds(pl.multiple_of(e * rows + starts[e], PACK), MOE_WIN)
        xg_ref[dst, 0:D] += moved[e * MOE_WIN:(e + 1) * MOE_WIN]
        xg_ref[dst, D:D + LANES] += moved_aff[e * MOE_WIN:(e + 1) * MOE_WIN]

    sub1 = lax.broadcasted_iota(jnp.int32, (MOE_WIN, 1), 0) + 1
    n_wins = [_window_count(bnd_ref[(b * n_e + e) * LANES + k], bnd_ref[(b * n_e + e) * LANES + k + 1], starts[e])
              for e in range(n_e)]

    @pl.when(functools.reduce(jnp.maximum, n_wins) > 1)
    def _():
        for e in range(n_e):
            def extra(w, c, e=e):
                hit = rel[e:e + 1, :] == (sub1 + w * MOE_WIN).astype(f32)
                oh = jnp.where(hit, 1.0, 0.0).astype(bf16)
                dst = pl.ds(pl.multiple_of(e * rows + starts[e] + w * MOE_WIN, PACK), MOE_WIN)
                xg_ref[dst, 0:D] += _mm(oh, tok).astype(bf16)
                xg_ref[dst, D:D + LANES] += _mm(oh, tok_aff).astype(bf16)
                return c

            lax.fori_loop(1, n_wins[e], extra, 0)


def _gather(h2, affx, cp, bnd, n_slots):
    B, T, D = h2.shape
    E = cp.shape[1]
    nc = T // MOE_CHUNK
    rows = n_slots + MOE_WIN
    assert n_slots % PACK == 0 and MOE_WIN % PACK == 0 and nc + 1 <= LANES and 3 * E <= LANES
    return pl.pallas_call(
        functools.partial(_gather_kernel, E, rows),
        grid_spec=pltpu.PrefetchScalarGridSpec(
            num_scalar_prefetch=1,
            grid=(B, nc),
            in_specs=[
                pl.BlockSpec((None, MOE_CHUNK, D), lambda b, k, s: (b, k, 0)),
                pl.BlockSpec((None, MOE_CHUNK, LANES), lambda b, k, s: (b, k, 0)),
                pl.BlockSpec((None, E, MOE_CHUNK), lambda b, k, s: (b, 0, k)),
            ],
            out_specs=pl.BlockSpec((None, E * rows, D + LANES), lambda b, k, s: (b, 0, 0)),
        ),
        out_shape=jax.ShapeDtypeStruct((B, E * rows, D + LANES), bf16),
        compiler_params=_params("arbitrary", "arbitrary"),
        name="moe_gather",
    )(bnd, h2, affx, cp)


def _ffn_kernel(n_slots, n_e, xg_ref, wg_ref, wu_ref, wd_ref, y_ref, wg_s, wu_s, wd_s):
    e, b = pl.program_id(0), pl.program_id(1)
    D = wg_ref.shape[0]

    @pl.when(b == 0)
    def _():
        wg_s[...] = wg_ref[...].astype(bf16)
        wu_s[...] = wu_ref[...].astype(bf16)
        wd_s[...] = wd_ref[...].astype(bf16)

    n_b = xg_ref.shape[0]
    xg = jnp.concatenate([xg_ref[s, 0:n_slots, 0:D] for s in range(n_b)], axis=0)
    a = _mm(xg, wg_s[...])
    u = _mm(xg, wu_s[...])
    act = (a * jax.nn.sigmoid(a) * u).astype(bf16)
    lane = lax.broadcasted_iota(jnp.int32, (1, LANES), 1)
    mine = jnp.logical_and(lax.rem(lane, n_e) == e, lane < 3 * n_e)
    parts = jnp.concatenate([xg_ref[s, 0:n_slots, D:D + LANES] for s in range(n_b)], axis=0).astype(f32)
    gate = jnp.sum(jnp.where(mine, parts, 0.0), axis=1, keepdims=True)
    y = (_mm(act, wd_s[...]) * gate).astype(bf16)
    for s in range(n_b):
        y_ref[s, 0:n_slots, :] = y[s * n_slots:(s + 1) * n_slots]
        y_ref[s, n_slots:, :] = jnp.zeros((y_ref.shape[1] - n_slots, y_ref.shape[2]), bf16)


def _ffn(xg, w_gate, w_up, w_down, layer, n_slots):
    B = xg.shape[0]
    _, E, D, F = w_gate.shape
    rows = xg.shape[1] // E
    n_b = 2 if B % 2 == 0 else 1
    return pl.pallas_call(
        functools.partial(_ffn_kernel, n_slots, E),
        grid=(E, B // n_b),
        in_specs=[
            pl.BlockSpec((n_b, None, rows, D + LANES), lambda e, b: (b, e, 0, 0)),
            pl.BlockSpec((None, None, D, F), lambda e, b: (layer, e, 0, 0)),
            pl.BlockSpec((None, None, D, F), lambda e, b: (layer, e, 0, 0)),
            pl.BlockSpec((None, None, F, D), lambda e, b: (layer, e, 0, 0)),
        ],
        out_specs=pl.BlockSpec((n_b, None, rows, D), lambda e, b: (b, e, 0, 0)),
        out_shape=jax.ShapeDtypeStruct((B, E, rows, D), bf16),
        scratch_shapes=[pltpu.VMEM((D, F), bf16), pltpu.VMEM((D, F), bf16), pltpu.VMEM((F, D), bf16)],
        compiler_params=_params("arbitrary", "arbitrary"),
        name="moe_ffn",
    )(xg.reshape(B, E, rows, D + LANES), w_gate, w_up, w_down)


def _combine_kernel(n_e, rows, residual, bnd_ref, y_ref, cpt_ref, expand_ref, *refs):
    o_ref = refs[-1]
    b, k = pl.program_id(0), pl.program_id(1)
    cpt = cpt_ref[...]
    hi_part = jnp.floor(cpt * (1.0 / 32.0))
    lo_part = cpt - 32.0 * hi_part
    expand = expand_ref[...]
    rank = 32.0 * _mm(hi_part.astype(bf16), expand) + _mm(lo_part.astype(bf16), expand)
    lane = lax.broadcasted_iota(jnp.int32, (1, n_e * MOE_WIN), 1)
    lane_e = lax.div(lane, MOE_WIN)
    starts = []
    tgt = (lane - lane_e * MOE_WIN + 1).astype(f32)
    for e in range(n_e):
        r0 = lax.div(bnd_ref[(b * n_e + e) * LANES + k], PACK) * PACK
        starts.append(r0)
        tgt = tgt + jnp.where(lane_e == e, r0.astype(f32), 0.0)
    onehot = jnp.where(rank == tgt, 1.0, 0.0).astype(bf16)
    ycat = jnp.concatenate(
        [y_ref[pl.ds(pl.multiple_of(e * rows + starts[e], PACK), MOE_WIN), :] for e in range(n_e)], axis=0)
    o_ref[...] = _mm(onehot, ycat)

    lane_w = lax.broadcasted_iota(jnp.int32, (1, MOE_WIN), 1)
    n_wins = [_window_count(bnd_ref[(b * n_e + e) * LANES + k], bnd_ref[(b * n_e + e) * LANES + k + 1], starts[e])
              for e in range(n_e)]

    @pl.when(functools.reduce(jnp.maximum, n_wins) > 1)
    def _():
        for e in range(n_e):
            rank_e = rank[:, e * MOE_WIN:(e + 1) * MOE_WIN]

            def extra(w, c, e=e, rank_e=rank_e):
                r = pl.multiple_of(starts[e] + w * MOE_WIN, PACK)
                hit = rank_e == (lane_w + (r + 1)).astype(f32)
                o_ref[...] += _mm(jnp.where(hit, 1.0, 0.0).astype(bf16),
                                  y_ref[pl.ds(pl.multiple_of(e * rows + r, PACK), MOE_WIN), :])
                return c

            lax.fori_loop(1, n_wins[e], extra, 0)

    if residual:
        x_ref, mod_ref = refs[0], refs[1]
        o_ref[...] = x_ref[...] + mod_ref[5:6, :] * o_ref[...]


def _combine(y, cp, bnd, x=None, mod=None, n_tok=None):
    B, E, rows, D = y.shape
    assert rows <= 32 * 256
    residual = x is not None
    n_tok = n_tok if residual else cp.shape[2]
    expand = jnp.asarray(np.kron(np.eye(E), np.ones((1, MOE_WIN))), bf16)
    chunk = pl.BlockSpec((None, MOE_CHUNK, D), lambda b, k, s: (b, k, 0))
    in_specs = [
        pl.BlockSpec((None, E * rows, D), lambda b, k, s: (b, 0, 0)),
        pl.BlockSpec((None, MOE_CHUNK, E), lambda b, k, s: (b, k, 0)),
        pl.BlockSpec(expand.shape, lambda b, k, s: (0, 0)),
    ]
    args = [bnd, y.reshape(B, E * rows, D), cp.transpose(0, 2, 1), expand]
    if residual:
        in_specs += [chunk, pl.BlockSpec((None, None, 6, D), lambda b, k, s: (b, 0, 0, 0))]
        args += [x, mod]
    return pl.pallas_call(
        functools.partial(_combine_kernel, E, rows, residual),
        grid_spec=pltpu.PrefetchScalarGridSpec(
            num_scalar_prefetch=1,
            grid=(B, n_tok // MOE_CHUNK),
            in_specs=in_specs,
            out_specs=chunk,
        ),
        out_shape=jax.ShapeDtypeStruct((B, n_tok, D), f32),
        compiler_params=_params("arbitrary", "arbitrary"),
        name="moe_combine",
    )(*args)


def _rope_tables(S, L):
    t = jnp.arange(S)
    row, col = t // GRID_W, t % GRID_W
    f = ROPE_F
    inv = ROPE_BASE ** (-jnp.arange(f, dtype=f32) / f)
    ar = row.astype(f32)[:, None] * inv
    ac = col.astype(f32)[:, None] * inv
    cos = jnp.concatenate([jnp.cos(ar), jnp.cos(ar), jnp.cos(ac), jnp.cos(ac)], axis=1)
    sin = jnp.concatenate([-jnp.sin(ar), jnp.sin(ar), -jnp.sin(ac), jnp.sin(ac)], axis=1)
    cos = jnp.concatenate([jnp.tile(cos, (1, 2)), jnp.ones((L, LANES), f32)], axis=0)
    sin = jnp.concatenate([jnp.tile(sin, (1, 2)), jnp.zeros((L, LANES), f32)], axis=0)
    return cos, sin


def _pad_heads(w, n_heads, width, padded):
    lead = w.shape[:-1]
    w = w.reshape(lead + (n_heads, width))
    w = jnp.pad(w, [(0, 0)] * len(lead) + [(0, 0), (0, padded - width)])
    return w.reshape(lead + (n_heads * padded,))


def _layer_weights(l, D, w_in, norm2_g, na_q_g, na_k_g, mla_q_norm, mla_w_uq, mla_kv_norm, mla_w_ukv,
                   mla_q_g, mla_k_g, swa_q_g, swa_k_g, w_na_o, w_mla_o, w_swa_o, w_o, router):
    o = np.cumsum((0, NA_W, NA_W, NA_W, MLA_Q_RANK, MLA_KV_RANK, MLA_ROPE, SWA_W, SWA_KV_W, SWA_KV_W))
    o_naq, o_nak, o_nav, o_cq, o_ckv, o_kr, o_swq, o_swk, o_swv, o_gl = (int(v) for v in o)
    wi = w_in[l]
    swq = wi[:, o_swq:o_swq + SWA_W].reshape(D, SWA_HEADS, SWA_DIM)[:, SWA_HEAD_ORDER, :].reshape(D, SWA_W)
    kr = jnp.pad(wi[:, o_kr:o_kr + MLA_ROPE], ((0, 0), (0, LANES - MLA_ROPE)))
    w_all = jnp.concatenate([
        wi[:, o_naq:o_naq + 3 * NA_W], swq, wi[:, o_cq:o_cq + MLA_Q_RANK], wi[:, o_ckv:o_ckv + MLA_KV_RANK], kr,
        wi[:, o_swk:o_swk + 2 * SWA_KV_W], 0.5 * wi[:, o_gl:]], axis=1).astype(bf16)
    ukv = mla_w_ukv[l].reshape(MLA_KV_RANK, MLA_HEADS, MLA_NOPE + MLA_V)
    swa_o = w_swa_o[l].reshape(SWA_HEADS, SWA_DIM, D)[SWA_HEAD_ORDER, :, :].reshape(SWA_W, D)
    r_hi = router[l].astype(bf16)
    r_lo = (router[l] - r_hi.astype(f32)).astype(bf16)
    return dict(
        w_all=w_all,
        naq_g=jnp.tile(na_q_g[l] * (NA_DIM ** -0.5 * LOG2E), NA_HEADS)[None],
        nak_g=jnp.tile(na_k_g[l], NA_HEADS)[None],
        swq_g=jnp.tile(swa_q_g[l] * (SWA_DIM ** -0.5 * LOG2E), SWA_HEADS)[None],
        swk_g=jnp.tile(swa_k_g[l], SWA_KV_HEADS)[None],
        qn=mla_q_norm[l][None], kvn=mla_kv_norm[l][None],
        wuq=_pad_heads(mla_w_uq[l], MLA_HEADS, MLA_QK, MLA_PAD).astype(bf16),
        wuk=ukv[:, :, :MLA_NOPE].reshape(MLA_KV_RANK, MLA_HEADS * MLA_NOPE).astype(bf16),
        wuv=ukv[:, :, MLA_NOPE:].reshape(MLA_KV_RANK, MLA_W).astype(bf16),
        qg=jnp.tile(jnp.pad(mla_q_g[l] * (MLA_QK ** -0.5 * LOG2E), (0, MLA_PAD - MLA_QK)), MLA_HEADS)[None],
        kg=jnp.tile(jnp.pad(mla_k_g[l], (0, MLA_PAD - MLA_QK)), MLA_HEADS)[None],
        wna=w_na_o[l].astype(bf16), wmla=w_mla_o[l].astype(bf16), wswa=swa_o.astype(bf16),
        wo=(0.5 * w_o[l]).astype(bf16), g2=norm2_g[l][None], router=jnp.concatenate([r_hi, r_lo], axis=1),
    )


def kernel(x, c, ctx, c_ctx, norm1_g, norm2_g, w_ada, b_ada, w_in, na_q_g, na_k_g, na_rpb, mla_q_norm, mla_w_uq, mla_kv_norm, mla_w_ukv, mla_q_g, mla_k_g, swa_q_g, swa_k_g, swa_sink, w_na_o, w_mla_o, w_swa_o, w_o, router, w_gate, w_up, w_down):
    B, S, D = x.shape
    L = ctx.shape[1]
    depth = w_in.shape[0]
    E = router.shape[2]
    T = S + L
    n_lat = S // TT
    rows = S // GRID_W
    assert S % TT == 0 and L % TM == 0 and L <= TT and rows >= NA_BAND_ROWS and S >= SWA_KEYS and B < COND_ROWS
    cap_s = CAPACITY_FACTOR * S // E
    cap_l = CAPACITY_FACTOR * L // E

    cpad = jnp.zeros((COND_ROWS, D), f32).at[:B].set(c).at[B].set(c_ctx)
    mod = _ada(cpad, w_ada, b_ada).reshape(depth, COND_ROWS, 6, D)
    mod = jnp.stack([mod[:, :B], jnp.broadcast_to(mod[:, B:B + 1], (depth, B, 6, D))], axis=2)

    cos, sin = _rope_tables(S, L)
    gmat = jnp.asarray(np.kron(np.eye(NA_HEADS), np.full((NA_DIM, NA_DIM), 1.0 / NA_DIM)), bf16)

    xs = jnp.concatenate([x, ctx], axis=1)
    moe = None
    for l in range(depth):
        lw = _layer_weights(l, D, w_in, norm2_g, na_q_g, na_k_g, mla_q_norm, mla_w_uq, mla_kv_norm, mla_w_ukv,
                            mla_q_g, mla_k_g, swa_q_g, swa_k_g, w_na_o, w_mla_o, w_swa_o, w_o, router)
        modp = mod[l - 1] if l > 0 else None
        xs, proj, gl = _in_proj(xs, moe, modp, mod[l], norm1_g[l][None], lw["w_all"], n_lat)
        qka, mq, mk, mv = _prep(proj, cos, sin, gmat, lw)
        ona = _na(qka, proj, na_rpb[l], S)
        omla = _mla(mq, mk, mv, S)
        oswa = _swa(qka, proj, swa_sink[l] * LOG2E, S)
        xs, h2, aff, affx = _post(ona, omla, oswa, gl, xs, mod[l], lw, n_lat)
        cp, bnd = _topk(aff.transpose(0, 2, 1), S, cap_s, cap_l)
        bnd = bnd.astype(jnp.int32).reshape(-1)
        xg = _gather(h2, affx, cp, bnd, cap_s + cap_l)
        y = _ffn(xg, w_gate, w_up, w_down, l, cap_s + cap_l)
        if l == depth - 1:
            return _combine(y, cp, bnd, xs, mod[l], S)
        moe = _combine(y, cp, bnd)
```

```python
import functools

import numpy as np
import jax
import jax.numpy as jnp
from jax import lax
from jax.experimental import pallas as pl
from jax.experimental.pallas import tpu as pltpu

GRID_W = 64
NA_HEADS, NA_DIM, NA_WIN_ROWS, NA_WIN_COLS = 6, 64, 8, 16
MLA_HEADS, MLA_Q_RANK, MLA_KV_RANK, MLA_NOPE, MLA_ROPE, MLA_V = 4, 256, 128, 128, 64, 128
MLA_QK = MLA_NOPE + MLA_ROPE
MLA_PAD = 256
SWA_HEADS, SWA_KV_HEADS, SWA_DIM, SWA_WINDOW = 6, 2, 64, 128
CAPACITY_FACTOR = 2
N_BRANCH = 3
ROPE_BASE = 10000.0
EPS = 1e-6

NA_W = NA_HEADS * NA_DIM
SWA_W = SWA_HEADS * SWA_DIM
SWA_KV_W = SWA_KV_HEADS * SWA_DIM
MLA_W = MLA_HEADS * MLA_V

LANES = 128
PACK = 16
COND_ROWS = 16
ROPE_F = SWA_DIM // 4
TM = 256
TT = 512
MLA_TQ = 1024
MLA_TK = 256
LOG2E = 1.4426950408889634
NA_TILE_ROWS = TM // GRID_W
NA_BAND_ROWS = NA_TILE_ROWS + NA_WIN_ROWS - 1
NA_BAND = NA_BAND_ROWS * GRID_W
SWA_KEYS = TM + 2 * SWA_WINDOW
MOE_CHUNK = 256
MOE_WIN = 64
NEG = -1e30
VMEM_LIMIT = 56 * 1024 * 1024

P_NAQ, P_NAK, P_NAV, P_SWQ = 0, 384, 768, 1152
P_CQ, P_CKV, P_KR, P_SWK, P_SWV = 1536, 1792, 1920, 2048, 2176
PROJ_W = 2304
N_CHUNK = 768
A_NAQ, A_NAK, A_SWQ, A_SWK = 0, 384, 768, 1152
QKA_W = 1280
SWA_HEAD_ORDER = (0, 3, 1, 4, 2, 5)

bf16 = jnp.bfloat16
f32 = jnp.float32


def _mm(a, b):
    return jnp.dot(a, b, preferred_element_type=f32)


def _nt(a, b):
    return lax.dot_general(a, b, (((1,), (1,)), ((), ())), preferred_element_type=f32)


def _params(*sem):
    return pltpu.CompilerParams(dimension_semantics=sem, vmem_limit_bytes=VMEM_LIMIT)


def _ada_kernel(c_ref, w_ref, b_ref, o_ref):
    a = c_ref[...]
    a = (a * jax.nn.sigmoid(a)).astype(bf16)
    o_ref[...] = _mm(a, w_ref[...].astype(bf16)) + b_ref[...]


def _ada(cpad, w_ada, b_ada):
    L, D, N = w_ada.shape
    tn = 1536
    return pl.pallas_call(
        _ada_kernel,
        grid=(L, N // tn),
        in_specs=[
            pl.BlockSpec((COND_ROWS, D), lambda l, j: (0, 0)),
            pl.BlockSpec((None, D, tn), lambda l, j: (l, 0, j)),
            pl.BlockSpec((None, 1, tn), lambda l, j: (l, 0, j)),
        ],
        out_specs=pl.BlockSpec((None, COND_ROWS, tn), lambda l, j: (l, 0, j)),
        out_shape=jax.ShapeDtypeStruct((L, COND_ROWS, N), f32),
        compiler_params=_params("arbitrary", "arbitrary"),
        name="ada",
    )(cpad, w_ada, b_ada.reshape(L, 1, N))


def _in_proj_kernel(has_moe, n_gl_chunks, *refs):
    if has_moe:
        x_ref, moe_ref, modp_ref, mod_ref, g_ref, w_ref, xo_ref, proj_ref, gl_ref = refs
    else:
        x_ref, mod_ref, g_ref, w_ref, proj_ref, gl_ref = refs
    n_sub = 2
    sub = x_ref.shape[0] // n_sub
    for r in range(n_sub):
        rows = slice(r * sub, (r + 1) * sub)
        x = x_ref[rows, :]
        if has_moe:
            x = x + modp_ref[5:6, :] * moe_ref[rows, :]
            xo_ref[rows, :] = x
        ms = jnp.mean(x * x, axis=-1, keepdims=True)
        h = x * lax.rsqrt(ms + EPS) * (g_ref[...] * (1.0 + mod_ref[1:2, :])) + mod_ref[0:1, :]
        h = h.astype(bf16)
        for c in range(PROJ_W // N_CHUNK):
            sl = slice(c * N_CHUNK, (c + 1) * N_CHUNK)
            proj_ref[rows, sl] = _mm(h, w_ref[:, sl]).astype(bf16)
        for c in range(n_gl_chunks):
            sl = slice(c * N_CHUNK, (c + 1) * N_CHUNK)
            gl_ref[rows, sl] = _mm(h, w_ref[:, PROJ_W + c * N_CHUNK:PROJ_W + (c + 1) * N_CHUNK]).astype(bf16)


def _mod_spec(n_lat, D):
    return pl.BlockSpec((None, None, 6, D), lambda b, i: (b, jnp.where(i >= n_lat, 1, 0), 0, 0))


def _in_proj(x, moe, modp, mod, g1, w, n_lat):
    B, T, D = x.shape
    nt = pl.cdiv(T, TT)
    has_moe = moe is not None
    n_gl_chunks = N_BRANCH * D // N_CHUNK
    tok = pl.BlockSpec((None, TT, D), lambda b, i: (b, i, 0))
    in_specs = [tok]
    args = [x]
    if has_moe:
        in_specs += [tok, _mod_spec(n_lat, D)]
        args += [moe, modp]
    in_specs += [
        _mod_spec(n_lat, D),
        pl.BlockSpec((1, D), lambda b, i: (0, 0)),
        pl.BlockSpec(w.shape, lambda b, i: (0, 0), pipeline_mode=pl.Buffered(1)),
    ]
    args += [mod, g1, w]
    out_specs = [
        pl.BlockSpec((None, TT, PROJ_W), lambda b, i: (b, i, 0)),
        pl.BlockSpec((None, TT, N_BRANCH * D), lambda b, i: (b, i, 0)),
    ]
    out_shape = [
        jax.ShapeDtypeStruct((B, T, PROJ_W), bf16),
        jax.ShapeDtypeStruct((B, T, N_BRANCH * D), bf16),
    ]
    if has_moe:
        out_specs = [tok] + out_specs
        out_shape = [jax.ShapeDtypeStruct((B, T, D), f32)] + out_shape
    res = pl.pallas_call(
        functools.partial(_in_proj_kernel, has_moe, n_gl_chunks),
        grid=(B, nt),
        in_specs=in_specs,
        out_specs=out_specs,
        out_shape=out_shape,
        compiler_params=_params("arbitrary", "arbitrary"),
        name="in_proj",
    )(*args)
    if has_moe:
        return res
    return [x] + list(res)


def _prep_kernel(proj_ref, cos_ref, sin_ref, perm_ref, gmat_ref, naq_g, nak_g, swq_g, swk_g,
                 qn_ref, wuq_ref, qg_ref, kvn_ref, wuk_ref, wuv_ref, kg_ref,
                 qka_ref, mq_ref, mk_ref, mv_ref):
    cos = cos_ref[...]
    sin = sin_ref[...]
    perm = perm_ref[...]

    def rope(x):
        hi = x.astype(bf16)
        lo = (x - hi.astype(f32)).astype(bf16)
        partner = _mm(hi, perm) + _mm(lo, perm)
        return x * cos + partner * sin

    def headnorm(x, gain):
        w = x.shape[1]
        ms = _mm((x * x).astype(bf16), gmat_ref[:w, :w])
        return x * lax.rsqrt(ms + EPS) * gain

    def rmsnorm(x, gain):
        ms = jnp.mean(x * x, axis=-1, keepdims=True)
        return x * lax.rsqrt(ms + EPS) * gain

    naq = headnorm(proj_ref[:, P_NAQ:P_NAQ + NA_W].astype(f32), naq_g[...])
    qka_ref[:, A_NAQ:A_NAQ + NA_W] = naq.astype(bf16)
    nak = headnorm(proj_ref[:, P_NAK:P_NAK + NA_W].astype(f32), nak_g[...])
    qka_ref[:, A_NAK:A_NAK + NA_W] = nak.astype(bf16)

    swq = headnorm(proj_ref[:, P_SWQ:P_SWQ + SWA_W].astype(f32), swq_g[...])
    for p in range(SWA_W // LANES):
        sl = slice(p * LANES, (p + 1) * LANES)
        qka_ref[:, A_SWQ + p * LANES:A_SWQ + (p + 1) * LANES] = rope(swq[:, sl]).astype(bf16)
    swk = headnorm(proj_ref[:, P_SWK:P_SWK + SWA_KV_W].astype(f32), swk_g[...])
    qka_ref[:, A_SWK:A_SWK + SWA_KV_W] = rope(swk).astype(bf16)

    cq = rmsnorm(proj_ref[:, P_CQ:P_CQ + MLA_Q_RANK].astype(f32), qn_ref[...]).astype(bf16)
    q = _mm(cq, wuq_ref[...])
    ckv = rmsnorm(proj_ref[:, P_CKV:P_CKV + MLA_KV_RANK].astype(f32), kvn_ref[...]).astype(bf16)
    kn = _mm(ckv, wuk_ref[...])
    v = _mm(ckv, wuv_ref[...]).astype(bf16)
    ones_col = jnp.where(lax.broadcasted_iota(jnp.int32, (v.shape[0], LANES), 1) == 0, 1.0, 0.0).astype(bf16)
    for h in range(MLA_HEADS):
        mv_ref[:, h * MLA_PAD:h * MLA_PAD + MLA_V] = v[:, h * MLA_V:(h + 1) * MLA_V]
        mv_ref[:, h * MLA_PAD + MLA_V:(h + 1) * MLA_PAD] = ones_col
    kr = proj_ref[:, P_KR:P_KR + LANES].astype(f32)
    kr_ss = jnp.sum(kr * kr, axis=-1, keepdims=True)
    for h in range(MLA_HEADS):
        o = h * MLA_PAD
        qh = q[:, o:o + MLA_PAD]
        r = lax.rsqrt(jnp.sum(qh * qh, axis=-1, keepdims=True) * (1.0 / MLA_QK) + EPS)
        qh = qh * r * qg_ref[:, o:o + MLA_PAD]
        mq_ref[:, o:o + LANES] = qh[:, :LANES].astype(bf16)
        mq_ref[:, o + LANES:o + MLA_PAD] = rope(qh[:, LANES:]).astype(bf16)
        kh = kn[:, h * MLA_NOPE:(h + 1) * MLA_NOPE]
        r = lax.rsqrt((jnp.sum(kh * kh, axis=-1, keepdims=True) + kr_ss) * (1.0 / MLA_QK) + EPS)
        mk_ref[:, o:o + LANES] = (kh * r * kg_ref[:, o:o + LANES]).astype(bf16)
        mk_ref[:, o + LANES:o + MLA_PAD] = rope(kr * r * kg_ref[:, o + LANES:o + MLA_PAD]).astype(bf16)


def _prep(proj, cos, sin, gmat, lw):
    B, T, _ = proj.shape
    nt = pl.cdiv(T, TT)

    def const(a):
        return pl.BlockSpec(a.shape, lambda b, i: (0,) * a.ndim)

    lane = np.arange(LANES)
    perm = jnp.asarray(lane[:, None] == (lane[None, :] ^ ROPE_F), bf16)
    consts = [perm, gmat, lw["naq_g"], lw["nak_g"], lw["swq_g"], lw["swk_g"], lw["qn"], lw["wuq"], lw["qg"],
              lw["kvn"], lw["wuk"], lw["wuv"], lw["kg"]]
    widths = (QKA_W, MLA_HEADS * MLA_PAD, MLA_HEADS * MLA_PAD, MLA_HEADS * MLA_PAD)
    return pl.pallas_call(
        _prep_kernel,
        grid=(B, nt),
        in_specs=[
            pl.BlockSpec((None, TT, PROJ_W), lambda b, i: (b, i, 0)),
            pl.BlockSpec((TT, LANES), lambda b, i: (i, 0)),
            pl.BlockSpec((TT, LANES), lambda b, i: (i, 0)),
        ] + [const(a) for a in consts],
        out_specs=[pl.BlockSpec((None, TT, w), lambda b, i: (b, i, 0)) for w in widths],
        out_shape=[jax.ShapeDtypeStruct((B, T, w), bf16) for w in widths],
        compiler_params=_params("arbitrary", "arbitrary"),
        name="prep",
    )(proj, cos, sin, *consts)


def _half_masks():
    lane = lax.broadcasted_iota(jnp.int32, (1, LANES), 1)
    lo = jnp.where(lane < 64, 1.0, 0.0).astype(bf16)
    return lo, (1.0 - lo.astype(f32)).astype(bf16)


def _pairs_attend(jobs, masks):
    scored = []
    for qp, parts, _ in jobs:
        q2 = jnp.concatenate([qp * masks[0], qp * masks[1]], axis=0)
        scores = []
        for k, _, bias in parts:
            s = _nt(q2, k)
            scores.append(s if bias is None else s + bias)
        scored.append(scores)
    normed = []
    for (qp, _, sinks), scores in zip(jobs, scored):
        n_q = qp.shape[0]
        m = functools.reduce(jnp.maximum, [jnp.max(s, axis=-1, keepdims=True) for s in scores])
        if sinks is not None:
            first = lax.broadcasted_iota(jnp.int32, (2 * n_q, 1), 0) < n_q
            sink = jnp.where(first, sinks[0], sinks[1])
            m = jnp.maximum(m, sink)
        ps = [jnp.exp2(s - m) for s in scores]
        l = functools.reduce(jnp.add, [jnp.sum(p, axis=-1, keepdims=True) for p in ps])
        if sinks is not None:
            l = l + jnp.exp2(sink - m)
        normed.append((ps, l))
    lane = lax.broadcasted_iota(jnp.int32, (1, LANES), 1)
    outs = []
    for (qp, parts, _), (ps, l) in zip(jobs, normed):
        n_q = qp.shape[0]
        o = functools.reduce(jnp.add, [_mm(p.astype(bf16), v) for p, (_, v, _) in zip(ps, parts)]) / l
        outs.append(jnp.where(lane < 64, o[:n_q], o[n_q:]))
    return outs


def _pair_attend(qp, parts, masks, sinks=None):
    return _pairs_attend([(qp, parts, sinks)], masks)[0]


def _na_kernel(n_lat, S, rows, offs, q_ref, k_ref, v_ref, tall_ref, rmask_ref, o_ref):
    i = pl.program_id(1)
    masks = _half_masks()
    T = k_ref.shape[0]

    def latent(ty):
        start = GRID_W * jnp.clip(NA_TILE_ROWS * i - NA_WIN_ROWS // 2, 0, rows - NA_BAND_ROWS)
        start = pl.multiple_of(start, GRID_W)
        jobs = []
        for p in range(NA_W // LANES):
            sl = slice(p * LANES, (p + 1) * LANES)
            slabs = []
            for head in (2 * p, 2 * p + 1):
                for qr in range(NA_TILE_ROWS):
                    odd = int(offs[ty, qr]) % 2
                    lo = (int(offs[ty, qr]) - odd) * GRID_W
                    slabs.append(tall_ref[head, odd, :, lo:lo + NA_BAND] + rmask_ref[ty, qr:qr + 1, :])
            band = (k_ref[pl.ds(start, NA_BAND), sl], v_ref[pl.ds(start, NA_BAND), sl],
                    jnp.concatenate(slabs, axis=0))
            ctx = (k_ref[S:T, sl], v_ref[S:T, sl], None)
            jobs.append((q_ref[:, sl], [band, ctx], None))
        for p, o in enumerate(_pairs_attend(jobs, masks)):
            o_ref[:, p * LANES:(p + 1) * LANES] = o.astype(bf16)

    pl.when(i == 0)(functools.partial(latent, 0))
    pl.when(jnp.logical_and(i > 0, i < n_lat - 1))(functools.partial(latent, 1))
    pl.when(i == n_lat - 1)(functools.partial(latent, 2))

    @pl.when(i >= n_lat)
    def _():
        for p in range(NA_W // LANES):
            sl = slice(p * LANES, (p + 1) * LANES)
            ctx = (k_ref[S:T, sl], v_ref[S:T, sl], None)
            o_ref[:, sl] = _pair_attend(q_ref[:, sl], [ctx], masks).astype(bf16)


def _na_geometry(rows):
    n_tiles = rows // NA_TILE_ROWS
    band = np.arange(NA_BAND_ROWS)
    dr0 = np.zeros((3, NA_TILE_ROWS), np.int64)
    valid = np.zeros((3, NA_TILE_ROWS, NA_BAND_ROWS), bool)
    for ty, rt in enumerate((0, 1, n_tiles - 1)):
        bs = int(np.clip(NA_TILE_ROWS * rt - NA_WIN_ROWS // 2, 0, rows - NA_BAND_ROWS))
        for qr in range(NA_TILE_ROWS):
            r = NA_TILE_ROWS * rt + qr
            s_r = int(np.clip(r - NA_WIN_ROWS // 2, 0, rows - NA_WIN_ROWS))
            dr0[ty, qr] = bs - r + NA_WIN_ROWS - 1
            valid[ty, qr] = (bs + band >= s_r) & (bs + band < s_r + NA_WIN_ROWS)
    front = int(max(0, -dr0.min()))
    back = int(max(0, dr0.max() + NA_BAND_ROWS - (2 * NA_WIN_ROWS - 1)))
    row_mask = np.where(np.repeat(valid, GRID_W, axis=2), 0.0, NEG).astype(np.float32)
    return dr0 + front, front, back, row_mask


def _na_bias(rpb, front, back):
    col = np.arange(GRID_W)
    c0 = np.clip(col - NA_WIN_COLS // 2, 0, GRID_W - NA_WIN_COLS)
    col_ok = (col[None, :] >= c0[:, None]) & (col[None, :] < c0[:, None] + NA_WIN_COLS)
    dc = np.clip(col[None, :] - col[:, None] + NA_WIN_COLS - 1, 0, 2 * NA_WIN_COLS - 2)
    pick = jnp.asarray(np.arange(2 * NA_WIN_COLS - 1)[:, None, None] == dc[None], f32)
    tz = jnp.einsum("hrd,dqk->hrqk", rpb, pick, precision=lax.Precision.HIGHEST)
    tz = jnp.where(col_ok, tz * LOG2E, NEG)
    n_blocks = front + tz.shape[1] + back
    width = pl.cdiv(n_blocks * GRID_W, LANES) * LANES
    tall = tz.transpose(0, 2, 1, 3).reshape(NA_HEADS, GRID_W, -1)
    tall = jnp.pad(tall, ((0, 0), (0, 0), (front * GRID_W, width + GRID_W - (front + tz.shape[1]) * GRID_W)),
                   constant_values=NEG)
    return jnp.stack([tall[:, :, :width], tall[:, :, GRID_W:GRID_W + width]], axis=1)


def _mla_kernel(n_q, S, q_ref, k_ref, v_ref, o_ref):
    i = pl.program_id(2)
    T = k_ref.shape[0]

    def attend(q, bounds):
        m_run = acc = None
        for lo, hi in bounds:
            s = _nt(q, k_ref[lo:hi, :])
            m_new = jnp.max(s, axis=-1, keepdims=True)
            if m_run is not None:
                m_new = jnp.maximum(m_run, m_new)
            pv = _mm(jnp.exp2(s - m_new).astype(bf16), v_ref[lo:hi, :])
            acc = pv if acc is None else acc * jnp.exp2(m_run - m_new) + pv
            m_run = m_new
        return (acc[:, :MLA_V] / acc[:, MLA_V:MLA_V + 1]).astype(bf16)

    @pl.when(i < n_q)
    def _():
        chunks = [(lo, lo + MLA_TK) for lo in range(0, S, MLA_TK)] + [(S, T)]
        o_ref[...] = attend(q_ref[...], chunks)

    @pl.when(i >= n_q)
    def _():
        o_ref[0:T - S, :] = attend(q_ref[0:T - S, :], [(S, T)])


def _mla(mq, mk, mv, S):
    B, T, _ = mq.shape
    tq = min(MLA_TQ, S)
    n_q = S // tq
    assert S % tq == 0 and S % MLA_TK == 0 and T - S <= tq
    return pl.pallas_call(
        functools.partial(_mla_kernel, n_q, S),
        grid=(B, MLA_HEADS, n_q + 1),
        in_specs=[
            pl.BlockSpec((None, tq, MLA_PAD), lambda b, h, i: (b, i, h)),
            pl.BlockSpec((None, T, MLA_PAD), lambda b, h, i: (b, 0, h)),
            pl.BlockSpec((None, T, MLA_PAD), lambda b, h, i: (b, 0, h)),
        ],
        out_specs=pl.BlockSpec((None, tq, MLA_V), lambda b, h, i: (b, i, h)),
        out_shape=jax.ShapeDtypeStruct((B, T, MLA_W), bf16),
        compiler_params=_params("arbitrary", "arbitrary", "arbitrary"),
        name="mla_attn",
    )(mq, mk, mv)


def _swa_kernel(n_lat, S, sink_ref, q_ref, k_ref, v_ref, band_ref, o_ref):
    i = pl.program_id(1)
    masks = _half_masks()
    T = k_ref.shape[0]
    kc, vc = k_ref[S:T, :], v_ref[S:T, :]

    def sinks(p):
        return (sink_ref[SWA_HEAD_ORDER[2 * p]], sink_ref[SWA_HEAD_ORDER[2 * p + 1]])

    @pl.when(i < n_lat)
    def _():
        start = pl.multiple_of(jnp.clip(i * TM - SWA_WINDOW, 0, S - SWA_KEYS), SWA_WINDOW)
        band = band_ref[jnp.where(i == 0, 0, jnp.where(i == n_lat - 1, 2, 1))]
        kb, vb = k_ref[pl.ds(start, SWA_KEYS), :], v_ref[pl.ds(start, SWA_KEYS), :]
        jobs = [(q_ref[:, p * LANES:(p + 1) * LANES], [(kb, vb, band), (kc, vc, None)], sinks(p))
                for p in range(SWA_W // LANES)]
        for p, o in enumerate(_pairs_attend(jobs, masks)):
            o_ref[:, p * LANES:(p + 1) * LANES] = o.astype(bf16)

    @pl.when(i >= n_lat)
    def _():
        for p in range(SWA_W // LANES):
            sl = slice(p * LANES, (p + 1) * LANES)
            o_ref[:, sl] = _pair_attend(q_ref[:, sl], [(kc, vc, None)], masks, sinks(p)).astype(bf16)


def _swa_band_masks(S):
    n_lat = S // TM
    out = np.zeros((3, 2 * TM, SWA_KEYS), np.float32)
    for ty, i in enumerate((0, 1, n_lat - 1)):
        start = int(np.clip(i * TM - SWA_WINDOW, 0, S - SWA_KEYS))
        qpos = i * TM + np.tile(np.arange(TM), 2)
        kpos = start + np.arange(SWA_KEYS)
        out[ty] = np.where(np.abs(qpos[:, None] - kpos[None, :]) <= SWA_WINDOW, 0.0, NEG)
    return jnp.asarray(out)


def _local_kernel(n_lat, S, rows, offs, sink_ref, q_na, k_na, v_na, tall_ref, rmask_ref, q_sw, k_sw, v_sw, band_ref,
                  o_na, o_sw):
    _na_kernel(n_lat, S, rows, offs, q_na, k_na, v_na, tall_ref, rmask_ref, o_na)
    _swa_kernel(n_lat, S, sink_ref, q_sw, k_sw, v_sw, band_ref, o_sw)


def _local_attn(qka, proj, rpb, sink, S):
    B, T, _ = qka.shape
    nt, n_lat, rows = T // TM, S // TM, S // GRID_W
    assert n_lat >= 3
    offs, front, back, row_mask = _na_geometry(rows)
    tall = _na_bias(rpb, front, back)
    row_mask = jnp.asarray(row_mask)
    band = _swa_band_masks(S)
    return pl.pallas_call(
        functools.partial(_local_kernel, n_lat, S, rows, offs),
        grid=(B, nt),
        in_specs=[
            pl.BlockSpec(memory_space=pltpu.SMEM),
            pl.BlockSpec((None, TM, NA_W), lambda b, i: (b, i, A_NAQ // NA_W)),
            pl.BlockSpec((None, T, NA_W), lambda b, i: (b, 0, A_NAK // NA_W)),
            pl.BlockSpec((None, T, NA_W), lambda b, i: (b, 0, P_NAV // NA_W)),
            pl.BlockSpec(tall.shape, lambda b, i: (0, 0, 0, 0)),
            pl.BlockSpec(row_mask.shape, lambda b, i: (0, 0, 0)),
            pl.BlockSpec((None, TM, SWA_W), lambda b, i: (b, i, A_SWQ // SWA_W)),
            pl.BlockSpec((None, T, SWA_KV_W), lambda b, i: (b, 0, A_SWK // SWA_KV_W)),
            pl.BlockSpec((None, T, SWA_KV_W), lambda b, i: (b, 0, P_SWV // SWA_KV_W)),
            pl.BlockSpec(band.shape, lambda b, i: (0, 0, 0)),
        ],
        out_specs=[pl.BlockSpec((None, TM, NA_W), lambda b, i: (b, i, 0)),
                   pl.BlockSpec((None, TM, SWA_W), lambda b, i: (b, i, 0))],
        out_shape=[jax.ShapeDtypeStruct((B, T, NA_W), bf16), jax.ShapeDtypeStruct((B, T, SWA_W), bf16)],
        compiler_params=_params("arbitrary", "arbitrary"),
        name="local_attn",
    )(sink, qka, qka, proj, tall, row_mask, qka, qka, proj, band)


def _post_kernel(ona_ref, omla_ref, oswa_ref, gl_ref, x_ref, mod_ref, wna_ref, wmla_ref, wswa_ref, wo_ref,
                 g2_ref, rt_ref, xo_ref, h2_ref, aff_ref, affx_ref):
    D = x_ref.shape[1]
    n_e = aff_ref.shape[1]
    n_sub = 2
    sub = x_ref.shape[0] // n_sub
    halves = [slice(r * sub, (r + 1) * sub) for r in range(n_sub)]
    gated = []
    for rows in halves:
        merged = None
        for j, (o_ref, w_ref) in enumerate(((ona_ref, wna_ref), (omla_ref, wmla_ref), (oswa_ref, wswa_ref))):
            y = _mm(o_ref[rows, :], w_ref[...])
            gy = (jnp.tanh(gl_ref[rows, j * D:(j + 1) * D].astype(f32)) + 1.0) * y
            merged = gy if merged is None else merged + gy
        gated.append(merged.astype(bf16))
    mixed = [_mm(m, wo_ref[...]) for m in gated]
    for rows, res in zip(halves, mixed):
        x = x_ref[rows, :] + mod_ref[2:3, :] * res
        xo_ref[rows, :] = x
        ms = jnp.mean(x * x, axis=-1, keepdims=True)
        h2 = x * lax.rsqrt(ms + EPS) * (g2_ref[...] * (1.0 + mod_ref[4:5, :])) + mod_ref[3:4, :]
        h_hi = h2.astype(bf16)
        h2_ref[rows, :] = h_hi
        h_lo = (h2 - h_hi.astype(f32)).astype(bf16)
        parts = _mm(h_hi, rt_ref[...]) + _mm(h_lo, rt_ref[...])
        logits = parts[:, :n_e] + parts[:, n_e:]
        e = jnp.exp(logits - jnp.max(logits, axis=1, keepdims=True))
        aff = e / jnp.sum(e, axis=1, keepdims=True)
        aff_ref[rows, :] = aff
        p0 = aff.astype(bf16)
        r1 = aff - p0.astype(f32)
        p1 = r1.astype(bf16)
        p2 = (r1 - p1.astype(f32)).astype(bf16)
        pad = jnp.zeros((sub, LANES - 3 * n_e), bf16)
        affx_ref[rows, :] = jnp.concatenate([p0, p1, p2, pad], axis=1)


def _post(ona, omla, oswa, gl, x, mod, lw, n_lat):
    B, T, D = x.shape
    nt = pl.cdiv(T, TT)
    E = lw["router"].shape[1] // 2

    def tok(w):
        return pl.BlockSpec((None, TT, w), lambda b, i: (b, i, 0))

    def const(a):
        return pl.BlockSpec(a.shape, lambda b, i: (0,) * a.ndim)

    consts = [lw["wna"], lw["wmla"], lw["wswa"], lw["wo"], lw["g2"], lw["router"]]
    return pl.pallas_call(
        _post_kernel,
        grid=(B, nt),
        in_specs=[tok(NA_W), tok(MLA_W), tok(SWA_W), tok(N_BRANCH * D), tok(D), _mod_spec(n_lat, D)]
        + [const(a) for a in consts],
        out_specs=[tok(D), tok(D), tok(E), tok(LANES)],
        out_shape=[jax.ShapeDtypeStruct((B, T, D), f32), jax.ShapeDtypeStruct((B, T, D), bf16),
                   jax.ShapeDtypeStruct((B, T, E), f32), jax.ShapeDtypeStruct((B, T, LANES), bf16)],
        compiler_params=_params("arbitrary", "arbitrary"),
        name="post_attn",
    )(ona, omla, oswa, gl, x, mod, *consts)


def _lane_cumsum(mask, tri):
    E, n = mask.shape
    carry = jnp.zeros((E, 1), f32)
    outs = []
    for k in range(n // LANES):
        w = _mm(mask[:, k * LANES:(k + 1) * LANES].astype(bf16), tri) + carry
        outs.append(w)
        carry = w[:, LANES - 1:LANES]
    return jnp.concatenate(outs, axis=1)


def _select_slots(aff, cap, base, tri):
    bits = lax.bitcast_convert_type(aff, jnp.int32)
    thr = jnp.zeros((aff.shape[0], 1), jnp.int32)
    for bit in range(30, -1, -1):
        cand = thr | (1 << bit)
        cnt = jnp.sum(jnp.where(bits >= cand, 1.0, 0.0), axis=1, keepdims=True)
        thr = jnp.where(cnt >= cap, cand, thr)
    gt = jnp.where(bits > thr, 1.0, 0.0)
    eq = jnp.where(bits == thr, 1.0, 0.0)
    need = cap - jnp.sum(gt, axis=1, keepdims=True)
    sel = jnp.maximum(gt, jnp.where(_lane_cumsum(eq, tri) <= need, eq, 0.0))
    return jnp.where(sel > 0.0, _lane_cumsum(sel, tri) + base, 0.0)


def _topk_kernel(S, cap_s, cap_l, aff_ref, tri_ref, before_ref, cp_ref, bnd_ref):
    T = aff_ref.shape[1]
    tri = tri_ref[...]
    cp_ref[:, 0:S] = _select_slots(aff_ref[:, 0:S], cap_s, 0.0, tri)
    cp_ref[:, S:T] = _select_slots(aff_ref[:, S:T], cap_l, float(cap_s), tri)
    bnd_ref[...] = _mm(jnp.where(cp_ref[...] > 0.0, 1.0, 0.0).astype(bf16), before_ref[...])


def _topk(aff, S, cap_s, cap_l):
    B, E, T = aff.shape
    tri = jnp.asarray(np.triu(np.ones((LANES, LANES), np.float32)), bf16)
    before = jnp.asarray(np.arange(T)[:, None] < MOE_CHUNK * np.arange(LANES)[None, :], bf16)
    return pl.pallas_call(
        functools.partial(_topk_kernel, S, cap_s, cap_l),
        grid=(B,),
        in_specs=[pl.BlockSpec((None, E, T), lambda b: (b, 0, 0)),
                  pl.BlockSpec((LANES, LANES), lambda b: (0, 0)),
                  pl.BlockSpec((T, LANES), lambda b: (0, 0))],
        out_specs=[pl.BlockSpec((None, E, T), lambda b: (b, 0, 0)),
                   pl.BlockSpec((None, E, LANES), lambda b: (b, 0, 0))],
        out_shape=[jax.ShapeDtypeStruct((B, E, T), f32), jax.ShapeDtypeStruct((B, E, LANES), f32)],
        compiler_params=_params("arbitrary"),
        name="expert_select",
    )(aff, tri, before)


def _window_count(lo, hi, r0):
    return jnp.where(hi > lo, lax.div(hi - r0 + (MOE_WIN - 1), MOE_WIN), 0)


def _gather_kernel(n_e, rows, bnd_ref, h2_ref, affx_ref, cp_ref, xg_ref):
    b, k = pl.program_id(0), pl.program_id(1)
    D = h2_ref.shape[1]

    @pl.when(k == 0)
    def _():
        xg_ref[...] = jnp.zeros_like(xg_ref)

    ex = lax.broadcasted_iota(jnp.int32, (n_e, 1), 0)
    starts = []
    start_col = jnp.zeros((n_e, 1), f32)
    for e in range(n_e):
        r0 = lax.div(bnd_ref[(b * n_e + e) * LANES + k], PACK) * PACK
        starts.append(r0)
        start_col = start_col + jnp.where(ex == e, r0.astype(f32), 0.0)
    rel = cp_ref[...] - start_col
    rel_rows = jnp.broadcast_to(rel[:, None, :], (n_e, MOE_WIN, MOE_CHUNK)).reshape(n_e * MOE_WIN, MOE_CHUNK)
    j1 = lax.broadcasted_iota(jnp.int32, (n_e, MOE_WIN, 1), 1).reshape(n_e * MOE_WIN, 1) + 1
    onehot = jnp.where(rel_rows == j1.astype(f32), 1.0, 0.0).astype(bf16)
    tok, tok_aff = h2_ref[...], affx_ref[...]
    moved = _mm(onehot, tok).astype(bf16)
    moved_aff = _mm(onehot, tok_aff).astype(bf16)
    for e in range(n_e):
        dst = pl.ds(pl.multiple_of(e * rows + starts[e], PACK), MOE_WIN)
        xg_ref[dst, 0:D] += moved[e * MOE_WIN:(e + 1) * MOE_WIN]
        xg_ref[dst, D:D + LANES] += moved_aff[e * MOE_WIN:(e + 1) * MOE_WIN]

    sub1 = lax.broadcasted_iota(jnp.int32, (MOE_WIN, 1), 0) + 1
    n_wins = [_window_count(bnd_ref[(b * n_e + e) * LANES + k], bnd_ref[(b * n_e + e) * LANES + k + 1], starts[e])
              for e in range(n_e)]

    @pl.when(functools.reduce(jnp.maximum, n_wins) > 1)
    def _():
        for e in range(n_e):
            def extra(w, c, e=e):
                hit = rel[e:e + 1, :] == (sub1 + w * MOE_WIN).astype(f32)
                oh = jnp.where(hit, 1.0, 0.0).astype(bf16)
                dst = pl.ds(pl.multiple_of(e * rows + starts[e] + w * MOE_WIN, PACK), MOE_WIN)
                xg_ref[dst, 0:D] += _mm(oh, tok).astype(bf16)
                xg_ref[dst, D:D + LANES] += _mm(oh, tok_aff).astype(bf16)
                return c

            lax.fori_loop(1, n_wins[e], extra, 0)


def _gather(h2, affx, cp, bnd, n_slots):
    B, T, D = h2.shape
    E = cp.shape[1]
    nc = T // MOE_CHUNK
    rows = n_slots + MOE_WIN
    assert n_slots % PACK == 0 and MOE_WIN % PACK == 0 and nc + 1 <= LANES and 3 * E <= LANES
    return pl.pallas_call(
        functools.partial(_gather_kernel, E, rows),
        grid_spec=pltpu.PrefetchScalarGridSpec(
            num_scalar_prefetch=1,
            grid=(B, nc),
            in_specs=[
                pl.BlockSpec((None, MOE_CHUNK, D), lambda b, k, s: (b, k, 0)),
                pl.BlockSpec((None, MOE_CHUNK, LANES), lambda b, k, s: (b, k, 0)),
                pl.BlockSpec((None, E, MOE_CHUNK), lambda b, k, s: (b, 0, k)),
            ],
            out_specs=pl.BlockSpec((None, E * rows, D + LANES), lambda b, k, s: (b, 0, 0)),
        ),
        out_shape=jax.ShapeDtypeStruct((B, E * rows, D + LANES), bf16),
        compiler_params=_params("arbitrary", "arbitrary"),
        name="moe_gather",
    )(bnd, h2, affx, cp)


def _ffn_kernel(n_slots, n_e, xg_ref, wg_ref, wu_ref, wd_ref, y_ref, wg_s, wu_s, wd_s):
    e, b = pl.program_id(0), pl.program_id(1)
    D = wg_ref.shape[0]

    @pl.when(b == 0)
    def _():
        wg_s[...] = wg_ref[...].astype(bf16)
        wu_s[...] = wu_ref[...].astype(bf16)
        wd_s[...] = wd_ref[...].astype(bf16)

    n_b = xg_ref.shape[0]
    xg = jnp.concatenate([xg_ref[s, 0:n_slots, 0:D] for s in range(n_b)], axis=0)
    a = _mm(xg, wg_s[...])
    u = _mm(xg, wu_s[...])
    act = (a * jax.nn.sigmoid(a) * u).astype(bf16)
    lane = lax.broadcasted_iota(jnp.int32, (1, LANES), 1)
    mine = jnp.logical_and(lax.rem(lane, n_e) == e, lane < 3 * n_e)
    parts = jnp.concatenate([xg_ref[s, 0:n_slots, D:D + LANES] for s in range(n_b)], axis=0).astype(f32)
    gate = jnp.sum(jnp.where(mine, parts, 0.0), axis=1, keepdims=True)
    y = (_mm(act, wd_s[...]) * gate).astype(bf16)
    for s in range(n_b):
        y_ref[s, 0:n_slots, :] = y[s * n_slots:(s + 1) * n_slots]
        y_ref[s, n_slots:, :] = jnp.zeros((y_ref.shape[1] - n_slots, y_ref.shape[2]), bf16)


def _ffn(xg, w_gate, w_up, w_down, layer, n_slots):
    B = xg.shape[0]
    _, E, D, F = w_gate.shape
    rows = xg.shape[1] // E
    n_b = 2 if B % 2 == 0 else 1
    return pl.pallas_call(
        functools.partial(_ffn_kernel, n_slots, E),
        grid=(E, B // n_b),
        in_specs=[
            pl.BlockSpec((n_b, None, rows, D + LANES), lambda e, b: (b, e, 0, 0)),
            pl.BlockSpec((None, None, D, F), lambda e, b: (layer, e, 0, 0)),
            pl.BlockSpec((None, None, D, F), lambda e, b: (layer, e, 0, 0)),
            pl.BlockSpec((None, None, F, D), lambda e, b: (layer, e, 0, 0)),
        ],
        out_specs=pl.BlockSpec((n_b, None, rows, D), lambda e, b: (b, e, 0, 0)),
        out_shape=jax.ShapeDtypeStruct((B, E, rows, D), bf16),
        scratch_shapes=[pltpu.VMEM((D, F), bf16), pltpu.VMEM((D, F), bf16), pltpu.VMEM((F, D), bf16)],
        compiler_params=_params("arbitrary", "arbitrary"),
        name="moe_ffn",
    )(xg.reshape(B, E, rows, D + LANES), w_gate, w_up, w_down)


def _combine_kernel(n_e, rows, residual, bnd_ref, y_ref, cpt_ref, expand_ref, *refs):
    o_ref = refs[-1]
    b, k = pl.program_id(0), pl.program_id(1)
    cpt = cpt_ref[...]
    hi_part = jnp.floor(cpt * (1.0 / 32.0))
    lo_part = cpt - 32.0 * hi_part
    expand = expand_ref[...]
    rank = 32.0 * _mm(hi_part.astype(bf16), expand) + _mm(lo_part.astype(bf16), expand)
    lane = lax.broadcasted_iota(jnp.int32, (1, n_e * MOE_WIN), 1)
    lane_e = lax.div(lane, MOE_WIN)
    starts = []
    tgt = (lane - lane_e * MOE_WIN + 1).astype(f32)
    for e in range(n_e):
        r0 = lax.div(bnd_ref[(b * n_e + e) * LANES + k], PACK) * PACK
        starts.append(r0)
        tgt = tgt + jnp.where(lane_e == e, r0.astype(f32), 0.0)
    onehot = jnp.where(rank == tgt, 1.0, 0.0).astype(bf16)
    ycat = jnp.concatenate(
        [y_ref[pl.ds(pl.multiple_of(e * rows + starts[e], PACK), MOE_WIN), :] for e in range(n_e)], axis=0)
    o_ref[...] = _mm(onehot, ycat)

    lane_w = lax.broadcasted_iota(jnp.int32, (1, MOE_WIN), 1)
    n_wins = [_window_count(bnd_ref[(b * n_e + e) * LANES + k], bnd_ref[(b * n_e + e) * LANES + k + 1], starts[e])
              for e in range(n_e)]

    @pl.when(functools.reduce(jnp.maximum, n_wins) > 1)
    def _():
        for e in range(n_e):
            rank_e = rank[:, e * MOE_WIN:(e + 1) * MOE_WIN]

            def extra(w, c, e=e, rank_e=rank_e):
                r = pl.multiple_of(starts[e] + w * MOE_WIN, PACK)
                hit = rank_e == (lane_w + (r + 1)).astype(f32)
                o_ref[...] += _mm(jnp.where(hit, 1.0, 0.0).astype(bf16),
                                  y_ref[pl.ds(pl.multiple_of(e * rows + r, PACK), MOE_WIN), :])
                return c

            lax.fori_loop(1, n_wins[e], extra, 0)

    if residual:
        x_ref, mod_ref = refs[0], refs[1]
        o_ref[...] = x_ref[...] + mod_ref[5:6, :] * o_ref[...]


def _combine(y, cp, bnd, x=None, mod=None, n_tok=None):
    B, E, rows, D = y.shape
    assert rows <= 32 * 256
    residual = x is not None
    n_tok = n_tok if residual else cp.shape[2]
    expand = jnp.asarray(np.kron(np.eye(E), np.ones((1, MOE_WIN))), bf16)
    chunk = pl.BlockSpec((None, MOE_CHUNK, D), lambda b, k, s: (b, k, 0))
    in_specs = [
        pl.BlockSpec((None, E * rows, D), lambda b, k, s: (b, 0, 0)),
        pl.BlockSpec((None, MOE_CHUNK, E), lambda b, k, s: (b, k, 0)),
        pl.BlockSpec(expand.shape, lambda b, k, s: (0, 0)),
    ]
    args = [bnd, y.reshape(B, E * rows, D), cp.transpose(0, 2, 1), expand]
    if residual:
        in_specs += [chunk, pl.BlockSpec((None, None, 6, D), lambda b, k, s: (b, 0, 0, 0))]
        args += [x, mod]
    return pl.pallas_call(
        functools.partial(_combine_kernel, E, rows, residual),
        grid_spec=pltpu.PrefetchScalarGridSpec(
            num_scalar_prefetch=1,
            grid=(B, n_tok // MOE_CHUNK),
            in_specs=in_specs,
            out_specs=chunk,
        ),
        out_shape=jax.ShapeDtypeStruct((B, n_tok, D), f32),
        compiler_params=_params("arbitrary", "arbitrary"),
        name="moe_combine",
    )(*args)


def _rope_tables(S, L):
    t = jnp.arange(S)
    row, col = t // GRID_W, t % GRID_W
    f = ROPE_F
    inv = ROPE_BASE ** (-jnp.arange(f, dtype=f32) / f)
    ar = row.astype(f32)[:, None] * inv
    ac = col.astype(f32)[:, None] * inv
    cos = jnp.concatenate([jnp.cos(ar), jnp.cos(ar), jnp.cos(ac), jnp.cos(ac)], axis=1)
    sin = jnp.concatenate([-jnp.sin(ar), jnp.sin(ar), -jnp.sin(ac), jnp.sin(ac)], axis=1)
    cos = jnp.concatenate([jnp.tile(cos, (1, 2)), jnp.ones((L, LANES), f32)], axis=0)
    sin = jnp.concatenate([jnp.tile(sin, (1, 2)), jnp.zeros((L, LANES), f32)], axis=0)
    return cos, sin


def _pad_heads(w, n_heads, width, padded):
    lead = w.shape[:-1]
    w = w.reshape(lead + (n_heads, width))
    w = jnp.pad(w, [(0, 0)] * len(lead) + [(0, 0), (0, padded - width)])
    return w.reshape(lead + (n_heads * padded,))


def _layer_weights(l, D, w_in, norm2_g, na_q_g, na_k_g, mla_q_norm, mla_w_uq, mla_kv_norm, mla_w_ukv,
                   mla_q_g, mla_k_g, swa_q_g, swa_k_g, w_na_o, w_mla_o, w_swa_o, w_o, router):
    o = np.cumsum((0, NA_W, NA_W, NA_W, MLA_Q_RANK, MLA_KV_RANK, MLA_ROPE, SWA_W, SWA_KV_W, SWA_KV_W))
    o_naq, o_nak, o_nav, o_cq, o_ckv, o_kr, o_swq, o_swk, o_swv, o_gl = (int(v) for v in o)
    wi = w_in[l]
    swq = wi[:, o_swq:o_swq + SWA_W].reshape(D, SWA_HEADS, SWA_DIM)[:, SWA_HEAD_ORDER, :].reshape(D, SWA_W)
    kr = jnp.pad(wi[:, o_kr:o_kr + MLA_ROPE], ((0, 0), (0, LANES - MLA_ROPE)))
    w_all = jnp.concatenate([
        wi[:, o_naq:o_naq + 3 * NA_W], swq, wi[:, o_cq:o_cq + MLA_Q_RANK], wi[:, o_ckv:o_ckv + MLA_KV_RANK], kr,
        wi[:, o_swk:o_swk + 2 * SWA_KV_W], 0.5 * wi[:, o_gl:]], axis=1).astype(bf16)
    ukv = mla_w_ukv[l].reshape(MLA_KV_RANK, MLA_HEADS, MLA_NOPE + MLA_V)
    swa_o = w_swa_o[l].reshape(SWA_HEADS, SWA_DIM, D)[SWA_HEAD_ORDER, :, :].reshape(SWA_W, D)
    r_hi = router[l].astype(bf16)
    r_lo = (router[l] - r_hi.astype(f32)).astype(bf16)
    return dict(
        w_all=w_all,
        naq_g=jnp.tile(na_q_g[l] * (NA_DIM ** -0.5 * LOG2E), NA_HEADS)[None],
        nak_g=jnp.tile(na_k_g[l], NA_HEADS)[None],
        swq_g=jnp.tile(swa_q_g[l] * (SWA_DIM ** -0.5 * LOG2E), SWA_HEADS)[None],
        swk_g=jnp.tile(swa_k_g[l], SWA_KV_HEADS)[None],
        qn=mla_q_norm[l][None], kvn=mla_kv_norm[l][None],
        wuq=_pad_heads(mla_w_uq[l], MLA_HEADS, MLA_QK, MLA_PAD).astype(bf16),
        wuk=ukv[:, :, :MLA_NOPE].reshape(MLA_KV_RANK, MLA_HEADS * MLA_NOPE).astype(bf16),
        wuv=ukv[:, :, MLA_NOPE:].reshape(MLA_KV_RANK, MLA_W).astype(bf16),
        qg=jnp.tile(jnp.pad(mla_q_g[l] * (MLA_QK ** -0.5 * LOG2E), (0, MLA_PAD - MLA_QK)), MLA_HEADS)[None],
        kg=jnp.tile(jnp.pad(mla_k_g[l], (0, MLA_PAD - MLA_QK)), MLA_HEADS)[None],
        wna=w_na_o[l].astype(bf16), wmla=w_mla_o[l].astype(bf16), wswa=swa_o.astype(bf16),
        wo=(0.5 * w_o[l]).astype(bf16), g2=norm2_g[l][None], router=jnp.concatenate([r_hi, r_lo], axis=1),
    )


def kernel(x, c, ctx, c_ctx, norm1_g, norm2_g, w_ada, b_ada, w_in, na_q_g, na_k_g, na_rpb, mla_q_norm, mla_w_uq, mla_kv_norm, mla_w_ukv, mla_q_g, mla_k_g, swa_q_g, swa_k_g, swa_sink, w_na_o, w_mla_o, w_swa_o, w_o, router, w_gate, w_up, w_down):
    B, S, D = x.shape
    L = ctx.shape[1]
    depth = w_in.shape[0]
    E = router.shape[2]
    T = S + L
    n_lat = S // TT
    rows = S // GRID_W
    assert S % TT == 0 and L % TM == 0 and L <= TT and rows >= NA_BAND_ROWS and S >= SWA_KEYS and B < COND_ROWS
    cap_s = CAPACITY_FACTOR * S // E
    cap_l = CAPACITY_FACTOR * L // E

    cpad = jnp.zeros((COND_ROWS, D), f32).at[:B].set(c).at[B].set(c_ctx)
    mod = _ada(cpad, w_ada, b_ada).reshape(depth, COND_ROWS, 6, D)
    mod = jnp.stack([mod[:, :B], jnp.broadcast_to(mod[:, B:B + 1], (depth, B, 6, D))], axis=2)

    cos, sin = _rope_tables(S, L)
    gmat = jnp.asarray(np.kron(np.eye(NA_HEADS), np.full((NA_DIM, NA_DIM), 1.0 / NA_DIM)), bf16)

    xs = jnp.concatenate([x, ctx], axis=1)
    moe = None
    for l in range(depth):
        lw = _layer_weights(l, D, w_in, norm2_g, na_q_g, na_k_g, mla_q_norm, mla_w_uq, mla_kv_norm, mla_w_ukv,
                            mla_q_g, mla_k_g, swa_q_g, swa_k_g, w_na_o, w_mla_o, w_swa_o, w_o, router)
        modp = mod[l - 1] if l > 0 else None
        xs, proj, gl = _in_proj(xs, moe, modp, mod[l], norm1_g[l][None], lw["w_all"], n_lat)
        qka, mq, mk, mv = _prep(proj, cos, sin, gmat, lw)
        ona, oswa = _local_attn(qka, proj, na_rpb[l], swa_sink[l] * LOG2E, S)
        omla = _mla(mq, mk, mv, S)
        xs, h2, aff, affx = _post(ona, omla, oswa, gl, xs, mod[l], lw, n_lat)
        cp, bnd = _topk(aff.transpose(0, 2, 1), S, cap_s, cap_l)
        bnd = bnd.astype(jnp.int32).reshape(-1)
        xg = _gather(h2, affx, cp, bnd, cap_s + cap_l)
        y = _ffn(xg, w_gate, w_up, w_down, l, cap_s + cap_l)
        if l == depth - 1:
            return _combine(y, cp, bnd, xs, mod[l], S)
        moe = _combine(y, cp, bnd)
```
